```python
import math
import jax, jax.numpy as jnp
from jax import lax
import numpy as np

D_MODEL = 1024
BATCH = 8
SEQ = 2048
DEPTH = 1
DEC_BATCH = 128
DEC_SEQ = 4
PAST_LEN = 16384
PAGE_SIZE = 128

D_MIX = 2 * D_MODEL
D_SSM = D_MIX // 2
SSM_HEAD_DIM = 64
SSM_HEADS = D_SSM // SSM_HEAD_DIM
SSM_GROUPS = 2
HEADS_PER_GROUP = SSM_HEADS // SSM_GROUPS
SSM_STATE = 128
CONV_K = 4
C_CONV = D_SSM + 2 * SSM_GROUPS * SSM_STATE
SSD_CHUNK = 128
D_SG = D_MIX - D_SSM
SG_CHUNK = 128
SG_HEADS = 8
SG_HEAD_DIM = D_SG // SG_HEADS
IN_COLS = D_SSM + C_CONV + SSM_HEADS + 2 * D_SG
N_EXPERT_GROUPS = 4
EXPERTS_PER_GROUP = 8
N_EXPERTS = N_EXPERT_GROUPS * EXPERTS_PER_GROUP
TOP_K = 2
D_FF_EXPERT = D_MODEL // 2
MOE_BLOCK = 128
EPS = 1e-6

kernel_name = 'hymba_ssd_gmlp_hiermoe_step'


def rms_norm(x, w):
    xf = x.astype(jnp.float32)
    y = xf * lax.rsqrt(jnp.mean(xf * xf, axis=-1, keepdims=True) + EPS)
    return (y * w.astype(jnp.float32)).astype(x.dtype)


def gated_group_rms_norm(y, z, w):
    g = y.astype(jnp.float32) * jax.nn.silu(z.astype(jnp.float32))
    shp = g.shape
    g = g.reshape(shp[:-1] + (SSM_GROUPS, shp[-1] // SSM_GROUPS))
    g = g * lax.rsqrt(jnp.mean(g * g, axis=-1, keepdims=True) + EPS)
    return (g.reshape(shp) * w.astype(jnp.float32)).astype(z.dtype)


def ssd_scan(x, dt, a, b_in, c_in, s0, chunk):
    f32 = jnp.float32
    bsz, seq = x.shape[:2]
    nc = seq // chunk
    x = x.astype(f32).reshape((bsz, nc, chunk) + x.shape[2:])
    dt = dt.astype(f32).reshape((bsz, nc, chunk) + dt.shape[2:])
    b_in = b_in.astype(f32).reshape((bsz, nc, chunk) + b_in.shape[2:])
    c_in = c_in.astype(f32).reshape((bsz, nc, chunk) + c_in.shape[2:])
    acum = jnp.cumsum(dt * a, axis=2)
    xdt = x * dt[..., None]
    causal = jnp.tril(jnp.ones((chunk, chunk), bool))[:, :, None, None]
    seg = acum[:, :, :, None] - acum[:, :, None, :]
    decay = jnp.exp(jnp.where(causal, seg, -jnp.inf))
    cb = jnp.einsum('bclgn,bcsgn->bclsg', c_in, b_in)
    y_diag = jnp.einsum('bclsg,bclsgh,bcsghp->bclghp', cb, decay, xdt)
    decay_end = jnp.exp(acum[:, :, -1:] - acum)
    chunk_states = jnp.einsum('bclgn,bclgh,bclghp->bcghpn', b_in, decay_end, xdt)
    chunk_decay = jnp.exp(acum[:, :, -1])

    def step(s, inp):
        st, dec = inp
        return dec[..., None, None] * s + st, s

    s_fin, s_in = lax.scan(step, s0.astype(f32),
                           (jnp.moveaxis(chunk_states, 1, 0), jnp.moveaxis(chunk_decay, 1, 0)))
    s_in = jnp.moveaxis(s_in, 0, 1)
    y_off = jnp.einsum('bclgn,bcghpn,bclgh->bclghp', c_in, s_in, jnp.exp(acum))
    y = (y_diag + y_off).reshape((bsz, seq) + y_diag.shape[3:])
    return y, s_fin


def spatial_gating(u, v, sg_w, sg_b):
    bsz, seq, _ = v.shape
    n = min(seq, SG_CHUNK)
    nc = seq // n
    causal = jnp.tril(jnp.ones((n, n), bool))
    w_m = jnp.where(causal, sg_w[:, :n, :n], 0.0)
    vh = v.reshape(bsz, nc, n, SG_HEADS, SG_HEAD_DIM)
    sv = jnp.einsum('hts,bcshd->bcthd', w_m.astype(v.dtype), vh) + sg_b[:, :n].T[None, None, :, :, None].astype(v.dtype)
    return u * sv.reshape(bsz, seq, D_SG)


def mixer(hn, conv_prev, ssm_prev, w_in, conv_w, conv_b, dt_bias, a_log, d_skip,
          ssm_norm_w, sg_norm_w, sg_w, sg_b, w_out):
    bsz, seq, _ = hn.shape
    proj = hn @ w_in
    z, xbc, dt_raw, uv = jnp.split(proj, [D_SSM, D_SSM + C_CONV, D_SSM + C_CONV + SSM_HEADS], axis=-1)
    xbc_full = jnp.concatenate([conv_prev.astype(xbc.dtype), xbc], axis=1)
    conv_out = lax.conv_general_dilated(
        xbc_full, conv_w[:, None, :].astype(xbc.dtype), (1,), 'VALID',
        dimension_numbers=('NWC', 'WIO', 'NWC'), feature_group_count=C_CONV) + conv_b.astype(xbc.dtype)
    xbc_act = jax.nn.silu(conv_out)
    xs, b_in, c_in = jnp.split(xbc_act, [D_SSM, D_SSM + SSM_GROUPS * SSM_STATE], axis=-1)
    dt = jax.nn.softplus(dt_raw.astype(jnp.float32) + dt_bias.astype(jnp.float32))
    a = -jnp.exp(a_log.astype(jnp.float32))
    xs_h = xs.reshape(bsz, seq, SSM_GROUPS, HEADS_PER_GROUP, SSM_HEAD_DIM)
    y, s_fin = ssd_scan(xs_h, dt.reshape(bsz, seq, SSM_GROUPS, HEADS_PER_GROUP),
                        a.reshape(SSM_GROUPS, HEADS_PER_GROUP),
                        b_in.reshape(bsz, seq, SSM_GROUPS, SSM_STATE),
                        c_in.reshape(bsz, seq, SSM_GROUPS, SSM_STATE),
                        ssm_prev, min(SSD_CHUNK, seq))
    y = y + d_skip.astype(jnp.float32).reshape(SSM_GROUPS, HEADS_PER_GROUP)[..., None] * xs_h.astype(jnp.float32)
    y_ssm = gated_group_rms_norm(y.reshape(bsz, seq, D_SSM), z, ssm_norm_w)
    u, v = jnp.split(jax.nn.gelu(uv, approximate=False), 2, axis=-1)
    v = rms_norm(v, sg_norm_w)
    y_sg = spatial_gating(u, v, sg_w, sg_b)
    mix = jnp.concatenate([y_ssm, y_sg.astype(y_ssm.dtype)], axis=-1) @ w_out
    return mix, xbc_full[:, -(CONV_K - 1):], s_fin.astype(ssm_prev.dtype), v


def hier_moe(h, w_router_group, b_router_group, w_router_expert, b_router_expert, w_gate, w_up, w_down):
    shp = h.shape
    x = h.reshape(-1, D_MODEL)
    n_tok = x.shape[0]
    p_grp = jax.nn.softmax((x @ w_router_group).astype(jnp.float32) + b_router_group.astype(jnp.float32), axis=-1)
    g_idx = jnp.argmax(p_grp, axis=-1)
    g_p = jnp.max(p_grp, axis=-1)
    le = ((x @ w_router_expert).astype(jnp.float32) + b_router_expert.astype(jnp.float32))
    le = le.reshape(n_tok, N_EXPERT_GROUPS, EXPERTS_PER_GROUP)
    le = jnp.take_along_axis(le, g_idx[:, None, None], axis=1)[:, 0]
    top_v, top_i = lax.top_k(jax.nn.softmax(le, axis=-1), TOP_K)
    gate = g_p[:, None] * top_v / jnp.sum(top_v, axis=-1, keepdims=True)
    expert_idx = g_idx[:, None] * EXPERTS_PER_GROUP + top_i
    n_asg = n_tok * TOP_K
    e_flat = expert_idx.reshape(-1).astype(jnp.int32)
    t_flat = jnp.repeat(jnp.arange(n_tok, dtype=jnp.int32), TOP_K)
    w_flat = gate.reshape(-1)
    order = jnp.argsort(e_flat)
    e_s, t_s, w_s = e_flat[order], t_flat[order], w_flat[order]
    counts = jnp.bincount(e_flat, length=N_EXPERTS)
    padded = (counts + MOE_BLOCK - 1) // MOE_BLOCK * MOE_BLOCK
    start = jnp.cumsum(counts) - counts
    pend = jnp.cumsum(padded)
    pstart = pend - padded
    dest = pstart[e_s] + (jnp.arange(n_asg, dtype=jnp.int32) - start[e_s])
    n_blocks = -(-n_asg // MOE_BLOCK) + N_EXPERTS
    n_rows = n_blocks * MOE_BLOCK
    buf_tok = jnp.full((n_rows,), n_tok, jnp.int32).at[dest].set(t_s)
    buf_w = jnp.zeros((n_rows,), x.dtype).at[dest].set(w_s.astype(x.dtype))
    blk_expert = jnp.minimum(jnp.searchsorted(pend, jnp.arange(n_blocks) * MOE_BLOCK, side='right'),
                             N_EXPERTS - 1)
    x_pad = jnp.concatenate([x, jnp.zeros((1, D_MODEL), x.dtype)], axis=0)

    def expert_block(args):
        e, tok = args
        xb = x_pad[tok]
        hmid = jax.nn.silu(xb @ w_gate[e]) * (xb @ w_up[e])
        return hmid @ w_down[e]

    out = lax.map(expert_block, (blk_expert, buf_tok.reshape(n_blocks, MOE_BLOCK)))
    out = out.reshape(n_rows, D_MODEL) * buf_w[:, None]
    y = jax.ops.segment_sum(out, buf_tok, num_segments=n_tok + 1)[:n_tok]
    return y.reshape(shp).astype(h.dtype)


def decoder_layer(x, conv_prev, ssm_prev, norm_mix_w, w_in, conv_w, conv_b, dt_bias, a_log, d_skip,
                  ssm_norm_w, sg_norm_w, sg_w, sg_b, w_out, norm_ffn_w, w_router_group, b_router_group,
                  w_router_expert, b_router_expert, w_gate, w_up, w_down):
    mix, conv_new, ssm_new, v_rows = mixer(rms_norm(x, norm_mix_w), conv_prev, ssm_prev, w_in, conv_w, conv_b,
                                           dt_bias, a_log, d_skip, ssm_norm_w, sg_norm_w, sg_w, sg_b, w_out)
    h = x + mix
    h = h + hier_moe(rms_norm(h, norm_ffn_w), w_router_group, b_router_group, w_router_expert,
                     b_router_expert, w_gate, w_up, w_down)
    return h, conv_new, ssm_new, v_rows


def setup_inputs(seed: int = 0) -> dict:
    key = jax.random.key(seed)
    ks = jax.random.split(key, 24)
    f32 = jnp.float32

    def nrm(k, shape, scale):
        return jax.random.normal(k, shape, f32) * scale

    u_dt = jax.random.uniform(ks[6], (DEPTH, SSM_HEADS), f32)
    dt0 = jnp.exp(u_dt * (math.log(0.1) - math.log(0.001)) + math.log(0.001))
    return {
        'x_prompt': nrm(ks[0], (BATCH, SEQ, D_MODEL), 1.0),
        'x_sample': nrm(ks[1], (DEC_BATCH, DEC_SEQ, D_MODEL), 1.0),
        'state_conv': nrm(ks[2], (DEPTH, DEC_BATCH, CONV_K - 1, C_CONV), 1.0),
        'state_ssm': nrm(ks[3], (DEPTH, DEC_BATCH, SSM_GROUPS, HEADS_PER_GROUP, SSM_HEAD_DIM, SSM_STATE), 0.3),
        'norm_mix_w': 1.0 + nrm(ks[4], (DEPTH, D_MODEL), 0.02),
        'w_in': nrm(ks[5], (DEPTH, D_MODEL, IN_COLS), D_MODEL ** -0.5),
        'conv_w': nrm(ks[7], (DEPTH, CONV_K, C_CONV), CONV_K ** -0.5),
        'conv_b': nrm(ks[8], (DEPTH, C_CONV), 0.02),
        'dt_bias': dt0 + jnp.log(-jnp.expm1(-dt0)),
        'a_log': jnp.log(jax.random.uniform(ks[9], (DEPTH, SSM_HEADS), f32, 1.0, 16.0)),
        'd_skip': 1.0 + nrm(ks[10], (DEPTH, SSM_HEADS), 0.1),
        'ssm_norm_w': 1.0 + nrm(ks[11], (DEPTH, D_SSM), 0.02),
        'sg_norm_w': 1.0 + nrm(ks[12], (DEPTH, D_SG), 0.02),
        'sg_w': nrm(ks[13], (DEPTH, SG_HEADS, SG_CHUNK, SG_CHUNK), SG_CHUNK ** -0.5),
        'sg_b': 1.0 + nrm(ks[14], (DEPTH, SG_HEADS, SG_CHUNK), 0.1),
        'w_out': nrm(ks[15], (DEPTH, D_MIX, D_MODEL), D_MIX ** -0.5),
        'norm_ffn_w': 1.0 + nrm(ks[16], (DEPTH, D_MODEL), 0.02),
        'w_router_group': nrm(ks[17], (DEPTH, D_MODEL, N_EXPERT_GROUPS), D_MODEL ** -0.5),
        'b_router_group': nrm(ks[18], (DEPTH, N_EXPERT_GROUPS), 0.01),
        'w_router_expert': nrm(ks[19], (DEPTH, D_MODEL, N_EXPERTS), D_MODEL ** -0.5),
        'b_router_expert': nrm(ks[20], (DEPTH, N_EXPERTS), 0.01),
        'w_gate': nrm(ks[21], (DEPTH, N_EXPERTS, D_MODEL, D_FF_EXPERT), D_MODEL ** -0.5),
        'w_up': nrm(ks[22], (DEPTH, N_EXPERTS, D_MODEL, D_FF_EXPERT), D_MODEL ** -0.5),
        'w_down': nrm(ks[23], (DEPTH, N_EXPERTS, D_FF_EXPERT, D_MODEL), D_FF_EXPERT ** -0.5),
        'norm_final_w': 1.0 + nrm(jax.random.fold_in(key, 99), (D_MODEL,), 0.02),
    }


def reference(x_prompt, x_sample, state_conv, state_ssm, norm_mix_w, w_in, conv_w, conv_b, dt_bias, a_log,
              d_skip, ssm_norm_w, sg_norm_w, sg_w, sg_b, w_out, norm_ffn_w, w_router_group, b_router_group,
              w_router_expert, b_router_expert, w_gate, w_up, w_down, norm_final_w):
    yp, ys = x_prompt, x_sample
    bp = x_prompt.shape[0]
    conv_p, ssm_p, conv_s, ssm_s, v_s = [], [], [], [], []
    for l in range(DEPTH):
        lp = (norm_mix_w[l], w_in[l], conv_w[l], conv_b[l], dt_bias[l], a_log[l], d_skip[l], ssm_norm_w[l],
              sg_norm_w[l], sg_w[l], sg_b[l], w_out[l], norm_ffn_w[l], w_router_group[l], b_router_group[l],
              w_router_expert[l], b_router_expert[l], w_gate[l], w_up[l], w_down[l])
        conv0 = jnp.zeros((bp, CONV_K - 1, C_CONV), x_prompt.dtype)
        ssm0 = jnp.zeros((bp,) + state_ssm.shape[2:], state_ssm.dtype)
        yp, cp, sp, _ = decoder_layer(yp, conv0, ssm0, *lp)
        ys, cs, ss, vs = decoder_layer(ys, state_conv[l], state_ssm[l], *lp)
        conv_p.append(cp)
        ssm_p.append(sp)
        conv_s.append(cs)
        ssm_s.append(ss)
        v_s.append(vs)
    y_prompt = rms_norm(yp, norm_final_w)
    y_sample = rms_norm(ys, norm_final_w)
    return (y_prompt, y_sample, jnp.stack(conv_p), jnp.stack(ssm_p), jnp.stack(conv_s), jnp.stack(ssm_s), jnp.stack(v_s))
```

```python
import functools

import jax
import jax.numpy as jnp
from jax import lax
from jax.experimental import pallas as pl
from jax.experimental.pallas import tpu as pltpu

D_MODEL = 1024
D_SSM = 1024
SSM_HEAD_DIM = 64
SSM_HEADS = 16
SSM_GROUPS = 2
HEADS_PER_GROUP = 8
SSM_STATE = 128
CONV_K = 4
C_CONV = D_SSM + 2 * SSM_GROUPS * SSM_STATE
D_SG = 1024
SG_HEADS = 8
SG_HEAD_DIM = 128
CHUNK = 128
N_EXPERT_GROUPS = 4
EXPERTS_PER_GROUP = 8
N_EXPERTS = 32
D_FF = 512
EPS = 1e-6

LANES = 128
SUBLANES = 8
VMEM_LIMIT = 56 * 1024 * 1024

F32 = jnp.float32
BF16 = jnp.bfloat16
HIGHEST = lax.Precision.HIGHEST


def _dot(a, b):
    return jnp.dot(a, b, preferred_element_type=F32)


def _dot_f32(a, b):
    return jnp.dot(a, b, preferred_element_type=F32, precision=HIGHEST)


def _rms(x, w):
    return x * lax.rsqrt(jnp.mean(x * x, axis=-1, keepdims=True) + EPS) * w


def _silu(x):
    return x * (1.0 / (1.0 + jnp.exp(-x)))


def _gelu(x):
    return 0.5 * x * (1.0 + lax.erf(x * 0.7071067811865476))


def _softplus(x):
    return jnp.maximum(x, 0.0) + jnp.log1p(jnp.exp(-jnp.abs(x)))


def _gated_group_norm(y, z, w):
    g = y * _silu(z)
    half = D_SSM // SSM_GROUPS
    parts = []
    for k in range(SSM_GROUPS):
        gk = g[:, k * half:(k + 1) * half]
        parts.append(gk * lax.rsqrt(jnp.mean(gk * gk, axis=-1, keepdims=True) + EPS))
    return jnp.concatenate(parts, axis=1) * w


def _route(logits):
    r = logits.shape[0]
    lane = lax.broadcasted_iota(jnp.int32, (r, LANES), 1)
    lanef = lane.astype(F32)
    big = float(LANES)
    neg = -jnp.inf
    gmask = lane < N_EXPERT_GROUPS
    gl = jnp.where(gmask, logits, neg)
    ge = jnp.exp(gl - jnp.max(gl, axis=1, keepdims=True))
    p_grp = ge / jnp.sum(ge, axis=1, keepdims=True)
    g_p = jnp.max(p_grp, axis=1, keepdims=True)
    g_idx = jnp.min(jnp.where(gmask & (p_grp == g_p), lanef, big), axis=1, keepdims=True)
    lo = N_EXPERT_GROUPS + EXPERTS_PER_GROUP * g_idx
    emask = (lanef >= lo) & (lanef < lo + EXPERTS_PER_GROUP)
    el = jnp.where(emask, logits, neg)
    ee = jnp.exp(el - jnp.max(el, axis=1, keepdims=True))
    pe = ee / jnp.sum(ee, axis=1, keepdims=True)
    v1 = jnp.max(pe, axis=1, keepdims=True)
    i1 = jnp.min(jnp.where(emask & (pe == v1), lanef, big), axis=1, keepdims=True)
    rest = emask & (lanef != i1)
    pe2 = jnp.where(rest, pe, -1.0)
    v2 = jnp.max(pe2, axis=1, keepdims=True)
    i2 = jnp.min(jnp.where(rest & (pe2 == v2), lanef, big), axis=1, keepdims=True)
    den = v1 + v2
    w1 = g_p * v1 / den
    w2 = g_p * v2 / den
    out = jnp.where(lane == 0, i1 - N_EXPERT_GROUPS, 0.0)
    out = jnp.where(lane == 1, i2 - N_EXPERT_GROUPS, out)
    out = jnp.where(lane == 2, w1, out)
    out = jnp.where(lane == 3, w2, out)
    return out


def _ffn_front(h, nfw, wr, br):
    hn2 = _rms(h, nfw)
    logits = _dot_f32(hn2, wr) + br
    return hn2, _route(logits)


def _prompt_mixer_kernel(
        x_ref, nmw_ref, wz_ref, wxbc_ref, wdt_ref, wuv_ref, convw_ref, convb_ref, dtb_ref,
        alog_ref, dskip_ref, ssmnw_ref, sgnw_ref, sgw_ref, sgb_ref, wout_ref, nfw_ref,
        wr_ref, br_ref,
        h_ref, hn2_ref, route_ref, convt_ref, ssm_ref,
        xbc_scr, state_scr, y_scr, mix_scr, *, tl):
    t = pl.program_id(1)
    nt = pl.num_programs(1)

    @pl.when(t == 0)
    def _():
        xbc_scr[0:SUBLANES, :] = jnp.zeros((SUBLANES, C_CONV), F32)
        state_scr[...] = jnp.zeros(state_scr.shape, F32)

    x = x_ref[...]
    hn = _rms(x, nmw_ref[...]).astype(BF16)
    z = _dot(hn, wz_ref[...])
    xbc = _dot(hn, wxbc_ref[...])
    dtr = _dot(hn, wdt_ref[...])
    uv = _gelu(_dot(hn, wuv_ref[...]))

    xbc_scr[SUBLANES:SUBLANES + tl, :] = xbc
    conv = convb_ref[...]
    for k in range(CONV_K):
        off = SUBLANES - (CONV_K - 1) + k
        conv = conv + convw_ref[k:k + 1, :] * xbc_scr[off:off + tl, :]
    tail = xbc_scr[tl:tl + SUBLANES, :]
    xbc_scr[0:SUBLANES, :] = tail
    convt_ref[...] = tail
    act = _silu(conv)
    xs = act[:, :D_SSM]
    bmat = act[:, D_SSM:D_SSM + SSM_GROUPS * SSM_STATE]
    cmat = act[:, D_SSM + SSM_GROUPS * SSM_STATE:]

    dt = _softplus(dtr + dtb_ref[...])
    a_row = -jnp.exp(alog_ref[...])

    li = lax.broadcasted_iota(jnp.int32, (CHUNK, CHUNK), 0)
    si = lax.broadcasted_iota(jnp.int32, (CHUNK, CHUNK), 1)
    causal = li >= si
    tri = jnp.where(causal, 1.0, 0.0).astype(F32)
    lo_half = si < SSM_HEAD_DIM

    for c in range(tl // CHUNK):
        rows = slice(c * CHUNK, (c + 1) * CHUNK)
        dt_c = dt[rows]
        acum = _dot_f32(tri, dt_c * a_row)
        acum_t = acum.T
        dt_t = dt_c.T
        w_t = dt_t * jnp.exp(acum_t[:, CHUNK - 1:CHUNK] - acum_t)
        xs_c = xs[rows]
        for g in range(SSM_GROUPS):
            gl = slice(g * SSM_STATE, (g + 1) * SSM_STATE)
            b_g = bmat[rows, gl]
            c_g = cmat[rows, gl]
            b_gt = b_g.T
            cb = _dot(c_g.astype(BF16), b_gt.astype(BF16))
            for j in range(HEADS_PER_GROUP // 2):
                pl_ = slice(g * 512 + j * LANES, g * 512 + (j + 1) * LANES)
                m_l, ec_l, s_l, ea_last = [], [], [], []
                for k in range(2):
                    hh = g * HEADS_PER_GROUP + 2 * j + k
                    colb = jnp.broadcast_to(acum[:, hh:hh + 1], (CHUNK, CHUNK))
                    rowb = jnp.broadcast_to(acum_t[hh:hh + 1, :], (CHUNK, CHUNK))
                    ea = jnp.exp(colb)
                    seg = jnp.where(causal, colb - rowb, 0.0)
                    lmat = jnp.where(causal, jnp.exp(seg), 0.0)
                    m_l.append(cb * lmat * jnp.broadcast_to(dt_t[hh:hh + 1, :], (CHUNK, CHUNK)))
                    ec_l.append(c_g * ea)
                    s_l.append(b_gt * jnp.broadcast_to(w_t[hh:hh + 1, :], (CHUNK, CHUNK)))
                    ea_last.append(ea[CHUNK - 1:CHUNK, :])
                xs_p = xs_c[:, pl_]
                st_p = state_scr[g, :, j * LANES:(j + 1) * LANES]
                rx = jnp.concatenate([jnp.where(lo_half, xs_p, 0.0),
                                      jnp.where(lo_half, 0.0, xs_p)], axis=0).astype(BF16)
                rs = jnp.concatenate([jnp.where(lo_half, st_p, 0.0),
                                      jnp.where(lo_half, 0.0, st_p)], axis=0).astype(BF16)
                lhs = jnp.concatenate(m_l + ec_l, axis=1).astype(BF16)
                y_p = _dot(lhs, jnp.concatenate([rx, rs], axis=0))
                y_scr[rows, pl_] = y_p
                dec = jnp.where(lo_half[0:1, :], ea_last[0], ea_last[1])
                upd = _dot(jnp.concatenate(s_l, axis=1).astype(BF16), rx)
                state_scr[g, :, j * LANES:(j + 1) * LANES] = st_p * dec + upd

    y = y_scr[...] + dskip_ref[...] * xs
    y_ssm = _gated_group_norm(y, z, ssmnw_ref[...])
    mix_scr[:, :D_SSM] = y_ssm.astype(BF16)

    u = uv[:, :D_SG]
    v = _rms(uv[:, D_SG:], sgnw_ref[...])
    vb = v.astype(BF16)
    for c in range(tl // CHUNK):
        rows = slice(c * CHUNK, (c + 1) * CHUNK)
        for hd in range(SG_HEADS):
            hl = slice(hd * SG_HEAD_DIM, (hd + 1) * SG_HEAD_DIM)
            sv = _dot(sgw_ref[hd], vb[rows, hl]) + sgb_ref[:, hl]
            mix_scr[rows, D_SSM + hd * SG_HEAD_DIM:D_SSM + (hd + 1) * SG_HEAD_DIM] = (
                u[rows, hl] * sv).astype(BF16)

    h = x + _dot(mix_scr[...], wout_ref[...])
    h_ref[...] = h
    hn2, route = _ffn_front(h, nfw_ref[...], wr_ref[...], br_ref[...])
    hn2_ref[...] = hn2
    route_ref[...] = route

    @pl.when(t == nt - 1)
    def _():
        for g in range(SSM_GROUPS):
            ssm_ref[g] = state_scr[g].T


def _const_spec(shape):
    zeros = (0,) * len(shape)
    return pl.BlockSpec(shape, lambda *_: zeros)


def _prompt_mixer(x, w, tl):
    bsz, seq, _ = x.shape
    nt = seq // tl
    row_spec = lambda d: pl.BlockSpec((None, tl, d), lambda b, t: (b, t, 0))
    weights = [w['nmw'], w['wz'], w['wxbc'], w['wdt'], w['wuv'], w['convw'], w['convb'], w['dtb'],
               w['alog'], w['dskip'], w['ssmnw'], w['sgnw'], w['sgw'], w['sgb'], w['wout'], w['nfw'],
               w['wr'], w['br']]
    in_specs = [row_spec(D_MODEL)] + [_const_spec(a.shape) for a in weights]
    out_shape = [
        jax.ShapeDtypeStruct((bsz, seq, D_MODEL), F32),
        jax.ShapeDtypeStruct((bsz, seq, D_MODEL), F32),
        jax.ShapeDtypeStruct((bsz, seq, LANES), F32),
        jax.ShapeDtypeStruct((bsz, SUBLANES, C_CONV), F32),
        jax.ShapeDtypeStruct((bsz, SSM_GROUPS, HEADS_PER_GROUP * SSM_HEAD_DIM, SSM_STATE), F32),
    ]
    out_specs = [
        row_spec(D_MODEL), row_spec(D_MODEL), row_spec(LANES),
        pl.BlockSpec((None, SUBLANES, C_CONV), lambda b, t: (b, 0, 0)),
        pl.BlockSpec((None, SSM_GROUPS, HEADS_PER_GROUP * SSM_HEAD_DIM, SSM_STATE),
                     lambda b, t: (b, 0, 0, 0)),
    ]
    scratch = [
        pltpu.VMEM((tl + SUBLANES, C_CONV), F32),
        pltpu.VMEM((SSM_GROUPS, SSM_STATE, HEADS_PER_GROUP * SSM_HEAD_DIM), F32),
        pltpu.VMEM((tl, D_SSM), F32),
        pltpu.VMEM((tl, D_SSM + D_SG), BF16),
    ]
    return pl.pallas_call(
        functools.partial(_prompt_mixer_kernel, tl=tl),
        grid=(bsz, nt),
        in_specs=in_specs,
        out_specs=out_specs,
        out_shape=out_shape,
        scratch_shapes=scratch,
        compiler_params=pltpu.CompilerParams(
            dimension_semantics=("arbitrary", "arbitrary"), vmem_limit_bytes=VMEM_LIMIT),
        name="prompt_mixer",
    )(x, *weights)


def _sample_front_kernel(
        x_ref, cprev_ref, nmw_ref, wz_ref, wxbc_ref, wdt_ref, wuv_ref, convw_ref, convb_ref,
        dtb_ref, alog_ref, dskip_ref, sgnw_ref, w4_ref, b4_ref, expand_ref,
        z_ref, ypart_ref, ea_ref, ysg_ref, v_ref, convn_ref, c_ref, b_ref, xw_ref, dec_ref,
        *, nb, nt):
    x = x_ref[...]
    hn = _rms(x, nmw_ref[...]).astype(BF16)
    z_ref[...] = _dot(hn, wz_ref[...])
    xbc = _dot(hn, wxbc_ref[...])
    dtr = _dot(hn, wdt_ref[...])
    uv = _gelu(_dot(hn, wuv_ref[...]))

    slab = lambda a, t: a[t * nb:(t + 1) * nb]
    full = [cprev_ref[k] for k in range(CONV_K - 1)] + [slab(xbc, t) for t in range(nt)]
    for k in range(CONV_K - 1):
        convn_ref[k] = full[nt + k]
    xs, bm, cm = [], [], []
    for t in range(nt):
        conv = convb_ref[...]
        for k in range(CONV_K):
            conv = conv + convw_ref[k:k + 1, :] * full[t + k]
        act = _silu(conv)
        xs.append(act[:, :D_SSM])
        bm.append(act[:, D_SSM:D_SSM + SSM_GROUPS * SSM_STATE])
        cm.append(act[:, D_SSM + SSM_GROUPS * SSM_STATE:])
        b_ref[t * nb:(t + 1) * nb, :] = bm[t]
        c_ref[t * nb:(t + 1) * nb, :] = cm[t]

    dt = _softplus(dtr + dtb_ref[...])
    a_row = -jnp.exp(alog_ref[...])
    dts = [slab(dt, t) for t in range(nt)]
    cum = []
    for t in range(nt):
        da = dts[t] * a_row
        cum.append(da if t == 0 else cum[t - 1] + da)
    dec_ref[...] = jnp.exp(cum[nt - 1])

    lane = lax.broadcasted_iota(jnp.int32, (nb, LANES), 1)
    first_group = lane < HEADS_PER_GROUP
    facs = [jnp.exp(cum[t]) for t in range(nt)]
    facs += [dts[s] * jnp.exp(cum[nt - 1] - cum[s]) for s in range(nt)]
    pairs = []
    for t in range(nt):
        for s in range(t + 1):
            cb = []
            for g in range(SSM_GROUPS):
                gl = slice(g * SSM_STATE, (g + 1) * SSM_STATE)
                cb.append(jnp.sum(cm[t][:, gl] * bm[s][:, gl], axis=1, keepdims=True))
            cbh = jnp.where(first_group, cb[0], cb[1])
            facs.append(jnp.exp(cum[t] - cum[s]) * dts[s] * cbh)
            pairs.append((t, s))
    fx = _dot_f32(jnp.concatenate(facs, axis=0), expand_ref[...])
    fslab = lambda i: fx[i * nb:(i + 1) * nb]
    for t in range(nt):
        ea_ref[t * nb:(t + 1) * nb, :] = fslab(t)
        xw_ref[t * nb:(t + 1) * nb, :] = xs[t] * fslab(nt + t)
    for t in range(nt):
        acc = dskip_ref[...] * xs[t]
        for i, (tt, s) in enumerate(pairs):
            if tt == t:
                acc = acc + fslab(2 * nt + i) * xs[s]
        ypart_ref[t * nb:(t + 1) * nb, :] = acc

    u = uv[:, :D_SG]
    v = _rms(uv[:, D_SG:], sgnw_ref[...])
    v_ref[...] = v
    for t in range(nt):
        sv = b4_ref[t:t + 1, :]
        for s in range(t + 1):
            sv = sv + w4_ref[t, s:s + 1, :] * slab(v, s)
        ysg_ref[t * nb:(t + 1) * nb, :] = slab(u, t) * sv


def _sample_front(x_slab, cprev, w, nb, nt):
    r = nb * nt
    weights = [w['nmw'], w['wz'], w['wxbc'], w['wdt'], w['wuv'], w['convw'], w['convb'], w['dtb'],
               w['alog'], w['dskip'], w['sgnw'], w['w4'], w['b4'], w['expand']]
    ins = [x_slab, cprev] + weights
    out_shape = [
        jax.ShapeDtypeStruct((r, D_SSM), F32),
        jax.ShapeDtypeStruct((r, D_SSM), F32),
        jax.ShapeDtypeStruct((r, D_SSM), F32),
        jax.ShapeDtypeStruct((r, D_SG), F32),
        jax.ShapeDtypeStruct((r, D_SG), F32),
        jax.ShapeDtypeStruct((CONV_K - 1, nb, C_CONV), F32),
        jax.ShapeDtypeStruct((r, SSM_GROUPS * SSM_STATE), F32),
        jax.ShapeDtypeStruct((r, SSM_GROUPS * SSM_STATE), F32),
        jax.ShapeDtypeStruct((r, D_SSM), F32),
        jax.ShapeDtypeStruct((nb, LANES), F32),
    ]
    return pl.pallas_call(
        functools.partial(_sample_front_kernel, nb=nb, nt=nt),
        grid=(1,),
        in_specs=[_const_spec(a.shape) for a in ins],
        out_specs=[_const_spec(s.shape) for s in out_shape],
        out_shape=out_shape,
        compiler_params=pltpu.CompilerParams(
            dimension_semantics=("arbitrary",), vmem_limit_bytes=VMEM_LIMIT),
        name="sample_front",
    )(*ins)


def _sample_state_kernel(dec_ref, st_ref, cq_ref, bq_ref, xw_ref, so_ref, z_ref, *, seqs):
    for bb in range(seqs):
        for g in range(SSM_GROUPS):
            s0 = st_ref[bb, g]
            z_ref[bb, g] = lax.dot_general(
                cq_ref[bb, g].astype(BF16), s0.astype(BF16), (((1,), (1,)), ((), ())),
                preferred_element_type=F32)
            upd = lax.dot_general(
                xw_ref[bb, g].astype(BF16), bq_ref[bb, g].astype(BF16), (((0,), (0,)), ((), ())),
                preferred_element_type=F32)
            for hh in range(HEADS_PER_GROUP):
                hs = slice(hh * SSM_HEAD_DIM, (hh + 1) * SSM_HEAD_DIM)
                so_ref[bb, g, hs, :] = s0[hs] * dec_ref[bb, g * HEADS_PER_GROUP + hh] + upd[hs]


def _sample_state(dec, state, cq, bq, xwq, seqs):
    nb = state.shape[0]
    q = HEADS_PER_GROUP * SSM_HEAD_DIM
    blk = lambda *tail: pl.BlockSpec((seqs, SSM_GROUPS) + tail, lambda i: (i, 0, 0, 0))
    return pl.pallas_call(
        functools.partial(_sample_state_kernel, seqs=seqs),
        grid=(nb // seqs,),
        in_specs=[
            pl.BlockSpec((seqs, SSM_HEADS), lambda i: (i, 0), memory_space=pltpu.SMEM),
            blk(q, SSM_STATE), blk(SUBLANES, SSM_STATE), blk(SUBLANES, SSM_STATE), blk(SUBLANES, q),
        ],
        out_specs=[blk(q, SSM_STATE), blk(SUBLANES, q)],
        out_shape=[
            jax.ShapeDtypeStruct((nb, SSM_GROUPS, q, SSM_STATE), F32),
            jax.ShapeDtypeStruct((nb, SSM_GROUPS, SUBLANES, q), F32),
        ],
        compiler_params=pltpu.CompilerParams(
            dimension_semantics=("arbitrary",), vmem_limit_bytes=VMEM_LIMIT),
        name="sample_state",
    )(dec, state, cq, bq, xwq)


def _sample_back_kernel(x_ref, ypart_ref, ea_ref, zoff_ref, z_ref, ysg_ref, ssmnw_ref, wout_ref,
                        nfw_ref, wr_ref, br_ref, h_ref, hn2_ref, route_ref):
    y = ypart_ref[...] + ea_ref[...] * zoff_ref[...]
    y_ssm = _gated_group_norm(y, z_ref[...], ssmnw_ref[...])
    mix_in = jnp.concatenate([y_ssm.astype(BF16), ysg_ref[...].astype(BF16)], axis=1)
    h = x_ref[...] + _dot(mix_in, wout_ref[...])
    h_ref[...] = h
    hn2, route = _ffn_front(h, nfw_ref[...], wr_ref[...], br_ref[...])
    hn2_ref[...] = hn2
    route_ref[...] = route


def _sample_back(x_slab, ypart, ea, zoff, z, ysg, w):
    r = x_slab.shape[0]
    ins = [x_slab, ypart, ea, zoff, z, ysg, w['ssmnw'], w['wout'], w['nfw'], w['wr'], w['br']]
    out_shape = [
        jax.ShapeDtypeStruct((r, D_MODEL), F32),
        jax.ShapeDtypeStruct((r, D_MODEL), F32),
        jax.ShapeDtypeStruct((r, LANES), F32),
    ]
    return pl.pallas_call(
        _sample_back_kernel,
        grid=(1,),
        in_specs=[_const_spec(a.shape) for a in ins],
        out_specs=[_const_spec(s.shape) for s in out_shape],
        out_shape=out_shape,
        compiler_params=pltpu.CompilerParams(
            dimension_semantics=("arbitrary",), vmem_limit_bytes=VMEM_LIMIT),
        name="sample_back",
    )(*ins)


def _row_copy_kernel(src_idx_ref, dst_idx_ref, src_hbm, dst_hbm, sem, *, rows):
    def row_copy(r):
        return pltpu.make_async_copy(
            src_hbm.at[pl.ds(src_idx_ref[0, r], 1)], dst_hbm.at[pl.ds(dst_idx_ref[0, r], 1)], sem)

    def start(r, carry):
        @pl.when(dst_idx_ref[0, r] >= 0)
        def _():
            row_copy(r).start()
        return carry

    def wait(r, carry):
        @pl.when(dst_idx_ref[0, r] >= 0)
        def _():
            row_copy(r).wait()
        return carry

    lax.fori_loop(0, rows, start, 0)
    lax.fori_loop(0, rows, wait, 0)


def _row_copy(src_idx, dst_idx, src, n_dst, rows):
    n = src_idx.shape[0]
    steps = n // rows
    idx_spec = pl.BlockSpec((None, 1, rows), lambda i: (i, 0, 0), memory_space=pltpu.SMEM)
    any_spec = pl.BlockSpec(memory_space=pl.ANY)
    return pl.pallas_call(
        functools.partial(_row_copy_kernel, rows=rows),
        grid=(steps,),
        in_specs=[idx_spec, idx_spec, any_spec],
        out_specs=any_spec,
        out_shape=jax.ShapeDtypeStruct((n_dst, src.shape[1]), src.dtype),
        scratch_shapes=[pltpu.SemaphoreType.DMA(())],
        compiler_params=pltpu.CompilerParams(dimension_semantics=("arbitrary",)),
        name="row_copy",
    )(src_idx.reshape(steps, 1, rows), dst_idx.reshape(steps, 1, rows), src)


def _expert_mlp_kernel(be_ref, nu_ref, x_ref, wg_ref, wu_ref, wd_ref, o_ref):
    del be_ref
    used = pl.program_id(0) < nu_ref[0]

    @pl.when(used)
    def _():
        x = x_ref[...].astype(BF16)
        hmid = _silu(_dot(x, wg_ref[...])) * _dot(x, wu_ref[...])
        o_ref[...] = _dot(hmid.astype(BF16), wd_ref[...])

    @pl.when(jnp.logical_not(used))
    def _():
        o_ref[...] = jnp.zeros(o_ref.shape, F32)


def _expert_mlp(blk_expert, n_used, xs, wg, wu, wd, bm):
    n_blocks = blk_expert.shape[0]
    row_map = lambda j, be, nu: (jnp.minimum(j, nu[0] - 1), 0)
    w_map = lambda j, be, nu: (be[j], 0, 0)
    grid_spec = pltpu.PrefetchScalarGridSpec(
        num_scalar_prefetch=2,
        grid=(n_blocks,),
        in_specs=[
            pl.BlockSpec((bm, D_MODEL), row_map),
            pl.BlockSpec((None, D_MODEL, D_FF), w_map),
            pl.BlockSpec((None, D_MODEL, D_FF), w_map),
            pl.BlockSpec((None, D_FF, D_MODEL), w_map),
        ],
        out_specs=pl.BlockSpec((bm, D_MODEL), lambda j, be, nu: (j, 0)),
    )
    return pl.pallas_call(
        _expert_mlp_kernel,
        grid_spec=grid_spec,
        out_shape=jax.ShapeDtypeStruct(xs.shape, F32),
        compiler_params=pltpu.CompilerParams(
            dimension_semantics=("arbitrary",), vmem_limit_bytes=VMEM_LIMIT),
        name="expert_mlp",
    )(blk_expert, n_used, xs, wg, wu, wd)


def _moe_plan(e_idx, bm, n_blocks):
    n_tok = e_idx.shape[0]
    e_flat = e_idx.reshape(-1)
    order = jnp.argsort(e_flat).astype(jnp.int32)
    counts = jnp.sum((e_flat[:, None] == jnp.arange(N_EXPERTS, dtype=jnp.int32)[None, :]).astype(jnp.int32),
                     axis=0)
    nblk = (counts + bm - 1) // bm
    blk_end = jnp.cumsum(nblk)
    blk_start = blk_end - nblk
    n_used = blk_end[-1]
    start = jnp.cumsum(counts) - counts
    j = jnp.arange(n_blocks, dtype=jnp.int32)
    be = jnp.minimum(jnp.searchsorted(blk_end, j, side='right'), N_EXPERTS - 1).astype(jnp.int32)
    be = jnp.where(j < n_used, be, be[jnp.maximum(n_used - 1, 0)])
    r = jnp.arange(n_blocks * bm, dtype=jnp.int32)
    jb = r // bm
    e_r = be[jb]
    off = (jb - blk_start[e_r]) * bm + r % bm
    valid = (jb < n_used) & (off < counts[e_r])
    i = order[jnp.where(valid, start[e_r] + off, 0)]
    tok = jnp.where(valid, i // 2, r % n_tok)
    dst = jnp.where(valid, (i % 2) * n_tok + i // 2, -1)
    return tok.astype(jnp.int32), dst.astype(jnp.int32), be, n_used.astype(jnp.int32).reshape(1)


def _final_kernel(h_ref, p0_ref, p1_ref, route_ref, w_ref, o_ref):
    route = route_ref[...]
    y = h_ref[...] + route[:, 2:3] * p0_ref[...] + route[:, 3:4] * p1_ref[...]
    o_ref[...] = _rms(y, w_ref[...])


def _final(h, parts, route, w, tf):
    n_tok = h.shape[0]
    row = lambda d: pl.BlockSpec((tf, d), lambda i: (i, 0))
    part = lambda k: pl.BlockSpec((None, tf, D_MODEL), lambda i: (k, i, 0))
    return pl.pallas_call(
        _final_kernel,
        grid=(n_tok // tf,),
        in_specs=[row(D_MODEL), part(0), part(1), row(LANES), _const_spec(w.shape)],
        out_specs=row(D_MODEL),
        out_shape=jax.ShapeDtypeStruct((n_tok, D_MODEL), F32),
        compiler_params=pltpu.CompilerParams(
            dimension_semantics=("arbitrary",), vmem_limit_bytes=VMEM_LIMIT),
        name="final_combine",
    )(h, parts, parts, route, w)


PROMPT_TILE = 256
MOE_BLOCK_ROWS = 256
COPY_ROWS = 512
FINAL_TILE = 512
STATE_SEQS = 8


def _pad_lanes(a, width=LANES):
    return jnp.pad(a, [(0, 0)] * (a.ndim - 1) + [(0, width - a.shape[-1])])


def _layer_weights(l, norm_mix_w, w_in, conv_w, conv_b, dt_bias, a_log, d_skip, ssm_norm_w, sg_norm_w,
                   sg_w, sg_b, w_out, norm_ffn_w, w_router_group, b_router_group, w_router_expert,
                   b_router_expert, n_dec):
    c0, c1, c2 = D_SSM, D_SSM + C_CONV, D_SSM + C_CONV + SSM_HEADS
    wi = w_in[l]
    causal = jnp.tril(jnp.ones((CHUNK, CHUNK), bool))
    sgw_l = jnp.where(causal, sg_w[l], 0.0)
    head_of_lane = jnp.arange(D_SSM, dtype=jnp.int32) // SSM_HEAD_DIM
    expand = (jnp.arange(LANES, dtype=jnp.int32)[:, None] == head_of_lane[None, :]).astype(F32)
    row = lambda a: a.reshape(1, -1).astype(F32)
    return {
        'nmw': row(norm_mix_w[l]),
        'wz': wi[:, :c0].astype(BF16),
        'wxbc': wi[:, c0:c1].astype(BF16),
        'wdt': _pad_lanes(wi[:, c1:c2]).astype(BF16),
        'wuv': wi[:, c2:].astype(BF16),
        'convw': conv_w[l].astype(F32),
        'convb': row(conv_b[l]),
        'dtb': _pad_lanes(row(dt_bias[l])),
        'alog': _pad_lanes(row(a_log[l])),
        'dskip': row(jnp.repeat(d_skip[l], SSM_HEAD_DIM)),
        'ssmnw': row(ssm_norm_w[l]),
        'sgnw': row(sg_norm_w[l]),
        'sgw': sgw_l.astype(BF16),
        'sgb': jnp.repeat(sg_b[l].T, SG_HEAD_DIM, axis=1).astype(F32),
        'w4': jnp.repeat(jnp.transpose(sgw_l[:, :n_dec, :n_dec], (1, 2, 0)), SG_HEAD_DIM, axis=2).astype(F32),
        'b4': jnp.repeat(sg_b[l][:, :n_dec].T, SG_HEAD_DIM, axis=1).astype(F32),
        'expand': expand,
        'wout': w_out[l].astype(BF16),
        'nfw': row(norm_ffn_w[l]),
        'wr': _pad_lanes(jnp.concatenate([w_router_group[l], w_router_expert[l]], axis=1)).astype(F32),
        'br': _pad_lanes(row(jnp.concatenate([b_router_group[l], b_router_expert[l]]))),
    }


def _layer(l, xp, xs_slab, state_conv, state_ssm, w, w_gate, w_up, w_down):
    bp, seq, _ = xp.shape
    nb, nt = state_conv.shape[1], xs_slab.shape[0] // state_conv.shape[1]
    q = HEADS_PER_GROUP * SSM_HEAD_DIM

    h_p, hn2_p, route_p, convt_p, ssm_p = _prompt_mixer(xp, w, PROMPT_TILE)

    cprev = jnp.transpose(state_conv[l], (1, 0, 2))
    z, ypart, ea, ysg, v, convn, cmat, bmat, xw, dec = _sample_front(xs_slab, cprev, w, nb, nt)
    to_seq = lambda a, d: jnp.pad(
        jnp.transpose(a.reshape(nt, nb, SSM_GROUPS, d), (1, 2, 0, 3)),
        ((0, 0), (0, 0), (0, SUBLANES - nt), (0, 0)))
    ssm_s, zoff = _sample_state(
        dec[:, :SSM_HEADS], state_ssm[l].reshape(nb, SSM_GROUPS, q, SSM_STATE),
        to_seq(cmat, SSM_STATE), to_seq(bmat, SSM_STATE), to_seq(xw, q), STATE_SEQS)
    zoff = jnp.transpose(zoff[:, :, :nt], (2, 0, 1, 3)).reshape(nt * nb, D_SSM)
    h_s, hn2_s, route_s = _sample_back(xs_slab, ypart, ea, zoff, z, ysg, w)

    n_p = bp * seq
    h = jnp.concatenate([h_p.reshape(n_p, D_MODEL), h_s], axis=0)
    hn2 = jnp.concatenate([hn2_p.reshape(n_p, D_MODEL), hn2_s], axis=0)
    route = jnp.concatenate([route_p.reshape(n_p, LANES), route_s], axis=0)
    n_tok = h.shape[0]
    bm = MOE_BLOCK_ROWS
    n_blocks = -(-(2 * n_tok) // bm) + N_EXPERTS
    n_blocks = -(-n_blocks * bm // COPY_ROWS) * COPY_ROWS // bm
    tok, dst, blk_expert, n_used = _moe_plan(route[:, :2].astype(jnp.int32), bm, n_blocks)
    rows = jnp.arange(n_blocks * bm, dtype=jnp.int32)
    xs_sorted = _row_copy(tok, rows, hn2, n_blocks * bm, COPY_ROWS)
    out_sorted = _expert_mlp(blk_expert, n_used, xs_sorted, w_gate[l].astype(BF16), w_up[l].astype(BF16),
                             w_down[l].astype(BF16), bm)
    parts = _row_copy(rows, dst, out_sorted, 2 * n_tok, COPY_ROWS).reshape(2, n_tok, D_MODEL)
    h_out = h
    outs = dict(
        h=h_out, parts=parts, route=route,
        conv_p=convt_p[:, SUBLANES - (CONV_K - 1):],
        ssm_p=ssm_p.reshape(bp, SSM_GROUPS, HEADS_PER_GROUP, SSM_HEAD_DIM, SSM_STATE),
        conv_s=jnp.transpose(convn, (1, 0, 2)),
        ssm_s=ssm_s.reshape(nb, SSM_GROUPS, HEADS_PER_GROUP, SSM_HEAD_DIM, SSM_STATE),
        v_s=jnp.transpose(v.reshape(nt, nb, D_SG), (1, 0, 2)),
    )
    return outs


def _combine(outs, w_row, final):
    del final
    return _final(outs['h'], outs['parts'], outs['route'], w_row, FINAL_TILE)


def kernel(x_prompt, x_sample, state_conv, state_ssm, norm_mix_w, w_in, conv_w, conv_b, dt_bias, a_log, d_skip, ssm_norm_w, sg_norm_w, sg_w, sg_b, w_out, norm_ffn_w, w_router_group, b_router_group, w_router_expert, b_router_expert, w_gate, w_up, w_down, norm_final_w):
    depth = w_in.shape[0]
    assert depth == 1, "the fused final norm assumes a single layer"
    bp, seq, _ = x_prompt.shape
    nb, nt, _ = x_sample.shape
    l = 0
    w = _layer_weights(l, norm_mix_w, w_in, conv_w, conv_b, dt_bias, a_log, d_skip, ssm_norm_w, sg_norm_w,
                       sg_w, sg_b, w_out, norm_ffn_w, w_router_group, b_router_group, w_router_expert,
                       b_router_expert, nt)
    xs_slab = jnp.transpose(x_sample, (1, 0, 2)).reshape(nt * nb, D_MODEL)
    outs = _layer(l, x_prompt, xs_slab, state_conv, state_ssm, w, w_gate, w_up, w_down)
    y = _combine(outs, norm_final_w.reshape(1, -1).astype(F32), True)
    n_p = bp * seq
    y_prompt = y[:n_p].reshape(bp, seq, D_MODEL)
    y_sample = jnp.transpose(y[n_p:].reshape(nt, nb, D_MODEL), (1, 0, 2))
    return (y_prompt, y_sample, outs['conv_p'][None], outs['ssm_p'][None], outs['conv_s'][None],
            outs['ssm_s'][None], outs['v_s'][None])
```

```python
import functools

import jax
import jax.numpy as jnp
from jax import lax
from jax.experimental import pallas as pl
from jax.experimental.pallas import tpu as pltpu

D_MODEL = 1024
D_SSM = 1024
SSM_HEAD_DIM = 64
SSM_HEADS = 16
SSM_GROUPS = 2
HEADS_PER_GROUP = 8
SSM_STATE = 128
CONV_K = 4
C_CONV = D_SSM + 2 * SSM_GROUPS * SSM_STATE
D_SG = 1024
SG_HEADS = 8
SG_HEAD_DIM = 128
CHUNK = 128
N_EXPERT_GROUPS = 4
EXPERTS_PER_GROUP = 8
N_EXPERTS = 32
D_FF = 512
EPS = 1e-6

LANES = 128
SUBLANES = 8
VMEM_LIMIT = 56 * 1024 * 1024

F32 = jnp.float32
BF16 = jnp.bfloat16
HIGHEST = lax.Precision.HIGHEST


def _dot(a, b):
    return jnp.dot(a, b, preferred_element_type=F32)


def _dot_f32(a, b):
    return jnp.dot(a, b, preferred_element_type=F32, precision=HIGHEST)


def _rms(x, w):
    return x * lax.rsqrt(jnp.mean(x * x, axis=-1, keepdims=True) + EPS) * w


def _silu(x):
    return x * (1.0 / (1.0 + jnp.exp(-x)))


def _gelu(x):
    return 0.5 * x * (1.0 + lax.erf(x * 0.7071067811865476))


def _softplus(x):
    return jnp.maximum(x, 0.0) + jnp.log1p(jnp.exp(-jnp.abs(x)))


def _gated_group_norm(y, z, w):
    g = y * _silu(z)
    half = D_SSM // SSM_GROUPS
    parts = []
    for k in range(SSM_GROUPS):
        gk = g[:, k * half:(k + 1) * half]
        parts.append(gk * lax.rsqrt(jnp.mean(gk * gk, axis=-1, keepdims=True) + EPS))
    return jnp.concatenate(parts, axis=1) * w


def _route(logits):
    r = logits.shape[0]
    lane = lax.broadcasted_iota(jnp.int32, (r, LANES), 1)
    lanef = lane.astype(F32)
    big = float(LANES)
    neg = -jnp.inf
    gmask = lane < N_EXPERT_GROUPS
    gl = jnp.where(gmask, logits, neg)
    ge = jnp.exp(gl - jnp.max(gl, axis=1, keepdims=True))
    p_grp = ge / jnp.sum(ge, axis=1, keepdims=True)
    g_p = jnp.max(p_grp, axis=1, keepdims=True)
    g_idx = jnp.min(jnp.where(gmask & (p_grp == g_p), lanef, big), axis=1, keepdims=True)
    lo = N_EXPERT_GROUPS + EXPERTS_PER_GROUP * g_idx
    emask = (lanef >= lo) & (lanef < lo + EXPERTS_PER_GROUP)
    el = jnp.where(emask, logits, neg)
    ee = jnp.exp(el - jnp.max(el, axis=1, keepdims=True))
    pe = ee / jnp.sum(ee, axis=1, keepdims=True)
    v1 = jnp.max(pe, axis=1, keepdims=True)
    i1 = jnp.min(jnp.where(emask & (pe == v1), lanef, big), axis=1, keepdims=True)
    rest = emask & (lanef != i1)
    pe2 = jnp.where(rest, pe, -1.0)
    v2 = jnp.max(pe2, axis=1, keepdims=True)
    i2 = jnp.min(jnp.where(rest & (pe2 == v2), lanef, big), axis=1, keepdims=True)
    den = v1 + v2
    w1 = g_p * v1 / den
    w2 = g_p * v2 / den
    out = jnp.where(lane == 0, i1 - N_EXPERT_GROUPS, 0.0)
    out = jnp.where(lane == 1, i2 - N_EXPERT_GROUPS, out)
    out = jnp.where(lane == 2, w1, out)
    out = jnp.where(lane == 3, w2, out)
    return out


def _ffn_front(h, nfw, wr, br):
    hn2 = _rms(h, nfw)
    logits = _dot_f32(hn2, wr) + br
    return hn2, _route(logits)


def _prompt_step(
        t, nt, x_ref, nmw_ref, wz_ref, wxbc_ref, wdt_ref, wuv_ref, convw_ref, convb_ref, dtb_ref,
        alog_ref, dskip_ref, ssmnw_ref, sgnw_ref, sgw_ref, sgb_ref, wout_ref, nfw_ref,
        wr_ref, br_ref, h_ref, hn2_ref, route_ref, convt_ref, ssm_ref,
        xbc_scr, state_scr, y_scr, mix_scr, tl):
    @pl.when(t == 0)
    def _():
        xbc_scr[0:SUBLANES, :] = jnp.zeros((SUBLANES, C_CONV), F32)
        state_scr[...] = jnp.zeros(state_scr.shape, F32)

    x = x_ref[...]
    hn = _rms(x, nmw_ref[...]).astype(BF16)
    z = _dot(hn, wz_ref[...])
    xbc = _dot(hn, wxbc_ref[...])
    dtr = _dot(hn, wdt_ref[...])
    uv = _gelu(_dot(hn, wuv_ref[...]))

    xbc_scr[SUBLANES:SUBLANES + tl, :] = xbc
    conv = convb_ref[...]
    for k in range(CONV_K):
        off = SUBLANES - (CONV_K - 1) + k
        conv = conv + convw_ref[k:k + 1, :] * xbc_scr[off:off + tl, :]
    tail = xbc_scr[tl:tl + SUBLANES, :]
    xbc_scr[0:SUBLANES, :] = tail
    convt_ref[...] = tail
    act = _silu(conv)
    xs = act[:, :D_SSM]
    bmat = act[:, D_SSM:D_SSM + SSM_GROUPS * SSM_STATE]
    cmat = act[:, D_SSM + SSM_GROUPS * SSM_STATE:]

    dt = _softplus(dtr + dtb_ref[...])
    a_row = -jnp.exp(alog_ref[...])

    li = lax.broadcasted_iota(jnp.int32, (CHUNK, CHUNK), 0)
    si = lax.broadcasted_iota(jnp.int32, (CHUNK, CHUNK), 1)
    causal = li >= si
    tri = jnp.where(causal, 1.0, 0.0).astype(F32)
    lo_half = si < SSM_HEAD_DIM

    for c in range(tl // CHUNK):
        rows = slice(c * CHUNK, (c + 1) * CHUNK)
        dt_c = dt[rows]
        acum = _dot_f32(tri, dt_c * a_row)
        acum_t = acum.T
        dt_t = dt_c.T
        w_t = dt_t * jnp.exp(acum_t[:, CHUNK - 1:CHUNK] - acum_t)
        xs_c = xs[rows]
        for g in range(SSM_GROUPS):
            gl = slice(g * SSM_STATE, (g + 1) * SSM_STATE)
            b_g = bmat[rows, gl]
            c_g = cmat[rows, gl]
            b_gt = b_g.T
            cb = _dot(c_g.astype(BF16), b_gt.astype(BF16))
            for j in range(HEADS_PER_GROUP // 2):
                pl_ = slice(g * 512 + j * LANES, g * 512 + (j + 1) * LANES)
                m_l, ec_l, s_l, ea_last = [], [], [], []
                for k in range(2):
                    hh = g * HEADS_PER_GROUP + 2 * j + k
                    colb = jnp.broadcast_to(acum[:, hh:hh + 1], (CHUNK, CHUNK))
                    rowb = jnp.broadcast_to(acum_t[hh:hh + 1, :], (CHUNK, CHUNK))
                    ea = jnp.exp(colb)
                    seg = jnp.where(causal, colb - rowb, 0.0)
                    lmat = jnp.where(causal, jnp.exp(seg), 0.0)
                    m_l.append(cb * lmat * jnp.broadcast_to(dt_t[hh:hh + 1, :], (CHUNK, CHUNK)))
                    ec_l.append(c_g * ea)
                    s_l.append(b_gt * jnp.broadcast_to(w_t[hh:hh + 1, :], (CHUNK, CHUNK)))
                    ea_last.append(ea[CHUNK - 1:CHUNK, :])
                xs_p = xs_c[:, pl_]
                st_p = state_scr[g, :, j * LANES:(j + 1) * LANES]
                rx = jnp.concatenate([jnp.where(lo_half, xs_p, 0.0),
                                      jnp.where(lo_half, 0.0, xs_p)], axis=0).astype(BF16)
                rs = jnp.concatenate([jnp.where(lo_half, st_p, 0.0),
                                      jnp.where(lo_half, 0.0, st_p)], axis=0).astype(BF16)
                lhs = jnp.concatenate(m_l + ec_l, axis=1).astype(BF16)
                y_p = _dot(lhs, jnp.concatenate([rx, rs], axis=0))
                y_scr[rows, pl_] = y_p
                dec = jnp.where(lo_half[0:1, :], ea_last[0], ea_last[1])
                upd = _dot(jnp.concatenate(s_l, axis=1).astype(BF16), rx)
                state_scr[g, :, j * LANES:(j + 1) * LANES] = st_p * dec + upd

    y = y_scr[...] + dskip_ref[...] * xs
    y_ssm = _gated_group_norm(y, z, ssmnw_ref[...])
    mix_scr[:, :D_SSM] = y_ssm.astype(BF16)

    u = uv[:, :D_SG]
    v = _rms(uv[:, D_SG:], sgnw_ref[...])
    vb = v.astype(BF16)
    for c in range(tl // CHUNK):
        rows = slice(c * CHUNK, (c + 1) * CHUNK)
        for hd in range(SG_HEADS):
            hl = slice(hd * SG_HEAD_DIM, (hd + 1) * SG_HEAD_DIM)
            sv = _dot(sgw_ref[hd], vb[rows, hl]) + sgb_ref[:, hl]
            mix_scr[rows, D_SSM + hd * SG_HEAD_DIM:D_SSM + (hd + 1) * SG_HEAD_DIM] = (
                u[rows, hl] * sv).astype(BF16)

    h = x + _dot(mix_scr[...], wout_ref[...])
    h_ref[...] = h
    hn2, route = _ffn_front(h, nfw_ref[...], wr_ref[...], br_ref[...])
    hn2_ref[...] = hn2
    route_ref[...] = route

    @pl.when(t == nt - 1)
    def _():
        for g in range(SSM_GROUPS):
            ssm_ref[g] = state_scr[g].T


def _prompt_mixer_kernel(
        x_ref, nmw_ref, wz_ref, wxbc_ref, wdt_ref, wuv_ref, convw_ref, convb_ref, dtb_ref,
        alog_ref, dskip_ref, ssmnw_ref, sgnw_ref, sgw_ref, sgb_ref, wout_ref, nfw_ref,
        wr_ref, br_ref, hs_ref, hn2s_ref, routes_ref,
        h_ref, hn2_ref, route_ref, convt_ref, ssm_ref,
        xbc_scr, state_scr, y_scr, mix_scr, *, tl, nt, n_main):
    i = pl.program_id(0)

    @pl.when(i >= n_main)
    def _():
        h_ref[...] = hs_ref[...]
        hn2_ref[...] = hn2s_ref[...]
        route_ref[...] = routes_ref[...]

    @pl.when(i < n_main)
    def _():
        _prompt_step(
            lax.rem(i, nt), nt, x_ref, nmw_ref, wz_ref, wxbc_ref, wdt_ref, wuv_ref, convw_ref,
            convb_ref, dtb_ref, alog_ref, dskip_ref, ssmnw_ref, sgnw_ref, sgw_ref, sgb_ref, wout_ref,
            nfw_ref, wr_ref, br_ref, h_ref, hn2_ref, route_ref, convt_ref, ssm_ref,
            xbc_scr, state_scr, y_scr, mix_scr, tl)


def _const_spec(shape):
    zeros = (0,) * len(shape)
    return pl.BlockSpec(shape, lambda *_: zeros)


def _prompt_mixer(x, w, tl, h_s, hn2_s, route_s):
    bsz, seq, _ = x.shape
    nt = seq // tl
    n_main = bsz * nt
    n_s = h_s.shape[0]
    assert seq % tl == 0 and n_s % tl == 0
    n_tok = bsz * seq + n_s
    main = lambda i: jnp.minimum(i, n_main - 1)
    tail = lambda i: jnp.maximum(i - n_main, 0)
    weights = [w['nmw'], w['wz'], w['wxbc'], w['wdt'], w['wuv'], w['convw'], w['convb'], w['dtb'],
               w['alog'], w['dskip'], w['ssmnw'], w['sgnw'], w['sgw'], w['sgb'], w['wout'], w['nfw'],
               w['wr'], w['br']]
    in_specs = ([pl.BlockSpec((tl, D_MODEL), lambda i: (main(i), 0))]
                + [_const_spec(a.shape) for a in weights]
                + [pl.BlockSpec((tl, d), lambda i: (tail(i), 0)) for d in (D_MODEL, D_MODEL, LANES)])
    q = HEADS_PER_GROUP * SSM_HEAD_DIM
    out_shape = [
        jax.ShapeDtypeStruct((n_tok, D_MODEL), F32),
        jax.ShapeDtypeStruct((n_tok, D_MODEL), F32),
        jax.ShapeDtypeStruct((n_tok, LANES), F32),
        jax.ShapeDtypeStruct((bsz, SUBLANES, C_CONV), F32),
        jax.ShapeDtypeStruct((bsz, SSM_GROUPS, q, SSM_STATE), F32),
    ]
    out_specs = [
        pl.BlockSpec((tl, D_MODEL), lambda i: (i, 0)),
        pl.BlockSpec((tl, D_MODEL), lambda i: (i, 0)),
        pl.BlockSpec((tl, LANES), lambda i: (i, 0)),
        pl.BlockSpec((None, SUBLANES, C_CONV), lambda i: (main(i) // nt, 0, 0)),
        pl.BlockSpec((None, SSM_GROUPS, q, SSM_STATE), lambda i: (main(i) // nt, 0, 0, 0)),
    ]
    scratch = [
        pltpu.VMEM((tl + SUBLANES, C_CONV), F32),
        pltpu.VMEM((SSM_GROUPS, SSM_STATE, q), F32),
        pltpu.VMEM((tl, D_SSM), F32),
        pltpu.VMEM((tl, D_SSM + D_SG), BF16),
    ]
    return pl.pallas_call(
        functools.partial(_prompt_mixer_kernel, tl=tl, nt=nt, n_main=n_main),
        grid=(n_main + n_s // tl,),
        in_specs=in_specs,
        out_specs=out_specs,
        out_shape=out_shape,
        scratch_shapes=scratch,
        compiler_params=pltpu.CompilerParams(
            dimension_semantics=("arbitrary",), vmem_limit_bytes=VMEM_LIMIT),
        name="prompt_mixer",
    )(x.reshape(bsz * seq, D_MODEL), *weights, h_s, hn2_s, route_s)


def _sample_front_kernel(
        x_ref, cprev_ref, nmw_ref, wz_ref, wxbc_ref, wdt_ref, wuv_ref, convw_ref, convb_ref,
        dtb_ref, alog_ref, dskip_ref, sgnw_ref, w4_ref, b4_ref, expand_ref,
        z_ref, ypart_ref, ea_ref, ysg_ref, v_ref, convn_ref, c_ref, b_ref, xw_ref, dec_ref,
        *, nb, nt):
    x = x_ref[...]
    hn = _rms(x, nmw_ref[...]).astype(BF16)
    z_ref[...] = _dot(hn, wz_ref[...])
    xbc = _dot(hn, wxbc_ref[...])
    dtr = _dot(hn, wdt_ref[...])
    uv = _gelu(_dot(hn, wuv_ref[...]))

    slab = lambda a, t: a[t * nb:(t + 1) * nb]
    full = [cprev_ref[k] for k in range(CONV_K - 1)] + [slab(xbc, t) for t in range(nt)]
    for k in range(CONV_K - 1):
        convn_ref[k] = full[nt + k]
    xs, bm, cm = [], [], []
    for t in range(nt):
        conv = convb_ref[...]
        for k in range(CONV_K):
            conv = conv + convw_ref[k:k + 1, :] * full[t + k]
        act = _silu(conv)
        xs.append(act[:, :D_SSM])
        bm.append(act[:, D_SSM:D_SSM + SSM_GROUPS * SSM_STATE])
        cm.append(act[:, D_SSM + SSM_GROUPS * SSM_STATE:])
        b_ref[t * nb:(t + 1) * nb, :] = bm[t]
        c_ref[t * nb:(t + 1) * nb, :] = cm[t]

    dt = _softplus(dtr + dtb_ref[...])
    a_row = -jnp.exp(alog_ref[...])
    dts = [slab(dt, t) for t in range(nt)]
    cum = []
    for t in range(nt):
        da = dts[t] * a_row
        cum.append(da if t == 0 else cum[t - 1] + da)
    dec_ref[...] = jnp.exp(cum[nt - 1])

    lane = lax.broadcasted_iota(jnp.int32, (nb, LANES), 1)
    first_group = lane < HEADS_PER_GROUP
    facs = [jnp.exp(cum[t]) for t in range(nt)]
    facs += [dts[s] * jnp.exp(cum[nt - 1] - cum[s]) for s in range(nt)]
    pairs = []
    for t in range(nt):
        for s in range(t + 1):
            cb = []
            for g in range(SSM_GROUPS):
                gl = slice(g * SSM_STATE, (g + 1) * SSM_STATE)
                cb.append(jnp.sum(cm[t][:, gl] * bm[s][:, gl], axis=1, keepdims=True))
            cbh = jnp.where(first_group, cb[0], cb[1])
            facs.append(jnp.exp(cum[t] - cum[s]) * dts[s] * cbh)
            pairs.append((t, s))
    fx = _dot_f32(jnp.concatenate(facs, axis=0), expand_ref[...])
    fslab = lambda i: fx[i * nb:(i + 1) * nb]
    for t in range(nt):
        ea_ref[t * nb:(t + 1) * nb, :] = fslab(t)
        xw_ref[t * nb:(t + 1) * nb, :] = xs[t] * fslab(nt + t)
    for t in range(nt):
        acc = dskip_ref[...] * xs[t]
        for i, (tt, s) in enumerate(pairs):
            if tt == t:
                acc = acc + fslab(2 * nt + i) * xs[s]
        ypart_ref[t * nb:(t + 1) * nb, :] = acc

    u = uv[:, :D_SG]
    v = _rms(uv[:, D_SG:], sgnw_ref[...])
    v_ref[...] = v
    for t in range(nt):
        sv = b4_ref[t:t + 1, :]
        for s in range(t + 1):
            sv = sv + w4_ref[t, s:s + 1, :] * slab(v, s)
        ysg_ref[t * nb:(t + 1) * nb, :] = slab(u, t) * sv


def _sample_front(x_slab, cprev, w, nb, nt):
    r = nb * nt
    weights = [w['nmw'], w['wz'], w['wxbc'], w['wdt'], w['wuv'], w['convw'], w['convb'], w['dtb'],
               w['alog'], w['dskip'], w['sgnw'], w['w4'], w['b4'], w['expand']]
    ins = [x_slab, cprev] + weights
    out_shape = [
        jax.ShapeDtypeStruct((r, D_SSM), F32),
        jax.ShapeDtypeStruct((r, D_SSM), F32),
        jax.ShapeDtypeStruct((r, D_SSM), F32),
        jax.ShapeDtypeStruct((r, D_SG), F32),
        jax.ShapeDtypeStruct((r, D_SG), F32),
        jax.ShapeDtypeStruct((CONV_K - 1, nb, C_CONV), F32),
        jax.ShapeDtypeStruct((r, SSM_GROUPS * SSM_STATE), F32),
        jax.ShapeDtypeStruct((r, SSM_GROUPS * SSM_STATE), F32),
        jax.ShapeDtypeStruct((r, D_SSM), F32),
        jax.ShapeDtypeStruct((nb, LANES), F32),
    ]
    return pl.pallas_call(
        functools.partial(_sample_front_kernel, nb=nb, nt=nt),
        grid=(1,),
        in_specs=[_const_spec(a.shape) for a in ins],
        out_specs=[_const_spec(s.shape) for s in out_shape],
        out_shape=out_shape,
        compiler_params=pltpu.CompilerParams(
            dimension_semantics=("arbitrary",), vmem_limit_bytes=VMEM_LIMIT),
        name="sample_front",
    )(*ins)


def _sample_state_kernel(dec_ref, st_ref, cq_ref, bq_ref, xw_ref, so_ref, z_ref, *, seqs):
    for bb in range(seqs):
        for g in range(SSM_GROUPS):
            s0 = st_ref[bb, g]
            z_ref[bb, g] = lax.dot_general(
                cq_ref[bb, g].astype(BF16), s0.astype(BF16), (((1,), (1,)), ((), ())),
                preferred_element_type=F32)
            upd = lax.dot_general(
                xw_ref[bb, g].astype(BF16), bq_ref[bb, g].astype(BF16), (((0,), (0,)), ((), ())),
                preferred_element_type=F32)
            for hh in range(HEADS_PER_GROUP):
                hs = slice(hh * SSM_HEAD_DIM, (hh + 1) * SSM_HEAD_DIM)
                so_ref[bb, g, hs, :] = s0[hs] * dec_ref[bb, g * HEADS_PER_GROUP + hh] + upd[hs]


def _sample_state(dec, state, cq, bq, xwq, seqs):
    nb = state.shape[0]
    q = HEADS_PER_GROUP * SSM_HEAD_DIM
    blk = lambda *tail: pl.BlockSpec((seqs, SSM_GROUPS) + tail, lambda i: (i, 0, 0, 0))
    return pl.pallas_call(
        functools.partial(_sample_state_kernel, seqs=seqs),
        grid=(nb // seqs,),
        in_specs=[
            pl.BlockSpec((seqs, SSM_HEADS), lambda i: (i, 0), memory_space=pltpu.SMEM),
            blk(q, SSM_STATE), blk(SUBLANES, SSM_STATE), blk(SUBLANES, SSM_STATE), blk(SUBLANES, q),
        ],
        out_specs=[blk(q, SSM_STATE), blk(SUBLANES, q)],
        out_shape=[
            jax.ShapeDtypeStruct((nb, SSM_GROUPS, q, SSM_STATE), F32),
            jax.ShapeDtypeStruct((nb, SSM_GROUPS, SUBLANES, q), F32),
        ],
        compiler_params=pltpu.CompilerParams(
            dimension_semantics=("arbitrary",), vmem_limit_bytes=VMEM_LIMIT),
        name="sample_state",
    )(dec, state, cq, bq, xwq)


def _sample_back_kernel(x_ref, ypart_ref, ea_ref, zoff_ref, z_ref, ysg_ref, ssmnw_ref, wout_ref,
                        nfw_ref, wr_ref, br_ref, h_ref, hn2_ref, route_ref):
    y = ypart_ref[...] + ea_ref[...] * zoff_ref[...]
    y_ssm = _gated_group_norm(y, z_ref[...], ssmnw_ref[...])
    mix_in = jnp.concatenate([y_ssm.astype(BF16), ysg_ref[...].astype(BF16)], axis=1)
    h = x_ref[...] + _dot(mix_in, wout_ref[...])
    h_ref[...] = h
    hn2, route = _ffn_front(h, nfw_ref[...], wr_ref[...], br_ref[...])
    hn2_ref[...] = hn2
    route_ref[...] = route


def _sample_back(x_slab, ypart, ea, zoff, z, ysg, w):
    r = x_slab.shape[0]
    ins = [x_slab, ypart, ea, zoff, z, ysg, w['ssmnw'], w['wout'], w['nfw'], w['wr'], w['br']]
    out_shape = [
        jax.ShapeDtypeStruct((r, D_MODEL), F32),
        jax.ShapeDtypeStruct((r, D_MODEL), F32),
        jax.ShapeDtypeStruct((r, LANES), F32),
    ]
    return pl.pallas_call(
        _sample_back_kernel,
        grid=(1,),
        in_specs=[_const_spec(a.shape) for a in ins],
        out_specs=[_const_spec(s.shape) for s in out_shape],
        out_shape=out_shape,
        compiler_params=pltpu.CompilerParams(
            dimension_semantics=("arbitrary",), vmem_limit_bytes=VMEM_LIMIT),
        name="sample_back",
    )(*ins)


def _expert_mlp_kernel(be_ref, nu_ref, tok_ref, tokn_ref, dst_ref, x_hbm, wg_ref, wu_ref, wd_ref,
                       parts_hbm, xbuf, obuf, gsem, ssem, *, bm):
    del be_ref
    j = pl.program_id(0)
    nu = nu_ref[0]
    slot = lax.rem(j, 2)

    def gather_start(idx_ref, s):
        def body(r, carry):
            pltpu.make_async_copy(x_hbm.at[pl.ds(idx_ref[0, r], 1)], xbuf.at[s, pl.ds(r, 1)],
                                  gsem.at[s]).start()
            return carry
        lax.fori_loop(0, bm, body, 0, unroll=8)

    def gather_wait(s):
        pltpu.make_async_copy(x_hbm.at[pl.ds(0, bm)], xbuf.at[s], gsem.at[s]).wait()

    def scatter_start(s):
        def body(r, carry):
            pltpu.make_async_copy(obuf.at[s, pl.ds(r, 1)], parts_hbm.at[pl.ds(dst_ref[0, r], 1)],
                                  ssem.at[s]).start()
            return carry
        lax.fori_loop(0, bm, body, 0, unroll=8)

    def scatter_wait(s):
        pltpu.make_async_copy(obuf.at[s], parts_hbm.at[pl.ds(0, bm)], ssem.at[s]).wait()

    @pl.when(j == 0)
    def _():
        n_real = parts_hbm.shape[0] - 2 * bm
        obuf[...] = jnp.zeros(obuf.shape, F32)
        spare = [pltpu.make_async_copy(obuf.at[s], parts_hbm.at[pl.ds(n_real + s * bm, bm)], ssem.at[s])
                 for s in range(2)]
        for cp in spare:
            cp.start()
        for cp in spare:
            cp.wait()

    @pl.when(j < nu)
    def _():
        @pl.when(j == 0)
        def _():
            gather_start(tok_ref, 0)

        gather_wait(slot)

        @pl.when(j + 1 < nu)
        def _():
            gather_start(tokn_ref, 1 - slot)

        @pl.when(j >= 2)
        def _():
            scatter_wait(slot)

        x = xbuf[slot].astype(BF16)
        hmid = _silu(_dot(x, wg_ref[...])) * _dot(x, wu_ref[...])
        obuf[slot] = _dot(hmid.astype(BF16), wd_ref[...])
        scatter_start(slot)

        @pl.when(j == nu - 1)
        def _():
            scatter_wait(slot)

            @pl.when(j >= 1)
            def _():
                scatter_wait(1 - slot)


def _expert_mlp(blk_expert, n_used, tok, dst, x, wg, wu, wd, bm, n_out):
    n_blocks = blk_expert.shape[0]
    w_map = lambda j, be, nu: (be[j], 0, 0)
    idx_spec = lambda f: pl.BlockSpec((None, 1, bm), lambda j, be, nu: (f(j), 0, 0),
                                      memory_space=pltpu.SMEM)
    any_spec = pl.BlockSpec(memory_space=pl.ANY)
    grid_spec = pltpu.PrefetchScalarGridSpec(
        num_scalar_prefetch=2,
        grid=(n_blocks,),
        in_specs=[
            idx_spec(lambda j: j),
            idx_spec(lambda j: jnp.minimum(j + 1, n_blocks - 1)),
            idx_spec(lambda j: j),
            any_spec,
            pl.BlockSpec((None, D_MODEL, D_FF), w_map),
            pl.BlockSpec((None, D_MODEL, D_FF), w_map),
            pl.BlockSpec((None, D_FF, D_MODEL), w_map),
        ],
        out_specs=any_spec,
        scratch_shapes=[
            pltpu.VMEM((2, bm, D_MODEL), F32),
            pltpu.VMEM((2, bm, D_MODEL), F32),
            pltpu.SemaphoreType.DMA((2,)),
            pltpu.SemaphoreType.DMA((2,)),
        ],
    )
    tok3 = tok.reshape(n_blocks, 1, bm)
    return pl.pallas_call(
        functools.partial(_expert_mlp_kernel, bm=bm),
        grid_spec=grid_spec,
        out_shape=jax.ShapeDtypeStruct((n_out, D_MODEL), F32),
        compiler_params=pltpu.CompilerParams(
            dimension_semantics=("arbitrary",), vmem_limit_bytes=VMEM_LIMIT),
        name="expert_mlp",
    )(blk_expert, n_used, tok3, tok3, dst.reshape(n_blocks, 1, bm), x, wg, wu, wd)


def _moe_plan(e_idx, bm, n_blocks):
    n_tok = e_idx.shape[0]
    e_flat = e_idx.reshape(-1)
    order = jnp.argsort(e_flat).astype(jnp.int32)
    experts = jnp.arange(N_EXPERTS, dtype=jnp.int32)
    counts = jnp.sum((e_flat[:, None] == experts[None, :]).astype(jnp.int32), axis=0)
    nblk = (counts + bm - 1) // bm
    blk_end = jnp.cumsum(nblk)
    blk_start = blk_end - nblk
    n_used = blk_end[-1]
    start = jnp.cumsum(counts) - counts
    j = jnp.arange(n_blocks, dtype=jnp.int32)
    be = jnp.sum((blk_end[None, :] <= jnp.minimum(j, n_used - 1)[:, None]).astype(jnp.int32), axis=1)
    be = jnp.minimum(be, N_EXPERTS - 1)
    onehot = be[:, None] == experts[None, :]
    pick = lambda tbl: jnp.sum(jnp.where(onehot, tbl[None, :], 0), axis=1)
    row0 = pick(start) + (j - pick(blk_start)) * bm
    row_end = pick(start) + pick(counts)
    q = jnp.arange(bm, dtype=jnp.int32)
    valid = (j[:, None] < n_used) & (row0[:, None] + q[None, :] < row_end[:, None])
    order_pad = jnp.concatenate([order, jnp.zeros((bm,), jnp.int32)])
    window = jax.vmap(lambda o: lax.dynamic_slice(order_pad, (o,), (bm,)))(jnp.clip(row0, 0, 2 * n_tok))
    tok = jnp.where(valid, window // 2, (j[:, None] * bm + q[None, :]) % n_tok)
    dst = jnp.where(valid, (window % 2) * n_tok + window // 2,
                    2 * n_tok + (j[:, None] % 2) * bm + q[None, :])
    return tok.astype(jnp.int32), dst.astype(jnp.int32), be, n_used.astype(jnp.int32).reshape(1)


def _final_kernel(h_ref, p0_ref, p1_ref, route_ref, w_ref, op_ref, os_ref, *, n_prompt_steps):
    route = route_ref[...]
    y = _rms(h_ref[...] + route[:, 2:3] * p0_ref[...] + route[:, 3:4] * p1_ref[...], w_ref[...])
    i = pl.program_id(0)

    @pl.when(i < n_prompt_steps)
    def _():
        op_ref[...] = y

    @pl.when(i >= n_prompt_steps)
    def _():
        os_ref[...] = y


def _final(h, parts, route, w, tf, n_prompt):
    n_tok = h.shape[0]
    assert n_tok % tf == 0 and n_prompt % tf == 0
    n_p = n_prompt // tf
    row = lambda d: pl.BlockSpec((tf, d), lambda i: (i, 0))
    part = lambda k: pl.BlockSpec((tf, D_MODEL), lambda i: (k * (n_tok // tf) + i, 0))
    return pl.pallas_call(
        functools.partial(_final_kernel, n_prompt_steps=n_p),
        grid=(n_tok // tf,),
        in_specs=[row(D_MODEL), part(0), part(1), row(LANES), _const_spec(w.shape)],
        out_specs=[pl.BlockSpec((tf, D_MODEL), lambda i: (jnp.minimum(i, n_p - 1), 0)),
                   pl.BlockSpec((tf, D_MODEL), lambda i: (jnp.maximum(i - n_p, 0), 0))],
        out_shape=[jax.ShapeDtypeStruct((n_prompt, D_MODEL), F32),
                   jax.ShapeDtypeStruct((n_tok - n_prompt, D_MODEL), F32)],
        compiler_params=pltpu.CompilerParams(
            dimension_semantics=("arbitrary",), vmem_limit_bytes=VMEM_LIMIT),
        name="final_combine",
    )(h, parts, parts, route, w)


PROMPT_TILE = 256
MOE_BLOCK_ROWS = 256
FINAL_TILE = 512
STATE_SEQS = 8


def _pad_lanes(a, width=LANES):
    return jnp.pad(a, [(0, 0)] * (a.ndim - 1) + [(0, width - a.shape[-1])])


def _layer_weights(l, norm_mix_w, w_in, conv_w, conv_b, dt_bias, a_log, d_skip, ssm_norm_w, sg_norm_w,
                   sg_w, sg_b, w_out, norm_ffn_w, w_router_group, b_router_group, w_router_expert,
                   b_router_expert, n_dec):
    c0, c1, c2 = D_SSM, D_SSM + C_CONV, D_SSM + C_CONV + SSM_HEADS
    wi = w_in[l]
    causal = jnp.tril(jnp.ones((CHUNK, CHUNK), bool))
    sgw_l = jnp.where(causal, sg_w[l], 0.0)
    head_of_lane = jnp.arange(D_SSM, dtype=jnp.int32) // SSM_HEAD_DIM
    expand = (jnp.arange(LANES, dtype=jnp.int32)[:, None] == head_of_lane[None, :]).astype(F32)
    row = lambda a: a.reshape(1, -1).astype(F32)
    return {
        'nmw': row(norm_mix_w[l]),
        'wz': wi[:, :c0].astype(BF16),
        'wxbc': wi[:, c0:c1].astype(BF16),
        'wdt': _pad_lanes(wi[:, c1:c2]).astype(BF16),
        'wuv': wi[:, c2:].astype(BF16),
        'convw': conv_w[l].astype(F32),
        'convb': row(conv_b[l]),
        'dtb': _pad_lanes(row(dt_bias[l])),
        'alog': _pad_lanes(row(a_log[l])),
        'dskip': row(jnp.repeat(d_skip[l], SSM_HEAD_DIM)),
        'ssmnw': row(ssm_norm_w[l]),
        'sgnw': row(sg_norm_w[l]),
        'sgw': sgw_l.astype(BF16),
        'sgb': jnp.repeat(sg_b[l].T, SG_HEAD_DIM, axis=1).astype(F32),
        'w4': jnp.repeat(jnp.transpose(sgw_l[:, :n_dec, :n_dec], (1, 2, 0)), SG_HEAD_DIM, axis=2).astype(F32),
        'b4': jnp.repeat(sg_b[l][:, :n_dec].T, SG_HEAD_DIM, axis=1).astype(F32),
        'expand': expand,
        'wout': w_out[l].astype(BF16),
        'nfw': row(norm_ffn_w[l]),
        'wr': _pad_lanes(jnp.concatenate([w_router_group[l], w_router_expert[l]], axis=1)).astype(F32),
        'br': _pad_lanes(row(jnp.concatenate([b_router_group[l], b_router_expert[l]]))),
    }


def _layer(l, xp, xs_slab, state_conv, state_ssm, w, w_gate, w_up, w_down):
    bp, seq, _ = xp.shape
    nb, nt = state_conv.shape[1], xs_slab.shape[0] // state_conv.shape[1]
    q = HEADS_PER_GROUP * SSM_HEAD_DIM

    cprev = jnp.transpose(state_conv[l], (1, 0, 2))
    z, ypart, ea, ysg, v, convn, cmat, bmat, xw, dec = _sample_front(xs_slab, cprev, w, nb, nt)
    to_seq = lambda a, d: jnp.pad(
        jnp.transpose(a.reshape(nt, nb, SSM_GROUPS, d), (1, 2, 0, 3)),
        ((0, 0), (0, 0), (0, SUBLANES - nt), (0, 0)))
    ssm_s, zoff = _sample_state(
        dec[:, :SSM_HEADS], state_ssm[l].reshape(nb, SSM_GROUPS, q, SSM_STATE),
        to_seq(cmat, SSM_STATE), to_seq(bmat, SSM_STATE), to_seq(xw, q), STATE_SEQS)
    zoff = jnp.transpose(zoff[:, :, :nt], (2, 0, 1, 3)).reshape(nt * nb, D_SSM)
    h_s, hn2_s, route_s = _sample_back(xs_slab, ypart, ea, zoff, z, ysg, w)

    h, hn2, route, convt_p, ssm_p = _prompt_mixer(xp, w, PROMPT_TILE, h_s, hn2_s, route_s)

    n_tok = h.shape[0]
    bm = MOE_BLOCK_ROWS
    n_blocks = -(-(2 * n_tok) // bm) + N_EXPERTS
    tok, dst, blk_expert, n_used = _moe_plan(route[:, :2].astype(jnp.int32), bm, n_blocks)
    parts = _expert_mlp(blk_expert, n_used, tok, dst, hn2, w_gate[l].astype(BF16), w_up[l].astype(BF16),
                        w_down[l].astype(BF16), bm, 2 * n_tok + 2 * bm)
    outs = dict(
        h=h, parts=parts, route=route,
        conv_p=convt_p[:, SUBLANES - (CONV_K - 1):],
        ssm_p=ssm_p.reshape(bp, SSM_GROUPS, HEADS_PER_GROUP, SSM_HEAD_DIM, SSM_STATE),
        conv_s=jnp.transpose(convn, (1, 0, 2)),
        ssm_s=ssm_s.reshape(nb, SSM_GROUPS, HEADS_PER_GROUP, SSM_HEAD_DIM, SSM_STATE),
        v_s=jnp.transpose(v.reshape(nt, nb, D_SG), (1, 0, 2)),
    )
    return outs


def kernel(x_prompt, x_sample, state_conv, state_ssm, norm_mix_w, w_in, conv_w, conv_b, dt_bias, a_log, d_skip, ssm_norm_w, sg_norm_w, sg_w, sg_b, w_out, norm_ffn_w, w_router_group, b_router_group, w_router_expert, b_router_expert, w_gate, w_up, w_down, norm_final_w):
    depth = w_in.shape[0]
    assert depth == 1, "the fused final norm assumes a single layer"
    bp, seq, _ = x_prompt.shape
    nb, nt, _ = x_sample.shape
    l = 0
    w = _layer_weights(l, norm_mix_w, w_in, conv_w, conv_b, dt_bias, a_log, d_skip, ssm_norm_w, sg_norm_w,
                       sg_w, sg_b, w_out, norm_ffn_w, w_router_group, b_router_group, w_router_expert,
                       b_router_expert, nt)
    xs_slab = jnp.transpose(x_sample, (1, 0, 2)).reshape(nt * nb, D_MODEL)
    outs = _layer(l, x_prompt, xs_slab, state_conv, state_ssm, w, w_gate, w_up, w_down)
    y_p, y_s = _final(outs['h'], outs['parts'], outs['route'], norm_final_w.reshape(1, -1).astype(F32),
                      FINAL_TILE, bp * seq)
    y_prompt = y_p.reshape(bp, seq, D_MODEL)
    y_sample = jnp.transpose(y_s.reshape(nt, nb, D_MODEL), (1, 0, 2))
    return (y_prompt, y_sample, outs['conv_p'][None], outs['ssm_p'][None], outs['conv_s'][None],
            outs['ssm_s'][None], outs['v_s'][None])
```

```python
import functools

import jax
import jax.numpy as jnp
from jax import lax
from jax.experimental import pallas as pl
from jax.experimental.pallas import tpu as pltpu

D_MODEL = 1024
D_SSM = 1024
SSM_HEAD_DIM = 64
SSM_HEADS = 16
SSM_GROUPS = 2
HEADS_PER_GROUP = 8
SSM_STATE = 128
CONV_K = 4
C_CONV = D_SSM + 2 * SSM_GROUPS * SSM_STATE
D_SG = 1024
SG_HEADS = 8
SG_HEAD_DIM = 128
CHUNK = 128
N_EXPERT_GROUPS = 4
EXPERTS_PER_GROUP = 8
N_EXPERTS = 32
D_FF = 512
EPS = 1e-6

LANES = 128
SUBLANES = 8
VMEM_LIMIT = 56 * 1024 * 1024

F32 = jnp.float32
BF16 = jnp.bfloat16
HIGHEST = lax.Precision.HIGHEST


def _dot(a, b):
    return jnp.dot(a, b, preferred_element_type=F32)


def _dot_f32(a, b):
    return jnp.dot(a, b, preferred_element_type=F32, precision=HIGHEST)


def _rms(x, w):
    return x * lax.rsqrt(jnp.mean(x * x, axis=-1, keepdims=True) + EPS) * w


def _silu(x):
    return x * (1.0 / (1.0 + jnp.exp(-x)))


def _gelu(x):
    return 0.5 * x * (1.0 + lax.erf(x * 0.7071067811865476))


def _softplus(x):
    return jnp.maximum(x, 0.0) + jnp.log1p(jnp.exp(-jnp.abs(x)))


def _gated_group_norm(y, z, w):
    g = y * _silu(z)
    half = D_SSM // SSM_GROUPS
    parts = []
    for k in range(SSM_GROUPS):
        gk = g[:, k * half:(k + 1) * half]
        parts.append(gk * lax.rsqrt(jnp.mean(gk * gk, axis=-1, keepdims=True) + EPS))
    return jnp.concatenate(parts, axis=1) * w


def _route(logits):
    r = logits.shape[0]
    lane = lax.broadcasted_iota(jnp.int32, (r, LANES), 1)
    lanef = lane.astype(F32)
    big = float(LANES)
    neg = -jnp.inf
    gmask = lane < N_EXPERT_GROUPS
    gl = jnp.where(gmask, logits, neg)
    ge = jnp.exp(gl - jnp.max(gl, axis=1, keepdims=True))
    p_grp = ge / jnp.sum(ge, axis=1, keepdims=True)
    g_p = jnp.max(p_grp, axis=1, keepdims=True)
    g_idx = jnp.min(jnp.where(gmask & (p_grp == g_p), lanef, big), axis=1, keepdims=True)
    lo = N_EXPERT_GROUPS + EXPERTS_PER_GROUP * g_idx
    emask = (lanef >= lo) & (lanef < lo + EXPERTS_PER_GROUP)
    el = jnp.where(emask, logits, neg)
    ee = jnp.exp(el - jnp.max(el, axis=1, keepdims=True))
    pe = ee / jnp.sum(ee, axis=1, keepdims=True)
    v1 = jnp.max(pe, axis=1, keepdims=True)
    i1 = jnp.min(jnp.where(emask & (pe == v1), lanef, big), axis=1, keepdims=True)
    rest = emask & (lanef != i1)
    pe2 = jnp.where(rest, pe, -1.0)
    v2 = jnp.max(pe2, axis=1, keepdims=True)
    i2 = jnp.min(jnp.where(rest & (pe2 == v2), lanef, big), axis=1, keepdims=True)
    den = v1 + v2
    w1 = g_p * v1 / den
    w2 = g_p * v2 / den
    out = jnp.where(lane == 0, i1 - N_EXPERT_GROUPS, 0.0)
    out = jnp.where(lane == 1, i2 - N_EXPERT_GROUPS, out)
    out = jnp.where(lane == 2, w1, out)
    out = jnp.where(lane == 3, w2, out)
    return out


def _ffn_front(h, nfw, wr, br):
    hn2 = _rms(h, nfw)
    logits = _dot_f32(hn2, wr) + br
    return hn2, _route(logits)


def _prompt_step(
        t, nt, x_ref, nmw_ref, wz_ref, wxbc_ref, wdt_ref, wuv_ref, convw_ref, convb_ref, dtb_ref,
        alog_ref, dskip_ref, ssmnw_ref, sgnw_ref, sgw_ref, sgb_ref, wout_ref, nfw_ref,
        wr_ref, br_ref, h_ref, hn2_ref, route_ref, convt_ref, ssm_ref,
        xbc_scr, state_scr, y_scr, mix_scr, tl):
    @pl.when(t == 0)
    def _():
        xbc_scr[0:SUBLANES, :] = jnp.zeros((SUBLANES, C_CONV), F32)
        state_scr[...] = jnp.zeros(state_scr.shape, F32)

    x = x_ref[...]
    hn = _rms(x, nmw_ref[...]).astype(BF16)
    z = _dot(hn, wz_ref[...])
    xbc = _dot(hn, wxbc_ref[...])
    dtr = _dot(hn, wdt_ref[...])
    uv = _gelu(_dot(hn, wuv_ref[...]))

    xbc_scr[SUBLANES:SUBLANES + tl, :] = xbc
    conv = convb_ref[...]
    for k in range(CONV_K):
        off = SUBLANES - (CONV_K - 1) + k
        conv = conv + convw_ref[k:k + 1, :] * xbc_scr[off:off + tl, :]
    tail = xbc_scr[tl:tl + SUBLANES, :]
    xbc_scr[0:SUBLANES, :] = tail
    convt_ref[...] = tail
    act = _silu(conv)
    xs = act[:, :D_SSM]
    bmat = act[:, D_SSM:D_SSM + SSM_GROUPS * SSM_STATE]
    cmat = act[:, D_SSM + SSM_GROUPS * SSM_STATE:]

    dt = _softplus(dtr + dtb_ref[...])
    a_row = -jnp.exp(alog_ref[...])

    li = lax.broadcasted_iota(jnp.int32, (CHUNK, CHUNK), 0)
    si = lax.broadcasted_iota(jnp.int32, (CHUNK, CHUNK), 1)
    causal = li >= si
    tri = jnp.where(causal, 1.0, 0.0).astype(F32)
    lo_half = si < SSM_HEAD_DIM

    for c in range(tl // CHUNK):
        rows = slice(c * CHUNK, (c + 1) * CHUNK)
        dt_c = dt[rows]
        acum = _dot_f32(tri, dt_c * a_row)
        acum_t = acum.T
        dt_t = dt_c.T
        w_t = dt_t * jnp.exp(acum_t[:, CHUNK - 1:CHUNK] - acum_t)
        xs_c = xs[rows]
        for g in range(SSM_GROUPS):
            gl = slice(g * SSM_STATE, (g + 1) * SSM_STATE)
            b_g = bmat[rows, gl]
            c_g = cmat[rows, gl]
            b_gt = b_g.T
            cb = _dot(c_g.astype(BF16), b_gt.astype(BF16))
            for j in range(HEADS_PER_GROUP // 2):
                pl_ = slice(g * 512 + j * LANES, g * 512 + (j + 1) * LANES)
                m_l, ec_l, s_l, ea_last = [], [], [], []
                for k in range(2):
                    hh = g * HEADS_PER_GROUP + 2 * j + k
                    colb = jnp.broadcast_to(acum[:, hh:hh + 1], (CHUNK, CHUNK))
                    rowb = jnp.broadcast_to(acum_t[hh:hh + 1, :], (CHUNK, CHUNK))
                    ea = jnp.exp(colb)
                    seg = jnp.where(causal, colb - rowb, 0.0)
                    lmat = jnp.where(causal, jnp.exp(seg), 0.0)
                    m_l.append(cb * lmat * jnp.broadcast_to(dt_t[hh:hh + 1, :], (CHUNK, CHUNK)))
                    ec_l.append(c_g * ea)
                    s_l.append(b_gt * jnp.broadcast_to(w_t[hh:hh + 1, :], (CHUNK, CHUNK)))
                    ea_last.append(ea[CHUNK - 1:CHUNK, :])
                xs_p = xs_c[:, pl_]
                st_p = state_scr[g, :, j * LANES:(j + 1) * LANES]
                rx = jnp.concatenate([jnp.where(lo_half, xs_p, 0.0),
                                      jnp.where(lo_half, 0.0, xs_p)], axis=0).astype(BF16)
                rs = jnp.concatenate([jnp.where(lo_half, st_p, 0.0),
                                      jnp.where(lo_half, 0.0, st_p)], axis=0).astype(BF16)
                lhs = jnp.concatenate(m_l + ec_l, axis=1).astype(BF16)
                y_p = _dot(lhs, jnp.concatenate([rx, rs], axis=0))
                y_scr[rows, pl_] = y_p
                dec = jnp.where(lo_half[0:1, :], ea_last[0], ea_last[1])
                upd = _dot(jnp.concatenate(s_l, axis=1).astype(BF16), rx)
                state_scr[g, :, j * LANES:(j + 1) * LANES] = st_p * dec + upd

    y = y_scr[...] + dskip_ref[...] * xs
    y_ssm = _gated_group_norm(y, z, ssmnw_ref[...])
    mix_scr[:, :D_SSM] = y_ssm.astype(BF16)

    u = uv[:, :D_SG]
    v = _rms(uv[:, D_SG:], sgnw_ref[...])
    vb = v.astype(BF16)
    for c in range(tl // CHUNK):
        rows = slice(c * CHUNK, (c + 1) * CHUNK)
        for hd in range(SG_HEADS):
            hl = slice(hd * SG_HEAD_DIM, (hd + 1) * SG_HEAD_DIM)
            sv = _dot(sgw_ref[hd], vb[rows, hl]) + sgb_ref[:, hl]
            mix_scr[rows, D_SSM + hd * SG_HEAD_DIM:D_SSM + (hd + 1) * SG_HEAD_DIM] = (
                u[rows, hl] * sv).astype(BF16)

    h = x + _dot(mix_scr[...], wout_ref[...])
    h_ref[...] = h
    hn2, route = _ffn_front(h, nfw_ref[...], wr_ref[...], br_ref[...])
    hn2_ref[...] = hn2
    route_ref[...] = route

    @pl.when(t == nt - 1)
    def _():
        for g in range(SSM_GROUPS):
            ssm_ref[g] = state_scr[g].T


def _prompt_mixer_kernel(
        x_ref, nmw_ref, wz_ref, wxbc_ref, wdt_ref, wuv_ref, convw_ref, convb_ref, dtb_ref,
        alog_ref, dskip_ref, ssmnw_ref, sgnw_ref, sgw_ref, sgb_ref, wout_ref, nfw_ref,
        wr_ref, br_ref, hs_ref, hn2s_ref, routes_ref,
        h_ref, hn2_ref, route_ref, convt_ref, ssm_ref,
        xbc_scr, state_scr, y_scr, mix_scr, *, tl, nt, n_main):
    i = pl.program_id(0)

    @pl.when(i >= n_main)
    def _():
        h_ref[...] = hs_ref[...]
        hn2_ref[...] = hn2s_ref[...]
        route_ref[...] = routes_ref[...]

    @pl.when(i < n_main)
    def _():
        _prompt_step(
            lax.rem(i, nt), nt, x_ref, nmw_ref, wz_ref, wxbc_ref, wdt_ref, wuv_ref, convw_ref,
            convb_ref, dtb_ref, alog_ref, dskip_ref, ssmnw_ref, sgnw_ref, sgw_ref, sgb_ref, wout_ref,
            nfw_ref, wr_ref, br_ref, h_ref, hn2_ref, route_ref, convt_ref, ssm_ref,
            xbc_scr, state_scr, y_scr, mix_scr, tl)


def _const_spec(shape):
    zeros = (0,) * len(shape)
    return pl.BlockSpec(shape, lambda *_: zeros)


def _prompt_mixer(x, w, tl, h_s, hn2_s, route_s):
    bsz, seq, _ = x.shape
    nt = seq // tl
    n_main = bsz * nt
    n_s = h_s.shape[0]
    assert seq % tl == 0 and n_s % tl == 0
    n_tok = bsz * seq + n_s
    main = lambda i: jnp.minimum(i, n_main - 1)
    tail = lambda i: jnp.maximum(i - n_main, 0)
    weights = [w['nmw'], w['wz'], w['wxbc'], w['wdt'], w['wuv'], w['convw'], w['convb'], w['dtb'],
               w['alog'], w['dskip'], w['ssmnw'], w['sgnw'], w['sgw'], w['sgb'], w['wout'], w['nfw'],
               w['wr'], w['br']]
    in_specs = ([pl.BlockSpec((tl, D_MODEL), lambda i: (main(i), 0))]
                + [_const_spec(a.shape) for a in weights]
                + [pl.BlockSpec((tl, d), lambda i: (tail(i), 0)) for d in (D_MODEL, D_MODEL, LANES)])
    q = HEADS_PER_GROUP * SSM_HEAD_DIM
    out_shape = [
        jax.ShapeDtypeStruct((n_tok, D_MODEL), F32),
        jax.ShapeDtypeStruct((n_tok, D_MODEL), F32),
        jax.ShapeDtypeStruct((n_tok, LANES), F32),
        jax.ShapeDtypeStruct((bsz, SUBLANES, C_CONV), F32),
        jax.ShapeDtypeStruct((bsz, SSM_GROUPS, q, SSM_STATE), F32),
    ]
    out_specs = [
        pl.BlockSpec((tl, D_MODEL), lambda i: (i, 0)),
        pl.BlockSpec((tl, D_MODEL), lambda i: (i, 0)),
        pl.BlockSpec((tl, LANES), lambda i: (i, 0)),
        pl.BlockSpec((None, SUBLANES, C_CONV), lambda i: (main(i) // nt, 0, 0)),
        pl.BlockSpec((None, SSM_GROUPS, q, SSM_STATE), lambda i: (main(i) // nt, 0, 0, 0)),
    ]
    scratch = [
        pltpu.VMEM((tl + SUBLANES, C_CONV), F32),
        pltpu.VMEM((SSM_GROUPS, SSM_STATE, q), F32),
        pltpu.VMEM((tl, D_SSM), F32),
        pltpu.VMEM((tl, D_SSM + D_SG), BF16),
    ]
    return pl.pallas_call(
        functools.partial(_prompt_mixer_kernel, tl=tl, nt=nt, n_main=n_main),
        grid=(n_main + n_s // tl,),
        in_specs=in_specs,
        out_specs=out_specs,
        out_shape=out_shape,
        scratch_shapes=scratch,
        compiler_params=pltpu.CompilerParams(
            dimension_semantics=("arbitrary",), vmem_limit_bytes=VMEM_LIMIT),
        name="prompt_mixer",
    )(x.reshape(bsz * seq, D_MODEL), *weights, h_s, hn2_s, route_s)


def _sample_front_kernel(
        x_ref, cprev_ref, nmw_ref, wz_ref, wxbc_ref, wdt_ref, wuv_ref, convw_ref, convb_ref,
        dtb_ref, alog_ref, dskip_ref, sgnw_ref, w4_ref, b4_ref, expand_ref,
        z_ref, ypart_ref, ea_ref, ysg_ref, v_ref, convn_ref, c_ref, b_ref, xw_ref, dec_ref,
        *, nb, nt):
    x = x_ref[...]
    hn = _rms(x, nmw_ref[...]).astype(BF16)
    z_ref[...] = _dot(hn, wz_ref[...])
    xbc = _dot(hn, wxbc_ref[...])
    dtr = _dot(hn, wdt_ref[...])
    uv = _gelu(_dot(hn, wuv_ref[...]))

    slab = lambda a, t: a[t * nb:(t + 1) * nb]
    full = [cprev_ref[k] for k in range(CONV_K - 1)] + [slab(xbc, t) for t in range(nt)]
    for k in range(CONV_K - 1):
        convn_ref[k] = full[nt + k]
    xs, bm, cm = [], [], []
    for t in range(nt):
        conv = convb_ref[...]
        for k in range(CONV_K):
            conv = conv + convw_ref[k:k + 1, :] * full[t + k]
        act = _silu(conv)
        xs.append(act[:, :D_SSM])
        bm.append(act[:, D_SSM:D_SSM + SSM_GROUPS * SSM_STATE])
        cm.append(act[:, D_SSM + SSM_GROUPS * SSM_STATE:])
        b_ref[t * nb:(t + 1) * nb, :] = bm[t]
        c_ref[t * nb:(t + 1) * nb, :] = cm[t]

    dt = _softplus(dtr + dtb_ref[...])
    a_row = -jnp.exp(alog_ref[...])
    dts = [slab(dt, t) for t in range(nt)]
    cum = []
    for t in range(nt):
        da = dts[t] * a_row
        cum.append(da if t == 0 else cum[t - 1] + da)
    dec_ref[...] = jnp.exp(cum[nt - 1])

    lane = lax.broadcasted_iota(jnp.int32, (nb, LANES), 1)
    first_group = lane < HEADS_PER_GROUP
    facs = [jnp.exp(cum[t]) for t in range(nt)]
    facs += [dts[s] * jnp.exp(cum[nt - 1] - cum[s]) for s in range(nt)]
    pairs = []
    for t in range(nt):
        for s in range(t + 1):
            cb = []
            for g in range(SSM_GROUPS):
                gl = slice(g * SSM_STATE, (g + 1) * SSM_STATE)
                cb.append(jnp.sum(cm[t][:, gl] * bm[s][:, gl], axis=1, keepdims=True))
            cbh = jnp.where(first_group, cb[0], cb[1])
            facs.append(jnp.exp(cum[t] - cum[s]) * dts[s] * cbh)
            pairs.append((t, s))
    fx = _dot_f32(jnp.concatenate(facs, axis=0), expand_ref[...])
    fslab = lambda i: fx[i * nb:(i + 1) * nb]
    for t in range(nt):
        ea_ref[t * nb:(t + 1) * nb, :] = fslab(t)
        xw_ref[t * nb:(t + 1) * nb, :] = xs[t] * fslab(nt + t)
    for t in range(nt):
        acc = dskip_ref[...] * xs[t]
        for i, (tt, s) in enumerate(pairs):
            if tt == t:
                acc = acc + fslab(2 * nt + i) * xs[s]
        ypart_ref[t * nb:(t + 1) * nb, :] = acc

    u = uv[:, :D_SG]
    v = _rms(uv[:, D_SG:], sgnw_ref[...])
    v_ref[...] = v
    for t in range(nt):
        sv = b4_ref[t:t + 1, :]
        for s in range(t + 1):
            sv = sv + w4_ref[t, s:s + 1, :] * slab(v, s)
        ysg_ref[t * nb:(t + 1) * nb, :] = slab(u, t) * sv


def _sample_front(x_slab, cprev, w, nb, nt):
    r = nb * nt
    weights = [w['nmw'], w['wz'], w['wxbc'], w['wdt'], w['wuv'], w['convw'], w['convb'], w['dtb'],
               w['alog'], w['dskip'], w['sgnw'], w['w4'], w['b4'], w['expand']]
    ins = [x_slab, cprev] + weights
    out_shape = [
        jax.ShapeDtypeStruct((r, D_SSM), F32),
        jax.ShapeDtypeStruct((r, D_SSM), F32),
        jax.ShapeDtypeStruct((r, D_SSM), F32),
        jax.ShapeDtypeStruct((r, D_SG), F32),
        jax.ShapeDtypeStruct((r, D_SG), F32),
        jax.ShapeDtypeStruct((CONV_K - 1, nb, C_CONV), F32),
        jax.ShapeDtypeStruct((r, SSM_GROUPS * SSM_STATE), F32),
        jax.ShapeDtypeStruct((r, SSM_GROUPS * SSM_STATE), F32),
        jax.ShapeDtypeStruct((r, D_SSM), F32),
        jax.ShapeDtypeStruct((nb, LANES), F32),
    ]
    return pl.pallas_call(
        functools.partial(_sample_front_kernel, nb=nb, nt=nt),
        grid=(1,),
        in_specs=[_const_spec(a.shape) for a in ins],
        out_specs=[_const_spec(s.shape) for s in out_shape],
        out_shape=out_shape,
        compiler_params=pltpu.CompilerParams(
            dimension_semantics=("arbitrary",), vmem_limit_bytes=VMEM_LIMIT),
        name="sample_front",
    )(*ins)


def _sample_state_kernel(dec_ref, st_ref, cq_ref, bq_ref, xw_ref, so_ref, z_ref, *, seqs):
    for bb in range(seqs):
        for g in range(SSM_GROUPS):
            s0 = st_ref[bb, g]
            z_ref[bb, g] = lax.dot_general(
                cq_ref[bb, g].astype(BF16), s0.astype(BF16), (((1,), (1,)), ((), ())),
                preferred_element_type=F32)
            upd = lax.dot_general(
                xw_ref[bb, g].astype(BF16), bq_ref[bb, g].astype(BF16), (((0,), (0,)), ((), ())),
                preferred_element_type=F32)
            for hh in range(HEADS_PER_GROUP):
                hs = slice(hh * SSM_HEAD_DIM, (hh + 1) * SSM_HEAD_DIM)
                so_ref[bb, g, hs, :] = s0[hs] * dec_ref[bb, g * HEADS_PER_GROUP + hh] + upd[hs]


def _sample_state(dec, state, cq, bq, xwq, seqs):
    nb = state.shape[0]
    q = HEADS_PER_GROUP * SSM_HEAD_DIM
    blk = lambda *tail: pl.BlockSpec((seqs, SSM_GROUPS) + tail, lambda i: (i, 0, 0, 0))
    return pl.pallas_call(
        functools.partial(_sample_state_kernel, seqs=seqs),
        grid=(nb // seqs,),
        in_specs=[
            pl.BlockSpec((seqs, SSM_HEADS), lambda i: (i, 0), memory_space=pltpu.SMEM),
            blk(q, SSM_STATE), blk(SUBLANES, SSM_STATE), blk(SUBLANES, SSM_STATE), blk(SUBLANES, q),
        ],
        out_specs=[blk(q, SSM_STATE), blk(SUBLANES, q)],
        out_shape=[
            jax.ShapeDtypeStruct((nb, SSM_GROUPS, q, SSM_STATE), F32),
            jax.ShapeDtypeStruct((nb, SSM_GROUPS, SUBLANES, q), F32),
        ],
        compiler_params=pltpu.CompilerParams(
            dimension_semantics=("arbitrary",), vmem_limit_bytes=VMEM_LIMIT),
        name="sample_state",
    )(dec, state, cq, bq, xwq)


def _sample_back_kernel(x_ref, ypart_ref, ea_ref, zoff_ref, z_ref, ysg_ref, ssmnw_ref, wout_ref,
                        nfw_ref, wr_ref, br_ref, h_ref, hn2_ref, route_ref):
    y = ypart_ref[...] + ea_ref[...] * zoff_ref[...]
    y_ssm = _gated_group_norm(y, z_ref[...], ssmnw_ref[...])
    mix_in = jnp.concatenate([y_ssm.astype(BF16), ysg_ref[...].astype(BF16)], axis=1)
    h = x_ref[...] + _dot(mix_in, wout_ref[...])
    h_ref[...] = h
    hn2, route = _ffn_front(h, nfw_ref[...], wr_ref[...], br_ref[...])
    hn2_ref[...] = hn2
    route_ref[...] = route


def _sample_back(x_slab, ypart, ea, zoff, z, ysg, w):
    r = x_slab.shape[0]
    ins = [x_slab, ypart, ea, zoff, z, ysg, w['ssmnw'], w['wout'], w['nfw'], w['wr'], w['br']]
    out_shape = [
        jax.ShapeDtypeStruct((r, D_MODEL), F32),
        jax.ShapeDtypeStruct((r, D_MODEL), F32),
        jax.ShapeDtypeStruct((r, LANES), F32),
    ]
    return pl.pallas_call(
        _sample_back_kernel,
        grid=(1,),
        in_specs=[_const_spec(a.shape) for a in ins],
        out_specs=[_const_spec(s.shape) for s in out_shape],
        out_shape=out_shape,
        compiler_params=pltpu.CompilerParams(
            dimension_semantics=("arbitrary",), vmem_limit_bytes=VMEM_LIMIT),
        name="sample_back",
    )(*ins)


def _expert_mlp_kernel(be_ref, nu_ref, tok_ref, tokn_ref, dst_ref, x_hbm, wg_ref, wu_ref, wd_ref,
                       parts_hbm, xbuf, obuf, wg_b, wu_b, wd_b, gsem, ssem, *, bm):
    j = pl.program_id(0)
    nu = nu_ref[0]

    def gather_start(idx_ref, s):
        for r in range(bm):
            pltpu.make_async_copy(x_hbm.at[pl.ds(idx_ref[0, r], 1)], xbuf.at[s, pl.ds(r, 1)],
                                  gsem.at[s]).start()

    def gather_wait(s):
        pltpu.make_async_copy(x_hbm.at[pl.ds(0, bm)], xbuf.at[s], gsem.at[s]).wait()

    def scatter_start(s):
        for r in range(bm):
            pltpu.make_async_copy(obuf.at[s, pl.ds(r, 1)], parts_hbm.at[pl.ds(dst_ref[0, r], 1)],
                                  ssem.at[s]).start()

    def scatter_wait(s):
        pltpu.make_async_copy(obuf.at[s], parts_hbm.at[pl.ds(0, bm)], ssem.at[s]).wait()

    def step(slot):
        gather_wait(slot)

        @pl.when(j + 1 < nu)
        def _():
            gather_start(tokn_ref, 1 - slot)

        @pl.when(j >= 2)
        def _():
            scatter_wait(slot)

        x = xbuf[slot].astype(BF16)
        hmid = _silu(_dot(x, wg_b[...])) * _dot(x, wu_b[...])
        obuf[slot] = _dot(hmid.astype(BF16), wd_b[...])
        scatter_start(slot)

        @pl.when(j == nu - 1)
        def _():
            scatter_wait(slot)

            @pl.when(j >= 1)
            def _():
                scatter_wait(1 - slot)

    @pl.when(j == 0)
    def _():
        n_real = parts_hbm.shape[0] - 2 * bm
        obuf[...] = jnp.zeros(obuf.shape, F32)
        spare = [pltpu.make_async_copy(obuf.at[s], parts_hbm.at[pl.ds(n_real + s * bm, bm)], ssem.at[s])
                 for s in range(2)]
        for cp in spare:
            cp.start()
        for cp in spare:
            cp.wait()

    used = j < nu

    @pl.when(used & (j == 0))
    def _():
        gather_start(tok_ref, 0)

    @pl.when(used & ((j == 0) | (be_ref[j] != be_ref[jnp.maximum(j - 1, 0)])))
    def _():
        wg_b[...] = wg_ref[...].astype(BF16)
        wu_b[...] = wu_ref[...].astype(BF16)
        wd_b[...] = wd_ref[...].astype(BF16)

    for slot in range(2):
        @pl.when(used & (lax.rem(j, 2) == slot))
        def _(slot=slot):
            step(slot)


def _expert_mlp(blk_expert, n_used, tok, dst, x, wg, wu, wd, bm, n_out):
    n_blocks = blk_expert.shape[0]
    w_map = lambda j, be, nu: (be[j], 0, 0)
    idx_spec = lambda f: pl.BlockSpec((None, 1, bm), lambda j, be, nu: (f(j), 0, 0),
                                      memory_space=pltpu.SMEM)
    any_spec = pl.BlockSpec(memory_space=pl.ANY)
    grid_spec = pltpu.PrefetchScalarGridSpec(
        num_scalar_prefetch=2,
        grid=(n_blocks,),
        in_specs=[
            idx_spec(lambda j: j),
            idx_spec(lambda j: jnp.minimum(j + 1, n_blocks - 1)),
            idx_spec(lambda j: j),
            any_spec,
            pl.BlockSpec((None, D_MODEL, D_FF), w_map),
            pl.BlockSpec((None, D_MODEL, D_FF), w_map),
            pl.BlockSpec((None, D_FF, D_MODEL), w_map),
        ],
        out_specs=any_spec,
        scratch_shapes=[
            pltpu.VMEM((2, bm, D_MODEL), F32),
            pltpu.VMEM((2, bm, D_MODEL), F32),
            pltpu.VMEM((D_MODEL, D_FF), BF16),
            pltpu.VMEM((D_MODEL, D_FF), BF16),
            pltpu.VMEM((D_FF, D_MODEL), BF16),
            pltpu.SemaphoreType.DMA((2,)),
            pltpu.SemaphoreType.DMA((2,)),
        ],
    )
    tok3 = tok.reshape(n_blocks, 1, bm)
    return pl.pallas_call(
        functools.partial(_expert_mlp_kernel, bm=bm),
        grid_spec=grid_spec,
        out_shape=jax.ShapeDtypeStruct((n_out, D_MODEL), F32),
        compiler_params=pltpu.CompilerParams(
            dimension_semantics=("arbitrary",), vmem_limit_bytes=VMEM_LIMIT),
        name="expert_mlp",
    )(blk_expert, n_used, tok3, tok3, dst.reshape(n_blocks, 1, bm), x, wg, wu, wd)


def _moe_plan(e_idx, bm, n_blocks):
    n_tok = e_idx.shape[0]
    n_asg = 2 * n_tok
    n_pad = n_blocks * bm - n_asg
    e_flat = e_idx.reshape(-1)
    experts = jnp.arange(N_EXPERTS, dtype=jnp.int32)
    counts = jnp.sum((e_flat[:, None] == experts[None, :]).astype(jnp.int32), axis=0)
    nblk = (counts + bm - 1) // bm
    blk_end = jnp.cumsum(nblk)
    n_used = blk_end[-1]
    j = jnp.arange(n_blocks, dtype=jnp.int32)
    be = jnp.sum((blk_end[None, :] <= jnp.minimum(j, n_used - 1)[:, None]).astype(jnp.int32), axis=1)
    be = jnp.minimum(be, N_EXPERTS - 1)
    pad_end = jnp.cumsum(nblk * bm - counts)
    pad_expert = jnp.sum((pad_end[None, :] <= jnp.arange(n_pad, dtype=jnp.int32)[:, None]).astype(jnp.int32),
                         axis=1)
    keys = jnp.concatenate([2 * e_flat, 2 * pad_expert + 1])
    ids = jnp.concatenate([jnp.arange(n_asg, dtype=jnp.int32), jnp.full((n_pad,), -1, jnp.int32)])
    _, window = lax.sort((keys, ids), num_keys=1, is_stable=True)
    window = window.reshape(n_blocks, bm)
    valid = window >= 0
    q = jnp.arange(bm, dtype=jnp.int32)
    tok = jnp.where(valid, window // 2, (j[:, None] * bm + q[None, :]) % n_tok)
    dst = jnp.where(valid, (window % 2) * n_tok + window // 2,
                    2 * n_tok + (j[:, None] % 2) * bm + q[None, :])
    return tok.astype(jnp.int32), dst.astype(jnp.int32), be, n_used.astype(jnp.int32).reshape(1)


def _final_kernel(h_ref, p0_ref, p1_ref, route_ref, w_ref, op_ref, os_ref, *, n_prompt_steps):
    route = route_ref[...]
    y = _rms(h_ref[...] + route[:, 2:3] * p0_ref[...] + route[:, 3:4] * p1_ref[...], w_ref[...])
    i = pl.program_id(0)

    @pl.when(i < n_prompt_steps)
    def _():
        op_ref[...] = y

    @pl.when(i >= n_prompt_steps)
    def _():
        os_ref[...] = y


def _final(h, parts, route, w, tf, n_prompt):
    n_tok = h.shape[0]
    assert n_tok % tf == 0 and n_prompt % tf == 0
    n_p = n_prompt // tf
    row = lambda d: pl.BlockSpec((tf, d), lambda i: (i, 0))
    part = lambda k: pl.BlockSpec((tf, D_MODEL), lambda i: (k * (n_tok // tf) + i, 0))
    return pl.pallas_call(
        functools.partial(_final_kernel, n_prompt_steps=n_p),
        grid=(n_tok // tf,),
        in_specs=[row(D_MODEL), part(0), part(1), row(LANES), _const_spec(w.shape)],
        out_specs=[pl.BlockSpec((tf, D_MODEL), lambda i: (jnp.minimum(i, n_p - 1), 0)),
                   pl.BlockSpec((tf, D_MODEL), lambda i: (jnp.maximum(i - n_p, 0), 0))],
        out_shape=[jax.ShapeDtypeStruct((n_prompt, D_MODEL), F32),
                   jax.ShapeDtypeStruct((n_tok - n_prompt, D_MODEL), F32)],
        compiler_params=pltpu.CompilerParams(
            dimension_semantics=("arbitrary",), vmem_limit_bytes=VMEM_LIMIT),
        name="final_combine",
    )(h, parts, parts, route, w)


PROMPT_TILE = 256
MOE_BLOCK_ROWS = 256
FINAL_TILE = 512
STATE_SEQS = 8


def _pad_lanes(a, width=LANES):
    return jnp.pad(a, [(0, 0)] * (a.ndim - 1) + [(0, width - a.shape[-1])])


def _layer_weights(l, norm_mix_w, w_in, conv_w, conv_b, dt_bias, a_log, d_skip, ssm_norm_w, sg_norm_w,
                   sg_w, sg_b, w_out, norm_ffn_w, w_router_group, b_router_group, w_router_expert,
                   b_router_expert, n_dec):
    c0, c1, c2 = D_SSM, D_SSM + C_CONV, D_SSM + C_CONV + SSM_HEADS
    wi = w_in[l]
    causal = jnp.tril(jnp.ones((CHUNK, CHUNK), bool))
    sgw_l = jnp.where(causal, sg_w[l], 0.0)
    head_of_lane = jnp.arange(D_SSM, dtype=jnp.int32) // SSM_HEAD_DIM
    expand = (jnp.arange(LANES, dtype=jnp.int32)[:, None] == head_of_lane[None, :]).astype(F32)
    row = lambda a: a.reshape(1, -1).astype(F32)
    return {
        'nmw': row(norm_mix_w[l]),
        'wz': wi[:, :c0].astype(BF16),
        'wxbc': wi[:, c0:c1].astype(BF16),
        'wdt': _pad_lanes(wi[:, c1:c2]).astype(BF16),
        'wuv': wi[:, c2:].astype(BF16),
        'convw': conv_w[l].astype(F32),
        'convb': row(conv_b[l]),
        'dtb': _pad_lanes(row(dt_bias[l])),
        'alog': _pad_lanes(row(a_log[l])),
        'dskip': row(jnp.repeat(d_skip[l], SSM_HEAD_DIM)),
        'ssmnw': row(ssm_norm_w[l]),
        'sgnw': row(sg_norm_w[l]),
        'sgw': sgw_l.astype(BF16),
        'sgb': jnp.repeat(sg_b[l].T, SG_HEAD_DIM, axis=1).astype(F32),
        'w4': jnp.repeat(jnp.transpose(sgw_l[:, :n_dec, :n_dec], (1, 2, 0)), SG_HEAD_DIM, axis=2).astype(F32),
        'b4': jnp.repeat(sg_b[l][:, :n_dec].T, SG_HEAD_DIM, axis=1).astype(F32),
        'expand': expand,
        'wout': w_out[l].astype(BF16),
        'nfw': row(norm_ffn_w[l]),
        'wr': _pad_lanes(jnp.concatenate([w_router_group[l], w_router_expert[l]], axis=1)).astype(F32),
        'br': _pad_lanes(row(jnp.concatenate([b_router_group[l], b_router_expert[l]]))),
    }


def _layer(l, xp, xs_slab, state_conv, state_ssm, w, w_gate, w_up, w_down):
    bp, seq, _ = xp.shape
    nb, nt = state_conv.shape[1], xs_slab.shape[0] // state_conv.shape[1]
    q = HEADS_PER_GROUP * SSM_HEAD_DIM

    cprev = jnp.transpose(state_conv[l], (1, 0, 2))
    z, ypart, ea, ysg, v, convn, cmat, bmat, xw, dec = _sample_front(xs_slab, cprev, w, nb, nt)
    to_seq = lambda a, d: jnp.pad(
        jnp.transpose(a.reshape(nt, nb, SSM_GROUPS, d), (1, 2, 0, 3)),
        ((0, 0), (0, 0), (0, SUBLANES - nt), (0, 0)))
    ssm_s, zoff = _sample_state(
        dec[:, :SSM_HEADS], state_ssm[l].reshape(nb, SSM_GROUPS, q, SSM_STATE),
        to_seq(cmat, SSM_STATE), to_seq(bmat, SSM_STATE), to_seq(xw, q), STATE_SEQS)
    zoff = jnp.transpose(zoff[:, :, :nt], (2, 0, 1, 3)).reshape(nt * nb, D_SSM)
    h_s, hn2_s, route_s = _sample_back(xs_slab, ypart, ea, zoff, z, ysg, w)

    h, hn2, route, convt_p, ssm_p = _prompt_mixer(xp, w, PROMPT_TILE, h_s, hn2_s, route_s)

    n_tok = h.shape[0]
    bm = MOE_BLOCK_ROWS
    n_blocks = -(-(2 * n_tok) // bm) + N_EXPERTS
    tok, dst, blk_expert, n_used = _moe_plan(route[:, :2].astype(jnp.int32), bm, n_blocks)
    parts = _expert_mlp(blk_expert, n_used, tok, dst, hn2, w_gate[l], w_up[l], w_down[l], bm,
                        2 * n_tok + 2 * bm)
    outs = dict(
        h=h, parts=parts, route=route,
        conv_p=convt_p[:, SUBLANES - (CONV_K - 1):],
        ssm_p=ssm_p.reshape(bp, SSM_GROUPS, HEADS_PER_GROUP, SSM_HEAD_DIM, SSM_STATE),
        conv_s=jnp.transpose(convn, (1, 0, 2)),
        ssm_s=ssm_s.reshape(nb, SSM_GROUPS, HEADS_PER_GROUP, SSM_HEAD_DIM, SSM_STATE),
        v_s=jnp.transpose(v.reshape(nt, nb, D_SG), (1, 0, 2)),
    )
    return outs


def kernel(x_prompt, x_sample, state_conv, state_ssm, norm_mix_w, w_in, conv_w, conv_b, dt_bias, a_log, d_skip, ssm_norm_w, sg_norm_w, sg_w, sg_b, w_out, norm_ffn_w, w_router_group, b_router_group, w_router_expert, b_router_expert, w_gate, w_up, w_down, norm_final_w):
    depth = w_in.shape[0]
    assert depth == 1, "the fused final norm assumes a single layer"
    bp, seq, _ = x_prompt.shape
    nb, nt, _ = x_sample.shape
    l = 0
    w = _layer_weights(l, norm_mix_w, w_in, conv_w, conv_b, dt_bias, a_log, d_skip, ssm_norm_w, sg_norm_w,
                       sg_w, sg_b, w_out, norm_ffn_w, w_router_group, b_router_group, w_router_expert,
                       b_router_expert, nt)
    xs_slab = jnp.transpose(x_sample, (1, 0, 2)).reshape(nt * nb, D_MODEL)
    outs = _layer(l, x_prompt, xs_slab, state_conv, state_ssm, w, w_gate, w_up, w_down)
    y_p, y_s = _final(outs['h'], outs['parts'], outs['route'], norm_final_w.reshape(1, -1).astype(F32),
                      FINAL_TILE, bp * seq)
    y_prompt = y_p.reshape(bp, seq, D_MODEL)
    y_sample = jnp.transpose(y_s.reshape(nt, nb, D_MODEL), (1, 0, 2))
    return (y_prompt, y_sample, outs['conv_p'][None], outs['ssm_p'][None], outs['conv_s'][None],
            outs['ssm_s'][None], outs['v_s'][None])
```

```python
import functools

import jax
import jax.numpy as jnp
from jax import lax
from jax.experimental import pallas as pl
from jax.experimental.pallas import tpu as pltpu

D_MODEL = 1024
D_SSM = 1024
SSM_HEAD_DIM = 64
SSM_HEADS = 16
SSM_GROUPS = 2
HEADS_PER_GROUP = 8
SSM_STATE = 128
CONV_K = 4
C_CONV = D_SSM + 2 * SSM_GROUPS * SSM_STATE
D_SG = 1024
SG_HEADS = 8
SG_HEAD_DIM = 128
CHUNK = 128
N_EXPERT_GROUPS = 4
EXPERTS_PER_GROUP = 8
N_EXPERTS = 32
D_FF = 512
EPS = 1e-6

LANES = 128
SUBLANES = 8
VMEM_LIMIT = 56 * 1024 * 1024

F32 = jnp.float32
BF16 = jnp.bfloat16
HIGHEST = lax.Precision.HIGHEST


def _dot(a, b):
    return jnp.dot(a, b, preferred_element_type=F32)


def _dot_f32(a, b):
    return jnp.dot(a, b, preferred_element_type=F32, precision=HIGHEST)


def _rms(x, w):
    return x * lax.rsqrt(jnp.mean(x * x, axis=-1, keepdims=True) + EPS) * w


def _silu(x):
    return x * (1.0 / (1.0 + jnp.exp(-x)))


def _gelu(x):
    return 0.5 * x * (1.0 + lax.erf(x * 0.7071067811865476))


def _softplus(x):
    return jnp.maximum(x, 0.0) + jnp.log1p(jnp.exp(-jnp.abs(x)))


def _gated_group_norm(y, z, w):
    g = y * _silu(z)
    half = D_SSM // SSM_GROUPS
    parts = []
    for k in range(SSM_GROUPS):
        gk = g[:, k * half:(k + 1) * half]
        parts.append(gk * lax.rsqrt(jnp.mean(gk * gk, axis=-1, keepdims=True) + EPS))
    return jnp.concatenate(parts, axis=1) * w


ROUTER_GROUP_ROW = 0
ROUTER_EXPERT_ROW = SUBLANES


def _route_t(lt):
    r = lt.shape[1]
    row = lax.broadcasted_iota(jnp.int32, (SUBLANES, r), 0)
    rowf = row.astype(F32)
    big = float(SUBLANES)
    red = lambda f, a: f(a, axis=0, keepdims=True)
    gl = jnp.where(row < N_EXPERT_GROUPS, lt[ROUTER_GROUP_ROW:ROUTER_GROUP_ROW + SUBLANES], -jnp.inf)
    ge = jnp.exp(gl - red(jnp.max, gl))
    p_grp = ge / red(jnp.sum, ge)
    g_p = red(jnp.max, p_grp)
    g_idx = red(jnp.min, jnp.where(p_grp == g_p, rowf, big))
    el = lt[ROUTER_EXPERT_ROW:ROUTER_EXPERT_ROW + EXPERTS_PER_GROUP]
    for g in range(1, N_EXPERT_GROUPS):
        lo = ROUTER_EXPERT_ROW + g * EXPERTS_PER_GROUP
        el = jnp.where(g_idx == float(g), lt[lo:lo + EXPERTS_PER_GROUP], el)
    ee = jnp.exp(el - red(jnp.max, el))
    pe = ee / red(jnp.sum, ee)
    v1 = red(jnp.max, pe)
    i1 = red(jnp.min, jnp.where(pe == v1, rowf, big))
    rest = rowf != i1
    pe2 = jnp.where(rest, pe, -1.0)
    v2 = red(jnp.max, pe2)
    i2 = red(jnp.min, jnp.where(rest & (pe2 == v2), rowf, big))
    den = v1 + v2
    base = g_idx * float(EXPERTS_PER_GROUP)
    out = jnp.where(row == 0, base + i1, 0.0)
    out = jnp.where(row == 1, base + i2, out)
    out = jnp.where(row == 2, g_p * v1 / den, out)
    out = jnp.where(row == 3, g_p * v2 / den, out)
    return jnp.concatenate([out, jnp.zeros((LANES - SUBLANES, r), F32)], axis=0).T


def _ffn_front(h, nfw, wrt_ref, brc):
    hn2 = _rms(h, nfw)
    hi = hn2.astype(BF16)
    lo = (hn2 - hi.astype(F32)).astype(BF16)
    nt = lambda a, b: lax.dot_general(a, b, (((1,), (1,)), ((), ())), preferred_element_type=F32)
    lt = nt(wrt_ref[0], hi) + nt(wrt_ref[0], lo) + nt(wrt_ref[1], hi) + brc
    return hn2, _route_t(lt)


ROW_TILES = D_MODEL // LANES


def _store_row_tiles(ref, val):
    r = val.shape[0]
    for k in range(ROW_TILES):
        ref[pl.ds(k, r, stride=ROW_TILES), :] = val[:, k * LANES:(k + 1) * LANES]


def _load_row_tiles(ref):
    r = ref.shape[0] // ROW_TILES
    return jnp.concatenate([ref[pl.ds(k, r, stride=ROW_TILES), :] for k in range(ROW_TILES)], axis=1)


def _prompt_step(
        t, nt, x_ref, nmw_ref, wz_ref, wxbc_ref, wdt_ref, wuv_ref, convw_ref, convb_ref, dtb_ref,
        alog_ref, dskip_ref, ssmnw_ref, sgnw_ref, sgw_ref, sgb_ref, wout_ref, nfw_ref,
        wr_ref, br_ref, h_ref, hn2_ref, route_ref, convt_ref, ssm_ref,
        xbc_scr, state_scr, y_scr, mix_scr, tl):
    @pl.when(t == 0)
    def _():
        xbc_scr[0:SUBLANES, :] = jnp.zeros((SUBLANES, C_CONV), F32)
        state_scr[...] = jnp.zeros(state_scr.shape, F32)

    x = x_ref[...]
    hn = _rms(x, nmw_ref[...]).astype(BF16)
    z = _dot(hn, wz_ref[...])
    xbc = _dot(hn, wxbc_ref[...])
    dtr = _dot(hn, wdt_ref[...])
    uv = _gelu(_dot(hn, wuv_ref[...]))

    xbc_scr[SUBLANES:SUBLANES + tl, :] = xbc
    conv = convb_ref[...]
    for k in range(CONV_K):
        off = SUBLANES - (CONV_K - 1) + k
        conv = conv + convw_ref[k:k + 1, :] * xbc_scr[off:off + tl, :]
    tail = xbc_scr[tl:tl + SUBLANES, :]
    xbc_scr[0:SUBLANES, :] = tail
    convt_ref[...] = tail
    act = _silu(conv)
    xs = act[:, :D_SSM]
    bmat = act[:, D_SSM:D_SSM + SSM_GROUPS * SSM_STATE]
    cmat = act[:, D_SSM + SSM_GROUPS * SSM_STATE:]

    dt = _softplus(dtr + dtb_ref[...])
    a_row = -jnp.exp(alog_ref[...])

    li = lax.broadcasted_iota(jnp.int32, (CHUNK, CHUNK), 0)
    si = lax.broadcasted_iota(jnp.int32, (CHUNK, CHUNK), 1)
    causal = li >= si
    tri = jnp.where(causal, 1.0, 0.0).astype(F32)
    lo_half = si < SSM_HEAD_DIM

    for c in range(tl // CHUNK):
        rows = slice(c * CHUNK, (c + 1) * CHUNK)
        dt_c = dt[rows]
        acum = _dot_f32(tri, dt_c * a_row)
        acum_t = acum.T
        dt_t = dt_c.T
        w_t = dt_t * jnp.exp(acum_t[:, CHUNK - 1:CHUNK] - acum_t)
        xs_c = xs[rows]
        for g in range(SSM_GROUPS):
            gl = slice(g * SSM_STATE, (g + 1) * SSM_STATE)
            b_g = bmat[rows, gl]
            c_g = cmat[rows, gl]
            b_gt = b_g.T
            cb = _dot(c_g.astype(BF16), b_gt.astype(BF16))
            for j in range(HEADS_PER_GROUP // 2):
                pl_ = slice(g * 512 + j * LANES, g * 512 + (j + 1) * LANES)
                m_l, ec_l, s_l, ea_last = [], [], [], []
                for k in range(2):
                    hh = g * HEADS_PER_GROUP + 2 * j + k
                    colb = jnp.broadcast_to(acum[:, hh:hh + 1], (CHUNK, CHUNK))
                    rowb = jnp.broadcast_to(acum_t[hh:hh + 1, :], (CHUNK, CHUNK))
                    ea = jnp.exp(colb)
                    seg = jnp.where(causal, colb - rowb, 0.0)
                    lmat = jnp.where(causal, jnp.exp(seg), 0.0)
                    m_l.append(cb * lmat * jnp.broadcast_to(dt_t[hh:hh + 1, :], (CHUNK, CHUNK)))
                    ec_l.append(c_g * ea)
                    s_l.append(b_gt * jnp.broadcast_to(w_t[hh:hh + 1, :], (CHUNK, CHUNK)))
                    ea_last.append(ea[CHUNK - 1:CHUNK, :])
                xs_p = xs_c[:, pl_]
                st_p = state_scr[g, :, j * LANES:(j + 1) * LANES]
                rx = jnp.concatenate([jnp.where(lo_half, xs_p, 0.0),
                                      jnp.where(lo_half, 0.0, xs_p)], axis=0).astype(BF16)
                rs = jnp.concatenate([jnp.where(lo_half, st_p, 0.0),
                                      jnp.where(lo_half, 0.0, st_p)], axis=0).astype(BF16)
                lhs = jnp.concatenate(m_l + ec_l, axis=1).astype(BF16)
                y_p = _dot(lhs, jnp.concatenate([rx, rs], axis=0))
                y_scr[rows, pl_] = y_p
                dec = jnp.where(lo_half[0:1, :], ea_last[0], ea_last[1])
                upd = _dot(jnp.concatenate(s_l, axis=1).astype(BF16), rx)
                state_scr[g, :, j * LANES:(j + 1) * LANES] = st_p * dec + upd

    y = y_scr[...] + dskip_ref[...] * xs
    y_ssm = _gated_group_norm(y, z, ssmnw_ref[...])
    mix_scr[:, :D_SSM] = y_ssm.astype(BF16)

    u = uv[:, :D_SG]
    v = _rms(uv[:, D_SG:], sgnw_ref[...])
    vb = v.astype(BF16)
    for c in range(tl // CHUNK):
        rows = slice(c * CHUNK, (c + 1) * CHUNK)
        for hd in range(SG_HEADS):
            hl = slice(hd * SG_HEAD_DIM, (hd + 1) * SG_HEAD_DIM)
            sv = _dot(sgw_ref[hd], vb[rows, hl]) + sgb_ref[:, hl]
            mix_scr[rows, D_SSM + hd * SG_HEAD_DIM:D_SSM + (hd + 1) * SG_HEAD_DIM] = (
                u[rows, hl] * sv).astype(BF16)

    h = x + _dot(mix_scr[...], wout_ref[...])
    h_ref[...] = h
    hn2, route = _ffn_front(h, nfw_ref[...], wr_ref, br_ref[...])
    _store_row_tiles(hn2_ref, hn2)
    route_ref[...] = route

    @pl.when(t == nt - 1)
    def _():
        for g in range(SSM_GROUPS):
            ssm_ref[g] = state_scr[g].T


def _prompt_mixer_kernel(
        x_ref, nmw_ref, wz_ref, wxbc_ref, wdt_ref, wuv_ref, convw_ref, convb_ref, dtb_ref,
        alog_ref, dskip_ref, ssmnw_ref, sgnw_ref, sgw_ref, sgb_ref, wout_ref, nfw_ref,
        wr_ref, br_ref, hs_ref, hn2s_ref, routes_ref,
        h_ref, hn2_ref, route_ref, convt_ref, ssm_ref,
        xbc_scr, state_scr, y_scr, mix_scr, *, tl, nt, n_main):
    i = pl.program_id(0)

    @pl.when(i >= n_main)
    def _():
        h_ref[...] = hs_ref[...]
        hn2_ref[...] = hn2s_ref[...]
        route_ref[...] = routes_ref[...]

    @pl.when(i < n_main)
    def _():
        _prompt_step(
            lax.rem(i, nt), nt, x_ref, nmw_ref, wz_ref, wxbc_ref, wdt_ref, wuv_ref, convw_ref,
            convb_ref, dtb_ref, alog_ref, dskip_ref, ssmnw_ref, sgnw_ref, sgw_ref, sgb_ref, wout_ref,
            nfw_ref, wr_ref, br_ref, h_ref, hn2_ref, route_ref, convt_ref, ssm_ref,
            xbc_scr, state_scr, y_scr, mix_scr, tl)


def _const_spec(shape):
    zeros = (0,) * len(shape)
    return pl.BlockSpec(shape, lambda *_: zeros)


def _prompt_mixer(x, w, tl, h_s, hn2_s, route_s):
    bsz, seq, _ = x.shape
    nt = seq // tl
    n_main = bsz * nt
    n_s = h_s.shape[0]
    assert seq % tl == 0 and n_s % tl == 0
    n_tok = bsz * seq + n_s
    main = lambda i: jnp.minimum(i, n_main - 1)
    tail = lambda i: jnp.maximum(i - n_main, 0)
    weights = [w['nmw'], w['wz'], w['wxbc'], w['wdt'], w['wuv'], w['convw'], w['convb'], w['dtb'],
               w['alog'], w['dskip'], w['ssmnw'], w['sgnw'], w['sgw'], w['sgb'], w['wout'], w['nfw'],
               w['wr'], w['br']]
    in_specs = ([pl.BlockSpec((tl, D_MODEL), lambda i: (main(i), 0))]
                + [_const_spec(a.shape) for a in weights]
                + [pl.BlockSpec(blk, lambda i: (tail(i), 0))
                   for blk in ((tl, D_MODEL), (tl * ROW_TILES, LANES), (tl, LANES))])
    q = HEADS_PER_GROUP * SSM_HEAD_DIM
    out_shape = [
        jax.ShapeDtypeStruct((n_tok, D_MODEL), F32),
        jax.ShapeDtypeStruct((n_tok * ROW_TILES, LANES), F32),
        jax.ShapeDtypeStruct((n_tok, LANES), F32),
        jax.ShapeDtypeStruct((bsz, SUBLANES, C_CONV), F32),
        jax.ShapeDtypeStruct((bsz, SSM_GROUPS, q, SSM_STATE), F32),
    ]
    out_specs = [
        pl.BlockSpec((tl, D_MODEL), lambda i: (i, 0)),
        pl.BlockSpec((tl * ROW_TILES, LANES), lambda i: (i, 0)),
        pl.BlockSpec((tl, LANES), lambda i: (i, 0)),
        pl.BlockSpec((None, SUBLANES, C_CONV), lambda i: (main(i) // nt, 0, 0)),
        pl.BlockSpec((None, SSM_GROUPS, q, SSM_STATE), lambda i: (main(i) // nt, 0, 0, 0)),
    ]
    scratch = [
        pltpu.VMEM((tl + SUBLANES, C_CONV), F32),
        pltpu.VMEM((SSM_GROUPS, SSM_STATE, q), F32),
        pltpu.VMEM((tl, D_SSM), F32),
        pltpu.VMEM((tl, D_SSM + D_SG), BF16),
    ]
    return pl.pallas_call(
        functools.partial(_prompt_mixer_kernel, tl=tl, nt=nt, n_main=n_main),
        grid=(n_main + n_s // tl,),
        in_specs=in_specs,
        out_specs=out_specs,
        out_shape=out_shape,
        scratch_shapes=scratch,
        compiler_params=pltpu.CompilerParams(
            dimension_semantics=("arbitrary",), vmem_limit_bytes=VMEM_LIMIT),
        name="prompt_mixer",
    )(x.reshape(bsz * seq, D_MODEL), *weights, h_s, hn2_s, route_s)


def _sample_front_kernel(
        x_ref, cprev_ref, nmw_ref, wz_ref, wxbc_ref, wdt_ref, wuv_ref, convw_ref, convb_ref,
        dtb_ref, alog_ref, dskip_ref, sgnw_ref, w4_ref, b4_ref, expand_ref,
        z_ref, ypart_ref, ea_ref, ysg_ref, v_ref, convn_ref, c_ref, b_ref, xw_ref, dec_ref,
        *, nb, nt):
    x = x_ref[...]
    hn = _rms(x, nmw_ref[...]).astype(BF16)
    z_ref[...] = _dot(hn, wz_ref[...])
    xbc = _dot(hn, wxbc_ref[...])
    dtr = _dot(hn, wdt_ref[...])
    uv = _gelu(_dot(hn, wuv_ref[...]))

    slab = lambda a, t: a[t * nb:(t + 1) * nb]
    full = [cprev_ref[k] for k in range(CONV_K - 1)] + [slab(xbc, t) for t in range(nt)]
    for k in range(CONV_K - 1):
        convn_ref[k] = full[nt + k]
    xs, bm, cm = [], [], []
    for t in range(nt):
        conv = convb_ref[...]
        for k in range(CONV_K):
            conv = conv + convw_ref[k:k + 1, :] * full[t + k]
        act = _silu(conv)
        xs.append(act[:, :D_SSM])
        bm.append(act[:, D_SSM:D_SSM + SSM_GROUPS * SSM_STATE])
        cm.append(act[:, D_SSM + SSM_GROUPS * SSM_STATE:])
        b_ref[t * nb:(t + 1) * nb, :] = bm[t]
        c_ref[t * nb:(t + 1) * nb, :] = cm[t]

    dt = _softplus(dtr + dtb_ref[...])
    a_row = -jnp.exp(alog_ref[...])
    dts = [slab(dt, t) for t in range(nt)]
    cum = []
    for t in range(nt):
        da = dts[t] * a_row
        cum.append(da if t == 0 else cum[t - 1] + da)
    dec_ref[...] = jnp.exp(cum[nt - 1])

    lane = lax.broadcasted_iota(jnp.int32, (nb, LANES), 1)
    first_group = lane < HEADS_PER_GROUP
    facs = [jnp.exp(cum[t]) for t in range(nt)]
    facs += [dts[s] * jnp.exp(cum[nt - 1] - cum[s]) for s in range(nt)]
    pairs = []
    for t in range(nt):
        for s in range(t + 1):
            cb = []
            for g in range(SSM_GROUPS):
                gl = slice(g * SSM_STATE, (g + 1) * SSM_STATE)
                cb.append(jnp.sum(cm[t][:, gl] * bm[s][:, gl], axis=1, keepdims=True))
            cbh = jnp.where(first_group, cb[0], cb[1])
            facs.append(jnp.exp(cum[t] - cum[s]) * dts[s] * cbh)
            pairs.append((t, s))
    fx = _dot_f32(jnp.concatenate(facs, axis=0), expand_ref[...])
    fslab = lambda i: fx[i * nb:(i + 1) * nb]
    for t in range(nt):
        ea_ref[t * nb:(t + 1) * nb, :] = fslab(t)
        xw_ref[t * nb:(t + 1) * nb, :] = xs[t] * fslab(nt + t)
    for t in range(nt):
        acc = dskip_ref[...] * xs[t]
        for i, (tt, s) in enumerate(pairs):
            if tt == t:
                acc = acc + fslab(2 * nt + i) * xs[s]
        ypart_ref[t * nb:(t + 1) * nb, :] = acc

    u = uv[:, :D_SG]
    v = _rms(uv[:, D_SG:], sgnw_ref[...])
    v_ref[...] = v
    for t in range(nt):
        sv = b4_ref[t:t + 1, :]
        for s in range(t + 1):
            sv = sv + w4_ref[t, s:s + 1, :] * slab(v, s)
        ysg_ref[t * nb:(t + 1) * nb, :] = slab(u, t) * sv


def _sample_front(x_slab, cprev, w, nb, nt):
    r = nb * nt
    weights = [w['nmw'], w['wz'], w['wxbc'], w['wdt'], w['wuv'], w['convw'], w['convb'], w['dtb'],
               w['alog'], w['dskip'], w['sgnw'], w['w4'], w['b4'], w['expand']]
    ins = [x_slab, cprev] + weights
    out_shape = [
        jax.ShapeDtypeStruct((r, D_SSM), F32),
        jax.ShapeDtypeStruct((r, D_SSM), F32),
        jax.ShapeDtypeStruct((r, D_SSM), F32),
        jax.ShapeDtypeStruct((r, D_SG), F32),
        jax.ShapeDtypeStruct((r, D_SG), F32),
        jax.ShapeDtypeStruct((CONV_K - 1, nb, C_CONV), F32),
        jax.ShapeDtypeStruct((r, SSM_GROUPS * SSM_STATE), F32),
        jax.ShapeDtypeStruct((r, SSM_GROUPS * SSM_STATE), F32),
        jax.ShapeDtypeStruct((r, D_SSM), F32),
        jax.ShapeDtypeStruct((nb, LANES), F32),
    ]
    return pl.pallas_call(
        functools.partial(_sample_front_kernel, nb=nb, nt=nt),
        grid=(1,),
        in_specs=[_const_spec(a.shape) for a in ins],
        out_specs=[_const_spec(s.shape) for s in out_shape],
        out_shape=out_shape,
        compiler_params=pltpu.CompilerParams(
            dimension_semantics=("arbitrary",), vmem_limit_bytes=VMEM_LIMIT),
        name="sample_front",
    )(*ins)


def _sample_state_kernel(dec_ref, st_ref, cq_ref, bq_ref, xw_ref, so_ref, z_ref, *, seqs):
    for bb in range(seqs):
        for g in range(SSM_GROUPS):
            s0 = st_ref[bb, g]
            z_ref[bb, g] = lax.dot_general(
                cq_ref[bb, g].astype(BF16), s0.astype(BF16), (((1,), (1,)), ((), ())),
                preferred_element_type=F32)
            upd = lax.dot_general(
                xw_ref[bb, g].astype(BF16), bq_ref[bb, g].astype(BF16), (((0,), (0,)), ((), ())),
                preferred_element_type=F32)
            for hh in range(HEADS_PER_GROUP):
                hs = slice(hh * SSM_HEAD_DIM, (hh + 1) * SSM_HEAD_DIM)
                so_ref[bb, g, hs, :] = s0[hs] * dec_ref[bb, g * HEADS_PER_GROUP + hh] + upd[hs]


def _sample_state(dec, state, cq, bq, xwq, seqs):
    nb = state.shape[0]
    q = HEADS_PER_GROUP * SSM_HEAD_DIM
    blk = lambda *tail: pl.BlockSpec((seqs, SSM_GROUPS) + tail, lambda i: (i, 0, 0, 0))
    return pl.pallas_call(
        functools.partial(_sample_state_kernel, seqs=seqs),
        grid=(nb // seqs,),
        in_specs=[
            pl.BlockSpec((seqs, SSM_HEADS), lambda i: (i, 0), memory_space=pltpu.SMEM),
            blk(q, SSM_STATE), blk(SUBLANES, SSM_STATE), blk(SUBLANES, SSM_STATE), blk(SUBLANES, q),
        ],
        out_specs=[blk(q, SSM_STATE), blk(SUBLANES, q)],
        out_shape=[
            jax.ShapeDtypeStruct((nb, SSM_GROUPS, q, SSM_STATE), F32),
            jax.ShapeDtypeStruct((nb, SSM_GROUPS, SUBLANES, q), F32),
        ],
        compiler_params=pltpu.CompilerParams(
            dimension_semantics=("arbitrary",), vmem_limit_bytes=VMEM_LIMIT),
        name="sample_state",
    )(dec, state, cq, bq, xwq)


def _sample_back_kernel(x_ref, ypart_ref, ea_ref, zoff_ref, z_ref, ysg_ref, ssmnw_ref, wout_ref,
                        nfw_ref, wr_ref, br_ref, h_ref, hn2_ref, route_ref):
    y = ypart_ref[...] + ea_ref[...] * zoff_ref[...]
    y_ssm = _gated_group_norm(y, z_ref[...], ssmnw_ref[...])
    mix_in = jnp.concatenate([y_ssm.astype(BF16), ysg_ref[...].astype(BF16)], axis=1)
    h = x_ref[...] + _dot(mix_in, wout_ref[...])
    h_ref[...] = h
    hn2, route = _ffn_front(h, nfw_ref[...], wr_ref, br_ref[...])
    _store_row_tiles(hn2_ref, hn2)
    route_ref[...] = route


def _sample_back(x_slab, ypart, ea, zoff, z, ysg, w):
    r = x_slab.shape[0]
    ins = [x_slab, ypart, ea, zoff, z, ysg, w['ssmnw'], w['wout'], w['nfw'], w['wr'], w['br']]
    out_shape = [
        jax.ShapeDtypeStruct((r, D_MODEL), F32),
        jax.ShapeDtypeStruct((r * ROW_TILES, LANES), F32),
        jax.ShapeDtypeStruct((r, LANES), F32),
    ]
    return pl.pallas_call(
        _sample_back_kernel,
        grid=(1,),
        in_specs=[_const_spec(a.shape) for a in ins],
        out_specs=[_const_spec(s.shape) for s in out_shape],
        out_shape=out_shape,
        compiler_params=pltpu.CompilerParams(
            dimension_semantics=("arbitrary",), vmem_limit_bytes=VMEM_LIMIT),
        name="sample_back",
    )(*ins)


def _expert_mlp_kernel(be_ref, nu_ref, tok_ref, tokn_ref, dst_ref, x_hbm, wg_ref, wu_ref, wd_ref,
                       parts_hbm, xbuf, obuf, wg_b, wu_b, wd_b, gsem, ssem, *, bm):
    j = pl.program_id(0)
    nu = nu_ref[0]

    nt = ROW_TILES
    tile = lambda ref, start: ref.at[pl.ds(pl.multiple_of(start, nt), nt)]

    def gather_start(idx_ref, s):
        for r in range(bm):
            pltpu.make_async_copy(tile(x_hbm, idx_ref[0, r]), xbuf.at[s, pl.ds(r * nt, nt)],
                                  gsem.at[s]).start()

    def gather_wait(s):
        pltpu.make_async_copy(x_hbm.at[pl.ds(0, bm * nt)], xbuf.at[s], gsem.at[s]).wait()

    def scatter_start(s):
        for r in range(bm):
            pltpu.make_async_copy(obuf.at[s, pl.ds(r * nt, nt)], tile(parts_hbm, dst_ref[0, r]),
                                  ssem.at[s]).start()

    def scatter_wait(s):
        pltpu.make_async_copy(obuf.at[s], parts_hbm.at[pl.ds(0, bm * nt)], ssem.at[s]).wait()

    def step(slot):
        gather_wait(slot)

        @pl.when(j + 1 < nu)
        def _():
            gather_start(tokn_ref, 1 - slot)

        @pl.when(j >= 2)
        def _():
            scatter_wait(slot)

        x = _load_row_tiles(xbuf.at[slot]).astype(BF16)
        hmid = _silu(_dot(x, wg_b[...])) * _dot(x, wu_b[...])
        _store_row_tiles(obuf.at[slot], _dot(hmid.astype(BF16), wd_b[...]))
        scatter_start(slot)

        @pl.when(j == nu - 1)
        def _():
            scatter_wait(slot)

            @pl.when(j >= 1)
            def _():
                scatter_wait(1 - slot)

    @pl.when(j == 0)
    def _():
        n_real = parts_hbm.shape[0] - 2 * bm * nt
        obuf[...] = jnp.zeros(obuf.shape, F32)
        spare = [pltpu.make_async_copy(obuf.at[s], parts_hbm.at[pl.ds(n_real + s * bm * nt, bm * nt)],
                                       ssem.at[s])
                 for s in range(2)]
        for cp in spare:
            cp.start()
        for cp in spare:
            cp.wait()

    used = j < nu

    @pl.when(used & (j == 0))
    def _():
        gather_start(tok_ref, 0)

    @pl.when(used & ((j == 0) | (be_ref[j] != be_ref[jnp.maximum(j - 1, 0)])))
    def _():
        wg_b[...] = wg_ref[...].astype(BF16)
        wu_b[...] = wu_ref[...].astype(BF16)
        wd_b[...] = wd_ref[...].astype(BF16)

    for slot in range(2):
        @pl.when(used & (lax.rem(j, 2) == slot))
        def _(slot=slot):
            step(slot)


def _expert_mlp(blk_expert, n_used, tok, dst, x, wg, wu, wd, bm, n_out):
    n_blocks = blk_expert.shape[0]
    w_map = lambda j, be, nu: (be[j], 0, 0)
    idx_spec = lambda f: pl.BlockSpec((None, 1, bm), lambda j, be, nu: (f(j), 0, 0),
                                      memory_space=pltpu.SMEM)
    any_spec = pl.BlockSpec(memory_space=pl.ANY)
    grid_spec = pltpu.PrefetchScalarGridSpec(
        num_scalar_prefetch=2,
        grid=(n_blocks,),
        in_specs=[
            idx_spec(lambda j: j),
            idx_spec(lambda j: jnp.minimum(j + 1, n_blocks - 1)),
            idx_spec(lambda j: j),
            any_spec,
            pl.BlockSpec((None, D_MODEL, D_FF), w_map),
            pl.BlockSpec((None, D_MODEL, D_FF), w_map),
            pl.BlockSpec((None, D_FF, D_MODEL), w_map),
        ],
        out_specs=any_spec,
        scratch_shapes=[
            pltpu.VMEM((2, bm * ROW_TILES, LANES), F32),
            pltpu.VMEM((2, bm * ROW_TILES, LANES), F32),
            pltpu.VMEM((D_MODEL, D_FF), BF16),
            pltpu.VMEM((D_MODEL, D_FF), BF16),
            pltpu.VMEM((D_FF, D_MODEL), BF16),
            pltpu.SemaphoreType.DMA((2,)),
            pltpu.SemaphoreType.DMA((2,)),
        ],
    )
    tok3 = (tok * ROW_TILES).reshape(n_blocks, 1, bm)
    return pl.pallas_call(
        functools.partial(_expert_mlp_kernel, bm=bm),
        grid_spec=grid_spec,
        out_shape=jax.ShapeDtypeStruct((n_out * ROW_TILES, LANES), F32),
        compiler_params=pltpu.CompilerParams(
            dimension_semantics=("arbitrary",), vmem_limit_bytes=VMEM_LIMIT),
        name="expert_mlp",
    )(blk_expert, n_used, tok3, tok3, (dst * ROW_TILES).reshape(n_blocks, 1, bm), x, wg, wu, wd)


def _moe_plan(e_idx, bm, n_blocks):
    n_tok = e_idx.shape[0]
    n_asg = 2 * n_tok
    n_pad = n_blocks * bm - n_asg
    e_flat = e_idx.reshape(-1)
    experts = jnp.arange(N_EXPERTS, dtype=jnp.int32)
    counts = jnp.sum((e_flat[:, None] == experts[None, :]).astype(jnp.int32), axis=0)
    nblk = (counts + bm - 1) // bm
    blk_end = jnp.cumsum(nblk)
    n_used = blk_end[-1]
    j = jnp.arange(n_blocks, dtype=jnp.int32)
    be = jnp.sum((blk_end[None, :] <= jnp.minimum(j, n_used - 1)[:, None]).astype(jnp.int32), axis=1)
    be = jnp.minimum(be, N_EXPERTS - 1)
    pad_end = jnp.cumsum(nblk * bm - counts)
    pad_expert = jnp.sum((pad_end[None, :] <= jnp.arange(n_pad, dtype=jnp.int32)[:, None]).astype(jnp.int32),
                         axis=1)
    keys = jnp.concatenate([2 * e_flat, 2 * pad_expert + 1])
    ids = jnp.concatenate([jnp.arange(n_asg, dtype=jnp.int32), jnp.full((n_pad,), -1, jnp.int32)])
    _, window = lax.sort((keys, ids), num_keys=1, is_stable=True)
    window = window.reshape(n_blocks, bm)
    valid = window >= 0
    q = jnp.arange(bm, dtype=jnp.int32)
    tok = jnp.where(valid, window // 2, (j[:, None] * bm + q[None, :]) % n_tok)
    dst = jnp.where(valid, (window % 2) * n_tok + window // 2,
                    2 * n_tok + (j[:, None] % 2) * bm + q[None, :])
    return tok.astype(jnp.int32), dst.astype(jnp.int32), be, n_used.astype(jnp.int32).reshape(1)


def _final_kernel(h_ref, p0_ref, p1_ref, route_ref, w_ref, op_ref, os_ref, *, n_prompt_steps):
    route = route_ref[...]
    p0 = _load_row_tiles(p0_ref)
    p1 = _load_row_tiles(p1_ref)
    y = _rms(h_ref[...] + route[:, 2:3] * p0 + route[:, 3:4] * p1, w_ref[...])
    i = pl.program_id(0)

    @pl.when(i < n_prompt_steps)
    def _():
        op_ref[...] = y

    @pl.when(i >= n_prompt_steps)
    def _():
        os_ref[...] = y


def _final(h, parts, route, w, tf, n_prompt):
    n_tok = h.shape[0]
    assert n_tok % tf == 0 and n_prompt % tf == 0
    n_p = n_prompt // tf
    row = lambda d: pl.BlockSpec((tf, d), lambda i: (i, 0))
    part = lambda k: pl.BlockSpec((tf * ROW_TILES, LANES), lambda i: (k * (n_tok // tf) + i, 0))
    return pl.pallas_call(
        functools.partial(_final_kernel, n_prompt_steps=n_p),
        grid=(n_tok // tf,),
        in_specs=[row(D_MODEL), part(0), part(1), row(LANES), _const_spec(w.shape)],
        out_specs=[pl.BlockSpec((tf, D_MODEL), lambda i: (jnp.minimum(i, n_p - 1), 0)),
                   pl.BlockSpec((tf, D_MODEL), lambda i: (jnp.maximum(i - n_p, 0), 0))],
        out_shape=[jax.ShapeDtypeStruct((n_prompt, D_MODEL), F32),
                   jax.ShapeDtypeStruct((n_tok - n_prompt, D_MODEL), F32)],
        compiler_params=pltpu.CompilerParams(
            dimension_semantics=("arbitrary",), vmem_limit_bytes=VMEM_LIMIT),
        name="final_combine",
    )(h, parts, parts, route, w)


PROMPT_TILE = 256
MOE_BLOCK_ROWS = 256
FINAL_TILE = 512
STATE_SEQS = 8


def _pad_lanes(a, width=LANES):
    return jnp.pad(a, [(0, 0)] * (a.ndim - 1) + [(0, width - a.shape[-1])])


def _layer_weights(l, norm_mix_w, w_in, conv_w, conv_b, dt_bias, a_log, d_skip, ssm_norm_w, sg_norm_w,
                   sg_w, sg_b, w_out, norm_ffn_w, w_router_group, b_router_group, w_router_expert,
                   b_router_expert, n_dec):
    c0, c1, c2 = D_SSM, D_SSM + C_CONV, D_SSM + C_CONV + SSM_HEADS
    wi = w_in[l]
    causal = jnp.tril(jnp.ones((CHUNK, CHUNK), bool))
    sgw_l = jnp.where(causal, sg_w[l], 0.0)
    head_of_lane = jnp.arange(D_SSM, dtype=jnp.int32) // SSM_HEAD_DIM
    expand = (jnp.arange(LANES, dtype=jnp.int32)[:, None] == head_of_lane[None, :]).astype(F32)
    row = lambda a: a.reshape(1, -1).astype(F32)
    gap = ROUTER_EXPERT_ROW - N_EXPERT_GROUPS
    rest = LANES - ROUTER_EXPERT_ROW - N_EXPERTS
    wrt = jnp.concatenate([w_router_group[l].T, jnp.zeros((gap, D_MODEL), F32), w_router_expert[l].T,
                           jnp.zeros((rest, D_MODEL), F32)], axis=0).astype(F32)
    brc = jnp.concatenate([b_router_group[l], jnp.zeros((gap,), F32), b_router_expert[l],
                           jnp.zeros((rest,), F32)]).astype(F32).reshape(LANES, 1)
    return {
        'nmw': row(norm_mix_w[l]),
        'wz': wi[:, :c0].astype(BF16),
        'wxbc': wi[:, c0:c1].astype(BF16),
        'wdt': _pad_lanes(wi[:, c1:c2]).astype(BF16),
        'wuv': wi[:, c2:].astype(BF16),
        'convw': conv_w[l].astype(F32),
        'convb': row(conv_b[l]),
        'dtb': _pad_lanes(row(dt_bias[l])),
        'alog': _pad_lanes(row(a_log[l])),
        'dskip': row(jnp.repeat(d_skip[l], SSM_HEAD_DIM)),
        'ssmnw': row(ssm_norm_w[l]),
        'sgnw': row(sg_norm_w[l]),
        'sgw': sgw_l.astype(BF16),
        'sgb': jnp.repeat(sg_b[l].T, SG_HEAD_DIM, axis=1).astype(F32),
        'w4': jnp.repeat(jnp.transpose(sgw_l[:, :n_dec, :n_dec], (1, 2, 0)), SG_HEAD_DIM, axis=2).astype(F32),
        'b4': jnp.repeat(sg_b[l][:, :n_dec].T, SG_HEAD_DIM, axis=1).astype(F32),
        'expand': expand,
        'wout': w_out[l].astype(BF16),
        'nfw': row(norm_ffn_w[l]),
        'wr': jnp.stack([wrt.astype(BF16), (wrt - wrt.astype(BF16).astype(F32)).astype(BF16)]),
        'br': brc,
    }


def _layer(l, xp, xs_slab, state_conv, state_ssm, w, w_gate, w_up, w_down):
    bp, seq, _ = xp.shape
    nb, nt = state_conv.shape[1], xs_slab.shape[0] // state_conv.shape[1]
    q = HEADS_PER_GROUP * SSM_HEAD_DIM

    cprev = jnp.transpose(state_conv[l], (1, 0, 2))
    z, ypart, ea, ysg, v, convn, cmat, bmat, xw, dec = _sample_front(xs_slab, cprev, w, nb, nt)
    to_seq = lambda a, d: jnp.pad(
        jnp.transpose(a.reshape(nt, nb, SSM_GROUPS, d), (1, 2, 0, 3)),
        ((0, 0), (0, 0), (0, SUBLANES - nt), (0, 0)))
    ssm_s, zoff = _sample_state(
        dec[:, :SSM_HEADS], state_ssm[l].reshape(nb, SSM_GROUPS, q, SSM_STATE),
        to_seq(cmat, SSM_STATE), to_seq(bmat, SSM_STATE), to_seq(xw, q), STATE_SEQS)
    zoff = jnp.transpose(zoff[:, :, :nt], (2, 0, 1, 3)).reshape(nt * nb, D_SSM)
    h_s, hn2_s, route_s = _sample_back(xs_slab, ypart, ea, zoff, z, ysg, w)

    h, hn2, route, convt_p, ssm_p = _prompt_mixer(xp, w, PROMPT_TILE, h_s, hn2_s, route_s)

    n_tok = h.shape[0]
    bm = MOE_BLOCK_ROWS
    n_blocks = -(-(2 * n_tok) // bm) + N_EXPERTS
    tok, dst, blk_expert, n_used = _moe_plan(route[:, :2].astype(jnp.int32), bm, n_blocks)
    parts = _expert_mlp(blk_expert, n_used, tok, dst, hn2, w_gate[l], w_up[l], w_down[l], bm,
                        2 * n_tok + 2 * bm)
    outs = dict(
        h=h, parts=parts, route=route,
        conv_p=convt_p[:, SUBLANES - (CONV_K - 1):],
        ssm_p=ssm_p.reshape(bp, SSM_GROUPS, HEADS_PER_GROUP, SSM_HEAD_DIM, SSM_STATE),
        conv_s=jnp.transpose(convn, (1, 0, 2)),
        ssm_s=ssm_s.reshape(nb, SSM_GROUPS, HEADS_PER_GROUP, SSM_HEAD_DIM, SSM_STATE),
        v_s=jnp.transpose(v.reshape(nt, nb, D_SG), (1, 0, 2)),
    )
    return outs


def kernel(x_prompt, x_sample, state_conv, state_ssm, norm_mix_w, w_in, conv_w, conv_b, dt_bias, a_log, d_skip, ssm_norm_w, sg_norm_w, sg_w, sg_b, w_out, norm_ffn_w, w_router_group, b_router_group, w_router_expert, b_router_expert, w_gate, w_up, w_down, norm_final_w):
    depth = w_in.shape[0]
    assert depth == 1, "the fused final norm assumes a single layer"
    bp, seq, _ = x_prompt.shape
    nb, nt, _ = x_sample.shape
    l = 0
    w = _layer_weights(l, norm_mix_w, w_in, conv_w, conv_b, dt_bias, a_log, d_skip, ssm_norm_w, sg_norm_w,
                       sg_w, sg_b, w_out, norm_ffn_w, w_router_group, b_router_group, w_router_expert,
                       b_router_expert, nt)
    xs_slab = jnp.transpose(x_sample, (1, 0, 2)).reshape(nt * nb, D_MODEL)
    outs = _layer(l, x_prompt, xs_slab, state_conv, state_ssm, w, w_gate, w_up, w_down)
    y_p, y_s = _final(outs['h'], outs['parts'], outs['route'], norm_final_w.reshape(1, -1).astype(F32),
                      FINAL_TILE, bp * seq)
    y_prompt = y_p.reshape(bp, seq, D_MODEL)
    y_sample = jnp.transpose(y_s.reshape(nt, nb, D_MODEL), (1, 0, 2))
    return (y_prompt, y_sample, outs['conv_p'][None], outs['ssm_p'][None], outs['conv_s'][None],
            outs['ssm_s'][None], outs['v_s'][None])
```

```python
import functools

import jax
import jax.numpy as jnp
from jax import lax
from jax.experimental import pallas as pl
from jax.experimental.pallas import tpu as pltpu

D_MODEL = 1024
D_SSM = 1024
SSM_HEAD_DIM = 64
SSM_HEADS = 16
SSM_GROUPS = 2
HEADS_PER_GROUP = 8
SSM_STATE = 128
CONV_K = 4
C_CONV = D_SSM + 2 * SSM_GROUPS * SSM_STATE
D_SG = 1024
SG_HEADS = 8
SG_HEAD_DIM = 128
CHUNK = 128
N_EXPERT_GROUPS = 4
EXPERTS_PER_GROUP = 8
N_EXPERTS = 32
D_FF = 512
EPS = 1e-6

LANES = 128
SUBLANES = 8
VMEM_LIMIT = 56 * 1024 * 1024

F32 = jnp.float32
BF16 = jnp.bfloat16
HIGHEST = lax.Precision.HIGHEST


def _dot(a, b):
    return jnp.dot(a, b, preferred_element_type=F32)


def _dot_f32(a, b):
    return jnp.dot(a, b, preferred_element_type=F32, precision=HIGHEST)


def _rms(x, w):
    return x * lax.rsqrt(jnp.mean(x * x, axis=-1, keepdims=True) + EPS) * w


def _silu(x):
    return x * (1.0 / (1.0 + jnp.exp(-x)))


def _gelu(x):
    return 0.5 * x * (1.0 + lax.erf(x * 0.7071067811865476))


def _softplus(x):
    return jnp.maximum(x, 0.0) + jnp.log1p(jnp.exp(-jnp.abs(x)))


def _gated_group_norm(y, z, w):
    g = y * _silu(z)
    half = D_SSM // SSM_GROUPS
    parts = []
    for k in range(SSM_GROUPS):
        gk = g[:, k * half:(k + 1) * half]
        parts.append(gk * lax.rsqrt(jnp.mean(gk * gk, axis=-1, keepdims=True) + EPS))
    return jnp.concatenate(parts, axis=1) * w


ROUTER_GROUP_ROW = 0
ROUTER_EXPERT_ROW = SUBLANES


def _route_t(lt):
    r = lt.shape[1]
    row = lax.broadcasted_iota(jnp.int32, (SUBLANES, r), 0)
    rowf = row.astype(F32)
    big = float(SUBLANES)
    red = lambda f, a: f(a, axis=0, keepdims=True)
    gl = jnp.where(row < N_EXPERT_GROUPS, lt[ROUTER_GROUP_ROW:ROUTER_GROUP_ROW + SUBLANES], -jnp.inf)
    ge = jnp.exp(gl - red(jnp.max, gl))
    p_grp = ge / red(jnp.sum, ge)
    g_p = red(jnp.max, p_grp)
    g_idx = red(jnp.min, jnp.where(p_grp == g_p, rowf, big))
    el = lt[ROUTER_EXPERT_ROW:ROUTER_EXPERT_ROW + EXPERTS_PER_GROUP]
    for g in range(1, N_EXPERT_GROUPS):
        lo = ROUTER_EXPERT_ROW + g * EXPERTS_PER_GROUP
        el = jnp.where(g_idx == float(g), lt[lo:lo + EXPERTS_PER_GROUP], el)
    ee = jnp.exp(el - red(jnp.max, el))
    pe = ee / red(jnp.sum, ee)
    v1 = red(jnp.max, pe)
    i1 = red(jnp.min, jnp.where(pe == v1, rowf, big))
    rest = rowf != i1
    pe2 = jnp.where(rest, pe, -1.0)
    v2 = red(jnp.max, pe2)
    i2 = red(jnp.min, jnp.where(rest & (pe2 == v2), rowf, big))
    den = v1 + v2
    base = g_idx * float(EXPERTS_PER_GROUP)
    out = jnp.where(row == 0, base + i1, 0.0)
    out = jnp.where(row == 1, base + i2, out)
    out = jnp.where(row == 2, g_p * v1 / den, out)
    out = jnp.where(row == 3, g_p * v2 / den, out)
    return jnp.concatenate([out, jnp.zeros((LANES - SUBLANES, r), F32)], axis=0).T


def _ffn_front(h, nfw, wrt_ref, brc):
    hn2 = _rms(h, nfw)
    hi = hn2.astype(BF16)
    lo = (hn2 - hi.astype(F32)).astype(BF16)
    nt = lambda a, b: lax.dot_general(a, b, (((1,), (1,)), ((), ())), preferred_element_type=F32)
    lt = nt(wrt_ref[0], hi) + nt(wrt_ref[0], lo) + nt(wrt_ref[1], hi) + brc
    return hn2, _route_t(lt)


ROW_TILES = D_MODEL // LANES


def _store_row_tiles(ref, val):
    r = val.shape[0]
    for k in range(ROW_TILES):
        ref[pl.ds(k, r, stride=ROW_TILES), :] = val[:, k * LANES:(k + 1) * LANES]


def _load_row_tiles(ref):
    r = ref.shape[0] // ROW_TILES
    return jnp.concatenate([ref[pl.ds(k, r, stride=ROW_TILES), :] for k in range(ROW_TILES)], axis=1)


def _prompt_step(
        t, nt, x_ref, nmw_ref, wz_ref, wxbc_ref, wdt_ref, wuv_ref, convw_ref, convb_ref, dtb_ref,
        alog_ref, dskip_ref, ssmnw_ref, sgnw_ref, sgw_ref, sgb_ref, wout_ref, nfw_ref,
        wr_ref, br_ref, h_ref, hn2_ref, route_ref, convt_ref, ssm_ref,
        xbc_scr, state_scr, y_scr, mix_scr, tl):
    @pl.when(t == 0)
    def _():
        xbc_scr[0:SUBLANES, :] = jnp.zeros((SUBLANES, C_CONV), F32)
        state_scr[...] = jnp.zeros(state_scr.shape, F32)

    x = x_ref[...]
    hn = _rms(x, nmw_ref[...]).astype(BF16)
    z = _dot(hn, wz_ref[...])
    xbc = _dot(hn, wxbc_ref[...])
    dtr = _dot(hn, wdt_ref[...])
    uv = _gelu(_dot(hn, wuv_ref[...]))

    xbc_scr[SUBLANES:SUBLANES + tl, :] = xbc
    conv = convb_ref[...]
    for k in range(CONV_K):
        off = SUBLANES - (CONV_K - 1) + k
        conv = conv + convw_ref[k:k + 1, :] * xbc_scr[off:off + tl, :]
    tail = xbc_scr[tl:tl + SUBLANES, :]
    xbc_scr[0:SUBLANES, :] = tail
    convt_ref[...] = tail
    act = _silu(conv)
    xs = act[:, :D_SSM]
    bmat = act[:, D_SSM:D_SSM + SSM_GROUPS * SSM_STATE]
    cmat = act[:, D_SSM + SSM_GROUPS * SSM_STATE:]

    dt = _softplus(dtr + dtb_ref[...])
    a_row = -jnp.exp(alog_ref[...])

    li = lax.broadcasted_iota(jnp.int32, (CHUNK, CHUNK), 0)
    si = lax.broadcasted_iota(jnp.int32, (CHUNK, CHUNK), 1)
    causal = li >= si
    tri = jnp.where(causal, 1.0, 0.0).astype(F32)
    lo_half = si < SSM_HEAD_DIM

    for c in range(tl // CHUNK):
        rows = slice(c * CHUNK, (c + 1) * CHUNK)
        dt_c = dt[rows]
        acum = _dot_f32(tri, dt_c * a_row)
        acum_t = acum.T
        dt_t = dt_c.T
        w_t = dt_t * jnp.exp(acum_t[:, CHUNK - 1:CHUNK] - acum_t)
        xs_c = xs[rows]
        for g in range(SSM_GROUPS):
            gl = slice(g * SSM_STATE, (g + 1) * SSM_STATE)
            b_g = bmat[rows, gl]
            c_g = cmat[rows, gl]
            b_gt = b_g.T
            cb = _dot(c_g.astype(BF16), b_gt.astype(BF16))
            for j in range(HEADS_PER_GROUP // 2):
                pl_ = slice(g * 512 + j * LANES, g * 512 + (j + 1) * LANES)
                m_l, ec_l, s_l, ea_last = [], [], [], []
                for k in range(2):
                    hh = g * HEADS_PER_GROUP + 2 * j + k
                    colb = jnp.broadcast_to(acum[:, hh:hh + 1], (CHUNK, CHUNK))
                    rowb = jnp.broadcast_to(acum_t[hh:hh + 1, :], (CHUNK, CHUNK))
                    ea = jnp.exp(colb)
                    seg = jnp.where(causal, colb - rowb, 0.0)
                    lmat = jnp.where(causal, jnp.exp(seg), 0.0)
                    m_l.append(cb * lmat * jnp.broadcast_to(dt_t[hh:hh + 1, :], (CHUNK, CHUNK)))
                    ec_l.append(c_g * ea)
                    s_l.append(b_gt * jnp.broadcast_to(w_t[hh:hh + 1, :], (CHUNK, CHUNK)))
                    ea_last.append(ea[CHUNK - 1:CHUNK, :])
                xs_p = xs_c[:, pl_]
                st_p = state_scr[g, :, j * LANES:(j + 1) * LANES]
                rx = jnp.concatenate([jnp.where(lo_half, xs_p, 0.0),
                                      jnp.where(lo_half, 0.0, xs_p)], axis=0).astype(BF16)
                rs = jnp.concatenate([jnp.where(lo_half, st_p, 0.0),
                                      jnp.where(lo_half, 0.0, st_p)], axis=0).astype(BF16)
                lhs = jnp.concatenate(m_l + ec_l, axis=1).astype(BF16)
                y_p = _dot(lhs, jnp.concatenate([rx, rs], axis=0))
                y_scr[rows, pl_] = y_p
                dec = jnp.where(lo_half[0:1, :], ea_last[0], ea_last[1])
                upd = _dot(jnp.concatenate(s_l, axis=1).astype(BF16), rx)
                state_scr[g, :, j * LANES:(j + 1) * LANES] = st_p * dec + upd

    y = y_scr[...] + dskip_ref[...] * xs
    y_ssm = _gated_group_norm(y, z, ssmnw_ref[...])
    mix_scr[:, :D_SSM] = y_ssm.astype(BF16)

    u = uv[:, :D_SG]
    v = _rms(uv[:, D_SG:], sgnw_ref[...])
    vb = v.astype(BF16)
    for c in range(tl // CHUNK):
        rows = slice(c * CHUNK, (c + 1) * CHUNK)
        for hd in range(SG_HEADS):
            hl = slice(hd * SG_HEAD_DIM, (hd + 1) * SG_HEAD_DIM)
            sv = _dot(sgw_ref[hd], vb[rows, hl]) + sgb_ref[:, hl]
            mix_scr[rows, D_SSM + hd * SG_HEAD_DIM:D_SSM + (hd + 1) * SG_HEAD_DIM] = (
                u[rows, hl] * sv).astype(BF16)

    h = x + _dot(mix_scr[...], wout_ref[...])
    h_ref[...] = h
    hn2, route = _ffn_front(h, nfw_ref[...], wr_ref, br_ref[...])
    _store_row_tiles(hn2_ref, hn2)
    route_ref[...] = route

    @pl.when(t == nt - 1)
    def _():
        for g in range(SSM_GROUPS):
            ssm_ref[g] = state_scr[g].T


def _prompt_mixer_kernel(
        x_ref, nmw_ref, wz_ref, wxbc_ref, wdt_ref, wuv_ref, convw_ref, convb_ref, dtb_ref,
        alog_ref, dskip_ref, ssmnw_ref, sgnw_ref, sgw_ref, sgb_ref, wout_ref, nfw_ref,
        wr_ref, br_ref, hs_ref, hn2s_ref, routes_ref,
        h_ref, hn2_ref, route_ref, convt_ref, ssm_ref,
        xbc_scr, state_scr, y_scr, mix_scr, *, tl, nt, n_main):
    i = pl.program_id(0)

    @pl.when(i >= n_main)
    def _():
        h_ref[...] = hs_ref[...]
        hn2_ref[...] = hn2s_ref[...]
        route_ref[...] = routes_ref[...]

    @pl.when(i < n_main)
    def _():
        _prompt_step(
            lax.rem(i, nt), nt, x_ref, nmw_ref, wz_ref, wxbc_ref, wdt_ref, wuv_ref, convw_ref,
            convb_ref, dtb_ref, alog_ref, dskip_ref, ssmnw_ref, sgnw_ref, sgw_ref, sgb_ref, wout_ref,
            nfw_ref, wr_ref, br_ref, h_ref, hn2_ref, route_ref, convt_ref, ssm_ref,
            xbc_scr, state_scr, y_scr, mix_scr, tl)


def _const_spec(shape):
    zeros = (0,) * len(shape)
    return pl.BlockSpec(shape, lambda *_: zeros)


def _prompt_mixer(x, w, tl, h_s, hn2_s, route_s):
    bsz, seq, _ = x.shape
    nt = seq // tl
    n_main = bsz * nt
    n_s = h_s.shape[0]
    assert seq % tl == 0 and n_s % tl == 0
    n_tok = bsz * seq + n_s
    main = lambda i: jnp.minimum(i, n_main - 1)
    tail = lambda i: jnp.maximum(i - n_main, 0)
    weights = [w['nmw'], w['wz'], w['wxbc'], w['wdt'], w['wuv'], w['convw'], w['convb'], w['dtb'],
               w['alog'], w['dskip'], w['ssmnw'], w['sgnw'], w['sgw'], w['sgb'], w['wout'], w['nfw'],
               w['wr'], w['br']]
    in_specs = ([pl.BlockSpec((tl, D_MODEL), lambda i: (main(i), 0))]
                + [_const_spec(a.shape) for a in weights]
                + [pl.BlockSpec(blk, lambda i: (tail(i), 0))
                   for blk in ((tl, D_MODEL), (tl * ROW_TILES, LANES), (tl, LANES))])
    q = HEADS_PER_GROUP * SSM_HEAD_DIM
    out_shape = [
        jax.ShapeDtypeStruct((n_tok, D_MODEL), F32),
        jax.ShapeDtypeStruct((n_tok * ROW_TILES, LANES), F32),
        jax.ShapeDtypeStruct((n_tok, LANES), F32),
        jax.ShapeDtypeStruct((bsz, SUBLANES, C_CONV), F32),
        jax.ShapeDtypeStruct((bsz, SSM_GROUPS, q, SSM_STATE), F32),
    ]
    out_specs = [
        pl.BlockSpec((tl, D_MODEL), lambda i: (i, 0)),
        pl.BlockSpec((tl * ROW_TILES, LANES), lambda i: (i, 0)),
        pl.BlockSpec((tl, LANES), lambda i: (i, 0)),
        pl.BlockSpec((None, SUBLANES, C_CONV), lambda i: (main(i) // nt, 0, 0)),
        pl.BlockSpec((None, SSM_GROUPS, q, SSM_STATE), lambda i: (main(i) // nt, 0, 0, 0)),
    ]
    scratch = [
        pltpu.VMEM((tl + SUBLANES, C_CONV), F32),
        pltpu.VMEM((SSM_GROUPS, SSM_STATE, q), F32),
        pltpu.VMEM((tl, D_SSM), F32),
        pltpu.VMEM((tl, D_SSM + D_SG), BF16),
    ]
    return pl.pallas_call(
        functools.partial(_prompt_mixer_kernel, tl=tl, nt=nt, n_main=n_main),
        grid=(n_main + n_s // tl,),
        in_specs=in_specs,
        out_specs=out_specs,
        out_shape=out_shape,
        scratch_shapes=scratch,
        compiler_params=pltpu.CompilerParams(
            dimension_semantics=("arbitrary",), vmem_limit_bytes=VMEM_LIMIT),
        name="prompt_mixer",
    )(x.reshape(bsz * seq, D_MODEL), *weights, h_s, hn2_s, route_s)


def _sample_front_kernel(
        x_ref, cprev_ref, nmw_ref, wz_ref, wxbc_ref, wdt_ref, wuv_ref, convw_ref, convb_ref,
        dtb_ref, alog_ref, dskip_ref, sgnw_ref, w4_ref, b4_ref, expand_ref,
        z_ref, ypart_ref, ea_ref, ysg_ref, v_ref, convn_ref, c_ref, b_ref, xw_ref, dec_ref,
        *, nb, nt):
    x = x_ref[...]
    hn = _rms(x, nmw_ref[...]).astype(BF16)
    z_ref[...] = _dot(hn, wz_ref[...])
    xbc = _dot(hn, wxbc_ref[...])
    dtr = _dot(hn, wdt_ref[...])
    uv = _gelu(_dot(hn, wuv_ref[...]))

    slab = lambda a, t: a[t * nb:(t + 1) * nb]
    full = [cprev_ref[k] for k in range(CONV_K - 1)] + [slab(xbc, t) for t in range(nt)]
    for k in range(CONV_K - 1):
        convn_ref[k] = full[nt + k]
    xs, bm, cm = [], [], []
    for t in range(nt):
        conv = convb_ref[...]
        for k in range(CONV_K):
            conv = conv + convw_ref[k:k + 1, :] * full[t + k]
        act = _silu(conv)
        xs.append(act[:, :D_SSM])
        bm.append(act[:, D_SSM:D_SSM + SSM_GROUPS * SSM_STATE])
        cm.append(act[:, D_SSM + SSM_GROUPS * SSM_STATE:])
        b_ref[t * nb:(t + 1) * nb, :] = bm[t]
        c_ref[t * nb:(t + 1) * nb, :] = cm[t]

    dt = _softplus(dtr + dtb_ref[...])
    a_row = -jnp.exp(alog_ref[...])
    dts = [slab(dt, t) for t in range(nt)]
    cum = []
    for t in range(nt):
        da = dts[t] * a_row
        cum.append(da if t == 0 else cum[t - 1] + da)
    dec_ref[...] = jnp.exp(cum[nt - 1])

    lane = lax.broadcasted_iota(jnp.int32, (nb, LANES), 1)
    first_group = lane < HEADS_PER_GROUP
    facs = [jnp.exp(cum[t]) for t in range(nt)]
    facs += [dts[s] * jnp.exp(cum[nt - 1] - cum[s]) for s in range(nt)]
    pairs = []
    for t in range(nt):
        for s in range(t + 1):
            cb = []
            for g in range(SSM_GROUPS):
                gl = slice(g * SSM_STATE, (g + 1) * SSM_STATE)
                cb.append(jnp.sum(cm[t][:, gl] * bm[s][:, gl], axis=1, keepdims=True))
            cbh = jnp.where(first_group, cb[0], cb[1])
            facs.append(jnp.exp(cum[t] - cum[s]) * dts[s] * cbh)
            pairs.append((t, s))
    fx = _dot_f32(jnp.concatenate(facs, axis=0), expand_ref[...])
    fslab = lambda i: fx[i * nb:(i + 1) * nb]
    for t in range(nt):
        ea_ref[t * nb:(t + 1) * nb, :] = fslab(t)
        xw_ref[t * nb:(t + 1) * nb, :] = xs[t] * fslab(nt + t)
    for t in range(nt):
        acc = dskip_ref[...] * xs[t]
        for i, (tt, s) in enumerate(pairs):
            if tt == t:
                acc = acc + fslab(2 * nt + i) * xs[s]
        ypart_ref[t * nb:(t + 1) * nb, :] = acc

    u = uv[:, :D_SG]
    v = _rms(uv[:, D_SG:], sgnw_ref[...])
    v_ref[...] = v
    for t in range(nt):
        sv = b4_ref[t:t + 1, :]
        for s in range(t + 1):
            sv = sv + w4_ref[t, s:s + 1, :] * slab(v, s)
        ysg_ref[t * nb:(t + 1) * nb, :] = slab(u, t) * sv


def _sample_front(x_slab, cprev, w, nb, nt):
    r = nb * nt
    weights = [w['nmw'], w['wz'], w['wxbc'], w['wdt'], w['wuv'], w['convw'], w['convb'], w['dtb'],
               w['alog'], w['dskip'], w['sgnw'], w['w4'], w['b4'], w['expand']]
    ins = [x_slab, cprev] + weights
    out_shape = [
        jax.ShapeDtypeStruct((r, D_SSM), F32),
        jax.ShapeDtypeStruct((r, D_SSM), F32),
        jax.ShapeDtypeStruct((r, D_SSM), F32),
        jax.ShapeDtypeStruct((r, D_SG), F32),
        jax.ShapeDtypeStruct((r, D_SG), F32),
        jax.ShapeDtypeStruct((CONV_K - 1, nb, C_CONV), F32),
        jax.ShapeDtypeStruct((r, SSM_GROUPS * SSM_STATE), F32),
        jax.ShapeDtypeStruct((r, SSM_GROUPS * SSM_STATE), F32),
        jax.ShapeDtypeStruct((r, D_SSM), F32),
        jax.ShapeDtypeStruct((nb, LANES), F32),
    ]
    return pl.pallas_call(
        functools.partial(_sample_front_kernel, nb=nb, nt=nt),
        grid=(1,),
        in_specs=[_const_spec(a.shape) for a in ins],
        out_specs=[_const_spec(s.shape) for s in out_shape],
        out_shape=out_shape,
        compiler_params=pltpu.CompilerParams(
            dimension_semantics=("arbitrary",), vmem_limit_bytes=VMEM_LIMIT),
        name="sample_front",
    )(*ins)


def _sample_state_kernel(dec_ref, st_ref, cq_ref, bq_ref, xw_ref, so_ref, z_ref, *, seqs):
    for bb in range(seqs):
        for g in range(SSM_GROUPS):
            s0 = st_ref[bb, g]
            z_ref[bb, g] = lax.dot_general(
                cq_ref[bb, g].astype(BF16), s0.astype(BF16), (((1,), (1,)), ((), ())),
                preferred_element_type=F32)
            upd = lax.dot_general(
                xw_ref[bb, g].astype(BF16), bq_ref[bb, g].astype(BF16), (((0,), (0,)), ((), ())),
                preferred_element_type=F32)
            for hh in range(HEADS_PER_GROUP):
                hs = slice(hh * SSM_HEAD_DIM, (hh + 1) * SSM_HEAD_DIM)
                so_ref[bb, g, hs, :] = s0[hs] * dec_ref[bb, g * HEADS_PER_GROUP + hh] + upd[hs]


def _sample_state(dec, state, cq, bq, xwq, seqs):
    nb = state.shape[0]
    q = HEADS_PER_GROUP * SSM_HEAD_DIM
    blk = lambda *tail: pl.BlockSpec((seqs, SSM_GROUPS) + tail, lambda i: (i, 0, 0, 0))
    return pl.pallas_call(
        functools.partial(_sample_state_kernel, seqs=seqs),
        grid=(nb // seqs,),
        in_specs=[
            pl.BlockSpec((seqs, SSM_HEADS), lambda i: (i, 0), memory_space=pltpu.SMEM),
            blk(q, SSM_STATE), blk(SUBLANES, SSM_STATE), blk(SUBLANES, SSM_STATE), blk(SUBLANES, q),
        ],
        out_specs=[blk(q, SSM_STATE), blk(SUBLANES, q)],
        out_shape=[
            jax.ShapeDtypeStruct((nb, SSM_GROUPS, q, SSM_STATE), F32),
            jax.ShapeDtypeStruct((nb, SSM_GROUPS, SUBLANES, q), F32),
        ],
        compiler_params=pltpu.CompilerParams(
            dimension_semantics=("arbitrary",), vmem_limit_bytes=VMEM_LIMIT),
        name="sample_state",
    )(dec, state, cq, bq, xwq)


def _sample_back_kernel(x_ref, ypart_ref, ea_ref, zoff_ref, z_ref, ysg_ref, ssmnw_ref, wout_ref,
                        nfw_ref, wr_ref, br_ref, h_ref, hn2_ref, route_ref):
    y = ypart_ref[...] + ea_ref[...] * zoff_ref[...]
    y_ssm = _gated_group_norm(y, z_ref[...], ssmnw_ref[...])
    mix_in = jnp.concatenate([y_ssm.astype(BF16), ysg_ref[...].astype(BF16)], axis=1)
    h = x_ref[...] + _dot(mix_in, wout_ref[...])
    h_ref[...] = h
    hn2, route = _ffn_front(h, nfw_ref[...], wr_ref, br_ref[...])
    _store_row_tiles(hn2_ref, hn2)
    route_ref[...] = route


def _sample_back(x_slab, ypart, ea, zoff, z, ysg, w):
    r = x_slab.shape[0]
    ins = [x_slab, ypart, ea, zoff, z, ysg, w['ssmnw'], w['wout'], w['nfw'], w['wr'], w['br']]
    out_shape = [
        jax.ShapeDtypeStruct((r, D_MODEL), F32),
        jax.ShapeDtypeStruct((r * ROW_TILES, LANES), F32),
        jax.ShapeDtypeStruct((r, LANES), F32),
    ]
    return pl.pallas_call(
        _sample_back_kernel,
        grid=(1,),
        in_specs=[_const_spec(a.shape) for a in ins],
        out_specs=[_const_spec(s.shape) for s in out_shape],
        out_shape=out_shape,
        compiler_params=pltpu.CompilerParams(
            dimension_semantics=("arbitrary",), vmem_limit_bytes=VMEM_LIMIT),
        name="sample_back",
    )(*ins)


EXPERT_PIECES = 8
SCATTER_DMA_PRIORITY = 1


def _expert_mlp_kernel(be_ref, nu_ref, tok0_ref, tokn_ref, dstp_ref, x_hbm, wg_ref, wu_ref, wd_ref,
                       parts_hbm, xbuf, obuf, xb, hb, wg_b, wu_b, wd_b, gsem, ssem, *, bm):
    j = pl.program_id(0)
    nu = nu_ref[0]
    nt = ROW_TILES
    tile = lambda ref, start: ref.at[pl.ds(pl.multiple_of(start, nt), nt)]

    def gather_copy(idx_ref, s, r):
        return pltpu.make_async_copy(tile(x_hbm, idx_ref[0, r]), xbuf.at[s, pl.ds(r * nt, nt)], gsem.at[s])

    def scatter_copy(s, r):
        return pltpu.make_async_copy(obuf.at[s, pl.ds(r * nt, nt)], tile(parts_hbm, dstp_ref[0, r]),
                                     ssem.at[s])

    def gather_wait(s):
        pltpu.make_async_copy(x_hbm.at[pl.ds(0, bm * nt)], xbuf.at[s], gsem.at[s]).wait()

    def scatter_wait(s):
        pltpu.make_async_copy(obuf.at[s], parts_hbm.at[pl.ds(0, bm * nt)], ssem.at[s]).wait()

    def step(slot):
        gather_wait(slot)

        @pl.when(j >= 1)
        def _():
            scatter_wait(slot)

        per = bm // EXPERT_PIECES

        def start_rows(piece):
            for r in range(piece * per, (piece + 1) * per):
                gather_copy(tokn_ref, 1 - slot, r).start()
                scatter_copy(1 - slot, r).start(priority=SCATTER_DMA_PRIORITY)

        xb[...] = _load_row_tiles(xbuf.at[slot]).astype(BF16)
        half = D_FF // 2
        for c in range(2):
            cols = slice(c * half, (c + 1) * half)
            start_rows(2 * c)
            g = _dot(xb[...], wg_b[:, cols])
            start_rows(2 * c + 1)
            u = _dot(xb[...], wu_b[:, cols])
            hb[:, cols] = (_silu(g) * u).astype(BF16)
        quarter = D_MODEL // 4
        for c in range(4):
            start_rows(4 + c)
            o = _dot(hb[...], wd_b[:, c * quarter:(c + 1) * quarter])
            for k in range(quarter // LANES):
                kk = c * (quarter // LANES) + k
                obuf.at[slot][pl.ds(kk, bm, stride=nt), :] = o[:, k * LANES:(k + 1) * LANES]

    def drain(slot):
        gather_wait(slot)
        scatter_wait(slot)
        for r in range(bm):
            scatter_copy(1 - slot, r).start(priority=SCATTER_DMA_PRIORITY)
        scatter_wait(1 - slot)

    @pl.when(j == 0)
    def _():
        obuf[...] = jnp.zeros(obuf.shape, F32)
        n_real = parts_hbm.shape[0] - bm * nt
        spare = pltpu.make_async_copy(obuf.at[0], parts_hbm.at[pl.ds(n_real, bm * nt)], ssem.at[0])
        spare.start()
        spare.wait()

        def start0(r, carry):
            pltpu.make_async_copy(tile(x_hbm, tok0_ref[0, r]), xbuf.at[0, pl.ds(pl.multiple_of(r * nt, nt), nt)],
                                  gsem.at[0]).start()
            return carry
        lax.fori_loop(0, bm, start0, 0)

    used = j < nu

    @pl.when(used & ((j == 0) | (be_ref[j] != be_ref[jnp.maximum(j - 1, 0)])))
    def _():
        wg_b[...] = wg_ref[...].astype(BF16)
        wu_b[...] = wu_ref[...].astype(BF16)
        wd_b[...] = wd_ref[...].astype(BF16)

    for slot in range(2):
        parity = lax.rem(j, 2) == slot

        @pl.when(used & parity)
        def _(slot=slot):
            step(slot)

        @pl.when((j == nu) & parity)
        def _(slot=slot):
            drain(slot)


def _expert_mlp(blk_expert, n_used, tok, dst, x, wg, wu, wd, bm, n_out):
    n_blocks = blk_expert.shape[0]
    assert bm % EXPERT_PIECES == 0
    w_map = lambda j, be, nu: (be[j], 0, 0)
    idx_spec = lambda f: pl.BlockSpec((None, 1, bm), lambda j, be, nu: (f(j), 0, 0),
                                      memory_space=pltpu.SMEM)
    any_spec = pl.BlockSpec(memory_space=pl.ANY)
    grid_spec = pltpu.PrefetchScalarGridSpec(
        num_scalar_prefetch=2,
        grid=(n_blocks + 1,),
        in_specs=[
            idx_spec(lambda j: 0),
            idx_spec(lambda j: jnp.minimum(j + 1, n_blocks - 1)),
            idx_spec(lambda j: j),
            any_spec,
            pl.BlockSpec((None, D_MODEL, D_FF), w_map),
            pl.BlockSpec((None, D_MODEL, D_FF), w_map),
            pl.BlockSpec((None, D_FF, D_MODEL), w_map),
        ],
        out_specs=any_spec,
        scratch_shapes=[
            pltpu.VMEM((2, bm * ROW_TILES, LANES), F32),
            pltpu.VMEM((2, bm * ROW_TILES, LANES), F32),
            pltpu.VMEM((bm, D_MODEL), BF16),
            pltpu.VMEM((bm, D_FF), BF16),
            pltpu.VMEM((D_MODEL, D_FF), BF16),
            pltpu.VMEM((D_MODEL, D_FF), BF16),
            pltpu.VMEM((D_FF, D_MODEL), BF16),
            pltpu.SemaphoreType.DMA((2,)),
            pltpu.SemaphoreType.DMA((2,)),
        ],
    )
    tok3 = (tok * ROW_TILES).reshape(n_blocks, 1, bm)
    spare = (n_out - bm + jnp.arange(bm, dtype=jnp.int32)).reshape(1, bm)
    dst_prev = (jnp.concatenate([spare, dst], axis=0) * ROW_TILES).reshape(n_blocks + 1, 1, bm)
    blk_expert = jnp.concatenate([blk_expert, blk_expert[-1:]])
    return pl.pallas_call(
        functools.partial(_expert_mlp_kernel, bm=bm),
        grid_spec=grid_spec,
        out_shape=jax.ShapeDtypeStruct((n_out * ROW_TILES, LANES), F32),
        compiler_params=pltpu.CompilerParams(
            dimension_semantics=("arbitrary",), vmem_limit_bytes=VMEM_LIMIT),
        name="expert_mlp",
    )(blk_expert, n_used, tok3, tok3, dst_prev, x, wg, wu, wd)


def _moe_plan(e_idx, bm, n_blocks):
    n_tok = e_idx.shape[0]
    n_asg = 2 * n_tok
    n_pad = n_blocks * bm - n_asg
    e_flat = e_idx.reshape(-1)
    experts = jnp.arange(N_EXPERTS, dtype=jnp.int32)
    counts = jnp.sum((e_flat[:, None] == experts[None, :]).astype(jnp.int32), axis=0)
    nblk = (counts + bm - 1) // bm
    blk_end = jnp.cumsum(nblk)
    n_used = blk_end[-1]
    j = jnp.arange(n_blocks, dtype=jnp.int32)
    be = jnp.sum((blk_end[None, :] <= jnp.minimum(j, n_used - 1)[:, None]).astype(jnp.int32), axis=1)
    be = jnp.minimum(be, N_EXPERTS - 1)
    pad_end = jnp.cumsum(nblk * bm - counts)
    pad_expert = jnp.sum((pad_end[None, :] <= jnp.arange(n_pad, dtype=jnp.int32)[:, None]).astype(jnp.int32),
                         axis=1)
    keys = jnp.concatenate([2 * e_flat, 2 * pad_expert + 1])
    ids = jnp.concatenate([jnp.arange(n_asg, dtype=jnp.int32), jnp.full((n_pad,), -1, jnp.int32)])
    _, window = lax.sort((keys, ids), num_keys=1, is_stable=True)
    window = window.reshape(n_blocks, bm)
    valid = window >= 0
    q = jnp.arange(bm, dtype=jnp.int32)
    tok = jnp.where(valid, window // 2, (j[:, None] * bm + q[None, :]) % n_tok)
    dst = jnp.where(valid, (window % 2) * n_tok + window // 2,
                    2 * n_tok + q[None, :])
    return tok.astype(jnp.int32), dst.astype(jnp.int32), be, n_used.astype(jnp.int32).reshape(1)


def _final_kernel(h_ref, p0_ref, p1_ref, route_ref, w_ref, op_ref, os_ref, *, n_prompt_steps):
    route = route_ref[...]
    p0 = _load_row_tiles(p0_ref)
    p1 = _load_row_tiles(p1_ref)
    y = _rms(h_ref[...] + route[:, 2:3] * p0 + route[:, 3:4] * p1, w_ref[...])
    i = pl.program_id(0)

    @pl.when(i < n_prompt_steps)
    def _():
        op_ref[...] = y

    @pl.when(i >= n_prompt_steps)
    def _():
        os_ref[...] = y


def _final(h, parts, route, w, tf, n_prompt):
    n_tok = h.shape[0]
    assert n_tok % tf == 0 and n_prompt % tf == 0
    n_p = n_prompt // tf
    row = lambda d: pl.BlockSpec((tf, d), lambda i: (i, 0))
    part = lambda k: pl.BlockSpec((tf * ROW_TILES, LANES), lambda i: (k * (n_tok // tf) + i, 0))
    return pl.pallas_call(
        functools.partial(_final_kernel, n_prompt_steps=n_p),
        grid=(n_tok // tf,),
        in_specs=[row(D_MODEL), part(0), part(1), row(LANES), _const_spec(w.shape)],
        out_specs=[pl.BlockSpec((tf, D_MODEL), lambda i: (jnp.minimum(i, n_p - 1), 0)),
                   pl.BlockSpec((tf, D_MODEL), lambda i: (jnp.maximum(i - n_p, 0), 0))],
        out_shape=[jax.ShapeDtypeStruct((n_prompt, D_MODEL), F32),
                   jax.ShapeDtypeStruct((n_tok - n_prompt, D_MODEL), F32)],
        compiler_params=pltpu.CompilerParams(
            dimension_semantics=("arbitrary",), vmem_limit_bytes=VMEM_LIMIT),
        name="final_combine",
    )(h, parts, parts, route, w)


PROMPT_TILE = 256
MOE_BLOCK_ROWS = 256
FINAL_TILE = 512
STATE_SEQS = 8


def _pad_lanes(a, width=LANES):
    return jnp.pad(a, [(0, 0)] * (a.ndim - 1) + [(0, width - a.shape[-1])])


def _layer_weights(l, norm_mix_w, w_in, conv_w, conv_b, dt_bias, a_log, d_skip, ssm_norm_w, sg_norm_w,
                   sg_w, sg_b, w_out, norm_ffn_w, w_router_group, b_router_group, w_router_expert,
                   b_router_expert, n_dec):
    c0, c1, c2 = D_SSM, D_SSM + C_CONV, D_SSM + C_CONV + SSM_HEADS
    wi = w_in[l]
    causal = jnp.tril(jnp.ones((CHUNK, CHUNK), bool))
    sgw_l = jnp.where(causal, sg_w[l], 0.0)
    head_of_lane = jnp.arange(D_SSM, dtype=jnp.int32) // SSM_HEAD_DIM
    expand = (jnp.arange(LANES, dtype=jnp.int32)[:, None] == head_of_lane[None, :]).astype(F32)
    row = lambda a: a.reshape(1, -1).astype(F32)
    gap = ROUTER_EXPERT_ROW - N_EXPERT_GROUPS
    rest = LANES - ROUTER_EXPERT_ROW - N_EXPERTS
    wrt = jnp.concatenate([w_router_group[l].T, jnp.zeros((gap, D_MODEL), F32), w_router_expert[l].T,
                           jnp.zeros((rest, D_MODEL), F32)], axis=0).astype(F32)
    brc = jnp.concatenate([b_router_group[l], jnp.zeros((gap,), F32), b_router_expert[l],
                           jnp.zeros((rest,), F32)]).astype(F32).reshape(LANES, 1)
    return {
        'nmw': row(norm_mix_w[l]),
        'wz': wi[:, :c0].astype(BF16),
        'wxbc': wi[:, c0:c1].astype(BF16),
        'wdt': _pad_lanes(wi[:, c1:c2]).astype(BF16),
        'wuv': wi[:, c2:].astype(BF16),
        'convw': conv_w[l].astype(F32),
        'convb': row(conv_b[l]),
        'dtb': _pad_lanes(row(dt_bias[l])),
        'alog': _pad_lanes(row(a_log[l])),
        'dskip': row(jnp.repeat(d_skip[l], SSM_HEAD_DIM)),
        'ssmnw': row(ssm_norm_w[l]),
        'sgnw': row(sg_norm_w[l]),
        'sgw': sgw_l.astype(BF16),
        'sgb': jnp.repeat(sg_b[l].T, SG_HEAD_DIM, axis=1).astype(F32),
        'w4': jnp.repeat(jnp.transpose(sgw_l[:, :n_dec, :n_dec], (1, 2, 0)), SG_HEAD_DIM, axis=2).astype(F32),
        'b4': jnp.repeat(sg_b[l][:, :n_dec].T, SG_HEAD_DIM, axis=1).astype(F32),
        'expand': expand,
        'wout': w_out[l].astype(BF16),
        'nfw': row(norm_ffn_w[l]),
        'wr': jnp.stack([wrt.astype(BF16), (wrt - wrt.astype(BF16).astype(F32)).astype(BF16)]),
        'br': brc,
    }


def _layer(l, xp, xs_slab, state_conv, state_ssm, w, w_gate, w_up, w_down):
    bp, seq, _ = xp.shape
    nb, nt = state_conv.shape[1], xs_slab.shape[0] // state_conv.shape[1]
    q = HEADS_PER_GROUP * SSM_HEAD_DIM

    cprev = jnp.transpose(state_conv[l], (1, 0, 2))
    z, ypart, ea, ysg, v, convn, cmat, bmat, xw, dec = _sample_front(xs_slab, cprev, w, nb, nt)
    to_seq = lambda a, d: jnp.pad(
        jnp.transpose(a.reshape(nt, nb, SSM_GROUPS, d), (1, 2, 0, 3)),
        ((0, 0), (0, 0), (0, SUBLANES - nt), (0, 0)))
    ssm_s, zoff = _sample_state(
        dec[:, :SSM_HEADS], state_ssm[l].reshape(nb, SSM_GROUPS, q, SSM_STATE),
        to_seq(cmat, SSM_STATE), to_seq(bmat, SSM_STATE), to_seq(xw, q), STATE_SEQS)
    zoff = jnp.transpose(zoff[:, :, :nt], (2, 0, 1, 3)).reshape(nt * nb, D_SSM)
    h_s, hn2_s, route_s = _sample_back(xs_slab, ypart, ea, zoff, z, ysg, w)

    h, hn2, route, convt_p, ssm_p = _prompt_mixer(xp, w, PROMPT_TILE, h_s, hn2_s, route_s)

    n_tok = h.shape[0]
    bm = MOE_BLOCK_ROWS
    n_blocks = -(-(2 * n_tok) // bm) + N_EXPERTS
    tok, dst, blk_expert, n_used = _moe_plan(route[:, :2].astype(jnp.int32), bm, n_blocks)
    parts = _expert_mlp(blk_expert, n_used, tok, dst, hn2, w_gate[l], w_up[l], w_down[l], bm,
                        2 * n_tok + bm)
    outs = dict(
        h=h, parts=parts, route=route,
        conv_p=convt_p[:, SUBLANES - (CONV_K - 1):],
        ssm_p=ssm_p.reshape(bp, SSM_GROUPS, HEADS_PER_GROUP, SSM_HEAD_DIM, SSM_STATE),
        conv_s=jnp.transpose(convn, (1, 0, 2)),
        ssm_s=ssm_s.reshape(nb, SSM_GROUPS, HEADS_PER_GROUP, SSM_HEAD_DIM, SSM_STATE),
        v_s=jnp.transpose(v.reshape(nt, nb, D_SG), (1, 0, 2)),
    )
    return outs


def kernel(x_prompt, x_sample, state_conv, state_ssm, norm_mix_w, w_in, conv_w, conv_b, dt_bias, a_log, d_skip, ssm_norm_w, sg_norm_w, sg_w, sg_b, w_out, norm_ffn_w, w_router_group, b_router_group, w_router_expert, b_router_expert, w_gate, w_up, w_down, norm_final_w):
    depth = w_in.shape[0]
    assert depth == 1, "the fused final norm assumes a single layer"
    bp, seq, _ = x_prompt.shape
    nb, nt, _ = x_sample.shape
    l = 0
    w = _layer_weights(l, norm_mix_w, w_in, conv_w, conv_b, dt_bias, a_log, d_skip, ssm_norm_w, sg_norm_w,
                       sg_w, sg_b, w_out, norm_ffn_w, w_router_group, b_router_group, w_router_expert,
                       b_router_expert, nt)
    xs_slab = jnp.transpose(x_sample, (1, 0, 2)).reshape(nt * nb, D_MODEL)
    outs = _layer(l, x_prompt, xs_slab, state_conv, state_ssm, w, w_gate, w_up, w_down)
    y_p, y_s = _final(outs['h'], outs['parts'], outs['route'], norm_final_w.reshape(1, -1).astype(F32),
                      FINAL_TILE, bp * seq)
    y_prompt = y_p.reshape(bp, seq, D_MODEL)
    y_sample = jnp.transpose(y_s.reshape(nt, nb, D_MODEL), (1, 0, 2))
    return (y_prompt, y_sample, outs['conv_p'][None], outs['ssm_p'][None], outs['conv_s'][None],
            outs['ssm_s'][None], outs['v_s'][None])
```

```python
import functools

import jax
import jax.numpy as jnp
from jax import lax
from jax.experimental import pallas as pl
from jax.experimental.pallas import tpu as pltpu

D_MODEL = 1024
D_SSM = 1024
SSM_HEAD_DIM = 64
SSM_HEADS = 16
SSM_GROUPS = 2
HEADS_PER_GROUP = 8
SSM_STATE = 128
CONV_K = 4
C_CONV = D_SSM + 2 * SSM_GROUPS * SSM_STATE
D_SG = 1024
SG_HEADS = 8
SG_HEAD_DIM = 128
CHUNK = 128
N_EXPERT_GROUPS = 4
EXPERTS_PER_GROUP = 8
N_EXPERTS = 32
D_FF = 512
EPS = 1e-6

LANES = 128
SUBLANES = 8
VMEM_LIMIT = 56 * 1024 * 1024

F32 = jnp.float32
BF16 = jnp.bfloat16
HIGHEST = lax.Precision.HIGHEST


def _dot(a, b):
    return jnp.dot(a, b, preferred_element_type=F32)


def _dot_f32(a, b):
    return jnp.dot(a, b, preferred_element_type=F32, precision=HIGHEST)


def _rms(x, w):
    return x * lax.rsqrt(jnp.mean(x * x, axis=-1, keepdims=True) + EPS) * w


def _silu(x):
    return x * (1.0 / (1.0 + jnp.exp(-x)))


def _gelu(x):
    return 0.5 * x * (1.0 + lax.erf(x * 0.7071067811865476))


def _softplus(x):
    return jnp.maximum(x, 0.0) + jnp.log1p(jnp.exp(-jnp.abs(x)))


def _gated_group_norm(y, z, w):
    g = y * _silu(z)
    half = D_SSM // SSM_GROUPS
    parts = []
    for k in range(SSM_GROUPS):
        gk = g[:, k * half:(k + 1) * half]
        parts.append(gk * lax.rsqrt(jnp.mean(gk * gk, axis=-1, keepdims=True) + EPS))
    return jnp.concatenate(parts, axis=1) * w


ROUTER_GROUP_ROW = 0
ROUTER_EXPERT_ROW = SUBLANES


def _route_t(lt):
    r = lt.shape[1]
    row = lax.broadcasted_iota(jnp.int32, (SUBLANES, r), 0)
    rowf = row.astype(F32)
    big = float(SUBLANES)
    red = lambda f, a: f(a, axis=0, keepdims=True)
    gl = jnp.where(row < N_EXPERT_GROUPS, lt[ROUTER_GROUP_ROW:ROUTER_GROUP_ROW + SUBLANES], -jnp.inf)
    ge = jnp.exp(gl - red(jnp.max, gl))
    p_grp = ge / red(jnp.sum, ge)
    g_p = red(jnp.max, p_grp)
    g_idx = red(jnp.min, jnp.where(p_grp == g_p, rowf, big))
    el = lt[ROUTER_EXPERT_ROW:ROUTER_EXPERT_ROW + EXPERTS_PER_GROUP]
    for g in range(1, N_EXPERT_GROUPS):
        lo = ROUTER_EXPERT_ROW + g * EXPERTS_PER_GROUP
        el = jnp.where(g_idx == float(g), lt[lo:lo + EXPERTS_PER_GROUP], el)
    ee = jnp.exp(el - red(jnp.max, el))
    pe = ee / red(jnp.sum, ee)
    v1 = red(jnp.max, pe)
    i1 = red(jnp.min, jnp.where(pe == v1, rowf, big))
    rest = rowf != i1
    pe2 = jnp.where(rest, pe, -1.0)
    v2 = red(jnp.max, pe2)
    i2 = red(jnp.min, jnp.where(rest & (pe2 == v2), rowf, big))
    den = v1 + v2
    base = g_idx * float(EXPERTS_PER_GROUP)
    out = jnp.where(row == 0, base + i1, 0.0)
    out = jnp.where(row == 1, base + i2, out)
    out = jnp.where(row == 2, g_p * v1 / den, out)
    out = jnp.where(row == 3, g_p * v2 / den, out)
    return jnp.concatenate([out, jnp.zeros((LANES - SUBLANES, r), F32)], axis=0).T


def _ffn_front(h, nfw, wrt_ref, brc):
    hn2 = _rms(h, nfw)
    hi = hn2.astype(BF16)
    lo = (hn2 - hi.astype(F32)).astype(BF16)
    nt = lambda a, b: lax.dot_general(a, b, (((1,), (1,)), ((), ())), preferred_element_type=F32)
    lt = nt(wrt_ref[0], hi) + nt(wrt_ref[0], lo) + nt(wrt_ref[1], hi) + brc
    return hn2, _route_t(lt)


ROW_TILES = D_MODEL // LANES


def _store_row_tiles(ref, val):
    r = val.shape[0]
    for k in range(ROW_TILES):
        ref[pl.ds(k, r, stride=ROW_TILES), :] = val[:, k * LANES:(k + 1) * LANES]


def _load_row_tiles(ref):
    r = ref.shape[0] // ROW_TILES
    return jnp.concatenate([ref[pl.ds(k, r, stride=ROW_TILES), :] for k in range(ROW_TILES)], axis=1)


def _prompt_step(
        t, nt, x_ref, nmw_ref, wz_ref, wxbc_ref, wdt_ref, wuv_ref, convw_ref, convb_ref, dtb_ref,
        alog_ref, dskip_ref, ssmnw_ref, sgnw_ref, sgw_ref, sgb_ref, wout_ref, nfw_ref,
        wr_ref, br_ref, h_ref, hn2_ref, route_ref, convt_ref, ssm_ref,
        xbc_scr, state_scr, y_scr, mix_scr, tl):
    @pl.when(t == 0)
    def _():
        xbc_scr[0:SUBLANES, :] = jnp.zeros((SUBLANES, C_CONV), F32)
        state_scr[...] = jnp.zeros(state_scr.shape, F32)

    x = x_ref[...]
    hn = _rms(x, nmw_ref[...]).astype(BF16)
    z = _dot(hn, wz_ref[...])
    xbc = _dot(hn, wxbc_ref[...])
    dtr = _dot(hn, wdt_ref[...])
    uv = _gelu(_dot(hn, wuv_ref[...]))

    xbc_scr[SUBLANES:SUBLANES + tl, :] = xbc
    conv = convb_ref[...]
    for k in range(CONV_K):
        off = SUBLANES - (CONV_K - 1) + k
        conv = conv + convw_ref[k:k + 1, :] * xbc_scr[off:off + tl, :]
    tail = xbc_scr[tl:tl + SUBLANES, :]
    xbc_scr[0:SUBLANES, :] = tail
    convt_ref[...] = tail
    act = _silu(conv)
    xs = act[:, :D_SSM]
    bmat = act[:, D_SSM:D_SSM + SSM_GROUPS * SSM_STATE]
    cmat = act[:, D_SSM + SSM_GROUPS * SSM_STATE:]

    dt = _softplus(dtr + dtb_ref[...])
    a_row = -jnp.exp(alog_ref[...])

    li = lax.broadcasted_iota(jnp.int32, (CHUNK, CHUNK), 0)
    si = lax.broadcasted_iota(jnp.int32, (CHUNK, CHUNK), 1)
    causal = li >= si
    tri = jnp.where(causal, 1.0, 0.0).astype(F32)
    lo_half = si < SSM_HEAD_DIM

    for c in range(tl // CHUNK):
        rows = slice(c * CHUNK, (c + 1) * CHUNK)
        dt_c = dt[rows]
        acum = _dot_f32(tri, dt_c * a_row)
        acum_t = acum.T
        dt_t = dt_c.T
        w_t = dt_t * jnp.exp(acum_t[:, CHUNK - 1:CHUNK] - acum_t)
        xs_c = xs[rows]
        for g in range(SSM_GROUPS):
            gl = slice(g * SSM_STATE, (g + 1) * SSM_STATE)
            b_g = bmat[rows, gl]
            c_g = cmat[rows, gl]
            b_gt = b_g.T
            cb = _dot(c_g.astype(BF16), b_gt.astype(BF16))
            for j in range(HEADS_PER_GROUP // 2):
                pl_ = slice(g * 512 + j * LANES, g * 512 + (j + 1) * LANES)
                m_l, ec_l, s_l, ea_last = [], [], [], []
                for k in range(2):
                    hh = g * HEADS_PER_GROUP + 2 * j + k
                    colb = jnp.broadcast_to(acum[:, hh:hh + 1], (CHUNK, CHUNK))
                    rowb = jnp.broadcast_to(acum_t[hh:hh + 1, :], (CHUNK, CHUNK))
                    ea = jnp.exp(colb)
                    seg = jnp.where(causal, colb - rowb, 0.0)
                    lmat = jnp.where(causal, jnp.exp(seg), 0.0)
                    m_l.append(cb * lmat * jnp.broadcast_to(dt_t[hh:hh + 1, :], (CHUNK, CHUNK)))
                    ec_l.append(c_g * ea)
                    s_l.append(b_gt * jnp.broadcast_to(w_t[hh:hh + 1, :], (CHUNK, CHUNK)))
                    ea_last.append(ea[CHUNK - 1:CHUNK, :])
                xs_p = xs_c[:, pl_]
                st_p = state_scr[g, :, j * LANES:(j + 1) * LANES]
                rx = jnp.concatenate([jnp.where(lo_half, xs_p, 0.0),
                                      jnp.where(lo_half, 0.0, xs_p)], axis=0).astype(BF16)
                rs = jnp.concatenate([jnp.where(lo_half, st_p, 0.0),
                                      jnp.where(lo_half, 0.0, st_p)], axis=0).astype(BF16)
                lhs = jnp.concatenate(m_l + ec_l, axis=1).astype(BF16)
                y_p = _dot(lhs, jnp.concatenate([rx, rs], axis=0))
                y_scr[rows, pl_] = y_p
                dec = jnp.where(lo_half[0:1, :], ea_last[0], ea_last[1])
                upd = _dot(jnp.concatenate(s_l, axis=1).astype(BF16), rx)
                state_scr[g, :, j * LANES:(j + 1) * LANES] = st_p * dec + upd

    y = y_scr[...] + dskip_ref[...] * xs
    y_ssm = _gated_group_norm(y, z, ssmnw_ref[...])
    mix_scr[:, :D_SSM] = y_ssm.astype(BF16)

    u = uv[:, :D_SG]
    v = _rms(uv[:, D_SG:], sgnw_ref[...])
    vb = v.astype(BF16)
    for c in range(tl // CHUNK):
        rows = slice(c * CHUNK, (c + 1) * CHUNK)
        for hd in range(SG_HEADS):
            hl = slice(hd * SG_HEAD_DIM, (hd + 1) * SG_HEAD_DIM)
            sv = _dot(sgw_ref[hd], vb[rows, hl]) + sgb_ref[:, hl]
            mix_scr[rows, D_SSM + hd * SG_HEAD_DIM:D_SSM + (hd + 1) * SG_HEAD_DIM] = (
                u[rows, hl] * sv).astype(BF16)

    h = x + _dot(mix_scr[...], wout_ref[...])
    h_ref[...] = h
    hn2, route = _ffn_front(h, nfw_ref[...], wr_ref, br_ref[...])
    _store_row_tiles(hn2_ref, hn2)
    route_ref[...] = route

    @pl.when(t == nt - 1)
    def _():
        for g in range(SSM_GROUPS):
            ssm_ref[g] = state_scr[g].T


def _prompt_mixer_kernel(
        x_ref, nmw_ref, wz_ref, wxbc_ref, wdt_ref, wuv_ref, convw_ref, convb_ref, dtb_ref,
        alog_ref, dskip_ref, ssmnw_ref, sgnw_ref, sgw_ref, sgb_ref, wout_ref, nfw_ref,
        wr_ref, br_ref, hs_ref, hn2s_ref, routes_ref,
        h_ref, hn2_ref, route_ref, convt_ref, ssm_ref,
        xbc_scr, state_scr, y_scr, mix_scr, *, tl, nt, n_main):
    i = pl.program_id(0)

    @pl.when(i >= n_main)
    def _():
        h_ref[...] = hs_ref[...]
        hn2_ref[...] = hn2s_ref[...]
        route_ref[...] = routes_ref[...]

    @pl.when(i < n_main)
    def _():
        _prompt_step(
            lax.rem(i, nt), nt, x_ref, nmw_ref, wz_ref, wxbc_ref, wdt_ref, wuv_ref, convw_ref,
            convb_ref, dtb_ref, alog_ref, dskip_ref, ssmnw_ref, sgnw_ref, sgw_ref, sgb_ref, wout_ref,
            nfw_ref, wr_ref, br_ref, h_ref, hn2_ref, route_ref, convt_ref, ssm_ref,
            xbc_scr, state_scr, y_scr, mix_scr, tl)


def _const_spec(shape):
    zeros = (0,) * len(shape)
    return pl.BlockSpec(shape, lambda *_: zeros)


def _prompt_mixer(x, w, tl, h_s, hn2_s, route_s):
    bsz, seq, _ = x.shape
    nt = seq // tl
    n_main = bsz * nt
    n_s = h_s.shape[0]
    assert seq % tl == 0 and n_s % tl == 0
    n_tok = bsz * seq + n_s
    main = lambda i: jnp.minimum(i, n_main - 1)
    tail = lambda i: jnp.maximum(i - n_main, 0)
    weights = [w['nmw'], w['wz'], w['wxbc'], w['wdt'], w['wuv'], w['convw'], w['convb'], w['dtb'],
               w['alog'], w['dskip'], w['ssmnw'], w['sgnw'], w['sgw'], w['sgb'], w['wout'], w['nfw'],
               w['wr'], w['br']]
    in_specs = ([pl.BlockSpec((tl, D_MODEL), lambda i: (main(i), 0))]
                + [_const_spec(a.shape) for a in weights]
                + [pl.BlockSpec(blk, lambda i: (tail(i), 0))
                   for blk in ((tl, D_MODEL), (tl * ROW_TILES, LANES), (tl, LANES))])
    q = HEADS_PER_GROUP * SSM_HEAD_DIM
    out_shape = [
        jax.ShapeDtypeStruct((n_tok, D_MODEL), F32),
        jax.ShapeDtypeStruct((n_tok * ROW_TILES, LANES), F32),
        jax.ShapeDtypeStruct((n_tok, LANES), F32),
        jax.ShapeDtypeStruct((bsz, SUBLANES, C_CONV), F32),
        jax.ShapeDtypeStruct((bsz, SSM_GROUPS, q, SSM_STATE), F32),
    ]
    out_specs = [
        pl.BlockSpec((tl, D_MODEL), lambda i: (i, 0)),
        pl.BlockSpec((tl * ROW_TILES, LANES), lambda i: (i, 0)),
        pl.BlockSpec((tl, LANES), lambda i: (i, 0)),
        pl.BlockSpec((None, SUBLANES, C_CONV), lambda i: (main(i) // nt, 0, 0)),
        pl.BlockSpec((None, SSM_GROUPS, q, SSM_STATE), lambda i: (main(i) // nt, 0, 0, 0)),
    ]
    scratch = [
        pltpu.VMEM((tl + SUBLANES, C_CONV), F32),
        pltpu.VMEM((SSM_GROUPS, SSM_STATE, q), F32),
        pltpu.VMEM((tl, D_SSM), F32),
        pltpu.VMEM((tl, D_SSM + D_SG), BF16),
    ]
    return pl.pallas_call(
        functools.partial(_prompt_mixer_kernel, tl=tl, nt=nt, n_main=n_main),
        grid=(n_main + n_s // tl,),
        in_specs=in_specs,
        out_specs=out_specs,
        out_shape=out_shape,
        scratch_shapes=scratch,
        compiler_params=pltpu.CompilerParams(
            dimension_semantics=("arbitrary",), vmem_limit_bytes=VMEM_LIMIT),
        name="prompt_mixer",
    )(x.reshape(bsz * seq, D_MODEL), *weights, h_s, hn2_s, route_s)


def _sample_front_kernel(
        x_ref, cprev_ref, nmw_ref, wz_ref, wxbc_ref, wdt_ref, wuv_ref, convw_ref, convb_ref,
        dtb_ref, alog_ref, dskip_ref, sgnw_ref, w4_ref, b4_ref, expand_ref,
        z_ref, ypart_ref, ea_ref, ysg_ref, v_ref, convn_ref, c_ref, b_ref, xw_ref, dec_ref,
        *, nb, nt):
    x = x_ref[...]
    hn = _rms(x, nmw_ref[...]).astype(BF16)
    z_ref[...] = _dot(hn, wz_ref[...])
    xbc = _dot(hn, wxbc_ref[...])
    dtr = _dot(hn, wdt_ref[...])
    uv = _gelu(_dot(hn, wuv_ref[...]))

    slab = lambda a, t: a[t * nb:(t + 1) * nb]
    full = [cprev_ref[k] for k in range(CONV_K - 1)] + [slab(xbc, t) for t in range(nt)]
    for k in range(CONV_K - 1):
        convn_ref[k] = full[nt + k]
    xs, bm, cm = [], [], []
    for t in range(nt):
        conv = convb_ref[...]
        for k in range(CONV_K):
            conv = conv + convw_ref[k:k + 1, :] * full[t + k]
        act = _silu(conv)
        xs.append(act[:, :D_SSM])
        bm.append(act[:, D_SSM:D_SSM + SSM_GROUPS * SSM_STATE])
        cm.append(act[:, D_SSM + SSM_GROUPS * SSM_STATE:])
        b_ref[t * nb:(t + 1) * nb, :] = bm[t]
        c_ref[t * nb:(t + 1) * nb, :] = cm[t]

    dt = _softplus(dtr + dtb_ref[...])
    a_row = -jnp.exp(alog_ref[...])
    dts = [slab(dt, t) for t in range(nt)]
    cum = []
    for t in range(nt):
        da = dts[t] * a_row
        cum.append(da if t == 0 else cum[t - 1] + da)
    dec_ref[...] = jnp.exp(cum[nt - 1])

    lane = lax.broadcasted_iota(jnp.int32, (nb, LANES), 1)
    first_group = lane < HEADS_PER_GROUP
    facs = [jnp.exp(cum[t]) for t in range(nt)]
    facs += [dts[s] * jnp.exp(cum[nt - 1] - cum[s]) for s in range(nt)]
    pairs = []
    for t in range(nt):
        for s in range(t + 1):
            cb = []
            for g in range(SSM_GROUPS):
                gl = slice(g * SSM_STATE, (g + 1) * SSM_STATE)
                cb.append(jnp.sum(cm[t][:, gl] * bm[s][:, gl], axis=1, keepdims=True))
            cbh = jnp.where(first_group, cb[0], cb[1])
            facs.append(jnp.exp(cum[t] - cum[s]) * dts[s] * cbh)
            pairs.append((t, s))
    fx = _dot_f32(jnp.concatenate(facs, axis=0), expand_ref[...])
    fslab = lambda i: fx[i * nb:(i + 1) * nb]
    for t in range(nt):
        ea_ref[t * nb:(t + 1) * nb, :] = fslab(t)
        xw_ref[t * nb:(t + 1) * nb, :] = xs[t] * fslab(nt + t)
    for t in range(nt):
        acc = dskip_ref[...] * xs[t]
        for i, (tt, s) in enumerate(pairs):
            if tt == t:
                acc = acc + fslab(2 * nt + i) * xs[s]
        ypart_ref[t * nb:(t + 1) * nb, :] = acc

    u = uv[:, :D_SG]
    v = _rms(uv[:, D_SG:], sgnw_ref[...])
    v_ref[...] = v
    for t in range(nt):
        sv = b4_ref[t:t + 1, :]
        for s in range(t + 1):
            sv = sv + w4_ref[t, s:s + 1, :] * slab(v, s)
        ysg_ref[t * nb:(t + 1) * nb, :] = slab(u, t) * sv


def _sample_front(x_slab, cprev, w, nb, nt):
    r = nb * nt
    weights = [w['nmw'], w['wz'], w['wxbc'], w['wdt'], w['wuv'], w['convw'], w['convb'], w['dtb'],
               w['alog'], w['dskip'], w['sgnw'], w['w4'], w['b4'], w['expand']]
    ins = [x_slab, cprev] + weights
    out_shape = [
        jax.ShapeDtypeStruct((r, D_SSM), F32),
        jax.ShapeDtypeStruct((r, D_SSM), F32),
        jax.ShapeDtypeStruct((r, D_SSM), F32),
        jax.ShapeDtypeStruct((r, D_SG), F32),
        jax.ShapeDtypeStruct((r, D_SG), F32),
        jax.ShapeDtypeStruct((CONV_K - 1, nb, C_CONV), F32),
        jax.ShapeDtypeStruct((r, SSM_GROUPS * SSM_STATE), F32),
        jax.ShapeDtypeStruct((r, SSM_GROUPS * SSM_STATE), F32),
        jax.ShapeDtypeStruct((r, D_SSM), F32),
        jax.ShapeDtypeStruct((nb, LANES), F32),
    ]
    return pl.pallas_call(
        functools.partial(_sample_front_kernel, nb=nb, nt=nt),
        grid=(1,),
        in_specs=[_const_spec(a.shape) for a in ins],
        out_specs=[_const_spec(s.shape) for s in out_shape],
        out_shape=out_shape,
        compiler_params=pltpu.CompilerParams(
            dimension_semantics=("arbitrary",), vmem_limit_bytes=VMEM_LIMIT),
        name="sample_front",
    )(*ins)


def _sample_state_kernel(dec_ref, st_ref, cq_ref, bq_ref, xw_ref, so_ref, z_ref, *, seqs):
    for bb in range(seqs):
        for g in range(SSM_GROUPS):
            s0 = st_ref[bb, g]
            z_ref[bb, g] = lax.dot_general(
                cq_ref[bb, g].astype(BF16), s0.astype(BF16), (((1,), (1,)), ((), ())),
                preferred_element_type=F32)
            upd = lax.dot_general(
                xw_ref[bb, g].astype(BF16), bq_ref[bb, g].astype(BF16), (((0,), (0,)), ((), ())),
                preferred_element_type=F32)
            for hh in range(HEADS_PER_GROUP):
                hs = slice(hh * SSM_HEAD_DIM, (hh + 1) * SSM_HEAD_DIM)
                so_ref[bb, g, hs, :] = s0[hs] * dec_ref[bb, g * HEADS_PER_GROUP + hh] + upd[hs]


def _sample_state(dec, state, cq, bq, xwq, seqs):
    nb = state.shape[0]
    q = HEADS_PER_GROUP * SSM_HEAD_DIM
    blk = lambda *tail: pl.BlockSpec((seqs, SSM_GROUPS) + tail, lambda i: (i, 0, 0, 0))
    return pl.pallas_call(
        functools.partial(_sample_state_kernel, seqs=seqs),
        grid=(nb // seqs,),
        in_specs=[
            pl.BlockSpec((seqs, SSM_HEADS), lambda i: (i, 0), memory_space=pltpu.SMEM),
            blk(q, SSM_STATE), blk(SUBLANES, SSM_STATE), blk(SUBLANES, SSM_STATE), blk(SUBLANES, q),
        ],
        out_specs=[blk(q, SSM_STATE), blk(SUBLANES, q)],
        out_shape=[
            jax.ShapeDtypeStruct((nb, SSM_GROUPS, q, SSM_STATE), F32),
            jax.ShapeDtypeStruct((nb, SSM_GROUPS, SUBLANES, q), F32),
        ],
        compiler_params=pltpu.CompilerParams(
            dimension_semantics=("arbitrary",), vmem_limit_bytes=VMEM_LIMIT),
        name="sample_state",
    )(dec, state, cq, bq, xwq)


def _sample_back_kernel(x_ref, ypart_ref, ea_ref, zoff_ref, z_ref, ysg_ref, ssmnw_ref, wout_ref,
                        nfw_ref, wr_ref, br_ref, h_ref, hn2_ref, route_ref):
    y = ypart_ref[...] + ea_ref[...] * zoff_ref[...]
    y_ssm = _gated_group_norm(y, z_ref[...], ssmnw_ref[...])
    mix_in = jnp.concatenate([y_ssm.astype(BF16), ysg_ref[...].astype(BF16)], axis=1)
    h = x_ref[...] + _dot(mix_in, wout_ref[...])
    h_ref[...] = h
    hn2, route = _ffn_front(h, nfw_ref[...], wr_ref, br_ref[...])
    _store_row_tiles(hn2_ref, hn2)
    route_ref[...] = route


def _sample_back(x_slab, ypart, ea, zoff, z, ysg, w):
    r = x_slab.shape[0]
    ins = [x_slab, ypart, ea, zoff, z, ysg, w['ssmnw'], w['wout'], w['nfw'], w['wr'], w['br']]
    out_shape = [
        jax.ShapeDtypeStruct((r, D_MODEL), F32),
        jax.ShapeDtypeStruct((r * ROW_TILES, LANES), F32),
        jax.ShapeDtypeStruct((r, LANES), F32),
    ]
    return pl.pallas_call(
        _sample_back_kernel,
        grid=(1,),
        in_specs=[_const_spec(a.shape) for a in ins],
        out_specs=[_const_spec(s.shape) for s in out_shape],
        out_shape=out_shape,
        compiler_params=pltpu.CompilerParams(
            dimension_semantics=("arbitrary",), vmem_limit_bytes=VMEM_LIMIT),
        name="sample_back",
    )(*ins)


EXPERT_PIECES = 8
DMA_PIECES = 4


def _expert_mlp_kernel(be_ref, nu_ref, tok0_ref, tokn_ref, dstp_ref, x_hbm, wg_ref, wu_ref, wd_ref,
                       parts_hbm, xbuf, obuf, xb, hb, wg_b, wu_b, wd_b, gsem, ssem, *, bm):
    j = pl.program_id(0)
    nu = nu_ref[0]
    nt = ROW_TILES
    tile = lambda ref, start: ref.at[pl.ds(pl.multiple_of(start, nt), nt)]

    def gather_copy(idx_ref, s, r):
        return pltpu.make_async_copy(tile(x_hbm, idx_ref[0, r]), xbuf.at[s, pl.ds(r * nt, nt)], gsem.at[s])

    def scatter_copy(s, r):
        return pltpu.make_async_copy(obuf.at[s, pl.ds(r * nt, nt)], tile(parts_hbm, dstp_ref[0, r]),
                                     ssem.at[s])

    def gather_wait(s):
        pltpu.make_async_copy(x_hbm.at[pl.ds(0, bm * nt)], xbuf.at[s], gsem.at[s]).wait()

    def scatter_wait(s):
        pltpu.make_async_copy(obuf.at[s], parts_hbm.at[pl.ds(0, bm * nt)], ssem.at[s]).wait()

    def step(slot):
        gather_wait(slot)

        @pl.when(j >= 1)
        def _():
            scatter_wait(slot)

        per = bm // DMA_PIECES

        def start_rows(piece):
            if piece >= DMA_PIECES:
                return
            for r in range(piece * per, (piece + 1) * per):
                gather_copy(tokn_ref, 1 - slot, r).start(priority=r % 2)
                scatter_copy(1 - slot, r).start(priority=(r + 1) % 2)

        xb[...] = _load_row_tiles(xbuf.at[slot]).astype(BF16)
        half = D_FF // 2
        for c in range(2):
            cols = slice(c * half, (c + 1) * half)
            start_rows(2 * c)
            g = _dot(xb[...], wg_b[:, cols])
            start_rows(2 * c + 1)
            u = _dot(xb[...], wu_b[:, cols])
            hb[:, cols] = (_silu(g) * u).astype(BF16)
        quarter = D_MODEL // 4
        for c in range(4):
            start_rows(4 + c)
            o = _dot(hb[...], wd_b[:, c * quarter:(c + 1) * quarter])
            for k in range(quarter // LANES):
                kk = c * (quarter // LANES) + k
                obuf.at[slot][pl.ds(kk, bm, stride=nt), :] = o[:, k * LANES:(k + 1) * LANES]

    def drain(slot):
        gather_wait(slot)
        scatter_wait(slot)
        for r in range(bm):
            scatter_copy(1 - slot, r).start(priority=r % 2)
        scatter_wait(1 - slot)

    @pl.when(j == 0)
    def _():
        obuf[...] = jnp.zeros(obuf.shape, F32)
        n_real = parts_hbm.shape[0] - bm * nt
        spare = pltpu.make_async_copy(obuf.at[0], parts_hbm.at[pl.ds(n_real, bm * nt)], ssem.at[0])
        spare.start()
        spare.wait()

        def start0(r, carry):
            pltpu.make_async_copy(tile(x_hbm, tok0_ref[0, r]), xbuf.at[0, pl.ds(pl.multiple_of(r * nt, nt), nt)],
                                  gsem.at[0]).start()
            return carry
        lax.fori_loop(0, bm, start0, 0)

    used = j < nu

    @pl.when(used & ((j == 0) | (be_ref[j] != be_ref[jnp.maximum(j - 1, 0)])))
    def _():
        wg_b[...] = wg_ref[...].astype(BF16)
        wu_b[...] = wu_ref[...].astype(BF16)
        wd_b[...] = wd_ref[...].astype(BF16)

    for slot in range(2):
        parity = lax.rem(j, 2) == slot

        @pl.when(used & parity)
        def _(slot=slot):
            step(slot)

        @pl.when((j == nu) & parity)
        def _(slot=slot):
            drain(slot)


def _expert_mlp(blk_expert, n_used, tok, dst, x, wg, wu, wd, bm, n_out):
    n_blocks = blk_expert.shape[0]
    assert bm % EXPERT_PIECES == 0
    w_map = lambda j, be, nu: (be[j], 0, 0)
    idx_spec = lambda f: pl.BlockSpec((None, 1, bm), lambda j, be, nu: (f(j), 0, 0),
                                      memory_space=pltpu.SMEM)
    any_spec = pl.BlockSpec(memory_space=pl.ANY)
    grid_spec = pltpu.PrefetchScalarGridSpec(
        num_scalar_prefetch=2,
        grid=(n_blocks + 1,),
        in_specs=[
            idx_spec(lambda j: 0),
            idx_spec(lambda j: jnp.minimum(j + 1, n_blocks - 1)),
            idx_spec(lambda j: j),
            any_spec,
            pl.BlockSpec((None, D_MODEL, D_FF), w_map),
            pl.BlockSpec((None, D_MODEL, D_FF), w_map),
            pl.BlockSpec((None, D_FF, D_MODEL), w_map),
        ],
        out_specs=any_spec,
        scratch_shapes=[
            pltpu.VMEM((2, bm * ROW_TILES, LANES), F32),
            pltpu.VMEM((2, bm * ROW_TILES, LANES), F32),
            pltpu.VMEM((bm, D_MODEL), BF16),
            pltpu.VMEM((bm, D_FF), BF16),
            pltpu.VMEM((D_MODEL, D_FF), BF16),
            pltpu.VMEM((D_MODEL, D_FF), BF16),
            pltpu.VMEM((D_FF, D_MODEL), BF16),
            pltpu.SemaphoreType.DMA((2,)),
            pltpu.SemaphoreType.DMA((2,)),
        ],
    )
    tok3 = (tok * ROW_TILES).reshape(n_blocks, 1, bm)
    spare = (n_out - bm + jnp.arange(bm, dtype=jnp.int32)).reshape(1, bm)
    dst_prev = (jnp.concatenate([spare, dst], axis=0) * ROW_TILES).reshape(n_blocks + 1, 1, bm)
    blk_expert = jnp.concatenate([blk_expert, blk_expert[-1:]])
    return pl.pallas_call(
        functools.partial(_expert_mlp_kernel, bm=bm),
        grid_spec=grid_spec,
        out_shape=jax.ShapeDtypeStruct((n_out * ROW_TILES, LANES), F32),
        compiler_params=pltpu.CompilerParams(
            dimension_semantics=("arbitrary",), vmem_limit_bytes=VMEM_LIMIT),
        name="expert_mlp",
    )(blk_expert, n_used, tok3, tok3, dst_prev, x, wg, wu, wd)


def _moe_plan(e_idx, bm, n_blocks):
    n_tok = e_idx.shape[0]
    n_asg = 2 * n_tok
    n_pad = n_blocks * bm - n_asg
    e_flat = e_idx.reshape(-1)
    experts = jnp.arange(N_EXPERTS, dtype=jnp.int32)
    counts = jnp.sum((e_flat[:, None] == experts[None, :]).astype(jnp.int32), axis=0)
    nblk = (counts + bm - 1) // bm
    blk_end = jnp.cumsum(nblk)
    n_used = blk_end[-1]
    j = jnp.arange(n_blocks, dtype=jnp.int32)
    be = jnp.sum((blk_end[None, :] <= jnp.minimum(j, n_used - 1)[:, None]).astype(jnp.int32), axis=1)
    be = jnp.minimum(be, N_EXPERTS - 1)
    pad_end = jnp.cumsum(nblk * bm - counts)
    pad_expert = jnp.sum((pad_end[None, :] <= jnp.arange(n_pad, dtype=jnp.int32)[:, None]).astype(jnp.int32),
                         axis=1)
    keys = jnp.concatenate([2 * e_flat, 2 * pad_expert + 1])
    ids = jnp.concatenate([jnp.arange(n_asg, dtype=jnp.int32), jnp.full((n_pad,), -1, jnp.int32)])
    _, window = lax.sort((keys, ids), num_keys=1, is_stable=True)
    window = window.reshape(n_blocks, bm)
    valid = window >= 0
    q = jnp.arange(bm, dtype=jnp.int32)
    tok = jnp.where(valid, window // 2, (j[:, None] * bm + q[None, :]) % n_tok)
    dst = jnp.where(valid, (window % 2) * n_tok + window // 2,
                    2 * n_tok + q[None, :])
    return tok.astype(jnp.int32), dst.astype(jnp.int32), be, n_used.astype(jnp.int32).reshape(1)


def _final_kernel(h_ref, p0_ref, p1_ref, route_ref, w_ref, op_ref, os_ref, *, n_prompt_steps):
    route = route_ref[...]
    p0 = _load_row_tiles(p0_ref)
    p1 = _load_row_tiles(p1_ref)
    y = _rms(h_ref[...] + route[:, 2:3] * p0 + route[:, 3:4] * p1, w_ref[...])
    i = pl.program_id(0)

    @pl.when(i < n_prompt_steps)
    def _():
        op_ref[...] = y

    @pl.when(i >= n_prompt_steps)
    def _():
        os_ref[...] = y


def _final(h, parts, route, w, tf, n_prompt):
    n_tok = h.shape[0]
    assert n_tok % tf == 0 and n_prompt % tf == 0
    n_p = n_prompt // tf
    row = lambda d: pl.BlockSpec((tf, d), lambda i: (i, 0))
    part = lambda k: pl.BlockSpec((tf * ROW_TILES, LANES), lambda i: (k * (n_tok // tf) + i, 0))
    return pl.pallas_call(
        functools.partial(_final_kernel, n_prompt_steps=n_p),
        grid=(n_tok // tf,),
        in_specs=[row(D_MODEL), part(0), part(1), row(LANES), _const_spec(w.shape)],
        out_specs=[pl.BlockSpec((tf, D_MODEL), lambda i: (jnp.minimum(i, n_p - 1), 0)),
                   pl.BlockSpec((tf, D_MODEL), lambda i: (jnp.maximum(i - n_p, 0), 0))],
        out_shape=[jax.ShapeDtypeStruct((n_prompt, D_MODEL), F32),
                   jax.ShapeDtypeStruct((n_tok - n_prompt, D_MODEL), F32)],
        compiler_params=pltpu.CompilerParams(
            dimension_semantics=("arbitrary",), vmem_limit_bytes=VMEM_LIMIT),
        name="final_combine",
    )(h, parts, parts, route, w)


PROMPT_TILE = 256
MOE_BLOCK_ROWS = 256
FINAL_TILE = 512
STATE_SEQS = 8


def _pad_lanes(a, width=LANES):
    return jnp.pad(a, [(0, 0)] * (a.ndim - 1) + [(0, width - a.shape[-1])])


def _layer_weights(l, norm_mix_w, w_in, conv_w, conv_b, dt_bias, a_log, d_skip, ssm_norm_w, sg_norm_w,
                   sg_w, sg_b, w_out, norm_ffn_w, w_router_group, b_router_group, w_router_expert,
                   b_router_expert, n_dec):
    c0, c1, c2 = D_SSM, D_SSM + C_CONV, D_SSM + C_CONV + SSM_HEADS
    wi = w_in[l]
    causal = jnp.tril(jnp.ones((CHUNK, CHUNK), bool))
    sgw_l = jnp.where(causal, sg_w[l], 0.0)
    head_of_lane = jnp.arange(D_SSM, dtype=jnp.int32) // SSM_HEAD_DIM
    expand = (jnp.arange(LANES, dtype=jnp.int32)[:, None] == head_of_lane[None, :]).astype(F32)
    row = lambda a: a.reshape(1, -1).astype(F32)
    gap = ROUTER_EXPERT_ROW - N_EXPERT_GROUPS
    rest = LANES - ROUTER_EXPERT_ROW - N_EXPERTS
    wrt = jnp.concatenate([w_router_group[l].T, jnp.zeros((gap, D_MODEL), F32), w_router_expert[l].T,
                           jnp.zeros((rest, D_MODEL), F32)], axis=0).astype(F32)
    brc = jnp.concatenate([b_router_group[l], jnp.zeros((gap,), F32), b_router_expert[l],
                           jnp.zeros((rest,), F32)]).astype(F32).reshape(LANES, 1)
    return {
        'nmw': row(norm_mix_w[l]),
        'wz': wi[:, :c0].astype(BF16),
        'wxbc': wi[:, c0:c1].astype(BF16),
        'wdt': _pad_lanes(wi[:, c1:c2]).astype(BF16),
        'wuv': wi[:, c2:].astype(BF16),
        'convw': conv_w[l].astype(F32),
        'convb': row(conv_b[l]),
        'dtb': _pad_lanes(row(dt_bias[l])),
        'alog': _pad_lanes(row(a_log[l])),
        'dskip': row(jnp.repeat(d_skip[l], SSM_HEAD_DIM)),
        'ssmnw': row(ssm_norm_w[l]),
        'sgnw': row(sg_norm_w[l]),
        'sgw': sgw_l.astype(BF16),
        'sgb': jnp.repeat(sg_b[l].T, SG_HEAD_DIM, axis=1).astype(F32),
        'w4': jnp.repeat(jnp.transpose(sgw_l[:, :n_dec, :n_dec], (1, 2, 0)), SG_HEAD_DIM, axis=2).astype(F32),
        'b4': jnp.repeat(sg_b[l][:, :n_dec].T, SG_HEAD_DIM, axis=1).astype(F32),
        'expand': expand,
        'wout': w_out[l].astype(BF16),
        'nfw': row(norm_ffn_w[l]),
        'wr': jnp.stack([wrt.astype(BF16), (wrt - wrt.astype(BF16).astype(F32)).astype(BF16)]),
        'br': brc,
    }


def _layer(l, xp, xs_slab, state_conv, state_ssm, w, w_gate, w_up, w_down):
    bp, seq, _ = xp.shape
    nb, nt = state_conv.shape[1], xs_slab.shape[0] // state_conv.shape[1]
    q = HEADS_PER_GROUP * SSM_HEAD_DIM

    cprev = jnp.transpose(state_conv[l], (1, 0, 2))
    z, ypart, ea, ysg, v, convn, cmat, bmat, xw, dec = _sample_front(xs_slab, cprev, w, nb, nt)
    to_seq = lambda a, d: jnp.pad(
        jnp.transpose(a.reshape(nt, nb, SSM_GROUPS, d), (1, 2, 0, 3)),
        ((0, 0), (0, 0), (0, SUBLANES - nt), (0, 0)))
    ssm_s, zoff = _sample_state(
        dec[:, :SSM_HEADS], state_ssm[l].reshape(nb, SSM_GROUPS, q, SSM_STATE),
        to_seq(cmat, SSM_STATE), to_seq(bmat, SSM_STATE), to_seq(xw, q), STATE_SEQS)
    zoff = jnp.transpose(zoff[:, :, :nt], (2, 0, 1, 3)).reshape(nt * nb, D_SSM)
    h_s, hn2_s, route_s = _sample_back(xs_slab, ypart, ea, zoff, z, ysg, w)

    h, hn2, route, convt_p, ssm_p = _prompt_mixer(xp, w, PROMPT_TILE, h_s, hn2_s, route_s)

    n_tok = h.shape[0]
    bm = MOE_BLOCK_ROWS
    n_blocks = -(-(2 * n_tok) // bm) + N_EXPERTS
    tok, dst, blk_expert, n_used = _moe_plan(route[:, :2].astype(jnp.int32), bm, n_blocks)
    parts = _expert_mlp(blk_expert, n_used, tok, dst, hn2, w_gate[l], w_up[l], w_down[l], bm,
                        2 * n_tok + bm)
    outs = dict(
        h=h, parts=parts, route=route,
        conv_p=convt_p[:, SUBLANES - (CONV_K - 1):],
        ssm_p=ssm_p.reshape(bp, SSM_GROUPS, HEADS_PER_GROUP, SSM_HEAD_DIM, SSM_STATE),
        conv_s=jnp.transpose(convn, (1, 0, 2)),
        ssm_s=ssm_s.reshape(nb, SSM_GROUPS, HEADS_PER_GROUP, SSM_HEAD_DIM, SSM_STATE),
        v_s=jnp.transpose(v.reshape(nt, nb, D_SG), (1, 0, 2)),
    )
    return outs


def kernel(x_prompt, x_sample, state_conv, state_ssm, norm_mix_w, w_in, conv_w, conv_b, dt_bias, a_log, d_skip, ssm_norm_w, sg_norm_w, sg_w, sg_b, w_out, norm_ffn_w, w_router_group, b_router_group, w_router_expert, b_router_expert, w_gate, w_up, w_down, norm_final_w):
    depth = w_in.shape[0]
    assert depth == 1, "the fused final norm assumes a single layer"
    bp, seq, _ = x_prompt.shape
    nb, nt, _ = x_sample.shape
    l = 0
    w = _layer_weights(l, norm_mix_w, w_in, conv_w, conv_b, dt_bias, a_log, d_skip, ssm_norm_w, sg_norm_w,
                       sg_w, sg_b, w_out, norm_ffn_w, w_router_group, b_router_group, w_router_expert,
                       b_router_expert, nt)
    xs_slab = jnp.transpose(x_sample, (1, 0, 2)).reshape(nt * nb, D_MODEL)
    outs = _layer(l, x_prompt, xs_slab, state_conv, state_ssm, w, w_gate, w_up, w_down)
    y_p, y_s = _final(outs['h'], outs['parts'], outs['route'], norm_final_w.reshape(1, -1).astype(F32),
                      FINAL_TILE, bp * seq)
    y_prompt = y_p.reshape(bp, seq, D_MODEL)
    y_sample = jnp.transpose(y_s.reshape(nt, nb, D_MODEL), (1, 0, 2))
    return (y_prompt, y_sample, outs['conv_p'][None], outs['ssm_p'][None], outs['conv_s'][None],
            outs['ssm_s'][None], outs['v_s'][None])
```

```python
import functools

import jax
import jax.numpy as jnp
from jax import lax
from jax.experimental import pallas as pl
from jax.experimental.pallas import tpu as pltpu

D_MODEL = 1024
D_SSM = 1024
SSM_HEAD_DIM = 64
SSM_HEADS = 16
SSM_GROUPS = 2
HEADS_PER_GROUP = 8
SSM_STATE = 128
CONV_K = 4
C_CONV = D_SSM + 2 * SSM_GROUPS * SSM_STATE
D_SG = 1024
SG_HEADS = 8
SG_HEAD_DIM = 128
CHUNK = 128
N_EXPERT_GROUPS = 4
EXPERTS_PER_GROUP = 8
N_EXPERTS = 32
D_FF = 512
EPS = 1e-6

LANES = 128
SUBLANES = 8
VMEM_LIMIT = 56 * 1024 * 1024

F32 = jnp.float32
BF16 = jnp.bfloat16
HIGHEST = lax.Precision.HIGHEST


def _dot(a, b):
    return jnp.dot(a, b, preferred_element_type=F32)


def _dot_f32(a, b):
    return jnp.dot(a, b, preferred_element_type=F32, precision=HIGHEST)


def _rms(x, w):
    return x * lax.rsqrt(jnp.mean(x * x, axis=-1, keepdims=True) + EPS) * w


def _silu(x):
    return x * (1.0 / (1.0 + jnp.exp(-x)))


def _gelu(x):
    return 0.5 * x * (1.0 + lax.erf(x * 0.7071067811865476))


def _softplus(x):
    return jnp.maximum(x, 0.0) + jnp.log1p(jnp.exp(-jnp.abs(x)))


def _gated_group_norm(y, z, w):
    g = y * _silu(z)
    half = D_SSM // SSM_GROUPS
    parts = []
    for k in range(SSM_GROUPS):
        gk = g[:, k * half:(k + 1) * half]
        parts.append(gk * lax.rsqrt(jnp.mean(gk * gk, axis=-1, keepdims=True) + EPS))
    return jnp.concatenate(parts, axis=1) * w


ROUTER_GROUP_ROW = 0
ROUTER_EXPERT_ROW = SUBLANES


def _route_t(lt):
    r = lt.shape[1]
    row = lax.broadcasted_iota(jnp.int32, (SUBLANES, r), 0)
    rowf = row.astype(F32)
    big = float(SUBLANES)
    red = lambda f, a: f(a, axis=0, keepdims=True)
    gl = jnp.where(row < N_EXPERT_GROUPS, lt[ROUTER_GROUP_ROW:ROUTER_GROUP_ROW + SUBLANES], -jnp.inf)
    ge = jnp.exp(gl - red(jnp.max, gl))
    p_grp = ge / red(jnp.sum, ge)
    g_p = red(jnp.max, p_grp)
    g_idx = red(jnp.min, jnp.where(p_grp == g_p, rowf, big))
    el = lt[ROUTER_EXPERT_ROW:ROUTER_EXPERT_ROW + EXPERTS_PER_GROUP]
    for g in range(1, N_EXPERT_GROUPS):
        lo = ROUTER_EXPERT_ROW + g * EXPERTS_PER_GROUP
        el = jnp.where(g_idx == float(g), lt[lo:lo + EXPERTS_PER_GROUP], el)
    ee = jnp.exp(el - red(jnp.max, el))
    pe = ee / red(jnp.sum, ee)
    v1 = red(jnp.max, pe)
    i1 = red(jnp.min, jnp.where(pe == v1, rowf, big))
    rest = rowf != i1
    pe2 = jnp.where(rest, pe, -1.0)
    v2 = red(jnp.max, pe2)
    i2 = red(jnp.min, jnp.where(rest & (pe2 == v2), rowf, big))
    den = v1 + v2
    base = g_idx * float(EXPERTS_PER_GROUP)
    out = jnp.where(row == 0, base + i1, 0.0)
    out = jnp.where(row == 1, base + i2, out)
    out = jnp.where(row == 2, g_p * v1 / den, out)
    out = jnp.where(row == 3, g_p * v2 / den, out)
    return jnp.concatenate([out, jnp.zeros((LANES - SUBLANES, r), F32)], axis=0).T


def _ffn_front(h, nfw, wrt_ref, brc):
    hn2 = _rms(h, nfw)
    hi = hn2.astype(BF16)
    lo = (hn2 - hi.astype(F32)).astype(BF16)
    nt = lambda a, b: lax.dot_general(a, b, (((1,), (1,)), ((), ())), preferred_element_type=F32)
    lt = nt(wrt_ref[0], hi) + nt(wrt_ref[0], lo) + nt(wrt_ref[1], hi) + brc
    return hn2, _route_t(lt)


ROW_TILES = D_MODEL // LANES


def _store_row_tiles(ref, val):
    r = val.shape[0]
    for k in range(ROW_TILES):
        ref[pl.ds(k, r, stride=ROW_TILES), :] = val[:, k * LANES:(k + 1) * LANES]


def _load_row_tiles(ref):
    r = ref.shape[0] // ROW_TILES
    return jnp.concatenate([ref[pl.ds(k, r, stride=ROW_TILES), :] for k in range(ROW_TILES)], axis=1)


def _prompt_step(
        t, nt, fillers, x_ref, nmw_ref, wz_ref, wxbc_ref, wdt_ref, wuv_ref, convw_ref, convb_ref, dtb_ref,
        alog_ref, dskip_ref, ssmnw_ref, sgnw_ref, sgw_ref, sgb_ref, convt_ref, ssm_ref,
        xbc_scr, state_scr, y_scr, mix_scr, tl):
    fillers = list(fillers)
    assert len(fillers) == 3 + tl // CHUNK
    @pl.when(t == 0)
    def _():
        xbc_scr[0:SUBLANES, :] = jnp.zeros((SUBLANES, C_CONV), F32)
        state_scr[...] = jnp.zeros(state_scr.shape, F32)

    x = x_ref[...]
    hn = _rms(x, nmw_ref[...]).astype(BF16)
    z = _dot(hn, wz_ref[...])
    xbc = _dot(hn, wxbc_ref[...])
    dtr = _dot(hn, wdt_ref[...])
    uv = _gelu(_dot(hn, wuv_ref[...]))
    fillers.pop(0)()

    xbc_scr[SUBLANES:SUBLANES + tl, :] = xbc
    conv = convb_ref[...]
    for k in range(CONV_K):
        off = SUBLANES - (CONV_K - 1) + k
        conv = conv + convw_ref[k:k + 1, :] * xbc_scr[off:off + tl, :]
    tail = xbc_scr[tl:tl + SUBLANES, :]
    xbc_scr[0:SUBLANES, :] = tail
    convt_ref[...] = tail
    act = _silu(conv)
    xs = act[:, :D_SSM]
    bmat = act[:, D_SSM:D_SSM + SSM_GROUPS * SSM_STATE]
    cmat = act[:, D_SSM + SSM_GROUPS * SSM_STATE:]
    fillers.pop(0)()

    dt = _softplus(dtr + dtb_ref[...])
    a_row = -jnp.exp(alog_ref[...])

    li = lax.broadcasted_iota(jnp.int32, (CHUNK, CHUNK), 0)
    si = lax.broadcasted_iota(jnp.int32, (CHUNK, CHUNK), 1)
    causal = li >= si
    tri = jnp.where(causal, 1.0, 0.0).astype(F32)
    lo_half = si < SSM_HEAD_DIM

    for c in range(tl // CHUNK):
        rows = slice(c * CHUNK, (c + 1) * CHUNK)
        dt_c = dt[rows]
        acum = _dot_f32(tri, dt_c * a_row)
        acum_t = acum.T
        dt_t = dt_c.T
        w_t = dt_t * jnp.exp(acum_t[:, CHUNK - 1:CHUNK] - acum_t)
        xs_c = xs[rows]
        for g in range(SSM_GROUPS):
            gl = slice(g * SSM_STATE, (g + 1) * SSM_STATE)
            b_g = bmat[rows, gl]
            c_g = cmat[rows, gl]
            b_gt = b_g.T
            cb = _dot(c_g.astype(BF16), b_gt.astype(BF16))
            for j in range(HEADS_PER_GROUP // 2):
                pl_ = slice(g * 512 + j * LANES, g * 512 + (j + 1) * LANES)
                m_l, ec_l, s_l, ea_last = [], [], [], []
                for k in range(2):
                    hh = g * HEADS_PER_GROUP + 2 * j + k
                    colb = jnp.broadcast_to(acum[:, hh:hh + 1], (CHUNK, CHUNK))
                    rowb = jnp.broadcast_to(acum_t[hh:hh + 1, :], (CHUNK, CHUNK))
                    ea = jnp.exp(colb)
                    seg = jnp.where(causal, colb - rowb, 0.0)
                    lmat = jnp.where(causal, jnp.exp(seg), 0.0)
                    m_l.append(cb * lmat * jnp.broadcast_to(dt_t[hh:hh + 1, :], (CHUNK, CHUNK)))
                    ec_l.append(c_g * ea)
                    s_l.append(b_gt * jnp.broadcast_to(w_t[hh:hh + 1, :], (CHUNK, CHUNK)))
                    ea_last.append(ea[CHUNK - 1:CHUNK, :])
                xs_p = xs_c[:, pl_]
                st_p = state_scr[g, :, j * LANES:(j + 1) * LANES]
                rx = jnp.concatenate([jnp.where(lo_half, xs_p, 0.0),
                                      jnp.where(lo_half, 0.0, xs_p)], axis=0).astype(BF16)
                rs = jnp.concatenate([jnp.where(lo_half, st_p, 0.0),
                                      jnp.where(lo_half, 0.0, st_p)], axis=0).astype(BF16)
                lhs = jnp.concatenate(m_l + ec_l, axis=1).astype(BF16)
                y_p = _dot(lhs, jnp.concatenate([rx, rs], axis=0))
                y_scr[rows, pl_] = y_p
                dec = jnp.where(lo_half[0:1, :], ea_last[0], ea_last[1])
                upd = _dot(jnp.concatenate(s_l, axis=1).astype(BF16), rx)
                state_scr[g, :, j * LANES:(j + 1) * LANES] = st_p * dec + upd
        fillers.pop(0)()

    y = y_scr[...] + dskip_ref[...] * xs
    y_ssm = _gated_group_norm(y, z, ssmnw_ref[...])
    mix_scr[:, :D_SSM] = y_ssm.astype(BF16)

    u = uv[:, :D_SG]
    v = _rms(uv[:, D_SG:], sgnw_ref[...])
    vb = v.astype(BF16)
    for c in range(tl // CHUNK):
        rows = slice(c * CHUNK, (c + 1) * CHUNK)
        for hd in range(SG_HEADS):
            hl = slice(hd * SG_HEAD_DIM, (hd + 1) * SG_HEAD_DIM)
            sv = _dot(sgw_ref[hd], vb[rows, hl]) + sgb_ref[:, hl]
            mix_scr[rows, D_SSM + hd * SG_HEAD_DIM:D_SSM + (hd + 1) * SG_HEAD_DIM] = (
                u[rows, hl] * sv).astype(BF16)

    fillers.pop(0)()

    @pl.when(t == nt - 1)
    def _():
        for g in range(SSM_GROUPS):
            ssm_ref[g] = state_scr[g].T


def _prompt_mixer_kernel(
        x_ref, xp_ref, nmw_ref, wz_ref, wxbc_ref, wdt_ref, wuv_ref, convw_ref, convb_ref, dtb_ref,
        alog_ref, dskip_ref, ssmnw_ref, sgnw_ref, sgw_ref, sgb_ref, wout_ref, nfw_ref,
        wr_ref, br_ref, hs_ref, hn2s_ref, routes_ref,
        h_ref, hn2_ref, route_ref, convt_ref, ssm_ref,
        xbc_scr, state_scr, y_scr, mix_scr, h_scr, *, tl, nt, n_main):
    i = pl.program_id(0)

    @pl.when(i == 0)
    def _():
        mix_scr[...] = jnp.zeros(mix_scr.shape, BF16)

    @pl.when(i > n_main)
    def _():
        h_ref[...] = hs_ref[...]
        hn2_ref[...] = hn2s_ref[...]
        route_ref[...] = routes_ref[...]

    n_out_chunks = 2 + tl // CHUNK
    width = D_MODEL // n_out_chunks
    assert width % LANES == 0 and width * n_out_chunks == D_MODEL

    def out_chunk(c):
        cols = slice(c * width, (c + 1) * width)
        h_scr[:, cols] = xp_ref[:, cols] + _dot(mix_scr[...], wout_ref[:, cols])

    def finish():
        h = h_scr[...]
        h_ref[...] = h
        hn2, route = _ffn_front(h, nfw_ref[...], wr_ref, br_ref[...])
        _store_row_tiles(hn2_ref, hn2)
        route_ref[...] = route

    @pl.when(i <= n_main)
    def _():
        def out_chunks(cs):
            for c in cs:
                out_chunk(c)

        half = n_out_chunks // 2
        fillers = ([functools.partial(out_chunks, range(half)),
                    functools.partial(out_chunks, range(half, n_out_chunks)), finish]
                   + [lambda: None] * (tl // CHUNK))
        _prompt_step(
            lax.rem(i, nt), nt, fillers, x_ref, nmw_ref, wz_ref, wxbc_ref, wdt_ref, wuv_ref,
            convw_ref, convb_ref, dtb_ref, alog_ref, dskip_ref, ssmnw_ref, sgnw_ref, sgw_ref, sgb_ref,
            convt_ref, ssm_ref, xbc_scr, state_scr, y_scr, mix_scr, tl)


def _const_spec(shape):
    zeros = (0,) * len(shape)
    return pl.BlockSpec(shape, lambda *_: zeros)


def _prompt_mixer(x, w, tl, h_s, hn2_s, route_s):
    bsz, seq, _ = x.shape
    nt = seq // tl
    n_main = bsz * nt
    n_s = h_s.shape[0]
    assert seq % tl == 0 and n_s % tl == 0
    n_tok = bsz * seq + n_s
    x2 = x.reshape(bsz * seq, D_MODEL)
    main = lambda i: jnp.minimum(i, n_main - 1)
    prev = lambda i: jnp.clip(i - 1, 0, n_main - 1)
    tail = lambda i: jnp.maximum(i - n_main - 1, 0)
    out_row = lambda i: jnp.maximum(i - 1, 0)
    weights = [w['nmw'], w['wz'], w['wxbc'], w['wdt'], w['wuv'], w['convw'], w['convb'], w['dtb'],
               w['alog'], w['dskip'], w['ssmnw'], w['sgnw'], w['sgw'], w['sgb'], w['wout'], w['nfw'],
               w['wr'], w['br']]
    in_specs = ([pl.BlockSpec((tl, D_MODEL), lambda i: (main(i), 0)),
                 pl.BlockSpec((tl, D_MODEL), lambda i: (prev(i), 0))]
                + [_const_spec(a.shape) for a in weights]
                + [pl.BlockSpec(blk, lambda i: (tail(i), 0))
                   for blk in ((tl, D_MODEL), (tl * ROW_TILES, LANES), (tl, LANES))])
    q = HEADS_PER_GROUP * SSM_HEAD_DIM
    out_shape = [
        jax.ShapeDtypeStruct((n_tok, D_MODEL), F32),
        jax.ShapeDtypeStruct((n_tok * ROW_TILES, LANES), F32),
        jax.ShapeDtypeStruct((n_tok, LANES), F32),
        jax.ShapeDtypeStruct((bsz, SUBLANES, C_CONV), F32),
        jax.ShapeDtypeStruct((bsz, SSM_GROUPS, q, SSM_STATE), F32),
    ]
    out_specs = [
        pl.BlockSpec((tl, D_MODEL), lambda i: (out_row(i), 0)),
        pl.BlockSpec((tl * ROW_TILES, LANES), lambda i: (out_row(i), 0)),
        pl.BlockSpec((tl, LANES), lambda i: (out_row(i), 0)),
        pl.BlockSpec((None, SUBLANES, C_CONV), lambda i: (main(i) // nt, 0, 0)),
        pl.BlockSpec((None, SSM_GROUPS, q, SSM_STATE), lambda i: (main(i) // nt, 0, 0, 0)),
    ]
    scratch = [
        pltpu.VMEM((tl + SUBLANES, C_CONV), F32),
        pltpu.VMEM((SSM_GROUPS, SSM_STATE, q), F32),
        pltpu.VMEM((tl, D_SSM), F32),
        pltpu.VMEM((tl, D_SSM + D_SG), BF16),
        pltpu.VMEM((tl, D_MODEL), F32),
    ]
    return pl.pallas_call(
        functools.partial(_prompt_mixer_kernel, tl=tl, nt=nt, n_main=n_main),
        grid=(n_main + 1 + n_s // tl,),
        in_specs=in_specs,
        out_specs=out_specs,
        out_shape=out_shape,
        scratch_shapes=scratch,
        compiler_params=pltpu.CompilerParams(
            dimension_semantics=("arbitrary",), vmem_limit_bytes=VMEM_LIMIT),
        name="prompt_mixer",
    )(x2, x2, *weights, h_s, hn2_s, route_s)


def _sample_front_kernel(
        x_ref, cprev_ref, nmw_ref, wz_ref, wxbc_ref, wdt_ref, wuv_ref, convw_ref, convb_ref,
        dtb_ref, alog_ref, dskip_ref, sgnw_ref, w4_ref, b4_ref, expand_ref,
        z_ref, ypart_ref, ea_ref, ysg_ref, v_ref, convn_ref, c_ref, b_ref, xw_ref, dec_ref,
        *, nb, nt):
    x = x_ref[...]
    hn = _rms(x, nmw_ref[...]).astype(BF16)
    z_ref[...] = _dot(hn, wz_ref[...])
    xbc = _dot(hn, wxbc_ref[...])
    dtr = _dot(hn, wdt_ref[...])
    uv = _gelu(_dot(hn, wuv_ref[...]))

    slab = lambda a, t: a[t * nb:(t + 1) * nb]
    full = [cprev_ref[k] for k in range(CONV_K - 1)] + [slab(xbc, t) for t in range(nt)]
    for k in range(CONV_K - 1):
        convn_ref[k] = full[nt + k]
    xs, bm, cm = [], [], []
    for t in range(nt):
        conv = convb_ref[...]
        for k in range(CONV_K):
            conv = conv + convw_ref[k:k + 1, :] * full[t + k]
        act = _silu(conv)
        xs.append(act[:, :D_SSM])
        bm.append(act[:, D_SSM:D_SSM + SSM_GROUPS * SSM_STATE])
        cm.append(act[:, D_SSM + SSM_GROUPS * SSM_STATE:])
        b_ref[t * nb:(t + 1) * nb, :] = bm[t]
        c_ref[t * nb:(t + 1) * nb, :] = cm[t]

    dt = _softplus(dtr + dtb_ref[...])
    a_row = -jnp.exp(alog_ref[...])
    dts = [slab(dt, t) for t in range(nt)]
    cum = []
    for t in range(nt):
        da = dts[t] * a_row
        cum.append(da if t == 0 else cum[t - 1] + da)
    dec_ref[...] = jnp.exp(cum[nt - 1])

    lane = lax.broadcasted_iota(jnp.int32, (nb, LANES), 1)
    first_group = lane < HEADS_PER_GROUP
    facs = [jnp.exp(cum[t]) for t in range(nt)]
    facs += [dts[s] * jnp.exp(cum[nt - 1] - cum[s]) for s in range(nt)]
    pairs = []
    for t in range(nt):
        for s in range(t + 1):
            cb = []
            for g in range(SSM_GROUPS):
                gl = slice(g * SSM_STATE, (g + 1) * SSM_STATE)
                cb.append(jnp.sum(cm[t][:, gl] * bm[s][:, gl], axis=1, keepdims=True))
            cbh = jnp.where(first_group, cb[0], cb[1])
            facs.append(jnp.exp(cum[t] - cum[s]) * dts[s] * cbh)
            pairs.append((t, s))
    fx = _dot_f32(jnp.concatenate(facs, axis=0), expand_ref[...])
    fslab = lambda i: fx[i * nb:(i + 1) * nb]
    for t in range(nt):
        ea_ref[t * nb:(t + 1) * nb, :] = fslab(t)
        xw_ref[t * nb:(t + 1) * nb, :] = xs[t] * fslab(nt + t)
    for t in range(nt):
        acc = dskip_ref[...] * xs[t]
        for i, (tt, s) in enumerate(pairs):
            if tt == t:
                acc = acc + fslab(2 * nt + i) * xs[s]
        ypart_ref[t * nb:(t + 1) * nb, :] = acc

    u = uv[:, :D_SG]
    v = _rms(uv[:, D_SG:], sgnw_ref[...])
    v_ref[...] = v
    for t in range(nt):
        sv = b4_ref[t:t + 1, :]
        for s in range(t + 1):
            sv = sv + w4_ref[t, s:s + 1, :] * slab(v, s)
        ysg_ref[t * nb:(t + 1) * nb, :] = slab(u, t) * sv


def _sample_front(x_slab, cprev, w, nb, nt):
    r = nb * nt
    weights = [w['nmw'], w['wz'], w['wxbc'], w['wdt'], w['wuv'], w['convw'], w['convb'], w['dtb'],
               w['alog'], w['dskip'], w['sgnw'], w['w4'], w['b4'], w['expand']]
    ins = [x_slab, cprev] + weights
    out_shape = [
        jax.ShapeDtypeStruct((r, D_SSM), F32),
        jax.ShapeDtypeStruct((r, D_SSM), F32),
        jax.ShapeDtypeStruct((r, D_SSM), F32),
        jax.ShapeDtypeStruct((r, D_SG), F32),
        jax.ShapeDtypeStruct((r, D_SG), F32),
        jax.ShapeDtypeStruct((CONV_K - 1, nb, C_CONV), F32),
        jax.ShapeDtypeStruct((r, SSM_GROUPS * SSM_STATE), F32),
        jax.ShapeDtypeStruct((r, SSM_GROUPS * SSM_STATE), F32),
        jax.ShapeDtypeStruct((r, D_SSM), F32),
        jax.ShapeDtypeStruct((nb, LANES), F32),
    ]
    return pl.pallas_call(
        functools.partial(_sample_front_kernel, nb=nb, nt=nt),
        grid=(1,),
        in_specs=[_const_spec(a.shape) for a in ins],
        out_specs=[_const_spec(s.shape) for s in out_shape],
        out_shape=out_shape,
        compiler_params=pltpu.CompilerParams(
            dimension_semantics=("arbitrary",), vmem_limit_bytes=VMEM_LIMIT),
        name="sample_front",
    )(*ins)


def _sample_state_kernel(dec_ref, st_ref, cq_ref, bq_ref, xw_ref, so_ref, z_ref, *, seqs):
    for bb in range(seqs):
        for g in range(SSM_GROUPS):
            s0 = st_ref[bb, g]
            z_ref[bb, g] = lax.dot_general(
                cq_ref[bb, g].astype(BF16), s0.astype(BF16), (((1,), (1,)), ((), ())),
                preferred_element_type=F32)
            upd = lax.dot_general(
                xw_ref[bb, g].astype(BF16), bq_ref[bb, g].astype(BF16), (((0,), (0,)), ((), ())),
                preferred_element_type=F32)
            for hh in range(HEADS_PER_GROUP):
                hs = slice(hh * SSM_HEAD_DIM, (hh + 1) * SSM_HEAD_DIM)
                so_ref[bb, g, hs, :] = s0[hs] * dec_ref[bb, g * HEADS_PER_GROUP + hh] + upd[hs]


def _sample_state(dec, state, cq, bq, xwq, seqs):
    nb = state.shape[0]
    q = HEADS_PER_GROUP * SSM_HEAD_DIM
    blk = lambda *tail: pl.BlockSpec((seqs, SSM_GROUPS) + tail, lambda i: (i, 0, 0, 0))
    return pl.pallas_call(
        functools.partial(_sample_state_kernel, seqs=seqs),
        grid=(nb // seqs,),
        in_specs=[
            pl.BlockSpec((seqs, SSM_HEADS), lambda i: (i, 0), memory_space=pltpu.SMEM),
            blk(q, SSM_STATE), blk(SUBLANES, SSM_STATE), blk(SUBLANES, SSM_STATE), blk(SUBLANES, q),
        ],
        out_specs=[blk(q, SSM_STATE), blk(SUBLANES, q)],
        out_shape=[
            jax.ShapeDtypeStruct((nb, SSM_GROUPS, q, SSM_STATE), F32),
            jax.ShapeDtypeStruct((nb, SSM_GROUPS, SUBLANES, q), F32),
        ],
        compiler_params=pltpu.CompilerParams(
            dimension_semantics=("arbitrary",), vmem_limit_bytes=VMEM_LIMIT),
        name="sample_state",
    )(dec, state, cq, bq, xwq)


def _sample_back_kernel(x_ref, ypart_ref, ea_ref, zoff_ref, z_ref, ysg_ref, ssmnw_ref, wout_ref,
                        nfw_ref, wr_ref, br_ref, h_ref, hn2_ref, route_ref):
    y = ypart_ref[...] + ea_ref[...] * zoff_ref[...]
    y_ssm = _gated_group_norm(y, z_ref[...], ssmnw_ref[...])
    mix_in = jnp.concatenate([y_ssm.astype(BF16), ysg_ref[...].astype(BF16)], axis=1)
    h = x_ref[...] + _dot(mix_in, wout_ref[...])
    h_ref[...] = h
    hn2, route = _ffn_front(h, nfw_ref[...], wr_ref, br_ref[...])
    _store_row_tiles(hn2_ref, hn2)
    route_ref[...] = route


def _sample_back(x_slab, ypart, ea, zoff, z, ysg, w):
    r = x_slab.shape[0]
    ins = [x_slab, ypart, ea, zoff, z, ysg, w['ssmnw'], w['wout'], w['nfw'], w['wr'], w['br']]
    out_shape = [
        jax.ShapeDtypeStruct((r, D_MODEL), F32),
        jax.ShapeDtypeStruct((r * ROW_TILES, LANES), F32),
        jax.ShapeDtypeStruct((r, LANES), F32),
    ]
    return pl.pallas_call(
        _sample_back_kernel,
        grid=(1,),
        in_specs=[_const_spec(a.shape) for a in ins],
        out_specs=[_const_spec(s.shape) for s in out_shape],
        out_shape=out_shape,
        compiler_params=pltpu.CompilerParams(
            dimension_semantics=("arbitrary",), vmem_limit_bytes=VMEM_LIMIT),
        name="sample_back",
    )(*ins)


EXPERT_PIECES = 8
DMA_PIECES = 4


def _expert_mlp_kernel(be_ref, nu_ref, tok0_ref, tokn_ref, dstp_ref, x_hbm, wg_ref, wu_ref, wd_ref,
                       parts_hbm, xbuf, obuf, xb, hb, wg_b, wu_b, wd_b, gsem, ssem, *, bm):
    j = pl.program_id(0)
    nu = nu_ref[0]
    nt = ROW_TILES
    tile = lambda ref, start: ref.at[pl.ds(pl.multiple_of(start, nt), nt)]

    def gather_copy(idx_ref, s, r):
        return pltpu.make_async_copy(tile(x_hbm, idx_ref[0, r]), xbuf.at[s, pl.ds(r * nt, nt)], gsem.at[s])

    def scatter_copy(s, r):
        return pltpu.make_async_copy(obuf.at[s, pl.ds(r * nt, nt)], tile(parts_hbm, dstp_ref[0, r]),
                                     ssem.at[s])

    def gather_wait(s):
        pltpu.make_async_copy(x_hbm.at[pl.ds(0, bm * nt)], xbuf.at[s], gsem.at[s]).wait()

    def scatter_wait(s):
        pltpu.make_async_copy(obuf.at[s], parts_hbm.at[pl.ds(0, bm * nt)], ssem.at[s]).wait()

    def step(slot):
        gather_wait(slot)

        @pl.when(j >= 1)
        def _():
            scatter_wait(slot)

        per = bm // DMA_PIECES

        def start_rows(piece):
            if piece >= DMA_PIECES:
                return
            for r in range(piece * per, (piece + 1) * per):
                gather_copy(tokn_ref, 1 - slot, r).start(priority=r % 2)
                scatter_copy(1 - slot, r).start(priority=(r + 1) % 2)

        xb[...] = _load_row_tiles(xbuf.at[slot]).astype(BF16)
        half = D_FF // 2
        for c in range(2):
            cols = slice(c * half, (c + 1) * half)
            start_rows(2 * c)
            g = _dot(xb[...], wg_b[:, cols])
            start_rows(2 * c + 1)
            u = _dot(xb[...], wu_b[:, cols])
            hb[:, cols] = (_silu(g) * u).astype(BF16)
        quarter = D_MODEL // 4
        for c in range(4):
            start_rows(4 + c)
            o = _dot(hb[...], wd_b[:, c * quarter:(c + 1) * quarter])
            for k in range(quarter // LANES):
                kk = c * (quarter // LANES) + k
                obuf.at[slot][pl.ds(kk, bm, stride=nt), :] = o[:, k * LANES:(k + 1) * LANES]

    def drain(slot):
        gather_wait(slot)
        scatter_wait(slot)
        for r in range(bm):
            scatter_copy(1 - slot, r).start(priority=r % 2)
        scatter_wait(1 - slot)

    @pl.when(j == 0)
    def _():
        obuf[...] = jnp.zeros(obuf.shape, F32)
        n_real = parts_hbm.shape[0] - bm * nt
        spare = pltpu.make_async_copy(obuf.at[0], parts_hbm.at[pl.ds(n_real, bm * nt)], ssem.at[0])
        spare.start()
        spare.wait()

        def start0(r, carry):
            pltpu.make_async_copy(tile(x_hbm, tok0_ref[0, r]), xbuf.at[0, pl.ds(pl.multiple_of(r * nt, nt), nt)],
                                  gsem.at[0]).start()
            return carry
        lax.fori_loop(0, bm, start0, 0)

    used = j < nu

    @pl.when(used & ((j == 0) | (be_ref[j] != be_ref[jnp.maximum(j - 1, 0)])))
    def _():
        wg_b[...] = wg_ref[...].astype(BF16)
        wu_b[...] = wu_ref[...].astype(BF16)
        wd_b[...] = wd_ref[...].astype(BF16)

    for slot in range(2):
        parity = lax.rem(j, 2) == slot

        @pl.when(used & parity)
        def _(slot=slot):
            step(slot)

        @pl.when((j == nu) & parity)
        def _(slot=slot):
            drain(slot)


def _expert_mlp(blk_expert, n_used, tok, dst, x, wg, wu, wd, bm, n_out):
    n_blocks = blk_expert.shape[0]
    assert bm % EXPERT_PIECES == 0
    w_map = lambda j, be, nu: (be[j], 0, 0)
    idx_spec = lambda f: pl.BlockSpec((None, 1, bm), lambda j, be, nu: (f(j), 0, 0),
                                      memory_space=pltpu.SMEM)
    any_spec = pl.BlockSpec(memory_space=pl.ANY)
    grid_spec = pltpu.PrefetchScalarGridSpec(
        num_scalar_prefetch=2,
        grid=(n_blocks + 1,),
        in_specs=[
            idx_spec(lambda j: 0),
            idx_spec(lambda j: jnp.minimum(j + 1, n_blocks - 1)),
            idx_spec(lambda j: j),
            any_spec,
            pl.BlockSpec((None, D_MODEL, D_FF), w_map),
            pl.BlockSpec((None, D_MODEL, D_FF), w_map),
            pl.BlockSpec((None, D_FF, D_MODEL), w_map),
        ],
        out_specs=any_spec,
        scratch_shapes=[
            pltpu.VMEM((2, bm * ROW_TILES, LANES), F32),
            pltpu.VMEM((2, bm * ROW_TILES, LANES), F32),
            pltpu.VMEM((bm, D_MODEL), BF16),
            pltpu.VMEM((bm, D_FF), BF16),
            pltpu.VMEM((D_MODEL, D_FF), BF16),
            pltpu.VMEM((D_MODEL, D_FF), BF16),
            pltpu.VMEM((D_FF, D_MODEL), BF16),
            pltpu.SemaphoreType.DMA((2,)),
            pltpu.SemaphoreType.DMA((2,)),
        ],
    )
    tok3 = (tok * ROW_TILES).reshape(n_blocks, 1, bm)
    spare = (n_out - bm + jnp.arange(bm, dtype=jnp.int32)).reshape(1, bm)
    dst_prev = (jnp.concatenate([spare, dst], axis=0) * ROW_TILES).reshape(n_blocks + 1, 1, bm)
    blk_expert = jnp.concatenate([blk_expert, blk_expert[-1:]])
    return pl.pallas_call(
        functools.partial(_expert_mlp_kernel, bm=bm),
        grid_spec=grid_spec,
        out_shape=jax.ShapeDtypeStruct((n_out * ROW_TILES, LANES), F32),
        compiler_params=pltpu.CompilerParams(
            dimension_semantics=("arbitrary",), vmem_limit_bytes=VMEM_LIMIT),
        name="expert_mlp",
    )(blk_expert, n_used, tok3, tok3, dst_prev, x, wg, wu, wd)


def _moe_plan(e_idx, bm, n_blocks):
    n_tok = e_idx.shape[0]
    n_asg = 2 * n_tok
    n_pad = n_blocks * bm - n_asg
    e_flat = e_idx.reshape(-1)
    experts = jnp.arange(N_EXPERTS, dtype=jnp.int32)
    counts = jnp.sum((e_flat[:, None] == experts[None, :]).astype(jnp.int32), axis=0)
    nblk = (counts + bm - 1) // bm
    blk_end = jnp.cumsum(nblk)
    n_used = blk_end[-1]
    j = jnp.arange(n_blocks, dtype=jnp.int32)
    be = jnp.sum((blk_end[None, :] <= jnp.minimum(j, n_used - 1)[:, None]).astype(jnp.int32), axis=1)
    be = jnp.minimum(be, N_EXPERTS - 1)
    pad_end = jnp.cumsum(nblk * bm - counts)
    pad_expert = jnp.sum((pad_end[None, :] <= jnp.arange(n_pad, dtype=jnp.int32)[:, None]).astype(jnp.int32),
                         axis=1)
    keys = jnp.concatenate([2 * e_flat, 2 * pad_expert + 1])
    ids = jnp.concatenate([jnp.arange(n_asg, dtype=jnp.int32), jnp.full((n_pad,), -1, jnp.int32)])
    _, window = lax.sort((keys, ids), num_keys=1, is_stable=True)
    window = window.reshape(n_blocks, bm)
    valid = window >= 0
    q = jnp.arange(bm, dtype=jnp.int32)
    tok = jnp.where(valid, window // 2, (j[:, None] * bm + q[None, :]) % n_tok)
    dst = jnp.where(valid, (window % 2) * n_tok + window // 2,
                    2 * n_tok + q[None, :])
    return tok.astype(jnp.int32), dst.astype(jnp.int32), be, n_used.astype(jnp.int32).reshape(1)


def _final_kernel(h_ref, p0_ref, p1_ref, route_ref, w_ref, op_ref, os_ref, *, n_prompt_steps):
    route = route_ref[...]
    p0 = _load_row_tiles(p0_ref)
    p1 = _load_row_tiles(p1_ref)
    y = _rms(h_ref[...] + route[:, 2:3] * p0 + route[:, 3:4] * p1, w_ref[...])
    i = pl.program_id(0)

    @pl.when(i < n_prompt_steps)
    def _():
        op_ref[...] = y

    @pl.when(i >= n_prompt_steps)
    def _():
        os_ref[...] = y


def _final(h, parts, route, w, tf, n_prompt):
    n_tok = h.shape[0]
    assert n_tok % tf == 0 and n_prompt % tf == 0
    n_p = n_prompt // tf
    row = lambda d: pl.BlockSpec((tf, d), lambda i: (i, 0))
    part = lambda k: pl.BlockSpec((tf * ROW_TILES, LANES), lambda i: (k * (n_tok // tf) + i, 0))
    return pl.pallas_call(
        functools.partial(_final_kernel, n_prompt_steps=n_p),
        grid=(n_tok // tf,),
        in_specs=[row(D_MODEL), part(0), part(1), row(LANES), _const_spec(w.shape)],
        out_specs=[pl.BlockSpec((tf, D_MODEL), lambda i: (jnp.minimum(i, n_p - 1), 0)),
                   pl.BlockSpec((tf, D_MODEL), lambda i: (jnp.maximum(i - n_p, 0), 0))],
        out_shape=[jax.ShapeDtypeStruct((n_prompt, D_MODEL), F32),
                   jax.ShapeDtypeStruct((n_tok - n_prompt, D_MODEL), F32)],
        compiler_params=pltpu.CompilerParams(
            dimension_semantics=("arbitrary",), vmem_limit_bytes=VMEM_LIMIT),
        name="final_combine",
    )(h, parts, parts, route, w)


PROMPT_TILE = 256
MOE_BLOCK_ROWS = 256
FINAL_TILE = 512
STATE_SEQS = 8


def _pad_lanes(a, width=LANES):
    return jnp.pad(a, [(0, 0)] * (a.ndim - 1) + [(0, width - a.shape[-1])])


def _layer_weights(l, norm_mix_w, w_in, conv_w, conv_b, dt_bias, a_log, d_skip, ssm_norm_w, sg_norm_w,
                   sg_w, sg_b, w_out, norm_ffn_w, w_router_group, b_router_group, w_router_expert,
                   b_router_expert, n_dec):
    c0, c1, c2 = D_SSM, D_SSM + C_CONV, D_SSM + C_CONV + SSM_HEADS
    wi = w_in[l]
    causal = jnp.tril(jnp.ones((CHUNK, CHUNK), bool))
    sgw_l = jnp.where(causal, sg_w[l], 0.0)
    head_of_lane = jnp.arange(D_SSM, dtype=jnp.int32) // SSM_HEAD_DIM
    expand = (jnp.arange(LANES, dtype=jnp.int32)[:, None] == head_of_lane[None, :]).astype(F32)
    row = lambda a: a.reshape(1, -1).astype(F32)
    gap = ROUTER_EXPERT_ROW - N_EXPERT_GROUPS
    rest = LANES - ROUTER_EXPERT_ROW - N_EXPERTS
    wrt = jnp.concatenate([w_router_group[l].T, jnp.zeros((gap, D_MODEL), F32), w_router_expert[l].T,
                           jnp.zeros((rest, D_MODEL), F32)], axis=0).astype(F32)
    brc = jnp.concatenate([b_router_group[l], jnp.zeros((gap,), F32), b_router_expert[l],
                           jnp.zeros((rest,), F32)]).astype(F32).reshape(LANES, 1)
    return {
        'nmw': row(norm_mix_w[l]),
        'wz': wi[:, :c0].astype(BF16),
        'wxbc': wi[:, c0:c1].astype(BF16),
        'wdt': _pad_lanes(wi[:, c1:c2]).astype(BF16),
        'wuv': wi[:, c2:].astype(BF16),
        'convw': conv_w[l].astype(F32),
        'convb': row(conv_b[l]),
        'dtb': _pad_lanes(row(dt_bias[l])),
        'alog': _pad_lanes(row(a_log[l])),
        'dskip': row(jnp.repeat(d_skip[l], SSM_HEAD_DIM)),
        'ssmnw': row(ssm_norm_w[l]),
        'sgnw': row(sg_norm_w[l]),
        'sgw': sgw_l.astype(BF16),
        'sgb': jnp.repeat(sg_b[l].T, SG_HEAD_DIM, axis=1).astype(F32),
        'w4': jnp.repeat(jnp.transpose(sgw_l[:, :n_dec, :n_dec], (1, 2, 0)), SG_HEAD_DIM, axis=2).astype(F32),
        'b4': jnp.repeat(sg_b[l][:, :n_dec].T, SG_HEAD_DIM, axis=1).astype(F32),
        'expand': expand,
        'wout': w_out[l].astype(BF16),
        'nfw': row(norm_ffn_w[l]),
        'wr': jnp.stack([wrt.astype(BF16), (wrt - wrt.astype(BF16).astype(F32)).astype(BF16)]),
        'br': brc,
    }


def _layer(l, xp, xs_slab, state_conv, state_ssm, w, w_gate, w_up, w_down):
    bp, seq, _ = xp.shape
    nb, nt = state_conv.shape[1], xs_slab.shape[0] // state_conv.shape[1]
    q = HEADS_PER_GROUP * SSM_HEAD_DIM

    cprev = jnp.transpose(state_conv[l], (1, 0, 2))
    z, ypart, ea, ysg, v, convn, cmat, bmat, xw, dec = _sample_front(xs_slab, cprev, w, nb, nt)
    to_seq = lambda a, d: jnp.pad(
        jnp.transpose(a.reshape(nt, nb, SSM_GROUPS, d), (1, 2, 0, 3)),
        ((0, 0), (0, 0), (0, SUBLANES - nt), (0, 0)))
    ssm_s, zoff = _sample_state(
        dec[:, :SSM_HEADS], state_ssm[l].reshape(nb, SSM_GROUPS, q, SSM_STATE),
        to_seq(cmat, SSM_STATE), to_seq(bmat, SSM_STATE), to_seq(xw, q), STATE_SEQS)
    zoff = jnp.transpose(zoff[:, :, :nt], (2, 0, 1, 3)).reshape(nt * nb, D_SSM)
    h_s, hn2_s, route_s = _sample_back(xs_slab, ypart, ea, zoff, z, ysg, w)

    h, hn2, route, convt_p, ssm_p = _prompt_mixer(xp, w, PROMPT_TILE, h_s, hn2_s, route_s)

    n_tok = h.shape[0]
    bm = MOE_BLOCK_ROWS
    n_blocks = -(-(2 * n_tok) // bm) + N_EXPERTS
    tok, dst, blk_expert, n_used = _moe_plan(route[:, :2].astype(jnp.int32), bm, n_blocks)
    parts = _expert_mlp(blk_expert, n_used, tok, dst, hn2, w_gate[l], w_up[l], w_down[l], bm,
                        2 * n_tok + bm)
    outs = dict(
        h=h, parts=parts, route=route,
        conv_p=convt_p[:, SUBLANES - (CONV_K - 1):],
        ssm_p=ssm_p.reshape(bp, SSM_GROUPS, HEADS_PER_GROUP, SSM_HEAD_DIM, SSM_STATE),
        conv_s=jnp.transpose(convn, (1, 0, 2)),
        ssm_s=ssm_s.reshape(nb, SSM_GROUPS, HEADS_PER_GROUP, SSM_HEAD_DIM, SSM_STATE),
        v_s=jnp.transpose(v.reshape(nt, nb, D_SG), (1, 0, 2)),
    )
    return outs


def kernel(x_prompt, x_sample, state_conv, state_ssm, norm_mix_w, w_in, conv_w, conv_b, dt_bias, a_log, d_skip, ssm_norm_w, sg_norm_w, sg_w, sg_b, w_out, norm_ffn_w, w_router_group, b_router_group, w_router_expert, b_router_expert, w_gate, w_up, w_down, norm_final_w):
    depth = w_in.shape[0]
    assert depth == 1, "the fused final norm assumes a single layer"
    bp, seq, _ = x_prompt.shape
    nb, nt, _ = x_sample.shape
    l = 0
    w = _layer_weights(l, norm_mix_w, w_in, conv_w, conv_b, dt_bias, a_log, d_skip, ssm_norm_w, sg_norm_w,
                       sg_w, sg_b, w_out, norm_ffn_w, w_router_group, b_router_group, w_router_expert,
                       b_router_expert, nt)
    xs_slab = jnp.transpose(x_sample, (1, 0, 2)).reshape(nt * nb, D_MODEL)
    outs = _layer(l, x_prompt, xs_slab, state_conv, state_ssm, w, w_gate, w_up, w_down)
    y_p, y_s = _final(outs['h'], outs['parts'], outs['route'], norm_final_w.reshape(1, -1).astype(F32),
                      FINAL_TILE, bp * seq)
    y_prompt = y_p.reshape(bp, seq, D_MODEL)
    y_sample = jnp.transpose(y_s.reshape(nt, nb, D_MODEL), (1, 0, 2))
    return (y_prompt, y_sample, outs['conv_p'][None], outs['ssm_p'][None], outs['conv_s'][None],
            outs['ssm_s'][None], outs['v_s'][None])
```

```python
import functools

import jax
import jax.numpy as jnp
from jax import lax
from jax.experimental import pallas as pl
from jax.experimental.pallas import tpu as pltpu

D_MODEL = 1024
D_SSM = 1024
SSM_HEAD_DIM = 64
SSM_HEADS = 16
SSM_GROUPS = 2
HEADS_PER_GROUP = 8
SSM_STATE = 128
CONV_K = 4
C_CONV = D_SSM + 2 * SSM_GROUPS * SSM_STATE
D_SG = 1024
SG_HEADS = 8
SG_HEAD_DIM = 128
CHUNK = 128
N_EXPERT_GROUPS = 4
EXPERTS_PER_GROUP = 8
N_EXPERTS = 32
D_FF = 512
EPS = 1e-6

LANES = 128
SUBLANES = 8
VMEM_LIMIT = 56 * 1024 * 1024

F32 = jnp.float32
BF16 = jnp.bfloat16
HIGHEST = lax.Precision.HIGHEST


def _dot(a, b):
    return jnp.dot(a, b, preferred_element_type=F32)


def _dot_f32(a, b):
    return jnp.dot(a, b, preferred_element_type=F32, precision=HIGHEST)


def _rms(x, w):
    return x * lax.rsqrt(jnp.mean(x * x, axis=-1, keepdims=True) + EPS) * w


def _silu(x):
    return x * (1.0 / (1.0 + jnp.exp(-x)))


def _gelu(x):
    return 0.5 * x * (1.0 + lax.erf(x * 0.7071067811865476))


def _softplus(x):
    return jnp.maximum(x, 0.0) + jnp.log1p(jnp.exp(-jnp.abs(x)))


def _gated_group_norm(y, z, w):
    g = y * _silu(z)
    half = D_SSM // SSM_GROUPS
    parts = []
    for k in range(SSM_GROUPS):
        gk = g[:, k * half:(k + 1) * half]
        parts.append(gk * lax.rsqrt(jnp.mean(gk * gk, axis=-1, keepdims=True) + EPS))
    return jnp.concatenate(parts, axis=1) * w


ROUTER_GROUP_ROW = 0
ROUTER_EXPERT_ROW = SUBLANES


def _route_t(lt):
    r = lt.shape[1]
    row = lax.broadcasted_iota(jnp.int32, (SUBLANES, r), 0)
    rowf = row.astype(F32)
    big = float(SUBLANES)
    red = lambda f, a: f(a, axis=0, keepdims=True)
    gl = jnp.where(row < N_EXPERT_GROUPS, lt[ROUTER_GROUP_ROW:ROUTER_GROUP_ROW + SUBLANES], -jnp.inf)
    ge = jnp.exp(gl - red(jnp.max, gl))
    p_grp = ge / red(jnp.sum, ge)
    g_p = red(jnp.max, p_grp)
    g_idx = red(jnp.min, jnp.where(p_grp == g_p, rowf, big))
    el = lt[ROUTER_EXPERT_ROW:ROUTER_EXPERT_ROW + EXPERTS_PER_GROUP]
    for g in range(1, N_EXPERT_GROUPS):
        lo = ROUTER_EXPERT_ROW + g * EXPERTS_PER_GROUP
        el = jnp.where(g_idx == float(g), lt[lo:lo + EXPERTS_PER_GROUP], el)
    ee = jnp.exp(el - red(jnp.max, el))
    pe = ee / red(jnp.sum, ee)
    v1 = red(jnp.max, pe)
    i1 = red(jnp.min, jnp.where(pe == v1, rowf, big))
    rest = rowf != i1
    pe2 = jnp.where(rest, pe, -1.0)
    v2 = red(jnp.max, pe2)
    i2 = red(jnp.min, jnp.where(rest & (pe2 == v2), rowf, big))
    den = v1 + v2
    base = g_idx * float(EXPERTS_PER_GROUP)
    out = jnp.where(row == 0, base + i1, 0.0)
    out = jnp.where(row == 1, base + i2, out)
    out = jnp.where(row == 2, g_p * v1 / den, out)
    out = jnp.where(row == 3, g_p * v2 / den, out)
    return jnp.concatenate([out, jnp.zeros((LANES - SUBLANES, r), F32)], axis=0).T


def _ffn_front(h, nfw, wrt_ref, brc):
    hn2 = _rms(h, nfw)
    hi = hn2.astype(BF16)
    lo = (hn2 - hi.astype(F32)).astype(BF16)
    nt = lambda a, b: lax.dot_general(a, b, (((1,), (1,)), ((), ())), preferred_element_type=F32)
    lt = nt(wrt_ref[0], hi) + nt(wrt_ref[0], lo) + nt(wrt_ref[1], hi) + brc
    return hn2, _route_t(lt)


ROW_TILES = D_MODEL // LANES
CONV_PIECES = 3


def _store_row_tiles(ref, val):
    r = val.shape[0]
    for k in range(ROW_TILES):
        ref[pl.ds(k, r, stride=ROW_TILES), :] = val[:, k * LANES:(k + 1) * LANES]


def _load_row_tiles(ref):
    r = ref.shape[0] // ROW_TILES
    return jnp.concatenate([ref[pl.ds(k, r, stride=ROW_TILES), :] for k in range(ROW_TILES)], axis=1)


def _prompt_step(
        t, nt, fill, pz, pdt, puv, convw_ref, convb_ref, dtb_ref,
        alog_ref, dskip_ref, ssmnw_ref, sgnw_ref, sgw_ref, sgb_ref, convt_ref, ssm_ref,
        xbc_scr, act_scr, state_scr, y_scr, yssm_scr, ysg_scr, tl):
    uv = _gelu(puv[...])
    u = uv[:, :D_SG]
    v = _rms(uv[:, D_SG:], sgnw_ref[...])
    vb = v.astype(BF16)
    for c in range(tl // CHUNK):
        rows = slice(c * CHUNK, (c + 1) * CHUNK)
        for hd in range(SG_HEADS):
            hl = slice(hd * SG_HEAD_DIM, (hd + 1) * SG_HEAD_DIM)
            sv = _dot(sgw_ref[hd], vb[rows, hl]) + sgb_ref[:, hl]
            ysg_scr[rows, hl] = (u[rows, hl] * sv).astype(BF16)
    fill['gmlp']()

    width = C_CONV // CONV_PIECES
    for p in range(CONV_PIECES):
        cols = slice(p * width, (p + 1) * width)
        conv = convb_ref[:, cols]
        for k in range(CONV_K):
            off = SUBLANES - (CONV_K - 1) + k
            conv = conv + convw_ref[k:k + 1, cols] * xbc_scr[off:off + tl, cols]
        act_scr[:, cols] = _silu(conv)
        fill['conv%d' % p]()
    tail = xbc_scr[tl:tl + SUBLANES, :]
    xbc_scr[0:SUBLANES, :] = tail
    convt_ref[...] = tail
    b_col, c_col = D_SSM, D_SSM + SSM_GROUPS * SSM_STATE
    fill['conv']()

    dt = _softplus(pdt[...] + dtb_ref[...])
    a_row = -jnp.exp(alog_ref[...])
    fill['dt']()

    li = lax.broadcasted_iota(jnp.int32, (CHUNK, CHUNK), 0)
    si = lax.broadcasted_iota(jnp.int32, (CHUNK, CHUNK), 1)
    causal = li >= si
    tri = jnp.where(causal, 1.0, 0.0).astype(F32)
    lo_half = si < SSM_HEAD_DIM

    for c in range(tl // CHUNK):
        rows = slice(c * CHUNK, (c + 1) * CHUNK)
        dt_c = dt[rows]
        acum = _dot_f32(tri, dt_c * a_row)
        acum_t = acum.T
        dt_t = dt_c.T
        w_t = dt_t * jnp.exp(acum_t[:, CHUNK - 1:CHUNK] - acum_t)
        xs_c = act_scr[rows, :D_SSM]
        for g in range(SSM_GROUPS):
            gl = slice(g * SSM_STATE, (g + 1) * SSM_STATE)
            b_g = act_scr[rows, b_col + g * SSM_STATE:b_col + (g + 1) * SSM_STATE]
            c_g = act_scr[rows, c_col + g * SSM_STATE:c_col + (g + 1) * SSM_STATE]
            b_gt = b_g.T
            cb = _dot(c_g.astype(BF16), b_gt.astype(BF16))
            for j in range(HEADS_PER_GROUP // 2):
                pl_ = slice(g * 512 + j * LANES, g * 512 + (j + 1) * LANES)
                m_l, ec_l, s_l, ea_last = [], [], [], []
                for k in range(2):
                    hh = g * HEADS_PER_GROUP + 2 * j + k
                    colb = jnp.broadcast_to(acum[:, hh:hh + 1], (CHUNK, CHUNK))
                    rowb = jnp.broadcast_to(acum_t[hh:hh + 1, :], (CHUNK, CHUNK))
                    ea = jnp.exp(colb)
                    seg = jnp.where(causal, colb - rowb, 0.0)
                    lmat = jnp.where(causal, jnp.exp(seg), 0.0)
                    m_l.append(cb * lmat * jnp.broadcast_to(dt_t[hh:hh + 1, :], (CHUNK, CHUNK)))
                    ec_l.append(c_g * ea)
                    s_l.append(b_gt * jnp.broadcast_to(w_t[hh:hh + 1, :], (CHUNK, CHUNK)))
                    ea_last.append(ea[CHUNK - 1:CHUNK, :])
                xs_p = xs_c[:, pl_]
                st_p = state_scr[g, :, j * LANES:(j + 1) * LANES]
                rx = jnp.concatenate([jnp.where(lo_half, xs_p, 0.0),
                                      jnp.where(lo_half, 0.0, xs_p)], axis=0).astype(BF16)
                rs = jnp.concatenate([jnp.where(lo_half, st_p, 0.0),
                                      jnp.where(lo_half, 0.0, st_p)], axis=0).astype(BF16)
                lhs = jnp.concatenate(m_l + ec_l, axis=1).astype(BF16)
                y_p = _dot(lhs, jnp.concatenate([rx, rs], axis=0))
                y_scr[rows, pl_] = y_p
                dec = jnp.where(lo_half[0:1, :], ea_last[0], ea_last[1])
                upd = _dot(jnp.concatenate(s_l, axis=1).astype(BF16), rx)
                state_scr[g, :, j * LANES:(j + 1) * LANES] = st_p * dec + upd
        fill['ssd%d' % c]()

    y = y_scr[...] + dskip_ref[...] * act_scr[:, :D_SSM]
    yssm_scr[...] = _gated_group_norm(y, pz[...], ssmnw_ref[...]).astype(BF16)
    fill['norm']()

    @pl.when(t == nt - 1)
    def _():
        for g in range(SSM_GROUPS):
            ssm_ref[g] = state_scr[g].T


def _prompt_mixer_kernel(
        x_ref, xp_ref, nmw_ref, wz_ref, wxbc_ref, wdt_ref, wuv_ref, convw_ref, convb_ref, dtb_ref,
        alog_ref, dskip_ref, ssmnw_ref, sgnw_ref, sgw_ref, sgb_ref, wout_ref, nfw_ref,
        wr_ref, br_ref, hs_ref, hn2s_ref, routes_ref,
        h_ref, hn2_ref, route_ref, convt_ref, ssm_ref,
        xbc_scr, act_scr, state_scr, y_scr, yssm_scr, ysg_scr, h_scr, hn_scr, pz, pdt, puv,
        *, tl, nt, n_main):
    i = pl.program_id(0)

    @pl.when(i == 0)
    def _():
        for ref in (xbc_scr, act_scr, state_scr, yssm_scr, ysg_scr, h_scr, pz, pdt, puv):
            ref[...] = jnp.zeros(ref.shape, ref.dtype)

    @pl.when(i > n_main + 1)
    def _():
        h_ref[...] = hs_ref[...]
        hn2_ref[...] = hn2s_ref[...]
        route_ref[...] = routes_ref[...]

    half = D_MODEL // 2

    def out_sg():
        h_scr[...] = xp_ref[...] + _dot(ysg_scr[...], wout_ref[D_SSM:, :])

    def out_ssm(c):
        cols = slice(c * half, (c + 1) * half)
        h_scr[:, cols] = h_scr[:, cols] + _dot(yssm_scr[...], wout_ref[:D_SSM, cols])

    def finish():
        h = h_scr[...]
        h_ref[...] = h
        hn2, route = _ffn_front(h, nfw_ref[...], wr_ref, br_ref[...])
        _store_row_tiles(hn2_ref, hn2)
        route_ref[...] = route

    def in_uv(p):
        cols = slice(p * 512, (p + 1) * 512)
        puv[:, cols] = _dot(hn_scr[...], wuv_ref[:, cols])

    def in_xbc(p):
        cols = slice(p * 512, (p + 1) * 512)
        xbc_scr[SUBLANES:SUBLANES + tl, cols] = _dot(hn_scr[...], wxbc_ref[:, cols])

    def in_dt():
        pdt[...] = _dot(hn_scr[...], wdt_ref[...])

    def in_z():
        pz[...] = _dot(hn_scr[...], wz_ref[...])

    def seq(*fns):
        def run():
            for fn in fns:
                fn()
        return run

    @pl.when(i <= n_main + 1)
    def _():
        assert tl // CHUNK == 2
        t_mix = lax.rem(i + nt - 1, nt)

        @pl.when(t_mix == 0)
        def _():
            xbc_scr[0:SUBLANES, :] = jnp.zeros((SUBLANES, C_CONV), F32)
            state_scr[...] = jnp.zeros(state_scr.shape, F32)

        hn_scr[...] = _rms(x_ref[...], nmw_ref[...]).astype(BF16)
        out_sg()
        part = functools.partial
        assert CONV_PIECES == 3
        fill = {'gmlp': part(in_uv, 0), 'conv0': part(in_uv, 1), 'conv1': part(in_uv, 2),
                'conv2': part(in_uv, 3), 'conv': lambda: None,
                'dt': seq(part(out_ssm, 0), part(out_ssm, 1)),
                'ssd0': seq(finish, part(in_xbc, 0)),
                'ssd1': seq(part(in_xbc, 1), part(in_xbc, 2), in_dt),
                'norm': in_z}
        _prompt_step(
            t_mix, nt, fill, pz, pdt, puv, convw_ref, convb_ref, dtb_ref, alog_ref,
            dskip_ref, ssmnw_ref, sgnw_ref, sgw_ref, sgb_ref, convt_ref, ssm_ref,
            xbc_scr, act_scr, state_scr, y_scr, yssm_scr, ysg_scr, tl)


def _const_spec(shape, single_buffer=False):
    zeros = (0,) * len(shape)
    mode = pl.Buffered(1) if single_buffer else None
    return pl.BlockSpec(shape, lambda *_: zeros, pipeline_mode=mode)


def _prompt_mixer(x, w, tl, h_s, hn2_s, route_s):
    bsz, seq, _ = x.shape
    nt = seq // tl
    n_main = bsz * nt
    n_s = h_s.shape[0]
    assert seq % tl == 0 and n_s % tl == 0
    n_tok = bsz * seq + n_s
    x2 = x.reshape(bsz * seq, D_MODEL)
    main = lambda i: jnp.minimum(i, n_main - 1)
    mix = lambda i: jnp.clip(i - 1, 0, n_main - 1)
    prev = lambda i: jnp.clip(i - 2, 0, n_main - 1)
    tail = lambda i: jnp.maximum(i - n_main - 2, 0)
    out_row = lambda i: jnp.maximum(i - 2, 0)
    weights = [w['nmw'], w['wz'], w['wxbc'], w['wdt'], w['wuv'], w['convw'], w['convb'], w['dtb'],
               w['alog'], w['dskip'], w['ssmnw'], w['sgnw'], w['sgw'], w['sgb'], w['wout'], w['nfw'],
               w['wr'], w['br']]
    in_specs = ([pl.BlockSpec((tl, D_MODEL), lambda i: (main(i), 0)),
                 pl.BlockSpec((tl, D_MODEL), lambda i: (prev(i), 0))]
                + [_const_spec(a.shape, single_buffer=True) for a in weights]
                + [pl.BlockSpec(blk, lambda i: (tail(i), 0))
                   for blk in ((tl, D_MODEL), (tl * ROW_TILES, LANES), (tl, LANES))])
    q = HEADS_PER_GROUP * SSM_HEAD_DIM
    out_shape = [
        jax.ShapeDtypeStruct((n_tok, D_MODEL), F32),
        jax.ShapeDtypeStruct((n_tok * ROW_TILES, LANES), F32),
        jax.ShapeDtypeStruct((n_tok, LANES), F32),
        jax.ShapeDtypeStruct((bsz, SUBLANES, C_CONV), F32),
        jax.ShapeDtypeStruct((bsz, SSM_GROUPS, q, SSM_STATE), F32),
    ]
    out_specs = [
        pl.BlockSpec((tl, D_MODEL), lambda i: (out_row(i), 0)),
        pl.BlockSpec((tl * ROW_TILES, LANES), lambda i: (out_row(i), 0)),
        pl.BlockSpec((tl, LANES), lambda i: (out_row(i), 0)),
        pl.BlockSpec((None, SUBLANES, C_CONV), lambda i: (mix(i) // nt, 0, 0)),
        pl.BlockSpec((None, SSM_GROUPS, q, SSM_STATE), lambda i: (mix(i) // nt, 0, 0, 0)),
    ]
    scratch = [
        pltpu.VMEM((tl + SUBLANES, C_CONV), F32),
        pltpu.VMEM((tl, C_CONV), F32),
        pltpu.VMEM((SSM_GROUPS, SSM_STATE, q), F32),
        pltpu.VMEM((tl, D_SSM), F32),
        pltpu.VMEM((tl, D_SSM), BF16),
        pltpu.VMEM((tl, D_SG), BF16),
        pltpu.VMEM((tl, D_MODEL), F32),
        pltpu.VMEM((tl, D_MODEL), BF16),
        pltpu.VMEM((tl, D_SSM), F32),
        pltpu.VMEM((tl, LANES), F32),
        pltpu.VMEM((tl, 2 * D_SG), F32),
    ]
    return pl.pallas_call(
        functools.partial(_prompt_mixer_kernel, tl=tl, nt=nt, n_main=n_main),
        grid=(n_main + 2 + n_s // tl,),
        in_specs=in_specs,
        out_specs=out_specs,
        out_shape=out_shape,
        scratch_shapes=scratch,
        compiler_params=pltpu.CompilerParams(
            dimension_semantics=("arbitrary",), vmem_limit_bytes=VMEM_LIMIT),
        name="prompt_mixer",
    )(x2, x2, *weights, h_s, hn2_s, route_s)


def _sample_front_kernel(
        x_ref, cprev_ref, nmw_ref, wz_ref, wxbc_ref, wdt_ref, wuv_ref, convw_ref, convb_ref,
        dtb_ref, alog_ref, dskip_ref, sgnw_ref, w4_ref, b4_ref, expand_ref,
        z_ref, ypart_ref, ea_ref, ysg_ref, v_ref, convn_ref, c_ref, b_ref, xw_ref, dec_ref,
        *, nb, nt):
    x = x_ref[...]
    hn = _rms(x, nmw_ref[...]).astype(BF16)
    z_ref[...] = _dot(hn, wz_ref[...])
    xbc = _dot(hn, wxbc_ref[...])
    dtr = _dot(hn, wdt_ref[...])
    uv = _gelu(_dot(hn, wuv_ref[...]))

    slab = lambda a, t: a[t * nb:(t + 1) * nb]
    full = [cprev_ref[k] for k in range(CONV_K - 1)] + [slab(xbc, t) for t in range(nt)]
    for k in range(CONV_K - 1):
        convn_ref[k] = full[nt + k]
    xs, bm, cm = [], [], []
    for t in range(nt):
        conv = convb_ref[...]
        for k in range(CONV_K):
            conv = conv + convw_ref[k:k + 1, :] * full[t + k]
        act = _silu(conv)
        xs.append(act[:, :D_SSM])
        bm.append(act[:, D_SSM:D_SSM + SSM_GROUPS * SSM_STATE])
        cm.append(act[:, D_SSM + SSM_GROUPS * SSM_STATE:])
        b_ref[t * nb:(t + 1) * nb, :] = bm[t]
        c_ref[t * nb:(t + 1) * nb, :] = cm[t]

    dt = _softplus(dtr + dtb_ref[...])
    a_row = -jnp.exp(alog_ref[...])
    dts = [slab(dt, t) for t in range(nt)]
    cum = []
    for t in range(nt):
        da = dts[t] * a_row
        cum.append(da if t == 0 else cum[t - 1] + da)
    dec_ref[...] = jnp.exp(cum[nt - 1])

    lane = lax.broadcasted_iota(jnp.int32, (nb, LANES), 1)
    first_group = lane < HEADS_PER_GROUP
    facs = [jnp.exp(cum[t]) for t in range(nt)]
    facs += [dts[s] * jnp.exp(cum[nt - 1] - cum[s]) for s in range(nt)]
    pairs = []
    for t in range(nt):
        for s in range(t + 1):
            cb = []
            for g in range(SSM_GROUPS):
                gl = slice(g * SSM_STATE, (g + 1) * SSM_STATE)
                cb.append(jnp.sum(cm[t][:, gl] * bm[s][:, gl], axis=1, keepdims=True))
            cbh = jnp.where(first_group, cb[0], cb[1])
            facs.append(jnp.exp(cum[t] - cum[s]) * dts[s] * cbh)
            pairs.append((t, s))
    fx = _dot_f32(jnp.concatenate(facs, axis=0), expand_ref[...])
    fslab = lambda i: fx[i * nb:(i + 1) * nb]
    for t in range(nt):
        ea_ref[t * nb:(t + 1) * nb, :] = fslab(t)
        xw_ref[t * nb:(t + 1) * nb, :] = xs[t] * fslab(nt + t)
    for t in range(nt):
        acc = dskip_ref[...] * xs[t]
        for i, (tt, s) in enumerate(pairs):
            if tt == t:
                acc = acc + fslab(2 * nt + i) * xs[s]
        ypart_ref[t * nb:(t + 1) * nb, :] = acc

    u = uv[:, :D_SG]
    v = _rms(uv[:, D_SG:], sgnw_ref[...])
    v_ref[...] = v
    for t in range(nt):
        sv = b4_ref[t:t + 1, :]
        for s in range(t + 1):
            sv = sv + w4_ref[t, s:s + 1, :] * slab(v, s)
        ysg_ref[t * nb:(t + 1) * nb, :] = slab(u, t) * sv


def _sample_front(x_slab, cprev, w, nb, nt):
    r = nb * nt
    weights = [w['nmw'], w['wz'], w['wxbc'], w['wdt'], w['wuv'], w['convw'], w['convb'], w['dtb'],
               w['alog'], w['dskip'], w['sgnw'], w['w4'], w['b4'], w['expand']]
    ins = [x_slab, cprev] + weights
    out_shape = [
        jax.ShapeDtypeStruct((r, D_SSM), F32),
        jax.ShapeDtypeStruct((r, D_SSM), F32),
        jax.ShapeDtypeStruct((r, D_SSM), F32),
        jax.ShapeDtypeStruct((r, D_SG), F32),
        jax.ShapeDtypeStruct((r, D_SG), F32),
        jax.ShapeDtypeStruct((CONV_K - 1, nb, C_CONV), F32),
        jax.ShapeDtypeStruct((r, SSM_GROUPS * SSM_STATE), F32),
        jax.ShapeDtypeStruct((r, SSM_GROUPS * SSM_STATE), F32),
        jax.ShapeDtypeStruct((r, D_SSM), F32),
        jax.ShapeDtypeStruct((nb, LANES), F32),
    ]
    return pl.pallas_call(
        functools.partial(_sample_front_kernel, nb=nb, nt=nt),
        grid=(1,),
        in_specs=[_const_spec(a.shape) for a in ins],
        out_specs=[_const_spec(s.shape) for s in out_shape],
        out_shape=out_shape,
        compiler_params=pltpu.CompilerParams(
            dimension_semantics=("arbitrary",), vmem_limit_bytes=VMEM_LIMIT),
        name="sample_front",
    )(*ins)


def _sample_state_kernel(dec_ref, st_ref, cq_ref, bq_ref, xw_ref, so_ref, z_ref, *, seqs):
    for bb in range(seqs):
        for g in range(SSM_GROUPS):
            s0 = st_ref[bb, g]
            z_ref[bb, g] = lax.dot_general(
                cq_ref[bb, g].astype(BF16), s0.astype(BF16), (((1,), (1,)), ((), ())),
                preferred_element_type=F32)
            upd = lax.dot_general(
                xw_ref[bb, g].astype(BF16), bq_ref[bb, g].astype(BF16), (((0,), (0,)), ((), ())),
                preferred_element_type=F32)
            for hh in range(HEADS_PER_GROUP):
                hs = slice(hh * SSM_HEAD_DIM, (hh + 1) * SSM_HEAD_DIM)
                so_ref[bb, g, hs, :] = s0[hs] * dec_ref[bb, g * HEADS_PER_GROUP + hh] + upd[hs]


def _sample_state(dec, state, cq, bq, xwq, seqs):
    nb = state.shape[0]
    q = HEADS_PER_GROUP * SSM_HEAD_DIM
    blk = lambda *tail: pl.BlockSpec((seqs, SSM_GROUPS) + tail, lambda i: (i, 0, 0, 0))
    return pl.pallas_call(
        functools.partial(_sample_state_kernel, seqs=seqs),
        grid=(nb // seqs,),
        in_specs=[
            pl.BlockSpec((seqs, SSM_HEADS), lambda i: (i, 0), memory_space=pltpu.SMEM),
            blk(q, SSM_STATE), blk(SUBLANES, SSM_STATE), blk(SUBLANES, SSM_STATE), blk(SUBLANES, q),
        ],
        out_specs=[blk(q, SSM_STATE), blk(SUBLANES, q)],
        out_shape=[
            jax.ShapeDtypeStruct((nb, SSM_GROUPS, q, SSM_STATE), F32),
            jax.ShapeDtypeStruct((nb, SSM_GROUPS, SUBLANES, q), F32),
        ],
        compiler_params=pltpu.CompilerParams(
            dimension_semantics=("arbitrary",), vmem_limit_bytes=VMEM_LIMIT),
        name="sample_state",
    )(dec, state, cq, bq, xwq)


def _sample_back_kernel(x_ref, ypart_ref, ea_ref, zoff_ref, z_ref, ysg_ref, ssmnw_ref, wout_ref,
                        nfw_ref, wr_ref, br_ref, h_ref, hn2_ref, route_ref):
    y = ypart_ref[...] + ea_ref[...] * zoff_ref[...]
    y_ssm = _gated_group_norm(y, z_ref[...], ssmnw_ref[...])
    mix_in = jnp.concatenate([y_ssm.astype(BF16), ysg_ref[...].astype(BF16)], axis=1)
    h = x_ref[...] + _dot(mix_in, wout_ref[...])
    h_ref[...] = h
    hn2, route = _ffn_front(h, nfw_ref[...], wr_ref, br_ref[...])
    _store_row_tiles(hn2_ref, hn2)
    route_ref[...] = route


def _sample_back(x_slab, ypart, ea, zoff, z, ysg, w):
    r = x_slab.shape[0]
    ins = [x_slab, ypart, ea, zoff, z, ysg, w['ssmnw'], w['wout'], w['nfw'], w['wr'], w['br']]
    out_shape = [
        jax.ShapeDtypeStruct((r, D_MODEL), F32),
        jax.ShapeDtypeStruct((r * ROW_TILES, LANES), F32),
        jax.ShapeDtypeStruct((r, LANES), F32),
    ]
    return pl.pallas_call(
        _sample_back_kernel,
        grid=(1,),
        in_specs=[_const_spec(a.shape) for a in ins],
        out_specs=[_const_spec(s.shape) for s in out_shape],
        out_shape=out_shape,
        compiler_params=pltpu.CompilerParams(
            dimension_semantics=("arbitrary",), vmem_limit_bytes=VMEM_LIMIT),
        name="sample_back",
    )(*ins)


EXPERT_PIECES = 8
DMA_PIECES = 4


def _expert_mlp_kernel(be_ref, nu_ref, tok0_ref, tokn_ref, dstp_ref, x_hbm, wg_ref, wu_ref, wd_ref,
                       parts_hbm, xbuf, obuf, xb, hb, wg_b, wu_b, wd_b, gsem, ssem, *, bm):
    j = pl.program_id(0)
    nu = nu_ref[0]
    nt = ROW_TILES
    tile = lambda ref, start: ref.at[pl.ds(pl.multiple_of(start, nt), nt)]

    def gather_copy(idx_ref, s, r):
        return pltpu.make_async_copy(tile(x_hbm, idx_ref[0, r]), xbuf.at[s, pl.ds(r * nt, nt)], gsem.at[s])

    def scatter_copy(s, r):
        return pltpu.make_async_copy(obuf.at[s, pl.ds(r * nt, nt)], tile(parts_hbm, dstp_ref[0, r]),
                                     ssem.at[s])

    def gather_wait(s):
        pltpu.make_async_copy(x_hbm.at[pl.ds(0, bm * nt)], xbuf.at[s], gsem.at[s]).wait()

    def scatter_wait(s):
        pltpu.make_async_copy(obuf.at[s], parts_hbm.at[pl.ds(0, bm * nt)], ssem.at[s]).wait()

    def step(slot):
        gather_wait(slot)

        @pl.when(j >= 1)
        def _():
            scatter_wait(slot)

        per = bm // DMA_PIECES

        def start_rows(piece):
            if piece >= DMA_PIECES:
                return
            for r in range(piece * per, (piece + 1) * per):
                gather_copy(tokn_ref, 1 - slot, r).start(priority=r % 2)
                scatter_copy(1 - slot, r).start(priority=(r + 1) % 2)

        xb[...] = _load_row_tiles(xbuf.at[slot]).astype(BF16)
        half = D_FF // 2
        for c in range(2):
            cols = slice(c * half, (c + 1) * half)
            start_rows(2 * c)
            g = _dot(xb[...], wg_b[:, cols])
            start_rows(2 * c + 1)
            u = _dot(xb[...], wu_b[:, cols])
            hb[:, cols] = (_silu(g) * u).astype(BF16)
        quarter = D_MODEL // 4
        for c in range(4):
            start_rows(4 + c)
            o = _dot(hb[...], wd_b[:, c * quarter:(c + 1) * quarter])
            for k in range(quarter // LANES):
                kk = c * (quarter // LANES) + k
                obuf.at[slot][pl.ds(kk, bm, stride=nt), :] = o[:, k * LANES:(k + 1) * LANES]

    def drain(slot):
        gather_wait(slot)
        scatter_wait(slot)
        for r in range(bm):
            scatter_copy(1 - slot, r).start(priority=r % 2)
        scatter_wait(1 - slot)

    @pl.when(j == 0)
    def _():
        obuf[...] = jnp.zeros(obuf.shape, F32)
        n_real = parts_hbm.shape[0] - bm * nt
        spare = pltpu.make_async_copy(obuf.at[0], parts_hbm.at[pl.ds(n_real, bm * nt)], ssem.at[0])
        spare.start()
        spare.wait()

        def start0(r, carry):
            pltpu.make_async_copy(tile(x_hbm, tok0_ref[0, r]), xbuf.at[0, pl.ds(pl.multiple_of(r * nt, nt), nt)],
                                  gsem.at[0]).start()
            return carry
        lax.fori_loop(0, bm, start0, 0)

    used = j < nu

    @pl.when(used & ((j == 0) | (be_ref[j] != be_ref[jnp.maximum(j - 1, 0)])))
    def _():
        wg_b[...] = wg_ref[...].astype(BF16)
        wu_b[...] = wu_ref[...].astype(BF16)
        wd_b[...] = wd_ref[...].astype(BF16)

    for slot in range(2):
        parity = lax.rem(j, 2) == slot

        @pl.when(used & parity)
        def _(slot=slot):
            step(slot)

        @pl.when((j == nu) & parity)
        def _(slot=slot):
            drain(slot)


def _expert_mlp(blk_expert, n_used, tok, dst, x, wg, wu, wd, bm, n_out):
    n_blocks = blk_expert.shape[0]
    assert bm % EXPERT_PIECES == 0
    w_map = lambda j, be, nu: (be[j], 0, 0)
    idx_spec = lambda f: pl.BlockSpec((None, 1, bm), lambda j, be, nu: (f(j), 0, 0),
                                      memory_space=pltpu.SMEM)
    any_spec = pl.BlockSpec(memory_space=pl.ANY)
    grid_spec = pltpu.PrefetchScalarGridSpec(
        num_scalar_prefetch=2,
        grid=(n_blocks + 1,),
        in_specs=[
            idx_spec(lambda j: 0),
            idx_spec(lambda j: jnp.minimum(j + 1, n_blocks - 1)),
            idx_spec(lambda j: j),
            any_spec,
            pl.BlockSpec((None, D_MODEL, D_FF), w_map),
            pl.BlockSpec((None, D_MODEL, D_FF), w_map),
            pl.BlockSpec((None, D_FF, D_MODEL), w_map),
        ],
        out_specs=any_spec,
        scratch_shapes=[
            pltpu.VMEM((2, bm * ROW_TILES, LANES), F32),
            pltpu.VMEM((2, bm * ROW_TILES, LANES), F32),
            pltpu.VMEM((bm, D_MODEL), BF16),
            pltpu.VMEM((bm, D_FF), BF16),
            pltpu.VMEM((D_MODEL, D_FF), BF16),
            pltpu.VMEM((D_MODEL, D_FF), BF16),
            pltpu.VMEM((D_FF, D_MODEL), BF16),
            pltpu.SemaphoreType.DMA((2,)),
            pltpu.SemaphoreType.DMA((2,)),
        ],
    )
    tok3 = (tok * ROW_TILES).reshape(n_blocks, 1, bm)
    spare = (n_out - bm + jnp.arange(bm, dtype=jnp.int32)).reshape(1, bm)
    dst_prev = (jnp.concatenate([spare, dst], axis=0) * ROW_TILES).reshape(n_blocks + 1, 1, bm)
    blk_expert = jnp.concatenate([blk_expert, blk_expert[-1:]])
    return pl.pallas_call(
        functools.partial(_expert_mlp_kernel, bm=bm),
        grid_spec=grid_spec,
        out_shape=jax.ShapeDtypeStruct((n_out * ROW_TILES, LANES), F32),
        compiler_params=pltpu.CompilerParams(
            dimension_semantics=("arbitrary",), vmem_limit_bytes=VMEM_LIMIT),
        name="expert_mlp",
    )(blk_expert, n_used, tok3, tok3, dst_prev, x, wg, wu, wd)


def _moe_plan(e_idx, bm, n_blocks):
    n_tok = e_idx.shape[0]
    n_asg = 2 * n_tok
    n_pad = n_blocks * bm - n_asg
    e_flat = e_idx.reshape(-1)
    experts = jnp.arange(N_EXPERTS, dtype=jnp.int32)
    counts = jnp.sum((e_flat[:, None] == experts[None, :]).astype(jnp.int32), axis=0)
    nblk = (counts + bm - 1) // bm
    blk_end = jnp.cumsum(nblk)
    n_used = blk_end[-1]
    j = jnp.arange(n_blocks, dtype=jnp.int32)
    be = jnp.sum((blk_end[None, :] <= jnp.minimum(j, n_used - 1)[:, None]).astype(jnp.int32), axis=1)
    be = jnp.minimum(be, N_EXPERTS - 1)
    pad_end = jnp.cumsum(nblk * bm - counts)
    pad_expert = jnp.sum((pad_end[None, :] <= jnp.arange(n_pad, dtype=jnp.int32)[:, None]).astype(jnp.int32),
                         axis=1)
    keys = jnp.concatenate([2 * e_flat, 2 * pad_expert + 1])
    ids = jnp.concatenate([jnp.arange(n_asg, dtype=jnp.int32), jnp.full((n_pad,), -1, jnp.int32)])
    _, window = lax.sort((keys, ids), num_keys=1, is_stable=True)
    window = window.reshape(n_blocks, bm)
    valid = window >= 0
    q = jnp.arange(bm, dtype=jnp.int32)
    tok = jnp.where(valid, window // 2, (j[:, None] * bm + q[None, :]) % n_tok)
    dst = jnp.where(valid, (window % 2) * n_tok + window // 2,
                    2 * n_tok + q[None, :])
    return tok.astype(jnp.int32), dst.astype(jnp.int32), be, n_used.astype(jnp.int32).reshape(1)


def _final_kernel(h_ref, p0_ref, p1_ref, route_ref, w_ref, op_ref, os_ref, *, n_prompt_steps):
    route = route_ref[...]
    p0 = _load_row_tiles(p0_ref)
    p1 = _load_row_tiles(p1_ref)
    y = _rms(h_ref[...] + route[:, 2:3] * p0 + route[:, 3:4] * p1, w_ref[...])
    i = pl.program_id(0)

    @pl.when(i < n_prompt_steps)
    def _():
        op_ref[...] = y

    @pl.when(i >= n_prompt_steps)
    def _():
        os_ref[...] = y


def _final(h, parts, route, w, tf, n_prompt):
    n_tok = h.shape[0]
    assert n_tok % tf == 0 and n_prompt % tf == 0
    n_p = n_prompt // tf
    row = lambda d: pl.BlockSpec((tf, d), lambda i: (i, 0))
    part = lambda k: pl.BlockSpec((tf * ROW_TILES, LANES), lambda i: (k * (n_tok // tf) + i, 0))
    return pl.pallas_call(
        functools.partial(_final_kernel, n_prompt_steps=n_p),
        grid=(n_tok // tf,),
        in_specs=[row(D_MODEL), part(0), part(1), row(LANES), _const_spec(w.shape)],
        out_specs=[pl.BlockSpec((tf, D_MODEL), lambda i: (jnp.minimum(i, n_p - 1), 0)),
                   pl.BlockSpec((tf, D_MODEL), lambda i: (jnp.maximum(i - n_p, 0), 0))],
        out_shape=[jax.ShapeDtypeStruct((n_prompt, D_MODEL), F32),
                   jax.ShapeDtypeStruct((n_tok - n_prompt, D_MODEL), F32)],
        compiler_params=pltpu.CompilerParams(
            dimension_semantics=("arbitrary",), vmem_limit_bytes=VMEM_LIMIT),
        name="final_combine",
    )(h, parts, parts, route, w)


PROMPT_TILE = 256
MOE_BLOCK_ROWS = 256
FINAL_TILE = 512
STATE_SEQS = 8


def _pad_lanes(a, width=LANES):
    return jnp.pad(a, [(0, 0)] * (a.ndim - 1) + [(0, width - a.shape[-1])])


def _layer_weights(l, norm_mix_w, w_in, conv_w, conv_b, dt_bias, a_log, d_skip, ssm_norm_w, sg_norm_w,
                   sg_w, sg_b, w_out, norm_ffn_w, w_router_group, b_router_group, w_router_expert,
                   b_router_expert, n_dec):
    c0, c1, c2 = D_SSM, D_SSM + C_CONV, D_SSM + C_CONV + SSM_HEADS
    wi = w_in[l]
    causal = jnp.tril(jnp.ones((CHUNK, CHUNK), bool))
    sgw_l = jnp.where(causal, sg_w[l], 0.0)
    head_of_lane = jnp.arange(D_SSM, dtype=jnp.int32) // SSM_HEAD_DIM
    expand = (jnp.arange(LANES, dtype=jnp.int32)[:, None] == head_of_lane[None, :]).astype(F32)
    row = lambda a: a.reshape(1, -1).astype(F32)
    gap = ROUTER_EXPERT_ROW - N_EXPERT_GROUPS
    rest = LANES - ROUTER_EXPERT_ROW - N_EXPERTS
    wrt = jnp.concatenate([w_router_group[l].T, jnp.zeros((gap, D_MODEL), F32), w_router_expert[l].T,
                           jnp.zeros((rest, D_MODEL), F32)], axis=0).astype(F32)
    brc = jnp.concatenate([b_router_group[l], jnp.zeros((gap,), F32), b_router_expert[l],
                           jnp.zeros((rest,), F32)]).astype(F32).reshape(LANES, 1)
    return {
        'nmw': row(norm_mix_w[l]),
        'wz': wi[:, :c0].astype(BF16),
        'wxbc': wi[:, c0:c1].astype(BF16),
        'wdt': _pad_lanes(wi[:, c1:c2]).astype(BF16),
        'wuv': wi[:, c2:].astype(BF16),
        'convw': conv_w[l].astype(F32),
        'convb': row(conv_b[l]),
        'dtb': _pad_lanes(row(dt_bias[l])),
        'alog': _pad_lanes(row(a_log[l])),
        'dskip': row(jnp.repeat(d_skip[l], SSM_HEAD_DIM)),
        'ssmnw': row(ssm_norm_w[l]),
        'sgnw': row(sg_norm_w[l]),
        'sgw': sgw_l.astype(BF16),
        'sgb': jnp.repeat(sg_b[l].T, SG_HEAD_DIM, axis=1).astype(F32),
        'w4': jnp.repeat(jnp.transpose(sgw_l[:, :n_dec, :n_dec], (1, 2, 0)), SG_HEAD_DIM, axis=2).astype(F32),
        'b4': jnp.repeat(sg_b[l][:, :n_dec].T, SG_HEAD_DIM, axis=1).astype(F32),
        'expand': expand,
        'wout': w_out[l].astype(BF16),
        'nfw': row(norm_ffn_w[l]),
        'wr': jnp.stack([wrt.astype(BF16), (wrt - wrt.astype(BF16).astype(F32)).astype(BF16)]),
        'br': brc,
    }


def _layer(l, xp, xs_slab, state_conv, state_ssm, w, w_gate, w_up, w_down):
    bp, seq, _ = xp.shape
    nb, nt = state_conv.shape[1], xs_slab.shape[0] // state_conv.shape[1]
    q = HEADS_PER_GROUP * SSM_HEAD_DIM

    cprev = jnp.transpose(state_conv[l], (1, 0, 2))
    z, ypart, ea, ysg, v, convn, cmat, bmat, xw, dec = _sample_front(xs_slab, cprev, w, nb, nt)
    to_seq = lambda a, d: jnp.pad(
        jnp.transpose(a.reshape(nt, nb, SSM_GROUPS, d), (1, 2, 0, 3)),
        ((0, 0), (0, 0), (0, SUBLANES - nt), (0, 0)))
    ssm_s, zoff = _sample_state(
        dec[:, :SSM_HEADS], state_ssm[l].reshape(nb, SSM_GROUPS, q, SSM_STATE),
        to_seq(cmat, SSM_STATE), to_seq(bmat, SSM_STATE), to_seq(xw, q), STATE_SEQS)
    zoff = jnp.transpose(zoff[:, :, :nt], (2, 0, 1, 3)).reshape(nt * nb, D_SSM)
    h_s, hn2_s, route_s = _sample_back(xs_slab, ypart, ea, zoff, z, ysg, w)

    h, hn2, route, convt_p, ssm_p = _prompt_mixer(xp, w, PROMPT_TILE, h_s, hn2_s, route_s)

    n_tok = h.shape[0]
    bm = MOE_BLOCK_ROWS
    n_blocks = -(-(2 * n_tok) // bm) + N_EXPERTS
    tok, dst, blk_expert, n_used = _moe_plan(route[:, :2].astype(jnp.int32), bm, n_blocks)
    parts = _expert_mlp(blk_expert, n_used, tok, dst, hn2, w_gate[l], w_up[l], w_down[l], bm,
                        2 * n_tok + bm)
    outs = dict(
        h=h, parts=parts, route=route,
        conv_p=convt_p[:, SUBLANES - (CONV_K - 1):],
        ssm_p=ssm_p.reshape(bp, SSM_GROUPS, HEADS_PER_GROUP, SSM_HEAD_DIM, SSM_STATE),
        conv_s=jnp.transpose(convn, (1, 0, 2)),
        ssm_s=ssm_s.reshape(nb, SSM_GROUPS, HEADS_PER_GROUP, SSM_HEAD_DIM, SSM_STATE),
        v_s=jnp.transpose(v.reshape(nt, nb, D_SG), (1, 0, 2)),
    )
    return outs


def kernel(x_prompt, x_sample, state_conv, state_ssm, norm_mix_w, w_in, conv_w, conv_b, dt_bias, a_log, d_skip, ssm_norm_w, sg_norm_w, sg_w, sg_b, w_out, norm_ffn_w, w_router_group, b_router_group, w_router_expert, b_router_expert, w_gate, w_up, w_down, norm_final_w):
    depth = w_in.shape[0]
    assert depth == 1, "the fused final norm assumes a single layer"
    bp, seq, _ = x_prompt.shape
    nb, nt, _ = x_sample.shape
    l = 0
    w = _layer_weights(l, norm_mix_w, w_in, conv_w, conv_b, dt_bias, a_log, d_skip, ssm_norm_w, sg_norm_w,
                       sg_w, sg_b, w_out, norm_ffn_w, w_router_group, b_router_group, w_router_expert,
                       b_router_expert, nt)
    xs_slab = jnp.transpose(x_sample, (1, 0, 2)).reshape(nt * nb, D_MODEL)
    outs = _layer(l, x_prompt, xs_slab, state_conv, state_ssm, w, w_gate, w_up, w_down)
    y_p, y_s = _final(outs['h'], outs['parts'], outs['route'], norm_final_w.reshape(1, -1).astype(F32),
                      FINAL_TILE, bp * seq)
    y_prompt = y_p.reshape(bp, seq, D_MODEL)
    y_sample = jnp.transpose(y_s.reshape(nt, nb, D_MODEL), (1, 0, 2))
    return (y_prompt, y_sample, outs['conv_p'][None], outs['ssm_p'][None], outs['conv_s'][None],
            outs['ssm_s'][None], outs['v_s'][None])
```

```python
import functools

import jax
import jax.numpy as jnp
from jax import lax
from jax.experimental import pallas as pl
from jax.experimental.pallas import tpu as pltpu

D_MODEL = 1024
D_SSM = 1024
SSM_HEAD_DIM = 64
SSM_HEADS = 16
SSM_GROUPS = 2
HEADS_PER_GROUP = 8
SSM_STATE = 128
CONV_K = 4
C_CONV = D_SSM + 2 * SSM_GROUPS * SSM_STATE
D_SG = 1024
SG_HEADS = 8
SG_HEAD_DIM = 128
CHUNK = 128
N_EXPERT_GROUPS = 4
EXPERTS_PER_GROUP = 8
N_EXPERTS = 32
D_FF = 512
EPS = 1e-6

LANES = 128
SUBLANES = 8
VMEM_LIMIT = 56 * 1024 * 1024

F32 = jnp.float32
BF16 = jnp.bfloat16
HIGHEST = lax.Precision.HIGHEST


def _dot(a, b):
    return jnp.dot(a, b, preferred_element_type=F32)


def _dot_f32(a, b):
    return jnp.dot(a, b, preferred_element_type=F32, precision=HIGHEST)


def _rms(x, w):
    return x * lax.rsqrt(jnp.mean(x * x, axis=-1, keepdims=True) + EPS) * w


def _silu(x):
    return x * (1.0 / (1.0 + jnp.exp(-x)))


def _gelu(x):
    return 0.5 * x * (1.0 + lax.erf(x * 0.7071067811865476))


def _softplus(x):
    return jnp.maximum(x, 0.0) + jnp.log1p(jnp.exp(-jnp.abs(x)))


def _gated_group_norm(y, z, w):
    g = y * _silu(z)
    half = D_SSM // SSM_GROUPS
    parts = []
    for k in range(SSM_GROUPS):
        gk = g[:, k * half:(k + 1) * half]
        parts.append(gk * lax.rsqrt(jnp.mean(gk * gk, axis=-1, keepdims=True) + EPS))
    return jnp.concatenate(parts, axis=1) * w


ROUTER_GROUP_ROW = 0
ROUTER_EXPERT_ROW = SUBLANES


def _route_t(lt):
    r = lt.shape[1]
    row = lax.broadcasted_iota(jnp.int32, (SUBLANES, r), 0)
    rowf = row.astype(F32)
    big = float(SUBLANES)
    red = lambda f, a: f(a, axis=0, keepdims=True)
    gl = jnp.where(row < N_EXPERT_GROUPS, lt[ROUTER_GROUP_ROW:ROUTER_GROUP_ROW + SUBLANES], -jnp.inf)
    ge = jnp.exp(gl - red(jnp.max, gl))
    p_grp = ge / red(jnp.sum, ge)
    g_p = red(jnp.max, p_grp)
    g_idx = red(jnp.min, jnp.where(p_grp == g_p, rowf, big))
    el = lt[ROUTER_EXPERT_ROW:ROUTER_EXPERT_ROW + EXPERTS_PER_GROUP]
    for g in range(1, N_EXPERT_GROUPS):
        lo = ROUTER_EXPERT_ROW + g * EXPERTS_PER_GROUP
        el = jnp.where(g_idx == float(g), lt[lo:lo + EXPERTS_PER_GROUP], el)
    ee = jnp.exp(el - red(jnp.max, el))
    pe = ee / red(jnp.sum, ee)
    v1 = red(jnp.max, pe)
    i1 = red(jnp.min, jnp.where(pe == v1, rowf, big))
    rest = rowf != i1
    pe2 = jnp.where(rest, pe, -1.0)
    v2 = red(jnp.max, pe2)
    i2 = red(jnp.min, jnp.where(rest & (pe2 == v2), rowf, big))
    den = v1 + v2
    base = g_idx * float(EXPERTS_PER_GROUP)
    out = jnp.where(row == 0, base + i1, 0.0)
    out = jnp.where(row == 1, base + i2, out)
    out = jnp.where(row == 2, g_p * v1 / den, out)
    out = jnp.where(row == 3, g_p * v2 / den, out)
    return jnp.concatenate([out, jnp.zeros((LANES - SUBLANES, r), F32)], axis=0).T


def _ffn_front(h, nfw, wrt_ref, brc):
    hn2 = _rms(h, nfw)
    hi = hn2.astype(BF16)
    lo = (hn2 - hi.astype(F32)).astype(BF16)
    nt = lambda a, b: lax.dot_general(a, b, (((1,), (1,)), ((), ())), preferred_element_type=F32)
    lt = nt(wrt_ref[0], hi) + nt(wrt_ref[0], lo) + nt(wrt_ref[1], hi) + brc
    return hn2, _route_t(lt)


ROW_TILES = D_MODEL // LANES
CONV_PIECES = 3


def _store_row_tiles(ref, val):
    r = val.shape[0]
    for k in range(ROW_TILES):
        ref[pl.ds(k, r, stride=ROW_TILES), :] = val[:, k * LANES:(k + 1) * LANES]


def _load_row_tiles(ref):
    r = ref.shape[0] // ROW_TILES
    return jnp.concatenate([ref[pl.ds(k, r, stride=ROW_TILES), :] for k in range(ROW_TILES)], axis=1)


def _prompt_step(
        t, nt, fill, pz, pdt, puv, convw_ref, convb_ref, dtb_ref,
        alog_ref, dskip_ref, ssmnw_ref, sgnw_ref, sgw_ref, sgb_ref, convt_ref, ssm_ref,
        xbc_scr, act_scr, state_scr, y_scr, yssm_scr, ysg_scr, tl):
    uv = _gelu(puv[...])
    u = uv[:, :D_SG]
    v = _rms(uv[:, D_SG:], sgnw_ref[...])
    vb = v.astype(BF16)
    for c in range(tl // CHUNK):
        rows = slice(c * CHUNK, (c + 1) * CHUNK)
        for hd in range(SG_HEADS):
            hl = slice(hd * SG_HEAD_DIM, (hd + 1) * SG_HEAD_DIM)
            sv = _dot(sgw_ref[hd], vb[rows, hl]) + sgb_ref[:, hl]
            ysg_scr[rows, hl] = (u[rows, hl] * sv).astype(BF16)
    fill['gmlp']()

    width = C_CONV // CONV_PIECES
    for p in range(CONV_PIECES):
        cols = slice(p * width, (p + 1) * width)
        conv = convb_ref[:, cols]
        for k in range(CONV_K):
            off = SUBLANES - (CONV_K - 1) + k
            conv = conv + convw_ref[k:k + 1, cols] * xbc_scr[off:off + tl, cols]
        act_scr[:, cols] = _silu(conv)
        fill['conv%d' % p]()
    tail = xbc_scr[tl:tl + SUBLANES, :]
    xbc_scr[0:SUBLANES, :] = tail
    convt_ref[...] = tail
    b_col, c_col = D_SSM, D_SSM + SSM_GROUPS * SSM_STATE
    fill['conv']()

    dt = _softplus(pdt[...] + dtb_ref[...])
    a_row = -jnp.exp(alog_ref[...])
    fill['dt']()

    li = lax.broadcasted_iota(jnp.int32, (CHUNK, CHUNK), 0)
    si = lax.broadcasted_iota(jnp.int32, (CHUNK, CHUNK), 1)
    causal = li >= si
    tri = jnp.where(causal, 1.0, 0.0).astype(F32)
    lo_half = si < SSM_HEAD_DIM

    for c in range(tl // CHUNK):
        rows = slice(c * CHUNK, (c + 1) * CHUNK)
        dt_c = dt[rows]
        acum = _dot_f32(tri, dt_c * a_row)
        acum_t = acum.T
        dt_t = dt_c.T
        w_t = dt_t * jnp.exp(acum_t[:, CHUNK - 1:CHUNK] - acum_t)
        xs_c = act_scr[rows, :D_SSM]
        for g in range(SSM_GROUPS):
            gl = slice(g * SSM_STATE, (g + 1) * SSM_STATE)
            b_g = act_scr[rows, b_col + g * SSM_STATE:b_col + (g + 1) * SSM_STATE]
            c_g = act_scr[rows, c_col + g * SSM_STATE:c_col + (g + 1) * SSM_STATE]
            b_gt = b_g.T
            cb = _dot(c_g.astype(BF16), b_gt.astype(BF16))
            for j in range(HEADS_PER_GROUP // 2):
                pl_ = slice(g * 512 + j * LANES, g * 512 + (j + 1) * LANES)
                m_l, ec_l, s_l, ea_last = [], [], [], []
                for k in range(2):
                    hh = g * HEADS_PER_GROUP + 2 * j + k
                    colb = jnp.broadcast_to(acum[:, hh:hh + 1], (CHUNK, CHUNK))
                    rowb = jnp.broadcast_to(acum_t[hh:hh + 1, :], (CHUNK, CHUNK))
                    ea = jnp.exp(colb)
                    seg = jnp.where(causal, colb - rowb, 0.0)
                    lmat = jnp.where(causal, jnp.exp(seg), 0.0)
                    m_l.append(cb * lmat * jnp.broadcast_to(dt_t[hh:hh + 1, :], (CHUNK, CHUNK)))
                    ec_l.append(c_g * ea)
                    s_l.append(b_gt * jnp.broadcast_to(w_t[hh:hh + 1, :], (CHUNK, CHUNK)))
                    ea_last.append(ea[CHUNK - 1:CHUNK, :])
                xs_p = xs_c[:, pl_]
                st_p = state_scr[g, :, j * LANES:(j + 1) * LANES]
                rx = jnp.concatenate([jnp.where(lo_half, xs_p, 0.0),
                                      jnp.where(lo_half, 0.0, xs_p)], axis=0).astype(BF16)
                rs = jnp.concatenate([jnp.where(lo_half, st_p, 0.0),
                                      jnp.where(lo_half, 0.0, st_p)], axis=0).astype(BF16)
                lhs = jnp.concatenate(m_l + ec_l, axis=1).astype(BF16)
                y_p = _dot(lhs, jnp.concatenate([rx, rs], axis=0))
                y_scr[rows, pl_] = y_p
                dec = jnp.where(lo_half[0:1, :], ea_last[0], ea_last[1])
                upd = _dot(jnp.concatenate(s_l, axis=1).astype(BF16), rx)
                state_scr[g, :, j * LANES:(j + 1) * LANES] = st_p * dec + upd
        fill['ssd%d' % c]()

    y = y_scr[...] + dskip_ref[...] * act_scr[:, :D_SSM]
    yssm_scr[...] = _gated_group_norm(y, pz[...], ssmnw_ref[...]).astype(BF16)
    fill['norm']()

    @pl.when(t == nt - 1)
    def _():
        for g in range(SSM_GROUPS):
            ssm_ref[g] = state_scr[g].T


def _prompt_mixer_kernel(
        x_ref, xp_ref, nmw_ref, wz_ref, wxbc_ref, wdt_ref, wuv_ref, convw_ref, convb_ref, dtb_ref,
        alog_ref, dskip_ref, ssmnw_ref, sgnw_ref, sgw_ref, sgb_ref, wout_ref, nfw_ref,
        wr_ref, br_ref, hs_ref, hn2s_ref, routes_ref,
        h_ref, hn2_ref, route_ref, convt_ref, ssm_ref,
        xbc_scr, act_scr, state_scr, y_scr, yssm_scr, ysg_scr, h_scr, hn_scr, pz, pdt, puv,
        *, tl, nt, n_main):
    i = pl.program_id(0)

    @pl.when(i == 0)
    def _():
        for ref in (xbc_scr, act_scr, state_scr, yssm_scr, ysg_scr, h_scr, pz, pdt, puv):
            ref[...] = jnp.zeros(ref.shape, ref.dtype)

    @pl.when(i > n_main + 1)
    def _():
        h_ref[...] = hs_ref[...]
        hn2_ref[...] = hn2s_ref[...]
        route_ref[...] = routes_ref[...]

    half = D_MODEL // 2

    def out_sg():
        h_scr[...] = xp_ref[...] + _dot(ysg_scr[...], wout_ref[D_SSM:, :])

    def out_ssm(c):
        cols = slice(c * half, (c + 1) * half)
        h_scr[:, cols] = h_scr[:, cols] + _dot(yssm_scr[...], wout_ref[:D_SSM, cols])

    def finish():
        h = h_scr[...]
        h_ref[...] = h
        hn2, route = _ffn_front(h, nfw_ref[...], wr_ref, br_ref[...])
        _store_row_tiles(hn2_ref, hn2)
        route_ref[...] = route

    def in_uv(p):
        cols = slice(p * 512, (p + 1) * 512)
        puv[:, cols] = _dot(hn_scr[...], wuv_ref[:, cols])

    def in_xbc(p):
        cols = slice(p * 512, (p + 1) * 512)
        xbc_scr[SUBLANES:SUBLANES + tl, cols] = _dot(hn_scr[...], wxbc_ref[:, cols])

    def in_dt():
        pdt[...] = _dot(hn_scr[...], wdt_ref[...])

    def in_z():
        pz[...] = _dot(hn_scr[...], wz_ref[...])

    def seq(*fns):
        def run():
            for fn in fns:
                fn()
        return run

    @pl.when(i <= n_main + 1)
    def _():
        assert tl // CHUNK == 2
        t_mix = lax.rem(i + nt - 1, nt)

        @pl.when(t_mix == 0)
        def _():
            xbc_scr[0:SUBLANES, :] = jnp.zeros((SUBLANES, C_CONV), F32)
            state_scr[...] = jnp.zeros(state_scr.shape, F32)

        hn_scr[...] = _rms(x_ref[...], nmw_ref[...]).astype(BF16)
        out_sg()
        out_ssm(0)
        out_ssm(1)
        part = functools.partial
        assert CONV_PIECES == 3
        fill = {'gmlp': seq(finish, part(in_uv, 0)), 'conv0': part(in_uv, 1), 'conv1': part(in_uv, 2),
                'conv2': part(in_uv, 3), 'conv': lambda: None,
                'dt': part(in_xbc, 0),
                'ssd0': seq(part(in_xbc, 1), in_dt),
                'ssd1': part(in_xbc, 2),
                'norm': in_z}
        _prompt_step(
            t_mix, nt, fill, pz, pdt, puv, convw_ref, convb_ref, dtb_ref, alog_ref,
            dskip_ref, ssmnw_ref, sgnw_ref, sgw_ref, sgb_ref, convt_ref, ssm_ref,
            xbc_scr, act_scr, state_scr, y_scr, yssm_scr, ysg_scr, tl)


def _const_spec(shape, single_buffer=False):
    zeros = (0,) * len(shape)
    mode = pl.Buffered(1) if single_buffer else None
    return pl.BlockSpec(shape, lambda *_: zeros, pipeline_mode=mode)


def _prompt_mixer(x, w, tl, h_s, hn2_s, route_s):
    bsz, seq, _ = x.shape
    nt = seq // tl
    n_main = bsz * nt
    n_s = h_s.shape[0]
    assert seq % tl == 0 and n_s % tl == 0
    n_tok = bsz * seq + n_s
    x2 = x.reshape(bsz * seq, D_MODEL)
    main = lambda i: jnp.minimum(i, n_main - 1)
    mix = lambda i: jnp.clip(i - 1, 0, n_main - 1)
    prev = lambda i: jnp.clip(i - 2, 0, n_main - 1)
    tail = lambda i: jnp.maximum(i - n_main - 2, 0)
    out_row = lambda i: jnp.maximum(i - 2, 0)
    weights = [w['nmw'], w['wz'], w['wxbc'], w['wdt'], w['wuv'], w['convw'], w['convb'], w['dtb'],
               w['alog'], w['dskip'], w['ssmnw'], w['sgnw'], w['sgw'], w['sgb'], w['wout'], w['nfw'],
               w['wr'], w['br']]
    in_specs = ([pl.BlockSpec((tl, D_MODEL), lambda i: (main(i), 0)),
                 pl.BlockSpec((tl, D_MODEL), lambda i: (prev(i), 0))]
                + [_const_spec(a.shape, single_buffer=True) for a in weights]
                + [pl.BlockSpec(blk, lambda i: (tail(i), 0))
                   for blk in ((tl, D_MODEL), (tl * ROW_TILES, LANES), (tl, LANES))])
    q = HEADS_PER_GROUP * SSM_HEAD_DIM
    out_shape = [
        jax.ShapeDtypeStruct((n_tok, D_MODEL), F32),
        jax.ShapeDtypeStruct((n_tok * ROW_TILES, LANES), F32),
        jax.ShapeDtypeStruct((n_tok, LANES), F32),
        jax.ShapeDtypeStruct((bsz, SUBLANES, C_CONV), F32),
        jax.ShapeDtypeStruct((bsz, SSM_GROUPS, q, SSM_STATE), F32),
    ]
    out_specs = [
        pl.BlockSpec((tl, D_MODEL), lambda i: (out_row(i), 0)),
        pl.BlockSpec((tl * ROW_TILES, LANES), lambda i: (out_row(i), 0)),
        pl.BlockSpec((tl, LANES), lambda i: (out_row(i), 0)),
        pl.BlockSpec((None, SUBLANES, C_CONV), lambda i: (mix(i) // nt, 0, 0)),
        pl.BlockSpec((None, SSM_GROUPS, q, SSM_STATE), lambda i: (mix(i) // nt, 0, 0, 0)),
    ]
    scratch = [
        pltpu.VMEM((tl + SUBLANES, C_CONV), F32),
        pltpu.VMEM((tl, C_CONV), F32),
        pltpu.VMEM((SSM_GROUPS, SSM_STATE, q), F32),
        pltpu.VMEM((tl, D_SSM), F32),
        pltpu.VMEM((tl, D_SSM), BF16),
        pltpu.VMEM((tl, D_SG), BF16),
        pltpu.VMEM((tl, D_MODEL), F32),
        pltpu.VMEM((tl, D_MODEL), BF16),
        pltpu.VMEM((tl, D_SSM), F32),
        pltpu.VMEM((tl, LANES), F32),
        pltpu.VMEM((tl, 2 * D_SG), F32),
    ]
    return pl.pallas_call(
        functools.partial(_prompt_mixer_kernel, tl=tl, nt=nt, n_main=n_main),
        grid=(n_main + 2 + n_s // tl,),
        in_specs=in_specs,
        out_specs=out_specs,
        out_shape=out_shape,
        scratch_shapes=scratch,
        compiler_params=pltpu.CompilerParams(
            dimension_semantics=("arbitrary",), vmem_limit_bytes=VMEM_LIMIT),
        name="prompt_mixer",
    )(x2, x2, *weights, h_s, hn2_s, route_s)


def _sample_front_kernel(
        x_ref, cprev_ref, nmw_ref, wz_ref, wxbc_ref, wdt_ref, wuv_ref, convw_ref, convb_ref,
        dtb_ref, alog_ref, dskip_ref, sgnw_ref, w4_ref, b4_ref, expand_ref,
        z_ref, ypart_ref, ea_ref, ysg_ref, v_ref, convn_ref, c_ref, b_ref, xw_ref, dec_ref,
        *, nb, nt):
    x = x_ref[...]
    hn = _rms(x, nmw_ref[...]).astype(BF16)
    z_ref[...] = _dot(hn, wz_ref[...])
    xbc = _dot(hn, wxbc_ref[...])
    dtr = _dot(hn, wdt_ref[...])
    uv = _gelu(_dot(hn, wuv_ref[...]))

    slab = lambda a, t: a[t * nb:(t + 1) * nb]
    full = [cprev_ref[k] for k in range(CONV_K - 1)] + [slab(xbc, t) for t in range(nt)]
    for k in range(CONV_K - 1):
        convn_ref[k] = full[nt + k]
    xs, bm, cm = [], [], []
    for t in range(nt):
        conv = convb_ref[...]
        for k in range(CONV_K):
            conv = conv + convw_ref[k:k + 1, :] * full[t + k]
        act = _silu(conv)
        xs.append(act[:, :D_SSM])
        bm.append(act[:, D_SSM:D_SSM + SSM_GROUPS * SSM_STATE])
        cm.append(act[:, D_SSM + SSM_GROUPS * SSM_STATE:])
        b_ref[t * nb:(t + 1) * nb, :] = bm[t]
        c_ref[t * nb:(t + 1) * nb, :] = cm[t]

    dt = _softplus(dtr + dtb_ref[...])
    a_row = -jnp.exp(alog_ref[...])
    dts = [slab(dt, t) for t in range(nt)]
    cum = []
    for t in range(nt):
        da = dts[t] * a_row
        cum.append(da if t == 0 else cum[t - 1] + da)
    dec_ref[...] = jnp.exp(cum[nt - 1])

    lane = lax.broadcasted_iota(jnp.int32, (nb, LANES), 1)
    first_group = lane < HEADS_PER_GROUP
    facs = [jnp.exp(cum[t]) for t in range(nt)]
    facs += [dts[s] * jnp.exp(cum[nt - 1] - cum[s]) for s in range(nt)]
    pairs = []
    for t in range(nt):
        for s in range(t + 1):
            cb = []
            for g in range(SSM_GROUPS):
                gl = slice(g * SSM_STATE, (g + 1) * SSM_STATE)
                cb.append(jnp.sum(cm[t][:, gl] * bm[s][:, gl], axis=1, keepdims=True))
            cbh = jnp.where(first_group, cb[0], cb[1])
            facs.append(jnp.exp(cum[t] - cum[s]) * dts[s] * cbh)
            pairs.append((t, s))
    fx = _dot_f32(jnp.concatenate(facs, axis=0), expand_ref[...])
    fslab = lambda i: fx[i * nb:(i + 1) * nb]
    for t in range(nt):
        ea_ref[t * nb:(t + 1) * nb, :] = fslab(t)
        xw_ref[t * nb:(t + 1) * nb, :] = xs[t] * fslab(nt + t)
    for t in range(nt):
        acc = dskip_ref[...] * xs[t]
        for i, (tt, s) in enumerate(pairs):
            if tt == t:
                acc = acc + fslab(2 * nt + i) * xs[s]
        ypart_ref[t * nb:(t + 1) * nb, :] = acc

    u = uv[:, :D_SG]
    v = _rms(uv[:, D_SG:], sgnw_ref[...])
    v_ref[...] = v
    for t in range(nt):
        sv = b4_ref[t:t + 1, :]
        for s in range(t + 1):
            sv = sv + w4_ref[t, s:s + 1, :] * slab(v, s)
        ysg_ref[t * nb:(t + 1) * nb, :] = slab(u, t) * sv


def _sample_front(x_slab, cprev, w, nb, nt):
    r = nb * nt
    weights = [w['nmw'], w['wz'], w['wxbc'], w['wdt'], w['wuv'], w['convw'], w['convb'], w['dtb'],
               w['alog'], w['dskip'], w['sgnw'], w['w4'], w['b4'], w['expand']]
    ins = [x_slab, cprev] + weights
    out_shape = [
        jax.ShapeDtypeStruct((r, D_SSM), F32),
        jax.ShapeDtypeStruct((r, D_SSM), F32),
        jax.ShapeDtypeStruct((r, D_SSM), F32),
        jax.ShapeDtypeStruct((r, D_SG), F32),
        jax.ShapeDtypeStruct((r, D_SG), F32),
        jax.ShapeDtypeStruct((CONV_K - 1, nb, C_CONV), F32),
        jax.ShapeDtypeStruct((r, SSM_GROUPS * SSM_STATE), F32),
        jax.ShapeDtypeStruct((r, SSM_GROUPS * SSM_STATE), F32),
        jax.ShapeDtypeStruct((r, D_SSM), F32),
        jax.ShapeDtypeStruct((nb, LANES), F32),
    ]
    return pl.pallas_call(
        functools.partial(_sample_front_kernel, nb=nb, nt=nt),
        grid=(1,),
        in_specs=[_const_spec(a.shape) for a in ins],
        out_specs=[_const_spec(s.shape) for s in out_shape],
        out_shape=out_shape,
        compiler_params=pltpu.CompilerParams(
            dimension_semantics=("arbitrary",), vmem_limit_bytes=VMEM_LIMIT),
        name="sample_front",
    )(*ins)


def _sample_state_kernel(dec_ref, st_ref, cq_ref, bq_ref, xw_ref, so_ref, z_ref, *, seqs):
    for bb in range(seqs):
        for g in range(SSM_GROUPS):
            s0 = st_ref[bb, g]
            z_ref[bb, g] = lax.dot_general(
                cq_ref[bb, g].astype(BF16), s0.astype(BF16), (((1,), (1,)), ((), ())),
                preferred_element_type=F32)
            upd = lax.dot_general(
                xw_ref[bb, g].astype(BF16), bq_ref[bb, g].astype(BF16), (((0,), (0,)), ((), ())),
                preferred_element_type=F32)
            for hh in range(HEADS_PER_GROUP):
                hs = slice(hh * SSM_HEAD_DIM, (hh + 1) * SSM_HEAD_DIM)
                so_ref[bb, g, hs, :] = s0[hs] * dec_ref[bb, g * HEADS_PER_GROUP + hh] + upd[hs]


def _sample_state(dec, state, cq, bq, xwq, seqs):
    nb = state.shape[0]
    q = HEADS_PER_GROUP * SSM_HEAD_DIM
    blk = lambda *tail: pl.BlockSpec((seqs, SSM_GROUPS) + tail, lambda i: (i, 0, 0, 0))
    return pl.pallas_call(
        functools.partial(_sample_state_kernel, seqs=seqs),
        grid=(nb // seqs,),
        in_specs=[
            pl.BlockSpec((seqs, SSM_HEADS), lambda i: (i, 0), memory_space=pltpu.SMEM),
            blk(q, SSM_STATE), blk(SUBLANES, SSM_STATE), blk(SUBLANES, SSM_STATE), blk(SUBLANES, q),
        ],
        out_specs=[blk(q, SSM_STATE), blk(SUBLANES, q)],
        out_shape=[
            jax.ShapeDtypeStruct((nb, SSM_GROUPS, q, SSM_STATE), F32),
            jax.ShapeDtypeStruct((nb, SSM_GROUPS, SUBLANES, q), F32),
        ],
        compiler_params=pltpu.CompilerParams(
            dimension_semantics=("arbitrary",), vmem_limit_bytes=VMEM_LIMIT),
        name="sample_state",
    )(dec, state, cq, bq, xwq)


def _sample_back_kernel(x_ref, ypart_ref, ea_ref, zoff_ref, z_ref, ysg_ref, ssmnw_ref, wout_ref,
                        nfw_ref, wr_ref, br_ref, h_ref, hn2_ref, route_ref):
    y = ypart_ref[...] + ea_ref[...] * zoff_ref[...]
    y_ssm = _gated_group_norm(y, z_ref[...], ssmnw_ref[...])
    mix_in = jnp.concatenate([y_ssm.astype(BF16), ysg_ref[...].astype(BF16)], axis=1)
    h = x_ref[...] + _dot(mix_in, wout_ref[...])
    h_ref[...] = h
    hn2, route = _ffn_front(h, nfw_ref[...], wr_ref, br_ref[...])
    _store_row_tiles(hn2_ref, hn2)
    route_ref[...] = route


def _sample_back(x_slab, ypart, ea, zoff, z, ysg, w):
    r = x_slab.shape[0]
    ins = [x_slab, ypart, ea, zoff, z, ysg, w['ssmnw'], w['wout'], w['nfw'], w['wr'], w['br']]
    out_shape = [
        jax.ShapeDtypeStruct((r, D_MODEL), F32),
        jax.ShapeDtypeStruct((r * ROW_TILES, LANES), F32),
        jax.ShapeDtypeStruct((r, LANES), F32),
    ]
    return pl.pallas_call(
        _sample_back_kernel,
        grid=(1,),
        in_specs=[_const_spec(a.shape) for a in ins],
        out_specs=[_const_spec(s.shape) for s in out_shape],
        out_shape=out_shape,
        compiler_params=pltpu.CompilerParams(
            dimension_semantics=("arbitrary",), vmem_limit_bytes=VMEM_LIMIT),
        name="sample_back",
    )(*ins)


EXPERT_PIECES = 8
DMA_PIECES = 4


def _expert_mlp_kernel(be_ref, nu_ref, tok0_ref, tokn_ref, dstp_ref, x_hbm, wg_ref, wu_ref, wd_ref,
                       parts_hbm, xbuf, obuf, xb, hb, wg_b, wu_b, wd_b, gsem, ssem, *, bm):
    j = pl.program_id(0)
    nu = nu_ref[0]
    nt = ROW_TILES
    tile = lambda ref, start: ref.at[pl.ds(pl.multiple_of(start, nt), nt)]

    def gather_copy(idx_ref, s, r):
        return pltpu.make_async_copy(tile(x_hbm, idx_ref[0, r]), xbuf.at[s, pl.ds(r * nt, nt)], gsem.at[s])

    def scatter_copy(s, r):
        return pltpu.make_async_copy(obuf.at[s, pl.ds(r * nt, nt)], tile(parts_hbm, dstp_ref[0, r]),
                                     ssem.at[s])

    def gather_wait(s):
        pltpu.make_async_copy(x_hbm.at[pl.ds(0, bm * nt)], xbuf.at[s], gsem.at[s]).wait()

    def scatter_wait(s):
        pltpu.make_async_copy(obuf.at[s], parts_hbm.at[pl.ds(0, bm * nt)], ssem.at[s]).wait()

    def step(slot):
        gather_wait(slot)

        @pl.when(j >= 1)
        def _():
            scatter_wait(slot)

        per = bm // DMA_PIECES

        def start_rows(piece):
            if piece >= DMA_PIECES:
                return
            for r in range(piece * per, (piece + 1) * per):
                gather_copy(tokn_ref, 1 - slot, r).start(priority=r % 2)
                scatter_copy(1 - slot, r).start(priority=(r + 1) % 2)

        xb[...] = _load_row_tiles(xbuf.at[slot]).astype(BF16)
        half = D_FF // 2
        for c in range(2):
            cols = slice(c * half, (c + 1) * half)
            start_rows(2 * c)
            g = _dot(xb[...], wg_b[:, cols])
            start_rows(2 * c + 1)
            u = _dot(xb[...], wu_b[:, cols])
            hb[:, cols] = (_silu(g) * u).astype(BF16)
        quarter = D_MODEL // 4
        for c in range(4):
            start_rows(4 + c)
            o = _dot(hb[...], wd_b[:, c * quarter:(c + 1) * quarter])
            for k in range(quarter // LANES):
                kk = c * (quarter // LANES) + k
                obuf.at[slot][pl.ds(kk, bm, stride=nt), :] = o[:, k * LANES:(k + 1) * LANES]

    def drain(slot):
        gather_wait(slot)
        scatter_wait(slot)
        for r in range(bm):
            scatter_copy(1 - slot, r).start(priority=r % 2)
        scatter_wait(1 - slot)

    @pl.when(j == 0)
    def _():
        obuf[...] = jnp.zeros(obuf.shape, F32)
        n_real = parts_hbm.shape[0] - bm * nt
        spare = pltpu.make_async_copy(obuf.at[0], parts_hbm.at[pl.ds(n_real, bm * nt)], ssem.at[0])
        spare.start()
        spare.wait()

        def start0(r, carry):
            pltpu.make_async_copy(tile(x_hbm, tok0_ref[0, r]), xbuf.at[0, pl.ds(pl.multiple_of(r * nt, nt), nt)],
                                  gsem.at[0]).start()
            return carry
        lax.fori_loop(0, bm, start0, 0)

    used = j < nu

    @pl.when(used & ((j == 0) | (be_ref[j] != be_ref[jnp.maximum(j - 1, 0)])))
    def _():
        wg_b[...] = wg_ref[...].astype(BF16)
        wu_b[...] = wu_ref[...].astype(BF16)
        wd_b[...] = wd_ref[...].astype(BF16)

    for slot in range(2):
        parity = lax.rem(j, 2) == slot

        @pl.when(used & parity)
        def _(slot=slot):
            step(slot)

        @pl.when((j == nu) & parity)
        def _(slot=slot):
            drain(slot)


def _expert_mlp(blk_expert, n_used, tok, dst, x, wg, wu, wd, bm, n_out):
    n_blocks = blk_expert.shape[0]
    assert bm % EXPERT_PIECES == 0
    w_map = lambda j, be, nu: (be[j], 0, 0)
    idx_spec = lambda f: pl.BlockSpec((None, 1, bm), lambda j, be, nu: (f(j), 0, 0),
                                      memory_space=pltpu.SMEM)
    any_spec = pl.BlockSpec(memory_space=pl.ANY)
    grid_spec = pltpu.PrefetchScalarGridSpec(
        num_scalar_prefetch=2,
        grid=(n_blocks + 1,),
        in_specs=[
            idx_spec(lambda j: 0),
            idx_spec(lambda j: jnp.minimum(j + 1, n_blocks - 1)),
            idx_spec(lambda j: j),
            any_spec,
            pl.BlockSpec((None, D_MODEL, D_FF), w_map),
            pl.BlockSpec((None, D_MODEL, D_FF), w_map),
            pl.BlockSpec((None, D_FF, D_MODEL), w_map),
        ],
        out_specs=any_spec,
        scratch_shapes=[
            pltpu.VMEM((2, bm * ROW_TILES, LANES), F32),
            pltpu.VMEM((2, bm * ROW_TILES, LANES), F32),
            pltpu.VMEM((bm, D_MODEL), BF16),
            pltpu.VMEM((bm, D_FF), BF16),
            pltpu.VMEM((D_MODEL, D_FF), BF16),
            pltpu.VMEM((D_MODEL, D_FF), BF16),
            pltpu.VMEM((D_FF, D_MODEL), BF16),
            pltpu.SemaphoreType.DMA((2,)),
            pltpu.SemaphoreType.DMA((2,)),
        ],
    )
    tok3 = (tok * ROW_TILES).reshape(n_blocks, 1, bm)
    spare = (n_out - bm + jnp.arange(bm, dtype=jnp.int32)).reshape(1, bm)
    dst_prev = (jnp.concatenate([spare, dst], axis=0) * ROW_TILES).reshape(n_blocks + 1, 1, bm)
    blk_expert = jnp.concatenate([blk_expert, blk_expert[-1:]])
    return pl.pallas_call(
        functools.partial(_expert_mlp_kernel, bm=bm),
        grid_spec=grid_spec,
        out_shape=jax.ShapeDtypeStruct((n_out * ROW_TILES, LANES), F32),
        compiler_params=pltpu.CompilerParams(
            dimension_semantics=("arbitrary",), vmem_limit_bytes=VMEM_LIMIT),
        name="expert_mlp",
    )(blk_expert, n_used, tok3, tok3, dst_prev, x, wg, wu, wd)


def _moe_plan(e_idx, bm, n_blocks):
    n_tok = e_idx.shape[0]
    n_asg = 2 * n_tok
    n_pad = n_blocks * bm - n_asg
    e_flat = e_idx.reshape(-1)
    experts = jnp.arange(N_EXPERTS, dtype=jnp.int32)
    counts = jnp.sum((e_flat[:, None] == experts[None, :]).astype(jnp.int32), axis=0)
    nblk = (counts + bm - 1) // bm
    blk_end = jnp.cumsum(nblk)
    n_used = blk_end[-1]
    j = jnp.arange(n_blocks, dtype=jnp.int32)
    be = jnp.sum((blk_end[None, :] <= jnp.minimum(j, n_used - 1)[:, None]).astype(jnp.int32), axis=1)
    be = jnp.minimum(be, N_EXPERTS - 1)
    pad_end = jnp.cumsum(nblk * bm - counts)
    pad_expert = jnp.sum((pad_end[None, :] <= jnp.arange(n_pad, dtype=jnp.int32)[:, None]).astype(jnp.int32),
                         axis=1)
    keys = jnp.concatenate([2 * e_flat, 2 * pad_expert + 1])
    ids = jnp.concatenate([jnp.arange(n_asg, dtype=jnp.int32), jnp.full((n_pad,), -1, jnp.int32)])
    _, window = lax.sort((keys, ids), num_keys=1, is_stable=True)
    window = window.reshape(n_blocks, bm)
    valid = window >= 0
    q = jnp.arange(bm, dtype=jnp.int32)
    tok = jnp.where(valid, window // 2, (j[:, None] * bm + q[None, :]) % n_tok)
    dst = jnp.where(valid, (window % 2) * n_tok + window // 2,
                    2 * n_tok + q[None, :])
    return tok.astype(jnp.int32), dst.astype(jnp.int32), be, n_used.astype(jnp.int32).reshape(1)


def _final_kernel(h_ref, p0_ref, p1_ref, route_ref, w_ref, op_ref, os_ref, *, n_prompt_steps):
    route = route_ref[...]
    p0 = _load_row_tiles(p0_ref)
    p1 = _load_row_tiles(p1_ref)
    y = _rms(h_ref[...] + route[:, 2:3] * p0 + route[:, 3:4] * p1, w_ref[...])
    i = pl.program_id(0)

    @pl.when(i < n_prompt_steps)
    def _():
        op_ref[...] = y

    @pl.when(i >= n_prompt_steps)
    def _():
        os_ref[...] = y


def _final(h, parts, route, w, tf, n_prompt):
    n_tok = h.shape[0]
    assert n_tok % tf == 0 and n_prompt % tf == 0
    n_p = n_prompt // tf
    row = lambda d: pl.BlockSpec((tf, d), lambda i: (i, 0))
    part = lambda k: pl.BlockSpec((tf * ROW_TILES, LANES), lambda i: (k * (n_tok // tf) + i, 0))
    return pl.pallas_call(
        functools.partial(_final_kernel, n_prompt_steps=n_p),
        grid=(n_tok // tf,),
        in_specs=[row(D_MODEL), part(0), part(1), row(LANES), _const_spec(w.shape)],
        out_specs=[pl.BlockSpec((tf, D_MODEL), lambda i: (jnp.minimum(i, n_p - 1), 0)),
                   pl.BlockSpec((tf, D_MODEL), lambda i: (jnp.maximum(i - n_p, 0), 0))],
        out_shape=[jax.ShapeDtypeStruct((n_prompt, D_MODEL), F32),
                   jax.ShapeDtypeStruct((n_tok - n_prompt, D_MODEL), F32)],
        compiler_params=pltpu.CompilerParams(
            dimension_semantics=("arbitrary",), vmem_limit_bytes=VMEM_LIMIT),
        name="final_combine",
    )(h, parts, parts, route, w)


PROMPT_TILE = 256
MOE_BLOCK_ROWS = 256
FINAL_TILE = 512
STATE_SEQS = 8


def _pad_lanes(a, width=LANES):
    return jnp.pad(a, [(0, 0)] * (a.ndim - 1) + [(0, width - a.shape[-1])])


def _layer_weights(l, norm_mix_w, w_in, conv_w, conv_b, dt_bias, a_log, d_skip, ssm_norm_w, sg_norm_w,
                   sg_w, sg_b, w_out, norm_ffn_w, w_router_group, b_router_group, w_router_expert,
                   b_router_expert, n_dec):
    c0, c1, c2 = D_SSM, D_SSM + C_CONV, D_SSM + C_CONV + SSM_HEADS
    wi = w_in[l]
    causal = jnp.tril(jnp.ones((CHUNK, CHUNK), bool))
    sgw_l = jnp.where(causal, sg_w[l], 0.0)
    head_of_lane = jnp.arange(D_SSM, dtype=jnp.int32) // SSM_HEAD_DIM
    expand = (jnp.arange(LANES, dtype=jnp.int32)[:, None] == head_of_lane[None, :]).astype(F32)
    row = lambda a: a.reshape(1, -1).astype(F32)
    gap = ROUTER_EXPERT_ROW - N_EXPERT_GROUPS
    rest = LANES - ROUTER_EXPERT_ROW - N_EXPERTS
    wrt = jnp.concatenate([w_router_group[l].T, jnp.zeros((gap, D_MODEL), F32), w_router_expert[l].T,
                           jnp.zeros((rest, D_MODEL), F32)], axis=0).astype(F32)
    brc = jnp.concatenate([b_router_group[l], jnp.zeros((gap,), F32), b_router_expert[l],
                           jnp.zeros((rest,), F32)]).astype(F32).reshape(LANES, 1)
    return {
        'nmw': row(norm_mix_w[l]),
        'wz': wi[:, :c0].astype(BF16),
        'wxbc': wi[:, c0:c1].astype(BF16),
        'wdt': _pad_lanes(wi[:, c1:c2]).astype(BF16),
        'wuv': wi[:, c2:].astype(BF16),
        'convw': conv_w[l].astype(F32),
        'convb': row(conv_b[l]),
        'dtb': _pad_lanes(row(dt_bias[l])),
        'alog': _pad_lanes(row(a_log[l])),
        'dskip': row(jnp.repeat(d_skip[l], SSM_HEAD_DIM)),
        'ssmnw': row(ssm_norm_w[l]),
        'sgnw': row(sg_norm_w[l]),
        'sgw': sgw_l.astype(BF16),
        'sgb': jnp.repeat(sg_b[l].T, SG_HEAD_DIM, axis=1).astype(F32),
        'w4': jnp.repeat(jnp.transpose(sgw_l[:, :n_dec, :n_dec], (1, 2, 0)), SG_HEAD_DIM, axis=2).astype(F32),
        'b4': jnp.repeat(sg_b[l][:, :n_dec].T, SG_HEAD_DIM, axis=1).astype(F32),
        'expand': expand,
        'wout': w_out[l].astype(BF16),
        'nfw': row(norm_ffn_w[l]),
        'wr': jnp.stack([wrt.astype(BF16), (wrt - wrt.astype(BF16).astype(F32)).astype(BF16)]),
        'br': brc,
    }


def _layer(l, xp, xs_slab, state_conv, state_ssm, w, w_gate, w_up, w_down):
    bp, seq, _ = xp.shape
    nb, nt = state_conv.shape[1], xs_slab.shape[0] // state_conv.shape[1]
    q = HEADS_PER_GROUP * SSM_HEAD_DIM

    cprev = jnp.transpose(state_conv[l], (1, 0, 2))
    z, ypart, ea, ysg, v, convn, cmat, bmat, xw, dec = _sample_front(xs_slab, cprev, w, nb, nt)
    to_seq = lambda a, d: jnp.pad(
        jnp.transpose(a.reshape(nt, nb, SSM_GROUPS, d), (1, 2, 0, 3)),
        ((0, 0), (0, 0), (0, SUBLANES - nt), (0, 0)))
    ssm_s, zoff = _sample_state(
        dec[:, :SSM_HEADS], state_ssm[l].reshape(nb, SSM_GROUPS, q, SSM_STATE),
        to_seq(cmat, SSM_STATE), to_seq(bmat, SSM_STATE), to_seq(xw, q), STATE_SEQS)
    zoff = jnp.transpose(zoff[:, :, :nt], (2, 0, 1, 3)).reshape(nt * nb, D_SSM)
    h_s, hn2_s, route_s = _sample_back(xs_slab, ypart, ea, zoff, z, ysg, w)

    h, hn2, route, convt_p, ssm_p = _prompt_mixer(xp, w, PROMPT_TILE, h_s, hn2_s, route_s)

    n_tok = h.shape[0]
    bm = MOE_BLOCK_ROWS
    n_blocks = -(-(2 * n_tok) // bm) + N_EXPERTS
    tok, dst, blk_expert, n_used = _moe_plan(route[:, :2].astype(jnp.int32), bm, n_blocks)
    parts = _expert_mlp(blk_expert, n_used, tok, dst, hn2, w_gate[l], w_up[l], w_down[l], bm,
                        2 * n_tok + bm)
    outs = dict(
        h=h, parts=parts, route=route,
        conv_p=convt_p[:, SUBLANES - (CONV_K - 1):],
        ssm_p=ssm_p.reshape(bp, SSM_GROUPS, HEADS_PER_GROUP, SSM_HEAD_DIM, SSM_STATE),
        conv_s=jnp.transpose(convn, (1, 0, 2)),
        ssm_s=ssm_s.reshape(nb, SSM_GROUPS, HEADS_PER_GROUP, SSM_HEAD_DIM, SSM_STATE),
        v_s=jnp.transpose(v.reshape(nt, nb, D_SG), (1, 0, 2)),
    )
    return outs


def kernel(x_prompt, x_sample, state_conv, state_ssm, norm_mix_w, w_in, conv_w, conv_b, dt_bias, a_log, d_skip, ssm_norm_w, sg_norm_w, sg_w, sg_b, w_out, norm_ffn_w, w_router_group, b_router_group, w_router_expert, b_router_expert, w_gate, w_up, w_down, norm_final_w):
    depth = w_in.shape[0]
    assert depth == 1, "the fused final norm assumes a single layer"
    bp, seq, _ = x_prompt.shape
    nb, nt, _ = x_sample.shape
    l = 0
    w = _layer_weights(l, norm_mix_w, w_in, conv_w, conv_b, dt_bias, a_log, d_skip, ssm_norm_w, sg_norm_w,
                       sg_w, sg_b, w_out, norm_ffn_w, w_router_group, b_router_group, w_router_expert,
                       b_router_expert, nt)
    xs_slab = jnp.transpose(x_sample, (1, 0, 2)).reshape(nt * nb, D_MODEL)
    outs = _layer(l, x_prompt, xs_slab, state_conv, state_ssm, w, w_gate, w_up, w_down)
    y_p, y_s = _final(outs['h'], outs['parts'], outs['route'], norm_final_w.reshape(1, -1).astype(F32),
                      FINAL_TILE, bp * seq)
    y_prompt = y_p.reshape(bp, seq, D_MODEL)
    y_sample = jnp.transpose(y_s.reshape(nt, nb, D_MODEL), (1, 0, 2))
    return (y_prompt, y_sample, outs['conv_p'][None], outs['ssm_p'][None], outs['conv_s'][None],
            outs['ssm_s'][None], outs['v_s'][None])
```

```python
import functools

import jax
import jax.numpy as jnp
from jax import lax
from jax.experimental import pallas as pl
from jax.experimental.pallas import tpu as pltpu

D_MODEL = 1024
D_SSM = 1024
SSM_HEAD_DIM = 64
SSM_HEADS = 16
SSM_GROUPS = 2
HEADS_PER_GROUP = 8
SSM_STATE = 128
CONV_K = 4
C_CONV = D_SSM + 2 * SSM_GROUPS * SSM_STATE
D_SG = 1024
SG_HEADS = 8
SG_HEAD_DIM = 128
CHUNK = 128
N_EXPERT_GROUPS = 4
EXPERTS_PER_GROUP = 8
N_EXPERTS = 32
D_FF = 512
EPS = 1e-6

LANES = 128
SUBLANES = 8
VMEM_LIMIT = 56 * 1024 * 1024

F32 = jnp.float32
BF16 = jnp.bfloat16
HIGHEST = lax.Precision.HIGHEST


def _dot(a, b):
    return jnp.dot(a, b, preferred_element_type=F32)


def _dot_f32(a, b):
    return jnp.dot(a, b, preferred_element_type=F32, precision=HIGHEST)


def _rms(x, w):
    return x * lax.rsqrt(jnp.mean(x * x, axis=-1, keepdims=True) + EPS) * w


def _silu(x):
    return x * (1.0 / (1.0 + jnp.exp(-x)))


def _gelu(x):
    return 0.5 * x * (1.0 + lax.erf(x * 0.7071067811865476))


def _softplus(x):
    return jnp.maximum(x, 0.0) + jnp.log1p(jnp.exp(-jnp.abs(x)))


def _gated_group_norm(y, z, w):
    g = y * _silu(z)
    half = D_SSM // SSM_GROUPS
    parts = []
    for k in range(SSM_GROUPS):
        gk = g[:, k * half:(k + 1) * half]
        parts.append(gk * lax.rsqrt(jnp.mean(gk * gk, axis=-1, keepdims=True) + EPS))
    return jnp.concatenate(parts, axis=1) * w


ROUTER_GROUP_ROW = 0
ROUTER_EXPERT_ROW = SUBLANES


def _route_t(lt):
    r = lt.shape[1]
    row = lax.broadcasted_iota(jnp.int32, (SUBLANES, r), 0)
    rowf = row.astype(F32)
    big = float(SUBLANES)
    red = lambda f, a: f(a, axis=0, keepdims=True)
    gl = jnp.where(row < N_EXPERT_GROUPS, lt[ROUTER_GROUP_ROW:ROUTER_GROUP_ROW + SUBLANES], -jnp.inf)
    ge = jnp.exp(gl - red(jnp.max, gl))
    p_grp = ge / red(jnp.sum, ge)
    g_p = red(jnp.max, p_grp)
    g_idx = red(jnp.min, jnp.where(p_grp == g_p, rowf, big))
    el = lt[ROUTER_EXPERT_ROW:ROUTER_EXPERT_ROW + EXPERTS_PER_GROUP]
    for g in range(1, N_EXPERT_GROUPS):
        lo = ROUTER_EXPERT_ROW + g * EXPERTS_PER_GROUP
        el = jnp.where(g_idx == float(g), lt[lo:lo + EXPERTS_PER_GROUP], el)
    ee = jnp.exp(el - red(jnp.max, el))
    pe = ee / red(jnp.sum, ee)
    v1 = red(jnp.max, pe)
    i1 = red(jnp.min, jnp.where(pe == v1, rowf, big))
    rest = rowf != i1
    pe2 = jnp.where(rest, pe, -1.0)
    v2 = red(jnp.max, pe2)
    i2 = red(jnp.min, jnp.where(rest & (pe2 == v2), rowf, big))
    den = v1 + v2
    base = g_idx * float(EXPERTS_PER_GROUP)
    out = jnp.where(row == 0, base + i1, 0.0)
    out = jnp.where(row == 1, base + i2, out)
    out = jnp.where(row == 2, g_p * v1 / den, out)
    out = jnp.where(row == 3, g_p * v2 / den, out)
    return jnp.concatenate([out, jnp.zeros((LANES - SUBLANES, r), F32)], axis=0).T


def _ffn_front(h, nfw, wrt_ref, brc):
    hn2 = _rms(h, nfw)
    hi = hn2.astype(BF16)
    lo = (hn2 - hi.astype(F32)).astype(BF16)
    nt = lambda a, b: lax.dot_general(a, b, (((1,), (1,)), ((), ())), preferred_element_type=F32)
    lt = nt(wrt_ref[0], hi) + nt(wrt_ref[0], lo) + nt(wrt_ref[1], hi) + brc
    return hn2, _route_t(lt)


ROW_TILES = D_MODEL // LANES
CONV_PIECES = 3


def _store_row_tiles(ref, val):
    r = val.shape[0]
    for k in range(ROW_TILES):
        ref[pl.ds(k, r, stride=ROW_TILES), :] = val[:, k * LANES:(k + 1) * LANES]


def _load_row_tiles(ref):
    r = ref.shape[0] // ROW_TILES
    return jnp.concatenate([ref[pl.ds(k, r, stride=ROW_TILES), :] for k in range(ROW_TILES)], axis=1)


def _prompt_step(
        t, nt, fill, pz, pdt, puv, convw_ref, convb_ref, dtb_ref,
        alog_ref, dskip_ref, ssmnw_ref, sgnw_ref, sgw_ref, sgb_ref, convt_ref, ssm_ref,
        xbc_scr, act_scr, state_scr, y_scr, yssm_scr, ysg_scr, tl):
    uv = _gelu(puv[...])
    u = uv[:, :D_SG]
    v = _rms(uv[:, D_SG:], sgnw_ref[...])
    vb = v.astype(BF16)
    for c in range(tl // CHUNK):
        rows = slice(c * CHUNK, (c + 1) * CHUNK)
        for hd in range(SG_HEADS):
            hl = slice(hd * SG_HEAD_DIM, (hd + 1) * SG_HEAD_DIM)
            sv = _dot(sgw_ref[hd], vb[rows, hl]) + sgb_ref[:, hl]
            ysg_scr[rows, hl] = (u[rows, hl] * sv).astype(BF16)
    fill['gmlp']()

    width = C_CONV // CONV_PIECES
    for p in range(CONV_PIECES):
        cols = slice(p * width, (p + 1) * width)
        conv = convb_ref[:, cols]
        for k in range(CONV_K):
            off = SUBLANES - (CONV_K - 1) + k
            conv = conv + convw_ref[k:k + 1, cols] * xbc_scr[off:off + tl, cols]
        act_scr[:, cols] = _silu(conv)
        fill['conv%d' % p]()
    tail = xbc_scr[tl:tl + SUBLANES, :]
    xbc_scr[0:SUBLANES, :] = tail
    convt_ref[...] = tail
    b_col, c_col = D_SSM, D_SSM + SSM_GROUPS * SSM_STATE
    fill['conv']()

    dt = _softplus(pdt[...] + dtb_ref[...])
    a_row = -jnp.exp(alog_ref[...])
    fill['dt']()

    li = lax.broadcasted_iota(jnp.int32, (CHUNK, CHUNK), 0)
    si = lax.broadcasted_iota(jnp.int32, (CHUNK, CHUNK), 1)
    causal = li >= si
    tri = jnp.where(causal, 1.0, 0.0).astype(F32)
    lo_half = si < SSM_HEAD_DIM

    for c in range(tl // CHUNK):
        rows = slice(c * CHUNK, (c + 1) * CHUNK)
        dt_c = dt[rows]
        acum = _dot_f32(tri, dt_c * a_row)
        acum_t = acum.T
        dt_t = dt_c.T
        w_t = dt_t * jnp.exp(acum_t[:, CHUNK - 1:CHUNK] - acum_t)
        xs_c = act_scr[rows, :D_SSM]
        for g in range(SSM_GROUPS):
            gl = slice(g * SSM_STATE, (g + 1) * SSM_STATE)
            b_g = act_scr[rows, b_col + g * SSM_STATE:b_col + (g + 1) * SSM_STATE]
            c_g = act_scr[rows, c_col + g * SSM_STATE:c_col + (g + 1) * SSM_STATE]
            b_gt = b_g.T
            cb = _dot(c_g.astype(BF16), b_gt.astype(BF16))
            for j in range(HEADS_PER_GROUP // 2):
                pl_ = slice(g * 512 + j * LANES, g * 512 + (j + 1) * LANES)
                m_l, ec_l, s_l, ea_last = [], [], [], []
                for k in range(2):
                    hh = g * HEADS_PER_GROUP + 2 * j + k
                    colb = jnp.broadcast_to(acum[:, hh:hh + 1], (CHUNK, CHUNK))
                    rowb = jnp.broadcast_to(acum_t[hh:hh + 1, :], (CHUNK, CHUNK))
                    ea = jnp.exp(colb)
                    seg = jnp.where(causal, colb - rowb, 0.0)
                    lmat = jnp.where(causal, jnp.exp(seg), 0.0)
                    m_l.append(cb * lmat * jnp.broadcast_to(dt_t[hh:hh + 1, :], (CHUNK, CHUNK)))
                    ec_l.append(c_g * ea)
                    s_l.append(b_gt * jnp.broadcast_to(w_t[hh:hh + 1, :], (CHUNK, CHUNK)))
                    ea_last.append(ea[CHUNK - 1:CHUNK, :])
                xs_p = xs_c[:, pl_]
                st_p = state_scr[g, :, j * LANES:(j + 1) * LANES]
                rx = jnp.concatenate([jnp.where(lo_half, xs_p, 0.0),
                                      jnp.where(lo_half, 0.0, xs_p)], axis=0).astype(BF16)
                rs = jnp.concatenate([jnp.where(lo_half, st_p, 0.0),
                                      jnp.where(lo_half, 0.0, st_p)], axis=0).astype(BF16)
                lhs = jnp.concatenate(m_l + ec_l, axis=1).astype(BF16)
                y_p = _dot(lhs, jnp.concatenate([rx, rs], axis=0))
                y_scr[rows, pl_] = y_p
                dec = jnp.where(lo_half[0:1, :], ea_last[0], ea_last[1])
                upd = _dot(jnp.concatenate(s_l, axis=1).astype(BF16), rx)
                state_scr[g, :, j * LANES:(j + 1) * LANES] = st_p * dec + upd
        fill['ssd%d' % c]()

    y = y_scr[...] + dskip_ref[...] * act_scr[:, :D_SSM]
    yssm_scr[...] = _gated_group_norm(y, pz[...], ssmnw_ref[...]).astype(BF16)
    fill['norm']()

    @pl.when(t == nt - 1)
    def _():
        for g in range(SSM_GROUPS):
            ssm_ref[g] = state_scr[g].T


def _prompt_mixer_kernel(
        x_ref, xp_ref, nmw_ref, wz_ref, wxbc_ref, wdt_ref, wuv_ref, convw_ref, convb_ref, dtb_ref,
        alog_ref, dskip_ref, ssmnw_ref, sgnw_ref, sgw_ref, sgb_ref, wout_ref, nfw_ref,
        wr_ref, br_ref, hs_ref, hn2s_ref, routes_ref,
        h_ref, hn2_ref, route_ref, convt_ref, ssm_ref,
        xbc_scr, act_scr, state_scr, y_scr, yssm_scr, ysg_scr, h_scr, hn_scr, pz, pdt, puv,
        *, tl, nt, n_main):
    i = pl.program_id(0)

    @pl.when(i == 0)
    def _():
        for ref in (xbc_scr, act_scr, state_scr, yssm_scr, ysg_scr, h_scr, pz, pdt, puv):
            ref[...] = jnp.zeros(ref.shape, ref.dtype)

    @pl.when(i > n_main + 1)
    def _():
        h_ref[...] = hs_ref[...]
        hn2_ref[...] = hn2s_ref[...]
        route_ref[...] = routes_ref[...]

    half = D_MODEL // 2

    def out_sg():
        h_scr[...] = xp_ref[...] + _dot(ysg_scr[...], wout_ref[D_SSM:, :])

    def out_ssm(c):
        cols = slice(c * half, (c + 1) * half)
        h_scr[:, cols] = h_scr[:, cols] + _dot(yssm_scr[...], wout_ref[:D_SSM, cols])

    def finish():
        h = h_scr[...]
        h_ref[...] = h
        hn2, route = _ffn_front(h, nfw_ref[...], wr_ref, br_ref[...])
        _store_row_tiles(hn2_ref, hn2)
        route_ref[...] = route

    def in_uv(p):
        cols = slice(p * 512, (p + 1) * 512)
        puv[:, cols] = _dot(hn_scr[...], wuv_ref[:, cols])

    def in_xbc(p):
        cols = slice(p * 512, (p + 1) * 512)
        xbc_scr[SUBLANES:SUBLANES + tl, cols] = _dot(hn_scr[...], wxbc_ref[:, cols])

    def in_dt():
        pdt[...] = _dot(hn_scr[...], wdt_ref[...])

    def in_z():
        pz[...] = _dot(hn_scr[...], wz_ref[...])

    def seq(*fns):
        def run():
            for fn in fns:
                fn()
        return run

    @pl.when(i <= n_main + 1)
    def _():
        assert tl // CHUNK == 2
        t_mix = lax.rem(i + nt - 1, nt)

        @pl.when(t_mix == 0)
        def _():
            xbc_scr[0:SUBLANES, :] = jnp.zeros((SUBLANES, C_CONV), F32)
            state_scr[...] = jnp.zeros(state_scr.shape, F32)

        hn_scr[...] = _rms(x_ref[...], nmw_ref[...]).astype(BF16)
        out_sg()
        out_ssm(0)
        out_ssm(1)
        part = functools.partial
        assert CONV_PIECES == 3
        fill = {'gmlp': seq(finish, part(in_uv, 0)), 'conv0': part(in_uv, 1), 'conv1': part(in_uv, 2),
                'conv2': part(in_uv, 3), 'conv': lambda: None,
                'dt': part(in_xbc, 0),
                'ssd0': seq(part(in_xbc, 1), in_dt),
                'ssd1': part(in_xbc, 2),
                'norm': in_z}
        _prompt_step(
            t_mix, nt, fill, pz, pdt, puv, convw_ref, convb_ref, dtb_ref, alog_ref,
            dskip_ref, ssmnw_ref, sgnw_ref, sgw_ref, sgb_ref, convt_ref, ssm_ref,
            xbc_scr, act_scr, state_scr, y_scr, yssm_scr, ysg_scr, tl)


def _const_spec(shape, single_buffer=False):
    zeros = (0,) * len(shape)
    mode = pl.Buffered(1) if single_buffer else None
    return pl.BlockSpec(shape, lambda *_: zeros, pipeline_mode=mode)


def _prompt_mixer(x, w, tl, h_s, hn2_s, route_s):
    bsz, seq, _ = x.shape
    nt = seq // tl
    n_main = bsz * nt
    n_s = h_s.shape[0]
    assert seq % tl == 0 and n_s % tl == 0
    n_tok = bsz * seq + n_s
    x2 = x.reshape(bsz * seq, D_MODEL)
    main = lambda i: jnp.minimum(i, n_main - 1)
    mix = lambda i: jnp.clip(i - 1, 0, n_main - 1)
    prev = lambda i: jnp.clip(i - 2, 0, n_main - 1)
    tail = lambda i: jnp.maximum(i - n_main - 2, 0)
    out_row = lambda i: jnp.maximum(i - 2, 0)
    weights = [w['nmw'], w['wz'], w['wxbc'], w['wdt'], w['wuv'], w['convw'], w['convb'], w['dtb'],
               w['alog'], w['dskip'], w['ssmnw'], w['sgnw'], w['sgw'], w['sgb'], w['wout'], w['nfw'],
               w['wr'], w['br']]
    in_specs = ([pl.BlockSpec((tl, D_MODEL), lambda i: (main(i), 0)),
                 pl.BlockSpec((tl, D_MODEL), lambda i: (prev(i), 0))]
                + [_const_spec(a.shape, single_buffer=True) for a in weights]
                + [pl.BlockSpec(blk, lambda i: (tail(i), 0))
                   for blk in ((tl, D_MODEL), (tl * ROW_TILES, LANES), (tl, LANES))])
    q = HEADS_PER_GROUP * SSM_HEAD_DIM
    out_shape = [
        jax.ShapeDtypeStruct((n_tok, D_MODEL), F32),
        jax.ShapeDtypeStruct((n_tok * ROW_TILES, LANES), F32),
        jax.ShapeDtypeStruct((n_tok, LANES), F32),
        jax.ShapeDtypeStruct((bsz, SUBLANES, C_CONV), F32),
        jax.ShapeDtypeStruct((bsz, SSM_GROUPS, q, SSM_STATE), F32),
    ]
    out_specs = [
        pl.BlockSpec((tl, D_MODEL), lambda i: (out_row(i), 0)),
        pl.BlockSpec((tl * ROW_TILES, LANES), lambda i: (out_row(i), 0)),
        pl.BlockSpec((tl, LANES), lambda i: (out_row(i), 0)),
        pl.BlockSpec((None, SUBLANES, C_CONV), lambda i: (mix(i) // nt, 0, 0)),
        pl.BlockSpec((None, SSM_GROUPS, q, SSM_STATE), lambda i: (mix(i) // nt, 0, 0, 0)),
    ]
    scratch = [
        pltpu.VMEM((tl + SUBLANES, C_CONV), F32),
        pltpu.VMEM((tl, C_CONV), F32),
        pltpu.VMEM((SSM_GROUPS, SSM_STATE, q), F32),
        pltpu.VMEM((tl, D_SSM), F32),
        pltpu.VMEM((tl, D_SSM), BF16),
        pltpu.VMEM((tl, D_SG), BF16),
        pltpu.VMEM((tl, D_MODEL), F32),
        pltpu.VMEM((tl, D_MODEL), BF16),
        pltpu.VMEM((tl, D_SSM), F32),
        pltpu.VMEM((tl, LANES), F32),
        pltpu.VMEM((tl, 2 * D_SG), F32),
    ]
    return pl.pallas_call(
        functools.partial(_prompt_mixer_kernel, tl=tl, nt=nt, n_main=n_main),
        grid=(n_main + 2 + n_s // tl,),
        in_specs=in_specs,
        out_specs=out_specs,
        out_shape=out_shape,
        scratch_shapes=scratch,
        compiler_params=pltpu.CompilerParams(
            dimension_semantics=("arbitrary",), vmem_limit_bytes=VMEM_LIMIT),
        name="prompt_mixer",
    )(x2, x2, *weights, h_s, hn2_s, route_s)


def _sample_front_kernel(
        x_ref, cprev_ref, nmw_ref, wz_ref, wxbc_ref, wdt_ref, wuv_ref, convw_ref, convb_ref,
        dtb_ref, alog_ref, dskip_ref, sgnw_ref, w4_ref, b4_ref, expand_ref,
        z_ref, ypart_ref, ea_ref, ysg_ref, v_ref, convn_ref, c_ref, b_ref, xw_ref, dec_ref,
        *, nb, nt):
    x = x_ref[...]
    hn = _rms(x, nmw_ref[...]).astype(BF16)
    z_ref[...] = _dot(hn, wz_ref[...])
    xbc = _dot(hn, wxbc_ref[...])
    dtr = _dot(hn, wdt_ref[...])
    uv = _gelu(_dot(hn, wuv_ref[...]))

    slab = lambda a, t: a[t * nb:(t + 1) * nb]
    full = [cprev_ref[k] for k in range(CONV_K - 1)] + [slab(xbc, t) for t in range(nt)]
    for k in range(CONV_K - 1):
        convn_ref[k] = full[nt + k]
    xs, bm, cm = [], [], []
    for t in range(nt):
        conv = convb_ref[...]
        for k in range(CONV_K):
            conv = conv + convw_ref[k:k + 1, :] * full[t + k]
        act = _silu(conv)
        xs.append(act[:, :D_SSM])
        bm.append(act[:, D_SSM:D_SSM + SSM_GROUPS * SSM_STATE])
        cm.append(act[:, D_SSM + SSM_GROUPS * SSM_STATE:])
        b_ref[t * nb:(t + 1) * nb, :] = bm[t]
        c_ref[t * nb:(t + 1) * nb, :] = cm[t]

    dt = _softplus(dtr + dtb_ref[...])
    a_row = -jnp.exp(alog_ref[...])
    dts = [slab(dt, t) for t in range(nt)]
    cum = []
    for t in range(nt):
        da = dts[t] * a_row
        cum.append(da if t == 0 else cum[t - 1] + da)
    dec_ref[...] = jnp.exp(cum[nt - 1])

    lane = lax.broadcasted_iota(jnp.int32, (nb, LANES), 1)
    first_group = lane < HEADS_PER_GROUP
    facs = [jnp.exp(cum[t]) for t in range(nt)]
    facs += [dts[s] * jnp.exp(cum[nt - 1] - cum[s]) for s in range(nt)]
    pairs = []
    for t in range(nt):
        for s in range(t + 1):
            cb = []
            for g in range(SSM_GROUPS):
                gl = slice(g * SSM_STATE, (g + 1) * SSM_STATE)
                cb.append(jnp.sum(cm[t][:, gl] * bm[s][:, gl], axis=1, keepdims=True))
            cbh = jnp.where(first_group, cb[0], cb[1])
            facs.append(jnp.exp(cum[t] - cum[s]) * dts[s] * cbh)
            pairs.append((t, s))
    fx = _dot_f32(jnp.concatenate(facs, axis=0), expand_ref[...])
    fslab = lambda i: fx[i * nb:(i + 1) * nb]
    for t in range(nt):
        ea_ref[t * nb:(t + 1) * nb, :] = fslab(t)
        xw_ref[t * nb:(t + 1) * nb, :] = xs[t] * fslab(nt + t)
    for t in range(nt):
        acc = dskip_ref[...] * xs[t]
        for i, (tt, s) in enumerate(pairs):
            if tt == t:
                acc = acc + fslab(2 * nt + i) * xs[s]
        ypart_ref[t * nb:(t + 1) * nb, :] = acc

    u = uv[:, :D_SG]
    v = _rms(uv[:, D_SG:], sgnw_ref[...])
    v_ref[...] = v
    for t in range(nt):
        sv = b4_ref[t:t + 1, :]
        for s in range(t + 1):
            sv = sv + w4_ref[t, s:s + 1, :] * slab(v, s)
        ysg_ref[t * nb:(t + 1) * nb, :] = slab(u, t) * sv


def _sample_front(x_slab, cprev, w, nb, nt):
    r = nb * nt
    weights = [w['nmw'], w['wz'], w['wxbc'], w['wdt'], w['wuv'], w['convw'], w['convb'], w['dtb'],
               w['alog'], w['dskip'], w['sgnw'], w['w4'], w['b4'], w['expand']]
    ins = [x_slab, cprev] + weights
    out_shape = [
        jax.ShapeDtypeStruct((r, D_SSM), F32),
        jax.ShapeDtypeStruct((r, D_SSM), F32),
        jax.ShapeDtypeStruct((r, D_SSM), F32),
        jax.ShapeDtypeStruct((r, D_SG), F32),
        jax.ShapeDtypeStruct((r, D_SG), F32),
        jax.ShapeDtypeStruct((CONV_K - 1, nb, C_CONV), F32),
        jax.ShapeDtypeStruct((r, SSM_GROUPS * SSM_STATE), F32),
        jax.ShapeDtypeStruct((r, SSM_GROUPS * SSM_STATE), F32),
        jax.ShapeDtypeStruct((r, D_SSM), F32),
        jax.ShapeDtypeStruct((nb, LANES), F32),
    ]
    return pl.pallas_call(
        functools.partial(_sample_front_kernel, nb=nb, nt=nt),
        grid=(1,),
        in_specs=[_const_spec(a.shape) for a in ins],
        out_specs=[_const_spec(s.shape) for s in out_shape],
        out_shape=out_shape,
        compiler_params=pltpu.CompilerParams(
            dimension_semantics=("arbitrary",), vmem_limit_bytes=VMEM_LIMIT),
        name="sample_front",
    )(*ins)


def _sample_state_kernel(dec_ref, st_ref, cq_ref, bq_ref, xw_ref, so_ref, z_ref, *, seqs):
    for bb in range(seqs):
        for g in range(SSM_GROUPS):
            s0 = st_ref[bb, g]
            z_ref[bb, g] = lax.dot_general(
                cq_ref[bb, g].astype(BF16), s0.astype(BF16), (((1,), (1,)), ((), ())),
                preferred_element_type=F32)
            upd = lax.dot_general(
                xw_ref[bb, g].astype(BF16), bq_ref[bb, g].astype(BF16), (((0,), (0,)), ((), ())),
                preferred_element_type=F32)
            for hh in range(HEADS_PER_GROUP):
                hs = slice(hh * SSM_HEAD_DIM, (hh + 1) * SSM_HEAD_DIM)
                so_ref[bb, g, hs, :] = s0[hs] * dec_ref[bb, g * HEADS_PER_GROUP + hh] + upd[hs]


def _sample_state(dec, state, cq, bq, xwq, seqs):
    nb = state.shape[0]
    q = HEADS_PER_GROUP * SSM_HEAD_DIM
    blk = lambda *tail: pl.BlockSpec((seqs, SSM_GROUPS) + tail, lambda i: (i, 0, 0, 0))
    return pl.pallas_call(
        functools.partial(_sample_state_kernel, seqs=seqs),
        grid=(nb // seqs,),
        in_specs=[
            pl.BlockSpec((seqs, SSM_HEADS), lambda i: (i, 0), memory_space=pltpu.SMEM),
            blk(q, SSM_STATE), blk(SUBLANES, SSM_STATE), blk(SUBLANES, SSM_STATE), blk(SUBLANES, q),
        ],
        out_specs=[blk(q, SSM_STATE), blk(SUBLANES, q)],
        out_shape=[
            jax.ShapeDtypeStruct((nb, SSM_GROUPS, q, SSM_STATE), F32),
            jax.ShapeDtypeStruct((nb, SSM_GROUPS, SUBLANES, q), F32),
        ],
        compiler_params=pltpu.CompilerParams(
            dimension_semantics=("arbitrary",), vmem_limit_bytes=VMEM_LIMIT),
        name="sample_state",
    )(dec, state, cq, bq, xwq)


def _sample_back_kernel(x_ref, ypart_ref, ea_ref, zoff_ref, z_ref, ysg_ref, ssmnw_ref, wout_ref,
                        nfw_ref, wr_ref, br_ref, h_ref, hn2_ref, route_ref):
    y = ypart_ref[...] + ea_ref[...] * zoff_ref[...]
    y_ssm = _gated_group_norm(y, z_ref[...], ssmnw_ref[...])
    mix_in = jnp.concatenate([y_ssm.astype(BF16), ysg_ref[...].astype(BF16)], axis=1)
    h = x_ref[...] + _dot(mix_in, wout_ref[...])
    h_ref[...] = h
    hn2, route = _ffn_front(h, nfw_ref[...], wr_ref, br_ref[...])
    _store_row_tiles(hn2_ref, hn2)
    route_ref[...] = route


def _sample_back(x_slab, ypart, ea, zoff, z, ysg, w):
    r = x_slab.shape[0]
    ins = [x_slab, ypart, ea, zoff, z, ysg, w['ssmnw'], w['wout'], w['nfw'], w['wr'], w['br']]
    out_shape = [
        jax.ShapeDtypeStruct((r, D_MODEL), F32),
        jax.ShapeDtypeStruct((r * ROW_TILES, LANES), F32),
        jax.ShapeDtypeStruct((r, LANES), F32),
    ]
    return pl.pallas_call(
        _sample_back_kernel,
        grid=(1,),
        in_specs=[_const_spec(a.shape) for a in ins],
        out_specs=[_const_spec(s.shape) for s in out_shape],
        out_shape=out_shape,
        compiler_params=pltpu.CompilerParams(
            dimension_semantics=("arbitrary",), vmem_limit_bytes=VMEM_LIMIT),
        name="sample_back",
    )(*ins)


DMA_GROUP = 32


def _expert_mlp_kernel(be_ref, nu_ref, nv_ref, tok0_ref, tokn_ref, dstp_ref, x_hbm, wg_ref, wu_ref, wd_ref,
                       parts_hbm, xbuf, obuf, xb, hb, wg_b, wu_b, wd_b, gsem, ssem, *, bm):
    j = pl.program_id(0)
    nu = nu_ref[0]
    nt = ROW_TILES
    n_groups = bm // DMA_GROUP
    tile = lambda ref, start: ref.at[pl.ds(pl.multiple_of(start, nt), nt)]
    rows_of = lambda k: nv_ref[k + 2]

    def for_groups(count, fn):
        for g in range(n_groups):
            @pl.when(g * DMA_GROUP < count)
            def _(g=g):
                fn(g)

    def gather_start(idx_ref, s, g):
        for r in range(g * DMA_GROUP, (g + 1) * DMA_GROUP):
            pltpu.make_async_copy(tile(x_hbm, idx_ref[0, r]), xbuf.at[s, pl.ds(r * nt, nt)],
                                  gsem.at[s]).start(priority=r % 2)

    def scatter_start(s, g):
        for r in range(g * DMA_GROUP, (g + 1) * DMA_GROUP):
            pltpu.make_async_copy(obuf.at[s, pl.ds(r * nt, nt)], tile(parts_hbm, dstp_ref[0, r]),
                                  ssem.at[s]).start(priority=(r + 1) % 2)

    group_rows = pl.ds(0, DMA_GROUP * nt)

    def gather_wait(s, g):
        del g
        pltpu.make_async_copy(x_hbm.at[group_rows], xbuf.at[s, group_rows], gsem.at[s]).wait()

    def scatter_wait(s, g):
        del g
        pltpu.make_async_copy(obuf.at[s, group_rows], parts_hbm.at[group_rows], ssem.at[s]).wait()

    part = functools.partial

    def step(slot):
        for_groups(rows_of(j), part(gather_wait, slot))
        for_groups(rows_of(j - 2), part(scatter_wait, slot))
        for_groups(rows_of(j + 1), part(gather_start, tokn_ref, 1 - slot))
        for_groups(rows_of(j - 1), part(scatter_start, 1 - slot))

        xb[...] = _load_row_tiles(xbuf.at[slot]).astype(BF16)
        hb[...] = (_silu(_dot(xb[...], wg_b[...])) * _dot(xb[...], wu_b[...])).astype(BF16)
        _store_row_tiles(obuf.at[slot], _dot(hb[...], wd_b[...]))

    def drain(slot):
        for_groups(rows_of(j - 2), part(scatter_wait, slot))
        for_groups(rows_of(j - 1), part(scatter_start, 1 - slot))
        for_groups(rows_of(j - 1), part(scatter_wait, 1 - slot))

    @pl.when(j == 0)
    def _():
        xbuf[...] = jnp.zeros(xbuf.shape, F32)
        obuf[...] = jnp.zeros(obuf.shape, F32)
        n_real = parts_hbm.shape[0] - bm * nt
        spare = pltpu.make_async_copy(obuf.at[0], parts_hbm.at[pl.ds(n_real, bm * nt)], ssem.at[0])
        spare.start()
        spare.wait()
        for_groups(rows_of(0), part(gather_start, tok0_ref, 0))

    used = j < nu

    @pl.when(used & ((j == 0) | (be_ref[j] != be_ref[jnp.maximum(j - 1, 0)])))
    def _():
        wg_b[...] = wg_ref[...].astype(BF16)
        wu_b[...] = wu_ref[...].astype(BF16)
        wd_b[...] = wd_ref[...].astype(BF16)

    for slot in range(2):
        parity = lax.rem(j, 2) == slot

        @pl.when(used & parity)
        def _(slot=slot):
            step(slot)

        @pl.when((j == nu) & parity)
        def _(slot=slot):
            drain(slot)


def _expert_mlp(blk_expert, n_used, n_valid, tok, dst, x, wg, wu, wd, bm, n_out):
    n_blocks = blk_expert.shape[0]
    assert bm % DMA_GROUP == 0
    w_map = lambda j, be, nu, nv: (be[j], 0, 0)
    idx_spec = lambda f: pl.BlockSpec((None, 1, bm), lambda j, be, nu, nv: (f(j), 0, 0),
                                      memory_space=pltpu.SMEM)
    any_spec = pl.BlockSpec(memory_space=pl.ANY)
    grid_spec = pltpu.PrefetchScalarGridSpec(
        num_scalar_prefetch=3,
        grid=(n_blocks + 1,),
        in_specs=[
            idx_spec(lambda j: 0),
            idx_spec(lambda j: jnp.minimum(j + 1, n_blocks - 1)),
            idx_spec(lambda j: j),
            any_spec,
            pl.BlockSpec((None, D_MODEL, D_FF), w_map),
            pl.BlockSpec((None, D_MODEL, D_FF), w_map),
            pl.BlockSpec((None, D_FF, D_MODEL), w_map),
        ],
        out_specs=any_spec,
        scratch_shapes=[
            pltpu.VMEM((2, bm * ROW_TILES, LANES), F32),
            pltpu.VMEM((2, bm * ROW_TILES, LANES), F32),
            pltpu.VMEM((bm, D_MODEL), BF16),
            pltpu.VMEM((bm, D_FF), BF16),
            pltpu.VMEM((D_MODEL, D_FF), BF16),
            pltpu.VMEM((D_MODEL, D_FF), BF16),
            pltpu.VMEM((D_FF, D_MODEL), BF16),
            pltpu.SemaphoreType.DMA((2,)),
            pltpu.SemaphoreType.DMA((2,)),
        ],
    )
    tok3 = (tok * ROW_TILES).reshape(n_blocks, 1, bm)
    dst_prev = (jnp.concatenate([dst[:1], dst], axis=0) * ROW_TILES).reshape(n_blocks + 1, 1, bm)
    blk_expert = jnp.concatenate([blk_expert, blk_expert[-1:]])
    zeros2 = jnp.zeros((2,), jnp.int32)
    n_valid = jnp.concatenate([zeros2, n_valid.astype(jnp.int32), zeros2])
    return pl.pallas_call(
        functools.partial(_expert_mlp_kernel, bm=bm),
        grid_spec=grid_spec,
        out_shape=jax.ShapeDtypeStruct((n_out * ROW_TILES, LANES), F32),
        compiler_params=pltpu.CompilerParams(
            dimension_semantics=("arbitrary",), vmem_limit_bytes=VMEM_LIMIT),
        name="expert_mlp",
    )(blk_expert, n_used, n_valid, tok3, tok3, dst_prev, x, wg, wu, wd)


def _moe_plan(e_idx, bm, n_blocks):
    n_tok = e_idx.shape[0]
    n_asg = 2 * n_tok
    n_pad = n_blocks * bm - n_asg
    e_flat = e_idx.reshape(-1)
    experts = jnp.arange(N_EXPERTS, dtype=jnp.int32)
    counts = jnp.sum((e_flat[:, None] == experts[None, :]).astype(jnp.int32), axis=0)
    nblk = (counts + bm - 1) // bm
    blk_end = jnp.cumsum(nblk)
    n_used = blk_end[-1]
    j = jnp.arange(n_blocks, dtype=jnp.int32)
    be = jnp.sum((blk_end[None, :] <= jnp.minimum(j, n_used - 1)[:, None]).astype(jnp.int32), axis=1)
    be = jnp.minimum(be, N_EXPERTS - 1)
    pad_end = jnp.cumsum(nblk * bm - counts)
    pad_expert = jnp.sum((pad_end[None, :] <= jnp.arange(n_pad, dtype=jnp.int32)[:, None]).astype(jnp.int32),
                         axis=1)
    keys = jnp.concatenate([2 * e_flat, 2 * pad_expert + 1])
    ids = jnp.concatenate([jnp.arange(n_asg, dtype=jnp.int32), jnp.full((n_pad,), -1, jnp.int32)])
    _, window = lax.sort((keys, ids), num_keys=1, is_stable=True)
    window = window.reshape(n_blocks, bm)
    valid = window >= 0
    q = jnp.arange(bm, dtype=jnp.int32)
    tok = jnp.where(valid, window // 2, (j[:, None] * bm + q[None, :]) % n_tok)
    dst = jnp.where(valid, (window % 2) * n_tok + window // 2,
                    2 * n_tok + q[None, :])
    n_valid = jnp.sum(valid.astype(jnp.int32), axis=1)
    return tok.astype(jnp.int32), dst.astype(jnp.int32), be, n_used.astype(jnp.int32).reshape(1), n_valid


def _final_kernel(h_ref, p0_ref, p1_ref, route_ref, w_ref, op_ref, os_ref, *, n_prompt_steps):
    route = route_ref[...]
    p0 = _load_row_tiles(p0_ref)
    p1 = _load_row_tiles(p1_ref)
    y = _rms(h_ref[...] + route[:, 2:3] * p0 + route[:, 3:4] * p1, w_ref[...])
    i = pl.program_id(0)

    @pl.when(i < n_prompt_steps)
    def _():
        op_ref[...] = y

    @pl.when(i >= n_prompt_steps)
    def _():
        os_ref[...] = y


def _final(h, parts, route, w, tf, n_prompt):
    n_tok = h.shape[0]
    assert n_tok % tf == 0 and n_prompt % tf == 0
    n_p = n_prompt // tf
    row = lambda d: pl.BlockSpec((tf, d), lambda i: (i, 0))
    part = lambda k: pl.BlockSpec((tf * ROW_TILES, LANES), lambda i: (k * (n_tok // tf) + i, 0))
    return pl.pallas_call(
        functools.partial(_final_kernel, n_prompt_steps=n_p),
        grid=(n_tok // tf,),
        in_specs=[row(D_MODEL), part(0), part(1), row(LANES), _const_spec(w.shape)],
        out_specs=[pl.BlockSpec((tf, D_MODEL), lambda i: (jnp.minimum(i, n_p - 1), 0)),
                   pl.BlockSpec((tf, D_MODEL), lambda i: (jnp.maximum(i - n_p, 0), 0))],
        out_shape=[jax.ShapeDtypeStruct((n_prompt, D_MODEL), F32),
                   jax.ShapeDtypeStruct((n_tok - n_prompt, D_MODEL), F32)],
        compiler_params=pltpu.CompilerParams(
            dimension_semantics=("arbitrary",), vmem_limit_bytes=VMEM_LIMIT),
        name="final_combine",
    )(h, parts, parts, route, w)


PROMPT_TILE = 256
MOE_BLOCK_ROWS = 256
FINAL_TILE = 512
STATE_SEQS = 8


def _pad_lanes(a, width=LANES):
    return jnp.pad(a, [(0, 0)] * (a.ndim - 1) + [(0, width - a.shape[-1])])


def _layer_weights(l, norm_mix_w, w_in, conv_w, conv_b, dt_bias, a_log, d_skip, ssm_norm_w, sg_norm_w,
                   sg_w, sg_b, w_out, norm_ffn_w, w_router_group, b_router_group, w_router_expert,
                   b_router_expert, n_dec):
    c0, c1, c2 = D_SSM, D_SSM + C_CONV, D_SSM + C_CONV + SSM_HEADS
    wi = w_in[l]
    causal = jnp.tril(jnp.ones((CHUNK, CHUNK), bool))
    sgw_l = jnp.where(causal, sg_w[l], 0.0)
    head_of_lane = jnp.arange(D_SSM, dtype=jnp.int32) // SSM_HEAD_DIM
    expand = (jnp.arange(LANES, dtype=jnp.int32)[:, None] == head_of_lane[None, :]).astype(F32)
    row = lambda a: a.reshape(1, -1).astype(F32)
    gap = ROUTER_EXPERT_ROW - N_EXPERT_GROUPS
    rest = LANES - ROUTER_EXPERT_ROW - N_EXPERTS
    wrt = jnp.concatenate([w_router_group[l].T, jnp.zeros((gap, D_MODEL), F32), w_router_expert[l].T,
                           jnp.zeros((rest, D_MODEL), F32)], axis=0).astype(F32)
    brc = jnp.concatenate([b_router_group[l], jnp.zeros((gap,), F32), b_router_expert[l],
                           jnp.zeros((rest,), F32)]).astype(F32).reshape(LANES, 1)
    return {
        'nmw': row(norm_mix_w[l]),
        'wz': wi[:, :c0].astype(BF16),
        'wxbc': wi[:, c0:c1].astype(BF16),
        'wdt': _pad_lanes(wi[:, c1:c2]).astype(BF16),
        'wuv': wi[:, c2:].astype(BF16),
        'convw': conv_w[l].astype(F32),
        'convb': row(conv_b[l]),
        'dtb': _pad_lanes(row(dt_bias[l])),
        'alog': _pad_lanes(row(a_log[l])),
        'dskip': row(jnp.repeat(d_skip[l], SSM_HEAD_DIM)),
        'ssmnw': row(ssm_norm_w[l]),
        'sgnw': row(sg_norm_w[l]),
        'sgw': sgw_l.astype(BF16),
        'sgb': jnp.repeat(sg_b[l].T, SG_HEAD_DIM, axis=1).astype(F32),
        'w4': jnp.repeat(jnp.transpose(sgw_l[:, :n_dec, :n_dec], (1, 2, 0)), SG_HEAD_DIM, axis=2).astype(F32),
        'b4': jnp.repeat(sg_b[l][:, :n_dec].T, SG_HEAD_DIM, axis=1).astype(F32),
        'expand': expand,
        'wout': w_out[l].astype(BF16),
        'nfw': row(norm_ffn_w[l]),
        'wr': jnp.stack([wrt.astype(BF16), (wrt - wrt.astype(BF16).astype(F32)).astype(BF16)]),
        'br': brc,
    }


def _layer(l, xp, xs_slab, state_conv, state_ssm, w, w_gate, w_up, w_down):
    bp, seq, _ = xp.shape
    nb, nt = state_conv.shape[1], xs_slab.shape[0] // state_conv.shape[1]
    q = HEADS_PER_GROUP * SSM_HEAD_DIM

    cprev = jnp.transpose(state_conv[l], (1, 0, 2))
    z, ypart, ea, ysg, v, convn, cmat, bmat, xw, dec = _sample_front(xs_slab, cprev, w, nb, nt)
    to_seq = lambda a, d: jnp.pad(
        jnp.transpose(a.reshape(nt, nb, SSM_GROUPS, d), (1, 2, 0, 3)),
        ((0, 0), (0, 0), (0, SUBLANES - nt), (0, 0)))
    ssm_s, zoff = _sample_state(
        dec[:, :SSM_HEADS], state_ssm[l].reshape(nb, SSM_GROUPS, q, SSM_STATE),
        to_seq(cmat, SSM_STATE), to_seq(bmat, SSM_STATE), to_seq(xw, q), STATE_SEQS)
    zoff = jnp.transpose(zoff[:, :, :nt], (2, 0, 1, 3)).reshape(nt * nb, D_SSM)
    h_s, hn2_s, route_s = _sample_back(xs_slab, ypart, ea, zoff, z, ysg, w)

    h, hn2, route, convt_p, ssm_p = _prompt_mixer(xp, w, PROMPT_TILE, h_s, hn2_s, route_s)

    n_tok = h.shape[0]
    bm = MOE_BLOCK_ROWS
    n_blocks = -(-(2 * n_tok) // bm) + N_EXPERTS
    tok, dst, blk_expert, n_used, n_valid = _moe_plan(route[:, :2].astype(jnp.int32), bm, n_blocks)
    parts = _expert_mlp(blk_expert, n_used, n_valid, tok, dst, hn2, w_gate[l], w_up[l], w_down[l], bm,
                        2 * n_tok + bm)
    outs = dict(
        h=h, parts=parts, route=route,
        conv_p=convt_p[:, SUBLANES - (CONV_K - 1):],
        ssm_p=ssm_p.reshape(bp, SSM_GROUPS, HEADS_PER_GROUP, SSM_HEAD_DIM, SSM_STATE),
        conv_s=jnp.transpose(convn, (1, 0, 2)),
        ssm_s=ssm_s.reshape(nb, SSM_GROUPS, HEADS_PER_GROUP, SSM_HEAD_DIM, SSM_STATE),
        v_s=jnp.transpose(v.reshape(nt, nb, D_SG), (1, 0, 2)),
    )
    return outs


def kernel(x_prompt, x_sample, state_conv, state_ssm, norm_mix_w, w_in, conv_w, conv_b, dt_bias, a_log, d_skip, ssm_norm_w, sg_norm_w, sg_w, sg_b, w_out, norm_ffn_w, w_router_group, b_router_group, w_router_expert, b_router_expert, w_gate, w_up, w_down, norm_final_w):
    depth = w_in.shape[0]
    assert depth == 1, "the fused final norm assumes a single layer"
    bp, seq, _ = x_prompt.shape
    nb, nt, _ = x_sample.shape
    l = 0
    w = _layer_weights(l, norm_mix_w, w_in, conv_w, conv_b, dt_bias, a_log, d_skip, ssm_norm_w, sg_norm_w,
                       sg_w, sg_b, w_out, norm_ffn_w, w_router_group, b_router_group, w_router_expert,
                       b_router_expert, nt)
    xs_slab = jnp.transpose(x_sample, (1, 0, 2)).reshape(nt * nb, D_MODEL)
    outs = _layer(l, x_prompt, xs_slab, state_conv, state_ssm, w, w_gate, w_up, w_down)
    y_p, y_s = _final(outs['h'], outs['parts'], outs['route'], norm_final_w.reshape(1, -1).astype(F32),
                      FINAL_TILE, bp * seq)
    y_prompt = y_p.reshape(bp, seq, D_MODEL)
    y_sample = jnp.transpose(y_s.reshape(nt, nb, D_MODEL), (1, 0, 2))
    return (y_prompt, y_sample, outs['conv_p'][None], outs['ssm_p'][None], outs['conv_s'][None],
            outs['ssm_s'][None], outs['v_s'][None])
```

```python
import functools

import jax
import jax.numpy as jnp
from jax import lax
from jax.experimental import pallas as pl
from jax.experimental.pallas import tpu as pltpu

D_MODEL = 1024
D_SSM = 1024
SSM_HEAD_DIM = 64
SSM_HEADS = 16
SSM_GROUPS = 2
HEADS_PER_GROUP = 8
SSM_STATE = 128
CONV_K = 4
C_CONV = D_SSM + 2 * SSM_GROUPS * SSM_STATE
D_SG = 1024
SG_HEADS = 8
SG_HEAD_DIM = 128
CHUNK = 128
N_EXPERT_GROUPS = 4
EXPERTS_PER_GROUP = 8
N_EXPERTS = 32
D_FF = 512
EPS = 1e-6

LANES = 128
SUBLANES = 8
VMEM_LIMIT = 56 * 1024 * 1024

F32 = jnp.float32
BF16 = jnp.bfloat16
HIGHEST = lax.Precision.HIGHEST


def _dot(a, b):
    return jnp.dot(a, b, preferred_element_type=F32)


def _dot_f32(a, b):
    return jnp.dot(a, b, preferred_element_type=F32, precision=HIGHEST)


def _rms(x, w):
    return x * lax.rsqrt(jnp.mean(x * x, axis=-1, keepdims=True) + EPS) * w


def _silu(x):
    return x * (1.0 / (1.0 + jnp.exp(-x)))


def _gelu(x):
    return 0.5 * x * (1.0 + lax.erf(x * 0.7071067811865476))


def _softplus(x):
    return jnp.maximum(x, 0.0) + jnp.log1p(jnp.exp(-jnp.abs(x)))


def _gated_group_norm(y, z, w):
    g = y * _silu(z)
    half = D_SSM // SSM_GROUPS
    parts = []
    for k in range(SSM_GROUPS):
        gk = g[:, k * half:(k + 1) * half]
        parts.append(gk * lax.rsqrt(jnp.mean(gk * gk, axis=-1, keepdims=True) + EPS))
    return jnp.concatenate(parts, axis=1) * w


ROUTER_GROUP_ROW = 0
ROUTER_EXPERT_ROW = SUBLANES


def _route_t(lt):
    r = lt.shape[1]
    row = lax.broadcasted_iota(jnp.int32, (SUBLANES, r), 0)
    rowf = row.astype(F32)
    big = float(SUBLANES)
    red = lambda f, a: f(a, axis=0, keepdims=True)
    gl = jnp.where(row < N_EXPERT_GROUPS, lt[ROUTER_GROUP_ROW:ROUTER_GROUP_ROW + SUBLANES], -jnp.inf)
    ge = jnp.exp(gl - red(jnp.max, gl))
    p_grp = ge / red(jnp.sum, ge)
    g_p = red(jnp.max, p_grp)
    g_idx = red(jnp.min, jnp.where(p_grp == g_p, rowf, big))
    el = lt[ROUTER_EXPERT_ROW:ROUTER_EXPERT_ROW + EXPERTS_PER_GROUP]
    for g in range(1, N_EXPERT_GROUPS):
        lo = ROUTER_EXPERT_ROW + g * EXPERTS_PER_GROUP
        el = jnp.where(g_idx == float(g), lt[lo:lo + EXPERTS_PER_GROUP], el)
    ee = jnp.exp(el - red(jnp.max, el))
    pe = ee / red(jnp.sum, ee)
    v1 = red(jnp.max, pe)
    i1 = red(jnp.min, jnp.where(pe == v1, rowf, big))
    rest = rowf != i1
    pe2 = jnp.where(rest, pe, -1.0)
    v2 = red(jnp.max, pe2)
    i2 = red(jnp.min, jnp.where(rest & (pe2 == v2), rowf, big))
    den = v1 + v2
    base = g_idx * float(EXPERTS_PER_GROUP)
    out = jnp.where(row == 0, base + i1, 0.0)
    out = jnp.where(row == 1, base + i2, out)
    out = jnp.where(row == 2, g_p * v1 / den, out)
    out = jnp.where(row == 3, g_p * v2 / den, out)
    return jnp.concatenate([out, jnp.zeros((LANES - SUBLANES, r), F32)], axis=0).T


def _ffn_front(h, nfw, wrt_ref, brc):
    hn2 = _rms(h, nfw)
    hi = hn2.astype(BF16)
    lo = (hn2 - hi.astype(F32)).astype(BF16)
    nt = lambda a, b: lax.dot_general(a, b, (((1,), (1,)), ((), ())), preferred_element_type=F32)
    lt = nt(wrt_ref[0], hi) + nt(wrt_ref[0], lo) + nt(wrt_ref[1], hi) + brc
    return hn2, _route_t(lt)


ROW_TILES = D_MODEL // LANES


def _store_row_tiles(ref, val):
    r = val.shape[0]
    for k in range(ROW_TILES):
        ref[pl.ds(k, r, stride=ROW_TILES), :] = val[:, k * LANES:(k + 1) * LANES]


def _load_row_tiles(ref):
    r = ref.shape[0] // ROW_TILES
    return jnp.concatenate([ref[pl.ds(k, r, stride=ROW_TILES), :] for k in range(ROW_TILES)], axis=1)


def _prompt_step(
        t, nt, fillers, x_ref, nmw_ref, wz_ref, wxbc_ref, wdt_ref, wuv_ref, convw_ref, convb_ref, dtb_ref,
        alog_ref, dskip_ref, ssmnw_ref, sgnw_ref, sgw_ref, sgb_ref, convt_ref, ssm_ref,
        xbc_scr, state_scr, y_scr, mix_scr, tl):
    fillers = list(fillers)
    assert len(fillers) == 3 + tl // CHUNK

    @pl.when(t == 0)
    def _():
        xbc_scr[0:SUBLANES, :] = jnp.zeros((SUBLANES, C_CONV), F32)
        state_scr[...] = jnp.zeros(state_scr.shape, F32)

    x = x_ref[...]
    hn = _rms(x, nmw_ref[...]).astype(BF16)
    z = _dot(hn, wz_ref[...])
    xbc = _dot(hn, wxbc_ref[...])
    dtr = _dot(hn, wdt_ref[...])
    uv = _gelu(_dot(hn, wuv_ref[...]))
    fillers.pop(0)()

    xbc_scr[SUBLANES:SUBLANES + tl, :] = xbc
    conv = convb_ref[...]
    for k in range(CONV_K):
        off = SUBLANES - (CONV_K - 1) + k
        conv = conv + convw_ref[k:k + 1, :] * xbc_scr[off:off + tl, :]
    tail = xbc_scr[tl:tl + SUBLANES, :]
    xbc_scr[0:SUBLANES, :] = tail
    convt_ref[...] = tail
    act = _silu(conv)
    xs = act[:, :D_SSM]
    bmat = act[:, D_SSM:D_SSM + SSM_GROUPS * SSM_STATE]
    cmat = act[:, D_SSM + SSM_GROUPS * SSM_STATE:]
    fillers.pop(0)()

    dt = _softplus(dtr + dtb_ref[...])
    a_row = -jnp.exp(alog_ref[...])

    li = lax.broadcasted_iota(jnp.int32, (CHUNK, CHUNK), 0)
    si = lax.broadcasted_iota(jnp.int32, (CHUNK, CHUNK), 1)
    causal = li >= si
    tri = jnp.where(causal, 1.0, 0.0).astype(F32)
    lo_half = si < SSM_HEAD_DIM

    for c in range(tl // CHUNK):
        rows = slice(c * CHUNK, (c + 1) * CHUNK)
        dt_c = dt[rows]
        acum = _dot_f32(tri, dt_c * a_row)
        acum_t = acum.T
        dt_t = dt_c.T
        w_t = dt_t * jnp.exp(acum_t[:, CHUNK - 1:CHUNK] - acum_t)
        xs_c = xs[rows]
        for g in range(SSM_GROUPS):
            gl = slice(g * SSM_STATE, (g + 1) * SSM_STATE)
            b_g = bmat[rows, gl]
            c_g = cmat[rows, gl]
            b_gt = b_g.T
            cb = _dot(c_g.astype(BF16), b_gt.astype(BF16))
            for j in range(HEADS_PER_GROUP // 2):
                pl_ = slice(g * 512 + j * LANES, g * 512 + (j + 1) * LANES)
                m_l, ec_l, s_l, ea_last = [], [], [], []
                for k in range(2):
                    hh = g * HEADS_PER_GROUP + 2 * j + k
                    colb = jnp.broadcast_to(acum[:, hh:hh + 1], (CHUNK, CHUNK))
                    rowb = jnp.broadcast_to(acum_t[hh:hh + 1, :], (CHUNK, CHUNK))
                    ea = jnp.exp(colb)
                    seg = jnp.where(causal, colb - rowb, 0.0)
                    lmat = jnp.where(causal, jnp.exp(seg), 0.0)
                    m_l.append(cb * lmat * jnp.broadcast_to(dt_t[hh:hh + 1, :], (CHUNK, CHUNK)))
                    ec_l.append(c_g * ea)
                    s_l.append(b_gt * jnp.broadcast_to(w_t[hh:hh + 1, :], (CHUNK, CHUNK)))
                    ea_last.append(ea[CHUNK - 1:CHUNK, :])
                xs_p = xs_c[:, pl_]
                st_p = state_scr[g, :, j * LANES:(j + 1) * LANES]
                rx = jnp.concatenate([jnp.where(lo_half, xs_p, 0.0),
                                      jnp.where(lo_half, 0.0, xs_p)], axis=0).astype(BF16)
                rs = jnp.concatenate([jnp.where(lo_half, st_p, 0.0),
                                      jnp.where(lo_half, 0.0, st_p)], axis=0).astype(BF16)
                lhs = jnp.concatenate(m_l + ec_l, axis=1).astype(BF16)
                y_p = _dot(lhs, jnp.concatenate([rx, rs], axis=0))
                y_scr[rows, pl_] = y_p
                dec = jnp.where(lo_half[0:1, :], ea_last[0], ea_last[1])
                upd = _dot(jnp.concatenate(s_l, axis=1).astype(BF16), rx)
                state_scr[g, :, j * LANES:(j + 1) * LANES] = st_p * dec + upd
        fillers.pop(0)()

    y = y_scr[...] + dskip_ref[...] * xs
    y_ssm = _gated_group_norm(y, z, ssmnw_ref[...])
    mix_scr[:, :D_SSM] = y_ssm.astype(BF16)

    u = uv[:, :D_SG]
    v = _rms(uv[:, D_SG:], sgnw_ref[...])
    vb = v.astype(BF16)
    for c in range(tl // CHUNK):
        rows = slice(c * CHUNK, (c + 1) * CHUNK)
        for hd in range(SG_HEADS):
            hl = slice(hd * SG_HEAD_DIM, (hd + 1) * SG_HEAD_DIM)
            sv = _dot(sgw_ref[hd], vb[rows, hl]) + sgb_ref[:, hl]
            mix_scr[rows, D_SSM + hd * SG_HEAD_DIM:D_SSM + (hd + 1) * SG_HEAD_DIM] = (
                u[rows, hl] * sv).astype(BF16)

    fillers.pop(0)()

    @pl.when(t == nt - 1)
    def _():
        for g in range(SSM_GROUPS):
            ssm_ref[g] = state_scr[g].T


def _prompt_mixer_kernel(
        x_ref, xp_ref, nmw_ref, wz_ref, wxbc_ref, wdt_ref, wuv_ref, convw_ref, convb_ref, dtb_ref,
        alog_ref, dskip_ref, ssmnw_ref, sgnw_ref, sgw_ref, sgb_ref, wout_ref, nfw_ref,
        wr_ref, br_ref, hs_ref, hn2s_ref, routes_ref,
        h_ref, hn2_ref, route_ref, convt_ref, ssm_ref,
        xbc_scr, state_scr, y_scr, mix_scr, h_scr, *, tl, nt, n_main):
    i = pl.program_id(0)

    @pl.when(i == 0)
    def _():
        mix_scr[...] = jnp.zeros(mix_scr.shape, BF16)

    @pl.when(i > n_main)
    def _():
        h_ref[...] = hs_ref[...]
        hn2_ref[...] = hn2s_ref[...]
        route_ref[...] = routes_ref[...]

    n_out_chunks = 2 + tl // CHUNK
    width = D_MODEL // n_out_chunks
    assert width % LANES == 0 and width * n_out_chunks == D_MODEL

    def out_chunk(c):
        cols = slice(c * width, (c + 1) * width)
        h_scr[:, cols] = xp_ref[:, cols] + _dot(mix_scr[...], wout_ref[:, cols])

    def finish():
        h = h_scr[...]
        h_ref[...] = h
        hn2, route = _ffn_front(h, nfw_ref[...], wr_ref, br_ref[...])
        _store_row_tiles(hn2_ref, hn2)
        route_ref[...] = route

    @pl.when(i <= n_main)
    def _():
        def out_chunks(cs):
            for c in cs:
                out_chunk(c)

        half = n_out_chunks // 2
        fillers = ([functools.partial(out_chunks, range(half)),
                    functools.partial(out_chunks, range(half, n_out_chunks)), finish]
                   + [lambda: None] * (tl // CHUNK))
        _prompt_step(
            lax.rem(i, nt), nt, fillers, x_ref, nmw_ref, wz_ref, wxbc_ref, wdt_ref, wuv_ref,
            convw_ref, convb_ref, dtb_ref, alog_ref, dskip_ref, ssmnw_ref, sgnw_ref, sgw_ref, sgb_ref,
            convt_ref, ssm_ref, xbc_scr, state_scr, y_scr, mix_scr, tl)


def _const_spec(shape):
    zeros = (0,) * len(shape)
    return pl.BlockSpec(shape, lambda *_: zeros)


def _prompt_mixer(x, w, tl, h_s, hn2_s, route_s):
    bsz, seq, _ = x.shape
    nt = seq // tl
    n_main = bsz * nt
    n_s = h_s.shape[0]
    assert seq % tl == 0 and n_s % tl == 0
    n_tok = bsz * seq + n_s
    x2 = x.reshape(bsz * seq, D_MODEL)
    main = lambda i: jnp.minimum(i, n_main - 1)
    prev = lambda i: jnp.clip(i - 1, 0, n_main - 1)
    tail = lambda i: jnp.maximum(i - n_main - 1, 0)
    out_row = lambda i: jnp.maximum(i - 1, 0)
    weights = [w['nmw'], w['wz'], w['wxbc'], w['wdt'], w['wuv'], w['convw'], w['convb'], w['dtb'],
               w['alog'], w['dskip'], w['ssmnw'], w['sgnw'], w['sgw'], w['sgb'], w['wout'], w['nfw'],
               w['wr'], w['br']]
    in_specs = ([pl.BlockSpec((tl, D_MODEL), lambda i: (main(i), 0)),
                 pl.BlockSpec((tl, D_MODEL), lambda i: (prev(i), 0))]
                + [_const_spec(a.shape) for a in weights]
                + [pl.BlockSpec(blk, lambda i: (tail(i), 0))
                   for blk in ((tl, D_MODEL), (tl * ROW_TILES, LANES), (tl, LANES))])
    q = HEADS_PER_GROUP * SSM_HEAD_DIM
    out_shape = [
        jax.ShapeDtypeStruct((n_tok, D_MODEL), F32),
        jax.ShapeDtypeStruct((n_tok * ROW_TILES, LANES), F32),
        jax.ShapeDtypeStruct((n_tok, LANES), F32),
        jax.ShapeDtypeStruct((bsz, SUBLANES, C_CONV), F32),
        jax.ShapeDtypeStruct((bsz, SSM_GROUPS, q, SSM_STATE), F32),
    ]
    out_specs = [
        pl.BlockSpec((tl, D_MODEL), lambda i: (out_row(i), 0)),
        pl.BlockSpec((tl * ROW_TILES, LANES), lambda i: (out_row(i), 0)),
        pl.BlockSpec((tl, LANES), lambda i: (out_row(i), 0)),
        pl.BlockSpec((None, SUBLANES, C_CONV), lambda i: (main(i) // nt, 0, 0)),
        pl.BlockSpec((None, SSM_GROUPS, q, SSM_STATE), lambda i: (main(i) // nt, 0, 0, 0)),
    ]
    scratch = [
        pltpu.VMEM((tl + SUBLANES, C_CONV), F32),
        pltpu.VMEM((SSM_GROUPS, SSM_STATE, q), F32),
        pltpu.VMEM((tl, D_SSM), F32),
        pltpu.VMEM((tl, D_SSM + D_SG), BF16),
        pltpu.VMEM((tl, D_MODEL), F32),
    ]
    return pl.pallas_call(
        functools.partial(_prompt_mixer_kernel, tl=tl, nt=nt, n_main=n_main),
        grid=(n_main + 1 + n_s // tl,),
        in_specs=in_specs,
        out_specs=out_specs,
        out_shape=out_shape,
        scratch_shapes=scratch,
        compiler_params=pltpu.CompilerParams(
            dimension_semantics=("arbitrary",), vmem_limit_bytes=VMEM_LIMIT),
        name="prompt_mixer",
    )(x2, x2, *weights, h_s, hn2_s, route_s)


def _sample_front_kernel(
        x_ref, cprev_ref, nmw_ref, wz_ref, wxbc_ref, wdt_ref, wuv_ref, convw_ref, convb_ref,
        dtb_ref, alog_ref, dskip_ref, sgnw_ref, w4_ref, b4_ref, expand_ref,
        z_ref, ypart_ref, ea_ref, ysg_ref, v_ref, convn_ref, c_ref, b_ref, xw_ref, dec_ref,
        *, nb, nt):
    x = x_ref[...]
    hn = _rms(x, nmw_ref[...]).astype(BF16)
    z_ref[...] = _dot(hn, wz_ref[...])
    xbc = _dot(hn, wxbc_ref[...])
    dtr = _dot(hn, wdt_ref[...])
    uv = _gelu(_dot(hn, wuv_ref[...]))

    slab = lambda a, t: a[t * nb:(t + 1) * nb]
    full = [cprev_ref[k] for k in range(CONV_K - 1)] + [slab(xbc, t) for t in range(nt)]
    for k in range(CONV_K - 1):
        convn_ref[k] = full[nt + k]
    xs, bm, cm = [], [], []
    for t in range(nt):
        conv = convb_ref[...]
        for k in range(CONV_K):
            conv = conv + convw_ref[k:k + 1, :] * full[t + k]
        act = _silu(conv)
        xs.append(act[:, :D_SSM])
        bm.append(act[:, D_SSM:D_SSM + SSM_GROUPS * SSM_STATE])
        cm.append(act[:, D_SSM + SSM_GROUPS * SSM_STATE:])
        b_ref[t * nb:(t + 1) * nb, :] = bm[t]
        c_ref[t * nb:(t + 1) * nb, :] = cm[t]

    dt = _softplus(dtr + dtb_ref[...])
    a_row = -jnp.exp(alog_ref[...])
    dts = [slab(dt, t) for t in range(nt)]
    cum = []
    for t in range(nt):
        da = dts[t] * a_row
        cum.append(da if t == 0 else cum[t - 1] + da)
    dec_ref[...] = jnp.exp(cum[nt - 1])

    lane = lax.broadcasted_iota(jnp.int32, (nb, LANES), 1)
    first_group = lane < HEADS_PER_GROUP
    facs = [jnp.exp(cum[t]) for t in range(nt)]
    facs += [dts[s] * jnp.exp(cum[nt - 1] - cum[s]) for s in range(nt)]
    pairs = []
    for t in range(nt):
        for s in range(t + 1):
            cb = []
            for g in range(SSM_GROUPS):
                gl = slice(g * SSM_STATE, (g + 1) * SSM_STATE)
                cb.append(jnp.sum(cm[t][:, gl] * bm[s][:, gl], axis=1, keepdims=True))
            cbh = jnp.where(first_group, cb[0], cb[1])
            facs.append(jnp.exp(cum[t] - cum[s]) * dts[s] * cbh)
            pairs.append((t, s))
    fx = _dot_f32(jnp.concatenate(facs, axis=0), expand_ref[...])
    fslab = lambda i: fx[i * nb:(i + 1) * nb]
    for t in range(nt):
        ea_ref[t * nb:(t + 1) * nb, :] = fslab(t)
        xw_ref[t * nb:(t + 1) * nb, :] = xs[t] * fslab(nt + t)
    for t in range(nt):
        acc = dskip_ref[...] * xs[t]
        for i, (tt, s) in enumerate(pairs):
            if tt == t:
                acc = acc + fslab(2 * nt + i) * xs[s]
        ypart_ref[t * nb:(t + 1) * nb, :] = acc

    u = uv[:, :D_SG]
    v = _rms(uv[:, D_SG:], sgnw_ref[...])
    v_ref[...] = v
    for t in range(nt):
        sv = b4_ref[t:t + 1, :]
        for s in range(t + 1):
            sv = sv + w4_ref[t, s:s + 1, :] * slab(v, s)
        ysg_ref[t * nb:(t + 1) * nb, :] = slab(u, t) * sv


def _sample_front(x_slab, cprev, w, nb, nt):
    r = nb * nt
    weights = [w['nmw'], w['wz'], w['wxbc'], w['wdt'], w['wuv'], w['convw'], w['convb'], w['dtb'],
               w['alog'], w['dskip'], w['sgnw'], w['w4'], w['b4'], w['expand']]
    ins = [x_slab, cprev] + weights
    out_shape = [
        jax.ShapeDtypeStruct((r, D_SSM), F32),
        jax.ShapeDtypeStruct((r, D_SSM), F32),
        jax.ShapeDtypeStruct((r, D_SSM), F32),
        jax.ShapeDtypeStruct((r, D_SG), F32),
        jax.ShapeDtypeStruct((r, D_SG), F32),
        jax.ShapeDtypeStruct((CONV_K - 1, nb, C_CONV), F32),
        jax.ShapeDtypeStruct((r, SSM_GROUPS * SSM_STATE), F32),
        jax.ShapeDtypeStruct((r, SSM_GROUPS * SSM_STATE), F32),
        jax.ShapeDtypeStruct((r, D_SSM), F32),
        jax.ShapeDtypeStruct((nb, LANES), F32),
    ]
    return pl.pallas_call(
        functools.partial(_sample_front_kernel, nb=nb, nt=nt),
        grid=(1,),
        in_specs=[_const_spec(a.shape) for a in ins],
        out_specs=[_const_spec(s.shape) for s in out_shape],
        out_shape=out_shape,
        compiler_params=pltpu.CompilerParams(
            dimension_semantics=("arbitrary",), vmem_limit_bytes=VMEM_LIMIT),
        name="sample_front",
    )(*ins)


def _sample_state_kernel(dec_ref, st_ref, cq_ref, bq_ref, xw_ref, so_ref, z_ref, *, seqs):
    for bb in range(seqs):
        for g in range(SSM_GROUPS):
            s0 = st_ref[bb, g]
            z_ref[bb, g] = lax.dot_general(
                cq_ref[bb, g].astype(BF16), s0.astype(BF16), (((1,), (1,)), ((), ())),
                preferred_element_type=F32)
            upd = lax.dot_general(
                xw_ref[bb, g].astype(BF16), bq_ref[bb, g].astype(BF16), (((0,), (0,)), ((), ())),
                preferred_element_type=F32)
            for hh in range(HEADS_PER_GROUP):
                hs = slice(hh * SSM_HEAD_DIM, (hh + 1) * SSM_HEAD_DIM)
                so_ref[bb, g, hs, :] = s0[hs] * dec_ref[bb, g * HEADS_PER_GROUP + hh] + upd[hs]


def _sample_state(dec, state, cq, bq, xwq, seqs):
    nb = state.shape[0]
    q = HEADS_PER_GROUP * SSM_HEAD_DIM
    blk = lambda *tail: pl.BlockSpec((seqs, SSM_GROUPS) + tail, lambda i: (i, 0, 0, 0))
    return pl.pallas_call(
        functools.partial(_sample_state_kernel, seqs=seqs),
        grid=(nb // seqs,),
        in_specs=[
            pl.BlockSpec((seqs, SSM_HEADS), lambda i: (i, 0), memory_space=pltpu.SMEM),
            blk(q, SSM_STATE), blk(SUBLANES, SSM_STATE), blk(SUBLANES, SSM_STATE), blk(SUBLANES, q),
        ],
        out_specs=[blk(q, SSM_STATE), blk(SUBLANES, q)],
        out_shape=[
            jax.ShapeDtypeStruct((nb, SSM_GROUPS, q, SSM_STATE), F32),
            jax.ShapeDtypeStruct((nb, SSM_GROUPS, SUBLANES, q), F32),
        ],
        compiler_params=pltpu.CompilerParams(
            dimension_semantics=("arbitrary",), vmem_limit_bytes=VMEM_LIMIT),
        name="sample_state",
    )(dec, state, cq, bq, xwq)


def _sample_back_kernel(x_ref, ypart_ref, ea_ref, zoff_ref, z_ref, ysg_ref, ssmnw_ref, wout_ref,
                        nfw_ref, wr_ref, br_ref, h_ref, hn2_ref, route_ref):
    y = ypart_ref[...] + ea_ref[...] * zoff_ref[...]
    y_ssm = _gated_group_norm(y, z_ref[...], ssmnw_ref[...])
    mix_in = jnp.concatenate([y_ssm.astype(BF16), ysg_ref[...].astype(BF16)], axis=1)
    h = x_ref[...] + _dot(mix_in, wout_ref[...])
    h_ref[...] = h
    hn2, route = _ffn_front(h, nfw_ref[...], wr_ref, br_ref[...])
    _store_row_tiles(hn2_ref, hn2)
    route_ref[...] = route


def _sample_back(x_slab, ypart, ea, zoff, z, ysg, w):
    r = x_slab.shape[0]
    ins = [x_slab, ypart, ea, zoff, z, ysg, w['ssmnw'], w['wout'], w['nfw'], w['wr'], w['br']]
    out_shape = [
        jax.ShapeDtypeStruct((r, D_MODEL), F32),
        jax.ShapeDtypeStruct((r * ROW_TILES, LANES), F32),
        jax.ShapeDtypeStruct((r, LANES), F32),
    ]
    return pl.pallas_call(
        _sample_back_kernel,
        grid=(1,),
        in_specs=[_const_spec(a.shape) for a in ins],
        out_specs=[_const_spec(s.shape) for s in out_shape],
        out_shape=out_shape,
        compiler_params=pltpu.CompilerParams(
            dimension_semantics=("arbitrary",), vmem_limit_bytes=VMEM_LIMIT),
        name="sample_back",
    )(*ins)


EXPERT_PIECES = 8
DMA_PIECES = 4


def _expert_mlp_kernel(be_ref, nu_ref, tok0_ref, tokn_ref, dstp_ref, x_hbm, wg_ref, wu_ref, wd_ref,
                       parts_hbm, xbuf, obuf, xb, hb, wg_b, wu_b, wd_b, gsem, ssem, *, bm):
    j = pl.program_id(0)
    nu = nu_ref[0]
    nt = ROW_TILES
    tile = lambda ref, start: ref.at[pl.ds(pl.multiple_of(start, nt), nt)]

    def gather_copy(idx_ref, s, r):
        return pltpu.make_async_copy(tile(x_hbm, idx_ref[0, r]), xbuf.at[s, pl.ds(r * nt, nt)], gsem.at[s])

    def scatter_copy(s, r):
        return pltpu.make_async_copy(obuf.at[s, pl.ds(r * nt, nt)], tile(parts_hbm, dstp_ref[0, r]),
                                     ssem.at[s])

    def gather_wait(s):
        pltpu.make_async_copy(x_hbm.at[pl.ds(0, bm * nt)], xbuf.at[s], gsem.at[s]).wait()

    def scatter_wait(s):
        pltpu.make_async_copy(obuf.at[s], parts_hbm.at[pl.ds(0, bm * nt)], ssem.at[s]).wait()

    def step(slot):
        gather_wait(slot)

        @pl.when(j >= 1)
        def _():
            scatter_wait(slot)

        per = bm // DMA_PIECES

        def start_rows(piece):
            if piece >= DMA_PIECES:
                return
            for r in range(piece * per, (piece + 1) * per):
                gather_copy(tokn_ref, 1 - slot, r).start(priority=r % 2)
                scatter_copy(1 - slot, r).start(priority=(r + 1) % 2)

        xb[...] = _load_row_tiles(xbuf.at[slot]).astype(BF16)
        half = D_FF // 2
        for c in range(2):
            cols = slice(c * half, (c + 1) * half)
            start_rows(2 * c)
            g = _dot(xb[...], wg_b[:, cols])
            start_rows(2 * c + 1)
            u = _dot(xb[...], wu_b[:, cols])
            hb[:, cols] = (_silu(g) * u).astype(BF16)
        quarter = D_MODEL // 4
        for c in range(4):
            start_rows(4 + c)
            o = _dot(hb[...], wd_b[:, c * quarter:(c + 1) * quarter])
            for k in range(quarter // LANES):
                kk = c * (quarter // LANES) + k
                obuf.at[slot][pl.ds(kk, bm, stride=nt), :] = o[:, k * LANES:(k + 1) * LANES]

    def drain(slot):
        gather_wait(slot)
        scatter_wait(slot)
        for r in range(bm):
            scatter_copy(1 - slot, r).start(priority=r % 2)
        scatter_wait(1 - slot)

    @pl.when(j == 0)
    def _():
        obuf[...] = jnp.zeros(obuf.shape, F32)
        n_real = parts_hbm.shape[0] - bm * nt
        spare = pltpu.make_async_copy(obuf.at[0], parts_hbm.at[pl.ds(n_real, bm * nt)], ssem.at[0])
        spare.start()
        spare.wait()

        def start0(r, carry):
            pltpu.make_async_copy(tile(x_hbm, tok0_ref[0, r]), xbuf.at[0, pl.ds(pl.multiple_of(r * nt, nt), nt)],
                                  gsem.at[0]).start()
            return carry
        lax.fori_loop(0, bm, start0, 0)

    used = j < nu

    @pl.when(used & ((j == 0) | (be_ref[j] != be_ref[jnp.maximum(j - 1, 0)])))
    def _():
        wg_b[...] = wg_ref[...].astype(BF16)
        wu_b[...] = wu_ref[...].astype(BF16)
        wd_b[...] = wd_ref[...].astype(BF16)

    for slot in range(2):
        parity = lax.rem(j, 2) == slot

        @pl.when(used & parity)
        def _(slot=slot):
            step(slot)

        @pl.when((j == nu) & parity)
        def _(slot=slot):
            drain(slot)


def _expert_mlp(blk_expert, n_used, tok, dst, x, wg, wu, wd, bm, n_out):
    n_blocks = blk_expert.shape[0]
    assert bm % EXPERT_PIECES == 0
    w_map = lambda j, be, nu: (be[j], 0, 0)
    idx_spec = lambda f: pl.BlockSpec((None, 1, bm), lambda j, be, nu: (f(j), 0, 0),
                                      memory_space=pltpu.SMEM)
    any_spec = pl.BlockSpec(memory_space=pl.ANY)
    grid_spec = pltpu.PrefetchScalarGridSpec(
        num_scalar_prefetch=2,
        grid=(n_blocks + 1,),
        in_specs=[
            idx_spec(lambda j: 0),
            idx_spec(lambda j: jnp.minimum(j + 1, n_blocks - 1)),
            idx_spec(lambda j: j),
            any_spec,
            pl.BlockSpec((None, D_MODEL, D_FF), w_map),
            pl.BlockSpec((None, D_MODEL, D_FF), w_map),
            pl.BlockSpec((None, D_FF, D_MODEL), w_map),
        ],
        out_specs=any_spec,
        scratch_shapes=[
            pltpu.VMEM((2, bm * ROW_TILES, LANES), F32),
            pltpu.VMEM((2, bm * ROW_TILES, LANES), F32),
            pltpu.VMEM((bm, D_MODEL), BF16),
            pltpu.VMEM((bm, D_FF), BF16),
            pltpu.VMEM((D_MODEL, D_FF), BF16),
            pltpu.VMEM((D_MODEL, D_FF), BF16),
            pltpu.VMEM((D_FF, D_MODEL), BF16),
            pltpu.SemaphoreType.DMA((2,)),
            pltpu.SemaphoreType.DMA((2,)),
        ],
    )
    tok3 = (tok * ROW_TILES).reshape(n_blocks, 1, bm)
    spare = (n_out - bm + jnp.arange(bm, dtype=jnp.int32)).reshape(1, bm)
    dst_prev = (jnp.concatenate([spare, dst], axis=0) * ROW_TILES).reshape(n_blocks + 1, 1, bm)
    blk_expert = jnp.concatenate([blk_expert, blk_expert[-1:]])
    return pl.pallas_call(
        functools.partial(_expert_mlp_kernel, bm=bm),
        grid_spec=grid_spec,
        out_shape=jax.ShapeDtypeStruct((n_out * ROW_TILES, LANES), F32),
        compiler_params=pltpu.CompilerParams(
            dimension_semantics=("arbitrary",), vmem_limit_bytes=VMEM_LIMIT),
        name="expert_mlp",
    )(blk_expert, n_used, tok3, tok3, dst_prev, x, wg, wu, wd)


PAD_ID = 2 ** 16 - 1


def _moe_plan(e_idx, bm, n_blocks):
    n_tok = e_idx.shape[0]
    n_asg = 2 * n_tok
    n_pad = n_blocks * bm - n_asg
    e_flat = e_idx.reshape(-1)
    experts = jnp.arange(N_EXPERTS, dtype=jnp.int32)
    counts = jnp.sum((e_flat[:, None] == experts[None, :]).astype(jnp.int32), axis=0)
    nblk = (counts + bm - 1) // bm
    blk_end = jnp.cumsum(nblk)
    n_used = blk_end[-1]
    j = jnp.arange(n_blocks, dtype=jnp.int32)
    be = jnp.sum((blk_end[None, :] <= jnp.minimum(j, n_used - 1)[:, None]).astype(jnp.int32), axis=1)
    be = jnp.minimum(be, N_EXPERTS - 1)
    assert n_asg < PAD_ID
    pad_end = jnp.cumsum(nblk * bm - counts)
    pad_expert = jnp.sum((pad_end[None, :] <= jnp.arange(n_pad, dtype=jnp.int32)[:, None]).astype(jnp.int32),
                         axis=1)
    keys = jnp.concatenate([2 * e_flat, 2 * pad_expert + 1])
    ids = jnp.concatenate([jnp.arange(n_asg, dtype=jnp.int32), jnp.full((n_pad,), PAD_ID, jnp.int32)])
    window = lax.sort(keys * (PAD_ID + 1) + ids) % (PAD_ID + 1)
    window = window.reshape(n_blocks, bm)
    valid = window != PAD_ID
    q = jnp.arange(bm, dtype=jnp.int32)
    tok = jnp.where(valid, window // 2, (j[:, None] * bm + q[None, :]) % n_tok)
    dst = jnp.where(valid, (window % 2) * n_tok + window // 2,
                    2 * n_tok + q[None, :])
    return tok.astype(jnp.int32), dst.astype(jnp.int32), be, n_used.astype(jnp.int32).reshape(1)


def _final_kernel(h_ref, p0_ref, p1_ref, route_ref, w_ref, op_ref, os_ref, *, n_prompt_steps):
    route = route_ref[...]
    p0 = _load_row_tiles(p0_ref)
    p1 = _load_row_tiles(p1_ref)
    y = _rms(h_ref[...] + route[:, 2:3] * p0 + route[:, 3:4] * p1, w_ref[...])
    i = pl.program_id(0)

    @pl.when(i < n_prompt_steps)
    def _():
        op_ref[...] = y

    @pl.when(i >= n_prompt_steps)
    def _():
        os_ref[...] = y


def _final(h, parts, route, w, tf, n_prompt):
    n_tok = h.shape[0]
    assert n_tok % tf == 0 and n_prompt % tf == 0
    n_p = n_prompt // tf
    row = lambda d: pl.BlockSpec((tf, d), lambda i: (i, 0))
    part = lambda k: pl.BlockSpec((tf * ROW_TILES, LANES), lambda i: (k * (n_tok // tf) + i, 0))
    return pl.pallas_call(
        functools.partial(_final_kernel, n_prompt_steps=n_p),
        grid=(n_tok // tf,),
        in_specs=[row(D_MODEL), part(0), part(1), row(LANES), _const_spec(w.shape)],
        out_specs=[pl.BlockSpec((tf, D_MODEL), lambda i: (jnp.minimum(i, n_p - 1), 0)),
                   pl.BlockSpec((tf, D_MODEL), lambda i: (jnp.maximum(i - n_p, 0), 0))],
        out_shape=[jax.ShapeDtypeStruct((n_prompt, D_MODEL), F32),
                   jax.ShapeDtypeStruct((n_tok - n_prompt, D_MODEL), F32)],
        compiler_params=pltpu.CompilerParams(
            dimension_semantics=("arbitrary",), vmem_limit_bytes=VMEM_LIMIT),
        name="final_combine",
    )(h, parts, parts, route, w)


PROMPT_TILE = 256
MOE_BLOCK_ROWS = 256
FINAL_TILE = 512
STATE_SEQS = 8


def _pad_lanes(a, width=LANES):
    return jnp.pad(a, [(0, 0)] * (a.ndim - 1) + [(0, width - a.shape[-1])])


def _layer_weights(l, norm_mix_w, w_in, conv_w, conv_b, dt_bias, a_log, d_skip, ssm_norm_w, sg_norm_w,
                   sg_w, sg_b, w_out, norm_ffn_w, w_router_group, b_router_group, w_router_expert,
                   b_router_expert, n_dec):
    c0, c1, c2 = D_SSM, D_SSM + C_CONV, D_SSM + C_CONV + SSM_HEADS
    wi = w_in[l]
    causal = jnp.tril(jnp.ones((CHUNK, CHUNK), bool))
    sgw_l = jnp.where(causal, sg_w[l], 0.0)
    head_of_lane = jnp.arange(D_SSM, dtype=jnp.int32) // SSM_HEAD_DIM
    expand = (jnp.arange(LANES, dtype=jnp.int32)[:, None] == head_of_lane[None, :]).astype(F32)
    row = lambda a: a.reshape(1, -1).astype(F32)
    gap = ROUTER_EXPERT_ROW - N_EXPERT_GROUPS
    rest = LANES - ROUTER_EXPERT_ROW - N_EXPERTS
    wrt = jnp.concatenate([w_router_group[l].T, jnp.zeros((gap, D_MODEL), F32), w_router_expert[l].T,
                           jnp.zeros((rest, D_MODEL), F32)], axis=0).astype(F32)
    brc = jnp.concatenate([b_router_group[l], jnp.zeros((gap,), F32), b_router_expert[l],
                           jnp.zeros((rest,), F32)]).astype(F32).reshape(LANES, 1)
    return {
        'nmw': row(norm_mix_w[l]),
        'wz': wi[:, :c0].astype(BF16),
        'wxbc': wi[:, c0:c1].astype(BF16),
        'wdt': _pad_lanes(wi[:, c1:c2]).astype(BF16),
        'wuv': wi[:, c2:].astype(BF16),
        'convw': conv_w[l].astype(F32),
        'convb': row(conv_b[l]),
        'dtb': _pad_lanes(row(dt_bias[l])),
        'alog': _pad_lanes(row(a_log[l])),
        'dskip': row(jnp.repeat(d_skip[l], SSM_HEAD_DIM)),
        'ssmnw': row(ssm_norm_w[l]),
        'sgnw': row(sg_norm_w[l]),
        'sgw': sgw_l.astype(BF16),
        'sgb': jnp.repeat(sg_b[l].T, SG_HEAD_DIM, axis=1).astype(F32),
        'w4': jnp.repeat(jnp.transpose(sgw_l[:, :n_dec, :n_dec], (1, 2, 0)), SG_HEAD_DIM, axis=2).astype(F32),
        'b4': jnp.repeat(sg_b[l][:, :n_dec].T, SG_HEAD_DIM, axis=1).astype(F32),
        'expand': expand,
        'wout': w_out[l].astype(BF16),
        'nfw': row(norm_ffn_w[l]),
        'wr': jnp.stack([wrt.astype(BF16), (wrt - wrt.astype(BF16).astype(F32)).astype(BF16)]),
        'br': brc,
    }


def _layer(l, xp, xs_slab, state_conv, state_ssm, w, w_gate, w_up, w_down):
    bp, seq, _ = xp.shape
    nb, nt = state_conv.shape[1], xs_slab.shape[0] // state_conv.shape[1]
    q = HEADS_PER_GROUP * SSM_HEAD_DIM

    cprev = jnp.transpose(state_conv[l], (1, 0, 2))
    z, ypart, ea, ysg, v, convn, cmat, bmat, xw, dec = _sample_front(xs_slab, cprev, w, nb, nt)
    to_seq = lambda a, d: jnp.pad(
        jnp.transpose(a.reshape(nt, nb, SSM_GROUPS, d), (1, 2, 0, 3)),
        ((0, 0), (0, 0), (0, SUBLANES - nt), (0, 0)))
    ssm_s, zoff = _sample_state(
        dec[:, :SSM_HEADS], state_ssm[l].reshape(nb, SSM_GROUPS, q, SSM_STATE),
        to_seq(cmat, SSM_STATE), to_seq(bmat, SSM_STATE), to_seq(xw, q), STATE_SEQS)
    zoff = jnp.transpose(zoff[:, :, :nt], (2, 0, 1, 3)).reshape(nt * nb, D_SSM)
    h_s, hn2_s, route_s = _sample_back(xs_slab, ypart, ea, zoff, z, ysg, w)

    h, hn2, route, convt_p, ssm_p = _prompt_mixer(xp, w, PROMPT_TILE, h_s, hn2_s, route_s)

    n_tok = h.shape[0]
    bm = MOE_BLOCK_ROWS
    n_blocks = -(-(2 * n_tok) // bm) + N_EXPERTS
    tok, dst, blk_expert, n_used = _moe_plan(route[:, :2].astype(jnp.int32), bm, n_blocks)
    parts = _expert_mlp(blk_expert, n_used, tok, dst, hn2, w_gate[l], w_up[l], w_down[l], bm,
                        2 * n_tok + bm)
    outs = dict(
        h=h, parts=parts, route=route,
        conv_p=convt_p[:, SUBLANES - (CONV_K - 1):],
        ssm_p=ssm_p.reshape(bp, SSM_GROUPS, HEADS_PER_GROUP, SSM_HEAD_DIM, SSM_STATE),
        conv_s=jnp.transpose(convn, (1, 0, 2)),
        ssm_s=ssm_s.reshape(nb, SSM_GROUPS, HEADS_PER_GROUP, SSM_HEAD_DIM, SSM_STATE),
        v_s=jnp.transpose(v.reshape(nt, nb, D_SG), (1, 0, 2)),
    )
    return outs


def kernel(x_prompt, x_sample, state_conv, state_ssm, norm_mix_w, w_in, conv_w, conv_b, dt_bias, a_log, d_skip, ssm_norm_w, sg_norm_w, sg_w, sg_b, w_out, norm_ffn_w, w_router_group, b_router_group, w_router_expert, b_router_expert, w_gate, w_up, w_down, norm_final_w):
    depth = w_in.shape[0]
    assert depth == 1, "the fused final norm assumes a single layer"
    bp, seq, _ = x_prompt.shape
    nb, nt, _ = x_sample.shape
    l = 0
    w = _layer_weights(l, norm_mix_w, w_in, conv_w, conv_b, dt_bias, a_log, d_skip, ssm_norm_w, sg_norm_w,
                       sg_w, sg_b, w_out, norm_ffn_w, w_router_group, b_router_group, w_router_expert,
                       b_router_expert, nt)
    xs_slab = jnp.transpose(x_sample, (1, 0, 2)).reshape(nt * nb, D_MODEL)
    outs = _layer(l, x_prompt, xs_slab, state_conv, state_ssm, w, w_gate, w_up, w_down)
    y_p, y_s = _final(outs['h'], outs['parts'], outs['route'], norm_final_w.reshape(1, -1).astype(F32),
                      FINAL_TILE, bp * seq)
    y_prompt = y_p.reshape(bp, seq, D_MODEL)
    y_sample = jnp.transpose(y_s.reshape(nt, nb, D_MODEL), (1, 0, 2))
    return (y_prompt, y_sample, outs['conv_p'][None], outs['ssm_p'][None], outs['conv_s'][None],
            outs['ssm_s'][None], outs['v_s'][None])
```

```python
import functools

import jax
import jax.numpy as jnp
from jax import lax
from jax.experimental import pallas as pl
from jax.experimental.pallas import tpu as pltpu

D_MODEL = 1024
D_SSM = 1024
SSM_HEAD_DIM = 64
SSM_HEADS = 16
SSM_GROUPS = 2
HEADS_PER_GROUP = 8
SSM_STATE = 128
CONV_K = 4
C_CONV = D_SSM + 2 * SSM_GROUPS * SSM_STATE
D_SG = 1024
SG_HEADS = 8
SG_HEAD_DIM = 128
CHUNK = 128
N_EXPERT_GROUPS = 4
EXPERTS_PER_GROUP = 8
N_EXPERTS = 32
D_FF = 512
EPS = 1e-6

LANES = 128
SUBLANES = 8
VMEM_LIMIT = 56 * 1024 * 1024

F32 = jnp.float32
BF16 = jnp.bfloat16
HIGHEST = lax.Precision.HIGHEST


def _dot(a, b):
    return jnp.dot(a, b, preferred_element_type=F32)


def _dot_f32(a, b):
    return jnp.dot(a, b, preferred_element_type=F32, precision=HIGHEST)


def _rms(x, w):
    return x * lax.rsqrt(jnp.mean(x * x, axis=-1, keepdims=True) + EPS) * w


def _silu(x):
    return x * (1.0 / (1.0 + jnp.exp(-x)))


def _gelu(x):
    return 0.5 * x * (1.0 + lax.erf(x * 0.7071067811865476))


def _softplus(x):
    return jnp.maximum(x, 0.0) + jnp.log1p(jnp.exp(-jnp.abs(x)))


def _gated_group_norm(y, z, w):
    g = y * _silu(z)
    half = D_SSM // SSM_GROUPS
    parts = []
    for k in range(SSM_GROUPS):
        gk = g[:, k * half:(k + 1) * half]
        parts.append(gk * lax.rsqrt(jnp.mean(gk * gk, axis=-1, keepdims=True) + EPS))
    return jnp.concatenate(parts, axis=1) * w


ROUTER_GROUP_ROW = 0
ROUTER_EXPERT_ROW = SUBLANES


def _route_t(lt):
    r = lt.shape[1]
    row = lax.broadcasted_iota(jnp.int32, (SUBLANES, r), 0)
    rowf = row.astype(F32)
    big = float(SUBLANES)
    red = lambda f, a: f(a, axis=0, keepdims=True)
    gl = jnp.where(row < N_EXPERT_GROUPS, lt[ROUTER_GROUP_ROW:ROUTER_GROUP_ROW + SUBLANES], -jnp.inf)
    ge = jnp.exp(gl - red(jnp.max, gl))
    p_grp = ge / red(jnp.sum, ge)
    g_p = red(jnp.max, p_grp)
    g_idx = red(jnp.min, jnp.where(p_grp == g_p, rowf, big))
    el = lt[ROUTER_EXPERT_ROW:ROUTER_EXPERT_ROW + EXPERTS_PER_GROUP]
    for g in range(1, N_EXPERT_GROUPS):
        lo = ROUTER_EXPERT_ROW + g * EXPERTS_PER_GROUP
        el = jnp.where(g_idx == float(g), lt[lo:lo + EXPERTS_PER_GROUP], el)
    ee = jnp.exp(el - red(jnp.max, el))
    pe = ee / red(jnp.sum, ee)
    v1 = red(jnp.max, pe)
    i1 = red(jnp.min, jnp.where(pe == v1, rowf, big))
    rest = rowf != i1
    pe2 = jnp.where(rest, pe, -1.0)
    v2 = red(jnp.max, pe2)
    i2 = red(jnp.min, jnp.where(rest & (pe2 == v2), rowf, big))
    den = v1 + v2
    base = g_idx * float(EXPERTS_PER_GROUP)
    out = jnp.where(row == 0, base + i1, 0.0)
    out = jnp.where(row == 1, base + i2, out)
    out = jnp.where(row == 2, g_p * v1 / den, out)
    out = jnp.where(row == 3, g_p * v2 / den, out)
    return jnp.concatenate([out, jnp.zeros((LANES - SUBLANES, r), F32)], axis=0).T


def _ffn_front(h, nfw, wrt_ref, brc):
    hn2 = _rms(h, nfw)
    hi = hn2.astype(BF16)
    lo = (hn2 - hi.astype(F32)).astype(BF16)
    nt = lambda a, b: lax.dot_general(a, b, (((1,), (1,)), ((), ())), preferred_element_type=F32)
    lt = nt(wrt_ref[0], hi) + nt(wrt_ref[0], lo) + nt(wrt_ref[1], hi) + brc
    return hn2, _route_t(lt)


ROW_TILES = D_MODEL // LANES


def _store_row_tiles(ref, val):
    r = val.shape[0]
    for k in range(ROW_TILES):
        ref[pl.ds(k, r, stride=ROW_TILES), :] = val[:, k * LANES:(k + 1) * LANES]


def _load_row_tiles(ref):
    r = ref.shape[0] // ROW_TILES
    return jnp.concatenate([ref[pl.ds(k, r, stride=ROW_TILES), :] for k in range(ROW_TILES)], axis=1)


def _prompt_step(
        t, nt, fillers, x_ref, nmw_ref, wz_ref, wxbc_ref, wdt_ref, wuv_ref, convw_ref, convb_ref, dtb_ref,
        alog_ref, dskip_ref, ssmnw_ref, sgnw_ref, sgw_ref, sgb_ref, convt_ref, ssm_ref,
        xbc_scr, state_scr, y_scr, mix_scr, tl):
    fillers = list(fillers)
    assert len(fillers) == 3 + tl // CHUNK

    @pl.when(t == 0)
    def _():
        xbc_scr[0:SUBLANES, :] = jnp.zeros((SUBLANES, C_CONV), F32)
        state_scr[...] = jnp.zeros(state_scr.shape, F32)

    x = x_ref[...]
    hn = _rms(x, nmw_ref[...]).astype(BF16)
    z = _dot(hn, wz_ref[...])
    xbc = _dot(hn, wxbc_ref[...])
    dtr = _dot(hn, wdt_ref[...])
    uv = _gelu(_dot(hn, wuv_ref[...]))
    fillers.pop(0)()

    xbc_scr[SUBLANES:SUBLANES + tl, :] = xbc
    conv = convb_ref[...]
    for k in range(CONV_K):
        off = SUBLANES - (CONV_K - 1) + k
        conv = conv + convw_ref[k:k + 1, :] * xbc_scr[off:off + tl, :]
    tail = xbc_scr[tl:tl + SUBLANES, :]
    xbc_scr[0:SUBLANES, :] = tail
    convt_ref[...] = tail
    act = _silu(conv)
    xs = act[:, :D_SSM]
    bmat = act[:, D_SSM:D_SSM + SSM_GROUPS * SSM_STATE]
    cmat = act[:, D_SSM + SSM_GROUPS * SSM_STATE:]
    fillers.pop(0)()

    dt = _softplus(dtr + dtb_ref[...])
    a_row = -jnp.exp(alog_ref[...])

    li = lax.broadcasted_iota(jnp.int32, (CHUNK, CHUNK), 0)
    si = lax.broadcasted_iota(jnp.int32, (CHUNK, CHUNK), 1)
    causal = li >= si
    tri = jnp.where(causal, 1.0, 0.0).astype(F32)
    lo_half = si < SSM_HEAD_DIM

    for c in range(tl // CHUNK):
        rows = slice(c * CHUNK, (c + 1) * CHUNK)
        dt_c = dt[rows]
        acum = _dot_f32(tri, dt_c * a_row)
        acum_t = acum.T
        dt_t = dt_c.T
        w_t = dt_t * jnp.exp(acum_t[:, CHUNK - 1:CHUNK] - acum_t)
        xs_c = xs[rows]
        for g in range(SSM_GROUPS):
            gl = slice(g * SSM_STATE, (g + 1) * SSM_STATE)
            b_g = bmat[rows, gl]
            c_g = cmat[rows, gl]
            b_gt = b_g.T
            cb = _dot(c_g.astype(BF16), b_gt.astype(BF16))
            for j in range(HEADS_PER_GROUP // 2):
                pl_ = slice(g * 512 + j * LANES, g * 512 + (j + 1) * LANES)
                m_l, ec_l, s_l, ea_last = [], [], [], []
                for k in range(2):
                    hh = g * HEADS_PER_GROUP + 2 * j + k
                    colb = jnp.broadcast_to(acum[:, hh:hh + 1], (CHUNK, CHUNK))
                    rowb = jnp.broadcast_to(acum_t[hh:hh + 1, :], (CHUNK, CHUNK))
                    ea = jnp.exp(colb)
                    seg = jnp.where(causal, colb - rowb, 0.0)
                    lmat = jnp.where(causal, jnp.exp(seg), 0.0)
                    m_l.append(cb * lmat * jnp.broadcast_to(dt_t[hh:hh + 1, :], (CHUNK, CHUNK)))
                    ec_l.append(c_g * ea)
                    s_l.append(b_gt * jnp.broadcast_to(w_t[hh:hh + 1, :], (CHUNK, CHUNK)))
                    ea_last.append(ea[CHUNK - 1:CHUNK, :])
                xs_p = xs_c[:, pl_]
                st_p = state_scr[g, :, j * LANES:(j + 1) * LANES]
                rx = jnp.concatenate([jnp.where(lo_half, xs_p, 0.0),
                                      jnp.where(lo_half, 0.0, xs_p)], axis=0).astype(BF16)
                rs = jnp.concatenate([jnp.where(lo_half, st_p, 0.0),
                                      jnp.where(lo_half, 0.0, st_p)], axis=0).astype(BF16)
                lhs = jnp.concatenate(m_l + ec_l, axis=1).astype(BF16)
                y_p = _dot(lhs, jnp.concatenate([rx, rs], axis=0))
                y_scr[rows, pl_] = y_p
                dec = jnp.where(lo_half[0:1, :], ea_last[0], ea_last[1])
                upd = _dot(jnp.concatenate(s_l, axis=1).astype(BF16), rx)
                state_scr[g, :, j * LANES:(j + 1) * LANES] = st_p * dec + upd
        fillers.pop(0)()

    y = y_scr[...] + dskip_ref[...] * xs
    y_ssm = _gated_group_norm(y, z, ssmnw_ref[...])
    mix_scr[:, :D_SSM] = y_ssm.astype(BF16)

    u = uv[:, :D_SG]
    v = _rms(uv[:, D_SG:], sgnw_ref[...])
    vb = v.astype(BF16)
    for c in range(tl // CHUNK):
        rows = slice(c * CHUNK, (c + 1) * CHUNK)
        for hd in range(SG_HEADS):
            hl = slice(hd * SG_HEAD_DIM, (hd + 1) * SG_HEAD_DIM)
            sv = _dot(sgw_ref[hd], vb[rows, hl]) + sgb_ref[:, hl]
            mix_scr[rows, D_SSM + hd * SG_HEAD_DIM:D_SSM + (hd + 1) * SG_HEAD_DIM] = (
                u[rows, hl] * sv).astype(BF16)

    fillers.pop(0)()

    @pl.when(t == nt - 1)
    def _():
        for g in range(SSM_GROUPS):
            ssm_ref[g] = state_scr[g].T


def _prompt_mixer_kernel(
        x_ref, xp_ref, nmw_ref, wz_ref, wxbc_ref, wdt_ref, wuv_ref, convw_ref, convb_ref, dtb_ref,
        alog_ref, dskip_ref, ssmnw_ref, sgnw_ref, sgw_ref, sgb_ref, wout_ref, nfw_ref,
        wr_ref, br_ref, hs_ref, hn2s_ref, routes_ref,
        h_ref, hn2_ref, route_ref, convt_ref, ssm_ref,
        xbc_scr, state_scr, y_scr, mix_scr, h_scr, *, tl, nt, n_main):
    i = pl.program_id(0)

    @pl.when(i == 0)
    def _():
        mix_scr[...] = jnp.zeros(mix_scr.shape, BF16)

    @pl.when(i > n_main)
    def _():
        h_ref[...] = hs_ref[...]
        hn2_ref[...] = hn2s_ref[...]
        route_ref[...] = routes_ref[...]

    n_out_chunks = 2 + tl // CHUNK
    width = D_MODEL // n_out_chunks
    assert width % LANES == 0 and width * n_out_chunks == D_MODEL

    def out_chunk(c):
        cols = slice(c * width, (c + 1) * width)
        h_scr[:, cols] = xp_ref[:, cols] + _dot(mix_scr[...], wout_ref[:, cols])

    def finish():
        h = h_scr[...]
        h_ref[...] = h
        hn2, route = _ffn_front(h, nfw_ref[...], wr_ref, br_ref[...])
        _store_row_tiles(hn2_ref, hn2)
        route_ref[...] = route

    @pl.when(i <= n_main)
    def _():
        def out_chunks(cs):
            for c in cs:
                out_chunk(c)

        half = n_out_chunks // 2
        fillers = ([functools.partial(out_chunks, range(half)),
                    functools.partial(out_chunks, range(half, n_out_chunks)), finish]
                   + [lambda: None] * (tl // CHUNK))
        _prompt_step(
            lax.rem(i, nt), nt, fillers, x_ref, nmw_ref, wz_ref, wxbc_ref, wdt_ref, wuv_ref,
            convw_ref, convb_ref, dtb_ref, alog_ref, dskip_ref, ssmnw_ref, sgnw_ref, sgw_ref, sgb_ref,
            convt_ref, ssm_ref, xbc_scr, state_scr, y_scr, mix_scr, tl)


def _const_spec(shape):
    zeros = (0,) * len(shape)
    return pl.BlockSpec(shape, lambda *_: zeros)


def _prompt_mixer(x, w, tl, h_s, hn2_s, route_s):
    bsz, seq, _ = x.shape
    nt = seq // tl
    n_main = bsz * nt
    n_s = h_s.shape[0]
    assert seq % tl == 0 and n_s % tl == 0
    n_tok = bsz * seq + n_s
    x2 = x.reshape(bsz * seq, D_MODEL)
    main = lambda i: jnp.minimum(i, n_main - 1)
    prev = lambda i: jnp.clip(i - 1, 0, n_main - 1)
    tail = lambda i: jnp.maximum(i - n_main - 1, 0)
    out_row = lambda i: jnp.maximum(i - 1, 0)
    weights = [w['nmw'], w['wz'], w['wxbc'], w['wdt'], w['wuv'], w['convw'], w['convb'], w['dtb'],
               w['alog'], w['dskip'], w['ssmnw'], w['sgnw'], w['sgw'], w['sgb'], w['wout'], w['nfw'],
               w['wr'], w['br']]
    in_specs = ([pl.BlockSpec((tl, D_MODEL), lambda i: (main(i), 0)),
                 pl.BlockSpec((tl, D_MODEL), lambda i: (prev(i), 0))]
                + [_const_spec(a.shape) for a in weights]
                + [pl.BlockSpec(blk, lambda i: (tail(i), 0))
                   for blk in ((tl, D_MODEL), (tl * ROW_TILES, LANES), (tl, LANES))])
    q = HEADS_PER_GROUP * SSM_HEAD_DIM
    out_shape = [
        jax.ShapeDtypeStruct((n_tok, D_MODEL), F32),
        jax.ShapeDtypeStruct((n_tok * ROW_TILES, LANES), F32),
        jax.ShapeDtypeStruct((n_tok, LANES), F32),
        jax.ShapeDtypeStruct((bsz, SUBLANES, C_CONV), F32),
        jax.ShapeDtypeStruct((bsz, SSM_GROUPS, q, SSM_STATE), F32),
    ]
    out_specs = [
        pl.BlockSpec((tl, D_MODEL), lambda i: (out_row(i), 0)),
        pl.BlockSpec((tl * ROW_TILES, LANES), lambda i: (out_row(i), 0)),
        pl.BlockSpec((tl, LANES), lambda i: (out_row(i), 0)),
        pl.BlockSpec((None, SUBLANES, C_CONV), lambda i: (main(i) // nt, 0, 0)),
        pl.BlockSpec((None, SSM_GROUPS, q, SSM_STATE), lambda i: (main(i) // nt, 0, 0, 0)),
    ]
    scratch = [
        pltpu.VMEM((tl + SUBLANES, C_CONV), F32),
        pltpu.VMEM((SSM_GROUPS, SSM_STATE, q), F32),
        pltpu.VMEM((tl, D_SSM), F32),
        pltpu.VMEM((tl, D_SSM + D_SG), BF16),
        pltpu.VMEM((tl, D_MODEL), F32),
    ]
    return pl.pallas_call(
        functools.partial(_prompt_mixer_kernel, tl=tl, nt=nt, n_main=n_main),
        grid=(n_main + 1 + n_s // tl,),
        in_specs=in_specs,
        out_specs=out_specs,
        out_shape=out_shape,
        scratch_shapes=scratch,
        compiler_params=pltpu.CompilerParams(
            dimension_semantics=("arbitrary",), vmem_limit_bytes=VMEM_LIMIT),
        name="prompt_mixer",
    )(x2, x2, *weights, h_s, hn2_s, route_s)


def _sample_front_kernel(
        x_ref, cprev_ref, nmw_ref, wz_ref, wxbc_ref, wdt_ref, wuv_ref, convw_ref, convb_ref,
        dtb_ref, alog_ref, dskip_ref, sgnw_ref, w4_ref, b4_ref, expand_ref,
        z_ref, ypart_ref, ea_ref, ysg_ref, v_ref, convn_ref, c_ref, b_ref, xw_ref, dec_ref,
        *, nb, nt):
    x = x_ref[...]
    hn = _rms(x, nmw_ref[...]).astype(BF16)
    z_ref[...] = _dot(hn, wz_ref[...])
    xbc = _dot(hn, wxbc_ref[...])
    dtr = _dot(hn, wdt_ref[...])
    uv = _gelu(_dot(hn, wuv_ref[...]))

    slab = lambda a, t: a[t * nb:(t + 1) * nb]
    full = [cprev_ref[k] for k in range(CONV_K - 1)] + [slab(xbc, t) for t in range(nt)]
    for k in range(CONV_K - 1):
        convn_ref[k] = full[nt + k]
    xs, bm, cm = [], [], []
    for t in range(nt):
        conv = convb_ref[...]
        for k in range(CONV_K):
            conv = conv + convw_ref[k:k + 1, :] * full[t + k]
        act = _silu(conv)
        xs.append(act[:, :D_SSM])
        bm.append(act[:, D_SSM:D_SSM + SSM_GROUPS * SSM_STATE])
        cm.append(act[:, D_SSM + SSM_GROUPS * SSM_STATE:])
        b_ref[t * nb:(t + 1) * nb, :] = bm[t]
        c_ref[t * nb:(t + 1) * nb, :] = cm[t]

    dt = _softplus(dtr + dtb_ref[...])
    a_row = -jnp.exp(alog_ref[...])
    dts = [slab(dt, t) for t in range(nt)]
    cum = []
    for t in range(nt):
        da = dts[t] * a_row
        cum.append(da if t == 0 else cum[t - 1] + da)
    dec_ref[...] = jnp.exp(cum[nt - 1])

    lane = lax.broadcasted_iota(jnp.int32, (nb, LANES), 1)
    first_group = lane < HEADS_PER_GROUP
    facs = [jnp.exp(cum[t]) for t in range(nt)]
    facs += [dts[s] * jnp.exp(cum[nt - 1] - cum[s]) for s in range(nt)]
    pairs = []
    for t in range(nt):
        for s in range(t + 1):
            cb = []
            for g in range(SSM_GROUPS):
                gl = slice(g * SSM_STATE, (g + 1) * SSM_STATE)
                cb.append(jnp.sum(cm[t][:, gl] * bm[s][:, gl], axis=1, keepdims=True))
            cbh = jnp.where(first_group, cb[0], cb[1])
            facs.append(jnp.exp(cum[t] - cum[s]) * dts[s] * cbh)
            pairs.append((t, s))
    fx = _dot_f32(jnp.concatenate(facs, axis=0), expand_ref[...])
    fslab = lambda i: fx[i * nb:(i + 1) * nb]
    for t in range(nt):
        ea_ref[t * nb:(t + 1) * nb, :] = fslab(t)
        xw_ref[t * nb:(t + 1) * nb, :] = xs[t] * fslab(nt + t)
    for t in range(nt):
        acc = dskip_ref[...] * xs[t]
        for i, (tt, s) in enumerate(pairs):
            if tt == t:
                acc = acc + fslab(2 * nt + i) * xs[s]
        ypart_ref[t * nb:(t + 1) * nb, :] = acc

    u = uv[:, :D_SG]
    v = _rms(uv[:, D_SG:], sgnw_ref[...])
    v_ref[...] = v
    for t in range(nt):
        sv = b4_ref[t:t + 1, :]
        for s in range(t + 1):
            sv = sv + w4_ref[t, s:s + 1, :] * slab(v, s)
        ysg_ref[t * nb:(t + 1) * nb, :] = slab(u, t) * sv


def _sample_front(x_slab, cprev, w, nb, nt):
    r = nb * nt
    weights = [w['nmw'], w['wz'], w['wxbc'], w['wdt'], w['wuv'], w['convw'], w['convb'], w['dtb'],
               w['alog'], w['dskip'], w['sgnw'], w['w4'], w['b4'], w['expand']]
    ins = [x_slab, cprev] + weights
    out_shape = [
        jax.ShapeDtypeStruct((r, D_SSM), F32),
        jax.ShapeDtypeStruct((r, D_SSM), F32),
        jax.ShapeDtypeStruct((r, D_SSM), F32),
        jax.ShapeDtypeStruct((r, D_SG), F32),
        jax.ShapeDtypeStruct((r, D_SG), F32),
        jax.ShapeDtypeStruct((CONV_K - 1, nb, C_CONV), F32),
        jax.ShapeDtypeStruct((r, SSM_GROUPS * SSM_STATE), F32),
        jax.ShapeDtypeStruct((r, SSM_GROUPS * SSM_STATE), F32),
        jax.ShapeDtypeStruct((r, D_SSM), F32),
        jax.ShapeDtypeStruct((nb, LANES), F32),
    ]
    return pl.pallas_call(
        functools.partial(_sample_front_kernel, nb=nb, nt=nt),
        grid=(1,),
        in_specs=[_const_spec(a.shape) for a in ins],
        out_specs=[_const_spec(s.shape) for s in out_shape],
        out_shape=out_shape,
        compiler_params=pltpu.CompilerParams(
            dimension_semantics=("arbitrary",), vmem_limit_bytes=VMEM_LIMIT),
        name="sample_front",
    )(*ins)


def _sample_state_kernel(dec_ref, st_ref, cq_ref, bq_ref, xw_ref, so_ref, z_ref, *, seqs):
    for bb in range(seqs):
        for g in range(SSM_GROUPS):
            s0 = st_ref[bb, g]
            z_ref[bb, g] = lax.dot_general(
                cq_ref[bb, g].astype(BF16), s0.astype(BF16), (((1,), (1,)), ((), ())),
                preferred_element_type=F32)
            upd = lax.dot_general(
                xw_ref[bb, g].astype(BF16), bq_ref[bb, g].astype(BF16), (((0,), (0,)), ((), ())),
                preferred_element_type=F32)
            for hh in range(HEADS_PER_GROUP):
                hs = slice(hh * SSM_HEAD_DIM, (hh + 1) * SSM_HEAD_DIM)
                so_ref[bb, g, hs, :] = s0[hs] * dec_ref[bb, g * HEADS_PER_GROUP + hh] + upd[hs]


def _sample_state(dec, state, cq, bq, xwq, seqs):
    nb = state.shape[0]
    q = HEADS_PER_GROUP * SSM_HEAD_DIM
    blk = lambda *tail: pl.BlockSpec((seqs, SSM_GROUPS) + tail, lambda i: (i, 0, 0, 0))
    return pl.pallas_call(
        functools.partial(_sample_state_kernel, seqs=seqs),
        grid=(nb // seqs,),
        in_specs=[
            pl.BlockSpec((seqs, SSM_HEADS), lambda i: (i, 0), memory_space=pltpu.SMEM),
            blk(q, SSM_STATE), blk(SUBLANES, SSM_STATE), blk(SUBLANES, SSM_STATE), blk(SUBLANES, q),
        ],
        out_specs=[blk(q, SSM_STATE), blk(SUBLANES, q)],
        out_shape=[
            jax.ShapeDtypeStruct((nb, SSM_GROUPS, q, SSM_STATE), F32),
            jax.ShapeDtypeStruct((nb, SSM_GROUPS, SUBLANES, q), F32),
        ],
        compiler_params=pltpu.CompilerParams(
            dimension_semantics=("arbitrary",), vmem_limit_bytes=VMEM_LIMIT),
        name="sample_state",
    )(dec, state, cq, bq, xwq)


def _sample_back_kernel(x_ref, ypart_ref, ea_ref, zoff_ref, z_ref, ysg_ref, ssmnw_ref, wout_ref,
                        nfw_ref, wr_ref, br_ref, h_ref, hn2_ref, route_ref):
    y = ypart_ref[...] + ea_ref[...] * zoff_ref[...]
    y_ssm = _gated_group_norm(y, z_ref[...], ssmnw_ref[...])
    mix_in = jnp.concatenate([y_ssm.astype(BF16), ysg_ref[...].astype(BF16)], axis=1)
    h = x_ref[...] + _dot(mix_in, wout_ref[...])
    h_ref[...] = h
    hn2, route = _ffn_front(h, nfw_ref[...], wr_ref, br_ref[...])
    _store_row_tiles(hn2_ref, hn2)
    route_ref[...] = route


def _sample_back(x_slab, ypart, ea, zoff, z, ysg, w):
    r = x_slab.shape[0]
    ins = [x_slab, ypart, ea, zoff, z, ysg, w['ssmnw'], w['wout'], w['nfw'], w['wr'], w['br']]
    out_shape = [
        jax.ShapeDtypeStruct((r, D_MODEL), F32),
        jax.ShapeDtypeStruct((r * ROW_TILES, LANES), F32),
        jax.ShapeDtypeStruct((r, LANES), F32),
    ]
    return pl.pallas_call(
        _sample_back_kernel,
        grid=(1,),
        in_specs=[_const_spec(a.shape) for a in ins],
        out_specs=[_const_spec(s.shape) for s in out_shape],
        out_shape=out_shape,
        compiler_params=pltpu.CompilerParams(
            dimension_semantics=("arbitrary",), vmem_limit_bytes=VMEM_LIMIT),
        name="sample_back",
    )(*ins)


EXPERT_PIECES = 8
DMA_PIECES = 4


def _expert_mlp_kernel(be_ref, nu_ref, tok0_ref, tokn_ref, dstp_ref, x_hbm, wg_ref, wu_ref, wd_ref,
                       parts_hbm, xbuf, obuf, xb, hb, wg_b, wu_b, wd_b, gsem, ssem, *, bm):
    j = pl.program_id(0)
    nu = nu_ref[0]
    nt = ROW_TILES
    tile = lambda ref, start: ref.at[pl.ds(pl.multiple_of(start, nt), nt)]

    def gather_copy(idx_ref, s, r):
        return pltpu.make_async_copy(tile(x_hbm, idx_ref[0, r]), xbuf.at[s, pl.ds(r * nt, nt)], gsem.at[s])

    def scatter_copy(s, r):
        return pltpu.make_async_copy(obuf.at[s, pl.ds(r * nt, nt)], tile(parts_hbm, dstp_ref[0, r]),
                                     ssem.at[s])

    def gather_wait(s):
        pltpu.make_async_copy(x_hbm.at[pl.ds(0, bm * nt)], xbuf.at[s], gsem.at[s]).wait()

    def scatter_wait(s):
        pltpu.make_async_copy(obuf.at[s], parts_hbm.at[pl.ds(0, bm * nt)], ssem.at[s]).wait()

    def step(slot):
        gather_wait(slot)

        @pl.when(j >= 1)
        def _():
            scatter_wait(slot)

        per = bm // DMA_PIECES

        def start_rows(piece):
            if piece >= DMA_PIECES:
                return
            for r in range(piece * per, (piece + 1) * per):
                gather_copy(tokn_ref, 1 - slot, r).start(priority=r % 2)
                scatter_copy(1 - slot, r).start(priority=(r + 1) % 2)

        xb[...] = _load_row_tiles(xbuf.at[slot]).astype(BF16)
        half = D_FF // 2
        for c in range(2):
            cols = slice(c * half, (c + 1) * half)
            start_rows(2 * c)
            g = _dot(xb[...], wg_b[:, cols])
            start_rows(2 * c + 1)
            u = _dot(xb[...], wu_b[:, cols])
            hb[:, cols] = (_silu(g) * u).astype(BF16)
        quarter = D_MODEL // 4
        for c in range(4):
            start_rows(4 + c)
            o = _dot(hb[...], wd_b[:, c * quarter:(c + 1) * quarter])
            for k in range(quarter // LANES):
                kk = c * (quarter // LANES) + k
                obuf.at[slot][pl.ds(kk, bm, stride=nt), :] = o[:, k * LANES:(k + 1) * LANES]

    def drain(slot):
        gather_wait(slot)
        scatter_wait(slot)
        for r in range(bm):
            scatter_copy(1 - slot, r).start(priority=r % 2)
        scatter_wait(1 - slot)

    @pl.when(j == 0)
    def _():
        obuf[...] = jnp.zeros(obuf.shape, F32)
        n_real = parts_hbm.shape[0] - bm * nt
        spare = pltpu.make_async_copy(obuf.at[0], parts_hbm.at[pl.ds(n_real, bm * nt)], ssem.at[0])
        spare.start()
        spare.wait()

        def start0(r, carry):
            pltpu.make_async_copy(tile(x_hbm, tok0_ref[0, r]), xbuf.at[0, pl.ds(pl.multiple_of(r * nt, nt), nt)],
                                  gsem.at[0]).start()
            return carry
        lax.fori_loop(0, bm, start0, 0)

    used = j < nu

    @pl.when(used & ((j == 0) | (be_ref[j] != be_ref[jnp.maximum(j - 1, 0)])))
    def _():
        wg_b[...] = wg_ref[...].astype(BF16)
        wu_b[...] = wu_ref[...].astype(BF16)
        wd_b[...] = wd_ref[...].astype(BF16)

    for slot in range(2):
        parity = lax.rem(j, 2) == slot

        @pl.when(used & parity)
        def _(slot=slot):
            step(slot)

        @pl.when((j == nu) & parity)
        def _(slot=slot):
            drain(slot)


def _expert_mlp(blk_expert, n_used, tok, dst, x, wg, wu, wd, bm, n_out):
    n_blocks = blk_expert.shape[0]
    assert bm % EXPERT_PIECES == 0
    w_map = lambda j, be, nu: (be[j], 0, 0)
    idx_spec = lambda f: pl.BlockSpec((None, 1, bm), lambda j, be, nu: (f(j), 0, 0),
                                      memory_space=pltpu.SMEM)
    any_spec = pl.BlockSpec(memory_space=pl.ANY)
    grid_spec = pltpu.PrefetchScalarGridSpec(
        num_scalar_prefetch=2,
        grid=(n_blocks + 1,),
        in_specs=[
            idx_spec(lambda j: 0),
            idx_spec(lambda j: jnp.minimum(j + 1, n_blocks - 1)),
            idx_spec(lambda j: j),
            any_spec,
            pl.BlockSpec((None, D_MODEL, D_FF), w_map),
            pl.BlockSpec((None, D_MODEL, D_FF), w_map),
            pl.BlockSpec((None, D_FF, D_MODEL), w_map),
        ],
        out_specs=any_spec,
        scratch_shapes=[
            pltpu.VMEM((2, bm * ROW_TILES, LANES), F32),
            pltpu.VMEM((2, bm * ROW_TILES, LANES), F32),
            pltpu.VMEM((bm, D_MODEL), BF16),
            pltpu.VMEM((bm, D_FF), BF16),
            pltpu.VMEM((D_MODEL, D_FF), BF16),
            pltpu.VMEM((D_MODEL, D_FF), BF16),
            pltpu.VMEM((D_FF, D_MODEL), BF16),
            pltpu.SemaphoreType.DMA((2,)),
            pltpu.SemaphoreType.DMA((2,)),
        ],
    )
    tok3 = (tok * ROW_TILES).reshape(n_blocks, 1, bm)
    spare = (n_out - bm + jnp.arange(bm, dtype=jnp.int32)).reshape(1, bm)
    dst_prev = (jnp.concatenate([spare, dst], axis=0) * ROW_TILES).reshape(n_blocks + 1, 1, bm)
    blk_expert = jnp.concatenate([blk_expert, blk_expert[-1:]])
    return pl.pallas_call(
        functools.partial(_expert_mlp_kernel, bm=bm),
        grid_spec=grid_spec,
        out_shape=jax.ShapeDtypeStruct((n_out * ROW_TILES, LANES), F32),
        compiler_params=pltpu.CompilerParams(
            dimension_semantics=("arbitrary",), vmem_limit_bytes=VMEM_LIMIT),
        name="expert_mlp",
    )(blk_expert, n_used, tok3, tok3, dst_prev, x, wg, wu, wd)


PAD_ID = 2 ** 16 - 1


def _moe_plan(e_idx, bm, n_blocks):
    n_tok = e_idx.shape[0]
    n_asg = 2 * n_tok
    n_pad = n_blocks * bm - n_asg
    e_flat = e_idx.reshape(-1)
    experts = jnp.arange(N_EXPERTS, dtype=jnp.int32)
    counts = jnp.sum((e_flat[:, None] == experts[None, :]).astype(jnp.int32), axis=0)
    nblk = (counts + bm - 1) // bm
    blk_end = jnp.cumsum(nblk)
    n_used = blk_end[-1]
    j = jnp.arange(n_blocks, dtype=jnp.int32)
    be = jnp.sum((blk_end[None, :] <= jnp.minimum(j, n_used - 1)[:, None]).astype(jnp.int32), axis=1)
    be = jnp.minimum(be, N_EXPERTS - 1)
    assert n_asg < PAD_ID
    pad_end = jnp.cumsum(nblk * bm - counts)
    pad_expert = jnp.sum((pad_end[None, :] <= jnp.arange(n_pad, dtype=jnp.int32)[:, None]).astype(jnp.int32),
                         axis=1)
    keys = jnp.concatenate([2 * e_flat, 2 * pad_expert + 1])
    ids = jnp.concatenate([jnp.arange(n_asg, dtype=jnp.int32), jnp.full((n_pad,), PAD_ID, jnp.int32)])
    window = lax.sort(keys * (PAD_ID + 1) + ids) % (PAD_ID + 1)
    window = window.reshape(n_blocks, bm)
    valid = window != PAD_ID
    q = jnp.arange(bm, dtype=jnp.int32)
    tok = jnp.where(valid, window // 2, (j[:, None] * bm + q[None, :]) % n_tok)
    dst = jnp.where(valid, (window % 2) * n_tok + window // 2,
                    2 * n_tok + q[None, :])
    return tok.astype(jnp.int32), dst.astype(jnp.int32), be, n_used.astype(jnp.int32).reshape(1)


def _final_kernel(h_ref, p0_ref, p1_ref, route_ref, w_ref, op_ref, os_ref, *, n_prompt_steps):
    route = route_ref[...]
    p0 = _load_row_tiles(p0_ref)
    p1 = _load_row_tiles(p1_ref)
    y = _rms(h_ref[...] + route[:, 2:3] * p0 + route[:, 3:4] * p1, w_ref[...])
    i = pl.program_id(0)

    @pl.when(i < n_prompt_steps)
    def _():
        op_ref[...] = y

    @pl.when(i >= n_prompt_steps)
    def _():
        os_ref[...] = y


def _final(h, parts, route, w, tf, n_prompt):
    n_tok = h.shape[0]
    assert n_tok % tf == 0 and n_prompt % tf == 0
    n_p = n_prompt // tf
    row = lambda d: pl.BlockSpec((tf, d), lambda i: (i, 0))
    part = lambda k: pl.BlockSpec((tf * ROW_TILES, LANES), lambda i: (k * (n_tok // tf) + i, 0))
    return pl.pallas_call(
        functools.partial(_final_kernel, n_prompt_steps=n_p),
        grid=(n_tok // tf,),
        in_specs=[row(D_MODEL), part(0), part(1), row(LANES), _const_spec(w.shape)],
        out_specs=[pl.BlockSpec((tf, D_MODEL), lambda i: (jnp.minimum(i, n_p - 1), 0)),
                   pl.BlockSpec((tf, D_MODEL), lambda i: (jnp.maximum(i - n_p, 0), 0))],
        out_shape=[jax.ShapeDtypeStruct((n_prompt, D_MODEL), F32),
                   jax.ShapeDtypeStruct((n_tok - n_prompt, D_MODEL), F32)],
        compiler_params=pltpu.CompilerParams(
            dimension_semantics=("arbitrary",), vmem_limit_bytes=VMEM_LIMIT),
        name="final_combine",
    )(h, parts, parts, route, w)


PROMPT_TILE = 256
MOE_BLOCK_ROWS = 128
FINAL_TILE = 512
STATE_SEQS = 8


def _pad_lanes(a, width=LANES):
    return jnp.pad(a, [(0, 0)] * (a.ndim - 1) + [(0, width - a.shape[-1])])


def _layer_weights(l, norm_mix_w, w_in, conv_w, conv_b, dt_bias, a_log, d_skip, ssm_norm_w, sg_norm_w,
                   sg_w, sg_b, w_out, norm_ffn_w, w_router_group, b_router_group, w_router_expert,
                   b_router_expert, n_dec):
    c0, c1, c2 = D_SSM, D_SSM + C_CONV, D_SSM + C_CONV + SSM_HEADS
    wi = w_in[l]
    causal = jnp.tril(jnp.ones((CHUNK, CHUNK), bool))
    sgw_l = jnp.where(causal, sg_w[l], 0.0)
    head_of_lane = jnp.arange(D_SSM, dtype=jnp.int32) // SSM_HEAD_DIM
    expand = (jnp.arange(LANES, dtype=jnp.int32)[:, None] == head_of_lane[None, :]).astype(F32)
    row = lambda a: a.reshape(1, -1).astype(F32)
    gap = ROUTER_EXPERT_ROW - N_EXPERT_GROUPS
    rest = LANES - ROUTER_EXPERT_ROW - N_EXPERTS
    wrt = jnp.concatenate([w_router_group[l].T, jnp.zeros((gap, D_MODEL), F32), w_router_expert[l].T,
                           jnp.zeros((rest, D_MODEL), F32)], axis=0).astype(F32)
    brc = jnp.concatenate([b_router_group[l], jnp.zeros((gap,), F32), b_router_expert[l],
                           jnp.zeros((rest,), F32)]).astype(F32).reshape(LANES, 1)
    return {
        'nmw': row(norm_mix_w[l]),
        'wz': wi[:, :c0].astype(BF16),
        'wxbc': wi[:, c0:c1].astype(BF16),
        'wdt': _pad_lanes(wi[:, c1:c2]).astype(BF16),
        'wuv': wi[:, c2:].astype(BF16),
        'convw': conv_w[l].astype(F32),
        'convb': row(conv_b[l]),
        'dtb': _pad_lanes(row(dt_bias[l])),
        'alog': _pad_lanes(row(a_log[l])),
        'dskip': row(jnp.repeat(d_skip[l], SSM_HEAD_DIM)),
        'ssmnw': row(ssm_norm_w[l]),
        'sgnw': row(sg_norm_w[l]),
        'sgw': sgw_l.astype(BF16),
        'sgb': jnp.repeat(sg_b[l].T, SG_HEAD_DIM, axis=1).astype(F32),
        'w4': jnp.repeat(jnp.transpose(sgw_l[:, :n_dec, :n_dec], (1, 2, 0)), SG_HEAD_DIM, axis=2).astype(F32),
        'b4': jnp.repeat(sg_b[l][:, :n_dec].T, SG_HEAD_DIM, axis=1).astype(F32),
        'expand': expand,
        'wout': w_out[l].astype(BF16),
        'nfw': row(norm_ffn_w[l]),
        'wr': jnp.stack([wrt.astype(BF16), (wrt - wrt.astype(BF16).astype(F32)).astype(BF16)]),
        'br': brc,
    }


def _layer(l, xp, xs_slab, state_conv, state_ssm, w, w_gate, w_up, w_down):
    bp, seq, _ = xp.shape
    nb, nt = state_conv.shape[1], xs_slab.shape[0] // state_conv.shape[1]
    q = HEADS_PER_GROUP * SSM_HEAD_DIM

    cprev = jnp.transpose(state_conv[l], (1, 0, 2))
    z, ypart, ea, ysg, v, convn, cmat, bmat, xw, dec = _sample_front(xs_slab, cprev, w, nb, nt)
    to_seq = lambda a, d: jnp.pad(
        jnp.transpose(a.reshape(nt, nb, SSM_GROUPS, d), (1, 2, 0, 3)),
        ((0, 0), (0, 0), (0, SUBLANES - nt), (0, 0)))
    ssm_s, zoff = _sample_state(
        dec[:, :SSM_HEADS], state_ssm[l].reshape(nb, SSM_GROUPS, q, SSM_STATE),
        to_seq(cmat, SSM_STATE), to_seq(bmat, SSM_STATE), to_seq(xw, q), STATE_SEQS)
    zoff = jnp.transpose(zoff[:, :, :nt], (2, 0, 1, 3)).reshape(nt * nb, D_SSM)
    h_s, hn2_s, route_s = _sample_back(xs_slab, ypart, ea, zoff, z, ysg, w)

    h, hn2, route, convt_p, ssm_p = _prompt_mixer(xp, w, PROMPT_TILE, h_s, hn2_s, route_s)

    n_tok = h.shape[0]
    bm = MOE_BLOCK_ROWS
    n_blocks = -(-(2 * n_tok) // bm) + N_EXPERTS
    tok, dst, blk_expert, n_used = _moe_plan(route[:, :2].astype(jnp.int32), bm, n_blocks)
    parts = _expert_mlp(blk_expert, n_used, tok, dst, hn2, w_gate[l], w_up[l], w_down[l], bm,
                        2 * n_tok + bm)
    outs = dict(
        h=h, parts=parts, route=route,
        conv_p=convt_p[:, SUBLANES - (CONV_K - 1):],
        ssm_p=ssm_p.reshape(bp, SSM_GROUPS, HEADS_PER_GROUP, SSM_HEAD_DIM, SSM_STATE),
        conv_s=jnp.transpose(convn, (1, 0, 2)),
        ssm_s=ssm_s.reshape(nb, SSM_GROUPS, HEADS_PER_GROUP, SSM_HEAD_DIM, SSM_STATE),
        v_s=jnp.transpose(v.reshape(nt, nb, D_SG), (1, 0, 2)),
    )
    return outs


def kernel(x_prompt, x_sample, state_conv, state_ssm, norm_mix_w, w_in, conv_w, conv_b, dt_bias, a_log, d_skip, ssm_norm_w, sg_norm_w, sg_w, sg_b, w_out, norm_ffn_w, w_router_group, b_router_group, w_router_expert, b_router_expert, w_gate, w_up, w_down, norm_final_w):
    depth = w_in.shape[0]
    assert depth == 1, "the fused final norm assumes a single layer"
    bp, seq, _ = x_prompt.shape
    nb, nt, _ = x_sample.shape
    l = 0
    w = _layer_weights(l, norm_mix_w, w_in, conv_w, conv_b, dt_bias, a_log, d_skip, ssm_norm_w, sg_norm_w,
                       sg_w, sg_b, w_out, norm_ffn_w, w_router_group, b_router_group, w_router_expert,
                       b_router_expert, nt)
    xs_slab = jnp.transpose(x_sample, (1, 0, 2)).reshape(nt * nb, D_MODEL)
    outs = _layer(l, x_prompt, xs_slab, state_conv, state_ssm, w, w_gate, w_up, w_down)
    y_p, y_s = _final(outs['h'], outs['parts'], outs['route'], norm_final_w.reshape(1, -1).astype(F32),
                      FINAL_TILE, bp * seq)
    y_prompt = y_p.reshape(bp, seq, D_MODEL)
    y_sample = jnp.transpose(y_s.reshape(nt, nb, D_MODEL), (1, 0, 2))
    return (y_prompt, y_sample, outs['conv_p'][None], outs['ssm_p'][None], outs['conv_s'][None],
            outs['ssm_s'][None], outs['v_s'][None])
```

```python
import functools

import jax
import jax.numpy as jnp
from jax import lax
from jax.experimental import pallas as pl
from jax.experimental.pallas import tpu as pltpu

D_MODEL = 1024
D_SSM = 1024
SSM_HEAD_DIM = 64
SSM_HEADS = 16
SSM_GROUPS = 2
HEADS_PER_GROUP = 8
SSM_STATE = 128
CONV_K = 4
C_CONV = D_SSM + 2 * SSM_GROUPS * SSM_STATE
D_SG = 1024
SG_HEADS = 8
SG_HEAD_DIM = 128
CHUNK = 128
N_EXPERT_GROUPS = 4
EXPERTS_PER_GROUP = 8
N_EXPERTS = 32
D_FF = 512
EPS = 1e-6

LANES = 128
SUBLANES = 8
VMEM_LIMIT = 56 * 1024 * 1024

F32 = jnp.float32
BF16 = jnp.bfloat16
HIGHEST = lax.Precision.HIGHEST


def _dot(a, b):
    return jnp.dot(a, b, preferred_element_type=F32)


def _dot_f32(a, b):
    return jnp.dot(a, b, preferred_element_type=F32, precision=HIGHEST)


def _rms(x, w):
    return x * lax.rsqrt(jnp.mean(x * x, axis=-1, keepdims=True) + EPS) * w


def _silu(x):
    return x * (1.0 / (1.0 + jnp.exp(-x)))


def _gelu(x):
    return 0.5 * x * (1.0 + lax.erf(x * 0.7071067811865476))


def _softplus(x):
    return jnp.maximum(x, 0.0) + jnp.log1p(jnp.exp(-jnp.abs(x)))


def _gated_group_norm(y, z, w):
    g = y * _silu(z)
    half = D_SSM // SSM_GROUPS
    parts = []
    for k in range(SSM_GROUPS):
        gk = g[:, k * half:(k + 1) * half]
        parts.append(gk * lax.rsqrt(jnp.mean(gk * gk, axis=-1, keepdims=True) + EPS))
    return jnp.concatenate(parts, axis=1) * w


ROUTER_GROUP_ROW = 0
ROUTER_EXPERT_ROW = SUBLANES


def _route_t(lt):
    r = lt.shape[1]
    row = lax.broadcasted_iota(jnp.int32, (SUBLANES, r), 0)
    rowf = row.astype(F32)
    big = float(SUBLANES)
    red = lambda f, a: f(a, axis=0, keepdims=True)
    gl = jnp.where(row < N_EXPERT_GROUPS, lt[ROUTER_GROUP_ROW:ROUTER_GROUP_ROW + SUBLANES], -jnp.inf)
    ge = jnp.exp(gl - red(jnp.max, gl))
    p_grp = ge / red(jnp.sum, ge)
    g_p = red(jnp.max, p_grp)
    g_idx = red(jnp.min, jnp.where(p_grp == g_p, rowf, big))
    el = lt[ROUTER_EXPERT_ROW:ROUTER_EXPERT_ROW + EXPERTS_PER_GROUP]
    for g in range(1, N_EXPERT_GROUPS):
        lo = ROUTER_EXPERT_ROW + g * EXPERTS_PER_GROUP
        el = jnp.where(g_idx == float(g), lt[lo:lo + EXPERTS_PER_GROUP], el)
    ee = jnp.exp(el - red(jnp.max, el))
    pe = ee / red(jnp.sum, ee)
    v1 = red(jnp.max, pe)
    i1 = red(jnp.min, jnp.where(pe == v1, rowf, big))
    rest = rowf != i1
    pe2 = jnp.where(rest, pe, -1.0)
    v2 = red(jnp.max, pe2)
    i2 = red(jnp.min, jnp.where(rest & (pe2 == v2), rowf, big))
    den = v1 + v2
    base = g_idx * float(EXPERTS_PER_GROUP)
    out = jnp.where(row == 0, base + i1, 0.0)
    out = jnp.where(row == 1, base + i2, out)
    out = jnp.where(row == 2, g_p * v1 / den, out)
    out = jnp.where(row == 3, g_p * v2 / den, out)
    return jnp.concatenate([out, jnp.zeros((LANES - SUBLANES, r), F32)], axis=0).T


def _ffn_front(h, nfw, wrt_ref, brc):
    hn2 = _rms(h, nfw)
    hi = hn2.astype(BF16)
    lo = (hn2 - hi.astype(F32)).astype(BF16)
    nt = lambda a, b: lax.dot_general(a, b, (((1,), (1,)), ((), ())), preferred_element_type=F32)
    lt = nt(wrt_ref[0], hi) + nt(wrt_ref[0], lo) + nt(wrt_ref[1], hi) + brc
    return hn2, _route_t(lt)


ROW_TILES = D_MODEL // LANES


def _store_row_tiles(ref, val):
    r = val.shape[0]
    for k in range(ROW_TILES):
        ref[pl.ds(k, r, stride=ROW_TILES), :] = val[:, k * LANES:(k + 1) * LANES]


def _load_row_tiles(ref):
    r = ref.shape[0] // ROW_TILES
    return jnp.concatenate([ref[pl.ds(k, r, stride=ROW_TILES), :] for k in range(ROW_TILES)], axis=1)


def _prompt_step(
        t, nt, fillers, x_ref, nmw_ref, wz_ref, wxbc_ref, wdt_ref, wuv_ref, convw_ref, convb_ref, dtb_ref,
        alog_ref, dskip_ref, ssmnw_ref, sgnw_ref, sgw_ref, sgb_ref, convt_ref, ssm_ref,
        xbc_scr, state_scr, y_scr, mix_scr, tl):
    fillers = list(fillers)
    assert len(fillers) == 3 + tl // CHUNK

    @pl.when(t == 0)
    def _():
        xbc_scr[0:SUBLANES, :] = jnp.zeros((SUBLANES, C_CONV), F32)
        state_scr[...] = jnp.zeros(state_scr.shape, F32)

    x = x_ref[...]
    hn = _rms(x, nmw_ref[...]).astype(BF16)
    z = _dot(hn, wz_ref[...])
    xbc = _dot(hn, wxbc_ref[...])
    dtr = _dot(hn, wdt_ref[...])
    uv = _gelu(_dot(hn, wuv_ref[...]))
    fillers.pop(0)()

    xbc_scr[SUBLANES:SUBLANES + tl, :] = xbc
    conv = convb_ref[...]
    for k in range(CONV_K):
        off = SUBLANES - (CONV_K - 1) + k
        conv = conv + convw_ref[k:k + 1, :] * xbc_scr[off:off + tl, :]
    tail = xbc_scr[tl:tl + SUBLANES, :]
    xbc_scr[0:SUBLANES, :] = tail
    convt_ref[...] = tail
    act = _silu(conv)
    xs = act[:, :D_SSM]
    bmat = act[:, D_SSM:D_SSM + SSM_GROUPS * SSM_STATE]
    cmat = act[:, D_SSM + SSM_GROUPS * SSM_STATE:]
    fillers.pop(0)()

    dt = _softplus(dtr + dtb_ref[...])
    a_row = -jnp.exp(alog_ref[...])

    li = lax.broadcasted_iota(jnp.int32, (CHUNK, CHUNK), 0)
    si = lax.broadcasted_iota(jnp.int32, (CHUNK, CHUNK), 1)
    causal = li >= si
    tri = jnp.where(causal, 1.0, 0.0).astype(F32)
    lo_half = si < SSM_HEAD_DIM

    for c in range(tl // CHUNK):
        rows = slice(c * CHUNK, (c + 1) * CHUNK)
        dt_c = dt[rows]
        acum = _dot_f32(tri, dt_c * a_row)
        acum_t = acum.T
        dt_t = dt_c.T
        w_t = dt_t * jnp.exp(acum_t[:, CHUNK - 1:CHUNK] - acum_t)
        xs_c = xs[rows]
        for g in range(SSM_GROUPS):
            gl = slice(g * SSM_STATE, (g + 1) * SSM_STATE)
            b_g = bmat[rows, gl]
            c_g = cmat[rows, gl]
            b_gt = b_g.T
            cb = _dot(c_g.astype(BF16), b_gt.astype(BF16))
            for j in range(HEADS_PER_GROUP // 2):
                pl_ = slice(g * 512 + j * LANES, g * 512 + (j + 1) * LANES)
                m_l, ec_l, s_l, ea_last = [], [], [], []
                for k in range(2):
                    hh = g * HEADS_PER_GROUP + 2 * j + k
                    colb = jnp.broadcast_to(acum[:, hh:hh + 1], (CHUNK, CHUNK))
                    rowb = jnp.broadcast_to(acum_t[hh:hh + 1, :], (CHUNK, CHUNK))
                    ea = jnp.exp(colb)
                    seg = jnp.where(causal, colb - rowb, 0.0)
                    lmat = jnp.where(causal, jnp.exp(seg), 0.0)
                    m_l.append(cb * lmat * jnp.broadcast_to(dt_t[hh:hh + 1, :], (CHUNK, CHUNK)))
                    ec_l.append(c_g * ea)
                    s_l.append(b_gt * jnp.broadcast_to(w_t[hh:hh + 1, :], (CHUNK, CHUNK)))
                    ea_last.append(ea[CHUNK - 1:CHUNK, :])
                xs_p = xs_c[:, pl_]
                st_p = state_scr[g, :, j * LANES:(j + 1) * LANES]
                rx = jnp.concatenate([jnp.where(lo_half, xs_p, 0.0),
                                      jnp.where(lo_half, 0.0, xs_p)], axis=0).astype(BF16)
                rs = jnp.concatenate([jnp.where(lo_half, st_p, 0.0),
                                      jnp.where(lo_half, 0.0, st_p)], axis=0).astype(BF16)
                lhs = jnp.concatenate(m_l + ec_l, axis=1).astype(BF16)
                y_p = _dot(lhs, jnp.concatenate([rx, rs], axis=0))
                y_scr[rows, pl_] = y_p
                dec = jnp.where(lo_half[0:1, :], ea_last[0], ea_last[1])
                upd = _dot(jnp.concatenate(s_l, axis=1).astype(BF16), rx)
                state_scr[g, :, j * LANES:(j + 1) * LANES] = st_p * dec + upd
        fillers.pop(0)()

    y = y_scr[...] + dskip_ref[...] * xs
    y_ssm = _gated_group_norm(y, z, ssmnw_ref[...])
    mix_scr[:, :D_SSM] = y_ssm.astype(BF16)

    u = uv[:, :D_SG]
    v = _rms(uv[:, D_SG:], sgnw_ref[...])
    vb = v.astype(BF16)
    for c in range(tl // CHUNK):
        rows = slice(c * CHUNK, (c + 1) * CHUNK)
        for hd in range(SG_HEADS):
            hl = slice(hd * SG_HEAD_DIM, (hd + 1) * SG_HEAD_DIM)
            sv = _dot(sgw_ref[hd], vb[rows, hl]) + sgb_ref[:, hl]
            mix_scr[rows, D_SSM + hd * SG_HEAD_DIM:D_SSM + (hd + 1) * SG_HEAD_DIM] = (
                u[rows, hl] * sv).astype(BF16)

    fillers.pop(0)()

    @pl.when(t == nt - 1)
    def _():
        for g in range(SSM_GROUPS):
            ssm_ref[g] = state_scr[g].T


def _prompt_mixer_kernel(
        x_ref, xp_ref, nmw_ref, wz_ref, wxbc_ref, wdt_ref, wuv_ref, convw_ref, convb_ref, dtb_ref,
        alog_ref, dskip_ref, ssmnw_ref, sgnw_ref, sgw_ref, sgb_ref, wout_ref, nfw_ref,
        wr_ref, br_ref, hs_ref, hn2s_ref, routes_ref,
        h_ref, hn2_ref, route_ref, convt_ref, ssm_ref,
        xbc_scr, state_scr, y_scr, mix_scr, h_scr, *, tl, nt, n_main):
    i = pl.program_id(0)

    @pl.when(i == 0)
    def _():
        mix_scr[...] = jnp.zeros(mix_scr.shape, BF16)

    @pl.when(i > n_main)
    def _():
        h_ref[...] = hs_ref[...]
        hn2_ref[...] = hn2s_ref[...]
        route_ref[...] = routes_ref[...]

    n_out_chunks = 2 + tl // CHUNK
    width = D_MODEL // n_out_chunks
    assert width % LANES == 0 and width * n_out_chunks == D_MODEL

    def out_chunk(c):
        cols = slice(c * width, (c + 1) * width)
        h_scr[:, cols] = xp_ref[:, cols] + _dot(mix_scr[...], wout_ref[:, cols])

    def finish():
        h = h_scr[...]
        h_ref[...] = h
        hn2, route = _ffn_front(h, nfw_ref[...], wr_ref, br_ref[...])
        _store_row_tiles(hn2_ref, hn2)
        route_ref[...] = route

    @pl.when(i <= n_main)
    def _():
        def out_chunks(cs):
            for c in cs:
                out_chunk(c)

        half = n_out_chunks // 2
        fillers = ([functools.partial(out_chunks, range(half)),
                    functools.partial(out_chunks, range(half, n_out_chunks)), finish]
                   + [lambda: None] * (tl // CHUNK))
        _prompt_step(
            lax.rem(i, nt), nt, fillers, x_ref, nmw_ref, wz_ref, wxbc_ref, wdt_ref, wuv_ref,
            convw_ref, convb_ref, dtb_ref, alog_ref, dskip_ref, ssmnw_ref, sgnw_ref, sgw_ref, sgb_ref,
            convt_ref, ssm_ref, xbc_scr, state_scr, y_scr, mix_scr, tl)


def _const_spec(shape):
    zeros = (0,) * len(shape)
    return pl.BlockSpec(shape, lambda *_: zeros)


def _prompt_mixer(x, w, tl, h_s, hn2_s, route_s):
    bsz, seq, _ = x.shape
    nt = seq // tl
    n_main = bsz * nt
    n_s = h_s.shape[0]
    assert seq % tl == 0 and n_s % tl == 0
    n_tok = bsz * seq + n_s
    x2 = x.reshape(bsz * seq, D_MODEL)
    main = lambda i: jnp.minimum(i, n_main - 1)
    prev = lambda i: jnp.clip(i - 1, 0, n_main - 1)
    tail = lambda i: jnp.maximum(i - n_main - 1, 0)
    out_row = lambda i: jnp.maximum(i - 1, 0)
    weights = [w['nmw'], w['wz'], w['wxbc'], w['wdt'], w['wuv'], w['convw'], w['convb'], w['dtb'],
               w['alog'], w['dskip'], w['ssmnw'], w['sgnw'], w['sgw'], w['sgb'], w['wout'], w['nfw'],
               w['wr'], w['br']]
    in_specs = ([pl.BlockSpec((tl, D_MODEL), lambda i: (main(i), 0)),
                 pl.BlockSpec((tl, D_MODEL), lambda i: (prev(i), 0))]
                + [_const_spec(a.shape) for a in weights]
                + [pl.BlockSpec(blk, lambda i: (tail(i), 0))
                   for blk in ((tl, D_MODEL), (tl * ROW_TILES, LANES), (tl, LANES))])
    q = HEADS_PER_GROUP * SSM_HEAD_DIM
    out_shape = [
        jax.ShapeDtypeStruct((n_tok, D_MODEL), F32),
        jax.ShapeDtypeStruct((n_tok * ROW_TILES, LANES), F32),
        jax.ShapeDtypeStruct((n_tok, LANES), F32),
        jax.ShapeDtypeStruct((bsz, SUBLANES, C_CONV), F32),
        jax.ShapeDtypeStruct((bsz, SSM_GROUPS, q, SSM_STATE), F32),
    ]
    out_specs = [
        pl.BlockSpec((tl, D_MODEL), lambda i: (out_row(i), 0)),
        pl.BlockSpec((tl * ROW_TILES, LANES), lambda i: (out_row(i), 0)),
        pl.BlockSpec((tl, LANES), lambda i: (out_row(i), 0)),
        pl.BlockSpec((None, SUBLANES, C_CONV), lambda i: (main(i) // nt, 0, 0)),
        pl.BlockSpec((None, SSM_GROUPS, q, SSM_STATE), lambda i: (main(i) // nt, 0, 0, 0)),
    ]
    scratch = [
        pltpu.VMEM((tl + SUBLANES, C_CONV), F32),
        pltpu.VMEM((SSM_GROUPS, SSM_STATE, q), F32),
        pltpu.VMEM((tl, D_SSM), F32),
        pltpu.VMEM((tl, D_SSM + D_SG), BF16),
        pltpu.VMEM((tl, D_MODEL), F32),
    ]
    return pl.pallas_call(
        functools.partial(_prompt_mixer_kernel, tl=tl, nt=nt, n_main=n_main),
        grid=(n_main + 1 + n_s // tl,),
        in_specs=in_specs,
        out_specs=out_specs,
        out_shape=out_shape,
        scratch_shapes=scratch,
        compiler_params=pltpu.CompilerParams(
            dimension_semantics=("arbitrary",), vmem_limit_bytes=VMEM_LIMIT),
        name="prompt_mixer",
    )(x2, x2, *weights, h_s, hn2_s, route_s)


def _sample_front_kernel(
        x_ref, cprev_ref, nmw_ref, wz_ref, wxbc_ref, wdt_ref, wuv_ref, convw_ref, convb_ref,
        dtb_ref, alog_ref, dskip_ref, sgnw_ref, w4_ref, b4_ref, expand_ref,
        z_ref, ypart_ref, ea_ref, ysg_ref, v_ref, convn_ref, c_ref, b_ref, xw_ref, dec_ref,
        *, nb, nt):
    x = x_ref[...]
    hn = _rms(x, nmw_ref[...]).astype(BF16)
    z_ref[...] = _dot(hn, wz_ref[...])
    xbc = _dot(hn, wxbc_ref[...])
    dtr = _dot(hn, wdt_ref[...])
    uv = _gelu(_dot(hn, wuv_ref[...]))

    slab = lambda a, t: a[t * nb:(t + 1) * nb]
    full = [cprev_ref[k] for k in range(CONV_K - 1)] + [slab(xbc, t) for t in range(nt)]
    for k in range(CONV_K - 1):
        convn_ref[k] = full[nt + k]
    xs, bm, cm = [], [], []
    for t in range(nt):
        conv = convb_ref[...]
        for k in range(CONV_K):
            conv = conv + convw_ref[k:k + 1, :] * full[t + k]
        act = _silu(conv)
        xs.append(act[:, :D_SSM])
        bm.append(act[:, D_SSM:D_SSM + SSM_GROUPS * SSM_STATE])
        cm.append(act[:, D_SSM + SSM_GROUPS * SSM_STATE:])
        b_ref[t * nb:(t + 1) * nb, :] = bm[t]
        c_ref[t * nb:(t + 1) * nb, :] = cm[t]

    dt = _softplus(dtr + dtb_ref[...])
    a_row = -jnp.exp(alog_ref[...])
    dts = [slab(dt, t) for t in range(nt)]
    cum = []
    for t in range(nt):
        da = dts[t] * a_row
        cum.append(da if t == 0 else cum[t - 1] + da)
    dec_ref[...] = jnp.exp(cum[nt - 1])

    lane = lax.broadcasted_iota(jnp.int32, (nb, LANES), 1)
    first_group = lane < HEADS_PER_GROUP
    facs = [jnp.exp(cum[t]) for t in range(nt)]
    facs += [dts[s] * jnp.exp(cum[nt - 1] - cum[s]) for s in range(nt)]
    pairs = []
    for t in range(nt):
        for s in range(t + 1):
            cb = []
            for g in range(SSM_GROUPS):
                gl = slice(g * SSM_STATE, (g + 1) * SSM_STATE)
                cb.append(jnp.sum(cm[t][:, gl] * bm[s][:, gl], axis=1, keepdims=True))
            cbh = jnp.where(first_group, cb[0], cb[1])
            facs.append(jnp.exp(cum[t] - cum[s]) * dts[s] * cbh)
            pairs.append((t, s))
    fx = _dot_f32(jnp.concatenate(facs, axis=0), expand_ref[...])
    fslab = lambda i: fx[i * nb:(i + 1) * nb]
    for t in range(nt):
        ea_ref[t * nb:(t + 1) * nb, :] = fslab(t)
        xw_ref[t * nb:(t + 1) * nb, :] = xs[t] * fslab(nt + t)
    for t in range(nt):
        acc = dskip_ref[...] * xs[t]
        for i, (tt, s) in enumerate(pairs):
            if tt == t:
                acc = acc + fslab(2 * nt + i) * xs[s]
        ypart_ref[t * nb:(t + 1) * nb, :] = acc

    u = uv[:, :D_SG]
    v = _rms(uv[:, D_SG:], sgnw_ref[...])
    v_ref[...] = v
    for t in range(nt):
        sv = b4_ref[t:t + 1, :]
        for s in range(t + 1):
            sv = sv + w4_ref[t, s:s + 1, :] * slab(v, s)
        ysg_ref[t * nb:(t + 1) * nb, :] = slab(u, t) * sv


def _sample_front(x_slab, cprev, w, nb, nt):
    r = nb * nt
    weights = [w['nmw'], w['wz'], w['wxbc'], w['wdt'], w['wuv'], w['convw'], w['convb'], w['dtb'],
               w['alog'], w['dskip'], w['sgnw'], w['w4'], w['b4'], w['expand']]
    ins = [x_slab, cprev] + weights
    out_shape = [
        jax.ShapeDtypeStruct((r, D_SSM), F32),
        jax.ShapeDtypeStruct((r, D_SSM), F32),
        jax.ShapeDtypeStruct((r, D_SSM), F32),
        jax.ShapeDtypeStruct((r, D_SG), F32),
        jax.ShapeDtypeStruct((r, D_SG), F32),
        jax.ShapeDtypeStruct((CONV_K - 1, nb, C_CONV), F32),
        jax.ShapeDtypeStruct((r, SSM_GROUPS * SSM_STATE), F32),
        jax.ShapeDtypeStruct((r, SSM_GROUPS * SSM_STATE), F32),
        jax.ShapeDtypeStruct((r, D_SSM), F32),
        jax.ShapeDtypeStruct((nb, LANES), F32),
    ]
    return pl.pallas_call(
        functools.partial(_sample_front_kernel, nb=nb, nt=nt),
        grid=(1,),
        in_specs=[_const_spec(a.shape) for a in ins],
        out_specs=[_const_spec(s.shape) for s in out_shape],
        out_shape=out_shape,
        compiler_params=pltpu.CompilerParams(
            dimension_semantics=("arbitrary",), vmem_limit_bytes=VMEM_LIMIT),
        name="sample_front",
    )(*ins)


def _sample_state_kernel(dec_ref, st_ref, cq_ref, bq_ref, xw_ref, so_ref, z_ref, *, seqs):
    for bb in range(seqs):
        for g in range(SSM_GROUPS):
            s0 = st_ref[bb, g]
            z_ref[bb, g] = lax.dot_general(
                cq_ref[bb, g].astype(BF16), s0.astype(BF16), (((1,), (1,)), ((), ())),
                preferred_element_type=F32)
            upd = lax.dot_general(
                xw_ref[bb, g].astype(BF16), bq_ref[bb, g].astype(BF16), (((0,), (0,)), ((), ())),
                preferred_element_type=F32)
            for hh in range(HEADS_PER_GROUP):
                hs = slice(hh * SSM_HEAD_DIM, (hh + 1) * SSM_HEAD_DIM)
                so_ref[bb, g, hs, :] = s0[hs] * dec_ref[bb, g * HEADS_PER_GROUP + hh] + upd[hs]


def _sample_state(dec, state, cq, bq, xwq, seqs):
    nb = state.shape[0]
    q = HEADS_PER_GROUP * SSM_HEAD_DIM
    blk = lambda *tail: pl.BlockSpec((seqs, SSM_GROUPS) + tail, lambda i: (i, 0, 0, 0))
    return pl.pallas_call(
        functools.partial(_sample_state_kernel, seqs=seqs),
        grid=(nb // seqs,),
        in_specs=[
            pl.BlockSpec((seqs, SSM_HEADS), lambda i: (i, 0), memory_space=pltpu.SMEM),
            blk(q, SSM_STATE), blk(SUBLANES, SSM_STATE), blk(SUBLANES, SSM_STATE), blk(SUBLANES, q),
        ],
        out_specs=[blk(q, SSM_STATE), blk(SUBLANES, q)],
        out_shape=[
            jax.ShapeDtypeStruct((nb, SSM_GROUPS, q, SSM_STATE), F32),
            jax.ShapeDtypeStruct((nb, SSM_GROUPS, SUBLANES, q), F32),
        ],
        compiler_params=pltpu.CompilerParams(
            dimension_semantics=("arbitrary",), vmem_limit_bytes=VMEM_LIMIT),
        name="sample_state",
    )(dec, state, cq, bq, xwq)


def _sample_back_kernel(x_ref, ypart_ref, ea_ref, zoff_ref, z_ref, ysg_ref, ssmnw_ref, wout_ref,
                        nfw_ref, wr_ref, br_ref, h_ref, hn2_ref, route_ref):
    y = ypart_ref[...] + ea_ref[...] * zoff_ref[...]
    y_ssm = _gated_group_norm(y, z_ref[...], ssmnw_ref[...])
    mix_in = jnp.concatenate([y_ssm.astype(BF16), ysg_ref[...].astype(BF16)], axis=1)
    h = x_ref[...] + _dot(mix_in, wout_ref[...])
    h_ref[...] = h
    hn2, route = _ffn_front(h, nfw_ref[...], wr_ref, br_ref[...])
    _store_row_tiles(hn2_ref, hn2)
    route_ref[...] = route


def _sample_back(x_slab, ypart, ea, zoff, z, ysg, w):
    r = x_slab.shape[0]
    ins = [x_slab, ypart, ea, zoff, z, ysg, w['ssmnw'], w['wout'], w['nfw'], w['wr'], w['br']]
    out_shape = [
        jax.ShapeDtypeStruct((r, D_MODEL), F32),
        jax.ShapeDtypeStruct((r * ROW_TILES, LANES), F32),
        jax.ShapeDtypeStruct((r, LANES), F32),
    ]
    return pl.pallas_call(
        _sample_back_kernel,
        grid=(1,),
        in_specs=[_const_spec(a.shape) for a in ins],
        out_specs=[_const_spec(s.shape) for s in out_shape],
        out_shape=out_shape,
        compiler_params=pltpu.CompilerParams(
            dimension_semantics=("arbitrary",), vmem_limit_bytes=VMEM_LIMIT),
        name="sample_back",
    )(*ins)


EXPERT_PIECES = 8
DMA_PIECES = 4


def _expert_mlp_kernel(be_ref, nu_ref, tok0_ref, tokn_ref, dstp_ref, x_hbm, wg_ref, wu_ref, wd_ref,
                       parts_hbm, xbuf, obuf, xb, hb, wg_b, wu_b, wd_b, gsem, ssem, *, bm):
    j = pl.program_id(0)
    nu = nu_ref[0]
    nt = ROW_TILES
    tile = lambda ref, start: ref.at[pl.ds(pl.multiple_of(start, nt), nt)]

    def gather_copy(idx_ref, s, r):
        return pltpu.make_async_copy(tile(x_hbm, idx_ref[0, r]), xbuf.at[s, pl.ds(r * nt, nt)], gsem.at[s])

    def scatter_copy(s, r):
        return pltpu.make_async_copy(obuf.at[s, pl.ds(r * nt, nt)], tile(parts_hbm, dstp_ref[0, r]),
                                     ssem.at[s])

    def gather_wait(s):
        pltpu.make_async_copy(x_hbm.at[pl.ds(0, bm * nt)], xbuf.at[s], gsem.at[s]).wait()

    def scatter_wait(s):
        pltpu.make_async_copy(obuf.at[s], parts_hbm.at[pl.ds(0, bm * nt)], ssem.at[s]).wait()

    def step(slot):
        gather_wait(slot)
        per = 2 * bm // DMA_PIECES

        def start_rows(piece):
            if piece >= DMA_PIECES:
                return
            half_pieces = DMA_PIECES // 2
            for r in range((piece % half_pieces) * per, (piece % half_pieces + 1) * per):
                if piece < half_pieces:
                    gather_copy(tokn_ref, 1 - slot, r).start(priority=r % 2)
                else:
                    scatter_copy(1 - slot, r).start(priority=r % 2)

        xb[...] = _load_row_tiles(xbuf.at[slot]).astype(BF16)
        half = D_FF // 2
        for c in range(2):
            cols = slice(c * half, (c + 1) * half)
            start_rows(2 * c)
            g = _dot(xb[...], wg_b[:, cols])
            start_rows(2 * c + 1)
            u = _dot(xb[...], wu_b[:, cols])
            hb[:, cols] = (_silu(g) * u).astype(BF16)
        quarter = D_MODEL // 4
        scatter_wait(slot)
        for c in range(4):
            start_rows(4 + c)
            o = _dot(hb[...], wd_b[:, c * quarter:(c + 1) * quarter])
            for k in range(quarter // LANES):
                kk = c * (quarter // LANES) + k
                obuf.at[slot][pl.ds(kk, bm, stride=nt), :] = o[:, k * LANES:(k + 1) * LANES]

    def drain(slot):
        gather_wait(slot)
        scatter_wait(slot)
        for r in range(bm):
            scatter_copy(1 - slot, r).start(priority=r % 2)
        scatter_wait(1 - slot)

    @pl.when(j == 0)
    def _():
        obuf[...] = jnp.zeros(obuf.shape, F32)
        n_real = parts_hbm.shape[0] - 2 * bm * nt
        pltpu.make_async_copy(obuf.at[0], parts_hbm.at[pl.ds(n_real, bm * nt)], ssem.at[0]).start()

        def start0(r, carry):
            pltpu.make_async_copy(tile(x_hbm, tok0_ref[0, r]), xbuf.at[0, pl.ds(pl.multiple_of(r * nt, nt), nt)],
                                  gsem.at[0]).start()
            return carry
        lax.fori_loop(0, bm, start0, 0)

    used = j < nu

    @pl.when(used & ((j == 0) | (be_ref[j] != be_ref[jnp.maximum(j - 1, 0)])))
    def _():
        wg_b[...] = wg_ref[...].astype(BF16)
        wu_b[...] = wu_ref[...].astype(BF16)
        wd_b[...] = wd_ref[...].astype(BF16)

    for slot in range(2):
        parity = lax.rem(j, 2) == slot

        @pl.when(used & parity)
        def _(slot=slot):
            step(slot)

        @pl.when((j == nu) & parity)
        def _(slot=slot):
            drain(slot)


def _expert_mlp(blk_expert, n_used, tok, dst, x, wg, wu, wd, bm, n_out):
    n_blocks = blk_expert.shape[0]
    assert bm % EXPERT_PIECES == 0
    w_map = lambda j, be, nu: (be[j], 0, 0)
    idx_spec = lambda f: pl.BlockSpec((None, 1, bm), lambda j, be, nu: (f(j), 0, 0),
                                      memory_space=pltpu.SMEM)
    any_spec = pl.BlockSpec(memory_space=pl.ANY)
    grid_spec = pltpu.PrefetchScalarGridSpec(
        num_scalar_prefetch=2,
        grid=(n_blocks + 1,),
        in_specs=[
            idx_spec(lambda j: 0),
            idx_spec(lambda j: jnp.minimum(j + 1, n_blocks - 1)),
            idx_spec(lambda j: j),
            any_spec,
            pl.BlockSpec((None, D_MODEL, D_FF), w_map),
            pl.BlockSpec((None, D_MODEL, D_FF), w_map),
            pl.BlockSpec((None, D_FF, D_MODEL), w_map),
        ],
        out_specs=any_spec,
        scratch_shapes=[
            pltpu.VMEM((2, bm * ROW_TILES, LANES), F32),
            pltpu.VMEM((2, bm * ROW_TILES, LANES), F32),
            pltpu.VMEM((bm, D_MODEL), BF16),
            pltpu.VMEM((bm, D_FF), BF16),
            pltpu.VMEM((D_MODEL, D_FF), BF16),
            pltpu.VMEM((D_MODEL, D_FF), BF16),
            pltpu.VMEM((D_FF, D_MODEL), BF16),
            pltpu.SemaphoreType.DMA((2,)),
            pltpu.SemaphoreType.DMA((2,)),
        ],
    )
    tok3 = (tok * ROW_TILES).reshape(n_blocks, 1, bm)
    spare = (n_out - bm + jnp.arange(bm, dtype=jnp.int32)).reshape(1, bm)
    dst_prev = (jnp.concatenate([spare, dst], axis=0) * ROW_TILES).reshape(n_blocks + 1, 1, bm)
    blk_expert = jnp.concatenate([blk_expert, blk_expert[-1:]])
    return pl.pallas_call(
        functools.partial(_expert_mlp_kernel, bm=bm),
        grid_spec=grid_spec,
        out_shape=jax.ShapeDtypeStruct((n_out * ROW_TILES, LANES), F32),
        compiler_params=pltpu.CompilerParams(
            dimension_semantics=("arbitrary",), vmem_limit_bytes=VMEM_LIMIT),
        name="expert_mlp",
    )(blk_expert, n_used, tok3, tok3, dst_prev, x, wg, wu, wd)


PAD_ID = 2 ** 16 - 1


def _moe_plan(e_idx, bm, n_blocks):
    n_tok = e_idx.shape[0]
    n_asg = 2 * n_tok
    n_pad = n_blocks * bm - n_asg
    e_flat = e_idx.reshape(-1)
    experts = jnp.arange(N_EXPERTS, dtype=jnp.int32)
    counts = jnp.sum((e_flat[:, None] == experts[None, :]).astype(jnp.int32), axis=0)
    nblk = (counts + bm - 1) // bm
    blk_end = jnp.cumsum(nblk)
    n_used = blk_end[-1]
    j = jnp.arange(n_blocks, dtype=jnp.int32)
    be = jnp.sum((blk_end[None, :] <= jnp.minimum(j, n_used - 1)[:, None]).astype(jnp.int32), axis=1)
    be = jnp.minimum(be, N_EXPERTS - 1)
    assert n_asg < PAD_ID
    pad_end = jnp.cumsum(nblk * bm - counts)
    pad_expert = jnp.sum((pad_end[None, :] <= jnp.arange(n_pad, dtype=jnp.int32)[:, None]).astype(jnp.int32),
                         axis=1)
    keys = jnp.concatenate([2 * e_flat, 2 * pad_expert + 1])
    ids = jnp.concatenate([jnp.arange(n_asg, dtype=jnp.int32), jnp.full((n_pad,), PAD_ID, jnp.int32)])
    window = lax.sort(keys * (PAD_ID + 1) + ids) % (PAD_ID + 1)
    window = window.reshape(n_blocks, bm)
    valid = window != PAD_ID
    q = jnp.arange(bm, dtype=jnp.int32)
    tok = jnp.where(valid, window // 2, (j[:, None] * bm + q[None, :]) % n_tok)
    dst = jnp.where(valid, (window % 2) * n_tok + window // 2,
                    2 * n_tok + q[None, :])
    return tok.astype(jnp.int32), dst.astype(jnp.int32), be, n_used.astype(jnp.int32).reshape(1)


def _final_kernel(h_ref, p0_ref, p1_ref, route_ref, w_ref, op_ref, os_ref, *, n_prompt_steps):
    route = route_ref[...]
    p0 = _load_row_tiles(p0_ref)
    p1 = _load_row_tiles(p1_ref)
    y = _rms(h_ref[...] + route[:, 2:3] * p0 + route[:, 3:4] * p1, w_ref[...])
    i = pl.program_id(0)

    @pl.when(i < n_prompt_steps)
    def _():
        op_ref[...] = y

    @pl.when(i >= n_prompt_steps)
    def _():
        os_ref[...] = y


def _final(h, parts, route, w, tf, n_prompt):
    n_tok = h.shape[0]
    assert n_tok % tf == 0 and n_prompt % tf == 0
    n_p = n_prompt // tf
    row = lambda d: pl.BlockSpec((tf, d), lambda i: (i, 0))
    part = lambda k: pl.BlockSpec((tf * ROW_TILES, LANES), lambda i: (k * (n_tok // tf) + i, 0))
    return pl.pallas_call(
        functools.partial(_final_kernel, n_prompt_steps=n_p),
        grid=(n_tok // tf,),
        in_specs=[row(D_MODEL), part(0), part(1), row(LANES), _const_spec(w.shape)],
        out_specs=[pl.BlockSpec((tf, D_MODEL), lambda i: (jnp.minimum(i, n_p - 1), 0)),
                   pl.BlockSpec((tf, D_MODEL), lambda i: (jnp.maximum(i - n_p, 0), 0))],
        out_shape=[jax.ShapeDtypeStruct((n_prompt, D_MODEL), F32),
                   jax.ShapeDtypeStruct((n_tok - n_prompt, D_MODEL), F32)],
        compiler_params=pltpu.CompilerParams(
            dimension_semantics=("arbitrary",), vmem_limit_bytes=VMEM_LIMIT),
        name="final_combine",
    )(h, parts, parts, route, w)


PROMPT_TILE = 256
MOE_BLOCK_ROWS = 256
FINAL_TILE = 512
STATE_SEQS = 8


def _pad_lanes(a, width=LANES):
    return jnp.pad(a, [(0, 0)] * (a.ndim - 1) + [(0, width - a.shape[-1])])


def _layer_weights(l, norm_mix_w, w_in, conv_w, conv_b, dt_bias, a_log, d_skip, ssm_norm_w, sg_norm_w,
                   sg_w, sg_b, w_out, norm_ffn_w, w_router_group, b_router_group, w_router_expert,
                   b_router_expert, n_dec):
    c0, c1, c2 = D_SSM, D_SSM + C_CONV, D_SSM + C_CONV + SSM_HEADS
    wi = w_in[l]
    causal = jnp.tril(jnp.ones((CHUNK, CHUNK), bool))
    sgw_l = jnp.where(causal, sg_w[l], 0.0)
    head_of_lane = jnp.arange(D_SSM, dtype=jnp.int32) // SSM_HEAD_DIM
    expand = (jnp.arange(LANES, dtype=jnp.int32)[:, None] == head_of_lane[None, :]).astype(F32)
    row = lambda a: a.reshape(1, -1).astype(F32)
    gap = ROUTER_EXPERT_ROW - N_EXPERT_GROUPS
    rest = LANES - ROUTER_EXPERT_ROW - N_EXPERTS
    wrt = jnp.concatenate([w_router_group[l].T, jnp.zeros((gap, D_MODEL), F32), w_router_expert[l].T,
                           jnp.zeros((rest, D_MODEL), F32)], axis=0).astype(F32)
    brc = jnp.concatenate([b_router_group[l], jnp.zeros((gap,), F32), b_router_expert[l],
                           jnp.zeros((rest,), F32)]).astype(F32).reshape(LANES, 1)
    return {
        'nmw': row(norm_mix_w[l]),
        'wz': wi[:, :c0].astype(BF16),
        'wxbc': wi[:, c0:c1].astype(BF16),
        'wdt': _pad_lanes(wi[:, c1:c2]).astype(BF16),
        'wuv': wi[:, c2:].astype(BF16),
        'convw': conv_w[l].astype(F32),
        'convb': row(conv_b[l]),
        'dtb': _pad_lanes(row(dt_bias[l])),
        'alog': _pad_lanes(row(a_log[l])),
        'dskip': row(jnp.repeat(d_skip[l], SSM_HEAD_DIM)),
        'ssmnw': row(ssm_norm_w[l]),
        'sgnw': row(sg_norm_w[l]),
        'sgw': sgw_l.astype(BF16),
        'sgb': jnp.repeat(sg_b[l].T, SG_HEAD_DIM, axis=1).astype(F32),
        'w4': jnp.repeat(jnp.transpose(sgw_l[:, :n_dec, :n_dec], (1, 2, 0)), SG_HEAD_DIM, axis=2).astype(F32),
        'b4': jnp.repeat(sg_b[l][:, :n_dec].T, SG_HEAD_DIM, axis=1).astype(F32),
        'expand': expand,
        'wout': w_out[l].astype(BF16),
        'nfw': row(norm_ffn_w[l]),
        'wr': jnp.stack([wrt.astype(BF16), (wrt - wrt.astype(BF16).astype(F32)).astype(BF16)]),
        'br': brc,
    }


def _layer(l, xp, xs_slab, state_conv, state_ssm, w, w_gate, w_up, w_down):
    bp, seq, _ = xp.shape
    nb, nt = state_conv.shape[1], xs_slab.shape[0] // state_conv.shape[1]
    q = HEADS_PER_GROUP * SSM_HEAD_DIM

    cprev = jnp.transpose(state_conv[l], (1, 0, 2))
    z, ypart, ea, ysg, v, convn, cmat, bmat, xw, dec = _sample_front(xs_slab, cprev, w, nb, nt)
    to_seq = lambda a, d: jnp.pad(
        jnp.transpose(a.reshape(nt, nb, SSM_GROUPS, d), (1, 2, 0, 3)),
        ((0, 0), (0, 0), (0, SUBLANES - nt), (0, 0)))
    ssm_s, zoff = _sample_state(
        dec[:, :SSM_HEADS], state_ssm[l].reshape(nb, SSM_GROUPS, q, SSM_STATE),
        to_seq(cmat, SSM_STATE), to_seq(bmat, SSM_STATE), to_seq(xw, q), STATE_SEQS)
    zoff = jnp.transpose(zoff[:, :, :nt], (2, 0, 1, 3)).reshape(nt * nb, D_SSM)
    h_s, hn2_s, route_s = _sample_back(xs_slab, ypart, ea, zoff, z, ysg, w)

    h, hn2, route, convt_p, ssm_p = _prompt_mixer(xp, w, PROMPT_TILE, h_s, hn2_s, route_s)

    n_tok = h.shape[0]
    bm = MOE_BLOCK_ROWS
    n_blocks = -(-(2 * n_tok) // bm) + N_EXPERTS
    tok, dst, blk_expert, n_used = _moe_plan(route[:, :2].astype(jnp.int32), bm, n_blocks)
    parts = _expert_mlp(blk_expert, n_used, tok, dst, hn2, w_gate[l], w_up[l], w_down[l], bm,
                        2 * n_tok + 2 * bm)
    outs = dict(
        h=h, parts=parts, route=route,
        conv_p=convt_p[:, SUBLANES - (CONV_K - 1):],
        ssm_p=ssm_p.reshape(bp, SSM_GROUPS, HEADS_PER_GROUP, SSM_HEAD_DIM, SSM_STATE),
        conv_s=jnp.transpose(convn, (1, 0, 2)),
        ssm_s=ssm_s.reshape(nb, SSM_GROUPS, HEADS_PER_GROUP, SSM_HEAD_DIM, SSM_STATE),
        v_s=jnp.transpose(v.reshape(nt, nb, D_SG), (1, 0, 2)),
    )
    return outs


def kernel(x_prompt, x_sample, state_conv, state_ssm, norm_mix_w, w_in, conv_w, conv_b, dt_bias, a_log, d_skip, ssm_norm_w, sg_norm_w, sg_w, sg_b, w_out, norm_ffn_w, w_router_group, b_router_group, w_router_expert, b_router_expert, w_gate, w_up, w_down, norm_final_w):
    depth = w_in.shape[0]
    assert depth == 1, "the fused final norm assumes a single layer"
    bp, seq, _ = x_prompt.shape
    nb, nt, _ = x_sample.shape
    l = 0
    w = _layer_weights(l, norm_mix_w, w_in, conv_w, conv_b, dt_bias, a_log, d_skip, ssm_norm_w, sg_norm_w,
                       sg_w, sg_b, w_out, norm_ffn_w, w_router_group, b_router_group, w_router_expert,
                       b_router_expert, nt)
    xs_slab = jnp.transpose(x_sample, (1, 0, 2)).reshape(nt * nb, D_MODEL)
    outs = _layer(l, x_prompt, xs_slab, state_conv, state_ssm, w, w_gate, w_up, w_down)
    y_p, y_s = _final(outs['h'], outs['parts'], outs['route'], norm_final_w.reshape(1, -1).astype(F32),
                      FINAL_TILE, bp * seq)
    y_prompt = y_p.reshape(bp, seq, D_MODEL)
    y_sample = jnp.transpose(y_s.reshape(nt, nb, D_MODEL), (1, 0, 2))
    return (y_prompt, y_sample, outs['conv_p'][None], outs['ssm_p'][None], outs['conv_s'][None],
            outs['ssm_s'][None], outs['v_s'][None])
```

```python
import functools

import jax
import jax.numpy as jnp
from jax import lax
from jax.experimental import pallas as pl
from jax.experimental.pallas import tpu as pltpu

D_MODEL = 1024
D_SSM = 1024
SSM_HEAD_DIM = 64
SSM_HEADS = 16
SSM_GROUPS = 2
HEADS_PER_GROUP = 8
SSM_STATE = 128
CONV_K = 4
C_CONV = D_SSM + 2 * SSM_GROUPS * SSM_STATE
D_SG = 1024
SG_HEADS = 8
SG_HEAD_DIM = 128
CHUNK = 128
N_EXPERT_GROUPS = 4
EXPERTS_PER_GROUP = 8
N_EXPERTS = 32
D_FF = 512
EPS = 1e-6

LANES = 128
SUBLANES = 8
VMEM_LIMIT = 56 * 1024 * 1024

F32 = jnp.float32
BF16 = jnp.bfloat16
HIGHEST = lax.Precision.HIGHEST


def _dot(a, b):
    return jnp.dot(a, b, preferred_element_type=F32)


def _dot_f32(a, b):
    return jnp.dot(a, b, preferred_element_type=F32, precision=HIGHEST)


def _rms(x, w):
    return x * lax.rsqrt(jnp.mean(x * x, axis=-1, keepdims=True) + EPS) * w


def _silu(x):
    return x * (1.0 / (1.0 + jnp.exp(-x)))


def _gelu(x):
    return 0.5 * x * (1.0 + lax.erf(x * 0.7071067811865476))


def _softplus(x):
    return jnp.maximum(x, 0.0) + jnp.log1p(jnp.exp(-jnp.abs(x)))


def _gated_group_norm(y, z, w):
    g = y * _silu(z)
    half = D_SSM // SSM_GROUPS
    parts = []
    for k in range(SSM_GROUPS):
        gk = g[:, k * half:(k + 1) * half]
        parts.append(gk * lax.rsqrt(jnp.mean(gk * gk, axis=-1, keepdims=True) + EPS))
    return jnp.concatenate(parts, axis=1) * w


ROUTER_GROUP_ROW = 0
ROUTER_EXPERT_ROW = SUBLANES


def _route_t(lt):
    r = lt.shape[1]
    row = lax.broadcasted_iota(jnp.int32, (SUBLANES, r), 0)
    rowf = row.astype(F32)
    big = float(SUBLANES)
    red = lambda f, a: f(a, axis=0, keepdims=True)
    gl = jnp.where(row < N_EXPERT_GROUPS, lt[ROUTER_GROUP_ROW:ROUTER_GROUP_ROW + SUBLANES], -jnp.inf)
    ge = jnp.exp(gl - red(jnp.max, gl))
    p_grp = ge / red(jnp.sum, ge)
    g_p = red(jnp.max, p_grp)
    g_idx = red(jnp.min, jnp.where(p_grp == g_p, rowf, big))
    el = lt[ROUTER_EXPERT_ROW:ROUTER_EXPERT_ROW + EXPERTS_PER_GROUP]
    for g in range(1, N_EXPERT_GROUPS):
        lo = ROUTER_EXPERT_ROW + g * EXPERTS_PER_GROUP
        el = jnp.where(g_idx == float(g), lt[lo:lo + EXPERTS_PER_GROUP], el)
    ee = jnp.exp(el - red(jnp.max, el))
    pe = ee / red(jnp.sum, ee)
    v1 = red(jnp.max, pe)
    i1 = red(jnp.min, jnp.where(pe == v1, rowf, big))
    rest = rowf != i1
    pe2 = jnp.where(rest, pe, -1.0)
    v2 = red(jnp.max, pe2)
    i2 = red(jnp.min, jnp.where(rest & (pe2 == v2), rowf, big))
    den = v1 + v2
    base = g_idx * float(EXPERTS_PER_GROUP)
    out = jnp.where(row == 0, base + i1, 0.0)
    out = jnp.where(row == 1, base + i2, out)
    out = jnp.where(row == 2, g_p * v1 / den, out)
    out = jnp.where(row == 3, g_p * v2 / den, out)
    return jnp.concatenate([out, jnp.zeros((LANES - SUBLANES, r), F32)], axis=0).T


def _ffn_front(h, nfw, wrt_ref, brc):
    hn2 = _rms(h, nfw)
    hi = hn2.astype(BF16)
    lo = (hn2 - hi.astype(F32)).astype(BF16)
    nt = lambda a, b: lax.dot_general(a, b, (((1,), (1,)), ((), ())), preferred_element_type=F32)
    lt = nt(wrt_ref[0], hi) + nt(wrt_ref[0], lo) + nt(wrt_ref[1], hi) + brc
    return hn2, _route_t(lt)


ROW_TILES = D_MODEL // LANES


def _store_row_tiles(ref, val):
    r = val.shape[0]
    for k in range(ROW_TILES):
        ref[pl.ds(k, r, stride=ROW_TILES), :] = val[:, k * LANES:(k + 1) * LANES]


def _load_row_tiles(ref):
    r = ref.shape[0] // ROW_TILES
    return jnp.concatenate([ref[pl.ds(k, r, stride=ROW_TILES), :] for k in range(ROW_TILES)], axis=1)


def _prompt_step(
        t, nt, fillers, x_ref, nmw_ref, wz_ref, wxbc_ref, wdt_ref, wuv_ref, convw_ref, convb_ref, dtb_ref,
        alog_ref, dskip_ref, ssmnw_ref, sgnw_ref, sgw_ref, sgb_ref, convt_ref, ssm_ref,
        xbc_scr, state_scr, y_scr, mix_scr, tl):
    fillers = list(fillers)
    assert len(fillers) == 3 + tl // CHUNK

    @pl.when(t == 0)
    def _():
        xbc_scr[0:SUBLANES, :] = jnp.zeros((SUBLANES, C_CONV), F32)
        state_scr[...] = jnp.zeros(state_scr.shape, F32)

    x = x_ref[...]
    hn = _rms(x, nmw_ref[...]).astype(BF16)
    z = _dot(hn, wz_ref[...])
    xbc = _dot(hn, wxbc_ref[...])
    dtr = _dot(hn, wdt_ref[...])
    uv = _gelu(_dot(hn, wuv_ref[...]))
    fillers.pop(0)()

    xbc_scr[SUBLANES:SUBLANES + tl, :] = xbc
    conv = convb_ref[...]
    for k in range(CONV_K):
        off = SUBLANES - (CONV_K - 1) + k
        conv = conv + convw_ref[k:k + 1, :] * xbc_scr[off:off + tl, :]
    tail = xbc_scr[tl:tl + SUBLANES, :]
    xbc_scr[0:SUBLANES, :] = tail
    convt_ref[...] = tail
    act = _silu(conv)
    xs = act[:, :D_SSM]
    bmat = act[:, D_SSM:D_SSM + SSM_GROUPS * SSM_STATE]
    cmat = act[:, D_SSM + SSM_GROUPS * SSM_STATE:]
    fillers.pop(0)()

    dt = _softplus(dtr + dtb_ref[...])
    a_row = -jnp.exp(alog_ref[...])

    li = lax.broadcasted_iota(jnp.int32, (CHUNK, CHUNK), 0)
    si = lax.broadcasted_iota(jnp.int32, (CHUNK, CHUNK), 1)
    causal = li >= si
    tri = jnp.where(causal, 1.0, 0.0).astype(F32)
    lo_half = si < SSM_HEAD_DIM

    for c in range(tl // CHUNK):
        rows = slice(c * CHUNK, (c + 1) * CHUNK)
        dt_c = dt[rows]
        acum = _dot_f32(tri, dt_c * a_row)
        acum_t = acum.T
        dt_t = dt_c.T
        w_t = dt_t * jnp.exp(acum_t[:, CHUNK - 1:CHUNK] - acum_t)
        xs_c = xs[rows]
        for g in range(SSM_GROUPS):
            gl = slice(g * SSM_STATE, (g + 1) * SSM_STATE)
            b_g = bmat[rows, gl]
            c_g = cmat[rows, gl]
            b_gt = b_g.T
            cb = _dot(c_g.astype(BF16), b_gt.astype(BF16))
            for j in range(HEADS_PER_GROUP // 2):
                pl_ = slice(g * 512 + j * LANES, g * 512 + (j + 1) * LANES)
                m_l, ec_l, s_l, ea_last = [], [], [], []
                for k in range(2):
                    hh = g * HEADS_PER_GROUP + 2 * j + k
                    colb = jnp.broadcast_to(acum[:, hh:hh + 1], (CHUNK, CHUNK))
                    rowb = jnp.broadcast_to(acum_t[hh:hh + 1, :], (CHUNK, CHUNK))
                    ea = jnp.exp(colb)
                    seg = jnp.where(causal, colb - rowb, 0.0)
                    lmat = jnp.where(causal, jnp.exp(seg), 0.0)
                    m_l.append(cb * lmat * jnp.broadcast_to(dt_t[hh:hh + 1, :], (CHUNK, CHUNK)))
                    ec_l.append(c_g * ea)
                    s_l.append(b_gt * jnp.broadcast_to(w_t[hh:hh + 1, :], (CHUNK, CHUNK)))
                    ea_last.append(ea[CHUNK - 1:CHUNK, :])
                xs_p = xs_c[:, pl_]
                st_p = state_scr[g, :, j * LANES:(j + 1) * LANES]
                rx = jnp.concatenate([jnp.where(lo_half, xs_p, 0.0),
                                      jnp.where(lo_half, 0.0, xs_p)], axis=0).astype(BF16)
                rs = jnp.concatenate([jnp.where(lo_half, st_p, 0.0),
                                      jnp.where(lo_half, 0.0, st_p)], axis=0).astype(BF16)
                lhs = jnp.concatenate(m_l + ec_l, axis=1).astype(BF16)
                y_p = _dot(lhs, jnp.concatenate([rx, rs], axis=0))
                y_scr[rows, pl_] = y_p
                dec = jnp.where(lo_half[0:1, :], ea_last[0], ea_last[1])
                upd = _dot(jnp.concatenate(s_l, axis=1).astype(BF16), rx)
                state_scr[g, :, j * LANES:(j + 1) * LANES] = st_p * dec + upd
        fillers.pop(0)()

    y = y_scr[...] + dskip_ref[...] * xs
    y_ssm = _gated_group_norm(y, z, ssmnw_ref[...])
    mix_scr[:, :D_SSM] = y_ssm.astype(BF16)

    u = uv[:, :D_SG]
    v = _rms(uv[:, D_SG:], sgnw_ref[...])
    vb = v.astype(BF16)
    for c in range(tl // CHUNK):
        rows = slice(c * CHUNK, (c + 1) * CHUNK)
        for hd in range(SG_HEADS):
            hl = slice(hd * SG_HEAD_DIM, (hd + 1) * SG_HEAD_DIM)
            sv = _dot(sgw_ref[hd], vb[rows, hl]) + sgb_ref[:, hl]
            mix_scr[rows, D_SSM + hd * SG_HEAD_DIM:D_SSM + (hd + 1) * SG_HEAD_DIM] = (
                u[rows, hl] * sv).astype(BF16)

    fillers.pop(0)()

    @pl.when(t == nt - 1)
    def _():
        for g in range(SSM_GROUPS):
            ssm_ref[g] = state_scr[g].T


def _prompt_mixer_kernel(
        x_ref, xp_ref, nmw_ref, wz_ref, wxbc_ref, wdt_ref, wuv_ref, convw_ref, convb_ref, dtb_ref,
        alog_ref, dskip_ref, ssmnw_ref, sgnw_ref, sgw_ref, sgb_ref, wout_ref, nfw_ref,
        wr_ref, br_ref, hs_ref, hn2s_ref, routes_ref,
        h_ref, hn2_ref, route_ref, convt_ref, ssm_ref,
        xbc_scr, state_scr, y_scr, mix_scr, h_scr, *, tl, nt, n_main):
    i = pl.program_id(0)

    @pl.when(i == 0)
    def _():
        mix_scr[...] = jnp.zeros(mix_scr.shape, BF16)

    @pl.when(i > n_main)
    def _():
        h_ref[...] = hs_ref[...]
        hn2_ref[...] = hn2s_ref[...]
        route_ref[...] = routes_ref[...]

    n_out_chunks = 2 + tl // CHUNK
    width = D_MODEL // n_out_chunks
    assert width % LANES == 0 and width * n_out_chunks == D_MODEL

    def out_chunk(c):
        cols = slice(c * width, (c + 1) * width)
        h_scr[:, cols] = xp_ref[:, cols] + _dot(mix_scr[...], wout_ref[:, cols])

    def finish():
        h = h_scr[...]
        h_ref[...] = h
        hn2, route = _ffn_front(h, nfw_ref[...], wr_ref, br_ref[...])
        _store_row_tiles(hn2_ref, hn2)
        route_ref[...] = route

    @pl.when(i <= n_main)
    def _():
        def out_chunks(cs):
            for c in cs:
                out_chunk(c)

        half = n_out_chunks // 2
        fillers = ([functools.partial(out_chunks, range(half)),
                    functools.partial(out_chunks, range(half, n_out_chunks)), finish]
                   + [lambda: None] * (tl // CHUNK))
        _prompt_step(
            lax.rem(i, nt), nt, fillers, x_ref, nmw_ref, wz_ref, wxbc_ref, wdt_ref, wuv_ref,
            convw_ref, convb_ref, dtb_ref, alog_ref, dskip_ref, ssmnw_ref, sgnw_ref, sgw_ref, sgb_ref,
            convt_ref, ssm_ref, xbc_scr, state_scr, y_scr, mix_scr, tl)


def _const_spec(shape):
    zeros = (0,) * len(shape)
    return pl.BlockSpec(shape, lambda *_: zeros)


def _prompt_mixer(x, w, tl, h_s, hn2_s, route_s):
    bsz, seq, _ = x.shape
    nt = seq // tl
    n_main = bsz * nt
    n_s = h_s.shape[0]
    assert seq % tl == 0 and n_s % tl == 0
    n_tok = bsz * seq + n_s
    x2 = x.reshape(bsz * seq, D_MODEL)
    main = lambda i: jnp.minimum(i, n_main - 1)
    prev = lambda i: jnp.clip(i - 1, 0, n_main - 1)
    tail = lambda i: jnp.maximum(i - n_main - 1, 0)
    out_row = lambda i: jnp.maximum(i - 1, 0)
    weights = [w['nmw'], w['wz'], w['wxbc'], w['wdt'], w['wuv'], w['convw'], w['convb'], w['dtb'],
               w['alog'], w['dskip'], w['ssmnw'], w['sgnw'], w['sgw'], w['sgb'], w['wout'], w['nfw'],
               w['wr'], w['br']]
    in_specs = ([pl.BlockSpec((tl, D_MODEL), lambda i: (main(i), 0)),
                 pl.BlockSpec((tl, D_MODEL), lambda i: (prev(i), 0))]
                + [_const_spec(a.shape) for a in weights]
                + [pl.BlockSpec(blk, lambda i: (tail(i), 0))
                   for blk in ((tl, D_MODEL), (tl * ROW_TILES, LANES), (tl, LANES))])
    q = HEADS_PER_GROUP * SSM_HEAD_DIM
    out_shape = [
        jax.ShapeDtypeStruct((n_tok, D_MODEL), F32),
        jax.ShapeDtypeStruct((n_tok * ROW_TILES, LANES), F32),
        jax.ShapeDtypeStruct((n_tok, LANES), F32),
        jax.ShapeDtypeStruct((bsz, SUBLANES, C_CONV), F32),
        jax.ShapeDtypeStruct((bsz, SSM_GROUPS, q, SSM_STATE), F32),
    ]
    out_specs = [
        pl.BlockSpec((tl, D_MODEL), lambda i: (out_row(i), 0)),
        pl.BlockSpec((tl * ROW_TILES, LANES), lambda i: (out_row(i), 0)),
        pl.BlockSpec((tl, LANES), lambda i: (out_row(i), 0)),
        pl.BlockSpec((None, SUBLANES, C_CONV), lambda i: (main(i) // nt, 0, 0)),
        pl.BlockSpec((None, SSM_GROUPS, q, SSM_STATE), lambda i: (main(i) // nt, 0, 0, 0)),
    ]
    scratch = [
        pltpu.VMEM((tl + SUBLANES, C_CONV), F32),
        pltpu.VMEM((SSM_GROUPS, SSM_STATE, q), F32),
        pltpu.VMEM((tl, D_SSM), F32),
        pltpu.VMEM((tl, D_SSM + D_SG), BF16),
        pltpu.VMEM((tl, D_MODEL), F32),
    ]
    return pl.pallas_call(
        functools.partial(_prompt_mixer_kernel, tl=tl, nt=nt, n_main=n_main),
        grid=(n_main + 1 + n_s // tl,),
        in_specs=in_specs,
        out_specs=out_specs,
        out_shape=out_shape,
        scratch_shapes=scratch,
        compiler_params=pltpu.CompilerParams(
            dimension_semantics=("arbitrary",), vmem_limit_bytes=VMEM_LIMIT),
        name="prompt_mixer",
    )(x2, x2, *weights, h_s, hn2_s, route_s)


def _sample_front_kernel(
        x_ref, cprev_ref, nmw_ref, wz_ref, wxbc_ref, wdt_ref, wuv_ref, convw_ref, convb_ref,
        dtb_ref, alog_ref, dskip_ref, sgnw_ref, w4_ref, b4_ref, expand_ref,
        z_ref, ypart_ref, ea_ref, ysg_ref, v_ref, convn_ref, c_ref, b_ref, xw_ref, dec_ref,
        *, nb, nt):
    x = x_ref[...]
    hn = _rms(x, nmw_ref[...]).astype(BF16)
    z_ref[...] = _dot(hn, wz_ref[...])
    xbc = _dot(hn, wxbc_ref[...])
    dtr = _dot(hn, wdt_ref[...])
    uv = _gelu(_dot(hn, wuv_ref[...]))

    slab = lambda a, t: a[t * nb:(t + 1) * nb]
    full = [cprev_ref[k] for k in range(CONV_K - 1)] + [slab(xbc, t) for t in range(nt)]
    for k in range(CONV_K - 1):
        convn_ref[k] = full[nt + k]
    xs, bm, cm = [], [], []
    for t in range(nt):
        conv = convb_ref[...]
        for k in range(CONV_K):
            conv = conv + convw_ref[k:k + 1, :] * full[t + k]
        act = _silu(conv)
        xs.append(act[:, :D_SSM])
        bm.append(act[:, D_SSM:D_SSM + SSM_GROUPS * SSM_STATE])
        cm.append(act[:, D_SSM + SSM_GROUPS * SSM_STATE:])
        b_ref[t * nb:(t + 1) * nb, :] = bm[t]
        c_ref[t * nb:(t + 1) * nb, :] = cm[t]

    dt = _softplus(dtr + dtb_ref[...])
    a_row = -jnp.exp(alog_ref[...])
    dts = [slab(dt, t) for t in range(nt)]
    cum = []
    for t in range(nt):
        da = dts[t] * a_row
        cum.append(da if t == 0 else cum[t - 1] + da)
    dec_ref[...] = jnp.exp(cum[nt - 1])

    lane = lax.broadcasted_iota(jnp.int32, (nb, LANES), 1)
    first_group = lane < HEADS_PER_GROUP
    facs = [jnp.exp(cum[t]) for t in range(nt)]
    facs += [dts[s] * jnp.exp(cum[nt - 1] - cum[s]) for s in range(nt)]
    pairs = []
    for t in range(nt):
        for s in range(t + 1):
            cb = []
            for g in range(SSM_GROUPS):
                gl = slice(g * SSM_STATE, (g + 1) * SSM_STATE)
                cb.append(jnp.sum(cm[t][:, gl] * bm[s][:, gl], axis=1, keepdims=True))
            cbh = jnp.where(first_group, cb[0], cb[1])
            facs.append(jnp.exp(cum[t] - cum[s]) * dts[s] * cbh)
            pairs.append((t, s))
    fx = _dot_f32(jnp.concatenate(facs, axis=0), expand_ref[...])
    fslab = lambda i: fx[i * nb:(i + 1) * nb]
    for t in range(nt):
        ea_ref[t * nb:(t + 1) * nb, :] = fslab(t)
        xw_ref[t * nb:(t + 1) * nb, :] = xs[t] * fslab(nt + t)
    for t in range(nt):
        acc = dskip_ref[...] * xs[t]
        for i, (tt, s) in enumerate(pairs):
            if tt == t:
                acc = acc + fslab(2 * nt + i) * xs[s]
        ypart_ref[t * nb:(t + 1) * nb, :] = acc

    u = uv[:, :D_SG]
    v = _rms(uv[:, D_SG:], sgnw_ref[...])
    v_ref[...] = v
    for t in range(nt):
        sv = b4_ref[t:t + 1, :]
        for s in range(t + 1):
            sv = sv + w4_ref[t, s:s + 1, :] * slab(v, s)
        ysg_ref[t * nb:(t + 1) * nb, :] = slab(u, t) * sv


def _sample_front(x_slab, cprev, w, nb, nt):
    r = nb * nt
    weights = [w['nmw'], w['wz'], w['wxbc'], w['wdt'], w['wuv'], w['convw'], w['convb'], w['dtb'],
               w['alog'], w['dskip'], w['sgnw'], w['w4'], w['b4'], w['expand']]
    ins = [x_slab, cprev] + weights
    out_shape = [
        jax.ShapeDtypeStruct((r, D_SSM), F32),
        jax.ShapeDtypeStruct((r, D_SSM), F32),
        jax.ShapeDtypeStruct((r, D_SSM), F32),
        jax.ShapeDtypeStruct((r, D_SG), F32),
        jax.ShapeDtypeStruct((r, D_SG), F32),
        jax.ShapeDtypeStruct((CONV_K - 1, nb, C_CONV), F32),
        jax.ShapeDtypeStruct((r, SSM_GROUPS * SSM_STATE), F32),
        jax.ShapeDtypeStruct((r, SSM_GROUPS * SSM_STATE), F32),
        jax.ShapeDtypeStruct((r, D_SSM), F32),
        jax.ShapeDtypeStruct((nb, LANES), F32),
    ]
    return pl.pallas_call(
        functools.partial(_sample_front_kernel, nb=nb, nt=nt),
        grid=(1,),
        in_specs=[_const_spec(a.shape) for a in ins],
        out_specs=[_const_spec(s.shape) for s in out_shape],
        out_shape=out_shape,
        compiler_params=pltpu.CompilerParams(
            dimension_semantics=("arbitrary",), vmem_limit_bytes=VMEM_LIMIT),
        name="sample_front",
    )(*ins)


def _sample_state_kernel(dec_ref, st_ref, cq_ref, bq_ref, xw_ref, so_ref, z_ref, *, seqs):
    for bb in range(seqs):
        for g in range(SSM_GROUPS):
            s0 = st_ref[bb, g]
            z_ref[bb, g] = lax.dot_general(
                cq_ref[bb, g].astype(BF16), s0.astype(BF16), (((1,), (1,)), ((), ())),
                preferred_element_type=F32)
            upd = lax.dot_general(
                xw_ref[bb, g].astype(BF16), bq_ref[bb, g].astype(BF16), (((0,), (0,)), ((), ())),
                preferred_element_type=F32)
            for hh in range(HEADS_PER_GROUP):
                hs = slice(hh * SSM_HEAD_DIM, (hh + 1) * SSM_HEAD_DIM)
                so_ref[bb, g, hs, :] = s0[hs] * dec_ref[bb, g * HEADS_PER_GROUP + hh] + upd[hs]


def _sample_state(dec, state, cq, bq, xwq, seqs):
    nb = state.shape[0]
    q = HEADS_PER_GROUP * SSM_HEAD_DIM
    blk = lambda *tail: pl.BlockSpec((seqs, SSM_GROUPS) + tail, lambda i: (i, 0, 0, 0))
    return pl.pallas_call(
        functools.partial(_sample_state_kernel, seqs=seqs),
        grid=(nb // seqs,),
        in_specs=[
            pl.BlockSpec((seqs, SSM_HEADS), lambda i: (i, 0), memory_space=pltpu.SMEM),
            blk(q, SSM_STATE), blk(SUBLANES, SSM_STATE), blk(SUBLANES, SSM_STATE), blk(SUBLANES, q),
        ],
        out_specs=[blk(q, SSM_STATE), blk(SUBLANES, q)],
        out_shape=[
            jax.ShapeDtypeStruct((nb, SSM_GROUPS, q, SSM_STATE), F32),
            jax.ShapeDtypeStruct((nb, SSM_GROUPS, SUBLANES, q), F32),
        ],
        compiler_params=pltpu.CompilerParams(
            dimension_semantics=("arbitrary",), vmem_limit_bytes=VMEM_LIMIT),
        name="sample_state",
    )(dec, state, cq, bq, xwq)


def _sample_back_kernel(x_ref, ypart_ref, ea_ref, zoff_ref, z_ref, ysg_ref, ssmnw_ref, wout_ref,
                        nfw_ref, wr_ref, br_ref, h_ref, hn2_ref, route_ref):
    y = ypart_ref[...] + ea_ref[...] * zoff_ref[...]
    y_ssm = _gated_group_norm(y, z_ref[...], ssmnw_ref[...])
    mix_in = jnp.concatenate([y_ssm.astype(BF16), ysg_ref[...].astype(BF16)], axis=1)
    h = x_ref[...] + _dot(mix_in, wout_ref[...])
    h_ref[...] = h
    hn2, route = _ffn_front(h, nfw_ref[...], wr_ref, br_ref[...])
    _store_row_tiles(hn2_ref, hn2)
    route_ref[...] = route


def _sample_back(x_slab, ypart, ea, zoff, z, ysg, w):
    r = x_slab.shape[0]
    ins = [x_slab, ypart, ea, zoff, z, ysg, w['ssmnw'], w['wout'], w['nfw'], w['wr'], w['br']]
    out_shape = [
        jax.ShapeDtypeStruct((r, D_MODEL), F32),
        jax.ShapeDtypeStruct((r * ROW_TILES, LANES), F32),
        jax.ShapeDtypeStruct((r, LANES), F32),
    ]
    return pl.pallas_call(
        _sample_back_kernel,
        grid=(1,),
        in_specs=[_const_spec(a.shape) for a in ins],
        out_specs=[_const_spec(s.shape) for s in out_shape],
        out_shape=out_shape,
        compiler_params=pltpu.CompilerParams(
            dimension_semantics=("arbitrary",), vmem_limit_bytes=VMEM_LIMIT),
        name="sample_back",
    )(*ins)


EXPERT_PIECES = 8
DMA_PIECES = 4


def _expert_mlp_kernel(be_ref, nu_ref, tok0_ref, tokn_ref, dstp_ref, x_hbm, wg_ref, wu_ref, wd_ref,
                       parts_hbm, xbuf, obuf, xb, hb, wg_b, wu_b, wd_b, gsem, ssem, *, bm):
    j = pl.program_id(0)
    nu = nu_ref[0]
    nt = ROW_TILES
    tile = lambda ref, start: ref.at[pl.ds(pl.multiple_of(start, nt), nt)]

    def gather_copy(idx_ref, s, r):
        return pltpu.make_async_copy(tile(x_hbm, idx_ref[0, r]), xbuf.at[s, pl.ds(r * nt, nt)], gsem.at[s])

    def scatter_copy(s, r):
        return pltpu.make_async_copy(obuf.at[s, pl.ds(r * nt, nt)], tile(parts_hbm, dstp_ref[0, r]),
                                     ssem.at[s])

    def gather_wait(s):
        pltpu.make_async_copy(x_hbm.at[pl.ds(0, bm * nt)], xbuf.at[s], gsem.at[s]).wait()

    def scatter_wait(s):
        pltpu.make_async_copy(obuf.at[s], parts_hbm.at[pl.ds(0, bm * nt)], ssem.at[s]).wait()

    def step(slot):
        gather_wait(slot)
        per = 2 * bm // DMA_PIECES

        def start_rows(piece):
            if piece >= DMA_PIECES:
                return
            half_pieces = DMA_PIECES // 2
            for r in range((piece % half_pieces) * per, (piece % half_pieces + 1) * per):
                if piece < half_pieces:
                    gather_copy(tokn_ref, 1 - slot, r).start(priority=r % 2)
                else:
                    scatter_copy(1 - slot, r).start(priority=r % 2)

        xb[...] = _load_row_tiles(xbuf.at[slot]).astype(BF16)
        half = D_FF // 2
        for c in range(2):
            cols = slice(c * half, (c + 1) * half)
            start_rows(2 * c)
            g = _dot(xb[...], wg_b[:, cols])
            start_rows(2 * c + 1)
            u = _dot(xb[...], wu_b[:, cols])
            hb[:, cols] = (_silu(g) * u).astype(BF16)
        quarter = D_MODEL // 4
        scatter_wait(slot)
        for c in range(4):
            start_rows(4 + c)
            o = _dot(hb[...], wd_b[:, c * quarter:(c + 1) * quarter])
            for k in range(quarter // LANES):
                kk = c * (quarter // LANES) + k
                obuf.at[slot][pl.ds(kk, bm, stride=nt), :] = o[:, k * LANES:(k + 1) * LANES]

    def drain(slot):
        gather_wait(slot)
        scatter_wait(slot)
        for r in range(bm):
            scatter_copy(1 - slot, r).start(priority=r % 2)
        scatter_wait(1 - slot)

    @pl.when(j == 0)
    def _():
        obuf[...] = jnp.zeros(obuf.shape, F32)
        n_real = parts_hbm.shape[0] - 2 * bm * nt
        pltpu.make_async_copy(obuf.at[0], parts_hbm.at[pl.ds(n_real, bm * nt)], ssem.at[0]).start()

        def start0(r, carry):
            pltpu.make_async_copy(tile(x_hbm, tok0_ref[0, r]), xbuf.at[0, pl.ds(pl.multiple_of(r * nt, nt), nt)],
                                  gsem.at[0]).start()
            return carry
        lax.fori_loop(0, bm, start0, 0)

    used = j < nu

    @pl.when(used & ((j == 0) | (be_ref[j] != be_ref[jnp.maximum(j - 1, 0)])))
    def _():
        wg_b[...] = wg_ref[...].astype(BF16)
        wu_b[...] = wu_ref[...].astype(BF16)
        wd_b[...] = wd_ref[...].astype(BF16)

    for slot in range(2):
        parity = lax.rem(j, 2) == slot

        @pl.when(used & parity)
        def _(slot=slot):
            step(slot)

        @pl.when((j == nu) & parity)
        def _(slot=slot):
            drain(slot)


def _expert_mlp(blk_expert, n_used, tok, dst, x, wg, wu, wd, bm, n_out):
    n_blocks = blk_expert.shape[0]
    assert bm % EXPERT_PIECES == 0
    w_map = lambda j, be, nu: (be[j], 0, 0)
    idx_spec = lambda f: pl.BlockSpec((None, 1, bm), lambda j, be, nu: (f(j), 0, 0),
                                      memory_space=pltpu.SMEM)
    any_spec = pl.BlockSpec(memory_space=pl.ANY)
    grid_spec = pltpu.PrefetchScalarGridSpec(
        num_scalar_prefetch=2,
        grid=(n_blocks + 1,),
        in_specs=[
            idx_spec(lambda j: 0),
            idx_spec(lambda j: jnp.minimum(j + 1, n_blocks - 1)),
            idx_spec(lambda j: j),
            any_spec,
            pl.BlockSpec((None, D_MODEL, D_FF), w_map),
            pl.BlockSpec((None, D_MODEL, D_FF), w_map),
            pl.BlockSpec((None, D_FF, D_MODEL), w_map),
        ],
        out_specs=any_spec,
        scratch_shapes=[
            pltpu.VMEM((2, bm * ROW_TILES, LANES), F32),
            pltpu.VMEM((2, bm * ROW_TILES, LANES), F32),
            pltpu.VMEM((bm, D_MODEL), BF16),
            pltpu.VMEM((bm, D_FF), BF16),
            pltpu.VMEM((D_MODEL, D_FF), BF16),
            pltpu.VMEM((D_MODEL, D_FF), BF16),
            pltpu.VMEM((D_FF, D_MODEL), BF16),
            pltpu.SemaphoreType.DMA((2,)),
            pltpu.SemaphoreType.DMA((2,)),
        ],
    )
    tok3 = (tok * ROW_TILES).reshape(n_blocks, 1, bm)
    spare = (n_out - bm + jnp.arange(bm, dtype=jnp.int32)).reshape(1, bm)
    dst_prev = (jnp.concatenate([spare, dst], axis=0) * ROW_TILES).reshape(n_blocks + 1, 1, bm)
    blk_expert = jnp.concatenate([blk_expert, blk_expert[-1:]])
    return pl.pallas_call(
        functools.partial(_expert_mlp_kernel, bm=bm),
        grid_spec=grid_spec,
        out_shape=jax.ShapeDtypeStruct((n_out * ROW_TILES, LANES), F32),
        compiler_params=pltpu.CompilerParams(
            dimension_semantics=("arbitrary",), vmem_limit_bytes=VMEM_LIMIT),
        name="expert_mlp",
    )(blk_expert, n_used, tok3, tok3, dst_prev, x, wg, wu, wd)


PAD_ID = 2 ** 16 - 1


def _moe_plan(e_idx, bm, n_blocks):
    n_tok = e_idx.shape[0]
    n_asg = 2 * n_tok
    n_pad = n_blocks * bm - n_asg
    e_flat = e_idx.reshape(-1)
    experts = jnp.arange(N_EXPERTS, dtype=jnp.int32)
    counts = jnp.sum((e_flat[:, None] == experts[None, :]).astype(jnp.int32), axis=0)
    nblk = (counts + bm - 1) // bm
    blk_end = jnp.cumsum(nblk)
    n_used = blk_end[-1]
    j = jnp.arange(n_blocks, dtype=jnp.int32)
    be = jnp.sum((blk_end[None, :] <= jnp.minimum(j, n_used - 1)[:, None]).astype(jnp.int32), axis=1)
    be = jnp.minimum(be, N_EXPERTS - 1)
    assert n_asg < PAD_ID
    pad_end = jnp.cumsum(nblk * bm - counts)
    pad_expert = jnp.sum((pad_end[None, :] <= jnp.arange(n_pad, dtype=jnp.int32)[:, None]).astype(jnp.int32),
                         axis=1)
    keys = jnp.concatenate([2 * e_flat, 2 * pad_expert + 1])
    ids = jnp.concatenate([jnp.arange(n_asg, dtype=jnp.int32), jnp.full((n_pad,), PAD_ID, jnp.int32)])
    window = lax.sort(keys * (PAD_ID + 1) + ids) % (PAD_ID + 1)
    window = window.reshape(n_blocks, bm)
    valid = window != PAD_ID
    q = jnp.arange(bm, dtype=jnp.int32)
    tok = jnp.where(valid, window // 2, (j[:, None] * bm + q[None, :]) % n_tok)
    dst = jnp.where(valid, (window % 2) * n_tok + window // 2,
                    2 * n_tok + q[None, :])
    return tok.astype(jnp.int32), dst.astype(jnp.int32), be, n_used.astype(jnp.int32).reshape(1)


def _final_kernel(h_ref, p0_ref, p1_ref, route_ref, w_ref, op_ref, os_ref, *, n_prompt_steps):
    route = route_ref[...]
    p0 = _load_row_tiles(p0_ref)
    p1 = _load_row_tiles(p1_ref)
    y = _rms(h_ref[...] + route[:, 2:3] * p0 + route[:, 3:4] * p1, w_ref[...])
    i = pl.program_id(0)

    @pl.when(i < n_prompt_steps)
    def _():
        op_ref[...] = y

    @pl.when(i >= n_prompt_steps)
    def _():
        os_ref[...] = y


def _final(h, parts, route, w, tf, n_prompt):
    n_tok = h.shape[0]
    assert n_tok % tf == 0 and n_prompt % tf == 0
    n_p = n_prompt // tf
    row = lambda d: pl.BlockSpec((tf, d), lambda i: (i, 0))
    part = lambda k: pl.BlockSpec((tf * ROW_TILES, LANES), lambda i: (k * (n_tok // tf) + i, 0))
    return pl.pallas_call(
        functools.partial(_final_kernel, n_prompt_steps=n_p),
        grid=(n_tok // tf,),
        in_specs=[row(D_MODEL), part(0), part(1), row(LANES), _const_spec(w.shape)],
        out_specs=[pl.BlockSpec((tf, D_MODEL), lambda i: (jnp.minimum(i, n_p - 1), 0)),
                   pl.BlockSpec((tf, D_MODEL), lambda i: (jnp.maximum(i - n_p, 0), 0))],
        out_shape=[jax.ShapeDtypeStruct((n_prompt, D_MODEL), F32),
                   jax.ShapeDtypeStruct((n_tok - n_prompt, D_MODEL), F32)],
        compiler_params=pltpu.CompilerParams(
            dimension_semantics=("arbitrary",), vmem_limit_bytes=VMEM_LIMIT),
        name="final_combine",
    )(h, parts, parts, route, w)


PROMPT_TILE = 256
MOE_BLOCK_ROWS = 512
FINAL_TILE = 512
STATE_SEQS = 8


def _pad_lanes(a, width=LANES):
    return jnp.pad(a, [(0, 0)] * (a.ndim - 1) + [(0, width - a.shape[-1])])


def _layer_weights(l, norm_mix_w, w_in, conv_w, conv_b, dt_bias, a_log, d_skip, ssm_norm_w, sg_norm_w,
                   sg_w, sg_b, w_out, norm_ffn_w, w_router_group, b_router_group, w_router_expert,
                   b_router_expert, n_dec):
    c0, c1, c2 = D_SSM, D_SSM + C_CONV, D_SSM + C_CONV + SSM_HEADS
    wi = w_in[l]
    causal = jnp.tril(jnp.ones((CHUNK, CHUNK), bool))
    sgw_l = jnp.where(causal, sg_w[l], 0.0)
    head_of_lane = jnp.arange(D_SSM, dtype=jnp.int32) // SSM_HEAD_DIM
    expand = (jnp.arange(LANES, dtype=jnp.int32)[:, None] == head_of_lane[None, :]).astype(F32)
    row = lambda a: a.reshape(1, -1).astype(F32)
    gap = ROUTER_EXPERT_ROW - N_EXPERT_GROUPS
    rest = LANES - ROUTER_EXPERT_ROW - N_EXPERTS
    wrt = jnp.concatenate([w_router_group[l].T, jnp.zeros((gap, D_MODEL), F32), w_router_expert[l].T,
                           jnp.zeros((rest, D_MODEL), F32)], axis=0).astype(F32)
    brc = jnp.concatenate([b_router_group[l], jnp.zeros((gap,), F32), b_router_expert[l],
                           jnp.zeros((rest,), F32)]).astype(F32).reshape(LANES, 1)
    return {
        'nmw': row(norm_mix_w[l]),
        'wz': wi[:, :c0].astype(BF16),
        'wxbc': wi[:, c0:c1].astype(BF16),
        'wdt': _pad_lanes(wi[:, c1:c2]).astype(BF16),
        'wuv': wi[:, c2:].astype(BF16),
        'convw': conv_w[l].astype(F32),
        'convb': row(conv_b[l]),
        'dtb': _pad_lanes(row(dt_bias[l])),
        'alog': _pad_lanes(row(a_log[l])),
        'dskip': row(jnp.repeat(d_skip[l], SSM_HEAD_DIM)),
        'ssmnw': row(ssm_norm_w[l]),
        'sgnw': row(sg_norm_w[l]),
        'sgw': sgw_l.astype(BF16),
        'sgb': jnp.repeat(sg_b[l].T, SG_HEAD_DIM, axis=1).astype(F32),
        'w4': jnp.repeat(jnp.transpose(sgw_l[:, :n_dec, :n_dec], (1, 2, 0)), SG_HEAD_DIM, axis=2).astype(F32),
        'b4': jnp.repeat(sg_b[l][:, :n_dec].T, SG_HEAD_DIM, axis=1).astype(F32),
        'expand': expand,
        'wout': w_out[l].astype(BF16),
        'nfw': row(norm_ffn_w[l]),
        'wr': jnp.stack([wrt.astype(BF16), (wrt - wrt.astype(BF16).astype(F32)).astype(BF16)]),
        'br': brc,
    }


def _layer(l, xp, xs_slab, state_conv, state_ssm, w, w_gate, w_up, w_down):
    bp, seq, _ = xp.shape
    nb, nt = state_conv.shape[1], xs_slab.shape[0] // state_conv.shape[1]
    q = HEADS_PER_GROUP * SSM_HEAD_DIM

    cprev = jnp.transpose(state_conv[l], (1, 0, 2))
    z, ypart, ea, ysg, v, convn, cmat, bmat, xw, dec = _sample_front(xs_slab, cprev, w, nb, nt)
    to_seq = lambda a, d: jnp.pad(
        jnp.transpose(a.reshape(nt, nb, SSM_GROUPS, d), (1, 2, 0, 3)),
        ((0, 0), (0, 0), (0, SUBLANES - nt), (0, 0)))
    ssm_s, zoff = _sample_state(
        dec[:, :SSM_HEADS], state_ssm[l].reshape(nb, SSM_GROUPS, q, SSM_STATE),
        to_seq(cmat, SSM_STATE), to_seq(bmat, SSM_STATE), to_seq(xw, q), STATE_SEQS)
    zoff = jnp.transpose(zoff[:, :, :nt], (2, 0, 1, 3)).reshape(nt * nb, D_SSM)
    h_s, hn2_s, route_s = _sample_back(xs_slab, ypart, ea, zoff, z, ysg, w)

    h, hn2, route, convt_p, ssm_p = _prompt_mixer(xp, w, PROMPT_TILE, h_s, hn2_s, route_s)

    n_tok = h.shape[0]
    bm = MOE_BLOCK_ROWS
    n_blocks = -(-(2 * n_tok) // bm) + N_EXPERTS
    tok, dst, blk_expert, n_used = _moe_plan(route[:, :2].astype(jnp.int32), bm, n_blocks)
    parts = _expert_mlp(blk_expert, n_used, tok, dst, hn2, w_gate[l], w_up[l], w_down[l], bm,
                        2 * n_tok + 2 * bm)
    outs = dict(
        h=h, parts=parts, route=route,
        conv_p=convt_p[:, SUBLANES - (CONV_K - 1):],
        ssm_p=ssm_p.reshape(bp, SSM_GROUPS, HEADS_PER_GROUP, SSM_HEAD_DIM, SSM_STATE),
        conv_s=jnp.transpose(convn, (1, 0, 2)),
        ssm_s=ssm_s.reshape(nb, SSM_GROUPS, HEADS_PER_GROUP, SSM_HEAD_DIM, SSM_STATE),
        v_s=jnp.transpose(v.reshape(nt, nb, D_SG), (1, 0, 2)),
    )
    return outs


def kernel(x_prompt, x_sample, state_conv, state_ssm, norm_mix_w, w_in, conv_w, conv_b, dt_bias, a_log, d_skip, ssm_norm_w, sg_norm_w, sg_w, sg_b, w_out, norm_ffn_w, w_router_group, b_router_group, w_router_expert, b_router_expert, w_gate, w_up, w_down, norm_final_w):
    depth = w_in.shape[0]
    assert depth == 1, "the fused final norm assumes a single layer"
    bp, seq, _ = x_prompt.shape
    nb, nt, _ = x_sample.shape
    l = 0
    w = _layer_weights(l, norm_mix_w, w_in, conv_w, conv_b, dt_bias, a_log, d_skip, ssm_norm_w, sg_norm_w,
                       sg_w, sg_b, w_out, norm_ffn_w, w_router_group, b_router_group, w_router_expert,
                       b_router_expert, nt)
    xs_slab = jnp.transpose(x_sample, (1, 0, 2)).reshape(nt * nb, D_MODEL)
    outs = _layer(l, x_prompt, xs_slab, state_conv, state_ssm, w, w_gate, w_up, w_down)
    y_p, y_s = _final(outs['h'], outs['parts'], outs['route'], norm_final_w.reshape(1, -1).astype(F32),
                      FINAL_TILE, bp * seq)
    y_prompt = y_p.reshape(bp, seq, D_MODEL)
    y_sample = jnp.transpose(y_s.reshape(nt, nb, D_MODEL), (1, 0, 2))
    return (y_prompt, y_sample, outs['conv_p'][None], outs['ssm_p'][None], outs['conv_s'][None],
            outs['ssm_s'][None], outs['v_s'][None])
```

```python
import functools

import jax
import jax.numpy as jnp
from jax import lax
from jax.experimental import pallas as pl
from jax.experimental.pallas import tpu as pltpu

D_MODEL = 1024
D_SSM = 1024
SSM_HEAD_DIM = 64
SSM_HEADS = 16
SSM_GROUPS = 2
HEADS_PER_GROUP = 8
SSM_STATE = 128
CONV_K = 4
C_CONV = D_SSM + 2 * SSM_GROUPS * SSM_STATE
D_SG = 1024
SG_HEADS = 8
SG_HEAD_DIM = 128
CHUNK = 128
N_EXPERT_GROUPS = 4
EXPERTS_PER_GROUP = 8
N_EXPERTS = 32
D_FF = 512
EPS = 1e-6

LANES = 128
SUBLANES = 8
VMEM_LIMIT = 56 * 1024 * 1024

F32 = jnp.float32
BF16 = jnp.bfloat16
HIGHEST = lax.Precision.HIGHEST


def _dot(a, b):
    return jnp.dot(a, b, preferred_element_type=F32)


def _dot_f32(a, b):
    return jnp.dot(a, b, preferred_element_type=F32, precision=HIGHEST)


def _rms(x, w):
    return x * lax.rsqrt(jnp.mean(x * x, axis=-1, keepdims=True) + EPS) * w


def _silu(x):
    return x * (1.0 / (1.0 + jnp.exp(-x)))


def _gelu(x):
    return 0.5 * x * (1.0 + lax.erf(x * 0.7071067811865476))


def _softplus(x):
    return jnp.maximum(x, 0.0) + jnp.log1p(jnp.exp(-jnp.abs(x)))


def _gated_group_norm(y, z, w):
    g = y * _silu(z)
    half = D_SSM // SSM_GROUPS
    parts = []
    for k in range(SSM_GROUPS):
        gk = g[:, k * half:(k + 1) * half]
        parts.append(gk * lax.rsqrt(jnp.mean(gk * gk, axis=-1, keepdims=True) + EPS))
    return jnp.concatenate(parts, axis=1) * w


ROUTER_GROUP_ROW = 0
ROUTER_EXPERT_ROW = SUBLANES


def _route_t(lt):
    r = lt.shape[1]
    row = lax.broadcasted_iota(jnp.int32, (SUBLANES, r), 0)
    rowf = row.astype(F32)
    big = float(SUBLANES)
    red = lambda f, a: f(a, axis=0, keepdims=True)
    gl = jnp.where(row < N_EXPERT_GROUPS, lt[ROUTER_GROUP_ROW:ROUTER_GROUP_ROW + SUBLANES], -jnp.inf)
    ge = jnp.exp(gl - red(jnp.max, gl))
    p_grp = ge / red(jnp.sum, ge)
    g_p = red(jnp.max, p_grp)
    g_idx = red(jnp.min, jnp.where(p_grp == g_p, rowf, big))
    el = lt[ROUTER_EXPERT_ROW:ROUTER_EXPERT_ROW + EXPERTS_PER_GROUP]
    for g in range(1, N_EXPERT_GROUPS):
        lo = ROUTER_EXPERT_ROW + g * EXPERTS_PER_GROUP
        el = jnp.where(g_idx == float(g), lt[lo:lo + EXPERTS_PER_GROUP], el)
    ee = jnp.exp(el - red(jnp.max, el))
    pe = ee / red(jnp.sum, ee)
    v1 = red(jnp.max, pe)
    i1 = red(jnp.min, jnp.where(pe == v1, rowf, big))
    rest = rowf != i1
    pe2 = jnp.where(rest, pe, -1.0)
    v2 = red(jnp.max, pe2)
    i2 = red(jnp.min, jnp.where(rest & (pe2 == v2), rowf, big))
    den = v1 + v2
    base = g_idx * float(EXPERTS_PER_GROUP)
    out = jnp.where(row == 0, base + i1, 0.0)
    out = jnp.where(row == 1, base + i2, out)
    out = jnp.where(row == 2, g_p * v1 / den, out)
    out = jnp.where(row == 3, g_p * v2 / den, out)
    return jnp.concatenate([out, jnp.zeros((LANES - SUBLANES, r), F32)], axis=0).T


def _ffn_front(h, nfw, wrt_ref, brc):
    hn2 = _rms(h, nfw)
    hi = hn2.astype(BF16)
    lo = (hn2 - hi.astype(F32)).astype(BF16)
    nt = lambda a, b: lax.dot_general(a, b, (((1,), (1,)), ((), ())), preferred_element_type=F32)
    lt = nt(wrt_ref[0], hi) + nt(wrt_ref[0], lo) + nt(wrt_ref[1], hi) + brc
    return hn2, _route_t(lt)


ROW_TILES = D_MODEL // LANES


def _store_row_tiles(ref, val):
    r = val.shape[0]
    for k in range(ROW_TILES):
        ref[pl.ds(k, r, stride=ROW_TILES), :] = val[:, k * LANES:(k + 1) * LANES]


def _load_row_tiles(ref):
    r = ref.shape[0] // ROW_TILES
    return jnp.concatenate([ref[pl.ds(k, r, stride=ROW_TILES), :] for k in range(ROW_TILES)], axis=1)


def _prompt_step(
        t, nt, fillers, x_ref, nmw_ref, wz_ref, wxbc_ref, wdt_ref, wuv_ref, convw_ref, convb_ref, dtb_ref,
        alog_ref, dskip_ref, ssmnw_ref, sgnw_ref, sgw_ref, sgb_ref, convt_ref, ssm_ref,
        xbc_scr, state_scr, y_scr, mix_scr, tl):
    fillers = list(fillers)
    assert len(fillers) == 3 + tl // CHUNK

    @pl.when(t == 0)
    def _():
        xbc_scr[0:SUBLANES, :] = jnp.zeros((SUBLANES, C_CONV), F32)
        state_scr[...] = jnp.zeros(state_scr.shape, F32)

    x = x_ref[...]
    hn = _rms(x, nmw_ref[...]).astype(BF16)
    z = _dot(hn, wz_ref[...])
    xbc = _dot(hn, wxbc_ref[...])
    dtr = _dot(hn, wdt_ref[...])
    uv = _gelu(_dot(hn, wuv_ref[...]))
    fillers.pop(0)()

    xbc_scr[SUBLANES:SUBLANES + tl, :] = xbc
    conv = convb_ref[...]
    for k in range(CONV_K):
        off = SUBLANES - (CONV_K - 1) + k
        conv = conv + convw_ref[k:k + 1, :] * xbc_scr[off:off + tl, :]
    tail = xbc_scr[tl:tl + SUBLANES, :]
    xbc_scr[0:SUBLANES, :] = tail
    convt_ref[...] = tail
    act = _silu(conv)
    xs = act[:, :D_SSM]
    bmat = act[:, D_SSM:D_SSM + SSM_GROUPS * SSM_STATE]
    cmat = act[:, D_SSM + SSM_GROUPS * SSM_STATE:]
    fillers.pop(0)()

    dt = _softplus(dtr + dtb_ref[...])
    a_row = -jnp.exp(alog_ref[...])

    li = lax.broadcasted_iota(jnp.int32, (CHUNK, CHUNK), 0)
    si = lax.broadcasted_iota(jnp.int32, (CHUNK, CHUNK), 1)
    causal = li >= si
    tri = jnp.where(causal, 1.0, 0.0).astype(F32)
    lo_half = si < SSM_HEAD_DIM

    for c in range(tl // CHUNK):
        rows = slice(c * CHUNK, (c + 1) * CHUNK)
        dt_c = dt[rows]
        acum = _dot_f32(tri, dt_c * a_row)
        acum_t = acum.T
        dt_t = dt_c.T
        w_t = dt_t * jnp.exp(acum_t[:, CHUNK - 1:CHUNK] - acum_t)
        xs_c = xs[rows]
        for g in range(SSM_GROUPS):
            gl = slice(g * SSM_STATE, (g + 1) * SSM_STATE)
            b_g = bmat[rows, gl]
            c_g = cmat[rows, gl]
            b_gt = b_g.T
            cb = _dot(c_g.astype(BF16), b_gt.astype(BF16))
            for j in range(HEADS_PER_GROUP // 2):
                pl_ = slice(g * 512 + j * LANES, g * 512 + (j + 1) * LANES)
                m_l, ec_l, s_l, ea_last = [], [], [], []
                for k in range(2):
                    hh = g * HEADS_PER_GROUP + 2 * j + k
                    colb = jnp.broadcast_to(acum[:, hh:hh + 1], (CHUNK, CHUNK))
                    rowb = jnp.broadcast_to(acum_t[hh:hh + 1, :], (CHUNK, CHUNK))
                    ea = jnp.exp(colb)
                    seg = jnp.where(causal, colb - rowb, 0.0)
                    lmat = jnp.where(causal, jnp.exp(seg), 0.0)
                    m_l.append(cb * lmat * jnp.broadcast_to(dt_t[hh:hh + 1, :], (CHUNK, CHUNK)))
                    ec_l.append(c_g * ea)
                    s_l.append(b_gt * jnp.broadcast_to(w_t[hh:hh + 1, :], (CHUNK, CHUNK)))
                    ea_last.append(ea[CHUNK - 1:CHUNK, :])
                xs_p = xs_c[:, pl_]
                st_p = state_scr[g, :, j * LANES:(j + 1) * LANES]
                rx = jnp.concatenate([jnp.where(lo_half, xs_p, 0.0),
                                      jnp.where(lo_half, 0.0, xs_p)], axis=0).astype(BF16)
                rs = jnp.concatenate([jnp.where(lo_half, st_p, 0.0),
                                      jnp.where(lo_half, 0.0, st_p)], axis=0).astype(BF16)
                lhs = jnp.concatenate(m_l + ec_l, axis=1).astype(BF16)
                y_p = _dot(lhs, jnp.concatenate([rx, rs], axis=0))
                y_scr[rows, pl_] = y_p
                dec = jnp.where(lo_half[0:1, :], ea_last[0], ea_last[1])
                upd = _dot(jnp.concatenate(s_l, axis=1).astype(BF16), rx)
                state_scr[g, :, j * LANES:(j + 1) * LANES] = st_p * dec + upd
        fillers.pop(0)()

    y = y_scr[...] + dskip_ref[...] * xs
    y_ssm = _gated_group_norm(y, z, ssmnw_ref[...])
    mix_scr[:, :D_SSM] = y_ssm.astype(BF16)

    u = uv[:, :D_SG]
    v = _rms(uv[:, D_SG:], sgnw_ref[...])
    vb = v.astype(BF16)
    for c in range(tl // CHUNK):
        rows = slice(c * CHUNK, (c + 1) * CHUNK)
        for hd in range(SG_HEADS):
            hl = slice(hd * SG_HEAD_DIM, (hd + 1) * SG_HEAD_DIM)
            sv = _dot(sgw_ref[hd], vb[rows, hl]) + sgb_ref[:, hl]
            mix_scr[rows, D_SSM + hd * SG_HEAD_DIM:D_SSM + (hd + 1) * SG_HEAD_DIM] = (
                u[rows, hl] * sv).astype(BF16)

    fillers.pop(0)()

    @pl.when(t == nt - 1)
    def _():
        for g in range(SSM_GROUPS):
            ssm_ref[g] = state_scr[g].T


def _prompt_mixer_kernel(
        x_ref, xp_ref, nmw_ref, wz_ref, wxbc_ref, wdt_ref, wuv_ref, convw_ref, convb_ref, dtb_ref,
        alog_ref, dskip_ref, ssmnw_ref, sgnw_ref, sgw_ref, sgb_ref, wout_ref, nfw_ref,
        wr_ref, br_ref, hs_ref, hn2s_ref, routes_ref,
        h_ref, hn2_ref, route_ref, convt_ref, ssm_ref,
        xbc_scr, state_scr, y_scr, mix_scr, h_scr, *, tl, nt, n_main):
    i = pl.program_id(0)

    @pl.when(i == 0)
    def _():
        mix_scr[...] = jnp.zeros(mix_scr.shape, BF16)

    @pl.when(i > n_main)
    def _():
        h_ref[...] = hs_ref[...]
        hn2_ref[...] = hn2s_ref[...]
        route_ref[...] = routes_ref[...]

    n_out_chunks = 2 + tl // CHUNK
    width = D_MODEL // n_out_chunks
    assert width % LANES == 0 and width * n_out_chunks == D_MODEL

    def out_chunk(c):
        cols = slice(c * width, (c + 1) * width)
        h_scr[:, cols] = xp_ref[:, cols] + _dot(mix_scr[...], wout_ref[:, cols])

    def finish():
        h = h_scr[...]
        h_ref[...] = h
        hn2, route = _ffn_front(h, nfw_ref[...], wr_ref, br_ref[...])
        _store_row_tiles(hn2_ref, hn2)
        route_ref[...] = route

    @pl.when(i <= n_main)
    def _():
        def out_chunks(cs):
            for c in cs:
                out_chunk(c)

        half = n_out_chunks // 2
        fillers = ([functools.partial(out_chunks, range(half)),
                    functools.partial(out_chunks, range(half, n_out_chunks)), finish]
                   + [lambda: None] * (tl // CHUNK))
        _prompt_step(
            lax.rem(i, nt), nt, fillers, x_ref, nmw_ref, wz_ref, wxbc_ref, wdt_ref, wuv_ref,
            convw_ref, convb_ref, dtb_ref, alog_ref, dskip_ref, ssmnw_ref, sgnw_ref, sgw_ref, sgb_ref,
            convt_ref, ssm_ref, xbc_scr, state_scr, y_scr, mix_scr, tl)


def _const_spec(shape):
    zeros = (0,) * len(shape)
    return pl.BlockSpec(shape, lambda *_: zeros)


def _prompt_mixer(x, w, tl, h_s, hn2_s, route_s):
    bsz, seq, _ = x.shape
    nt = seq // tl
    n_main = bsz * nt
    n_s = h_s.shape[0]
    assert seq % tl == 0 and n_s % tl == 0
    n_tok = bsz * seq + n_s
    x2 = x.reshape(bsz * seq, D_MODEL)
    main = lambda i: jnp.minimum(i, n_main - 1)
    prev = lambda i: jnp.clip(i - 1, 0, n_main - 1)
    tail = lambda i: jnp.maximum(i - n_main - 1, 0)
    out_row = lambda i: jnp.maximum(i - 1, 0)
    weights = [w['nmw'], w['wz'], w['wxbc'], w['wdt'], w['wuv'], w['convw'], w['convb'], w['dtb'],
               w['alog'], w['dskip'], w['ssmnw'], w['sgnw'], w['sgw'], w['sgb'], w['wout'], w['nfw'],
               w['wr'], w['br']]
    in_specs = ([pl.BlockSpec((tl, D_MODEL), lambda i: (main(i), 0)),
                 pl.BlockSpec((tl, D_MODEL), lambda i: (prev(i), 0))]
                + [_const_spec(a.shape) for a in weights]
                + [pl.BlockSpec(blk, lambda i: (tail(i), 0))
                   for blk in ((tl, D_MODEL), (tl * ROW_TILES, LANES), (tl, LANES))])
    q = HEADS_PER_GROUP * SSM_HEAD_DIM
    out_shape = [
        jax.ShapeDtypeStruct((n_tok, D_MODEL), F32),
        jax.ShapeDtypeStruct((n_tok * ROW_TILES, LANES), F32),
        jax.ShapeDtypeStruct((n_tok, LANES), F32),
        jax.ShapeDtypeStruct((bsz, SUBLANES, C_CONV), F32),
        jax.ShapeDtypeStruct((bsz, SSM_GROUPS, q, SSM_STATE), F32),
    ]
    out_specs = [
        pl.BlockSpec((tl, D_MODEL), lambda i: (out_row(i), 0)),
        pl.BlockSpec((tl * ROW_TILES, LANES), lambda i: (out_row(i), 0)),
        pl.BlockSpec((tl, LANES), lambda i: (out_row(i), 0)),
        pl.BlockSpec((None, SUBLANES, C_CONV), lambda i: (main(i) // nt, 0, 0)),
        pl.BlockSpec((None, SSM_GROUPS, q, SSM_STATE), lambda i: (main(i) // nt, 0, 0, 0)),
    ]
    scratch = [
        pltpu.VMEM((tl + SUBLANES, C_CONV), F32),
        pltpu.VMEM((SSM_GROUPS, SSM_STATE, q), F32),
        pltpu.VMEM((tl, D_SSM), F32),
        pltpu.VMEM((tl, D_SSM + D_SG), BF16),
        pltpu.VMEM((tl, D_MODEL), F32),
    ]
    return pl.pallas_call(
        functools.partial(_prompt_mixer_kernel, tl=tl, nt=nt, n_main=n_main),
        grid=(n_main + 1 + n_s // tl,),
        in_specs=in_specs,
        out_specs=out_specs,
        out_shape=out_shape,
        scratch_shapes=scratch,
        compiler_params=pltpu.CompilerParams(
            dimension_semantics=("arbitrary",), vmem_limit_bytes=VMEM_LIMIT),
        name="prompt_mixer",
    )(x2, x2, *weights, h_s, hn2_s, route_s)


def _sample_front_kernel(
        x_ref, cprev_ref, nmw_ref, wz_ref, wxbc_ref, wdt_ref, wuv_ref, convw_ref, convb_ref,
        dtb_ref, alog_ref, dskip_ref, sgnw_ref, w4_ref, b4_ref, expand_ref,
        z_ref, ypart_ref, ea_ref, ysg_ref, v_ref, convn_ref, c_ref, b_ref, xw_ref, dec_ref,
        *, nb, nt):
    x = x_ref[...]
    hn = _rms(x, nmw_ref[...]).astype(BF16)
    z_ref[...] = _dot(hn, wz_ref[...])
    xbc = _dot(hn, wxbc_ref[...])
    dtr = _dot(hn, wdt_ref[...])
    uv = _gelu(_dot(hn, wuv_ref[...]))

    slab = lambda a, t: a[t * nb:(t + 1) * nb]
    full = [cprev_ref[k] for k in range(CONV_K - 1)] + [slab(xbc, t) for t in range(nt)]
    for k in range(CONV_K - 1):
        convn_ref[k] = full[nt + k]
    xs, bm, cm = [], [], []
    for t in range(nt):
        conv = convb_ref[...]
        for k in range(CONV_K):
            conv = conv + convw_ref[k:k + 1, :] * full[t + k]
        act = _silu(conv)
        xs.append(act[:, :D_SSM])
        bm.append(act[:, D_SSM:D_SSM + SSM_GROUPS * SSM_STATE])
        cm.append(act[:, D_SSM + SSM_GROUPS * SSM_STATE:])
        b_ref[t * nb:(t + 1) * nb, :] = bm[t]
        c_ref[t * nb:(t + 1) * nb, :] = cm[t]

    dt = _softplus(dtr + dtb_ref[...])
    a_row = -jnp.exp(alog_ref[...])
    dts = [slab(dt, t) for t in range(nt)]
    cum = []
    for t in range(nt):
        da = dts[t] * a_row
        cum.append(da if t == 0 else cum[t - 1] + da)
    dec_ref[...] = jnp.exp(cum[nt - 1])

    lane = lax.broadcasted_iota(jnp.int32, (nb, LANES), 1)
    first_group = lane < HEADS_PER_GROUP
    facs = [jnp.exp(cum[t]) for t in range(nt)]
    facs += [dts[s] * jnp.exp(cum[nt - 1] - cum[s]) for s in range(nt)]
    pairs = []
    for t in range(nt):
        for s in range(t + 1):
            cb = []
            for g in range(SSM_GROUPS):
                gl = slice(g * SSM_STATE, (g + 1) * SSM_STATE)
                cb.append(jnp.sum(cm[t][:, gl] * bm[s][:, gl], axis=1, keepdims=True))
            cbh = jnp.where(first_group, cb[0], cb[1])
            facs.append(jnp.exp(cum[t] - cum[s]) * dts[s] * cbh)
            pairs.append((t, s))
    fx = _dot_f32(jnp.concatenate(facs, axis=0), expand_ref[...])
    fslab = lambda i: fx[i * nb:(i + 1) * nb]
    for t in range(nt):
        ea_ref[t * nb:(t + 1) * nb, :] = fslab(t)
        xw_ref[t * nb:(t + 1) * nb, :] = xs[t] * fslab(nt + t)
    for t in range(nt):
        acc = dskip_ref[...] * xs[t]
        for i, (tt, s) in enumerate(pairs):
            if tt == t:
                acc = acc + fslab(2 * nt + i) * xs[s]
        ypart_ref[t * nb:(t + 1) * nb, :] = acc

    u = uv[:, :D_SG]
    v = _rms(uv[:, D_SG:], sgnw_ref[...])
    v_ref[...] = v
    for t in range(nt):
        sv = b4_ref[t:t + 1, :]
        for s in range(t + 1):
            sv = sv + w4_ref[t, s:s + 1, :] * slab(v, s)
        ysg_ref[t * nb:(t + 1) * nb, :] = slab(u, t) * sv


def _sample_front(x_slab, cprev, w, nb, nt):
    r = nb * nt
    weights = [w['nmw'], w['wz'], w['wxbc'], w['wdt'], w['wuv'], w['convw'], w['convb'], w['dtb'],
               w['alog'], w['dskip'], w['sgnw'], w['w4'], w['b4'], w['expand']]
    ins = [x_slab, cprev] + weights
    out_shape = [
        jax.ShapeDtypeStruct((r, D_SSM), F32),
        jax.ShapeDtypeStruct((r, D_SSM), F32),
        jax.ShapeDtypeStruct((r, D_SSM), F32),
        jax.ShapeDtypeStruct((r, D_SG), F32),
        jax.ShapeDtypeStruct((r, D_SG), F32),
        jax.ShapeDtypeStruct((CONV_K - 1, nb, C_CONV), F32),
        jax.ShapeDtypeStruct((r, SSM_GROUPS * SSM_STATE), F32),
        jax.ShapeDtypeStruct((r, SSM_GROUPS * SSM_STATE), F32),
        jax.ShapeDtypeStruct((r, D_SSM), F32),
        jax.ShapeDtypeStruct((nb, LANES), F32),
    ]
    return pl.pallas_call(
        functools.partial(_sample_front_kernel, nb=nb, nt=nt),
        grid=(1,),
        in_specs=[_const_spec(a.shape) for a in ins],
        out_specs=[_const_spec(s.shape) for s in out_shape],
        out_shape=out_shape,
        compiler_params=pltpu.CompilerParams(
            dimension_semantics=("arbitrary",), vmem_limit_bytes=VMEM_LIMIT),
        name="sample_front",
    )(*ins)


def _sample_state_kernel(dec_ref, st_ref, cq_ref, bq_ref, xw_ref, so_ref, z_ref, *, seqs):
    for bb in range(seqs):
        for g in range(SSM_GROUPS):
            s0 = st_ref[bb, g]
            z_ref[bb, g] = lax.dot_general(
                cq_ref[bb, g].astype(BF16), s0.astype(BF16), (((1,), (1,)), ((), ())),
                preferred_element_type=F32)
            upd = lax.dot_general(
                xw_ref[bb, g].astype(BF16), bq_ref[bb, g].astype(BF16), (((0,), (0,)), ((), ())),
                preferred_element_type=F32)
            for hh in range(HEADS_PER_GROUP):
                hs = slice(hh * SSM_HEAD_DIM, (hh + 1) * SSM_HEAD_DIM)
                so_ref[bb, g, hs, :] = s0[hs] * dec_ref[bb, g * HEADS_PER_GROUP + hh] + upd[hs]


def _sample_state(dec, state, cq, bq, xwq, seqs):
    nb = state.shape[0]
    q = HEADS_PER_GROUP * SSM_HEAD_DIM
    blk = lambda *tail: pl.BlockSpec((seqs, SSM_GROUPS) + tail, lambda i: (i, 0, 0, 0))
    return pl.pallas_call(
        functools.partial(_sample_state_kernel, seqs=seqs),
        grid=(nb // seqs,),
        in_specs=[
            pl.BlockSpec((seqs, SSM_HEADS), lambda i: (i, 0), memory_space=pltpu.SMEM),
            blk(q, SSM_STATE), blk(SUBLANES, SSM_STATE), blk(SUBLANES, SSM_STATE), blk(SUBLANES, q),
        ],
        out_specs=[blk(q, SSM_STATE), blk(SUBLANES, q)],
        out_shape=[
            jax.ShapeDtypeStruct((nb, SSM_GROUPS, q, SSM_STATE), F32),
            jax.ShapeDtypeStruct((nb, SSM_GROUPS, SUBLANES, q), F32),
        ],
        compiler_params=pltpu.CompilerParams(
            dimension_semantics=("arbitrary",), vmem_limit_bytes=VMEM_LIMIT),
        name="sample_state",
    )(dec, state, cq, bq, xwq)


def _sample_back_kernel(x_ref, ypart_ref, ea_ref, zoff_ref, z_ref, ysg_ref, ssmnw_ref, wout_ref,
                        nfw_ref, wr_ref, br_ref, h_ref, hn2_ref, route_ref):
    y = ypart_ref[...] + ea_ref[...] * zoff_ref[...]
    y_ssm = _gated_group_norm(y, z_ref[...], ssmnw_ref[...])
    mix_in = jnp.concatenate([y_ssm.astype(BF16), ysg_ref[...].astype(BF16)], axis=1)
    h = x_ref[...] + _dot(mix_in, wout_ref[...])
    h_ref[...] = h
    hn2, route = _ffn_front(h, nfw_ref[...], wr_ref, br_ref[...])
    _store_row_tiles(hn2_ref, hn2)
    route_ref[...] = route


def _sample_back(x_slab, ypart, ea, zoff, z, ysg, w):
    r = x_slab.shape[0]
    ins = [x_slab, ypart, ea, zoff, z, ysg, w['ssmnw'], w['wout'], w['nfw'], w['wr'], w['br']]
    out_shape = [
        jax.ShapeDtypeStruct((r, D_MODEL), F32),
        jax.ShapeDtypeStruct((r * ROW_TILES, LANES), F32),
        jax.ShapeDtypeStruct((r, LANES), F32),
    ]
    return pl.pallas_call(
        _sample_back_kernel,
        grid=(1,),
        in_specs=[_const_spec(a.shape) for a in ins],
        out_specs=[_const_spec(s.shape) for s in out_shape],
        out_shape=out_shape,
        compiler_params=pltpu.CompilerParams(
            dimension_semantics=("arbitrary",), vmem_limit_bytes=VMEM_LIMIT),
        name="sample_back",
    )(*ins)


EXPERT_PIECES = 8
DMA_PIECES = 4


def _expert_mlp_kernel(be_ref, nu_ref, tok0_ref, tokn_ref, dstp_ref, x_hbm, wg_ref, wu_ref, wd_ref,
                       parts_hbm, xbuf, obuf, xb, hb, wg_b, wu_b, wd_b, gsem, ssem, *, bm):
    j = pl.program_id(0)
    nu = nu_ref[0]
    nt = ROW_TILES
    tile = lambda ref, start: ref.at[pl.ds(pl.multiple_of(start, nt), nt)]

    def gather_copy(idx_ref, s, r):
        return pltpu.make_async_copy(tile(x_hbm, idx_ref[0, r]), xbuf.at[s, pl.ds(r * nt, nt)], gsem.at[s])

    def scatter_copy(s, r):
        return pltpu.make_async_copy(obuf.at[s, pl.ds(r * nt, nt)], tile(parts_hbm, dstp_ref[0, r]),
                                     ssem.at[s])

    def gather_wait(s):
        pltpu.make_async_copy(x_hbm.at[pl.ds(0, bm * nt)], xbuf.at[s], gsem.at[s]).wait()

    def scatter_wait(s):
        pltpu.make_async_copy(obuf.at[s], parts_hbm.at[pl.ds(0, bm * nt)], ssem.at[s]).wait()

    def step(slot):
        gather_wait(slot)
        per = bm // DMA_PIECES

        def start_rows(piece):
            if piece >= DMA_PIECES:
                return
            for r in range(piece * per, (piece + 1) * per):
                gather_copy(tokn_ref, 1 - slot, r).start(priority=r % 2)
                scatter_copy(1 - slot, r).start(priority=(r + 1) % 2)

        xb[...] = _load_row_tiles(xbuf.at[slot]).astype(BF16)
        half = D_FF // 2
        for c in range(2):
            cols = slice(c * half, (c + 1) * half)
            start_rows(2 * c)
            g = _dot(xb[...], wg_b[:, cols])
            start_rows(2 * c + 1)
            u = _dot(xb[...], wu_b[:, cols])
            hb[:, cols] = (_silu(g) * u).astype(BF16)
        quarter = D_MODEL // 4
        scatter_wait(slot)
        for c in range(4):
            start_rows(4 + c)
            o = _dot(hb[...], wd_b[:, c * quarter:(c + 1) * quarter])
            for k in range(quarter // LANES):
                kk = c * (quarter // LANES) + k
                obuf.at[slot][pl.ds(kk, bm, stride=nt), :] = o[:, k * LANES:(k + 1) * LANES]

    def drain(slot):
        gather_wait(slot)
        scatter_wait(slot)
        for r in range(bm):
            scatter_copy(1 - slot, r).start(priority=r % 2)
        scatter_wait(1 - slot)

    @pl.when(j == 0)
    def _():
        obuf[...] = jnp.zeros(obuf.shape, F32)
        n_real = parts_hbm.shape[0] - 2 * bm * nt
        pltpu.make_async_copy(obuf.at[0], parts_hbm.at[pl.ds(n_real, bm * nt)], ssem.at[0]).start()

        def start0(r, carry):
            pltpu.make_async_copy(tile(x_hbm, tok0_ref[0, r]), xbuf.at[0, pl.ds(pl.multiple_of(r * nt, nt), nt)],
                                  gsem.at[0]).start()
            return carry
        lax.fori_loop(0, bm, start0, 0)

    used = j < nu

    @pl.when(used & ((j == 0) | (be_ref[j] != be_ref[jnp.maximum(j - 1, 0)])))
    def _():
        wg_b[...] = wg_ref[...].astype(BF16)
        wu_b[...] = wu_ref[...].astype(BF16)
        wd_b[...] = wd_ref[...].astype(BF16)

    for slot in range(2):
        parity = lax.rem(j, 2) == slot

        @pl.when(used & parity)
        def _(slot=slot):
            step(slot)

        @pl.when((j == nu) & parity)
        def _(slot=slot):
            drain(slot)


def _expert_mlp(blk_expert, n_used, tok, dst, x, wg, wu, wd, bm, n_out):
    n_blocks = blk_expert.shape[0]
    assert bm % EXPERT_PIECES == 0
    w_map = lambda j, be, nu: (be[j], 0, 0)
    idx_spec = lambda f: pl.BlockSpec((None, 1, bm), lambda j, be, nu: (f(j), 0, 0),
                                      memory_space=pltpu.SMEM)
    any_spec = pl.BlockSpec(memory_space=pl.ANY)
    grid_spec = pltpu.PrefetchScalarGridSpec(
        num_scalar_prefetch=2,
        grid=(n_blocks + 1,),
        in_specs=[
            idx_spec(lambda j: 0),
            idx_spec(lambda j: jnp.minimum(j + 1, n_blocks - 1)),
            idx_spec(lambda j: j),
            any_spec,
            pl.BlockSpec((None, D_MODEL, D_FF), w_map),
            pl.BlockSpec((None, D_MODEL, D_FF), w_map),
            pl.BlockSpec((None, D_FF, D_MODEL), w_map),
        ],
        out_specs=any_spec,
        scratch_shapes=[
            pltpu.VMEM((2, bm * ROW_TILES, LANES), F32),
            pltpu.VMEM((2, bm * ROW_TILES, LANES), F32),
            pltpu.VMEM((bm, D_MODEL), BF16),
            pltpu.VMEM((bm, D_FF), BF16),
            pltpu.VMEM((D_MODEL, D_FF), BF16),
            pltpu.VMEM((D_MODEL, D_FF), BF16),
            pltpu.VMEM((D_FF, D_MODEL), BF16),
            pltpu.SemaphoreType.DMA((2,)),
            pltpu.SemaphoreType.DMA((2,)),
        ],
    )
    tok3 = (tok * ROW_TILES).reshape(n_blocks, 1, bm)
    spare = (n_out - bm + jnp.arange(bm, dtype=jnp.int32)).reshape(1, bm)
    dst_prev = (jnp.concatenate([spare, dst], axis=0) * ROW_TILES).reshape(n_blocks + 1, 1, bm)
    blk_expert = jnp.concatenate([blk_expert, blk_expert[-1:]])
    return pl.pallas_call(
        functools.partial(_expert_mlp_kernel, bm=bm),
        grid_spec=grid_spec,
        out_shape=jax.ShapeDtypeStruct((n_out * ROW_TILES, LANES), F32),
        compiler_params=pltpu.CompilerParams(
            dimension_semantics=("arbitrary",), vmem_limit_bytes=VMEM_LIMIT),
        name="expert_mlp",
    )(blk_expert, n_used, tok3, tok3, dst_prev, x, wg, wu, wd)


PAD_ID = 2 ** 16 - 1


def _moe_plan(e_idx, bm, n_blocks):
    n_tok = e_idx.shape[0]
    n_asg = 2 * n_tok
    n_pad = n_blocks * bm - n_asg
    e_flat = e_idx.reshape(-1)
    experts = jnp.arange(N_EXPERTS, dtype=jnp.int32)
    counts = jnp.sum((e_flat[:, None] == experts[None, :]).astype(jnp.int32), axis=0)
    nblk = (counts + bm - 1) // bm
    blk_end = jnp.cumsum(nblk)
    n_used = blk_end[-1]
    j = jnp.arange(n_blocks, dtype=jnp.int32)
    be = jnp.sum((blk_end[None, :] <= jnp.minimum(j, n_used - 1)[:, None]).astype(jnp.int32), axis=1)
    be = jnp.minimum(be, N_EXPERTS - 1)
    assert n_asg < PAD_ID
    pad_end = jnp.cumsum(nblk * bm - counts)
    pad_expert = jnp.sum((pad_end[None, :] <= jnp.arange(n_pad, dtype=jnp.int32)[:, None]).astype(jnp.int32),
                         axis=1)
    keys = jnp.concatenate([2 * e_flat, 2 * pad_expert + 1])
    ids = jnp.concatenate([jnp.arange(n_asg, dtype=jnp.int32), jnp.full((n_pad,), PAD_ID, jnp.int32)])
    window = lax.sort(keys * (PAD_ID + 1) + ids) % (PAD_ID + 1)
    window = window.reshape(n_blocks, bm)
    valid = window != PAD_ID
    q = jnp.arange(bm, dtype=jnp.int32)
    tok = jnp.where(valid, window // 2, (j[:, None] * bm + q[None, :]) % n_tok)
    dst = jnp.where(valid, (window % 2) * n_tok + window // 2,
                    2 * n_tok + q[None, :])
    return tok.astype(jnp.int32), dst.astype(jnp.int32), be, n_used.astype(jnp.int32).reshape(1)


def _final_kernel(h_ref, p0_ref, p1_ref, route_ref, w_ref, op_ref, os_ref, *, n_prompt_steps):
    route = route_ref[...]
    p0 = _load_row_tiles(p0_ref)
    p1 = _load_row_tiles(p1_ref)
    y = _rms(h_ref[...] + route[:, 2:3] * p0 + route[:, 3:4] * p1, w_ref[...])
    i = pl.program_id(0)

    @pl.when(i < n_prompt_steps)
    def _():
        op_ref[...] = y

    @pl.when(i >= n_prompt_steps)
    def _():
        os_ref[...] = y


def _final(h, parts, route, w, tf, n_prompt):
    n_tok = h.shape[0]
    assert n_tok % tf == 0 and n_prompt % tf == 0
    n_p = n_prompt // tf
    row = lambda d: pl.BlockSpec((tf, d), lambda i: (i, 0))
    part = lambda k: pl.BlockSpec((tf * ROW_TILES, LANES), lambda i: (k * (n_tok // tf) + i, 0))
    return pl.pallas_call(
        functools.partial(_final_kernel, n_prompt_steps=n_p),
        grid=(n_tok // tf,),
        in_specs=[row(D_MODEL), part(0), part(1), row(LANES), _const_spec(w.shape)],
        out_specs=[pl.BlockSpec((tf, D_MODEL), lambda i: (jnp.minimum(i, n_p - 1), 0)),
                   pl.BlockSpec((tf, D_MODEL), lambda i: (jnp.maximum(i - n_p, 0), 0))],
        out_shape=[jax.ShapeDtypeStruct((n_prompt, D_MODEL), F32),
                   jax.ShapeDtypeStruct((n_tok - n_prompt, D_MODEL), F32)],
        compiler_params=pltpu.CompilerParams(
            dimension_semantics=("arbitrary",), vmem_limit_bytes=VMEM_LIMIT),
        name="final_combine",
    )(h, parts, parts, route, w)


PROMPT_TILE = 256
MOE_BLOCK_ROWS = 512
FINAL_TILE = 512
STATE_SEQS = 8


def _pad_lanes(a, width=LANES):
    return jnp.pad(a, [(0, 0)] * (a.ndim - 1) + [(0, width - a.shape[-1])])


def _layer_weights(l, norm_mix_w, w_in, conv_w, conv_b, dt_bias, a_log, d_skip, ssm_norm_w, sg_norm_w,
                   sg_w, sg_b, w_out, norm_ffn_w, w_router_group, b_router_group, w_router_expert,
                   b_router_expert, n_dec):
    c0, c1, c2 = D_SSM, D_SSM + C_CONV, D_SSM + C_CONV + SSM_HEADS
    wi = w_in[l]
    causal = jnp.tril(jnp.ones((CHUNK, CHUNK), bool))
    sgw_l = jnp.where(causal, sg_w[l], 0.0)
    head_of_lane = jnp.arange(D_SSM, dtype=jnp.int32) // SSM_HEAD_DIM
    expand = (jnp.arange(LANES, dtype=jnp.int32)[:, None] == head_of_lane[None, :]).astype(F32)
    row = lambda a: a.reshape(1, -1).astype(F32)
    gap = ROUTER_EXPERT_ROW - N_EXPERT_GROUPS
    rest = LANES - ROUTER_EXPERT_ROW - N_EXPERTS
    wrt = jnp.concatenate([w_router_group[l].T, jnp.zeros((gap, D_MODEL), F32), w_router_expert[l].T,
                           jnp.zeros((rest, D_MODEL), F32)], axis=0).astype(F32)
    brc = jnp.concatenate([b_router_group[l], jnp.zeros((gap,), F32), b_router_expert[l],
                           jnp.zeros((rest,), F32)]).astype(F32).reshape(LANES, 1)
    return {
        'nmw': row(norm_mix_w[l]),
        'wz': wi[:, :c0].astype(BF16),
        'wxbc': wi[:, c0:c1].astype(BF16),
        'wdt': _pad_lanes(wi[:, c1:c2]).astype(BF16),
        'wuv': wi[:, c2:].astype(BF16),
        'convw': conv_w[l].astype(F32),
        'convb': row(conv_b[l]),
        'dtb': _pad_lanes(row(dt_bias[l])),
        'alog': _pad_lanes(row(a_log[l])),
        'dskip': row(jnp.repeat(d_skip[l], SSM_HEAD_DIM)),
        'ssmnw': row(ssm_norm_w[l]),
        'sgnw': row(sg_norm_w[l]),
        'sgw': sgw_l.astype(BF16),
        'sgb': jnp.repeat(sg_b[l].T, SG_HEAD_DIM, axis=1).astype(F32),
        'w4': jnp.repeat(jnp.transpose(sgw_l[:, :n_dec, :n_dec], (1, 2, 0)), SG_HEAD_DIM, axis=2).astype(F32),
        'b4': jnp.repeat(sg_b[l][:, :n_dec].T, SG_HEAD_DIM, axis=1).astype(F32),
        'expand': expand,
        'wout': w_out[l].astype(BF16),
        'nfw': row(norm_ffn_w[l]),
        'wr': jnp.stack([wrt.astype(BF16), (wrt - wrt.astype(BF16).astype(F32)).astype(BF16)]),
        'br': brc,
    }


def _layer(l, xp, xs_slab, state_conv, state_ssm, w, w_gate, w_up, w_down):
    bp, seq, _ = xp.shape
    nb, nt = state_conv.shape[1], xs_slab.shape[0] // state_conv.shape[1]
    q = HEADS_PER_GROUP * SSM_HEAD_DIM

    cprev = jnp.transpose(state_conv[l], (1, 0, 2))
    z, ypart, ea, ysg, v, convn, cmat, bmat, xw, dec = _sample_front(xs_slab, cprev, w, nb, nt)
    to_seq = lambda a, d: jnp.pad(
        jnp.transpose(a.reshape(nt, nb, SSM_GROUPS, d), (1, 2, 0, 3)),
        ((0, 0), (0, 0), (0, SUBLANES - nt), (0, 0)))
    ssm_s, zoff = _sample_state(
        dec[:, :SSM_HEADS], state_ssm[l].reshape(nb, SSM_GROUPS, q, SSM_STATE),
        to_seq(cmat, SSM_STATE), to_seq(bmat, SSM_STATE), to_seq(xw, q), STATE_SEQS)
    zoff = jnp.transpose(zoff[:, :, :nt], (2, 0, 1, 3)).reshape(nt * nb, D_SSM)
    h_s, hn2_s, route_s = _sample_back(xs_slab, ypart, ea, zoff, z, ysg, w)

    h, hn2, route, convt_p, ssm_p = _prompt_mixer(xp, w, PROMPT_TILE, h_s, hn2_s, route_s)

    n_tok = h.shape[0]
    bm = MOE_BLOCK_ROWS
    n_blocks = -(-(2 * n_tok) // bm) + N_EXPERTS
    tok, dst, blk_expert, n_used = _moe_plan(route[:, :2].astype(jnp.int32), bm, n_blocks)
    parts = _expert_mlp(blk_expert, n_used, tok, dst, hn2, w_gate[l], w_up[l], w_down[l], bm,
                        2 * n_tok + 2 * bm)
    outs = dict(
        h=h, parts=parts, route=route,
        conv_p=convt_p[:, SUBLANES - (CONV_K - 1):],
        ssm_p=ssm_p.reshape(bp, SSM_GROUPS, HEADS_PER_GROUP, SSM_HEAD_DIM, SSM_STATE),
        conv_s=jnp.transpose(convn, (1, 0, 2)),
        ssm_s=ssm_s.reshape(nb, SSM_GROUPS, HEADS_PER_GROUP, SSM_HEAD_DIM, SSM_STATE),
        v_s=jnp.transpose(v.reshape(nt, nb, D_SG), (1, 0, 2)),
    )
    return outs


def kernel(x_prompt, x_sample, state_conv, state_ssm, norm_mix_w, w_in, conv_w, conv_b, dt_bias, a_log, d_skip, ssm_norm_w, sg_norm_w, sg_w, sg_b, w_out, norm_ffn_w, w_router_group, b_router_group, w_router_expert, b_router_expert, w_gate, w_up, w_down, norm_final_w):
    depth = w_in.shape[0]
    assert depth == 1, "the fused final norm assumes a single layer"
    bp, seq, _ = x_prompt.shape
    nb, nt, _ = x_sample.shape
    l = 0
    w = _layer_weights(l, norm_mix_w, w_in, conv_w, conv_b, dt_bias, a_log, d_skip, ssm_norm_w, sg_norm_w,
                       sg_w, sg_b, w_out, norm_ffn_w, w_router_group, b_router_group, w_router_expert,
                       b_router_expert, nt)
    xs_slab = jnp.transpose(x_sample, (1, 0, 2)).reshape(nt * nb, D_MODEL)
    outs = _layer(l, x_prompt, xs_slab, state_conv, state_ssm, w, w_gate, w_up, w_down)
    y_p, y_s = _final(outs['h'], outs['parts'], outs['route'], norm_final_w.reshape(1, -1).astype(F32),
                      FINAL_TILE, bp * seq)
    y_prompt = y_p.reshape(bp, seq, D_MODEL)
    y_sample = jnp.transpose(y_s.reshape(nt, nb, D_MODEL), (1, 0, 2))
    return (y_prompt, y_sample, outs['conv_p'][None], outs['ssm_p'][None], outs['conv_s'][None],
            outs['ssm_s'][None], outs['v_s'][None])
```

```python
import functools

import jax
import jax.numpy as jnp
from jax import lax
from jax.experimental import pallas as pl
from jax.experimental.pallas import tpu as pltpu

D_MODEL = 1024
D_SSM = 1024
SSM_HEAD_DIM = 64
SSM_HEADS = 16
SSM_GROUPS = 2
HEADS_PER_GROUP = 8
SSM_STATE = 128
CONV_K = 4
C_CONV = D_SSM + 2 * SSM_GROUPS * SSM_STATE
D_SG = 1024
SG_HEADS = 8
SG_HEAD_DIM = 128
CHUNK = 128
N_EXPERT_GROUPS = 4
EXPERTS_PER_GROUP = 8
N_EXPERTS = 32
D_FF = 512
EPS = 1e-6

LANES = 128
SUBLANES = 8
VMEM_LIMIT = 56 * 1024 * 1024

F32 = jnp.float32
BF16 = jnp.bfloat16
HIGHEST = lax.Precision.HIGHEST


def _dot(a, b):
    return jnp.dot(a, b, preferred_element_type=F32)


def _dot_f32(a, b):
    return jnp.dot(a, b, preferred_element_type=F32, precision=HIGHEST)


def _rms(x, w):
    return x * lax.rsqrt(jnp.mean(x * x, axis=-1, keepdims=True) + EPS) * w


def _silu(x):
    return x * (0.5 + 0.5 * jnp.tanh(0.5 * x))


def _gelu(x):
    return 0.5 * x * (1.0 + lax.erf(x * 0.7071067811865476))


def _softplus(x):
    return jnp.maximum(x, 0.0) + jnp.log1p(jnp.exp(-jnp.abs(x)))


def _gated_group_norm(y, z, w):
    g = y * _silu(z)
    half = D_SSM // SSM_GROUPS
    parts = []
    for k in range(SSM_GROUPS):
        gk = g[:, k * half:(k + 1) * half]
        parts.append(gk * lax.rsqrt(jnp.mean(gk * gk, axis=-1, keepdims=True) + EPS))
    return jnp.concatenate(parts, axis=1) * w


ROUTER_GROUP_ROW = 0
ROUTER_EXPERT_ROW = SUBLANES


def _route_t(lt):
    r = lt.shape[1]
    row = lax.broadcasted_iota(jnp.int32, (SUBLANES, r), 0)
    rowf = row.astype(F32)
    big = float(SUBLANES)
    red = lambda f, a: f(a, axis=0, keepdims=True)
    gl = jnp.where(row < N_EXPERT_GROUPS, lt[ROUTER_GROUP_ROW:ROUTER_GROUP_ROW + SUBLANES], -jnp.inf)
    ge = jnp.exp(gl - red(jnp.max, gl))
    p_grp = ge / red(jnp.sum, ge)
    g_p = red(jnp.max, p_grp)
    g_idx = red(jnp.min, jnp.where(p_grp == g_p, rowf, big))
    el = lt[ROUTER_EXPERT_ROW:ROUTER_EXPERT_ROW + EXPERTS_PER_GROUP]
    for g in range(1, N_EXPERT_GROUPS):
        lo = ROUTER_EXPERT_ROW + g * EXPERTS_PER_GROUP
        el = jnp.where(g_idx == float(g), lt[lo:lo + EXPERTS_PER_GROUP], el)
    ee = jnp.exp(el - red(jnp.max, el))
    pe = ee / red(jnp.sum, ee)
    v1 = red(jnp.max, pe)
    i1 = red(jnp.min, jnp.where(pe == v1, rowf, big))
    rest = rowf != i1
    pe2 = jnp.where(rest, pe, -1.0)
    v2 = red(jnp.max, pe2)
    i2 = red(jnp.min, jnp.where(rest & (pe2 == v2), rowf, big))
    den = v1 + v2
    base = g_idx * float(EXPERTS_PER_GROUP)
    out = jnp.where(row == 0, base + i1, 0.0)
    out = jnp.where(row == 1, base + i2, out)
    out = jnp.where(row == 2, g_p * v1 / den, out)
    out = jnp.where(row == 3, g_p * v2 / den, out)
    return jnp.concatenate([out, jnp.zeros((LANES - SUBLANES, r), F32)], axis=0).T


def _ffn_front(h, nfw, wrt_ref, brc):
    hn2 = _rms(h, nfw)
    hi = hn2.astype(BF16)
    lo = (hn2 - hi.astype(F32)).astype(BF16)
    nt = lambda a, b: lax.dot_general(a, b, (((1,), (1,)), ((), ())), preferred_element_type=F32)
    lt = nt(wrt_ref[0], hi) + nt(wrt_ref[0], lo) + nt(wrt_ref[1], hi) + brc
    return hn2, _route_t(lt)


ROW_TILES = D_MODEL // LANES


def _store_row_tiles(ref, val):
    r = val.shape[0]
    for k in range(ROW_TILES):
        ref[pl.ds(k, r, stride=ROW_TILES), :] = val[:, k * LANES:(k + 1) * LANES]


def _load_row_tiles(ref):
    r = ref.shape[0] // ROW_TILES
    return jnp.concatenate([ref[pl.ds(k, r, stride=ROW_TILES), :] for k in range(ROW_TILES)], axis=1)


def _prompt_step(
        t, nt, fillers, x_ref, nmw_ref, wz_ref, wxbc_ref, wdt_ref, wuv_ref, convw_ref, convb_ref, dtb_ref,
        alog_ref, dskip_ref, ssmnw_ref, sgnw_ref, sgw_ref, sgb_ref, convt_ref, ssm_ref,
        xbc_scr, state_scr, y_scr, mix_scr, tl):
    fillers = list(fillers)
    assert len(fillers) == 3 + tl // CHUNK

    @pl.when(t == 0)
    def _():
        xbc_scr[0:SUBLANES, :] = jnp.zeros((SUBLANES, C_CONV), F32)
        state_scr[...] = jnp.zeros(state_scr.shape, F32)

    x = x_ref[...]
    hn = _rms(x, nmw_ref[...]).astype(BF16)
    z = _dot(hn, wz_ref[...])
    xbc = _dot(hn, wxbc_ref[...])
    dtr = _dot(hn, wdt_ref[...])
    uv = _gelu(_dot(hn, wuv_ref[...]))
    fillers.pop(0)()

    xbc_scr[SUBLANES:SUBLANES + tl, :] = xbc
    conv = convb_ref[...]
    for k in range(CONV_K):
        off = SUBLANES - (CONV_K - 1) + k
        conv = conv + convw_ref[k:k + 1, :] * xbc_scr[off:off + tl, :]
    tail = xbc_scr[tl:tl + SUBLANES, :]
    xbc_scr[0:SUBLANES, :] = tail
    convt_ref[...] = tail
    act = _silu(conv)
    xs = act[:, :D_SSM]
    bmat = act[:, D_SSM:D_SSM + SSM_GROUPS * SSM_STATE]
    cmat = act[:, D_SSM + SSM_GROUPS * SSM_STATE:]
    fillers.pop(0)()

    dt = _softplus(dtr + dtb_ref[...])
    a_row = -jnp.exp(alog_ref[...])

    li = lax.broadcasted_iota(jnp.int32, (CHUNK, CHUNK), 0)
    si = lax.broadcasted_iota(jnp.int32, (CHUNK, CHUNK), 1)
    causal = li >= si
    tri = jnp.where(causal, 1.0, 0.0).astype(F32)
    lo_half = si < SSM_HEAD_DIM

    for c in range(tl // CHUNK):
        rows = slice(c * CHUNK, (c + 1) * CHUNK)
        dt_c = dt[rows]
        acum = _dot_f32(tri, dt_c * a_row)
        acum_t = acum.T
        dt_t = dt_c.T
        w_t = dt_t * jnp.exp(acum_t[:, CHUNK - 1:CHUNK] - acum_t)
        xs_c = xs[rows]
        for g in range(SSM_GROUPS):
            gl = slice(g * SSM_STATE, (g + 1) * SSM_STATE)
            b_g = bmat[rows, gl]
            c_g = cmat[rows, gl]
            b_gt = b_g.T
            cb = _dot(c_g.astype(BF16), b_gt.astype(BF16))
            for j in range(HEADS_PER_GROUP // 2):
                pl_ = slice(g * 512 + j * LANES, g * 512 + (j + 1) * LANES)
                m_l, ec_l, s_l, ea_last = [], [], [], []
                for k in range(2):
                    hh = g * HEADS_PER_GROUP + 2 * j + k
                    colb = jnp.broadcast_to(acum[:, hh:hh + 1], (CHUNK, CHUNK))
                    rowb = jnp.broadcast_to(acum_t[hh:hh + 1, :], (CHUNK, CHUNK))
                    ea = jnp.exp(colb)
                    seg = jnp.where(causal, colb - rowb, 0.0)
                    lmat = jnp.where(causal, jnp.exp(seg), 0.0)
                    m_l.append(cb * lmat * jnp.broadcast_to(dt_t[hh:hh + 1, :], (CHUNK, CHUNK)))
                    ec_l.append(c_g * ea)
                    s_l.append(b_gt * jnp.broadcast_to(w_t[hh:hh + 1, :], (CHUNK, CHUNK)))
                    ea_last.append(ea[CHUNK - 1:CHUNK, :])
                xs_p = xs_c[:, pl_]
                st_p = state_scr[g, :, j * LANES:(j + 1) * LANES]
                rx = jnp.concatenate([jnp.where(lo_half, xs_p, 0.0),
                                      jnp.where(lo_half, 0.0, xs_p)], axis=0).astype(BF16)
                rs = jnp.concatenate([jnp.where(lo_half, st_p, 0.0),
                                      jnp.where(lo_half, 0.0, st_p)], axis=0).astype(BF16)
                lhs = jnp.concatenate(m_l + ec_l, axis=1).astype(BF16)
                y_p = _dot(lhs, jnp.concatenate([rx, rs], axis=0))
                y_scr[rows, pl_] = y_p
                dec = jnp.where(lo_half[0:1, :], ea_last[0], ea_last[1])
                upd = _dot(jnp.concatenate(s_l, axis=1).astype(BF16), rx)
                state_scr[g, :, j * LANES:(j + 1) * LANES] = st_p * dec + upd
        fillers.pop(0)()

    y = y_scr[...] + dskip_ref[...] * xs
    y_ssm = _gated_group_norm(y, z, ssmnw_ref[...])
    mix_scr[:, :D_SSM] = y_ssm.astype(BF16)

    u = uv[:, :D_SG]
    v = _rms(uv[:, D_SG:], sgnw_ref[...])
    vb = v.astype(BF16)
    for c in range(tl // CHUNK):
        rows = slice(c * CHUNK, (c + 1) * CHUNK)
        for hd in range(SG_HEADS):
            hl = slice(hd * SG_HEAD_DIM, (hd + 1) * SG_HEAD_DIM)
            sv = _dot(sgw_ref[hd], vb[rows, hl]) + sgb_ref[:, hl]
            mix_scr[rows, D_SSM + hd * SG_HEAD_DIM:D_SSM + (hd + 1) * SG_HEAD_DIM] = (
                u[rows, hl] * sv).astype(BF16)

    fillers.pop(0)()

    @pl.when(t == nt - 1)
    def _():
        for g in range(SSM_GROUPS):
            ssm_ref[g] = state_scr[g].T


def _prompt_mixer_kernel(
        x_ref, xp_ref, nmw_ref, wz_ref, wxbc_ref, wdt_ref, wuv_ref, convw_ref, convb_ref, dtb_ref,
        alog_ref, dskip_ref, ssmnw_ref, sgnw_ref, sgw_ref, sgb_ref, wout_ref, nfw_ref,
        wr_ref, br_ref, hs_ref, hn2s_ref, routes_ref,
        h_ref, hn2_ref, route_ref, convt_ref, ssm_ref,
        xbc_scr, state_scr, y_scr, mix_scr, h_scr, *, tl, nt, n_main):
    i = pl.program_id(0)

    @pl.when(i == 0)
    def _():
        mix_scr[...] = jnp.zeros(mix_scr.shape, BF16)

    @pl.when(i > n_main)
    def _():
        h_ref[...] = hs_ref[...]
        hn2_ref[...] = hn2s_ref[...]
        route_ref[...] = routes_ref[...]

    n_out_chunks = 2 + tl // CHUNK
    width = D_MODEL // n_out_chunks
    assert width % LANES == 0 and width * n_out_chunks == D_MODEL

    def out_chunk(c):
        cols = slice(c * width, (c + 1) * width)
        h_scr[:, cols] = xp_ref[:, cols] + _dot(mix_scr[...], wout_ref[:, cols])

    def finish():
        h = h_scr[...]
        h_ref[...] = h
        hn2, route = _ffn_front(h, nfw_ref[...], wr_ref, br_ref[...])
        _store_row_tiles(hn2_ref, hn2)
        route_ref[...] = route

    @pl.when(i <= n_main)
    def _():
        def out_chunks(cs):
            for c in cs:
                out_chunk(c)

        half = n_out_chunks // 2
        fillers = ([functools.partial(out_chunks, range(half)),
                    functools.partial(out_chunks, range(half, n_out_chunks)), finish]
                   + [lambda: None] * (tl // CHUNK))
        _prompt_step(
            lax.rem(i, nt), nt, fillers, x_ref, nmw_ref, wz_ref, wxbc_ref, wdt_ref, wuv_ref,
            convw_ref, convb_ref, dtb_ref, alog_ref, dskip_ref, ssmnw_ref, sgnw_ref, sgw_ref, sgb_ref,
            convt_ref, ssm_ref, xbc_scr, state_scr, y_scr, mix_scr, tl)


def _const_spec(shape):
    zeros = (0,) * len(shape)
    return pl.BlockSpec(shape, lambda *_: zeros)


def _prompt_mixer(x, w, tl, h_s, hn2_s, route_s):
    bsz, seq, _ = x.shape
    nt = seq // tl
    n_main = bsz * nt
    n_s = h_s.shape[0]
    assert seq % tl == 0 and n_s % tl == 0
    n_tok = bsz * seq + n_s
    x2 = x.reshape(bsz * seq, D_MODEL)
    main = lambda i: jnp.minimum(i, n_main - 1)
    prev = lambda i: jnp.clip(i - 1, 0, n_main - 1)
    tail = lambda i: jnp.maximum(i - n_main - 1, 0)
    out_row = lambda i: jnp.maximum(i - 1, 0)
    weights = [w['nmw'], w['wz'], w['wxbc'], w['wdt'], w['wuv'], w['convw'], w['convb'], w['dtb'],
               w['alog'], w['dskip'], w['ssmnw'], w['sgnw'], w['sgw'], w['sgb'], w['wout'], w['nfw'],
               w['wr'], w['br']]
    in_specs = ([pl.BlockSpec((tl, D_MODEL), lambda i: (main(i), 0)),
                 pl.BlockSpec((tl, D_MODEL), lambda i: (prev(i), 0))]
                + [_const_spec(a.shape) for a in weights]
                + [pl.BlockSpec(blk, lambda i: (tail(i), 0))
                   for blk in ((tl, D_MODEL), (tl * ROW_TILES, LANES), (tl, LANES))])
    q = HEADS_PER_GROUP * SSM_HEAD_DIM
    out_shape = [
        jax.ShapeDtypeStruct((n_tok, D_MODEL), F32),
        jax.ShapeDtypeStruct((n_tok * ROW_TILES, LANES), F32),
        jax.ShapeDtypeStruct((n_tok, LANES), F32),
        jax.ShapeDtypeStruct((bsz, SUBLANES, C_CONV), F32),
        jax.ShapeDtypeStruct((bsz, SSM_GROUPS, q, SSM_STATE), F32),
    ]
    out_specs = [
        pl.BlockSpec((tl, D_MODEL), lambda i: (out_row(i), 0)),
        pl.BlockSpec((tl * ROW_TILES, LANES), lambda i: (out_row(i), 0)),
        pl.BlockSpec((tl, LANES), lambda i: (out_row(i), 0)),
        pl.BlockSpec((None, SUBLANES, C_CONV), lambda i: (main(i) // nt, 0, 0)),
        pl.BlockSpec((None, SSM_GROUPS, q, SSM_STATE), lambda i: (main(i) // nt, 0, 0, 0)),
    ]
    scratch = [
        pltpu.VMEM((tl + SUBLANES, C_CONV), F32),
        pltpu.VMEM((SSM_GROUPS, SSM_STATE, q), F32),
        pltpu.VMEM((tl, D_SSM), F32),
        pltpu.VMEM((tl, D_SSM + D_SG), BF16),
        pltpu.VMEM((tl, D_MODEL), F32),
    ]
    return pl.pallas_call(
        functools.partial(_prompt_mixer_kernel, tl=tl, nt=nt, n_main=n_main),
        grid=(n_main + 1 + n_s // tl,),
        in_specs=in_specs,
        out_specs=out_specs,
        out_shape=out_shape,
        scratch_shapes=scratch,
        compiler_params=pltpu.CompilerParams(
            dimension_semantics=("arbitrary",), vmem_limit_bytes=VMEM_LIMIT),
        name="prompt_mixer",
    )(x2, x2, *weights, h_s, hn2_s, route_s)


def _sample_front_kernel(
        x_ref, cprev_ref, nmw_ref, wz_ref, wxbc_ref, wdt_ref, wuv_ref, convw_ref, convb_ref,
        dtb_ref, alog_ref, dskip_ref, sgnw_ref, w4_ref, b4_ref, expand_ref,
        z_ref, ypart_ref, ea_ref, ysg_ref, v_ref, convn_ref, c_ref, b_ref, xw_ref, dec_ref,
        *, nb, nt):
    x = x_ref[...]
    hn = _rms(x, nmw_ref[...]).astype(BF16)
    z_ref[...] = _dot(hn, wz_ref[...])
    xbc = _dot(hn, wxbc_ref[...])
    dtr = _dot(hn, wdt_ref[...])
    uv = _gelu(_dot(hn, wuv_ref[...]))

    slab = lambda a, t: a[t * nb:(t + 1) * nb]
    full = [cprev_ref[k] for k in range(CONV_K - 1)] + [slab(xbc, t) for t in range(nt)]
    for k in range(CONV_K - 1):
        convn_ref[k] = full[nt + k]
    xs, bm, cm = [], [], []
    for t in range(nt):
        conv = convb_ref[...]
        for k in range(CONV_K):
            conv = conv + convw_ref[k:k + 1, :] * full[t + k]
        act = _silu(conv)
        xs.append(act[:, :D_SSM])
        bm.append(act[:, D_SSM:D_SSM + SSM_GROUPS * SSM_STATE])
        cm.append(act[:, D_SSM + SSM_GROUPS * SSM_STATE:])
        b_ref[t * nb:(t + 1) * nb, :] = bm[t]
        c_ref[t * nb:(t + 1) * nb, :] = cm[t]

    dt = _softplus(dtr + dtb_ref[...])
    a_row = -jnp.exp(alog_ref[...])
    dts = [slab(dt, t) for t in range(nt)]
    cum = []
    for t in range(nt):
        da = dts[t] * a_row
        cum.append(da if t == 0 else cum[t - 1] + da)
    dec_ref[...] = jnp.exp(cum[nt - 1])

    lane = lax.broadcasted_iota(jnp.int32, (nb, LANES), 1)
    first_group = lane < HEADS_PER_GROUP
    facs = [jnp.exp(cum[t]) for t in range(nt)]
    facs += [dts[s] * jnp.exp(cum[nt - 1] - cum[s]) for s in range(nt)]
    pairs = []
    for t in range(nt):
        for s in range(t + 1):
            cb = []
            for g in range(SSM_GROUPS):
                gl = slice(g * SSM_STATE, (g + 1) * SSM_STATE)
                cb.append(jnp.sum(cm[t][:, gl] * bm[s][:, gl], axis=1, keepdims=True))
            cbh = jnp.where(first_group, cb[0], cb[1])
            facs.append(jnp.exp(cum[t] - cum[s]) * dts[s] * cbh)
            pairs.append((t, s))
    fx = _dot_f32(jnp.concatenate(facs, axis=0), expand_ref[...])
    fslab = lambda i: fx[i * nb:(i + 1) * nb]
    for t in range(nt):
        ea_ref[t * nb:(t + 1) * nb, :] = fslab(t)
        xw_ref[t * nb:(t + 1) * nb, :] = xs[t] * fslab(nt + t)
    for t in range(nt):
        acc = dskip_ref[...] * xs[t]
        for i, (tt, s) in enumerate(pairs):
            if tt == t:
                acc = acc + fslab(2 * nt + i) * xs[s]
        ypart_ref[t * nb:(t + 1) * nb, :] = acc

    u = uv[:, :D_SG]
    v = _rms(uv[:, D_SG:], sgnw_ref[...])
    v_ref[...] = v
    for t in range(nt):
        sv = b4_ref[t:t + 1, :]
        for s in range(t + 1):
            sv = sv + w4_ref[t, s:s + 1, :] * slab(v, s)
        ysg_ref[t * nb:(t + 1) * nb, :] = slab(u, t) * sv


def _sample_front(x_slab, cprev, w, nb, nt):
    r = nb * nt
    weights = [w['nmw'], w['wz'], w['wxbc'], w['wdt'], w['wuv'], w['convw'], w['convb'], w['dtb'],
               w['alog'], w['dskip'], w['sgnw'], w['w4'], w['b4'], w['expand']]
    ins = [x_slab, cprev] + weights
    out_shape = [
        jax.ShapeDtypeStruct((r, D_SSM), F32),
        jax.ShapeDtypeStruct((r, D_SSM), F32),
        jax.ShapeDtypeStruct((r, D_SSM), F32),
        jax.ShapeDtypeStruct((r, D_SG), F32),
        jax.ShapeDtypeStruct((r, D_SG), F32),
        jax.ShapeDtypeStruct((CONV_K - 1, nb, C_CONV), F32),
        jax.ShapeDtypeStruct((r, SSM_GROUPS * SSM_STATE), F32),
        jax.ShapeDtypeStruct((r, SSM_GROUPS * SSM_STATE), F32),
        jax.ShapeDtypeStruct((r, D_SSM), F32),
        jax.ShapeDtypeStruct((nb, LANES), F32),
    ]
    return pl.pallas_call(
        functools.partial(_sample_front_kernel, nb=nb, nt=nt),
        grid=(1,),
        in_specs=[_const_spec(a.shape) for a in ins],
        out_specs=[_const_spec(s.shape) for s in out_shape],
        out_shape=out_shape,
        compiler_params=pltpu.CompilerParams(
            dimension_semantics=("arbitrary",), vmem_limit_bytes=VMEM_LIMIT),
        name="sample_front",
    )(*ins)


def _sample_state_kernel(dec_ref, st_ref, cq_ref, bq_ref, xw_ref, so_ref, z_ref, *, seqs):
    for bb in range(seqs):
        for g in range(SSM_GROUPS):
            s0 = st_ref[bb, g]
            z_ref[bb, g] = lax.dot_general(
                cq_ref[bb, g].astype(BF16), s0.astype(BF16), (((1,), (1,)), ((), ())),
                preferred_element_type=F32)
            upd = lax.dot_general(
                xw_ref[bb, g].astype(BF16), bq_ref[bb, g].astype(BF16), (((0,), (0,)), ((), ())),
                preferred_element_type=F32)
            for hh in range(HEADS_PER_GROUP):
                hs = slice(hh * SSM_HEAD_DIM, (hh + 1) * SSM_HEAD_DIM)
                so_ref[bb, g, hs, :] = s0[hs] * dec_ref[bb, g * HEADS_PER_GROUP + hh] + upd[hs]


def _sample_state(dec, state, cq, bq, xwq, seqs):
    nb = state.shape[0]
    q = HEADS_PER_GROUP * SSM_HEAD_DIM
    blk = lambda *tail: pl.BlockSpec((seqs, SSM_GROUPS) + tail, lambda i: (i, 0, 0, 0))
    return pl.pallas_call(
        functools.partial(_sample_state_kernel, seqs=seqs),
        grid=(nb // seqs,),
        in_specs=[
            pl.BlockSpec((seqs, SSM_HEADS), lambda i: (i, 0), memory_space=pltpu.SMEM),
            blk(q, SSM_STATE), blk(SUBLANES, SSM_STATE), blk(SUBLANES, SSM_STATE), blk(SUBLANES, q),
        ],
        out_specs=[blk(q, SSM_STATE), blk(SUBLANES, q)],
        out_shape=[
            jax.ShapeDtypeStruct((nb, SSM_GROUPS, q, SSM_STATE), F32),
            jax.ShapeDtypeStruct((nb, SSM_GROUPS, SUBLANES, q), F32),
        ],
        compiler_params=pltpu.CompilerParams(
            dimension_semantics=("arbitrary",), vmem_limit_bytes=VMEM_LIMIT),
        name="sample_state",
    )(dec, state, cq, bq, xwq)


def _sample_back_kernel(x_ref, ypart_ref, ea_ref, zoff_ref, z_ref, ysg_ref, ssmnw_ref, wout_ref,
                        nfw_ref, wr_ref, br_ref, h_ref, hn2_ref, route_ref):
    y = ypart_ref[...] + ea_ref[...] * zoff_ref[...]
    y_ssm = _gated_group_norm(y, z_ref[...], ssmnw_ref[...])
    mix_in = jnp.concatenate([y_ssm.astype(BF16), ysg_ref[...].astype(BF16)], axis=1)
    h = x_ref[...] + _dot(mix_in, wout_ref[...])
    h_ref[...] = h
    hn2, route = _ffn_front(h, nfw_ref[...], wr_ref, br_ref[...])
    _store_row_tiles(hn2_ref, hn2)
    route_ref[...] = route


def _sample_back(x_slab, ypart, ea, zoff, z, ysg, w):
    r = x_slab.shape[0]
    ins = [x_slab, ypart, ea, zoff, z, ysg, w['ssmnw'], w['wout'], w['nfw'], w['wr'], w['br']]
    out_shape = [
        jax.ShapeDtypeStruct((r, D_MODEL), F32),
        jax.ShapeDtypeStruct((r * ROW_TILES, LANES), F32),
        jax.ShapeDtypeStruct((r, LANES), F32),
    ]
    return pl.pallas_call(
        _sample_back_kernel,
        grid=(1,),
        in_specs=[_const_spec(a.shape) for a in ins],
        out_specs=[_const_spec(s.shape) for s in out_shape],
        out_shape=out_shape,
        compiler_params=pltpu.CompilerParams(
            dimension_semantics=("arbitrary",), vmem_limit_bytes=VMEM_LIMIT),
        name="sample_back",
    )(*ins)


EXPERT_PIECES = 8
DMA_PIECES = 4


def _expert_mlp_kernel(be_ref, nu_ref, tok0_ref, tokn_ref, dstp_ref, x_hbm, wg_ref, wu_ref, wd_ref,
                       parts_hbm, xbuf, obuf, xb, hb, wg_b, wu_b, wd_b, gsem, ssem, *, bm):
    j = pl.program_id(0)
    nu = nu_ref[0]
    nt = ROW_TILES
    tile = lambda ref, start: ref.at[pl.ds(pl.multiple_of(start, nt), nt)]

    def gather_copy(idx_ref, s, r):
        return pltpu.make_async_copy(tile(x_hbm, idx_ref[0, r]), xbuf.at[s, pl.ds(r * nt, nt)], gsem.at[s])

    def scatter_copy(s, r):
        return pltpu.make_async_copy(obuf.at[s, pl.ds(r * nt, nt)], tile(parts_hbm, dstp_ref[0, r]),
                                     ssem.at[s])

    def gather_wait(s):
        pltpu.make_async_copy(x_hbm.at[pl.ds(0, bm * nt)], xbuf.at[s], gsem.at[s]).wait()

    def scatter_wait(s):
        pltpu.make_async_copy(obuf.at[s], parts_hbm.at[pl.ds(0, bm * nt)], ssem.at[s]).wait()

    def step(slot):
        gather_wait(slot)
        per = bm // DMA_PIECES

        def start_rows(piece):
            if piece >= DMA_PIECES:
                return
            for r in range(piece * per, (piece + 1) * per):
                gather_copy(tokn_ref, 1 - slot, r).start(priority=r % 2)
                scatter_copy(1 - slot, r).start(priority=(r + 1) % 2)

        xb[...] = _load_row_tiles(xbuf.at[slot]).astype(BF16)
        half = D_FF // 2
        for c in range(2):
            cols = slice(c * half, (c + 1) * half)
            start_rows(2 * c)
            g = _dot(xb[...], wg_b[:, cols])
            start_rows(2 * c + 1)
            u = _dot(xb[...], wu_b[:, cols])
            hb[:, cols] = (_silu(g) * u).astype(BF16)
        quarter = D_MODEL // 4
        scatter_wait(slot)
        for c in range(4):
            start_rows(4 + c)
            o = _dot(hb[...], wd_b[:, c * quarter:(c + 1) * quarter])
            for k in range(quarter // LANES):
                kk = c * (quarter // LANES) + k
                obuf.at[slot][pl.ds(kk, bm, stride=nt), :] = o[:, k * LANES:(k + 1) * LANES]

    def drain(slot):
        gather_wait(slot)
        scatter_wait(slot)
        for r in range(bm):
            scatter_copy(1 - slot, r).start(priority=r % 2)
        scatter_wait(1 - slot)

    @pl.when(j == 0)
    def _():
        obuf[...] = jnp.zeros(obuf.shape, F32)
        n_real = parts_hbm.shape[0] - 2 * bm * nt
        pltpu.make_async_copy(obuf.at[0], parts_hbm.at[pl.ds(n_real, bm * nt)], ssem.at[0]).start()

        def start0(r, carry):
            pltpu.make_async_copy(tile(x_hbm, tok0_ref[0, r]), xbuf.at[0, pl.ds(pl.multiple_of(r * nt, nt), nt)],
                                  gsem.at[0]).start()
            return carry
        lax.fori_loop(0, bm, start0, 0)

    used = j < nu

    @pl.when(used & ((j == 0) | (be_ref[j] != be_ref[jnp.maximum(j - 1, 0)])))
    def _():
        wg_b[...] = wg_ref[...].astype(BF16)
        wu_b[...] = wu_ref[...].astype(BF16)
        wd_b[...] = wd_ref[...].astype(BF16)

    for slot in range(2):
        parity = lax.rem(j, 2) == slot

        @pl.when(used & parity)
        def _(slot=slot):
            step(slot)

        @pl.when((j == nu) & parity)
        def _(slot=slot):
            drain(slot)


def _expert_mlp(blk_expert, n_used, tok, dst, x, wg, wu, wd, bm, n_out):
    n_blocks = blk_expert.shape[0]
    assert bm % EXPERT_PIECES == 0
    w_map = lambda j, be, nu: (be[j], 0, 0)
    idx_spec = lambda f: pl.BlockSpec((None, 1, bm), lambda j, be, nu: (f(j), 0, 0),
                                      memory_space=pltpu.SMEM)
    any_spec = pl.BlockSpec(memory_space=pl.ANY)
    grid_spec = pltpu.PrefetchScalarGridSpec(
        num_scalar_prefetch=2,
        grid=(n_blocks + 1,),
        in_specs=[
            idx_spec(lambda j: 0),
            idx_spec(lambda j: jnp.minimum(j + 1, n_blocks - 1)),
            idx_spec(lambda j: j),
            any_spec,
            pl.BlockSpec((None, D_MODEL, D_FF), w_map),
            pl.BlockSpec((None, D_MODEL, D_FF), w_map),
            pl.BlockSpec((None, D_FF, D_MODEL), w_map),
        ],
        out_specs=any_spec,
        scratch_shapes=[
            pltpu.VMEM((2, bm * ROW_TILES, LANES), F32),
            pltpu.VMEM((2, bm * ROW_TILES, LANES), F32),
            pltpu.VMEM((bm, D_MODEL), BF16),
            pltpu.VMEM((bm, D_FF), BF16),
            pltpu.VMEM((D_MODEL, D_FF), BF16),
            pltpu.VMEM((D_MODEL, D_FF), BF16),
            pltpu.VMEM((D_FF, D_MODEL), BF16),
            pltpu.SemaphoreType.DMA((2,)),
            pltpu.SemaphoreType.DMA((2,)),
        ],
    )
    tok3 = (tok * ROW_TILES).reshape(n_blocks, 1, bm)
    spare = (n_out - bm + jnp.arange(bm, dtype=jnp.int32)).reshape(1, bm)
    dst_prev = (jnp.concatenate([spare, dst], axis=0) * ROW_TILES).reshape(n_blocks + 1, 1, bm)
    blk_expert = jnp.concatenate([blk_expert, blk_expert[-1:]])
    return pl.pallas_call(
        functools.partial(_expert_mlp_kernel, bm=bm),
        grid_spec=grid_spec,
        out_shape=jax.ShapeDtypeStruct((n_out * ROW_TILES, LANES), F32),
        compiler_params=pltpu.CompilerParams(
            dimension_semantics=("arbitrary",), vmem_limit_bytes=VMEM_LIMIT),
        name="expert_mlp",
    )(blk_expert, n_used, tok3, tok3, dst_prev, x, wg, wu, wd)


PAD_ID = 2 ** 16 - 1


def _moe_plan(e_idx, bm, n_blocks):
    n_tok = e_idx.shape[0]
    n_asg = 2 * n_tok
    n_pad = n_blocks * bm - n_asg
    e_flat = e_idx.reshape(-1)
    experts = jnp.arange(N_EXPERTS, dtype=jnp.int32)
    counts = jnp.sum((e_flat[:, None] == experts[None, :]).astype(jnp.int32), axis=0)
    nblk = (counts + bm - 1) // bm
    blk_end = jnp.cumsum(nblk)
    n_used = blk_end[-1]
    j = jnp.arange(n_blocks, dtype=jnp.int32)
    be = jnp.sum((blk_end[None, :] <= jnp.minimum(j, n_used - 1)[:, None]).astype(jnp.int32), axis=1)
    be = jnp.minimum(be, N_EXPERTS - 1)
    assert n_asg < PAD_ID
    pad_end = jnp.cumsum(nblk * bm - counts)
    pad_expert = jnp.sum((pad_end[None, :] <= jnp.arange(n_pad, dtype=jnp.int32)[:, None]).astype(jnp.int32),
                         axis=1)
    keys = jnp.concatenate([2 * e_flat, 2 * pad_expert + 1])
    ids = jnp.concatenate([jnp.arange(n_asg, dtype=jnp.int32), jnp.full((n_pad,), PAD_ID, jnp.int32)])
    window = lax.sort(keys * (PAD_ID + 1) + ids) % (PAD_ID + 1)
    window = window.reshape(n_blocks, bm)
    valid = window != PAD_ID
    q = jnp.arange(bm, dtype=jnp.int32)
    tok = jnp.where(valid, window // 2, (j[:, None] * bm + q[None, :]) % n_tok)
    dst = jnp.where(valid, (window % 2) * n_tok + window // 2,
                    2 * n_tok + q[None, :])
    return tok.astype(jnp.int32), dst.astype(jnp.int32), be, n_used.astype(jnp.int32).reshape(1)


def _final_kernel(h_ref, p0_ref, p1_ref, route_ref, w_ref, op_ref, os_ref, *, n_prompt_steps):
    route = route_ref[...]
    p0 = _load_row_tiles(p0_ref)
    p1 = _load_row_tiles(p1_ref)
    y = _rms(h_ref[...] + route[:, 2:3] * p0 + route[:, 3:4] * p1, w_ref[...])
    i = pl.program_id(0)

    @pl.when(i < n_prompt_steps)
    def _():
        op_ref[...] = y

    @pl.when(i >= n_prompt_steps)
    def _():
        os_ref[...] = y


def _final(h, parts, route, w, tf, n_prompt):
    n_tok = h.shape[0]
    assert n_tok % tf == 0 and n_prompt % tf == 0
    n_p = n_prompt // tf
    row = lambda d: pl.BlockSpec((tf, d), lambda i: (i, 0))
    part = lambda k: pl.BlockSpec((tf * ROW_TILES, LANES), lambda i: (k * (n_tok // tf) + i, 0))
    return pl.pallas_call(
        functools.partial(_final_kernel, n_prompt_steps=n_p),
        grid=(n_tok // tf,),
        in_specs=[row(D_MODEL), part(0), part(1), row(LANES), _const_spec(w.shape)],
        out_specs=[pl.BlockSpec((tf, D_MODEL), lambda i: (jnp.minimum(i, n_p - 1), 0)),
                   pl.BlockSpec((tf, D_MODEL), lambda i: (jnp.maximum(i - n_p, 0), 0))],
        out_shape=[jax.ShapeDtypeStruct((n_prompt, D_MODEL), F32),
                   jax.ShapeDtypeStruct((n_tok - n_prompt, D_MODEL), F32)],
        compiler_params=pltpu.CompilerParams(
            dimension_semantics=("arbitrary",), vmem_limit_bytes=VMEM_LIMIT),
        name="final_combine",
    )(h, parts, parts, route, w)


PROMPT_TILE = 256
MOE_BLOCK_ROWS = 512
FINAL_TILE = 512
STATE_SEQS = 8


def _pad_lanes(a, width=LANES):
    return jnp.pad(a, [(0, 0)] * (a.ndim - 1) + [(0, width - a.shape[-1])])


def _layer_weights(l, norm_mix_w, w_in, conv_w, conv_b, dt_bias, a_log, d_skip, ssm_norm_w, sg_norm_w,
                   sg_w, sg_b, w_out, norm_ffn_w, w_router_group, b_router_group, w_router_expert,
                   b_router_expert, n_dec):
    c0, c1, c2 = D_SSM, D_SSM + C_CONV, D_SSM + C_CONV + SSM_HEADS
    wi = w_in[l]
    causal = jnp.tril(jnp.ones((CHUNK, CHUNK), bool))
    sgw_l = jnp.where(causal, sg_w[l], 0.0)
    head_of_lane = jnp.arange(D_SSM, dtype=jnp.int32) // SSM_HEAD_DIM
    expand = (jnp.arange(LANES, dtype=jnp.int32)[:, None] == head_of_lane[None, :]).astype(F32)
    row = lambda a: a.reshape(1, -1).astype(F32)
    gap = ROUTER_EXPERT_ROW - N_EXPERT_GROUPS
    rest = LANES - ROUTER_EXPERT_ROW - N_EXPERTS
    wrt = jnp.concatenate([w_router_group[l].T, jnp.zeros((gap, D_MODEL), F32), w_router_expert[l].T,
                           jnp.zeros((rest, D_MODEL), F32)], axis=0).astype(F32)
    brc = jnp.concatenate([b_router_group[l], jnp.zeros((gap,), F32), b_router_expert[l],
                           jnp.zeros((rest,), F32)]).astype(F32).reshape(LANES, 1)
    return {
        'nmw': row(norm_mix_w[l]),
        'wz': wi[:, :c0].astype(BF16),
        'wxbc': wi[:, c0:c1].astype(BF16),
        'wdt': _pad_lanes(wi[:, c1:c2]).astype(BF16),
        'wuv': wi[:, c2:].astype(BF16),
        'convw': conv_w[l].astype(F32),
        'convb': row(conv_b[l]),
        'dtb': _pad_lanes(row(dt_bias[l])),
        'alog': _pad_lanes(row(a_log[l])),
        'dskip': row(jnp.repeat(d_skip[l], SSM_HEAD_DIM)),
        'ssmnw': row(ssm_norm_w[l]),
        'sgnw': row(sg_norm_w[l]),
        'sgw': sgw_l.astype(BF16),
        'sgb': jnp.repeat(sg_b[l].T, SG_HEAD_DIM, axis=1).astype(F32),
        'w4': jnp.repeat(jnp.transpose(sgw_l[:, :n_dec, :n_dec], (1, 2, 0)), SG_HEAD_DIM, axis=2).astype(F32),
        'b4': jnp.repeat(sg_b[l][:, :n_dec].T, SG_HEAD_DIM, axis=1).astype(F32),
        'expand': expand,
        'wout': w_out[l].astype(BF16),
        'nfw': row(norm_ffn_w[l]),
        'wr': jnp.stack([wrt.astype(BF16), (wrt - wrt.astype(BF16).astype(F32)).astype(BF16)]),
        'br': brc,
    }


def _layer(l, xp, xs_slab, state_conv, state_ssm, w, w_gate, w_up, w_down):
    bp, seq, _ = xp.shape
    nb, nt = state_conv.shape[1], xs_slab.shape[0] // state_conv.shape[1]
    q = HEADS_PER_GROUP * SSM_HEAD_DIM

    cprev = jnp.transpose(state_conv[l], (1, 0, 2))
    z, ypart, ea, ysg, v, convn, cmat, bmat, xw, dec = _sample_front(xs_slab, cprev, w, nb, nt)
    to_seq = lambda a, d: jnp.pad(
        jnp.transpose(a.reshape(nt, nb, SSM_GROUPS, d), (1, 2, 0, 3)),
        ((0, 0), (0, 0), (0, SUBLANES - nt), (0, 0)))
    ssm_s, zoff = _sample_state(
        dec[:, :SSM_HEADS], state_ssm[l].reshape(nb, SSM_GROUPS, q, SSM_STATE),
        to_seq(cmat, SSM_STATE), to_seq(bmat, SSM_STATE), to_seq(xw, q), STATE_SEQS)
    zoff = jnp.transpose(zoff[:, :, :nt], (2, 0, 1, 3)).reshape(nt * nb, D_SSM)
    h_s, hn2_s, route_s = _sample_back(xs_slab, ypart, ea, zoff, z, ysg, w)

    h, hn2, route, convt_p, ssm_p = _prompt_mixer(xp, w, PROMPT_TILE, h_s, hn2_s, route_s)

    n_tok = h.shape[0]
    bm = MOE_BLOCK_ROWS
    n_blocks = -(-(2 * n_tok) // bm) + N_EXPERTS
    tok, dst, blk_expert, n_used = _moe_plan(route[:, :2].astype(jnp.int32), bm, n_blocks)
    parts = _expert_mlp(blk_expert, n_used, tok, dst, hn2, w_gate[l], w_up[l], w_down[l], bm,
                        2 * n_tok + 2 * bm)
    outs = dict(
        h=h, parts=parts, route=route,
        conv_p=convt_p[:, SUBLANES - (CONV_K - 1):],
        ssm_p=ssm_p.reshape(bp, SSM_GROUPS, HEADS_PER_GROUP, SSM_HEAD_DIM, SSM_STATE),
        conv_s=jnp.transpose(convn, (1, 0, 2)),
        ssm_s=ssm_s.reshape(nb, SSM_GROUPS, HEADS_PER_GROUP, SSM_HEAD_DIM, SSM_STATE),
        v_s=jnp.transpose(v.reshape(nt, nb, D_SG), (1, 0, 2)),
    )
    return outs


def kernel(x_prompt, x_sample, state_conv, state_ssm, norm_mix_w, w_in, conv_w, conv_b, dt_bias, a_log, d_skip, ssm_norm_w, sg_norm_w, sg_w, sg_b, w_out, norm_ffn_w, w_router_group, b_router_group, w_router_expert, b_router_expert, w_gate, w_up, w_down, norm_final_w):
    depth = w_in.shape[0]
    assert depth == 1, "the fused final norm assumes a single layer"
    bp, seq, _ = x_prompt.shape
    nb, nt, _ = x_sample.shape
    l = 0
    w = _layer_weights(l, norm_mix_w, w_in, conv_w, conv_b, dt_bias, a_log, d_skip, ssm_norm_w, sg_norm_w,
                       sg_w, sg_b, w_out, norm_ffn_w, w_router_group, b_router_group, w_router_expert,
                       b_router_expert, nt)
    xs_slab = jnp.transpose(x_sample, (1, 0, 2)).reshape(nt * nb, D_MODEL)
    outs = _layer(l, x_prompt, xs_slab, state_conv, state_ssm, w, w_gate, w_up, w_down)
    y_p, y_s = _final(outs['h'], outs['parts'], outs['route'], norm_final_w.reshape(1, -1).astype(F32),
                      FINAL_TILE, bp * seq)
    y_prompt = y_p.reshape(bp, seq, D_MODEL)
    y_sample = jnp.transpose(y_s.reshape(nt, nb, D_MODEL), (1, 0, 2))
    return (y_prompt, y_sample, outs['conv_p'][None], outs['ssm_p'][None], outs['conv_s'][None],
            outs['ssm_s'][None], outs['v_s'][None])
```

```python
import functools

import jax
import jax.numpy as jnp
from jax import lax
from jax.experimental import pallas as pl
from jax.experimental.pallas import tpu as pltpu

D_MODEL = 1024
D_SSM = 1024
SSM_HEAD_DIM = 64
SSM_HEADS = 16
SSM_GROUPS = 2
HEADS_PER_GROUP = 8
SSM_STATE = 128
CONV_K = 4
C_CONV = D_SSM + 2 * SSM_GROUPS * SSM_STATE
D_SG = 1024
SG_HEADS = 8
SG_HEAD_DIM = 128
CHUNK = 128
N_EXPERT_GROUPS = 4
EXPERTS_PER_GROUP = 8
N_EXPERTS = 32
D_FF = 512
EPS = 1e-6

LANES = 128
SUBLANES = 8
VMEM_LIMIT = 56 * 1024 * 1024

F32 = jnp.float32
BF16 = jnp.bfloat16
HIGHEST = lax.Precision.HIGHEST


def _dot(a, b):
    return jnp.dot(a, b, preferred_element_type=F32)


def _dot_f32(a, b):
    return jnp.dot(a, b, preferred_element_type=F32, precision=HIGHEST)


def _rms(x, w):
    return x * lax.rsqrt(jnp.mean(x * x, axis=-1, keepdims=True) + EPS) * w


def _silu(x):
    return x * (0.5 + 0.5 * jnp.tanh(0.5 * x))


def _gelu(x):
    return 0.5 * x * (1.0 + lax.erf(x * 0.7071067811865476))


def _softplus(x):
    return jnp.maximum(x, 0.0) + jnp.log1p(jnp.exp(-jnp.abs(x)))


def _gated_group_norm(y, z, w):
    g = y * _silu(z)
    half = D_SSM // SSM_GROUPS
    parts = []
    for k in range(SSM_GROUPS):
        gk = g[:, k * half:(k + 1) * half]
        parts.append(gk * lax.rsqrt(jnp.mean(gk * gk, axis=-1, keepdims=True) + EPS))
    return jnp.concatenate(parts, axis=1) * w


ROUTER_GROUP_ROW = 0
ROUTER_EXPERT_ROW = SUBLANES


def _route_t(lt):
    r = lt.shape[1]
    row = lax.broadcasted_iota(jnp.int32, (SUBLANES, r), 0)
    rowf = row.astype(F32)
    big = float(SUBLANES)
    red = lambda f, a: f(a, axis=0, keepdims=True)
    gl = jnp.where(row < N_EXPERT_GROUPS, lt[ROUTER_GROUP_ROW:ROUTER_GROUP_ROW + SUBLANES], -jnp.inf)
    ge = jnp.exp(gl - red(jnp.max, gl))
    p_grp = ge / red(jnp.sum, ge)
    g_p = red(jnp.max, p_grp)
    g_idx = red(jnp.min, jnp.where(p_grp == g_p, rowf, big))
    el = lt[ROUTER_EXPERT_ROW:ROUTER_EXPERT_ROW + EXPERTS_PER_GROUP]
    for g in range(1, N_EXPERT_GROUPS):
        lo = ROUTER_EXPERT_ROW + g * EXPERTS_PER_GROUP
        el = jnp.where(g_idx == float(g), lt[lo:lo + EXPERTS_PER_GROUP], el)
    ee = jnp.exp(el - red(jnp.max, el))
    pe = ee / red(jnp.sum, ee)
    v1 = red(jnp.max, pe)
    i1 = red(jnp.min, jnp.where(pe == v1, rowf, big))
    rest = rowf != i1
    pe2 = jnp.where(rest, pe, -1.0)
    v2 = red(jnp.max, pe2)
    i2 = red(jnp.min, jnp.where(rest & (pe2 == v2), rowf, big))
    den = v1 + v2
    base = g_idx * float(EXPERTS_PER_GROUP)
    out = jnp.where(row == 0, base + i1, 0.0)
    out = jnp.where(row == 1, base + i2, out)
    out = jnp.where(row == 2, g_p * v1 / den, out)
    out = jnp.where(row == 3, g_p * v2 / den, out)
    return jnp.concatenate([out, jnp.zeros((LANES - SUBLANES, r), F32)], axis=0).T


def _ffn_front(h, nfw, wrt_ref, brc):
    hn2 = _rms(h, nfw)
    hi = hn2.astype(BF16)
    lo = (hn2 - hi.astype(F32)).astype(BF16)
    nt = lambda a, b: lax.dot_general(a, b, (((1,), (1,)), ((), ())), preferred_element_type=F32)
    lt = nt(wrt_ref[0], hi) + nt(wrt_ref[0], lo) + nt(wrt_ref[1], hi) + brc
    return hn2, _route_t(lt)


ROW_TILES = D_MODEL // LANES


def _store_row_tiles(ref, val):
    r = val.shape[0]
    for k in range(ROW_TILES):
        ref[pl.ds(k, r, stride=ROW_TILES), :] = val[:, k * LANES:(k + 1) * LANES]


def _load_row_tiles(ref):
    r = ref.shape[0] // ROW_TILES
    return jnp.concatenate([ref[pl.ds(k, r, stride=ROW_TILES), :] for k in range(ROW_TILES)], axis=1)


def _prompt_step(
        t, nt, fillers, x_ref, nmw_ref, wz_ref, wxbc_ref, wdt_ref, wuv_ref, convw_ref, convb_ref, dtb_ref,
        alog_ref, dskip_ref, ssmnw_ref, sgnw_ref, sgw_ref, sgb_ref, convt_ref, ssm_ref,
        xbc_scr, state_scr, y_scr, mix_scr, tl):
    fillers = list(fillers)
    assert len(fillers) == 3 + tl // CHUNK

    @pl.when(t == 0)
    def _():
        xbc_scr[0:SUBLANES, :] = jnp.zeros((SUBLANES, C_CONV), F32)
        state_scr[...] = jnp.zeros(state_scr.shape, F32)

    x = x_ref[...]
    hn = _rms(x, nmw_ref[...]).astype(BF16)
    z = _dot(hn, wz_ref[...])
    xbc = _dot(hn, wxbc_ref[...])
    dtr = _dot(hn, wdt_ref[...])
    uv = _gelu(_dot(hn, wuv_ref[...]))
    fillers.pop(0)()

    xbc_scr[SUBLANES:SUBLANES + tl, :] = xbc
    conv = convb_ref[...]
    for k in range(CONV_K):
        off = SUBLANES - (CONV_K - 1) + k
        conv = conv + convw_ref[k:k + 1, :] * xbc_scr[off:off + tl, :]
    tail = xbc_scr[tl:tl + SUBLANES, :]
    xbc_scr[0:SUBLANES, :] = tail
    convt_ref[...] = tail
    act = _silu(conv)
    xs = act[:, :D_SSM]
    bmat = act[:, D_SSM:D_SSM + SSM_GROUPS * SSM_STATE]
    cmat = act[:, D_SSM + SSM_GROUPS * SSM_STATE:]
    fillers.pop(0)()

    dt = _softplus(dtr + dtb_ref[...])
    a_row = -jnp.exp(alog_ref[...])

    li = lax.broadcasted_iota(jnp.int32, (CHUNK, CHUNK), 0)
    si = lax.broadcasted_iota(jnp.int32, (CHUNK, CHUNK), 1)
    causal = li >= si
    tri = jnp.where(causal, 1.0, 0.0).astype(F32)
    lo_half = si < SSM_HEAD_DIM

    for c in range(tl // CHUNK):
        rows = slice(c * CHUNK, (c + 1) * CHUNK)
        dt_c = dt[rows]
        acum = _dot_f32(tri, dt_c * a_row)
        acum_t = acum.T
        dt_t = dt_c.T
        w_t = dt_t * jnp.exp(acum_t[:, CHUNK - 1:CHUNK] - acum_t)
        xs_c = xs[rows]
        for g in range(SSM_GROUPS):
            gl = slice(g * SSM_STATE, (g + 1) * SSM_STATE)
            b_g = bmat[rows, gl]
            c_g = cmat[rows, gl]
            b_gt = b_g.T
            cb = _dot(c_g.astype(BF16), b_gt.astype(BF16))
            for j in range(HEADS_PER_GROUP // 2):
                pl_ = slice(g * 512 + j * LANES, g * 512 + (j + 1) * LANES)
                m_l, ec_l, s_l, ea_last = [], [], [], []
                for k in range(2):
                    hh = g * HEADS_PER_GROUP + 2 * j + k
                    colb = jnp.broadcast_to(acum[:, hh:hh + 1], (CHUNK, CHUNK))
                    rowb = jnp.broadcast_to(acum_t[hh:hh + 1, :], (CHUNK, CHUNK))
                    ea = jnp.exp(colb)
                    seg = jnp.where(causal, colb - rowb, 0.0)
                    lmat = jnp.where(causal, jnp.exp(seg), 0.0)
                    m_l.append(cb * lmat * jnp.broadcast_to(dt_t[hh:hh + 1, :], (CHUNK, CHUNK)))
                    ec_l.append(c_g * ea)
                    s_l.append(b_gt * jnp.broadcast_to(w_t[hh:hh + 1, :], (CHUNK, CHUNK)))
                    ea_last.append(ea[CHUNK - 1:CHUNK, :])
                xs_p = xs_c[:, pl_]
                st_p = state_scr[g, :, j * LANES:(j + 1) * LANES]
                rx = jnp.concatenate([jnp.where(lo_half, xs_p, 0.0),
                                      jnp.where(lo_half, 0.0, xs_p)], axis=0).astype(BF16)
                rs = jnp.concatenate([jnp.where(lo_half, st_p, 0.0),
                                      jnp.where(lo_half, 0.0, st_p)], axis=0).astype(BF16)
                lhs = jnp.concatenate(m_l + ec_l, axis=1).astype(BF16)
                y_p = _dot(lhs, jnp.concatenate([rx, rs], axis=0))
                y_scr[rows, pl_] = y_p
                dec = jnp.where(lo_half[0:1, :], ea_last[0], ea_last[1])
                upd = _dot(jnp.concatenate(s_l, axis=1).astype(BF16), rx)
                state_scr[g, :, j * LANES:(j + 1) * LANES] = st_p * dec + upd
        fillers.pop(0)()

    y = y_scr[...] + dskip_ref[...] * xs
    y_ssm = _gated_group_norm(y, z, ssmnw_ref[...])
    mix_scr[:, :D_SSM] = y_ssm.astype(BF16)

    u = uv[:, :D_SG]
    v = _rms(uv[:, D_SG:], sgnw_ref[...])
    vb = v.astype(BF16)
    for c in range(tl // CHUNK):
        rows = slice(c * CHUNK, (c + 1) * CHUNK)
        for hd in range(SG_HEADS):
            hl = slice(hd * SG_HEAD_DIM, (hd + 1) * SG_HEAD_DIM)
            sv = _dot(sgw_ref[hd], vb[rows, hl]) + sgb_ref[:, hl]
            mix_scr[rows, D_SSM + hd * SG_HEAD_DIM:D_SSM + (hd + 1) * SG_HEAD_DIM] = (
                u[rows, hl] * sv).astype(BF16)

    fillers.pop(0)()

    @pl.when(t == nt - 1)
    def _():
        for g in range(SSM_GROUPS):
            ssm_ref[g] = state_scr[g].T


def _prompt_mixer_kernel(
        x_ref, xp_ref, nmw_ref, wz_ref, wxbc_ref, wdt_ref, wuv_ref, convw_ref, convb_ref, dtb_ref,
        alog_ref, dskip_ref, ssmnw_ref, sgnw_ref, sgw_ref, sgb_ref, wout_ref, nfw_ref,
        wr_ref, br_ref, hs_ref, hn2s_ref, routes_ref,
        h_ref, hn2_ref, route_ref, convt_ref, ssm_ref,
        xbc_scr, state_scr, y_scr, mix_scr, h_scr, *, tl, nt, n_main):
    i = pl.program_id(0)

    @pl.when(i == 0)
    def _():
        mix_scr[...] = jnp.zeros(mix_scr.shape, BF16)

    @pl.when(i > n_main)
    def _():
        h_ref[...] = hs_ref[...]
        hn2_ref[...] = hn2s_ref[...]
        route_ref[...] = routes_ref[...]

    n_out_chunks = 2 + tl // CHUNK
    width = D_MODEL // n_out_chunks
    assert width % LANES == 0 and width * n_out_chunks == D_MODEL

    def out_chunk(c):
        cols = slice(c * width, (c + 1) * width)
        h_scr[:, cols] = xp_ref[:, cols] + _dot(mix_scr[...], wout_ref[:, cols])

    def finish():
        h = h_scr[...]
        h_ref[...] = h
        hn2, route = _ffn_front(h, nfw_ref[...], wr_ref, br_ref[...])
        _store_row_tiles(hn2_ref, hn2)
        route_ref[...] = route

    @pl.when(i <= n_main)
    def _():
        def out_chunks(cs):
            for c in cs:
                out_chunk(c)

        half = n_out_chunks // 2
        fillers = ([functools.partial(out_chunks, range(half)),
                    functools.partial(out_chunks, range(half, n_out_chunks)), finish]
                   + [lambda: None] * (tl // CHUNK))
        _prompt_step(
            lax.rem(i, nt), nt, fillers, x_ref, nmw_ref, wz_ref, wxbc_ref, wdt_ref, wuv_ref,
            convw_ref, convb_ref, dtb_ref, alog_ref, dskip_ref, ssmnw_ref, sgnw_ref, sgw_ref, sgb_ref,
            convt_ref, ssm_ref, xbc_scr, state_scr, y_scr, mix_scr, tl)


def _const_spec(shape):
    zeros = (0,) * len(shape)
    return pl.BlockSpec(shape, lambda *_: zeros)


def _prompt_mixer(x, w, tl, h_s, hn2_s, route_s):
    bsz, seq, _ = x.shape
    nt = seq // tl
    n_main = bsz * nt
    n_s = h_s.shape[0]
    assert seq % tl == 0 and n_s % tl == 0
    n_tok = bsz * seq + n_s
    x2 = x.reshape(bsz * seq, D_MODEL)
    main = lambda i: jnp.minimum(i, n_main - 1)
    prev = lambda i: jnp.clip(i - 1, 0, n_main - 1)
    tail = lambda i: jnp.maximum(i - n_main - 1, 0)
    out_row = lambda i: jnp.maximum(i - 1, 0)
    weights = [w['nmw'], w['wz'], w['wxbc'], w['wdt'], w['wuv'], w['convw'], w['convb'], w['dtb'],
               w['alog'], w['dskip'], w['ssmnw'], w['sgnw'], w['sgw'], w['sgb'], w['wout'], w['nfw'],
               w['wr'], w['br']]
    in_specs = ([pl.BlockSpec((tl, D_MODEL), lambda i: (main(i), 0)),
                 pl.BlockSpec((tl, D_MODEL), lambda i: (prev(i), 0))]
                + [_const_spec(a.shape) for a in weights]
                + [pl.BlockSpec(blk, lambda i: (tail(i), 0))
                   for blk in ((tl, D_MODEL), (tl * ROW_TILES, LANES), (tl, LANES))])
    q = HEADS_PER_GROUP * SSM_HEAD_DIM
    out_shape = [
        jax.ShapeDtypeStruct((n_tok, D_MODEL), F32),
        jax.ShapeDtypeStruct((n_tok * ROW_TILES, LANES), F32),
        jax.ShapeDtypeStruct((n_tok, LANES), F32),
        jax.ShapeDtypeStruct((bsz, SUBLANES, C_CONV), F32),
        jax.ShapeDtypeStruct((bsz, SSM_GROUPS, q, SSM_STATE), F32),
    ]
    out_specs = [
        pl.BlockSpec((tl, D_MODEL), lambda i: (out_row(i), 0)),
        pl.BlockSpec((tl * ROW_TILES, LANES), lambda i: (out_row(i), 0)),
        pl.BlockSpec((tl, LANES), lambda i: (out_row(i), 0)),
        pl.BlockSpec((None, SUBLANES, C_CONV), lambda i: (main(i) // nt, 0, 0)),
        pl.BlockSpec((None, SSM_GROUPS, q, SSM_STATE), lambda i: (main(i) // nt, 0, 0, 0)),
    ]
    scratch = [
        pltpu.VMEM((tl + SUBLANES, C_CONV), F32),
        pltpu.VMEM((SSM_GROUPS, SSM_STATE, q), F32),
        pltpu.VMEM((tl, D_SSM), F32),
        pltpu.VMEM((tl, D_SSM + D_SG), BF16),
        pltpu.VMEM((tl, D_MODEL), F32),
    ]
    return pl.pallas_call(
        functools.partial(_prompt_mixer_kernel, tl=tl, nt=nt, n_main=n_main),
        grid=(n_main + 1 + n_s // tl,),
        in_specs=in_specs,
        out_specs=out_specs,
        out_shape=out_shape,
        scratch_shapes=scratch,
        compiler_params=pltpu.CompilerParams(
            dimension_semantics=("arbitrary",), vmem_limit_bytes=VMEM_LIMIT),
        name="prompt_mixer",
    )(x2, x2, *weights, h_s, hn2_s, route_s)


def _sample_front_kernel(
        x_ref, cprev_ref, nmw_ref, wz_ref, wxbc_ref, wdt_ref, wuv_ref, convw_ref, convb_ref,
        dtb_ref, alog_ref, dskip_ref, sgnw_ref, w4_ref, b4_ref, expand_ref,
        z_ref, ypart_ref, ea_ref, ysg_ref, v_ref, convn_ref, c_ref, b_ref, xw_ref, dec_ref,
        *, nb, nt):
    x = x_ref[...]
    hn = _rms(x, nmw_ref[...]).astype(BF16)
    z_ref[...] = _dot(hn, wz_ref[...])
    xbc = _dot(hn, wxbc_ref[...])
    dtr = _dot(hn, wdt_ref[...])
    uv = _gelu(_dot(hn, wuv_ref[...]))

    slab = lambda a, t: a[t * nb:(t + 1) * nb]
    full = [cprev_ref[k] for k in range(CONV_K - 1)] + [slab(xbc, t) for t in range(nt)]
    for k in range(CONV_K - 1):
        convn_ref[k] = full[nt + k]
    xs, bm, cm = [], [], []
    for t in range(nt):
        conv = convb_ref[...]
        for k in range(CONV_K):
            conv = conv + convw_ref[k:k + 1, :] * full[t + k]
        act = _silu(conv)
        xs.append(act[:, :D_SSM])
        bm.append(act[:, D_SSM:D_SSM + SSM_GROUPS * SSM_STATE])
        cm.append(act[:, D_SSM + SSM_GROUPS * SSM_STATE:])
        b_ref[t * nb:(t + 1) * nb, :] = bm[t]
        c_ref[t * nb:(t + 1) * nb, :] = cm[t]

    dt = _softplus(dtr + dtb_ref[...])
    a_row = -jnp.exp(alog_ref[...])
    dts = [slab(dt, t) for t in range(nt)]
    cum = []
    for t in range(nt):
        da = dts[t] * a_row
        cum.append(da if t == 0 else cum[t - 1] + da)
    dec_ref[...] = jnp.exp(cum[nt - 1])

    lane = lax.broadcasted_iota(jnp.int32, (nb, LANES), 1)
    first_group = lane < HEADS_PER_GROUP
    facs = [jnp.exp(cum[t]) for t in range(nt)]
    facs += [dts[s] * jnp.exp(cum[nt - 1] - cum[s]) for s in range(nt)]
    pairs = []
    for t in range(nt):
        for s in range(t + 1):
            cb = []
            for g in range(SSM_GROUPS):
                gl = slice(g * SSM_STATE, (g + 1) * SSM_STATE)
                cb.append(jnp.sum(cm[t][:, gl] * bm[s][:, gl], axis=1, keepdims=True))
            cbh = jnp.where(first_group, cb[0], cb[1])
            facs.append(jnp.exp(cum[t] - cum[s]) * dts[s] * cbh)
            pairs.append((t, s))
    fx = _dot_f32(jnp.concatenate(facs, axis=0), expand_ref[...])
    fslab = lambda i: fx[i * nb:(i + 1) * nb]
    for t in range(nt):
        ea_ref[t * nb:(t + 1) * nb, :] = fslab(t)
        xw_ref[t * nb:(t + 1) * nb, :] = xs[t] * fslab(nt + t)
    for t in range(nt):
        acc = dskip_ref[...] * xs[t]
        for i, (tt, s) in enumerate(pairs):
            if tt == t:
                acc = acc + fslab(2 * nt + i) * xs[s]
        ypart_ref[t * nb:(t + 1) * nb, :] = acc

    u = uv[:, :D_SG]
    v = _rms(uv[:, D_SG:], sgnw_ref[...])
    v_ref[...] = v
    for t in range(nt):
        sv = b4_ref[t:t + 1, :]
        for s in range(t + 1):
            sv = sv + w4_ref[t, s:s + 1, :] * slab(v, s)
        ysg_ref[t * nb:(t + 1) * nb, :] = slab(u, t) * sv


def _sample_front(x_slab, cprev, w, nb, nt):
    r = nb * nt
    weights = [w['nmw'], w['wz'], w['wxbc'], w['wdt'], w['wuv'], w['convw'], w['convb'], w['dtb'],
               w['alog'], w['dskip'], w['sgnw'], w['w4'], w['b4'], w['expand']]
    ins = [x_slab, cprev] + weights
    out_shape = [
        jax.ShapeDtypeStruct((r, D_SSM), F32),
        jax.ShapeDtypeStruct((r, D_SSM), F32),
        jax.ShapeDtypeStruct((r, D_SSM), F32),
        jax.ShapeDtypeStruct((r, D_SG), F32),
        jax.ShapeDtypeStruct((r, D_SG), F32),
        jax.ShapeDtypeStruct((CONV_K - 1, nb, C_CONV), F32),
        jax.ShapeDtypeStruct((r, SSM_GROUPS * SSM_STATE), F32),
        jax.ShapeDtypeStruct((r, SSM_GROUPS * SSM_STATE), F32),
        jax.ShapeDtypeStruct((r, D_SSM), F32),
        jax.ShapeDtypeStruct((nb, LANES), F32),
    ]
    return pl.pallas_call(
        functools.partial(_sample_front_kernel, nb=nb, nt=nt),
        grid=(1,),
        in_specs=[_const_spec(a.shape) for a in ins],
        out_specs=[_const_spec(s.shape) for s in out_shape],
        out_shape=out_shape,
        compiler_params=pltpu.CompilerParams(
            dimension_semantics=("arbitrary",), vmem_limit_bytes=VMEM_LIMIT),
        name="sample_front",
    )(*ins)


def _sample_state_kernel(dec_ref, st_ref, cq_ref, bq_ref, xw_ref, so_ref, z_ref, *, seqs):
    for bb in range(seqs):
        for g in range(SSM_GROUPS):
            s0 = st_ref[bb, g]
            z_ref[bb, g] = lax.dot_general(
                cq_ref[bb, g].astype(BF16), s0.astype(BF16), (((1,), (1,)), ((), ())),
                preferred_element_type=F32)
            upd = lax.dot_general(
                xw_ref[bb, g].astype(BF16), bq_ref[bb, g].astype(BF16), (((0,), (0,)), ((), ())),
                preferred_element_type=F32)
            for hh in range(HEADS_PER_GROUP):
                hs = slice(hh * SSM_HEAD_DIM, (hh + 1) * SSM_HEAD_DIM)
                so_ref[bb, g, hs, :] = s0[hs] * dec_ref[bb, g * HEADS_PER_GROUP + hh] + upd[hs]


def _sample_state(dec, state, cq, bq, xwq, seqs):
    nb = state.shape[0]
    q = HEADS_PER_GROUP * SSM_HEAD_DIM
    blk = lambda *tail: pl.BlockSpec((seqs, SSM_GROUPS) + tail, lambda i: (i, 0, 0, 0))
    return pl.pallas_call(
        functools.partial(_sample_state_kernel, seqs=seqs),
        grid=(nb // seqs,),
        in_specs=[
            pl.BlockSpec((seqs, SSM_HEADS), lambda i: (i, 0), memory_space=pltpu.SMEM),
            blk(q, SSM_STATE), blk(SUBLANES, SSM_STATE), blk(SUBLANES, SSM_STATE), blk(SUBLANES, q),
        ],
        out_specs=[blk(q, SSM_STATE), blk(SUBLANES, q)],
        out_shape=[
            jax.ShapeDtypeStruct((nb, SSM_GROUPS, q, SSM_STATE), F32),
            jax.ShapeDtypeStruct((nb, SSM_GROUPS, SUBLANES, q), F32),
        ],
        compiler_params=pltpu.CompilerParams(
            dimension_semantics=("arbitrary",), vmem_limit_bytes=VMEM_LIMIT),
        name="sample_state",
    )(dec, state, cq, bq, xwq)


def _sample_back_kernel(x_ref, ypart_ref, ea_ref, zoff_ref, z_ref, ysg_ref, ssmnw_ref, wout_ref,
                        nfw_ref, wr_ref, br_ref, h_ref, hn2_ref, route_ref):
    y = ypart_ref[...] + ea_ref[...] * zoff_ref[...]
    y_ssm = _gated_group_norm(y, z_ref[...], ssmnw_ref[...])
    mix_in = jnp.concatenate([y_ssm.astype(BF16), ysg_ref[...].astype(BF16)], axis=1)
    h = x_ref[...] + _dot(mix_in, wout_ref[...])
    h_ref[...] = h
    hn2, route = _ffn_front(h, nfw_ref[...], wr_ref, br_ref[...])
    _store_row_tiles(hn2_ref, hn2)
    route_ref[...] = route


def _sample_back(x_slab, ypart, ea, zoff, z, ysg, w):
    r = x_slab.shape[0]
    ins = [x_slab, ypart, ea, zoff, z, ysg, w['ssmnw'], w['wout'], w['nfw'], w['wr'], w['br']]
    out_shape = [
        jax.ShapeDtypeStruct((r, D_MODEL), F32),
        jax.ShapeDtypeStruct((r * ROW_TILES, LANES), F32),
        jax.ShapeDtypeStruct((r, LANES), F32),
    ]
    return pl.pallas_call(
        _sample_back_kernel,
        grid=(1,),
        in_specs=[_const_spec(a.shape) for a in ins],
        out_specs=[_const_spec(s.shape) for s in out_shape],
        out_shape=out_shape,
        compiler_params=pltpu.CompilerParams(
            dimension_semantics=("arbitrary",), vmem_limit_bytes=VMEM_LIMIT),
        name="sample_back",
    )(*ins)


EXPERT_PIECES = 8
DMA_PIECES = 4


def _expert_mlp_kernel(be_ref, nu_ref, tok0_ref, tokn_ref, dstp_ref, x_hbm, wg_ref, wu_ref, wd_ref,
                       parts_hbm, xbuf, obuf, xb, hb, wg_b, wu_b, wd_b, gsem, ssem, *, bm):
    j = pl.program_id(0)
    nu = nu_ref[0]
    nt = ROW_TILES
    tile = lambda ref, start: ref.at[pl.ds(pl.multiple_of(start, nt), nt)]

    def gather_copy(idx_ref, s, r):
        return pltpu.make_async_copy(tile(x_hbm, idx_ref[0, r]), xbuf.at[s, pl.ds(r * nt, nt)], gsem.at[s])

    def scatter_copy(s, r):
        return pltpu.make_async_copy(obuf.at[s, pl.ds(r * nt, nt)], tile(parts_hbm, dstp_ref[0, r]),
                                     ssem.at[s])

    def gather_wait(s):
        pltpu.make_async_copy(x_hbm.at[pl.ds(0, bm * nt)], xbuf.at[s], gsem.at[s]).wait()

    def scatter_wait(s):
        pltpu.make_async_copy(obuf.at[s], parts_hbm.at[pl.ds(0, bm * nt)], ssem.at[s]).wait()

    def step(slot):
        gather_wait(slot)
        per = bm // DMA_PIECES

        def start_rows(piece):
            if piece >= DMA_PIECES:
                return
            for r in range(piece * per, (piece + 1) * per):
                gather_copy(tokn_ref, 1 - slot, r).start(priority=r % 2)
                scatter_copy(1 - slot, r).start(priority=(r + 1) % 2)

        xb[...] = _load_row_tiles(xbuf.at[slot]).astype(BF16)
        half = D_FF // 2
        for c in range(2):
            cols = slice(c * half, (c + 1) * half)
            start_rows(2 * c)
            g = _dot(xb[...], wg_b[:, cols])
            start_rows(2 * c + 1)
            u = _dot(xb[...], wu_b[:, cols])
            hb[:, cols] = (_silu(g) * u).astype(BF16)
        quarter = D_MODEL // 4
        scatter_wait(slot)
        for c in range(4):
            start_rows(4 + c)
            o = _dot(hb[...], wd_b[:, c * quarter:(c + 1) * quarter])
            for k in range(quarter // LANES):
                kk = c * (quarter // LANES) + k
                obuf.at[slot][pl.ds(kk, bm, stride=nt), :] = o[:, k * LANES:(k + 1) * LANES]

    def drain(slot):
        gather_wait(slot)
        scatter_wait(slot)
        for r in range(bm):
            scatter_copy(1 - slot, r).start(priority=r % 2)
        scatter_wait(1 - slot)

    @pl.when(j == 0)
    def _():
        obuf[...] = jnp.zeros(obuf.shape, F32)
        n_real = parts_hbm.shape[0] - 2 * bm * nt
        pltpu.make_async_copy(obuf.at[0], parts_hbm.at[pl.ds(n_real, bm * nt)], ssem.at[0]).start()

        def start0(r, carry):
            pltpu.make_async_copy(tile(x_hbm, tok0_ref[0, r]), xbuf.at[0, pl.ds(pl.multiple_of(r * nt, nt), nt)],
                                  gsem.at[0]).start()
            return carry
        lax.fori_loop(0, bm, start0, 0)

    used = j < nu

    @pl.when(used & ((j == 0) | (be_ref[j] != be_ref[jnp.maximum(j - 1, 0)])))
    def _():
        wg_b[...] = wg_ref[...].astype(BF16)
        wu_b[...] = wu_ref[...].astype(BF16)
        wd_b[...] = wd_ref[...].astype(BF16)

    for slot in range(2):
        parity = lax.rem(j, 2) == slot

        @pl.when(used & parity)
        def _(slot=slot):
            step(slot)

        @pl.when((j == nu) & parity)
        def _(slot=slot):
            drain(slot)


def _expert_mlp(blk_expert, n_used, tok, dst, x, wg, wu, wd, bm, n_out):
    n_blocks = blk_expert.shape[0]
    assert bm % EXPERT_PIECES == 0
    w_map = lambda j, be, nu: (be[j], 0, 0)
    idx_spec = lambda f: pl.BlockSpec((None, 1, bm), lambda j, be, nu: (f(j), 0, 0),
                                      memory_space=pltpu.SMEM)
    any_spec = pl.BlockSpec(memory_space=pl.ANY)
    grid_spec = pltpu.PrefetchScalarGridSpec(
        num_scalar_prefetch=2,
        grid=(n_blocks + 1,),
        in_specs=[
            idx_spec(lambda j: 0),
            idx_spec(lambda j: jnp.minimum(j + 1, n_blocks - 1)),
            idx_spec(lambda j: j),
            any_spec,
            pl.BlockSpec((None, D_MODEL, D_FF), w_map),
            pl.BlockSpec((None, D_MODEL, D_FF), w_map),
            pl.BlockSpec((None, D_FF, D_MODEL), w_map),
        ],
        out_specs=any_spec,
        scratch_shapes=[
            pltpu.VMEM((2, bm * ROW_TILES, LANES), F32),
            pltpu.VMEM((2, bm * ROW_TILES, LANES), F32),
            pltpu.VMEM((bm, D_MODEL), BF16),
            pltpu.VMEM((bm, D_FF), BF16),
            pltpu.VMEM((D_MODEL, D_FF), BF16),
            pltpu.VMEM((D_MODEL, D_FF), BF16),
            pltpu.VMEM((D_FF, D_MODEL), BF16),
            pltpu.SemaphoreType.DMA((2,)),
            pltpu.SemaphoreType.DMA((2,)),
        ],
    )
    tok3 = (tok * ROW_TILES).reshape(n_blocks, 1, bm)
    spare = (n_out - bm + jnp.arange(bm, dtype=jnp.int32)).reshape(1, bm)
    dst_prev = (jnp.concatenate([spare, dst], axis=0) * ROW_TILES).reshape(n_blocks + 1, 1, bm)
    blk_expert = jnp.concatenate([blk_expert, blk_expert[-1:]])
    return pl.pallas_call(
        functools.partial(_expert_mlp_kernel, bm=bm),
        grid_spec=grid_spec,
        out_shape=jax.ShapeDtypeStruct((n_out * ROW_TILES, LANES), F32),
        compiler_params=pltpu.CompilerParams(
            dimension_semantics=("arbitrary",), vmem_limit_bytes=VMEM_LIMIT),
        name="expert_mlp",
    )(blk_expert, n_used, tok3, tok3, dst_prev, x, wg, wu, wd)


PAD_ID = 2 ** 16 - 1


def _moe_plan(e_idx, bm, n_blocks):
    n_tok = e_idx.shape[0]
    n_asg = 2 * n_tok
    n_pad = n_blocks * bm - n_asg
    e_flat = e_idx.reshape(-1)
    experts = jnp.arange(N_EXPERTS, dtype=jnp.int32)
    counts = jnp.sum((e_flat[:, None] == experts[None, :]).astype(jnp.int32), axis=0)
    nblk = (counts + bm - 1) // bm
    blk_end = jnp.cumsum(nblk)
    n_used = blk_end[-1]
    j = jnp.arange(n_blocks, dtype=jnp.int32)
    be = jnp.sum((blk_end[None, :] <= jnp.minimum(j, n_used - 1)[:, None]).astype(jnp.int32), axis=1)
    be = jnp.minimum(be, N_EXPERTS - 1)
    assert n_asg < PAD_ID
    pad_end = jnp.cumsum(nblk * bm - counts)
    pad_expert = jnp.sum((pad_end[None, :] <= jnp.arange(n_pad, dtype=jnp.int32)[:, None]).astype(jnp.int32),
                         axis=1)
    keys = jnp.concatenate([2 * e_flat, 2 * pad_expert + 1])
    ids = jnp.concatenate([jnp.arange(n_asg, dtype=jnp.int32), jnp.full((n_pad,), PAD_ID, jnp.int32)])
    window = lax.sort(keys * (PAD_ID + 1) + ids) % (PAD_ID + 1)
    window = window.reshape(n_blocks, bm)
    valid = window != PAD_ID
    q = jnp.arange(bm, dtype=jnp.int32)
    tok = jnp.where(valid, window // 2, (j[:, None] * bm + q[None, :]) % n_tok)
    dst = jnp.where(valid, (window % 2) * n_tok + window // 2,
                    2 * n_tok + q[None, :])
    return tok.astype(jnp.int32), dst.astype(jnp.int32), be, n_used.astype(jnp.int32).reshape(1)


def _final_kernel(h_ref, p0_ref, p1_ref, route_ref, w_ref, op_ref, os_ref, *, n_prompt_steps):
    route = route_ref[...]
    p0 = _load_row_tiles(p0_ref)
    p1 = _load_row_tiles(p1_ref)
    y = _rms(h_ref[...] + route[:, 2:3] * p0 + route[:, 3:4] * p1, w_ref[...])
    i = pl.program_id(0)

    @pl.when(i < n_prompt_steps)
    def _():
        op_ref[...] = y

    @pl.when(i >= n_prompt_steps)
    def _():
        os_ref[...] = y


def _final(h, parts, route, w, tf, n_prompt):
    n_tok = h.shape[0]
    assert n_tok % tf == 0 and n_prompt % tf == 0
    n_p = n_prompt // tf
    row = lambda d: pl.BlockSpec((tf, d), lambda i: (i, 0))
    part = lambda k: pl.BlockSpec((tf * ROW_TILES, LANES), lambda i: (k * (n_tok // tf) + i, 0))
    return pl.pallas_call(
        functools.partial(_final_kernel, n_prompt_steps=n_p),
        grid=(n_tok // tf,),
        in_specs=[row(D_MODEL), part(0), part(1), row(LANES), _const_spec(w.shape)],
        out_specs=[pl.BlockSpec((tf, D_MODEL), lambda i: (jnp.minimum(i, n_p - 1), 0)),
                   pl.BlockSpec((tf, D_MODEL), lambda i: (jnp.maximum(i - n_p, 0), 0))],
        out_shape=[jax.ShapeDtypeStruct((n_prompt, D_MODEL), F32),
                   jax.ShapeDtypeStruct((n_tok - n_prompt, D_MODEL), F32)],
        compiler_params=pltpu.CompilerParams(
            dimension_semantics=("arbitrary",), vmem_limit_bytes=VMEM_LIMIT),
        name="final_combine",
    )(h, parts, parts, route, w)


PROMPT_TILE = 256
MOE_BLOCK_ROWS = 768
FINAL_TILE = 512
STATE_SEQS = 8


def _pad_lanes(a, width=LANES):
    return jnp.pad(a, [(0, 0)] * (a.ndim - 1) + [(0, width - a.shape[-1])])


def _layer_weights(l, norm_mix_w, w_in, conv_w, conv_b, dt_bias, a_log, d_skip, ssm_norm_w, sg_norm_w,
                   sg_w, sg_b, w_out, norm_ffn_w, w_router_group, b_router_group, w_router_expert,
                   b_router_expert, n_dec):
    c0, c1, c2 = D_SSM, D_SSM + C_CONV, D_SSM + C_CONV + SSM_HEADS
    wi = w_in[l]
    causal = jnp.tril(jnp.ones((CHUNK, CHUNK), bool))
    sgw_l = jnp.where(causal, sg_w[l], 0.0)
    head_of_lane = jnp.arange(D_SSM, dtype=jnp.int32) // SSM_HEAD_DIM
    expand = (jnp.arange(LANES, dtype=jnp.int32)[:, None] == head_of_lane[None, :]).astype(F32)
    row = lambda a: a.reshape(1, -1).astype(F32)
    gap = ROUTER_EXPERT_ROW - N_EXPERT_GROUPS
    rest = LANES - ROUTER_EXPERT_ROW - N_EXPERTS
    wrt = jnp.concatenate([w_router_group[l].T, jnp.zeros((gap, D_MODEL), F32), w_router_expert[l].T,
                           jnp.zeros((rest, D_MODEL), F32)], axis=0).astype(F32)
    brc = jnp.concatenate([b_router_group[l], jnp.zeros((gap,), F32), b_router_expert[l],
                           jnp.zeros((rest,), F32)]).astype(F32).reshape(LANES, 1)
    return {
        'nmw': row(norm_mix_w[l]),
        'wz': wi[:, :c0].astype(BF16),
        'wxbc': wi[:, c0:c1].astype(BF16),
        'wdt': _pad_lanes(wi[:, c1:c2]).astype(BF16),
        'wuv': wi[:, c2:].astype(BF16),
        'convw': conv_w[l].astype(F32),
        'convb': row(conv_b[l]),
        'dtb': _pad_lanes(row(dt_bias[l])),
        'alog': _pad_lanes(row(a_log[l])),
        'dskip': row(jnp.repeat(d_skip[l], SSM_HEAD_DIM)),
        'ssmnw': row(ssm_norm_w[l]),
        'sgnw': row(sg_norm_w[l]),
        'sgw': sgw_l.astype(BF16),
        'sgb': jnp.repeat(sg_b[l].T, SG_HEAD_DIM, axis=1).astype(F32),
        'w4': jnp.repeat(jnp.transpose(sgw_l[:, :n_dec, :n_dec], (1, 2, 0)), SG_HEAD_DIM, axis=2).astype(F32),
        'b4': jnp.repeat(sg_b[l][:, :n_dec].T, SG_HEAD_DIM, axis=1).astype(F32),
        'expand': expand,
        'wout': w_out[l].astype(BF16),
        'nfw': row(norm_ffn_w[l]),
        'wr': jnp.stack([wrt.astype(BF16), (wrt - wrt.astype(BF16).astype(F32)).astype(BF16)]),
        'br': brc,
    }


def _layer(l, xp, xs_slab, state_conv, state_ssm, w, w_gate, w_up, w_down):
    bp, seq, _ = xp.shape
    nb, nt = state_conv.shape[1], xs_slab.shape[0] // state_conv.shape[1]
    q = HEADS_PER_GROUP * SSM_HEAD_DIM

    cprev = jnp.transpose(state_conv[l], (1, 0, 2))
    z, ypart, ea, ysg, v, convn, cmat, bmat, xw, dec = _sample_front(xs_slab, cprev, w, nb, nt)
    to_seq = lambda a, d: jnp.pad(
        jnp.transpose(a.reshape(nt, nb, SSM_GROUPS, d), (1, 2, 0, 3)),
        ((0, 0), (0, 0), (0, SUBLANES - nt), (0, 0)))
    ssm_s, zoff = _sample_state(
        dec[:, :SSM_HEADS], state_ssm[l].reshape(nb, SSM_GROUPS, q, SSM_STATE),
        to_seq(cmat, SSM_STATE), to_seq(bmat, SSM_STATE), to_seq(xw, q), STATE_SEQS)
    zoff = jnp.transpose(zoff[:, :, :nt], (2, 0, 1, 3)).reshape(nt * nb, D_SSM)
    h_s, hn2_s, route_s = _sample_back(xs_slab, ypart, ea, zoff, z, ysg, w)

    h, hn2, route, convt_p, ssm_p = _prompt_mixer(xp, w, PROMPT_TILE, h_s, hn2_s, route_s)

    n_tok = h.shape[0]
    bm = MOE_BLOCK_ROWS
    n_blocks = -(-(2 * n_tok) // bm) + N_EXPERTS
    tok, dst, blk_expert, n_used = _moe_plan(route[:, :2].astype(jnp.int32), bm, n_blocks)
    parts = _expert_mlp(blk_expert, n_used, tok, dst, hn2, w_gate[l], w_up[l], w_down[l], bm,
                        2 * n_tok + 2 * bm)
    outs = dict(
        h=h, parts=parts, route=route,
        conv_p=convt_p[:, SUBLANES - (CONV_K - 1):],
        ssm_p=ssm_p.reshape(bp, SSM_GROUPS, HEADS_PER_GROUP, SSM_HEAD_DIM, SSM_STATE),
        conv_s=jnp.transpose(convn, (1, 0, 2)),
        ssm_s=ssm_s.reshape(nb, SSM_GROUPS, HEADS_PER_GROUP, SSM_HEAD_DIM, SSM_STATE),
        v_s=jnp.transpose(v.reshape(nt, nb, D_SG), (1, 0, 2)),
    )
    return outs


def kernel(x_prompt, x_sample, state_conv, state_ssm, norm_mix_w, w_in, conv_w, conv_b, dt_bias, a_log, d_skip, ssm_norm_w, sg_norm_w, sg_w, sg_b, w_out, norm_ffn_w, w_router_group, b_router_group, w_router_expert, b_router_expert, w_gate, w_up, w_down, norm_final_w):
    depth = w_in.shape[0]
    assert depth == 1, "the fused final norm assumes a single layer"
    bp, seq, _ = x_prompt.shape
    nb, nt, _ = x_sample.shape
    l = 0
    w = _layer_weights(l, norm_mix_w, w_in, conv_w, conv_b, dt_bias, a_log, d_skip, ssm_norm_w, sg_norm_w,
                       sg_w, sg_b, w_out, norm_ffn_w, w_router_group, b_router_group, w_router_expert,
                       b_router_expert, nt)
    xs_slab = jnp.transpose(x_sample, (1, 0, 2)).reshape(nt * nb, D_MODEL)
    outs = _layer(l, x_prompt, xs_slab, state_conv, state_ssm, w, w_gate, w_up, w_down)
    y_p, y_s = _final(outs['h'], outs['parts'], outs['route'], norm_final_w.reshape(1, -1).astype(F32),
                      FINAL_TILE, bp * seq)
    y_prompt = y_p.reshape(bp, seq, D_MODEL)
    y_sample = jnp.transpose(y_s.reshape(nt, nb, D_MODEL), (1, 0, 2))
    return (y_prompt, y_sample, outs['conv_p'][None], outs['ssm_p'][None], outs['conv_s'][None],
            outs['ssm_s'][None], outs['v_s'][None])
```

```python
import functools

import jax
import jax.numpy as jnp
from jax import lax
from jax.experimental import pallas as pl
from jax.experimental.pallas import tpu as pltpu

D_MODEL = 1024
D_SSM = 1024
SSM_HEAD_DIM = 64
SSM_HEADS = 16
SSM_GROUPS = 2
HEADS_PER_GROUP = 8
SSM_STATE = 128
CONV_K = 4
C_CONV = D_SSM + 2 * SSM_GROUPS * SSM_STATE
D_SG = 1024
SG_HEADS = 8
SG_HEAD_DIM = 128
CHUNK = 128
N_EXPERT_GROUPS = 4
EXPERTS_PER_GROUP = 8
N_EXPERTS = 32
D_FF = 512
EPS = 1e-6

LANES = 128
SUBLANES = 8
VMEM_LIMIT = 56 * 1024 * 1024

F32 = jnp.float32
BF16 = jnp.bfloat16
HIGHEST = lax.Precision.HIGHEST


def _dot(a, b):
    return jnp.dot(a, b, preferred_element_type=F32)


def _dot_f32(a, b):
    return jnp.dot(a, b, preferred_element_type=F32, precision=HIGHEST)


def _rms(x, w):
    return x * lax.rsqrt(jnp.mean(x * x, axis=-1, keepdims=True) + EPS) * w


def _silu(x):
    return x * (0.5 + 0.5 * jnp.tanh(0.5 * x))


def _gelu(x):
    return 0.5 * x * (1.0 + lax.erf(x * 0.7071067811865476))


def _softplus(x):
    return jnp.maximum(x, 0.0) + jnp.log1p(jnp.exp(-jnp.abs(x)))


def _gated_group_norm(y, z, w):
    g = y * _silu(z)
    half = D_SSM // SSM_GROUPS
    parts = []
    for k in range(SSM_GROUPS):
        gk = g[:, k * half:(k + 1) * half]
        parts.append(gk * lax.rsqrt(jnp.mean(gk * gk, axis=-1, keepdims=True) + EPS))
    return jnp.concatenate(parts, axis=1) * w


ROUTER_GROUP_ROW = 0
ROUTER_EXPERT_ROW = SUBLANES


def _route_t(lt):
    r = lt.shape[1]
    row = lax.broadcasted_iota(jnp.int32, (SUBLANES, r), 0)
    rowf = row.astype(F32)
    big = float(SUBLANES)
    red = lambda f, a: f(a, axis=0, keepdims=True)
    gl = jnp.where(row < N_EXPERT_GROUPS, lt[ROUTER_GROUP_ROW:ROUTER_GROUP_ROW + SUBLANES], -jnp.inf)
    ge = jnp.exp(gl - red(jnp.max, gl))
    p_grp = ge / red(jnp.sum, ge)
    g_p = red(jnp.max, p_grp)
    g_idx = red(jnp.min, jnp.where(p_grp == g_p, rowf, big))
    el = lt[ROUTER_EXPERT_ROW:ROUTER_EXPERT_ROW + EXPERTS_PER_GROUP]
    for g in range(1, N_EXPERT_GROUPS):
        lo = ROUTER_EXPERT_ROW + g * EXPERTS_PER_GROUP
        el = jnp.where(g_idx == float(g), lt[lo:lo + EXPERTS_PER_GROUP], el)
    ee = jnp.exp(el - red(jnp.max, el))
    pe = ee / red(jnp.sum, ee)
    v1 = red(jnp.max, pe)
    i1 = red(jnp.min, jnp.where(pe == v1, rowf, big))
    rest = rowf != i1
    pe2 = jnp.where(rest, pe, -1.0)
    v2 = red(jnp.max, pe2)
    i2 = red(jnp.min, jnp.where(rest & (pe2 == v2), rowf, big))
    den = v1 + v2
    base = g_idx * float(EXPERTS_PER_GROUP)
    out = jnp.where(row == 0, base + i1, 0.0)
    out = jnp.where(row == 1, base + i2, out)
    out = jnp.where(row == 2, g_p * v1 / den, out)
    out = jnp.where(row == 3, g_p * v2 / den, out)
    return jnp.concatenate([out, jnp.zeros((LANES - SUBLANES, r), F32)], axis=0).T


def _ffn_front(h, nfw, wrt_ref, brc):
    hn2 = _rms(h, nfw)
    hi = hn2.astype(BF16)
    lo = (hn2 - hi.astype(F32)).astype(BF16)
    nt = lambda a, b: lax.dot_general(a, b, (((1,), (1,)), ((), ())), preferred_element_type=F32)
    lt = nt(wrt_ref[0], hi) + nt(wrt_ref[0], lo) + nt(wrt_ref[1], hi) + brc
    return hn2, _route_t(lt)


ROW_TILES = D_MODEL // LANES


def _store_row_tiles(ref, val):
    r = val.shape[0]
    for k in range(ROW_TILES):
        ref[pl.ds(k, r, stride=ROW_TILES), :] = val[:, k * LANES:(k + 1) * LANES]


def _load_row_tiles(ref):
    r = ref.shape[0] // ROW_TILES
    return jnp.concatenate([ref[pl.ds(k, r, stride=ROW_TILES), :] for k in range(ROW_TILES)], axis=1)


def _prompt_step(
        t, nt, fillers, x_ref, nmw_ref, wz_ref, wxbc_ref, wdt_ref, wuv_ref, convw_ref, convb_ref, dtb_ref,
        alog_ref, dskip_ref, ssmnw_ref, sgnw_ref, sgw_ref, sgb_ref, convt_ref, ssm_ref,
        xbc_scr, state_scr, y_scr, mix_scr, tl):
    fillers = list(fillers)
    assert len(fillers) == 3 + tl // CHUNK

    @pl.when(t == 0)
    def _():
        xbc_scr[0:SUBLANES, :] = jnp.zeros((SUBLANES, C_CONV), F32)
        state_scr[...] = jnp.zeros(state_scr.shape, F32)

    x = x_ref[...]
    hn = _rms(x, nmw_ref[...]).astype(BF16)
    z = _dot(hn, wz_ref[...])
    xbc = _dot(hn, wxbc_ref[...])
    dtr = _dot(hn, wdt_ref[...])
    uv = _gelu(_dot(hn, wuv_ref[...]))
    fillers.pop(0)()

    xbc_scr[SUBLANES:SUBLANES + tl, :] = xbc
    conv = convb_ref[...]
    for k in range(CONV_K):
        off = SUBLANES - (CONV_K - 1) + k
        conv = conv + convw_ref[k:k + 1, :] * xbc_scr[off:off + tl, :]
    tail = xbc_scr[tl:tl + SUBLANES, :]
    xbc_scr[0:SUBLANES, :] = tail
    convt_ref[...] = tail
    act = _silu(conv)
    xs = act[:, :D_SSM]
    bmat = act[:, D_SSM:D_SSM + SSM_GROUPS * SSM_STATE]
    cmat = act[:, D_SSM + SSM_GROUPS * SSM_STATE:]
    fillers.pop(0)()

    dt = _softplus(dtr + dtb_ref[...])
    a_row = -jnp.exp(alog_ref[...])

    li = lax.broadcasted_iota(jnp.int32, (CHUNK, CHUNK), 0)
    si = lax.broadcasted_iota(jnp.int32, (CHUNK, CHUNK), 1)
    causal = li >= si
    tri = jnp.where(causal, 1.0, 0.0).astype(F32)
    lo_half = si < SSM_HEAD_DIM

    for c in range(tl // CHUNK):
        rows = slice(c * CHUNK, (c + 1) * CHUNK)
        dt_c = dt[rows]
        acum = _dot_f32(tri, dt_c * a_row)
        acum_t = acum.T
        dt_t = dt_c.T
        w_t = dt_t * jnp.exp(acum_t[:, CHUNK - 1:CHUNK] - acum_t)
        xs_c = xs[rows]
        for g in range(SSM_GROUPS):
            gl = slice(g * SSM_STATE, (g + 1) * SSM_STATE)
            b_g = bmat[rows, gl]
            c_g = cmat[rows, gl]
            b_gt = b_g.T
            cb = _dot(c_g.astype(BF16), b_gt.astype(BF16))
            for j in range(HEADS_PER_GROUP // 2):
                pl_ = slice(g * 512 + j * LANES, g * 512 + (j + 1) * LANES)
                m_l, ec_l, s_l, ea_last = [], [], [], []
                for k in range(2):
                    hh = g * HEADS_PER_GROUP + 2 * j + k
                    colb = jnp.broadcast_to(acum[:, hh:hh + 1], (CHUNK, CHUNK))
                    rowb = jnp.broadcast_to(acum_t[hh:hh + 1, :], (CHUNK, CHUNK))
                    ea = jnp.exp(colb)
                    seg = jnp.where(causal, colb - rowb, 0.0)
                    lmat = jnp.where(causal, jnp.exp(seg), 0.0)
                    m_l.append(cb * lmat * jnp.broadcast_to(dt_t[hh:hh + 1, :], (CHUNK, CHUNK)))
                    ec_l.append(c_g * ea)
                    s_l.append(b_gt * jnp.broadcast_to(w_t[hh:hh + 1, :], (CHUNK, CHUNK)))
                    ea_last.append(ea[CHUNK - 1:CHUNK, :])
                xs_p = xs_c[:, pl_]
                st_p = state_scr[g, :, j * LANES:(j + 1) * LANES]
                rx = jnp.concatenate([jnp.where(lo_half, xs_p, 0.0),
                                      jnp.where(lo_half, 0.0, xs_p)], axis=0).astype(BF16)
                rs = jnp.concatenate([jnp.where(lo_half, st_p, 0.0),
                                      jnp.where(lo_half, 0.0, st_p)], axis=0).astype(BF16)
                lhs = jnp.concatenate(m_l + ec_l, axis=1).astype(BF16)
                y_p = _dot(lhs, jnp.concatenate([rx, rs], axis=0))
                y_scr[rows, pl_] = y_p
                dec = jnp.where(lo_half[0:1, :], ea_last[0], ea_last[1])
                upd = _dot(jnp.concatenate(s_l, axis=1).astype(BF16), rx)
                state_scr[g, :, j * LANES:(j + 1) * LANES] = st_p * dec + upd
        fillers.pop(0)()

    y = y_scr[...] + dskip_ref[...] * xs
    y_ssm = _gated_group_norm(y, z, ssmnw_ref[...])
    mix_scr[:, :D_SSM] = y_ssm.astype(BF16)

    u = uv[:, :D_SG]
    v = _rms(uv[:, D_SG:], sgnw_ref[...])
    vb = v.astype(BF16)
    for c in range(tl // CHUNK):
        rows = slice(c * CHUNK, (c + 1) * CHUNK)
        for hd in range(SG_HEADS):
            hl = slice(hd * SG_HEAD_DIM, (hd + 1) * SG_HEAD_DIM)
            sv = _dot(sgw_ref[hd], vb[rows, hl]) + sgb_ref[:, hl]
            mix_scr[rows, D_SSM + hd * SG_HEAD_DIM:D_SSM + (hd + 1) * SG_HEAD_DIM] = (
                u[rows, hl] * sv).astype(BF16)

    fillers.pop(0)()

    @pl.when(t == nt - 1)
    def _():
        for g in range(SSM_GROUPS):
            ssm_ref[g] = state_scr[g].T


def _prompt_mixer_kernel(
        x_ref, xp_ref, nmw_ref, wz_ref, wxbc_ref, wdt_ref, wuv_ref, convw_ref, convb_ref, dtb_ref,
        alog_ref, dskip_ref, ssmnw_ref, sgnw_ref, sgw_ref, sgb_ref, wout_ref, nfw_ref,
        wr_ref, br_ref, hs_ref, hn2s_ref, routes_ref,
        h_ref, hn2_ref, route_ref, convt_ref, ssm_ref,
        xbc_scr, state_scr, y_scr, mix_scr, h_scr, *, tl, nt, n_main):
    i = pl.program_id(0)

    @pl.when(i == 0)
    def _():
        mix_scr[...] = jnp.zeros(mix_scr.shape, BF16)

    @pl.when(i > n_main)
    def _():
        h_ref[...] = hs_ref[...]
        hn2_ref[...] = hn2s_ref[...]
        route_ref[...] = routes_ref[...]

    n_out_chunks = 2 + tl // CHUNK
    width = D_MODEL // n_out_chunks
    assert width % LANES == 0 and width * n_out_chunks == D_MODEL

    def out_chunk(c):
        cols = slice(c * width, (c + 1) * width)
        h_scr[:, cols] = xp_ref[:, cols] + _dot(mix_scr[...], wout_ref[:, cols])

    def finish():
        h = h_scr[...]
        h_ref[...] = h
        hn2, route = _ffn_front(h, nfw_ref[...], wr_ref, br_ref[...])
        _store_row_tiles(hn2_ref, hn2)
        route_ref[...] = route

    @pl.when(i <= n_main)
    def _():
        def out_chunks(cs):
            for c in cs:
                out_chunk(c)

        half = n_out_chunks // 2
        fillers = ([functools.partial(out_chunks, range(half)),
                    functools.partial(out_chunks, range(half, n_out_chunks)), finish]
                   + [lambda: None] * (tl // CHUNK))
        _prompt_step(
            lax.rem(i, nt), nt, fillers, x_ref, nmw_ref, wz_ref, wxbc_ref, wdt_ref, wuv_ref,
            convw_ref, convb_ref, dtb_ref, alog_ref, dskip_ref, ssmnw_ref, sgnw_ref, sgw_ref, sgb_ref,
            convt_ref, ssm_ref, xbc_scr, state_scr, y_scr, mix_scr, tl)


def _const_spec(shape):
    zeros = (0,) * len(shape)
    return pl.BlockSpec(shape, lambda *_: zeros)


def _prompt_mixer(x, w, tl, h_s, hn2_s, route_s):
    bsz, seq, _ = x.shape
    nt = seq // tl
    n_main = bsz * nt
    n_s = h_s.shape[0]
    assert seq % tl == 0 and n_s % tl == 0
    n_tok = bsz * seq + n_s
    x2 = x.reshape(bsz * seq, D_MODEL)
    main = lambda i: jnp.minimum(i, n_main - 1)
    prev = lambda i: jnp.clip(i - 1, 0, n_main - 1)
    tail = lambda i: jnp.maximum(i - n_main - 1, 0)
    out_row = lambda i: jnp.maximum(i - 1, 0)
    weights = [w['nmw'], w['wz'], w['wxbc'], w['wdt'], w['wuv'], w['convw'], w['convb'], w['dtb'],
               w['alog'], w['dskip'], w['ssmnw'], w['sgnw'], w['sgw'], w['sgb'], w['wout'], w['nfw'],
               w['wr'], w['br']]
    in_specs = ([pl.BlockSpec((tl, D_MODEL), lambda i: (main(i), 0)),
                 pl.BlockSpec((tl, D_MODEL), lambda i: (prev(i), 0))]
                + [_const_spec(a.shape) for a in weights]
                + [pl.BlockSpec(blk, lambda i: (tail(i), 0))
                   for blk in ((tl, D_MODEL), (tl * ROW_TILES, LANES), (tl, LANES))])
    q = HEADS_PER_GROUP * SSM_HEAD_DIM
    out_shape = [
        jax.ShapeDtypeStruct((n_tok, D_MODEL), F32),
        jax.ShapeDtypeStruct((n_tok * ROW_TILES, LANES), F32),
        jax.ShapeDtypeStruct((n_tok, LANES), F32),
        jax.ShapeDtypeStruct((bsz, SUBLANES, C_CONV), F32),
        jax.ShapeDtypeStruct((bsz, SSM_GROUPS, q, SSM_STATE), F32),
    ]
    out_specs = [
        pl.BlockSpec((tl, D_MODEL), lambda i: (out_row(i), 0)),
        pl.BlockSpec((tl * ROW_TILES, LANES), lambda i: (out_row(i), 0)),
        pl.BlockSpec((tl, LANES), lambda i: (out_row(i), 0)),
        pl.BlockSpec((None, SUBLANES, C_CONV), lambda i: (main(i) // nt, 0, 0)),
        pl.BlockSpec((None, SSM_GROUPS, q, SSM_STATE), lambda i: (main(i) // nt, 0, 0, 0)),
    ]
    scratch = [
        pltpu.VMEM((tl + SUBLANES, C_CONV), F32),
        pltpu.VMEM((SSM_GROUPS, SSM_STATE, q), F32),
        pltpu.VMEM((tl, D_SSM), F32),
        pltpu.VMEM((tl, D_SSM + D_SG), BF16),
        pltpu.VMEM((tl, D_MODEL), F32),
    ]
    return pl.pallas_call(
        functools.partial(_prompt_mixer_kernel, tl=tl, nt=nt, n_main=n_main),
        grid=(n_main + 1 + n_s // tl,),
        in_specs=in_specs,
        out_specs=out_specs,
        out_shape=out_shape,
        scratch_shapes=scratch,
        compiler_params=pltpu.CompilerParams(
            dimension_semantics=("arbitrary",), vmem_limit_bytes=VMEM_LIMIT),
        name="prompt_mixer",
    )(x2, x2, *weights, h_s, hn2_s, route_s)


def _sample_front_kernel(
        x_ref, cprev_ref, nmw_ref, wz_ref, wxbc_ref, wdt_ref, wuv_ref, convw_ref, convb_ref,
        dtb_ref, alog_ref, dskip_ref, sgnw_ref, w4_ref, b4_ref, expand_ref,
        z_ref, ypart_ref, ea_ref, ysg_ref, v_ref, convn_ref, c_ref, b_ref, xw_ref, dec_ref,
        *, nb, nt):
    x = x_ref[...]
    hn = _rms(x, nmw_ref[...]).astype(BF16)
    z_ref[...] = _dot(hn, wz_ref[...])
    xbc = _dot(hn, wxbc_ref[...])
    dtr = _dot(hn, wdt_ref[...])
    uv = _gelu(_dot(hn, wuv_ref[...]))

    slab = lambda a, t: a[t * nb:(t + 1) * nb]
    full = [cprev_ref[k] for k in range(CONV_K - 1)] + [slab(xbc, t) for t in range(nt)]
    for k in range(CONV_K - 1):
        convn_ref[k] = full[nt + k]
    xs, bm, cm = [], [], []
    for t in range(nt):
        conv = convb_ref[...]
        for k in range(CONV_K):
            conv = conv + convw_ref[k:k + 1, :] * full[t + k]
        act = _silu(conv)
        xs.append(act[:, :D_SSM])
        bm.append(act[:, D_SSM:D_SSM + SSM_GROUPS * SSM_STATE])
        cm.append(act[:, D_SSM + SSM_GROUPS * SSM_STATE:])
        b_ref[t * nb:(t + 1) * nb, :] = bm[t]
        c_ref[t * nb:(t + 1) * nb, :] = cm[t]

    dt = _softplus(dtr + dtb_ref[...])
    a_row = -jnp.exp(alog_ref[...])
    dts = [slab(dt, t) for t in range(nt)]
    cum = []
    for t in range(nt):
        da = dts[t] * a_row
        cum.append(da if t == 0 else cum[t - 1] + da)
    dec_ref[...] = jnp.exp(cum[nt - 1])

    lane = lax.broadcasted_iota(jnp.int32, (nb, LANES), 1)
    first_group = lane < HEADS_PER_GROUP
    facs = [jnp.exp(cum[t]) for t in range(nt)]
    facs += [dts[s] * jnp.exp(cum[nt - 1] - cum[s]) for s in range(nt)]
    pairs = []
    for t in range(nt):
        for s in range(t + 1):
            cb = []
            for g in range(SSM_GROUPS):
                gl = slice(g * SSM_STATE, (g + 1) * SSM_STATE)
                cb.append(jnp.sum(cm[t][:, gl] * bm[s][:, gl], axis=1, keepdims=True))
            cbh = jnp.where(first_group, cb[0], cb[1])
            facs.append(jnp.exp(cum[t] - cum[s]) * dts[s] * cbh)
            pairs.append((t, s))
    fx = _dot_f32(jnp.concatenate(facs, axis=0), expand_ref[...])
    fslab = lambda i: fx[i * nb:(i + 1) * nb]
    for t in range(nt):
        ea_ref[t * nb:(t + 1) * nb, :] = fslab(t)
        xw_ref[t * nb:(t + 1) * nb, :] = xs[t] * fslab(nt + t)
    for t in range(nt):
        acc = dskip_ref[...] * xs[t]
        for i, (tt, s) in enumerate(pairs):
            if tt == t:
                acc = acc + fslab(2 * nt + i) * xs[s]
        ypart_ref[t * nb:(t + 1) * nb, :] = acc

    u = uv[:, :D_SG]
    v = _rms(uv[:, D_SG:], sgnw_ref[...])
    v_ref[...] = v
    for t in range(nt):
        sv = b4_ref[t:t + 1, :]
        for s in range(t + 1):
            sv = sv + w4_ref[t, s:s + 1, :] * slab(v, s)
        ysg_ref[t * nb:(t + 1) * nb, :] = slab(u, t) * sv


def _sample_front(x_slab, cprev, w, nb, nt):
    r = nb * nt
    weights = [w['nmw'], w['wz'], w['wxbc'], w['wdt'], w['wuv'], w['convw'], w['convb'], w['dtb'],
               w['alog'], w['dskip'], w['sgnw'], w['w4'], w['b4'], w['expand']]
    ins = [x_slab, cprev] + weights
    out_shape = [
        jax.ShapeDtypeStruct((r, D_SSM), F32),
        jax.ShapeDtypeStruct((r, D_SSM), F32),
        jax.ShapeDtypeStruct((r, D_SSM), F32),
        jax.ShapeDtypeStruct((r, D_SG), F32),
        jax.ShapeDtypeStruct((r, D_SG), F32),
        jax.ShapeDtypeStruct((CONV_K - 1, nb, C_CONV), F32),
        jax.ShapeDtypeStruct((r, SSM_GROUPS * SSM_STATE), F32),
        jax.ShapeDtypeStruct((r, SSM_GROUPS * SSM_STATE), F32),
        jax.ShapeDtypeStruct((r, D_SSM), F32),
        jax.ShapeDtypeStruct((nb, LANES), F32),
    ]
    return pl.pallas_call(
        functools.partial(_sample_front_kernel, nb=nb, nt=nt),
        grid=(1,),
        in_specs=[_const_spec(a.shape) for a in ins],
        out_specs=[_const_spec(s.shape) for s in out_shape],
        out_shape=out_shape,
        compiler_params=pltpu.CompilerParams(
            dimension_semantics=("arbitrary",), vmem_limit_bytes=VMEM_LIMIT),
        name="sample_front",
    )(*ins)


def _sample_state_kernel(dec_ref, st_ref, cq_ref, bq_ref, xw_ref, so_ref, z_ref, *, seqs):
    for bb in range(seqs):
        for g in range(SSM_GROUPS):
            s0 = st_ref[bb, g]
            z_ref[bb, g] = lax.dot_general(
                cq_ref[bb, g].astype(BF16), s0.astype(BF16), (((1,), (1,)), ((), ())),
                preferred_element_type=F32)
            upd = lax.dot_general(
                xw_ref[bb, g].astype(BF16), bq_ref[bb, g].astype(BF16), (((0,), (0,)), ((), ())),
                preferred_element_type=F32)
            for hh in range(HEADS_PER_GROUP):
                hs = slice(hh * SSM_HEAD_DIM, (hh + 1) * SSM_HEAD_DIM)
                so_ref[bb, g, hs, :] = s0[hs] * dec_ref[bb, g * HEADS_PER_GROUP + hh] + upd[hs]


def _sample_state(dec, state, cq, bq, xwq, seqs):
    nb = state.shape[0]
    q = HEADS_PER_GROUP * SSM_HEAD_DIM
    blk = lambda *tail: pl.BlockSpec((seqs, SSM_GROUPS) + tail, lambda i: (i, 0, 0, 0))
    return pl.pallas_call(
        functools.partial(_sample_state_kernel, seqs=seqs),
        grid=(nb // seqs,),
        in_specs=[
            pl.BlockSpec((seqs, SSM_HEADS), lambda i: (i, 0), memory_space=pltpu.SMEM),
            blk(q, SSM_STATE), blk(SUBLANES, SSM_STATE), blk(SUBLANES, SSM_STATE), blk(SUBLANES, q),
        ],
        out_specs=[blk(q, SSM_STATE), blk(SUBLANES, q)],
        out_shape=[
            jax.ShapeDtypeStruct((nb, SSM_GROUPS, q, SSM_STATE), F32),
            jax.ShapeDtypeStruct((nb, SSM_GROUPS, SUBLANES, q), F32),
        ],
        compiler_params=pltpu.CompilerParams(
            dimension_semantics=("arbitrary",), vmem_limit_bytes=VMEM_LIMIT),
        name="sample_state",
    )(dec, state, cq, bq, xwq)


def _sample_back_kernel(x_ref, ypart_ref, ea_ref, zoff_ref, z_ref, ysg_ref, ssmnw_ref, wout_ref,
                        nfw_ref, wr_ref, br_ref, h_ref, hn2_ref, route_ref):
    y = ypart_ref[...] + ea_ref[...] * zoff_ref[...]
    y_ssm = _gated_group_norm(y, z_ref[...], ssmnw_ref[...])
    mix_in = jnp.concatenate([y_ssm.astype(BF16), ysg_ref[...].astype(BF16)], axis=1)
    h = x_ref[...] + _dot(mix_in, wout_ref[...])
    h_ref[...] = h
    hn2, route = _ffn_front(h, nfw_ref[...], wr_ref, br_ref[...])
    _store_row_tiles(hn2_ref, hn2)
    route_ref[...] = route


def _sample_back(x_slab, ypart, ea, zoff, z, ysg, w):
    r = x_slab.shape[0]
    ins = [x_slab, ypart, ea, zoff, z, ysg, w['ssmnw'], w['wout'], w['nfw'], w['wr'], w['br']]
    out_shape = [
        jax.ShapeDtypeStruct((r, D_MODEL), F32),
        jax.ShapeDtypeStruct((r * ROW_TILES, LANES), F32),
        jax.ShapeDtypeStruct((r, LANES), F32),
    ]
    return pl.pallas_call(
        _sample_back_kernel,
        grid=(1,),
        in_specs=[_const_spec(a.shape) for a in ins],
        out_specs=[_const_spec(s.shape) for s in out_shape],
        out_shape=out_shape,
        compiler_params=pltpu.CompilerParams(
            dimension_semantics=("arbitrary",), vmem_limit_bytes=VMEM_LIMIT),
        name="sample_back",
    )(*ins)


EXPERT_PIECES = 8
DMA_PIECES = 4


def _expert_mlp_kernel(be_ref, nu_ref, tok0_ref, tokn_ref, dstp_ref, x_hbm, wg_ref, wu_ref, wd_ref,
                       parts_hbm, xbuf, obuf, xb, hb, wg_b, wu_b, wd_b, gsem, ssem, *, bm):
    j = pl.program_id(0)
    nu = nu_ref[0]
    nt = ROW_TILES
    tile = lambda ref, start: ref.at[pl.ds(pl.multiple_of(start, nt), nt)]

    def gather_copy(idx_ref, s, r):
        return pltpu.make_async_copy(tile(x_hbm, idx_ref[0, r]), xbuf.at[s, pl.ds(r * nt, nt)], gsem.at[s])

    def scatter_copy(s, r):
        return pltpu.make_async_copy(obuf.at[s, pl.ds(r * nt, nt)], tile(parts_hbm, dstp_ref[0, r]),
                                     ssem.at[s])

    def gather_wait(s):
        pltpu.make_async_copy(x_hbm.at[pl.ds(0, bm * nt)], xbuf.at[s], gsem.at[s]).wait()

    def scatter_wait(s):
        pltpu.make_async_copy(obuf.at[s], parts_hbm.at[pl.ds(0, bm * nt)], ssem.at[s]).wait()

    def step(slot):
        gather_wait(slot)
        per = bm // DMA_PIECES

        def start_rows(piece):
            if piece >= DMA_PIECES:
                return
            for r in range(piece * per, (piece + 1) * per):
                gather_copy(tokn_ref, 1 - slot, r).start(priority=r % 2)
                scatter_copy(1 - slot, r).start(priority=(r + 1) % 2)

        xb[...] = _load_row_tiles(xbuf.at[slot]).astype(BF16)
        half = D_FF // 2
        for c in range(2):
            cols = slice(c * half, (c + 1) * half)
            start_rows(2 * c)
            g = _dot(xb[...], wg_b[:, cols])
            start_rows(2 * c + 1)
            u = _dot(xb[...], wu_b[:, cols])
            hb[:, cols] = (_silu(g) * u).astype(BF16)
        quarter = D_MODEL // 4
        scatter_wait(slot)
        for c in range(4):
            start_rows(4 + c)
            o = _dot(hb[...], wd_b[:, c * quarter:(c + 1) * quarter])
            for k in range(quarter // LANES):
                kk = c * (quarter // LANES) + k
                obuf.at[slot][pl.ds(kk, bm, stride=nt), :] = o[:, k * LANES:(k + 1) * LANES]

    def drain(slot):
        gather_wait(slot)
        scatter_wait(slot)
        for r in range(bm):
            scatter_copy(1 - slot, r).start(priority=r % 2)
        scatter_wait(1 - slot)

    @pl.when(j == 0)
    def _():
        obuf[...] = jnp.zeros(obuf.shape, F32)
        n_real = parts_hbm.shape[0] - 2 * bm * nt
        pltpu.make_async_copy(obuf.at[0], parts_hbm.at[pl.ds(n_real, bm * nt)], ssem.at[0]).start()

        def start0(r, carry):
            pltpu.make_async_copy(tile(x_hbm, tok0_ref[0, r]), xbuf.at[0, pl.ds(pl.multiple_of(r * nt, nt), nt)],
                                  gsem.at[0]).start()
            return carry
        lax.fori_loop(0, bm, start0, 0)

    used = j < nu

    @pl.when(used & ((j == 0) | (be_ref[j] != be_ref[jnp.maximum(j - 1, 0)])))
    def _():
        wg_b[...] = wg_ref[...].astype(BF16)
        wu_b[...] = wu_ref[...].astype(BF16)
        wd_b[...] = wd_ref[...].astype(BF16)

    for slot in range(2):
        parity = lax.rem(j, 2) == slot

        @pl.when(used & parity)
        def _(slot=slot):
            step(slot)

        @pl.when((j == nu) & parity)
        def _(slot=slot):
            drain(slot)


def _expert_mlp(blk_expert, n_used, tok, dst, x, wg, wu, wd, bm, n_out):
    n_blocks = blk_expert.shape[0]
    assert bm % EXPERT_PIECES == 0
    w_map = lambda j, be, nu: (be[j], 0, 0)
    idx_spec = lambda f: pl.BlockSpec((None, 1, bm), lambda j, be, nu: (f(j), 0, 0),
                                      memory_space=pltpu.SMEM)
    any_spec = pl.BlockSpec(memory_space=pl.ANY)
    grid_spec = pltpu.PrefetchScalarGridSpec(
        num_scalar_prefetch=2,
        grid=(n_blocks + 1,),
        in_specs=[
            idx_spec(lambda j: 0),
            idx_spec(lambda j: jnp.minimum(j + 1, n_blocks - 1)),
            idx_spec(lambda j: j),
            any_spec,
            pl.BlockSpec((None, D_MODEL, D_FF), w_map),
            pl.BlockSpec((None, D_MODEL, D_FF), w_map),
            pl.BlockSpec((None, D_FF, D_MODEL), w_map),
        ],
        out_specs=any_spec,
        scratch_shapes=[
            pltpu.VMEM((2, bm * ROW_TILES, LANES), F32),
            pltpu.VMEM((2, bm * ROW_TILES, LANES), F32),
            pltpu.VMEM((bm, D_MODEL), BF16),
            pltpu.VMEM((bm, D_FF), BF16),
            pltpu.VMEM((D_MODEL, D_FF), BF16),
            pltpu.VMEM((D_MODEL, D_FF), BF16),
            pltpu.VMEM((D_FF, D_MODEL), BF16),
            pltpu.SemaphoreType.DMA((2,)),
            pltpu.SemaphoreType.DMA((2,)),
        ],
    )
    tok3 = (tok * ROW_TILES).reshape(n_blocks, 1, bm)
    spare = (n_out - bm + jnp.arange(bm, dtype=jnp.int32)).reshape(1, bm)
    dst_prev = (jnp.concatenate([spare, dst], axis=0) * ROW_TILES).reshape(n_blocks + 1, 1, bm)
    blk_expert = jnp.concatenate([blk_expert, blk_expert[-1:]])
    return pl.pallas_call(
        functools.partial(_expert_mlp_kernel, bm=bm),
        grid_spec=grid_spec,
        out_shape=jax.ShapeDtypeStruct((n_out * ROW_TILES, LANES), F32),
        compiler_params=pltpu.CompilerParams(
            dimension_semantics=("arbitrary",), vmem_limit_bytes=VMEM_LIMIT),
        name="expert_mlp",
    )(blk_expert, n_used, tok3, tok3, dst_prev, x, wg, wu, wd)


PAD_ID = 2 ** 16 - 1


def _moe_plan(e_idx, bm, n_blocks):
    n_tok = e_idx.shape[0]
    n_asg = 2 * n_tok
    n_pad = n_blocks * bm - n_asg
    e_flat = e_idx.reshape(-1)
    experts = jnp.arange(N_EXPERTS, dtype=jnp.int32)
    counts = jnp.sum((e_flat[:, None] == experts[None, :]).astype(jnp.int32), axis=0)
    nblk = (counts + bm - 1) // bm
    blk_end = jnp.cumsum(nblk)
    n_used = blk_end[-1]
    j = jnp.arange(n_blocks, dtype=jnp.int32)
    be = jnp.sum((blk_end[None, :] <= jnp.minimum(j, n_used - 1)[:, None]).astype(jnp.int32), axis=1)
    be = jnp.minimum(be, N_EXPERTS - 1)
    assert n_asg < PAD_ID
    pad_end = jnp.cumsum(nblk * bm - counts)
    pad_expert = jnp.sum((pad_end[None, :] <= jnp.arange(n_pad, dtype=jnp.int32)[:, None]).astype(jnp.int32),
                         axis=1)
    keys = jnp.concatenate([2 * e_flat, 2 * pad_expert + 1])
    ids = jnp.concatenate([jnp.arange(n_asg, dtype=jnp.int32), jnp.full((n_pad,), PAD_ID, jnp.int32)])
    window = lax.sort(keys * (PAD_ID + 1) + ids) % (PAD_ID + 1)
    window = window.reshape(n_blocks, bm)
    valid = window != PAD_ID
    q = jnp.arange(bm, dtype=jnp.int32)
    tok = jnp.where(valid, window // 2, (j[:, None] * bm + q[None, :]) % n_tok)
    dst = jnp.where(valid, (window % 2) * n_tok + window // 2,
                    2 * n_tok + (j[:, None] % 2) * bm + q[None, :])
    return tok.astype(jnp.int32), dst.astype(jnp.int32), be, n_used.astype(jnp.int32).reshape(1)


def _final_kernel(h_ref, p0_ref, p1_ref, route_ref, w_ref, op_ref, os_ref, *, n_prompt_steps):
    route = route_ref[...]
    p0 = _load_row_tiles(p0_ref)
    p1 = _load_row_tiles(p1_ref)
    y = _rms(h_ref[...] + route[:, 2:3] * p0 + route[:, 3:4] * p1, w_ref[...])
    i = pl.program_id(0)

    @pl.when(i < n_prompt_steps)
    def _():
        op_ref[...] = y

    @pl.when(i >= n_prompt_steps)
    def _():
        os_ref[...] = y


def _final(h, parts, route, w, tf, n_prompt):
    n_tok = h.shape[0]
    assert n_tok % tf == 0 and n_prompt % tf == 0
    n_p = n_prompt // tf
    row = lambda d: pl.BlockSpec((tf, d), lambda i: (i, 0))
    part = lambda k: pl.BlockSpec((tf * ROW_TILES, LANES), lambda i: (k * (n_tok // tf) + i, 0))
    return pl.pallas_call(
        functools.partial(_final_kernel, n_prompt_steps=n_p),
        grid=(n_tok // tf,),
        in_specs=[row(D_MODEL), part(0), part(1), row(LANES), _const_spec(w.shape)],
        out_specs=[pl.BlockSpec((tf, D_MODEL), lambda i: (jnp.minimum(i, n_p - 1), 0)),
                   pl.BlockSpec((tf, D_MODEL), lambda i: (jnp.maximum(i - n_p, 0), 0))],
        out_shape=[jax.ShapeDtypeStruct((n_prompt, D_MODEL), F32),
                   jax.ShapeDtypeStruct((n_tok - n_prompt, D_MODEL), F32)],
        compiler_params=pltpu.CompilerParams(
            dimension_semantics=("arbitrary",), vmem_limit_bytes=VMEM_LIMIT),
        name="final_combine",
    )(h, parts, parts, route, w)


PROMPT_TILE = 256
MOE_BLOCK_ROWS = 512
FINAL_TILE = 512
STATE_SEQS = 8


def _pad_lanes(a, width=LANES):
    return jnp.pad(a, [(0, 0)] * (a.ndim - 1) + [(0, width - a.shape[-1])])


def _layer_weights(l, norm_mix_w, w_in, conv_w, conv_b, dt_bias, a_log, d_skip, ssm_norm_w, sg_norm_w,
                   sg_w, sg_b, w_out, norm_ffn_w, w_router_group, b_router_group, w_router_expert,
                   b_router_expert, n_dec):
    c0, c1, c2 = D_SSM, D_SSM + C_CONV, D_SSM + C_CONV + SSM_HEADS
    wi = w_in[l]
    causal = jnp.tril(jnp.ones((CHUNK, CHUNK), bool))
    sgw_l = jnp.where(causal, sg_w[l], 0.0)
    head_of_lane = jnp.arange(D_SSM, dtype=jnp.int32) // SSM_HEAD_DIM
    expand = (jnp.arange(LANES, dtype=jnp.int32)[:, None] == head_of_lane[None, :]).astype(F32)
    row = lambda a: a.reshape(1, -1).astype(F32)
    gap = ROUTER_EXPERT_ROW - N_EXPERT_GROUPS
    rest = LANES - ROUTER_EXPERT_ROW - N_EXPERTS
    wrt = jnp.concatenate([w_router_group[l].T, jnp.zeros((gap, D_MODEL), F32), w_router_expert[l].T,
                           jnp.zeros((rest, D_MODEL), F32)], axis=0).astype(F32)
    brc = jnp.concatenate([b_router_group[l], jnp.zeros((gap,), F32), b_router_expert[l],
                           jnp.zeros((rest,), F32)]).astype(F32).reshape(LANES, 1)
    return {
        'nmw': row(norm_mix_w[l]),
        'wz': wi[:, :c0].astype(BF16),
        'wxbc': wi[:, c0:c1].astype(BF16),
        'wdt': _pad_lanes(wi[:, c1:c2]).astype(BF16),
        'wuv': wi[:, c2:].astype(BF16),
        'convw': conv_w[l].astype(F32),
        'convb': row(conv_b[l]),
        'dtb': _pad_lanes(row(dt_bias[l])),
        'alog': _pad_lanes(row(a_log[l])),
        'dskip': row(jnp.repeat(d_skip[l], SSM_HEAD_DIM)),
        'ssmnw': row(ssm_norm_w[l]),
        'sgnw': row(sg_norm_w[l]),
        'sgw': sgw_l.astype(BF16),
        'sgb': jnp.repeat(sg_b[l].T, SG_HEAD_DIM, axis=1).astype(F32),
        'w4': jnp.repeat(jnp.transpose(sgw_l[:, :n_dec, :n_dec], (1, 2, 0)), SG_HEAD_DIM, axis=2).astype(F32),
        'b4': jnp.repeat(sg_b[l][:, :n_dec].T, SG_HEAD_DIM, axis=1).astype(F32),
        'expand': expand,
        'wout': w_out[l].astype(BF16),
        'nfw': row(norm_ffn_w[l]),
        'wr': jnp.stack([wrt.astype(BF16), (wrt - wrt.astype(BF16).astype(F32)).astype(BF16)]),
        'br': brc,
    }


def _layer(l, xp, xs_slab, state_conv, state_ssm, w, w_gate, w_up, w_down):
    bp, seq, _ = xp.shape
    nb, nt = state_conv.shape[1], xs_slab.shape[0] // state_conv.shape[1]
    q = HEADS_PER_GROUP * SSM_HEAD_DIM

    cprev = jnp.transpose(state_conv[l], (1, 0, 2))
    z, ypart, ea, ysg, v, convn, cmat, bmat, xw, dec = _sample_front(xs_slab, cprev, w, nb, nt)
    to_seq = lambda a, d: jnp.pad(
        jnp.transpose(a.reshape(nt, nb, SSM_GROUPS, d), (1, 2, 0, 3)),
        ((0, 0), (0, 0), (0, SUBLANES - nt), (0, 0)))
    ssm_s, zoff = _sample_state(
        dec[:, :SSM_HEADS], state_ssm[l].reshape(nb, SSM_GROUPS, q, SSM_STATE),
        to_seq(cmat, SSM_STATE), to_seq(bmat, SSM_STATE), to_seq(xw, q), STATE_SEQS)
    zoff = jnp.transpose(zoff[:, :, :nt], (2, 0, 1, 3)).reshape(nt * nb, D_SSM)
    h_s, hn2_s, route_s = _sample_back(xs_slab, ypart, ea, zoff, z, ysg, w)

    h, hn2, route, convt_p, ssm_p = _prompt_mixer(xp, w, PROMPT_TILE, h_s, hn2_s, route_s)

    n_tok = h.shape[0]
    bm = MOE_BLOCK_ROWS
    n_blocks = -(-(2 * n_tok) // bm) + N_EXPERTS
    tok, dst, blk_expert, n_used = _moe_plan(route[:, :2].astype(jnp.int32), bm, n_blocks)
    parts = _expert_mlp(blk_expert, n_used, tok, dst, hn2, w_gate[l], w_up[l], w_down[l], bm,
                        2 * n_tok + 2 * bm)
    outs = dict(
        h=h, parts=parts, route=route,
        conv_p=convt_p[:, SUBLANES - (CONV_K - 1):],
        ssm_p=ssm_p.reshape(bp, SSM_GROUPS, HEADS_PER_GROUP, SSM_HEAD_DIM, SSM_STATE),
        conv_s=jnp.transpose(convn, (1, 0, 2)),
        ssm_s=ssm_s.reshape(nb, SSM_GROUPS, HEADS_PER_GROUP, SSM_HEAD_DIM, SSM_STATE),
        v_s=jnp.transpose(v.reshape(nt, nb, D_SG), (1, 0, 2)),
    )
    return outs


def kernel(x_prompt, x_sample, state_conv, state_ssm, norm_mix_w, w_in, conv_w, conv_b, dt_bias, a_log, d_skip, ssm_norm_w, sg_norm_w, sg_w, sg_b, w_out, norm_ffn_w, w_router_group, b_router_group, w_router_expert, b_router_expert, w_gate, w_up, w_down, norm_final_w):
    depth = w_in.shape[0]
    assert depth == 1, "the fused final norm assumes a single layer"
    bp, seq, _ = x_prompt.shape
    nb, nt, _ = x_sample.shape
    l = 0
    w = _layer_weights(l, norm_mix_w, w_in, conv_w, conv_b, dt_bias, a_log, d_skip, ssm_norm_w, sg_norm_w,
                       sg_w, sg_b, w_out, norm_ffn_w, w_router_group, b_router_group, w_router_expert,
                       b_router_expert, nt)
    xs_slab = jnp.transpose(x_sample, (1, 0, 2)).reshape(nt * nb, D_MODEL)
    outs = _layer(l, x_prompt, xs_slab, state_conv, state_ssm, w, w_gate, w_up, w_down)
    y_p, y_s = _final(outs['h'], outs['parts'], outs['route'], norm_final_w.reshape(1, -1).astype(F32),
                      FINAL_TILE, bp * seq)
    y_prompt = y_p.reshape(bp, seq, D_MODEL)
    y_sample = jnp.transpose(y_s.reshape(nt, nb, D_MODEL), (1, 0, 2))
    return (y_prompt, y_sample, outs['conv_p'][None], outs['ssm_p'][None], outs['conv_s'][None],
            outs['ssm_s'][None], outs['v_s'][None])
```

```python
import functools

import jax
import jax.numpy as jnp
from jax import lax
from jax.experimental import pallas as pl
from jax.experimental.pallas import tpu as pltpu

D_MODEL = 1024
D_SSM = 1024
SSM_HEAD_DIM = 64
SSM_HEADS = 16
SSM_GROUPS = 2
HEADS_PER_GROUP = 8
SSM_STATE = 128
CONV_K = 4
C_CONV = D_SSM + 2 * SSM_GROUPS * SSM_STATE
D_SG = 1024
SG_HEADS = 8
SG_HEAD_DIM = 128
CHUNK = 128
N_EXPERT_GROUPS = 4
EXPERTS_PER_GROUP = 8
N_EXPERTS = 32
D_FF = 512
EPS = 1e-6

LANES = 128
SUBLANES = 8
VMEM_LIMIT = 56 * 1024 * 1024

F32 = jnp.float32
BF16 = jnp.bfloat16
HIGHEST = lax.Precision.HIGHEST


def _dot(a, b):
    return jnp.dot(a, b, preferred_element_type=F32)


def _dot_f32(a, b):
    return jnp.dot(a, b, preferred_element_type=F32, precision=HIGHEST)


def _rms(x, w):
    return x * lax.rsqrt(jnp.mean(x * x, axis=-1, keepdims=True) + EPS) * w


def _silu(x):
    return x * (0.5 + 0.5 * jnp.tanh(0.5 * x))


def _gelu(x):
    return 0.5 * x * (1.0 + lax.erf(x * 0.7071067811865476))


def _softplus(x):
    return jnp.maximum(x, 0.0) + jnp.log1p(jnp.exp(-jnp.abs(x)))


def _gated_group_norm(y, z, w):
    g = y * _silu(z)
    half = D_SSM // SSM_GROUPS
    parts = []
    for k in range(SSM_GROUPS):
        gk = g[:, k * half:(k + 1) * half]
        parts.append(gk * lax.rsqrt(jnp.mean(gk * gk, axis=-1, keepdims=True) + EPS))
    return jnp.concatenate(parts, axis=1) * w


ROUTER_GROUP_ROW = 0
ROUTER_EXPERT_ROW = SUBLANES


def _route_t(lt):
    r = lt.shape[1]
    row = lax.broadcasted_iota(jnp.int32, (SUBLANES, r), 0)
    rowf = row.astype(F32)
    big = float(SUBLANES)
    red = lambda f, a: f(a, axis=0, keepdims=True)
    gl = jnp.where(row < N_EXPERT_GROUPS, lt[ROUTER_GROUP_ROW:ROUTER_GROUP_ROW + SUBLANES], -jnp.inf)
    ge = jnp.exp(gl - red(jnp.max, gl))
    p_grp = ge / red(jnp.sum, ge)
    g_p = red(jnp.max, p_grp)
    g_idx = red(jnp.min, jnp.where(p_grp == g_p, rowf, big))
    el = lt[ROUTER_EXPERT_ROW:ROUTER_EXPERT_ROW + EXPERTS_PER_GROUP]
    for g in range(1, N_EXPERT_GROUPS):
        lo = ROUTER_EXPERT_ROW + g * EXPERTS_PER_GROUP
        el = jnp.where(g_idx == float(g), lt[lo:lo + EXPERTS_PER_GROUP], el)
    ee = jnp.exp(el - red(jnp.max, el))
    pe = ee / red(jnp.sum, ee)
    v1 = red(jnp.max, pe)
    i1 = red(jnp.min, jnp.where(pe == v1, rowf, big))
    rest = rowf != i1
    pe2 = jnp.where(rest, pe, -1.0)
    v2 = red(jnp.max, pe2)
    i2 = red(jnp.min, jnp.where(rest & (pe2 == v2), rowf, big))
    den = v1 + v2
    base = g_idx * float(EXPERTS_PER_GROUP)
    out = jnp.where(row == 0, base + i1, 0.0)
    out = jnp.where(row == 1, base + i2, out)
    out = jnp.where(row == 2, g_p * v1 / den, out)
    out = jnp.where(row == 3, g_p * v2 / den, out)
    return jnp.concatenate([out, jnp.zeros((LANES - SUBLANES, r), F32)], axis=0).T


def _ffn_front(h, nfw, wrt_ref, brc):
    hn2 = _rms(h, nfw)
    hi = hn2.astype(BF16)
    lo = (hn2 - hi.astype(F32)).astype(BF16)
    nt = lambda a, b: lax.dot_general(a, b, (((1,), (1,)), ((), ())), preferred_element_type=F32)
    lt = nt(wrt_ref[0], hi) + nt(wrt_ref[0], lo) + nt(wrt_ref[1], hi) + brc
    return hn2, _route_t(lt)


ROW_TILES = D_MODEL // LANES


def _store_row_tiles(ref, val):
    r = val.shape[0]
    for k in range(ROW_TILES):
        ref[pl.ds(k, r, stride=ROW_TILES), :] = val[:, k * LANES:(k + 1) * LANES]


def _load_row_tiles(ref):
    r = ref.shape[0] // ROW_TILES
    return jnp.concatenate([ref[pl.ds(k, r, stride=ROW_TILES), :] for k in range(ROW_TILES)], axis=1)


def _prompt_step(
        t, nt, fillers, x_ref, nmw_ref, wz_ref, wxbc_ref, wdt_ref, wuv_ref, convw_ref, convb_ref, dtb_ref,
        alog_ref, dskip_ref, ssmnw_ref, sgnw_ref, sgw_ref, sgb_ref, convt_ref, ssm_ref,
        xbc_scr, state_scr, y_scr, mix_scr, tl):
    fillers = list(fillers)
    assert len(fillers) == 3 + tl // CHUNK

    @pl.when(t == 0)
    def _():
        xbc_scr[0:SUBLANES, :] = jnp.zeros((SUBLANES, C_CONV), F32)
        state_scr[...] = jnp.zeros(state_scr.shape, F32)

    x = x_ref[...]
    hn = _rms(x, nmw_ref[...]).astype(BF16)
    z = _dot(hn, wz_ref[...])
    xbc = _dot(hn, wxbc_ref[...])
    dtr = _dot(hn, wdt_ref[...])
    uv = _gelu(_dot(hn, wuv_ref[...]))
    fillers.pop(0)()

    xbc_scr[SUBLANES:SUBLANES + tl, :] = xbc
    conv = convb_ref[...]
    for k in range(CONV_K):
        off = SUBLANES - (CONV_K - 1) + k
        conv = conv + convw_ref[k:k + 1, :] * xbc_scr[off:off + tl, :]
    tail = xbc_scr[tl:tl + SUBLANES, :]
    xbc_scr[0:SUBLANES, :] = tail
    convt_ref[...] = tail
    act = _silu(conv)
    xs = act[:, :D_SSM]
    bmat = act[:, D_SSM:D_SSM + SSM_GROUPS * SSM_STATE]
    cmat = act[:, D_SSM + SSM_GROUPS * SSM_STATE:]
    fillers.pop(0)()

    dt = _softplus(dtr + dtb_ref[...])
    a_row = -jnp.exp(alog_ref[...])

    li = lax.broadcasted_iota(jnp.int32, (CHUNK, CHUNK), 0)
    si = lax.broadcasted_iota(jnp.int32, (CHUNK, CHUNK), 1)
    causal = li >= si
    tri = jnp.where(causal, 1.0, 0.0).astype(F32)
    lo_half = si < SSM_HEAD_DIM

    for c in range(tl // CHUNK):
        rows = slice(c * CHUNK, (c + 1) * CHUNK)
        dt_c = dt[rows]
        acum = _dot_f32(tri, dt_c * a_row)
        acum_t = acum.T
        dt_t = dt_c.T
        w_t = dt_t * jnp.exp(acum_t[:, CHUNK - 1:CHUNK] - acum_t)
        xs_c = xs[rows]
        for g in range(SSM_GROUPS):
            gl = slice(g * SSM_STATE, (g + 1) * SSM_STATE)
            b_g = bmat[rows, gl]
            c_g = cmat[rows, gl]
            b_gt = b_g.T
            cb = _dot(c_g.astype(BF16), b_gt.astype(BF16))
            for j in range(HEADS_PER_GROUP // 2):
                pl_ = slice(g * 512 + j * LANES, g * 512 + (j + 1) * LANES)
                m_l, ec_l, s_l, ea_last = [], [], [], []
                for k in range(2):
                    hh = g * HEADS_PER_GROUP + 2 * j + k
                    colb = jnp.broadcast_to(acum[:, hh:hh + 1], (CHUNK, CHUNK))
                    rowb = jnp.broadcast_to(acum_t[hh:hh + 1, :], (CHUNK, CHUNK))
                    ea = jnp.exp(colb)
                    seg = jnp.where(causal, colb - rowb, 0.0)
                    lmat = jnp.where(causal, jnp.exp(seg), 0.0)
                    m_l.append(cb * lmat * jnp.broadcast_to(dt_t[hh:hh + 1, :], (CHUNK, CHUNK)))
                    ec_l.append(c_g * ea)
                    s_l.append(b_gt * jnp.broadcast_to(w_t[hh:hh + 1, :], (CHUNK, CHUNK)))
                    ea_last.append(ea[CHUNK - 1:CHUNK, :])
                xs_p = xs_c[:, pl_]
                st_p = state_scr[g, :, j * LANES:(j + 1) * LANES]
                rx = jnp.concatenate([jnp.where(lo_half, xs_p, 0.0),
                                      jnp.where(lo_half, 0.0, xs_p)], axis=0).astype(BF16)
                rs = jnp.concatenate([jnp.where(lo_half, st_p, 0.0),
                                      jnp.where(lo_half, 0.0, st_p)], axis=0).astype(BF16)
                lhs = jnp.concatenate(m_l + ec_l, axis=1).astype(BF16)
                y_p = _dot(lhs, jnp.concatenate([rx, rs], axis=0))
                y_scr[rows, pl_] = y_p
                dec = jnp.where(lo_half[0:1, :], ea_last[0], ea_last[1])
                upd = _dot(jnp.concatenate(s_l, axis=1).astype(BF16), rx)
                state_scr[g, :, j * LANES:(j + 1) * LANES] = st_p * dec + upd
        fillers.pop(0)()

    y = y_scr[...] + dskip_ref[...] * xs
    y_ssm = _gated_group_norm(y, z, ssmnw_ref[...])
    mix_scr[:, :D_SSM] = y_ssm.astype(BF16)

    u = uv[:, :D_SG]
    v = _rms(uv[:, D_SG:], sgnw_ref[...])
    vb = v.astype(BF16)
    for c in range(tl // CHUNK):
        rows = slice(c * CHUNK, (c + 1) * CHUNK)
        for hd in range(SG_HEADS):
            hl = slice(hd * SG_HEAD_DIM, (hd + 1) * SG_HEAD_DIM)
            sv = _dot(sgw_ref[hd], vb[rows, hl]) + sgb_ref[:, hl]
            mix_scr[rows, D_SSM + hd * SG_HEAD_DIM:D_SSM + (hd + 1) * SG_HEAD_DIM] = (
                u[rows, hl] * sv).astype(BF16)

    fillers.pop(0)()

    @pl.when(t == nt - 1)
    def _():
        for g in range(SSM_GROUPS):
            ssm_ref[g] = state_scr[g].T


def _prompt_mixer_kernel(
        x_ref, xp_ref, nmw_ref, wz_ref, wxbc_ref, wdt_ref, wuv_ref, convw_ref, convb_ref, dtb_ref,
        alog_ref, dskip_ref, ssmnw_ref, sgnw_ref, sgw_ref, sgb_ref, wout_ref, nfw_ref,
        wr_ref, br_ref, hs_ref, hn2s_ref, routes_ref,
        h_ref, hn2_ref, route_ref, convt_ref, ssm_ref,
        xbc_scr, state_scr, y_scr, mix_scr, h_scr, *, tl, nt, n_main):
    i = pl.program_id(0)

    @pl.when(i == 0)
    def _():
        mix_scr[...] = jnp.zeros(mix_scr.shape, BF16)

    @pl.when(i > n_main)
    def _():
        h_ref[...] = hs_ref[...]
        hn2_ref[...] = hn2s_ref[...]
        route_ref[...] = routes_ref[...]

    n_out_chunks = 2 + tl // CHUNK
    width = D_MODEL // n_out_chunks
    assert width % LANES == 0 and width * n_out_chunks == D_MODEL

    def out_chunk(c):
        cols = slice(c * width, (c + 1) * width)
        h_scr[:, cols] = xp_ref[:, cols] + _dot(mix_scr[...], wout_ref[:, cols])

    def finish():
        h = h_scr[...]
        h_ref[...] = h
        hn2, route = _ffn_front(h, nfw_ref[...], wr_ref, br_ref[...])
        _store_row_tiles(hn2_ref, hn2)
        route_ref[...] = route

    @pl.when(i <= n_main)
    def _():
        def out_chunks(cs):
            for c in cs:
                out_chunk(c)

        half = n_out_chunks // 2
        fillers = ([functools.partial(out_chunks, range(half)),
                    functools.partial(out_chunks, range(half, n_out_chunks)), finish]
                   + [lambda: None] * (tl // CHUNK))
        _prompt_step(
            lax.rem(i, nt), nt, fillers, x_ref, nmw_ref, wz_ref, wxbc_ref, wdt_ref, wuv_ref,
            convw_ref, convb_ref, dtb_ref, alog_ref, dskip_ref, ssmnw_ref, sgnw_ref, sgw_ref, sgb_ref,
            convt_ref, ssm_ref, xbc_scr, state_scr, y_scr, mix_scr, tl)


def _const_spec(shape):
    zeros = (0,) * len(shape)
    return pl.BlockSpec(shape, lambda *_: zeros)


def _prompt_mixer(x, w, tl, h_s, hn2_s, route_s):
    bsz, seq, _ = x.shape
    nt = seq // tl
    n_main = bsz * nt
    n_s = h_s.shape[0]
    assert seq % tl == 0 and n_s % tl == 0
    n_tok = bsz * seq + n_s
    x2 = x.reshape(bsz * seq, D_MODEL)
    main = lambda i: jnp.minimum(i, n_main - 1)
    prev = lambda i: jnp.clip(i - 1, 0, n_main - 1)
    tail = lambda i: jnp.maximum(i - n_main - 1, 0)
    out_row = lambda i: jnp.maximum(i - 1, 0)
    weights = [w['nmw'], w['wz'], w['wxbc'], w['wdt'], w['wuv'], w['convw'], w['convb'], w['dtb'],
               w['alog'], w['dskip'], w['ssmnw'], w['sgnw'], w['sgw'], w['sgb'], w['wout'], w['nfw'],
               w['wr'], w['br']]
    in_specs = ([pl.BlockSpec((tl, D_MODEL), lambda i: (main(i), 0)),
                 pl.BlockSpec((tl, D_MODEL), lambda i: (prev(i), 0))]
                + [_const_spec(a.shape) for a in weights]
                + [pl.BlockSpec(blk, lambda i: (tail(i), 0))
                   for blk in ((tl, D_MODEL), (tl * ROW_TILES, LANES), (tl, LANES))])
    q = HEADS_PER_GROUP * SSM_HEAD_DIM
    out_shape = [
        jax.ShapeDtypeStruct((n_tok, D_MODEL), F32),
        jax.ShapeDtypeStruct((n_tok * ROW_TILES, LANES), F32),
        jax.ShapeDtypeStruct((n_tok, LANES), F32),
        jax.ShapeDtypeStruct((bsz, SUBLANES, C_CONV), F32),
        jax.ShapeDtypeStruct((bsz, SSM_GROUPS, q, SSM_STATE), F32),
    ]
    out_specs = [
        pl.BlockSpec((tl, D_MODEL), lambda i: (out_row(i), 0)),
        pl.BlockSpec((tl * ROW_TILES, LANES), lambda i: (out_row(i), 0)),
        pl.BlockSpec((tl, LANES), lambda i: (out_row(i), 0)),
        pl.BlockSpec((None, SUBLANES, C_CONV), lambda i: (main(i) // nt, 0, 0)),
        pl.BlockSpec((None, SSM_GROUPS, q, SSM_STATE), lambda i: (main(i) // nt, 0, 0, 0)),
    ]
    scratch = [
        pltpu.VMEM((tl + SUBLANES, C_CONV), F32),
        pltpu.VMEM((SSM_GROUPS, SSM_STATE, q), F32),
        pltpu.VMEM((tl, D_SSM), F32),
        pltpu.VMEM((tl, D_SSM + D_SG), BF16),
        pltpu.VMEM((tl, D_MODEL), F32),
    ]
    return pl.pallas_call(
        functools.partial(_prompt_mixer_kernel, tl=tl, nt=nt, n_main=n_main),
        grid=(n_main + 1 + n_s // tl,),
        in_specs=in_specs,
        out_specs=out_specs,
        out_shape=out_shape,
        scratch_shapes=scratch,
        compiler_params=pltpu.CompilerParams(
            dimension_semantics=("arbitrary",), vmem_limit_bytes=VMEM_LIMIT),
        name="prompt_mixer",
    )(x2, x2, *weights, h_s, hn2_s, route_s)


def _sample_front_kernel(
        x_ref, cprev_ref, nmw_ref, wz_ref, wxbc_ref, wdt_ref, wuv_ref, convw_ref, convb_ref,
        dtb_ref, alog_ref, dskip_ref, sgnw_ref, w4_ref, b4_ref, expand_ref,
        z_ref, ypart_ref, ea_ref, ysg_ref, v_ref, convn_ref, c_ref, b_ref, xw_ref, dec_ref,
        *, nb, nt):
    x = x_ref[...]
    hn = _rms(x, nmw_ref[...]).astype(BF16)
    z_ref[...] = _dot(hn, wz_ref[...])
    xbc = _dot(hn, wxbc_ref[...])
    dtr = _dot(hn, wdt_ref[...])
    uv = _gelu(_dot(hn, wuv_ref[...]))

    slab = lambda a, t: a[t * nb:(t + 1) * nb]
    full = [cprev_ref[k] for k in range(CONV_K - 1)] + [slab(xbc, t) for t in range(nt)]
    for k in range(CONV_K - 1):
        convn_ref[k] = full[nt + k]
    xs, bm, cm = [], [], []
    for t in range(nt):
        conv = convb_ref[...]
        for k in range(CONV_K):
            conv = conv + convw_ref[k:k + 1, :] * full[t + k]
        act = _silu(conv)
        xs.append(act[:, :D_SSM])
        bm.append(act[:, D_SSM:D_SSM + SSM_GROUPS * SSM_STATE])
        cm.append(act[:, D_SSM + SSM_GROUPS * SSM_STATE:])
        b_ref[t * nb:(t + 1) * nb, :] = bm[t]
        c_ref[t * nb:(t + 1) * nb, :] = cm[t]

    dt = _softplus(dtr + dtb_ref[...])
    a_row = -jnp.exp(alog_ref[...])
    dts = [slab(dt, t) for t in range(nt)]
    cum = []
    for t in range(nt):
        da = dts[t] * a_row
        cum.append(da if t == 0 else cum[t - 1] + da)
    dec_ref[...] = jnp.exp(cum[nt - 1])

    lane = lax.broadcasted_iota(jnp.int32, (nb, LANES), 1)
    first_group = lane < HEADS_PER_GROUP
    facs = [jnp.exp(cum[t]) for t in range(nt)]
    facs += [dts[s] * jnp.exp(cum[nt - 1] - cum[s]) for s in range(nt)]
    pairs = []
    for t in range(nt):
        for s in range(t + 1):
            cb = []
            for g in range(SSM_GROUPS):
                gl = slice(g * SSM_STATE, (g + 1) * SSM_STATE)
                cb.append(jnp.sum(cm[t][:, gl] * bm[s][:, gl], axis=1, keepdims=True))
            cbh = jnp.where(first_group, cb[0], cb[1])
            facs.append(jnp.exp(cum[t] - cum[s]) * dts[s] * cbh)
            pairs.append((t, s))
    fx = _dot_f32(jnp.concatenate(facs, axis=0), expand_ref[...])
    fslab = lambda i: fx[i * nb:(i + 1) * nb]
    for t in range(nt):
        ea_ref[t * nb:(t + 1) * nb, :] = fslab(t)
        xw_ref[t * nb:(t + 1) * nb, :] = xs[t] * fslab(nt + t)
    for t in range(nt):
        acc = dskip_ref[...] * xs[t]
        for i, (tt, s) in enumerate(pairs):
            if tt == t:
                acc = acc + fslab(2 * nt + i) * xs[s]
        ypart_ref[t * nb:(t + 1) * nb, :] = acc

    u = uv[:, :D_SG]
    v = _rms(uv[:, D_SG:], sgnw_ref[...])
    v_ref[...] = v
    for t in range(nt):
        sv = b4_ref[t:t + 1, :]
        for s in range(t + 1):
            sv = sv + w4_ref[t, s:s + 1, :] * slab(v, s)
        ysg_ref[t * nb:(t + 1) * nb, :] = slab(u, t) * sv


def _sample_front(x_slab, cprev, w, nb, nt):
    r = nb * nt
    weights = [w['nmw'], w['wz'], w['wxbc'], w['wdt'], w['wuv'], w['convw'], w['convb'], w['dtb'],
               w['alog'], w['dskip'], w['sgnw'], w['w4'], w['b4'], w['expand']]
    ins = [x_slab, cprev] + weights
    out_shape = [
        jax.ShapeDtypeStruct((r, D_SSM), F32),
        jax.ShapeDtypeStruct((r, D_SSM), F32),
        jax.ShapeDtypeStruct((r, D_SSM), F32),
        jax.ShapeDtypeStruct((r, D_SG), F32),
        jax.ShapeDtypeStruct((r, D_SG), F32),
        jax.ShapeDtypeStruct((CONV_K - 1, nb, C_CONV), F32),
        jax.ShapeDtypeStruct((r, SSM_GROUPS * SSM_STATE), F32),
        jax.ShapeDtypeStruct((r, SSM_GROUPS * SSM_STATE), F32),
        jax.ShapeDtypeStruct((r, D_SSM), F32),
        jax.ShapeDtypeStruct((nb, LANES), F32),
    ]
    return pl.pallas_call(
        functools.partial(_sample_front_kernel, nb=nb, nt=nt),
        grid=(1,),
        in_specs=[_const_spec(a.shape) for a in ins],
        out_specs=[_const_spec(s.shape) for s in out_shape],
        out_shape=out_shape,
        compiler_params=pltpu.CompilerParams(
            dimension_semantics=("arbitrary",), vmem_limit_bytes=VMEM_LIMIT),
        name="sample_front",
    )(*ins)


def _sample_state_kernel(dec_ref, st_ref, cq_ref, bq_ref, xw_ref, so_ref, z_ref, *, seqs):
    for bb in range(seqs):
        for g in range(SSM_GROUPS):
            s0 = st_ref[bb, g]
            z_ref[bb, g] = lax.dot_general(
                cq_ref[bb, g].astype(BF16), s0.astype(BF16), (((1,), (1,)), ((), ())),
                preferred_element_type=F32)
            upd = lax.dot_general(
                xw_ref[bb, g].astype(BF16), bq_ref[bb, g].astype(BF16), (((0,), (0,)), ((), ())),
                preferred_element_type=F32)
            for hh in range(HEADS_PER_GROUP):
                hs = slice(hh * SSM_HEAD_DIM, (hh + 1) * SSM_HEAD_DIM)
                so_ref[bb, g, hs, :] = s0[hs] * dec_ref[bb, g * HEADS_PER_GROUP + hh] + upd[hs]


def _sample_state(dec, state, cq, bq, xwq, seqs):
    nb = state.shape[0]
    q = HEADS_PER_GROUP * SSM_HEAD_DIM
    blk = lambda *tail: pl.BlockSpec((seqs, SSM_GROUPS) + tail, lambda i: (i, 0, 0, 0))
    return pl.pallas_call(
        functools.partial(_sample_state_kernel, seqs=seqs),
        grid=(nb // seqs,),
        in_specs=[
            pl.BlockSpec((seqs, SSM_HEADS), lambda i: (i, 0), memory_space=pltpu.SMEM),
            blk(q, SSM_STATE), blk(SUBLANES, SSM_STATE), blk(SUBLANES, SSM_STATE), blk(SUBLANES, q),
        ],
        out_specs=[blk(q, SSM_STATE), blk(SUBLANES, q)],
        out_shape=[
            jax.ShapeDtypeStruct((nb, SSM_GROUPS, q, SSM_STATE), F32),
            jax.ShapeDtypeStruct((nb, SSM_GROUPS, SUBLANES, q), F32),
        ],
        compiler_params=pltpu.CompilerParams(
            dimension_semantics=("arbitrary",), vmem_limit_bytes=VMEM_LIMIT),
        name="sample_state",
    )(dec, state, cq, bq, xwq)


def _sample_back_kernel(x_ref, ypart_ref, ea_ref, zoff_ref, z_ref, ysg_ref, ssmnw_ref, wout_ref,
                        nfw_ref, wr_ref, br_ref, h_ref, hn2_ref, route_ref):
    y = ypart_ref[...] + ea_ref[...] * zoff_ref[...]
    y_ssm = _gated_group_norm(y, z_ref[...], ssmnw_ref[...])
    mix_in = jnp.concatenate([y_ssm.astype(BF16), ysg_ref[...].astype(BF16)], axis=1)
    h = x_ref[...] + _dot(mix_in, wout_ref[...])
    h_ref[...] = h
    hn2, route = _ffn_front(h, nfw_ref[...], wr_ref, br_ref[...])
    _store_row_tiles(hn2_ref, hn2)
    route_ref[...] = route


def _sample_back(x_slab, ypart, ea, zoff, z, ysg, w):
    r = x_slab.shape[0]
    ins = [x_slab, ypart, ea, zoff, z, ysg, w['ssmnw'], w['wout'], w['nfw'], w['wr'], w['br']]
    out_shape = [
        jax.ShapeDtypeStruct((r, D_MODEL), F32),
        jax.ShapeDtypeStruct((r * ROW_TILES, LANES), F32),
        jax.ShapeDtypeStruct((r, LANES), F32),
    ]
    return pl.pallas_call(
        _sample_back_kernel,
        grid=(1,),
        in_specs=[_const_spec(a.shape) for a in ins],
        out_specs=[_const_spec(s.shape) for s in out_shape],
        out_shape=out_shape,
        compiler_params=pltpu.CompilerParams(
            dimension_semantics=("arbitrary",), vmem_limit_bytes=VMEM_LIMIT),
        name="sample_back",
    )(*ins)


EXPERT_PIECES = 8
DMA_PIECES = 2


def _expert_mlp_kernel(be_ref, nu_ref, tok0_ref, tokn_ref, dstp_ref, x_hbm, wg_ref, wu_ref, wd_ref,
                       parts_hbm, xbuf, obuf, xb, hb, wg_b, wu_b, wd_b, gsem, ssem, *, bm):
    j = pl.program_id(0)
    nu = nu_ref[0]
    nt = ROW_TILES
    tile = lambda ref, start: ref.at[pl.ds(pl.multiple_of(start, nt), nt)]

    def gather_copy(idx_ref, s, r):
        return pltpu.make_async_copy(tile(x_hbm, idx_ref[0, r]), xbuf.at[s, pl.ds(r * nt, nt)], gsem.at[s])

    def scatter_copy(s, r):
        return pltpu.make_async_copy(obuf.at[s, pl.ds(r * nt, nt)], tile(parts_hbm, dstp_ref[0, r]),
                                     ssem.at[s])

    def gather_wait(s):
        pltpu.make_async_copy(x_hbm.at[pl.ds(0, bm * nt)], xbuf.at[s], gsem.at[s]).wait()

    def scatter_wait(s):
        pltpu.make_async_copy(obuf.at[s], parts_hbm.at[pl.ds(0, bm * nt)], ssem.at[s]).wait()

    def step(slot):
        gather_wait(slot)
        per = bm // DMA_PIECES

        def start_rows(piece):
            if piece >= DMA_PIECES:
                return
            for r in range(piece * per, (piece + 1) * per):
                gather_copy(tokn_ref, 1 - slot, r).start(priority=r % 2)
                scatter_copy(1 - slot, r).start(priority=(r + 1) % 2)

        xb[...] = _load_row_tiles(xbuf.at[slot]).astype(BF16)
        half = D_FF // 2
        for c in range(2):
            cols = slice(c * half, (c + 1) * half)
            start_rows(2 * c)
            g = _dot(xb[...], wg_b[:, cols])
            start_rows(2 * c + 1)
            u = _dot(xb[...], wu_b[:, cols])
            hb[:, cols] = (_silu(g) * u).astype(BF16)
        quarter = D_MODEL // 4
        scatter_wait(slot)
        for c in range(4):
            start_rows(4 + c)
            o = _dot(hb[...], wd_b[:, c * quarter:(c + 1) * quarter])
            for k in range(quarter // LANES):
                kk = c * (quarter // LANES) + k
                obuf.at[slot][pl.ds(kk, bm, stride=nt), :] = o[:, k * LANES:(k + 1) * LANES]

    def drain(slot):
        gather_wait(slot)
        scatter_wait(slot)
        for r in range(bm):
            scatter_copy(1 - slot, r).start(priority=r % 2)
        scatter_wait(1 - slot)

    @pl.when(j == 0)
    def _():
        obuf[...] = jnp.zeros(obuf.shape, F32)
        n_real = parts_hbm.shape[0] - 2 * bm * nt
        pltpu.make_async_copy(obuf.at[0], parts_hbm.at[pl.ds(n_real, bm * nt)], ssem.at[0]).start()

        def start0(r, carry):
            pltpu.make_async_copy(tile(x_hbm, tok0_ref[0, r]), xbuf.at[0, pl.ds(pl.multiple_of(r * nt, nt), nt)],
                                  gsem.at[0]).start()
            return carry
        lax.fori_loop(0, bm, start0, 0)

    used = j < nu

    @pl.when(used & ((j == 0) | (be_ref[j] != be_ref[jnp.maximum(j - 1, 0)])))
    def _():
        wg_b[...] = wg_ref[...].astype(BF16)
        wu_b[...] = wu_ref[...].astype(BF16)
        wd_b[...] = wd_ref[...].astype(BF16)

    for slot in range(2):
        parity = lax.rem(j, 2) == slot

        @pl.when(used & parity)
        def _(slot=slot):
            step(slot)

        @pl.when((j == nu) & parity)
        def _(slot=slot):
            drain(slot)


def _expert_mlp(blk_expert, n_used, tok, dst, x, wg, wu, wd, bm, n_out):
    n_blocks = blk_expert.shape[0]
    assert bm % EXPERT_PIECES == 0
    w_map = lambda j, be, nu: (be[j], 0, 0)
    idx_spec = lambda f: pl.BlockSpec((None, 1, bm), lambda j, be, nu: (f(j), 0, 0),
                                      memory_space=pltpu.SMEM)
    any_spec = pl.BlockSpec(memory_space=pl.ANY)
    grid_spec = pltpu.PrefetchScalarGridSpec(
        num_scalar_prefetch=2,
        grid=(n_blocks + 1,),
        in_specs=[
            idx_spec(lambda j: 0),
            idx_spec(lambda j: jnp.minimum(j + 1, n_blocks - 1)),
            idx_spec(lambda j: j),
            any_spec,
            pl.BlockSpec((None, D_MODEL, D_FF), w_map),
            pl.BlockSpec((None, D_MODEL, D_FF), w_map),
            pl.BlockSpec((None, D_FF, D_MODEL), w_map),
        ],
        out_specs=any_spec,
        scratch_shapes=[
            pltpu.VMEM((2, bm * ROW_TILES, LANES), F32),
            pltpu.VMEM((2, bm * ROW_TILES, LANES), F32),
            pltpu.VMEM((bm, D_MODEL), BF16),
            pltpu.VMEM((bm, D_FF), BF16),
            pltpu.VMEM((D_MODEL, D_FF), BF16),
            pltpu.VMEM((D_MODEL, D_FF), BF16),
            pltpu.VMEM((D_FF, D_MODEL), BF16),
            pltpu.SemaphoreType.DMA((2,)),
            pltpu.SemaphoreType.DMA((2,)),
        ],
    )
    tok3 = (tok * ROW_TILES).reshape(n_blocks, 1, bm)
    spare = (n_out - bm + jnp.arange(bm, dtype=jnp.int32)).reshape(1, bm)
    dst_prev = (jnp.concatenate([spare, dst], axis=0) * ROW_TILES).reshape(n_blocks + 1, 1, bm)
    blk_expert = jnp.concatenate([blk_expert, blk_expert[-1:]])
    return pl.pallas_call(
        functools.partial(_expert_mlp_kernel, bm=bm),
        grid_spec=grid_spec,
        out_shape=jax.ShapeDtypeStruct((n_out * ROW_TILES, LANES), F32),
        compiler_params=pltpu.CompilerParams(
            dimension_semantics=("arbitrary",), vmem_limit_bytes=VMEM_LIMIT),
        name="expert_mlp",
    )(blk_expert, n_used, tok3, tok3, dst_prev, x, wg, wu, wd)


PAD_ID = 2 ** 16 - 1


def _moe_plan(e_idx, bm, n_blocks):
    n_tok = e_idx.shape[0]
    n_asg = 2 * n_tok
    n_pad = n_blocks * bm - n_asg
    e_flat = e_idx.reshape(-1)
    experts = jnp.arange(N_EXPERTS, dtype=jnp.int32)
    counts = jnp.sum((e_flat[:, None] == experts[None, :]).astype(jnp.int32), axis=0)
    nblk = (counts + bm - 1) // bm
    blk_end = jnp.cumsum(nblk)
    n_used = blk_end[-1]
    j = jnp.arange(n_blocks, dtype=jnp.int32)
    be = jnp.sum((blk_end[None, :] <= jnp.minimum(j, n_used - 1)[:, None]).astype(jnp.int32), axis=1)
    be = jnp.minimum(be, N_EXPERTS - 1)
    assert n_asg < PAD_ID
    pad_end = jnp.cumsum(nblk * bm - counts)
    pad_expert = jnp.sum((pad_end[None, :] <= jnp.arange(n_pad, dtype=jnp.int32)[:, None]).astype(jnp.int32),
                         axis=1)
    keys = jnp.concatenate([2 * e_flat, 2 * pad_expert + 1])
    ids = jnp.concatenate([jnp.arange(n_asg, dtype=jnp.int32), jnp.full((n_pad,), PAD_ID, jnp.int32)])
    window = lax.sort(keys * (PAD_ID + 1) + ids) % (PAD_ID + 1)
    window = window.reshape(n_blocks, bm)
    valid = window != PAD_ID
    q = jnp.arange(bm, dtype=jnp.int32)
    tok = jnp.where(valid, window // 2, (j[:, None] * bm + q[None, :]) % n_tok)
    dst = jnp.where(valid, (window % 2) * n_tok + window // 2,
                    2 * n_tok + (j[:, None] % 2) * bm + q[None, :])
    return tok.astype(jnp.int32), dst.astype(jnp.int32), be, n_used.astype(jnp.int32).reshape(1)


def _final_kernel(h_ref, p0_ref, p1_ref, route_ref, w_ref, op_ref, os_ref, *, n_prompt_steps):
    route = route_ref[...]
    p0 = _load_row_tiles(p0_ref)
    p1 = _load_row_tiles(p1_ref)
    y = _rms(h_ref[...] + route[:, 2:3] * p0 + route[:, 3:4] * p1, w_ref[...])
    i = pl.program_id(0)

    @pl.when(i < n_prompt_steps)
    def _():
        op_ref[...] = y

    @pl.when(i >= n_prompt_steps)
    def _():
        os_ref[...] = y


def _final(h, parts, route, w, tf, n_prompt):
    n_tok = h.shape[0]
    assert n_tok % tf == 0 and n_prompt % tf == 0
    n_p = n_prompt // tf
    row = lambda d: pl.BlockSpec((tf, d), lambda i: (i, 0))
    part = lambda k: pl.BlockSpec((tf * ROW_TILES, LANES), lambda i: (k * (n_tok // tf) + i, 0))
    return pl.pallas_call(
        functools.partial(_final_kernel, n_prompt_steps=n_p),
        grid=(n_tok // tf,),
        in_specs=[row(D_MODEL), part(0), part(1), row(LANES), _const_spec(w.shape)],
        out_specs=[pl.BlockSpec((tf, D_MODEL), lambda i: (jnp.minimum(i, n_p - 1), 0)),
                   pl.BlockSpec((tf, D_MODEL), lambda i: (jnp.maximum(i - n_p, 0), 0))],
        out_shape=[jax.ShapeDtypeStruct((n_prompt, D_MODEL), F32),
                   jax.ShapeDtypeStruct((n_tok - n_prompt, D_MODEL), F32)],
        compiler_params=pltpu.CompilerParams(
            dimension_semantics=("arbitrary",), vmem_limit_bytes=VMEM_LIMIT),
        name="final_combine",
    )(h, parts, parts, route, w)


PROMPT_TILE = 256
MOE_BLOCK_ROWS = 512
FINAL_TILE = 512
STATE_SEQS = 16


def _pad_lanes(a, width=LANES):
    return jnp.pad(a, [(0, 0)] * (a.ndim - 1) + [(0, width - a.shape[-1])])


def _layer_weights(l, norm_mix_w, w_in, conv_w, conv_b, dt_bias, a_log, d_skip, ssm_norm_w, sg_norm_w,
                   sg_w, sg_b, w_out, norm_ffn_w, w_router_group, b_router_group, w_router_expert,
                   b_router_expert, n_dec):
    c0, c1, c2 = D_SSM, D_SSM + C_CONV, D_SSM + C_CONV + SSM_HEADS
    wi = w_in[l]
    causal = jnp.tril(jnp.ones((CHUNK, CHUNK), bool))
    sgw_l = jnp.where(causal, sg_w[l], 0.0)
    head_of_lane = jnp.arange(D_SSM, dtype=jnp.int32) // SSM_HEAD_DIM
    expand = (jnp.arange(LANES, dtype=jnp.int32)[:, None] == head_of_lane[None, :]).astype(F32)
    row = lambda a: a.reshape(1, -1).astype(F32)
    gap = ROUTER_EXPERT_ROW - N_EXPERT_GROUPS
    rest = LANES - ROUTER_EXPERT_ROW - N_EXPERTS
    wrt = jnp.concatenate([w_router_group[l].T, jnp.zeros((gap, D_MODEL), F32), w_router_expert[l].T,
                           jnp.zeros((rest, D_MODEL), F32)], axis=0).astype(F32)
    brc = jnp.concatenate([b_router_group[l], jnp.zeros((gap,), F32), b_router_expert[l],
                           jnp.zeros((rest,), F32)]).astype(F32).reshape(LANES, 1)
    return {
        'nmw': row(norm_mix_w[l]),
        'wz': wi[:, :c0].astype(BF16),
        'wxbc': wi[:, c0:c1].astype(BF16),
        'wdt': _pad_lanes(wi[:, c1:c2]).astype(BF16),
        'wuv': wi[:, c2:].astype(BF16),
        'convw': conv_w[l].astype(F32),
        'convb': row(conv_b[l]),
        'dtb': _pad_lanes(row(dt_bias[l])),
        'alog': _pad_lanes(row(a_log[l])),
        'dskip': row(jnp.repeat(d_skip[l], SSM_HEAD_DIM)),
        'ssmnw': row(ssm_norm_w[l]),
        'sgnw': row(sg_norm_w[l]),
        'sgw': sgw_l.astype(BF16),
        'sgb': jnp.repeat(sg_b[l].T, SG_HEAD_DIM, axis=1).astype(F32),
        'w4': jnp.repeat(jnp.transpose(sgw_l[:, :n_dec, :n_dec], (1, 2, 0)), SG_HEAD_DIM, axis=2).astype(F32),
        'b4': jnp.repeat(sg_b[l][:, :n_dec].T, SG_HEAD_DIM, axis=1).astype(F32),
        'expand': expand,
        'wout': w_out[l].astype(BF16),
        'nfw': row(norm_ffn_w[l]),
        'wr': jnp.stack([wrt.astype(BF16), (wrt - wrt.astype(BF16).astype(F32)).astype(BF16)]),
        'br': brc,
    }


def _layer(l, xp, xs_slab, state_conv, state_ssm, w, w_gate, w_up, w_down):
    bp, seq, _ = xp.shape
    nb, nt = state_conv.shape[1], xs_slab.shape[0] // state_conv.shape[1]
    q = HEADS_PER_GROUP * SSM_HEAD_DIM

    cprev = jnp.transpose(state_conv[l], (1, 0, 2))
    z, ypart, ea, ysg, v, convn, cmat, bmat, xw, dec = _sample_front(xs_slab, cprev, w, nb, nt)
    to_seq = lambda a, d: jnp.pad(
        jnp.transpose(a.reshape(nt, nb, SSM_GROUPS, d), (1, 2, 0, 3)),
        ((0, 0), (0, 0), (0, SUBLANES - nt), (0, 0)))
    ssm_s, zoff = _sample_state(
        dec[:, :SSM_HEADS], state_ssm[l].reshape(nb, SSM_GROUPS, q, SSM_STATE),
        to_seq(cmat, SSM_STATE), to_seq(bmat, SSM_STATE), to_seq(xw, q), STATE_SEQS)
    zoff = jnp.transpose(zoff[:, :, :nt], (2, 0, 1, 3)).reshape(nt * nb, D_SSM)
    h_s, hn2_s, route_s = _sample_back(xs_slab, ypart, ea, zoff, z, ysg, w)

    h, hn2, route, convt_p, ssm_p = _prompt_mixer(xp, w, PROMPT_TILE, h_s, hn2_s, route_s)

    n_tok = h.shape[0]
    bm = MOE_BLOCK_ROWS
    n_blocks = -(-(2 * n_tok) // bm) + N_EXPERTS
    tok, dst, blk_expert, n_used = _moe_plan(route[:, :2].astype(jnp.int32), bm, n_blocks)
    parts = _expert_mlp(blk_expert, n_used, tok, dst, hn2, w_gate[l], w_up[l], w_down[l], bm,
                        2 * n_tok + 2 * bm)
    outs = dict(
        h=h, parts=parts, route=route,
        conv_p=convt_p[:, SUBLANES - (CONV_K - 1):],
        ssm_p=ssm_p.reshape(bp, SSM_GROUPS, HEADS_PER_GROUP, SSM_HEAD_DIM, SSM_STATE),
        conv_s=jnp.transpose(convn, (1, 0, 2)),
        ssm_s=ssm_s.reshape(nb, SSM_GROUPS, HEADS_PER_GROUP, SSM_HEAD_DIM, SSM_STATE),
        v_s=jnp.transpose(v.reshape(nt, nb, D_SG), (1, 0, 2)),
    )
    return outs


def kernel(x_prompt, x_sample, state_conv, state_ssm, norm_mix_w, w_in, conv_w, conv_b, dt_bias, a_log, d_skip, ssm_norm_w, sg_norm_w, sg_w, sg_b, w_out, norm_ffn_w, w_router_group, b_router_group, w_router_expert, b_router_expert, w_gate, w_up, w_down, norm_final_w):
    depth = w_in.shape[0]
    assert depth == 1, "the fused final norm assumes a single layer"
    bp, seq, _ = x_prompt.shape
    nb, nt, _ = x_sample.shape
    l = 0
    w = _layer_weights(l, norm_mix_w, w_in, conv_w, conv_b, dt_bias, a_log, d_skip, ssm_norm_w, sg_norm_w,
                       sg_w, sg_b, w_out, norm_ffn_w, w_router_group, b_router_group, w_router_expert,
                       b_router_expert, nt)
    xs_slab = jnp.transpose(x_sample, (1, 0, 2)).reshape(nt * nb, D_MODEL)
    outs = _layer(l, x_prompt, xs_slab, state_conv, state_ssm, w, w_gate, w_up, w_down)
    y_p, y_s = _final(outs['h'], outs['parts'], outs['route'], norm_final_w.reshape(1, -1).astype(F32),
                      FINAL_TILE, bp * seq)
    y_prompt = y_p.reshape(bp, seq, D_MODEL)
    y_sample = jnp.transpose(y_s.reshape(nt, nb, D_MODEL), (1, 0, 2))
    return (y_prompt, y_sample, outs['conv_p'][None], outs['ssm_p'][None], outs['conv_s'][None],
            outs['ssm_s'][None], outs['v_s'][None])
```

```python
import functools

import jax
import jax.numpy as jnp
from jax import lax
from jax.experimental import pallas as pl
from jax.experimental.pallas import tpu as pltpu

D_MODEL = 1024
D_SSM = 1024
SSM_HEAD_DIM = 64
SSM_HEADS = 16
SSM_GROUPS = 2
HEADS_PER_GROUP = 8
SSM_STATE = 128
CONV_K = 4
C_CONV = D_SSM + 2 * SSM_GROUPS * SSM_STATE
D_SG = 1024
SG_HEADS = 8
SG_HEAD_DIM = 128
CHUNK = 128
N_EXPERT_GROUPS = 4
EXPERTS_PER_GROUP = 8
N_EXPERTS = 32
D_FF = 512
EPS = 1e-6
LOG2_E = 1.4426950408889634

LANES = 128
SUBLANES = 8
VMEM_LIMIT = 56 * 1024 * 1024

F32 = jnp.float32
BF16 = jnp.bfloat16
HIGHEST = lax.Precision.HIGHEST


def _dot(a, b):
    return jnp.dot(a, b, preferred_element_type=F32)


def _dot_f32(a, b):
    return jnp.dot(a, b, preferred_element_type=F32, precision=HIGHEST)


def _rms(x, w):
    return x * lax.rsqrt(jnp.mean(x * x, axis=-1, keepdims=True) + EPS) * w


def _silu(x):
    return x * (0.5 + 0.5 * jnp.tanh(0.5 * x))


def _gelu(x):
    return 0.5 * x * (1.0 + lax.erf(x * 0.7071067811865476))


def _softplus(x):
    return jnp.maximum(x, 0.0) + jnp.log1p(jnp.exp(-jnp.abs(x)))


def _gated_group_norm(y, z, w):
    g = y * _silu(z)
    half = D_SSM // SSM_GROUPS
    parts = []
    for k in range(SSM_GROUPS):
        gk = g[:, k * half:(k + 1) * half]
        parts.append(gk * lax.rsqrt(jnp.mean(gk * gk, axis=-1, keepdims=True) + EPS))
    return jnp.concatenate(parts, axis=1) * w


ROUTER_GROUP_ROW = 0
ROUTER_EXPERT_ROW = SUBLANES


def _route_t(lt):
    r = lt.shape[1]
    row = lax.broadcasted_iota(jnp.int32, (SUBLANES, r), 0)
    rowf = row.astype(F32)
    big = float(SUBLANES)
    red = lambda f, a: f(a, axis=0, keepdims=True)
    gl = jnp.where(row < N_EXPERT_GROUPS, lt[ROUTER_GROUP_ROW:ROUTER_GROUP_ROW + SUBLANES], -jnp.inf)
    ge = jnp.exp(gl - red(jnp.max, gl))
    p_grp = ge / red(jnp.sum, ge)
    g_p = red(jnp.max, p_grp)
    g_idx = red(jnp.min, jnp.where(p_grp == g_p, rowf, big))
    el = lt[ROUTER_EXPERT_ROW:ROUTER_EXPERT_ROW + EXPERTS_PER_GROUP]
    for g in range(1, N_EXPERT_GROUPS):
        lo = ROUTER_EXPERT_ROW + g * EXPERTS_PER_GROUP
        el = jnp.where(g_idx == float(g), lt[lo:lo + EXPERTS_PER_GROUP], el)
    ee = jnp.exp(el - red(jnp.max, el))
    pe = ee / red(jnp.sum, ee)
    v1 = red(jnp.max, pe)
    i1 = red(jnp.min, jnp.where(pe == v1, rowf, big))
    rest = rowf != i1
    pe2 = jnp.where(rest, pe, -1.0)
    v2 = red(jnp.max, pe2)
    i2 = red(jnp.min, jnp.where(rest & (pe2 == v2), rowf, big))
    den = v1 + v2
    base = g_idx * float(EXPERTS_PER_GROUP)
    out = jnp.where(row == 0, base + i1, 0.0)
    out = jnp.where(row == 1, base + i2, out)
    out = jnp.where(row == 2, g_p * v1 / den, out)
    out = jnp.where(row == 3, g_p * v2 / den, out)
    return jnp.concatenate([out, jnp.zeros((LANES - SUBLANES, r), F32)], axis=0).T


def _ffn_front(h, nfw, wrt_ref, brc):
    hn2 = _rms(h, nfw)
    hi = hn2.astype(BF16)
    lo = (hn2 - hi.astype(F32)).astype(BF16)
    nt = lambda a, b: lax.dot_general(a, b, (((1,), (1,)), ((), ())), preferred_element_type=F32)
    lt = nt(wrt_ref[0], hi) + nt(wrt_ref[0], lo) + nt(wrt_ref[1], hi) + brc
    return hn2, _route_t(lt)


ROW_TILES = D_MODEL // LANES


def _store_row_tiles(ref, val):
    r = val.shape[0]
    for k in range(ROW_TILES):
        ref[pl.ds(k, r, stride=ROW_TILES), :] = val[:, k * LANES:(k + 1) * LANES]


def _load_row_tiles(ref):
    r = ref.shape[0] // ROW_TILES
    return jnp.concatenate([ref[pl.ds(k, r, stride=ROW_TILES), :] for k in range(ROW_TILES)], axis=1)


def _prompt_step(
        t, nt, fillers, x_ref, nmw_ref, wz_ref, wxbc_ref, wdt_ref, wuv_ref, convw_ref, convb_ref, dtb_ref,
        alog_ref, dskip_ref, ssmnw_ref, sgnw_ref, sgw_ref, sgb_ref, convt_ref, ssm_ref,
        xbc_scr, state_scr, y_scr, mix_scr, tl):
    fillers = list(fillers)
    assert len(fillers) == 3 + tl // CHUNK

    @pl.when(t == 0)
    def _():
        xbc_scr[0:SUBLANES, :] = jnp.zeros((SUBLANES, C_CONV), F32)
        state_scr[...] = jnp.zeros(state_scr.shape, F32)

    x = x_ref[...]
    hn = _rms(x, nmw_ref[...]).astype(BF16)
    z = _dot(hn, wz_ref[...])
    xbc = _dot(hn, wxbc_ref[...])
    dtr = _dot(hn, wdt_ref[...])
    uv = _gelu(_dot(hn, wuv_ref[...]))
    fillers.pop(0)()

    xbc_scr[SUBLANES:SUBLANES + tl, :] = xbc
    conv = convb_ref[...]
    for k in range(CONV_K):
        off = SUBLANES - (CONV_K - 1) + k
        conv = conv + convw_ref[k:k + 1, :] * xbc_scr[off:off + tl, :]
    tail = xbc_scr[tl:tl + SUBLANES, :]
    xbc_scr[0:SUBLANES, :] = tail
    convt_ref[...] = tail
    act = _silu(conv)
    xs = act[:, :D_SSM]
    bmat = act[:, D_SSM:D_SSM + SSM_GROUPS * SSM_STATE]
    cmat = act[:, D_SSM + SSM_GROUPS * SSM_STATE:]
    fillers.pop(0)()

    dt = _softplus(dtr + dtb_ref[...])
    a_row = -jnp.exp(alog_ref[...]) * LOG2_E

    li = lax.broadcasted_iota(jnp.int32, (CHUNK, CHUNK), 0)
    si = lax.broadcasted_iota(jnp.int32, (CHUNK, CHUNK), 1)
    causal = li >= si
    tri = jnp.where(causal, 1.0, 0.0).astype(F32)
    lo_half = si < SSM_HEAD_DIM

    for c in range(tl // CHUNK):
        rows = slice(c * CHUNK, (c + 1) * CHUNK)
        dt_c = dt[rows]
        acum = _dot_f32(tri, dt_c * a_row)
        acum_t = acum.T
        dt_t = dt_c.T
        w_t = dt_t * jnp.exp2(acum_t[:, CHUNK - 1:CHUNK] - acum_t)
        xs_c = xs[rows]
        for g in range(SSM_GROUPS):
            gl = slice(g * SSM_STATE, (g + 1) * SSM_STATE)
            b_g = bmat[rows, gl]
            c_g = cmat[rows, gl]
            b_gt = b_g.T
            cb = _dot(c_g.astype(BF16), b_gt.astype(BF16))
            for j in range(HEADS_PER_GROUP // 2):
                pl_ = slice(g * 512 + j * LANES, g * 512 + (j + 1) * LANES)
                m_l, ec_l, s_l, ea_last = [], [], [], []
                for k in range(2):
                    hh = g * HEADS_PER_GROUP + 2 * j + k
                    colb = jnp.broadcast_to(acum[:, hh:hh + 1], (CHUNK, CHUNK))
                    rowb = jnp.broadcast_to(acum_t[hh:hh + 1, :], (CHUNK, CHUNK))
                    ea = jnp.exp2(colb)
                    lmat = jnp.where(causal, jnp.exp2(colb - rowb), 0.0)
                    m_l.append(cb * lmat * jnp.broadcast_to(dt_t[hh:hh + 1, :], (CHUNK, CHUNK)))
                    ec_l.append(c_g * ea)
                    s_l.append(b_gt * jnp.broadcast_to(w_t[hh:hh + 1, :], (CHUNK, CHUNK)))
                    ea_last.append(ea[CHUNK - 1:CHUNK, :])
                xs_p = xs_c[:, pl_]
                st_p = state_scr[g, :, j * LANES:(j + 1) * LANES]
                rx = jnp.concatenate([jnp.where(lo_half, xs_p, 0.0),
                                      jnp.where(lo_half, 0.0, xs_p)], axis=0).astype(BF16)
                rs = jnp.concatenate([jnp.where(lo_half, st_p, 0.0),
                                      jnp.where(lo_half, 0.0, st_p)], axis=0).astype(BF16)
                lhs = jnp.concatenate(m_l + ec_l, axis=1).astype(BF16)
                y_p = _dot(lhs, jnp.concatenate([rx, rs], axis=0))
                y_scr[rows, pl_] = y_p
                dec = jnp.where(lo_half[0:1, :], ea_last[0], ea_last[1])
                upd = _dot(jnp.concatenate(s_l, axis=1).astype(BF16), rx)
                state_scr[g, :, j * LANES:(j + 1) * LANES] = st_p * dec + upd
        fillers.pop(0)()

    y = y_scr[...] + dskip_ref[...] * xs
    y_ssm = _gated_group_norm(y, z, ssmnw_ref[...])
    mix_scr[:, :D_SSM] = y_ssm.astype(BF16)

    u = uv[:, :D_SG]
    v = _rms(uv[:, D_SG:], sgnw_ref[...])
    vb = v.astype(BF16)
    for c in range(tl // CHUNK):
        rows = slice(c * CHUNK, (c + 1) * CHUNK)
        for hd in range(SG_HEADS):
            hl = slice(hd * SG_HEAD_DIM, (hd + 1) * SG_HEAD_DIM)
            sv = _dot(sgw_ref[hd], vb[rows, hl]) + sgb_ref[:, hl]
            mix_scr[rows, D_SSM + hd * SG_HEAD_DIM:D_SSM + (hd + 1) * SG_HEAD_DIM] = (
                u[rows, hl] * sv).astype(BF16)

    fillers.pop(0)()

    @pl.when(t == nt - 1)
    def _():
        for g in range(SSM_GROUPS):
            ssm_ref[g] = state_scr[g].T


def _prompt_mixer_kernel(
        x_ref, xp_ref, nmw_ref, wz_ref, wxbc_ref, wdt_ref, wuv_ref, convw_ref, convb_ref, dtb_ref,
        alog_ref, dskip_ref, ssmnw_ref, sgnw_ref, sgw_ref, sgb_ref, wout_ref, nfw_ref,
        wr_ref, br_ref, hs_ref, hn2s_ref, routes_ref,
        h_ref, hn2_ref, route_ref, convt_ref, ssm_ref,
        xbc_scr, state_scr, y_scr, mix_scr, h_scr, *, tl, nt, n_main):
    i = pl.program_id(0)

    @pl.when(i == 0)
    def _():
        mix_scr[...] = jnp.zeros(mix_scr.shape, BF16)

    @pl.when(i > n_main)
    def _():
        h_ref[...] = hs_ref[...]
        hn2_ref[...] = hn2s_ref[...]
        route_ref[...] = routes_ref[...]

    n_out_chunks = 2 + tl // CHUNK
    width = D_MODEL // n_out_chunks
    assert width % LANES == 0 and width * n_out_chunks == D_MODEL

    def out_chunk(c):
        cols = slice(c * width, (c + 1) * width)
        h_scr[:, cols] = xp_ref[:, cols] + _dot(mix_scr[...], wout_ref[:, cols])

    def finish():
        h = h_scr[...]
        h_ref[...] = h
        hn2, route = _ffn_front(h, nfw_ref[...], wr_ref, br_ref[...])
        _store_row_tiles(hn2_ref, hn2)
        route_ref[...] = route

    @pl.when(i <= n_main)
    def _():
        def out_chunks(cs):
            for c in cs:
                out_chunk(c)

        half = n_out_chunks // 2
        fillers = ([functools.partial(out_chunks, range(half)),
                    functools.partial(out_chunks, range(half, n_out_chunks)), finish]
                   + [lambda: None] * (tl // CHUNK))
        _prompt_step(
            lax.rem(i, nt), nt, fillers, x_ref, nmw_ref, wz_ref, wxbc_ref, wdt_ref, wuv_ref,
            convw_ref, convb_ref, dtb_ref, alog_ref, dskip_ref, ssmnw_ref, sgnw_ref, sgw_ref, sgb_ref,
            convt_ref, ssm_ref, xbc_scr, state_scr, y_scr, mix_scr, tl)


def _const_spec(shape):
    zeros = (0,) * len(shape)
    return pl.BlockSpec(shape, lambda *_: zeros)


def _prompt_mixer(x, w, tl, h_s, hn2_s, route_s):
    bsz, seq, _ = x.shape
    nt = seq // tl
    n_main = bsz * nt
    n_s = h_s.shape[0]
    assert seq % tl == 0 and n_s % tl == 0
    n_tok = bsz * seq + n_s
    x2 = x.reshape(bsz * seq, D_MODEL)
    main = lambda i: jnp.minimum(i, n_main - 1)
    prev = lambda i: jnp.clip(i - 1, 0, n_main - 1)
    tail = lambda i: jnp.maximum(i - n_main - 1, 0)
    out_row = lambda i: jnp.maximum(i - 1, 0)
    weights = [w['nmw'], w['wz'], w['wxbc'], w['wdt'], w['wuv'], w['convw'], w['convb'], w['dtb'],
               w['alog'], w['dskip'], w['ssmnw'], w['sgnw'], w['sgw'], w['sgb'], w['wout'], w['nfw'],
               w['wr'], w['br']]
    in_specs = ([pl.BlockSpec((tl, D_MODEL), lambda i: (main(i), 0)),
                 pl.BlockSpec((tl, D_MODEL), lambda i: (prev(i), 0))]
                + [_const_spec(a.shape) for a in weights]
                + [pl.BlockSpec(blk, lambda i: (tail(i), 0))
                   for blk in ((tl, D_MODEL), (tl * ROW_TILES, LANES), (tl, LANES))])
    q = HEADS_PER_GROUP * SSM_HEAD_DIM
    out_shape = [
        jax.ShapeDtypeStruct((n_tok, D_MODEL), F32),
        jax.ShapeDtypeStruct((n_tok * ROW_TILES, LANES), F32),
        jax.ShapeDtypeStruct((n_tok, LANES), F32),
        jax.ShapeDtypeStruct((bsz, SUBLANES, C_CONV), F32),
        jax.ShapeDtypeStruct((bsz, SSM_GROUPS, q, SSM_STATE), F32),
    ]
    out_specs = [
        pl.BlockSpec((tl, D_MODEL), lambda i: (out_row(i), 0)),
        pl.BlockSpec((tl * ROW_TILES, LANES), lambda i: (out_row(i), 0)),
        pl.BlockSpec((tl, LANES), lambda i: (out_row(i), 0)),
        pl.BlockSpec((None, SUBLANES, C_CONV), lambda i: (main(i) // nt, 0, 0)),
        pl.BlockSpec((None, SSM_GROUPS, q, SSM_STATE), lambda i: (main(i) // nt, 0, 0, 0)),
    ]
    scratch = [
        pltpu.VMEM((tl + SUBLANES, C_CONV), F32),
        pltpu.VMEM((SSM_GROUPS, SSM_STATE, q), F32),
        pltpu.VMEM((tl, D_SSM), F32),
        pltpu.VMEM((tl, D_SSM + D_SG), BF16),
        pltpu.VMEM((tl, D_MODEL), F32),
    ]
    return pl.pallas_call(
        functools.partial(_prompt_mixer_kernel, tl=tl, nt=nt, n_main=n_main),
        grid=(n_main + 1 + n_s // tl,),
        in_specs=in_specs,
        out_specs=out_specs,
        out_shape=out_shape,
        scratch_shapes=scratch,
        compiler_params=pltpu.CompilerParams(
            dimension_semantics=("arbitrary",), vmem_limit_bytes=VMEM_LIMIT),
        name="prompt_mixer",
    )(x2, x2, *weights, h_s, hn2_s, route_s)


def _sample_front_kernel(
        x_ref, cprev_ref, nmw_ref, wz_ref, wxbc_ref, wdt_ref, wuv_ref, convw_ref, convb_ref,
        dtb_ref, alog_ref, dskip_ref, sgnw_ref, w4_ref, b4_ref, expand_ref,
        z_ref, ypart_ref, ea_ref, ysg_ref, v_ref, convn_ref, c_ref, b_ref, xw_ref, dec_ref,
        *, nb, nt):
    x = x_ref[...]
    hn = _rms(x, nmw_ref[...]).astype(BF16)
    z_ref[...] = _dot(hn, wz_ref[...])
    xbc = _dot(hn, wxbc_ref[...])
    dtr = _dot(hn, wdt_ref[...])
    uv = _gelu(_dot(hn, wuv_ref[...]))

    slab = lambda a, t: a[t * nb:(t + 1) * nb]
    full = [cprev_ref[k] for k in range(CONV_K - 1)] + [slab(xbc, t) for t in range(nt)]
    for k in range(CONV_K - 1):
        convn_ref[k] = full[nt + k]
    xs, bm, cm = [], [], []
    for t in range(nt):
        conv = convb_ref[...]
        for k in range(CONV_K):
            conv = conv + convw_ref[k:k + 1, :] * full[t + k]
        act = _silu(conv)
        xs.append(act[:, :D_SSM])
        bm.append(act[:, D_SSM:D_SSM + SSM_GROUPS * SSM_STATE])
        cm.append(act[:, D_SSM + SSM_GROUPS * SSM_STATE:])
        b_ref[t * nb:(t + 1) * nb, :] = bm[t]
        c_ref[t * nb:(t + 1) * nb, :] = cm[t]

    dt = _softplus(dtr + dtb_ref[...])
    a_row = -jnp.exp(alog_ref[...])
    dts = [slab(dt, t) for t in range(nt)]
    cum = []
    for t in range(nt):
        da = dts[t] * a_row
        cum.append(da if t == 0 else cum[t - 1] + da)
    dec_ref[...] = jnp.exp(cum[nt - 1])

    lane = lax.broadcasted_iota(jnp.int32, (nb, LANES), 1)
    first_group = lane < HEADS_PER_GROUP
    facs = [jnp.exp(cum[t]) for t in range(nt)]
    facs += [dts[s] * jnp.exp(cum[nt - 1] - cum[s]) for s in range(nt)]
    pairs = []
    for t in range(nt):
        for s in range(t + 1):
            cb = []
            for g in range(SSM_GROUPS):
                gl = slice(g * SSM_STATE, (g + 1) * SSM_STATE)
                cb.append(jnp.sum(cm[t][:, gl] * bm[s][:, gl], axis=1, keepdims=True))
            cbh = jnp.where(first_group, cb[0], cb[1])
            facs.append(jnp.exp(cum[t] - cum[s]) * dts[s] * cbh)
            pairs.append((t, s))
    fx = _dot_f32(jnp.concatenate(facs, axis=0), expand_ref[...])
    fslab = lambda i: fx[i * nb:(i + 1) * nb]
    for t in range(nt):
        ea_ref[t * nb:(t + 1) * nb, :] = fslab(t)
        xw_ref[t * nb:(t + 1) * nb, :] = xs[t] * fslab(nt + t)
    for t in range(nt):
        acc = dskip_ref[...] * xs[t]
        for i, (tt, s) in enumerate(pairs):
            if tt == t:
                acc = acc + fslab(2 * nt + i) * xs[s]
        ypart_ref[t * nb:(t + 1) * nb, :] = acc

    u = uv[:, :D_SG]
    v = _rms(uv[:, D_SG:], sgnw_ref[...])
    v_ref[...] = v
    for t in range(nt):
        sv = b4_ref[t:t + 1, :]
        for s in range(t + 1):
            sv = sv + w4_ref[t, s:s + 1, :] * slab(v, s)
        ysg_ref[t * nb:(t + 1) * nb, :] = slab(u, t) * sv


def _sample_front(x_slab, cprev, w, nb, nt):
    r = nb * nt
    weights = [w['nmw'], w['wz'], w['wxbc'], w['wdt'], w['wuv'], w['convw'], w['convb'], w['dtb'],
               w['alog'], w['dskip'], w['sgnw'], w['w4'], w['b4'], w['expand']]
    ins = [x_slab, cprev] + weights
    out_shape = [
        jax.ShapeDtypeStruct((r, D_SSM), F32),
        jax.ShapeDtypeStruct((r, D_SSM), F32),
        jax.ShapeDtypeStruct((r, D_SSM), F32),
        jax.ShapeDtypeStruct((r, D_SG), F32),
        jax.ShapeDtypeStruct((r, D_SG), F32),
        jax.ShapeDtypeStruct((CONV_K - 1, nb, C_CONV), F32),
        jax.ShapeDtypeStruct((r, SSM_GROUPS * SSM_STATE), F32),
        jax.ShapeDtypeStruct((r, SSM_GROUPS * SSM_STATE), F32),
        jax.ShapeDtypeStruct((r, D_SSM), F32),
        jax.ShapeDtypeStruct((nb, LANES), F32),
    ]
    return pl.pallas_call(
        functools.partial(_sample_front_kernel, nb=nb, nt=nt),
        grid=(1,),
        in_specs=[_const_spec(a.shape) for a in ins],
        out_specs=[_const_spec(s.shape) for s in out_shape],
        out_shape=out_shape,
        compiler_params=pltpu.CompilerParams(
            dimension_semantics=("arbitrary",), vmem_limit_bytes=VMEM_LIMIT),
        name="sample_front",
    )(*ins)


def _sample_state_kernel(dec_ref, st_ref, cq_ref, bq_ref, xw_ref, so_ref, z_ref, *, seqs):
    for bb in range(seqs):
        for g in range(SSM_GROUPS):
            s0 = st_ref[bb, g]
            z_ref[bb, g] = lax.dot_general(
                cq_ref[bb, g].astype(BF16), s0.astype(BF16), (((1,), (1,)), ((), ())),
                preferred_element_type=F32)
            upd = lax.dot_general(
                xw_ref[bb, g].astype(BF16), bq_ref[bb, g].astype(BF16), (((0,), (0,)), ((), ())),
                preferred_element_type=F32)
            for hh in range(HEADS_PER_GROUP):
                hs = slice(hh * SSM_HEAD_DIM, (hh + 1) * SSM_HEAD_DIM)
                so_ref[bb, g, hs, :] = s0[hs] * dec_ref[bb, g * HEADS_PER_GROUP + hh] + upd[hs]


def _sample_state(dec, state, cq, bq, xwq, seqs):
    nb = state.shape[0]
    q = HEADS_PER_GROUP * SSM_HEAD_DIM
    blk = lambda *tail: pl.BlockSpec((seqs, SSM_GROUPS) + tail, lambda i: (i, 0, 0, 0))
    return pl.pallas_call(
        functools.partial(_sample_state_kernel, seqs=seqs),
        grid=(nb // seqs,),
        in_specs=[
            pl.BlockSpec((seqs, SSM_HEADS), lambda i: (i, 0), memory_space=pltpu.SMEM),
            blk(q, SSM_STATE), blk(SUBLANES, SSM_STATE), blk(SUBLANES, SSM_STATE), blk(SUBLANES, q),
        ],
        out_specs=[blk(q, SSM_STATE), blk(SUBLANES, q)],
        out_shape=[
            jax.ShapeDtypeStruct((nb, SSM_GROUPS, q, SSM_STATE), F32),
            jax.ShapeDtypeStruct((nb, SSM_GROUPS, SUBLANES, q), F32),
        ],
        compiler_params=pltpu.CompilerParams(
            dimension_semantics=("arbitrary",), vmem_limit_bytes=VMEM_LIMIT),
        name="sample_state",
    )(dec, state, cq, bq, xwq)


def _sample_back_kernel(x_ref, ypart_ref, ea_ref, zoff_ref, z_ref, ysg_ref, ssmnw_ref, wout_ref,
                        nfw_ref, wr_ref, br_ref, h_ref, hn2_ref, route_ref):
    y = ypart_ref[...] + ea_ref[...] * zoff_ref[...]
    y_ssm = _gated_group_norm(y, z_ref[...], ssmnw_ref[...])
    mix_in = jnp.concatenate([y_ssm.astype(BF16), ysg_ref[...].astype(BF16)], axis=1)
    h = x_ref[...] + _dot(mix_in, wout_ref[...])
    h_ref[...] = h
    hn2, route = _ffn_front(h, nfw_ref[...], wr_ref, br_ref[...])
    _store_row_tiles(hn2_ref, hn2)
    route_ref[...] = route


def _sample_back(x_slab, ypart, ea, zoff, z, ysg, w):
    r = x_slab.shape[0]
    ins = [x_slab, ypart, ea, zoff, z, ysg, w['ssmnw'], w['wout'], w['nfw'], w['wr'], w['br']]
    out_shape = [
        jax.ShapeDtypeStruct((r, D_MODEL), F32),
        jax.ShapeDtypeStruct((r * ROW_TILES, LANES), F32),
        jax.ShapeDtypeStruct((r, LANES), F32),
    ]
    return pl.pallas_call(
        _sample_back_kernel,
        grid=(1,),
        in_specs=[_const_spec(a.shape) for a in ins],
        out_specs=[_const_spec(s.shape) for s in out_shape],
        out_shape=out_shape,
        compiler_params=pltpu.CompilerParams(
            dimension_semantics=("arbitrary",), vmem_limit_bytes=VMEM_LIMIT),
        name="sample_back",
    )(*ins)


EXPERT_PIECES = 8
DMA_PIECES = 4


def _expert_mlp_kernel(be_ref, nu_ref, tok0_ref, tokn_ref, dstp_ref, x_hbm, wg_ref, wu_ref, wd_ref,
                       parts_hbm, xbuf, obuf, xb, hb, wg_b, wu_b, wd_b, gsem, ssem, *, bm):
    j = pl.program_id(0)
    nu = nu_ref[0]
    nt = ROW_TILES
    tile = lambda ref, start: ref.at[pl.ds(pl.multiple_of(start, nt), nt)]

    def gather_copy(idx_ref, s, r):
        return pltpu.make_async_copy(tile(x_hbm, idx_ref[0, r]), xbuf.at[s, pl.ds(r * nt, nt)], gsem.at[s])

    def scatter_copy(s, r):
        return pltpu.make_async_copy(obuf.at[s, pl.ds(r * nt, nt)], tile(parts_hbm, dstp_ref[0, r]),
                                     ssem.at[s])

    def gather_wait(s):
        pltpu.make_async_copy(x_hbm.at[pl.ds(0, bm * nt)], xbuf.at[s], gsem.at[s]).wait()

    def scatter_wait(s):
        pltpu.make_async_copy(obuf.at[s], parts_hbm.at[pl.ds(0, bm * nt)], ssem.at[s]).wait()

    def step(slot):
        gather_wait(slot)
        per = bm // DMA_PIECES

        def start_rows(piece):
            if piece >= DMA_PIECES:
                return
            for r in range(piece * per, (piece + 1) * per):
                gather_copy(tokn_ref, 1 - slot, r).start(priority=r % 2)
                scatter_copy(1 - slot, r).start(priority=(r + 1) % 2)

        xb[...] = _load_row_tiles(xbuf.at[slot]).astype(BF16)
        half = D_FF // 2
        for c in range(2):
            cols = slice(c * half, (c + 1) * half)
            start_rows(2 * c)
            g = _dot(xb[...], wg_b[:, cols])
            start_rows(2 * c + 1)
            u = _dot(xb[...], wu_b[:, cols])
            hb[:, cols] = (_silu(g) * u).astype(BF16)
        quarter = D_MODEL // 4
        scatter_wait(slot)
        for c in range(4):
            start_rows(4 + c)
            o = _dot(hb[...], wd_b[:, c * quarter:(c + 1) * quarter])
            for k in range(quarter // LANES):
                kk = c * (quarter // LANES) + k
                obuf.at[slot][pl.ds(kk, bm, stride=nt), :] = o[:, k * LANES:(k + 1) * LANES]

    def drain(slot):
        gather_wait(slot)
        scatter_wait(slot)
        for r in range(bm):
            scatter_copy(1 - slot, r).start(priority=r % 2)
        scatter_wait(1 - slot)

    @pl.when(j == 0)
    def _():
        obuf[...] = jnp.zeros(obuf.shape, F32)
        n_real = parts_hbm.shape[0] - 2 * bm * nt
        pltpu.make_async_copy(obuf.at[0], parts_hbm.at[pl.ds(n_real, bm * nt)], ssem.at[0]).start()

        def start0(r, carry):
            pltpu.make_async_copy(tile(x_hbm, tok0_ref[0, r]), xbuf.at[0, pl.ds(pl.multiple_of(r * nt, nt), nt)],
                                  gsem.at[0]).start()
            return carry
        lax.fori_loop(0, bm, start0, 0)

    used = j < nu

    @pl.when(used & ((j == 0) | (be_ref[j] != be_ref[jnp.maximum(j - 1, 0)])))
    def _():
        wg_b[...] = wg_ref[...].astype(BF16)
        wu_b[...] = wu_ref[...].astype(BF16)
        wd_b[...] = wd_ref[...].astype(BF16)

    for slot in range(2):
        parity = lax.rem(j, 2) == slot

        @pl.when(used & parity)
        def _(slot=slot):
            step(slot)

        @pl.when((j == nu) & parity)
        def _(slot=slot):
            drain(slot)


def _expert_mlp(blk_expert, n_used, tok, dst, x, wg, wu, wd, bm, n_out):
    n_blocks = blk_expert.shape[0]
    assert bm % EXPERT_PIECES == 0
    w_map = lambda j, be, nu: (be[j], 0, 0)
    idx_spec = lambda f: pl.BlockSpec((None, 1, bm), lambda j, be, nu: (f(j), 0, 0),
                                      memory_space=pltpu.SMEM)
    any_spec = pl.BlockSpec(memory_space=pl.ANY)
    grid_spec = pltpu.PrefetchScalarGridSpec(
        num_scalar_prefetch=2,
        grid=(n_blocks + 1,),
        in_specs=[
            idx_spec(lambda j: 0),
            idx_spec(lambda j: jnp.minimum(j + 1, n_blocks - 1)),
            idx_spec(lambda j: j),
            any_spec,
            pl.BlockSpec((None, D_MODEL, D_FF), w_map),
            pl.BlockSpec((None, D_MODEL, D_FF), w_map),
            pl.BlockSpec((None, D_FF, D_MODEL), w_map),
        ],
        out_specs=any_spec,
        scratch_shapes=[
            pltpu.VMEM((2, bm * ROW_TILES, LANES), F32),
            pltpu.VMEM((2, bm * ROW_TILES, LANES), F32),
            pltpu.VMEM((bm, D_MODEL), BF16),
            pltpu.VMEM((bm, D_FF), BF16),
            pltpu.VMEM((D_MODEL, D_FF), BF16),
            pltpu.VMEM((D_MODEL, D_FF), BF16),
            pltpu.VMEM((D_FF, D_MODEL), BF16),
            pltpu.SemaphoreType.DMA((2,)),
            pltpu.SemaphoreType.DMA((2,)),
        ],
    )
    tok3 = (tok * ROW_TILES).reshape(n_blocks, 1, bm)
    spare = (n_out - bm + jnp.arange(bm, dtype=jnp.int32)).reshape(1, bm)
    dst_prev = (jnp.concatenate([spare, dst], axis=0) * ROW_TILES).reshape(n_blocks + 1, 1, bm)
    blk_expert = jnp.concatenate([blk_expert, blk_expert[-1:]])
    return pl.pallas_call(
        functools.partial(_expert_mlp_kernel, bm=bm),
        grid_spec=grid_spec,
        out_shape=jax.ShapeDtypeStruct((n_out * ROW_TILES, LANES), F32),
        compiler_params=pltpu.CompilerParams(
            dimension_semantics=("arbitrary",), vmem_limit_bytes=VMEM_LIMIT),
        name="expert_mlp",
    )(blk_expert, n_used, tok3, tok3, dst_prev, x, wg, wu, wd)


PAD_ID = 2 ** 16 - 1


def _moe_plan(e_idx, bm, n_blocks):
    n_tok = e_idx.shape[0]
    n_asg = 2 * n_tok
    n_pad = n_blocks * bm - n_asg
    e_flat = e_idx.reshape(-1)
    experts = jnp.arange(N_EXPERTS, dtype=jnp.int32)
    counts = jnp.sum((e_flat[:, None] == experts[None, :]).astype(jnp.int32), axis=0)
    nblk = (counts + bm - 1) // bm
    blk_end = jnp.cumsum(nblk)
    n_used = blk_end[-1]
    j = jnp.arange(n_blocks, dtype=jnp.int32)
    be = jnp.sum((blk_end[None, :] <= jnp.minimum(j, n_used - 1)[:, None]).astype(jnp.int32), axis=1)
    be = jnp.minimum(be, N_EXPERTS - 1)
    assert n_asg < PAD_ID
    pad_end = jnp.cumsum(nblk * bm - counts)
    pad_expert = jnp.sum((pad_end[None, :] <= jnp.arange(n_pad, dtype=jnp.int32)[:, None]).astype(jnp.int32),
                         axis=1)
    keys = jnp.concatenate([2 * e_flat, 2 * pad_expert + 1])
    ids = jnp.concatenate([jnp.arange(n_asg, dtype=jnp.int32), jnp.full((n_pad,), PAD_ID, jnp.int32)])
    window = lax.sort(keys * (PAD_ID + 1) + ids) % (PAD_ID + 1)
    window = window.reshape(n_blocks, bm)
    valid = window != PAD_ID
    q = jnp.arange(bm, dtype=jnp.int32)
    tok = jnp.where(valid, window // 2, (j[:, None] * bm + q[None, :]) % n_tok)
    dst = jnp.where(valid, (window % 2) * n_tok + window // 2,
                    2 * n_tok + (j[:, None] % 2) * bm + q[None, :])
    return tok.astype(jnp.int32), dst.astype(jnp.int32), be, n_used.astype(jnp.int32).reshape(1)


def _final_kernel(h_ref, p0_ref, p1_ref, route_ref, w_ref, op_ref, os_ref, *, n_prompt_steps):
    route = route_ref[...]
    p0 = _load_row_tiles(p0_ref)
    p1 = _load_row_tiles(p1_ref)
    y = _rms(h_ref[...] + route[:, 2:3] * p0 + route[:, 3:4] * p1, w_ref[...])
    i = pl.program_id(0)

    @pl.when(i < n_prompt_steps)
    def _():
        op_ref[...] = y

    @pl.when(i >= n_prompt_steps)
    def _():
        os_ref[...] = y


def _final(h, parts, route, w, tf, n_prompt):
    n_tok = h.shape[0]
    assert n_tok % tf == 0 and n_prompt % tf == 0
    n_p = n_prompt // tf
    row = lambda d: pl.BlockSpec((tf, d), lambda i: (i, 0))
    part = lambda k: pl.BlockSpec((tf * ROW_TILES, LANES), lambda i: (k * (n_tok // tf) + i, 0))
    return pl.pallas_call(
        functools.partial(_final_kernel, n_prompt_steps=n_p),
        grid=(n_tok // tf,),
        in_specs=[row(D_MODEL), part(0), part(1), row(LANES), _const_spec(w.shape)],
        out_specs=[pl.BlockSpec((tf, D_MODEL), lambda i: (jnp.minimum(i, n_p - 1), 0)),
                   pl.BlockSpec((tf, D_MODEL), lambda i: (jnp.maximum(i - n_p, 0), 0))],
        out_shape=[jax.ShapeDtypeStruct((n_prompt, D_MODEL), F32),
                   jax.ShapeDtypeStruct((n_tok - n_prompt, D_MODEL), F32)],
        compiler_params=pltpu.CompilerParams(
            dimension_semantics=("arbitrary",), vmem_limit_bytes=VMEM_LIMIT),
        name="final_combine",
    )(h, parts, parts, route, w)


PROMPT_TILE = 256
MOE_BLOCK_ROWS = 512
FINAL_TILE = 512
STATE_SEQS = 8


def _pad_lanes(a, width=LANES):
    return jnp.pad(a, [(0, 0)] * (a.ndim - 1) + [(0, width - a.shape[-1])])


def _layer_weights(l, norm_mix_w, w_in, conv_w, conv_b, dt_bias, a_log, d_skip, ssm_norm_w, sg_norm_w,
                   sg_w, sg_b, w_out, norm_ffn_w, w_router_group, b_router_group, w_router_expert,
                   b_router_expert, n_dec):
    c0, c1, c2 = D_SSM, D_SSM + C_CONV, D_SSM + C_CONV + SSM_HEADS
    wi = w_in[l]
    causal = jnp.tril(jnp.ones((CHUNK, CHUNK), bool))
    sgw_l = jnp.where(causal, sg_w[l], 0.0)
    head_of_lane = jnp.arange(D_SSM, dtype=jnp.int32) // SSM_HEAD_DIM
    expand = (jnp.arange(LANES, dtype=jnp.int32)[:, None] == head_of_lane[None, :]).astype(F32)
    row = lambda a: a.reshape(1, -1).astype(F32)
    gap = ROUTER_EXPERT_ROW - N_EXPERT_GROUPS
    rest = LANES - ROUTER_EXPERT_ROW - N_EXPERTS
    wrt = jnp.concatenate([w_router_group[l].T, jnp.zeros((gap, D_MODEL), F32), w_router_expert[l].T,
                           jnp.zeros((rest, D_MODEL), F32)], axis=0).astype(F32)
    brc = jnp.concatenate([b_router_group[l], jnp.zeros((gap,), F32), b_router_expert[l],
                           jnp.zeros((rest,), F32)]).astype(F32).reshape(LANES, 1)
    return {
        'nmw': row(norm_mix_w[l]),
        'wz': wi[:, :c0].astype(BF16),
        'wxbc': wi[:, c0:c1].astype(BF16),
        'wdt': _pad_lanes(wi[:, c1:c2]).astype(BF16),
        'wuv': wi[:, c2:].astype(BF16),
        'convw': conv_w[l].astype(F32),
        'convb': row(conv_b[l]),
        'dtb': _pad_lanes(row(dt_bias[l])),
        'alog': _pad_lanes(row(a_log[l])),
        'dskip': row(jnp.repeat(d_skip[l], SSM_HEAD_DIM)),
        'ssmnw': row(ssm_norm_w[l]),
        'sgnw': row(sg_norm_w[l]),
        'sgw': sgw_l.astype(BF16),
        'sgb': jnp.repeat(sg_b[l].T, SG_HEAD_DIM, axis=1).astype(F32),
        'w4': jnp.repeat(jnp.transpose(sgw_l[:, :n_dec, :n_dec], (1, 2, 0)), SG_HEAD_DIM, axis=2).astype(F32),
        'b4': jnp.repeat(sg_b[l][:, :n_dec].T, SG_HEAD_DIM, axis=1).astype(F32),
        'expand': expand,
        'wout': w_out[l].astype(BF16),
        'nfw': row(norm_ffn_w[l]),
        'wr': jnp.stack([wrt.astype(BF16), (wrt - wrt.astype(BF16).astype(F32)).astype(BF16)]),
        'br': brc,
    }


def _layer(l, xp, xs_slab, state_conv, state_ssm, w, w_gate, w_up, w_down):
    bp, seq, _ = xp.shape
    nb, nt = state_conv.shape[1], xs_slab.shape[0] // state_conv.shape[1]
    q = HEADS_PER_GROUP * SSM_HEAD_DIM

    cprev = jnp.transpose(state_conv[l], (1, 0, 2))
    z, ypart, ea, ysg, v, convn, cmat, bmat, xw, dec = _sample_front(xs_slab, cprev, w, nb, nt)
    to_seq = lambda a, d: jnp.pad(
        jnp.transpose(a.reshape(nt, nb, SSM_GROUPS, d), (1, 2, 0, 3)),
        ((0, 0), (0, 0), (0, SUBLANES - nt), (0, 0)))
    ssm_s, zoff = _sample_state(
        dec[:, :SSM_HEADS], state_ssm[l].reshape(nb, SSM_GROUPS, q, SSM_STATE),
        to_seq(cmat, SSM_STATE), to_seq(bmat, SSM_STATE), to_seq(xw, q), STATE_SEQS)
    zoff = jnp.transpose(zoff[:, :, :nt], (2, 0, 1, 3)).reshape(nt * nb, D_SSM)
    h_s, hn2_s, route_s = _sample_back(xs_slab, ypart, ea, zoff, z, ysg, w)

    h, hn2, route, convt_p, ssm_p = _prompt_mixer(xp, w, PROMPT_TILE, h_s, hn2_s, route_s)

    n_tok = h.shape[0]
    bm = MOE_BLOCK_ROWS
    n_blocks = -(-(2 * n_tok) // bm) + N_EXPERTS
    tok, dst, blk_expert, n_used = _moe_plan(route[:, :2].astype(jnp.int32), bm, n_blocks)
    parts = _expert_mlp(blk_expert, n_used, tok, dst, hn2, w_gate[l], w_up[l], w_down[l], bm,
                        2 * n_tok + 2 * bm)
    outs = dict(
        h=h, parts=parts, route=route,
        conv_p=convt_p[:, SUBLANES - (CONV_K - 1):],
        ssm_p=ssm_p.reshape(bp, SSM_GROUPS, HEADS_PER_GROUP, SSM_HEAD_DIM, SSM_STATE),
        conv_s=jnp.transpose(convn, (1, 0, 2)),
        ssm_s=ssm_s.reshape(nb, SSM_GROUPS, HEADS_PER_GROUP, SSM_HEAD_DIM, SSM_STATE),
        v_s=jnp.transpose(v.reshape(nt, nb, D_SG), (1, 0, 2)),
    )
    return outs


def kernel(x_prompt, x_sample, state_conv, state_ssm, norm_mix_w, w_in, conv_w, conv_b, dt_bias, a_log, d_skip, ssm_norm_w, sg_norm_w, sg_w, sg_b, w_out, norm_ffn_w, w_router_group, b_router_group, w_router_expert, b_router_expert, w_gate, w_up, w_down, norm_final_w):
    depth = w_in.shape[0]
    assert depth == 1, "the fused final norm assumes a single layer"
    bp, seq, _ = x_prompt.shape
    nb, nt, _ = x_sample.shape
    l = 0
    w = _layer_weights(l, norm_mix_w, w_in, conv_w, conv_b, dt_bias, a_log, d_skip, ssm_norm_w, sg_norm_w,
                       sg_w, sg_b, w_out, norm_ffn_w, w_router_group, b_router_group, w_router_expert,
                       b_router_expert, nt)
    xs_slab = jnp.transpose(x_sample, (1, 0, 2)).reshape(nt * nb, D_MODEL)
    outs = _layer(l, x_prompt, xs_slab, state_conv, state_ssm, w, w_gate, w_up, w_down)
    y_p, y_s = _final(outs['h'], outs['parts'], outs['route'], norm_final_w.reshape(1, -1).astype(F32),
                      FINAL_TILE, bp * seq)
    y_prompt = y_p.reshape(bp, seq, D_MODEL)
    y_sample = jnp.transpose(y_s.reshape(nt, nb, D_MODEL), (1, 0, 2))
    return (y_prompt, y_sample, outs['conv_p'][None], outs['ssm_p'][None], outs['conv_s'][None],
            outs['ssm_s'][None], outs['v_s'][None])
```

```python
import functools

import jax
import jax.numpy as jnp
from jax import lax
from jax.experimental import pallas as pl
from jax.experimental.pallas import tpu as pltpu

D_MODEL = 1024
D_SSM = 1024
SSM_HEAD_DIM = 64
SSM_HEADS = 16
SSM_GROUPS = 2
HEADS_PER_GROUP = 8
SSM_STATE = 128
CONV_K = 4
C_CONV = D_SSM + 2 * SSM_GROUPS * SSM_STATE
D_SG = 1024
SG_HEADS = 8
SG_HEAD_DIM = 128
CHUNK = 128
N_EXPERT_GROUPS = 4
EXPERTS_PER_GROUP = 8
N_EXPERTS = 32
D_FF = 512
EPS = 1e-6
LOG2_E = 1.4426950408889634

LANES = 128
SUBLANES = 8
VMEM_LIMIT = 56 * 1024 * 1024

F32 = jnp.float32
BF16 = jnp.bfloat16
HIGHEST = lax.Precision.HIGHEST


def _dot(a, b):
    return jnp.dot(a, b, preferred_element_type=F32)


def _dot_f32(a, b):
    return jnp.dot(a, b, preferred_element_type=F32, precision=HIGHEST)


def _rms(x, w):
    return x * lax.rsqrt(jnp.mean(x * x, axis=-1, keepdims=True) + EPS) * w


def _silu(x):
    return x * (0.5 + 0.5 * jnp.tanh(0.5 * x))


def _gelu(x):
    return 0.5 * x * (1.0 + lax.erf(x * 0.7071067811865476))


def _softplus(x):
    return jnp.maximum(x, 0.0) + jnp.log1p(jnp.exp(-jnp.abs(x)))


def _gated_group_norm(y, z, w):
    g = y * _silu(z)
    half = D_SSM // SSM_GROUPS
    parts = []
    for k in range(SSM_GROUPS):
        gk = g[:, k * half:(k + 1) * half]
        parts.append(gk * lax.rsqrt(jnp.mean(gk * gk, axis=-1, keepdims=True) + EPS))
    return jnp.concatenate(parts, axis=1) * w


ROUTER_GROUP_ROW = 0
ROUTER_EXPERT_ROW = SUBLANES


def _route_t(lt):
    r = lt.shape[1]
    row = lax.broadcasted_iota(jnp.int32, (SUBLANES, r), 0)
    rowf = row.astype(F32)
    big = float(SUBLANES)
    red = lambda f, a: f(a, axis=0, keepdims=True)
    gl = jnp.where(row < N_EXPERT_GROUPS, lt[ROUTER_GROUP_ROW:ROUTER_GROUP_ROW + SUBLANES], -jnp.inf)
    ge = jnp.exp(gl - red(jnp.max, gl))
    p_grp = ge / red(jnp.sum, ge)
    g_p = red(jnp.max, p_grp)
    g_idx = red(jnp.min, jnp.where(p_grp == g_p, rowf, big))
    el = lt[ROUTER_EXPERT_ROW:ROUTER_EXPERT_ROW + EXPERTS_PER_GROUP]
    for g in range(1, N_EXPERT_GROUPS):
        lo = ROUTER_EXPERT_ROW + g * EXPERTS_PER_GROUP
        el = jnp.where(g_idx == float(g), lt[lo:lo + EXPERTS_PER_GROUP], el)
    ee = jnp.exp(el - red(jnp.max, el))
    pe = ee / red(jnp.sum, ee)
    v1 = red(jnp.max, pe)
    i1 = red(jnp.min, jnp.where(pe == v1, rowf, big))
    rest = rowf != i1
    pe2 = jnp.where(rest, pe, -1.0)
    v2 = red(jnp.max, pe2)
    i2 = red(jnp.min, jnp.where(rest & (pe2 == v2), rowf, big))
    den = v1 + v2
    base = g_idx * float(EXPERTS_PER_GROUP)
    out = jnp.where(row == 0, base + i1, 0.0)
    out = jnp.where(row == 1, base + i2, out)
    out = jnp.where(row == 2, g_p * v1 / den, out)
    out = jnp.where(row == 3, g_p * v2 / den, out)
    return jnp.concatenate([out, jnp.zeros((LANES - SUBLANES, r), F32)], axis=0).T


def _ffn_front(h, nfw, wrt_ref, brc):
    hn2 = _rms(h, nfw)
    hi = hn2.astype(BF16)
    lo = (hn2 - hi.astype(F32)).astype(BF16)
    nt = lambda a, b: lax.dot_general(a, b, (((1,), (1,)), ((), ())), preferred_element_type=F32)
    lt = nt(wrt_ref[0], hi) + nt(wrt_ref[0], lo) + nt(wrt_ref[1], hi) + brc
    return hn2, _route_t(lt)


ROW_TILES = D_MODEL // LANES


def _store_row_tiles(ref, val):
    r = val.shape[0]
    for k in range(ROW_TILES):
        ref[pl.ds(k, r, stride=ROW_TILES), :] = val[:, k * LANES:(k + 1) * LANES]


def _load_row_tiles(ref):
    r = ref.shape[0] // ROW_TILES
    return jnp.concatenate([ref[pl.ds(k, r, stride=ROW_TILES), :] for k in range(ROW_TILES)], axis=1)


def _prompt_step(
        t, nt, fillers, x_ref, nmw_ref, wz_ref, wxbc_ref, wdt_ref, wuv_ref, convw_ref, convb_ref, dtb_ref,
        alog_ref, dskip_ref, ssmnw_ref, sgnw_ref, sgw_ref, sgb_ref, convt_ref, ssm_ref,
        xbc_scr, state_scr, y_scr, mix_scr, tl):
    fillers = list(fillers)
    assert len(fillers) == 3 + tl // CHUNK

    @pl.when(t == 0)
    def _():
        xbc_scr[0:SUBLANES, :] = jnp.zeros((SUBLANES, C_CONV), F32)
        state_scr[...] = jnp.zeros(state_scr.shape, F32)

    x = x_ref[...]
    hn = _rms(x, nmw_ref[...]).astype(BF16)
    z = _dot(hn, wz_ref[...])
    xbc = _dot(hn, wxbc_ref[...])
    dtr = _dot(hn, wdt_ref[...])
    uv = _gelu(_dot(hn, wuv_ref[...]))
    fillers.pop(0)()

    xbc_scr[SUBLANES:SUBLANES + tl, :] = xbc
    conv = convb_ref[...]
    for k in range(CONV_K):
        off = SUBLANES - (CONV_K - 1) + k
        conv = conv + convw_ref[k:k + 1, :] * xbc_scr[off:off + tl, :]
    tail = xbc_scr[tl:tl + SUBLANES, :]
    xbc_scr[0:SUBLANES, :] = tail
    convt_ref[...] = tail
    act = _silu(conv)
    xs = act[:, :D_SSM]
    bmat = act[:, D_SSM:D_SSM + SSM_GROUPS * SSM_STATE]
    cmat = act[:, D_SSM + SSM_GROUPS * SSM_STATE:]
    fillers.pop(0)()

    dt = _softplus(dtr + dtb_ref[...])
    a_row = -jnp.exp(alog_ref[...]) * LOG2_E

    li = lax.broadcasted_iota(jnp.int32, (CHUNK, CHUNK), 0)
    si = lax.broadcasted_iota(jnp.int32, (CHUNK, CHUNK), 1)
    causal = li >= si
    tri = jnp.where(causal, 1.0, 0.0).astype(F32)
    lo_half = si < SSM_HEAD_DIM

    for c in range(tl // CHUNK):
        rows = slice(c * CHUNK, (c + 1) * CHUNK)
        dt_c = dt[rows]
        acum = _dot_f32(tri, dt_c * a_row)
        acum_t = acum.T
        dt_t = dt_c.T
        w_t = dt_t * jnp.exp2(acum_t[:, CHUNK - 1:CHUNK] - acum_t)
        xs_c = xs[rows]
        for g in range(SSM_GROUPS):
            gl = slice(g * SSM_STATE, (g + 1) * SSM_STATE)
            b_g = bmat[rows, gl]
            c_g = cmat[rows, gl]
            b_gt = b_g.T
            cb = _dot(c_g.astype(BF16), b_gt.astype(BF16))
            for j in range(HEADS_PER_GROUP // 2):
                gw = HEADS_PER_GROUP * SSM_HEAD_DIM
                pl_ = slice(g * gw + j * LANES, g * gw + (j + 1) * LANES)
                m_l, ec_l, s_l, ea_last = [], [], [], []
                for k in range(2):
                    hh = g * HEADS_PER_GROUP + 2 * j + k
                    colb = jnp.broadcast_to(acum[:, hh:hh + 1], (CHUNK, CHUNK))
                    rowb = jnp.broadcast_to(acum_t[hh:hh + 1, :], (CHUNK, CHUNK))
                    ea = jnp.exp2(colb)
                    lmat = jnp.where(causal, jnp.exp2(colb - rowb), 0.0)
                    m_l.append(cb * lmat * jnp.broadcast_to(dt_t[hh:hh + 1, :], (CHUNK, CHUNK)))
                    ec_l.append(c_g * ea)
                    s_l.append(b_gt * jnp.broadcast_to(w_t[hh:hh + 1, :], (CHUNK, CHUNK)))
                    ea_last.append(ea[CHUNK - 1:CHUNK, :])
                xs_p = xs_c[:, pl_]
                st_p = state_scr[g, :, j * LANES:(j + 1) * LANES]
                rx = jnp.concatenate([jnp.where(lo_half, xs_p, 0.0),
                                      jnp.where(lo_half, 0.0, xs_p)], axis=0).astype(BF16)
                rs = jnp.concatenate([jnp.where(lo_half, st_p, 0.0),
                                      jnp.where(lo_half, 0.0, st_p)], axis=0).astype(BF16)
                lhs = jnp.concatenate(m_l + ec_l, axis=1).astype(BF16)
                y_p = _dot(lhs, jnp.concatenate([rx, rs], axis=0))
                y_scr[rows, pl_] = y_p
                dec = jnp.where(lo_half[0:1, :], ea_last[0], ea_last[1])
                upd = _dot(jnp.concatenate(s_l, axis=1).astype(BF16), rx)
                state_scr[g, :, j * LANES:(j + 1) * LANES] = st_p * dec + upd
        fillers.pop(0)()

    y = y_scr[...] + dskip_ref[...] * xs
    y_ssm = _gated_group_norm(y, z, ssmnw_ref[...])
    mix_scr[:, :D_SSM] = y_ssm.astype(BF16)

    u = uv[:, :D_SG]
    v = _rms(uv[:, D_SG:], sgnw_ref[...])
    vb = v.astype(BF16)
    for c in range(tl // CHUNK):
        rows = slice(c * CHUNK, (c + 1) * CHUNK)
        for hd in range(SG_HEADS):
            hl = slice(hd * SG_HEAD_DIM, (hd + 1) * SG_HEAD_DIM)
            sv = _dot(sgw_ref[hd], vb[rows, hl]) + sgb_ref[:, hl]
            mix_scr[rows, D_SSM + hd * SG_HEAD_DIM:D_SSM + (hd + 1) * SG_HEAD_DIM] = (
                u[rows, hl] * sv).astype(BF16)

    fillers.pop(0)()

    @pl.when(t == nt - 1)
    def _():
        for g in range(SSM_GROUPS):
            ssm_ref[g] = state_scr[g].T


def _prompt_mixer_kernel(
        x_ref, xp_ref, nmw_ref, wz_ref, wxbc_ref, wdt_ref, wuv_ref, convw_ref, convb_ref, dtb_ref,
        alog_ref, dskip_ref, ssmnw_ref, sgnw_ref, sgw_ref, sgb_ref, wout_ref, nfw_ref,
        wr_ref, br_ref, hs_ref, hn2s_ref, routes_ref,
        h_ref, hn2_ref, route_ref, convt_ref, ssm_ref,
        xbc_scr, state_scr, y_scr, mix_scr, h_scr, *, tl, nt, n_main):
    i = pl.program_id(0)

    @pl.when(i == 0)
    def _():
        mix_scr[...] = jnp.zeros(mix_scr.shape, BF16)

    @pl.when(i > n_main)
    def _():
        h_ref[...] = hs_ref[...]
        hn2_ref[...] = hn2s_ref[...]
        route_ref[...] = routes_ref[...]

    n_out_chunks = 2 + tl // CHUNK
    width = D_MODEL // n_out_chunks
    assert width % LANES == 0 and width * n_out_chunks == D_MODEL

    def out_chunk(c):
        cols = slice(c * width, (c + 1) * width)
        h_scr[:, cols] = xp_ref[:, cols] + _dot(mix_scr[...], wout_ref[:, cols])

    def finish():
        h = h_scr[...]
        h_ref[...] = h
        hn2, route = _ffn_front(h, nfw_ref[...], wr_ref, br_ref[...])
        _store_row_tiles(hn2_ref, hn2)
        route_ref[...] = route

    @pl.when(i <= n_main)
    def _():
        def out_chunks(cs):
            for c in cs:
                out_chunk(c)

        half = n_out_chunks // 2
        fillers = ([functools.partial(out_chunks, range(half)),
                    functools.partial(out_chunks, range(half, n_out_chunks)), finish]
                   + [lambda: None] * (tl // CHUNK))
        _prompt_step(
            lax.rem(i, nt), nt, fillers, x_ref, nmw_ref, wz_ref, wxbc_ref, wdt_ref, wuv_ref,
            convw_ref, convb_ref, dtb_ref, alog_ref, dskip_ref, ssmnw_ref, sgnw_ref, sgw_ref, sgb_ref,
            convt_ref, ssm_ref, xbc_scr, state_scr, y_scr, mix_scr, tl)


def _const_spec(shape):
    zeros = (0,) * len(shape)
    return pl.BlockSpec(shape, lambda *_: zeros)


def _prompt_mixer(x, w, tl, h_s, hn2_s, route_s):
    bsz, seq, _ = x.shape
    nt = seq // tl
    n_main = bsz * nt
    n_s = h_s.shape[0]
    assert seq % tl == 0 and n_s % tl == 0
    n_tok = bsz * seq + n_s
    x2 = x.reshape(bsz * seq, D_MODEL)
    main = lambda i: jnp.minimum(i, n_main - 1)
    prev = lambda i: jnp.clip(i - 1, 0, n_main - 1)
    tail = lambda i: jnp.maximum(i - n_main - 1, 0)
    out_row = lambda i: jnp.maximum(i - 1, 0)
    weights = [w['nmw'], w['wz'], w['wxbc'], w['wdt'], w['wuv'], w['convw'], w['convb'], w['dtb'],
               w['alog'], w['dskip'], w['ssmnw'], w['sgnw'], w['sgw'], w['sgb'], w['wout'], w['nfw'],
               w['wr'], w['br']]
    in_specs = ([pl.BlockSpec((tl, D_MODEL), lambda i: (main(i), 0)),
                 pl.BlockSpec((tl, D_MODEL), lambda i: (prev(i), 0))]
                + [_const_spec(a.shape) for a in weights]
                + [pl.BlockSpec(blk, lambda i: (tail(i), 0))
                   for blk in ((tl, D_MODEL), (tl * ROW_TILES, LANES), (tl, LANES))])
    q = HEADS_PER_GROUP * SSM_HEAD_DIM
    out_shape = [
        jax.ShapeDtypeStruct((n_tok, D_MODEL), F32),
        jax.ShapeDtypeStruct((n_tok * ROW_TILES, LANES), F32),
        jax.ShapeDtypeStruct((n_tok, LANES), F32),
        jax.ShapeDtypeStruct((bsz, SUBLANES, C_CONV), F32),
        jax.ShapeDtypeStruct((bsz, SSM_GROUPS, q, SSM_STATE), F32),
    ]
    out_specs = [
        pl.BlockSpec((tl, D_MODEL), lambda i: (out_row(i), 0)),
        pl.BlockSpec((tl * ROW_TILES, LANES), lambda i: (out_row(i), 0)),
        pl.BlockSpec((tl, LANES), lambda i: (out_row(i), 0)),
        pl.BlockSpec((None, SUBLANES, C_CONV), lambda i: (main(i) // nt, 0, 0)),
        pl.BlockSpec((None, SSM_GROUPS, q, SSM_STATE), lambda i: (main(i) // nt, 0, 0, 0)),
    ]
    scratch = [
        pltpu.VMEM((tl + SUBLANES, C_CONV), F32),
        pltpu.VMEM((SSM_GROUPS, SSM_STATE, q), F32),
        pltpu.VMEM((tl, D_SSM), F32),
        pltpu.VMEM((tl, D_SSM + D_SG), BF16),
        pltpu.VMEM((tl, D_MODEL), F32),
    ]
    return pl.pallas_call(
        functools.partial(_prompt_mixer_kernel, tl=tl, nt=nt, n_main=n_main),
        grid=(n_main + 1 + n_s // tl,),
        in_specs=in_specs,
        out_specs=out_specs,
        out_shape=out_shape,
        scratch_shapes=scratch,
        compiler_params=pltpu.CompilerParams(
            dimension_semantics=("arbitrary",), vmem_limit_bytes=VMEM_LIMIT),
        name="prompt_mixer",
    )(x2, x2, *weights, h_s, hn2_s, route_s)


def _sample_front_kernel(
        x_ref, cprev_ref, nmw_ref, wz_ref, wxbc_ref, wdt_ref, wuv_ref, convw_ref, convb_ref,
        dtb_ref, alog_ref, dskip_ref, sgnw_ref, w4_ref, b4_ref, expand_ref,
        z_ref, ypart_ref, ea_ref, ysg_ref, v_ref, convn_ref, c_ref, b_ref, xw_ref, dec_ref,
        *, nb, nt):
    x = x_ref[...]
    hn = _rms(x, nmw_ref[...]).astype(BF16)
    z_ref[...] = _dot(hn, wz_ref[...])
    xbc = _dot(hn, wxbc_ref[...])
    dtr = _dot(hn, wdt_ref[...])
    uv = _gelu(_dot(hn, wuv_ref[...]))

    slab = lambda a, t: a[t * nb:(t + 1) * nb]
    full = [cprev_ref[k] for k in range(CONV_K - 1)] + [slab(xbc, t) for t in range(nt)]
    for k in range(CONV_K - 1):
        convn_ref[k] = full[nt + k]
    xs, bm, cm = [], [], []
    for t in range(nt):
        conv = convb_ref[...]
        for k in range(CONV_K):
            conv = conv + convw_ref[k:k + 1, :] * full[t + k]
        act = _silu(conv)
        xs.append(act[:, :D_SSM])
        bm.append(act[:, D_SSM:D_SSM + SSM_GROUPS * SSM_STATE])
        cm.append(act[:, D_SSM + SSM_GROUPS * SSM_STATE:])
        b_ref[t * nb:(t + 1) * nb, :] = bm[t]
        c_ref[t * nb:(t + 1) * nb, :] = cm[t]

    dt = _softplus(dtr + dtb_ref[...])
    a_row = -jnp.exp(alog_ref[...])
    dts = [slab(dt, t) for t in range(nt)]
    cum = []
    for t in range(nt):
        da = dts[t] * a_row
        cum.append(da if t == 0 else cum[t - 1] + da)
    dec_ref[...] = jnp.exp(cum[nt - 1])

    lane = lax.broadcasted_iota(jnp.int32, (nb, LANES), 1)
    first_group = lane < HEADS_PER_GROUP
    facs = [jnp.exp(cum[t]) for t in range(nt)]
    facs += [dts[s] * jnp.exp(cum[nt - 1] - cum[s]) for s in range(nt)]
    pairs = []
    for t in range(nt):
        for s in range(t + 1):
            cb = []
            for g in range(SSM_GROUPS):
                gl = slice(g * SSM_STATE, (g + 1) * SSM_STATE)
                cb.append(jnp.sum(cm[t][:, gl] * bm[s][:, gl], axis=1, keepdims=True))
            cbh = jnp.where(first_group, cb[0], cb[1])
            facs.append(jnp.exp(cum[t] - cum[s]) * dts[s] * cbh)
            pairs.append((t, s))
    fx = _dot_f32(jnp.concatenate(facs, axis=0), expand_ref[...])
    fslab = lambda i: fx[i * nb:(i + 1) * nb]
    for t in range(nt):
        ea_ref[t * nb:(t + 1) * nb, :] = fslab(t)
        xw_ref[t * nb:(t + 1) * nb, :] = xs[t] * fslab(nt + t)
    for t in range(nt):
        acc = dskip_ref[...] * xs[t]
        for i, (tt, s) in enumerate(pairs):
            if tt == t:
                acc = acc + fslab(2 * nt + i) * xs[s]
        ypart_ref[t * nb:(t + 1) * nb, :] = acc

    u = uv[:, :D_SG]
    v = _rms(uv[:, D_SG:], sgnw_ref[...])
    v_ref[...] = v
    for t in range(nt):
        sv = b4_ref[t:t + 1, :]
        for s in range(t + 1):
            sv = sv + w4_ref[t, s:s + 1, :] * slab(v, s)
        ysg_ref[t * nb:(t + 1) * nb, :] = slab(u, t) * sv


def _sample_front(x_slab, cprev, w, nb, nt):
    r = nb * nt
    weights = [w['nmw'], w['wz'], w['wxbc'], w['wdt'], w['wuv'], w['convw'], w['convb'], w['dtb'],
               w['alog'], w['dskip'], w['sgnw'], w['w4'], w['b4'], w['expand']]
    ins = [x_slab, cprev] + weights
    out_shape = [
        jax.ShapeDtypeStruct((r, D_SSM), F32),
        jax.ShapeDtypeStruct((r, D_SSM), F32),
        jax.ShapeDtypeStruct((r, D_SSM), F32),
        jax.ShapeDtypeStruct((r, D_SG), F32),
        jax.ShapeDtypeStruct((r, D_SG), F32),
        jax.ShapeDtypeStruct((CONV_K - 1, nb, C_CONV), F32),
        jax.ShapeDtypeStruct((r, SSM_GROUPS * SSM_STATE), F32),
        jax.ShapeDtypeStruct((r, SSM_GROUPS * SSM_STATE), F32),
        jax.ShapeDtypeStruct((r, D_SSM), F32),
        jax.ShapeDtypeStruct((nb, LANES), F32),
    ]
    return pl.pallas_call(
        functools.partial(_sample_front_kernel, nb=nb, nt=nt),
        grid=(1,),
        in_specs=[_const_spec(a.shape) for a in ins],
        out_specs=[_const_spec(s.shape) for s in out_shape],
        out_shape=out_shape,
        compiler_params=pltpu.CompilerParams(
            dimension_semantics=("arbitrary",), vmem_limit_bytes=VMEM_LIMIT),
        name="sample_front",
    )(*ins)


def _sample_state_kernel(dec_ref, st_ref, cq_ref, bq_ref, xw_ref, so_ref, z_ref, *, seqs):
    for bb in range(seqs):
        for g in range(SSM_GROUPS):
            s0 = st_ref[bb, g]
            z_ref[bb, g] = lax.dot_general(
                cq_ref[bb, g].astype(BF16), s0.astype(BF16), (((1,), (1,)), ((), ())),
                preferred_element_type=F32)
            upd = lax.dot_general(
                xw_ref[bb, g].astype(BF16), bq_ref[bb, g].astype(BF16), (((0,), (0,)), ((), ())),
                preferred_element_type=F32)
            for hh in range(HEADS_PER_GROUP):
                hs = slice(hh * SSM_HEAD_DIM, (hh + 1) * SSM_HEAD_DIM)
                so_ref[bb, g, hs, :] = s0[hs] * dec_ref[bb, g * HEADS_PER_GROUP + hh] + upd[hs]


def _sample_state(dec, state, cq, bq, xwq, seqs):
    nb = state.shape[0]
    q = HEADS_PER_GROUP * SSM_HEAD_DIM
    blk = lambda *tail: pl.BlockSpec((seqs, SSM_GROUPS) + tail, lambda i: (i, 0, 0, 0))
    return pl.pallas_call(
        functools.partial(_sample_state_kernel, seqs=seqs),
        grid=(nb // seqs,),
        in_specs=[
            pl.BlockSpec((seqs, SSM_HEADS), lambda i: (i, 0), memory_space=pltpu.SMEM),
            blk(q, SSM_STATE), blk(SUBLANES, SSM_STATE), blk(SUBLANES, SSM_STATE), blk(SUBLANES, q),
        ],
        out_specs=[blk(q, SSM_STATE), blk(SUBLANES, q)],
        out_shape=[
            jax.ShapeDtypeStruct((nb, SSM_GROUPS, q, SSM_STATE), F32),
            jax.ShapeDtypeStruct((nb, SSM_GROUPS, SUBLANES, q), F32),
        ],
        compiler_params=pltpu.CompilerParams(
            dimension_semantics=("arbitrary",), vmem_limit_bytes=VMEM_LIMIT),
        name="sample_state",
    )(dec, state, cq, bq, xwq)


def _sample_back_kernel(x_ref, ypart_ref, ea_ref, zoff_ref, z_ref, ysg_ref, ssmnw_ref, wout_ref,
                        nfw_ref, wr_ref, br_ref, h_ref, hn2_ref, route_ref):
    y = ypart_ref[...] + ea_ref[...] * zoff_ref[...]
    y_ssm = _gated_group_norm(y, z_ref[...], ssmnw_ref[...])
    mix_in = jnp.concatenate([y_ssm.astype(BF16), ysg_ref[...].astype(BF16)], axis=1)
    h = x_ref[...] + _dot(mix_in, wout_ref[...])
    h_ref[...] = h
    hn2, route = _ffn_front(h, nfw_ref[...], wr_ref, br_ref[...])
    _store_row_tiles(hn2_ref, hn2)
    route_ref[...] = route


def _sample_back(x_slab, ypart, ea, zoff, z, ysg, w):
    r = x_slab.shape[0]
    ins = [x_slab, ypart, ea, zoff, z, ysg, w['ssmnw'], w['wout'], w['nfw'], w['wr'], w['br']]
    out_shape = [
        jax.ShapeDtypeStruct((r, D_MODEL), F32),
        jax.ShapeDtypeStruct((r * ROW_TILES, LANES), F32),
        jax.ShapeDtypeStruct((r, LANES), F32),
    ]
    return pl.pallas_call(
        _sample_back_kernel,
        grid=(1,),
        in_specs=[_const_spec(a.shape) for a in ins],
        out_specs=[_const_spec(s.shape) for s in out_shape],
        out_shape=out_shape,
        compiler_params=pltpu.CompilerParams(
            dimension_semantics=("arbitrary",), vmem_limit_bytes=VMEM_LIMIT),
        name="sample_back",
    )(*ins)


EXPERT_PIECES = 8
DMA_PIECES = 4


def _expert_mlp_kernel(be_ref, nu_ref, tok0_ref, tokn_ref, dstp_ref, x_hbm, wg_ref, wu_ref, wd_ref,
                       parts_hbm, xbuf, obuf, xb, hb, wg_b, wu_b, wd_b, gsem, ssem, *, bm):
    j = pl.program_id(0)
    nu = nu_ref[0]
    nt = ROW_TILES
    tile = lambda ref, start: ref.at[pl.ds(pl.multiple_of(start, nt), nt)]

    def gather_copy(idx_ref, s, r):
        return pltpu.make_async_copy(tile(x_hbm, idx_ref[0, r]), xbuf.at[s, pl.ds(r * nt, nt)], gsem.at[s])

    def scatter_copy(s, r):
        return pltpu.make_async_copy(obuf.at[s, pl.ds(r * nt, nt)], tile(parts_hbm, dstp_ref[0, r]),
                                     ssem.at[s])

    def gather_wait(s):
        pltpu.make_async_copy(x_hbm.at[pl.ds(0, bm * nt)], xbuf.at[s], gsem.at[s]).wait()

    def scatter_wait(s):
        pltpu.make_async_copy(obuf.at[s], parts_hbm.at[pl.ds(0, bm * nt)], ssem.at[s]).wait()

    def step(slot):
        gather_wait(slot)
        per = bm // DMA_PIECES

        def start_rows(piece):
            if piece >= DMA_PIECES:
                return
            for r in range(piece * per, (piece + 1) * per):
                gather_copy(tokn_ref, 1 - slot, r).start(priority=r % 2)
                scatter_copy(1 - slot, r).start(priority=(r + 1) % 2)

        xb[...] = _load_row_tiles(xbuf.at[slot]).astype(BF16)
        assert EXPERT_PIECES == 8
        half = D_FF // 2
        for c in range(2):
            cols = slice(c * half, (c + 1) * half)
            start_rows(2 * c)
            g = _dot(xb[...], wg_b[:, cols])
            start_rows(2 * c + 1)
            u = _dot(xb[...], wu_b[:, cols])
            hb[:, cols] = (_silu(g) * u).astype(BF16)
        quarter = D_MODEL // 4
        scatter_wait(slot)
        for c in range(4):
            start_rows(4 + c)
            o = _dot(hb[...], wd_b[:, c * quarter:(c + 1) * quarter])
            for k in range(quarter // LANES):
                kk = c * (quarter // LANES) + k
                obuf.at[slot][pl.ds(kk, bm, stride=nt), :] = o[:, k * LANES:(k + 1) * LANES]

    def drain(slot):
        gather_wait(slot)
        scatter_wait(slot)
        for r in range(bm):
            scatter_copy(1 - slot, r).start(priority=r % 2)
        scatter_wait(1 - slot)

    @pl.when(j == 0)
    def _():
        obuf[...] = jnp.zeros(obuf.shape, F32)
        n_real = parts_hbm.shape[0] - 2 * bm * nt
        pltpu.make_async_copy(obuf.at[0], parts_hbm.at[pl.ds(n_real, bm * nt)], ssem.at[0]).start()

        def start0(r, carry):
            pltpu.make_async_copy(tile(x_hbm, tok0_ref[0, r]), xbuf.at[0, pl.ds(pl.multiple_of(r * nt, nt), nt)],
                                  gsem.at[0]).start()
            return carry
        lax.fori_loop(0, bm, start0, 0)

    used = j < nu

    @pl.when(used & ((j == 0) | (be_ref[j] != be_ref[jnp.maximum(j - 1, 0)])))
    def _():
        wg_b[...] = wg_ref[...].astype(BF16)
        wu_b[...] = wu_ref[...].astype(BF16)
        wd_b[...] = wd_ref[...].astype(BF16)

    for slot in range(2):
        parity = lax.rem(j, 2) == slot

        @pl.when(used & parity)
        def _(slot=slot):
            step(slot)

        @pl.when((j == nu) & parity)
        def _(slot=slot):
            drain(slot)


def _expert_mlp(blk_expert, n_used, tok, dst, x, wg, wu, wd, bm, n_out):
    n_blocks = blk_expert.shape[0]
    assert bm % EXPERT_PIECES == 0
    w_map = lambda j, be, nu: (be[j], 0, 0)
    idx_spec = lambda f: pl.BlockSpec((None, 1, bm), lambda j, be, nu: (f(j), 0, 0),
                                      memory_space=pltpu.SMEM)
    any_spec = pl.BlockSpec(memory_space=pl.ANY)
    grid_spec = pltpu.PrefetchScalarGridSpec(
        num_scalar_prefetch=2,
        grid=(n_blocks + 1,),
        in_specs=[
            idx_spec(lambda j: 0),
            idx_spec(lambda j: jnp.minimum(j + 1, n_blocks - 1)),
            idx_spec(lambda j: j),
            any_spec,
            pl.BlockSpec((None, D_MODEL, D_FF), w_map),
            pl.BlockSpec((None, D_MODEL, D_FF), w_map),
            pl.BlockSpec((None, D_FF, D_MODEL), w_map),
        ],
        out_specs=any_spec,
        scratch_shapes=[
            pltpu.VMEM((2, bm * ROW_TILES, LANES), F32),
            pltpu.VMEM((2, bm * ROW_TILES, LANES), F32),
            pltpu.VMEM((bm, D_MODEL), BF16),
            pltpu.VMEM((bm, D_FF), BF16),
            pltpu.VMEM((D_MODEL, D_FF), BF16),
            pltpu.VMEM((D_MODEL, D_FF), BF16),
            pltpu.VMEM((D_FF, D_MODEL), BF16),
            pltpu.SemaphoreType.DMA((2,)),
            pltpu.SemaphoreType.DMA((2,)),
        ],
    )
    tok3 = (tok * ROW_TILES).reshape(n_blocks, 1, bm)
    spare = (n_out - bm + jnp.arange(bm, dtype=jnp.int32)).reshape(1, bm)
    dst_prev = (jnp.concatenate([spare, dst], axis=0) * ROW_TILES).reshape(n_blocks + 1, 1, bm)
    blk_expert = jnp.concatenate([blk_expert, blk_expert[-1:]])
    return pl.pallas_call(
        functools.partial(_expert_mlp_kernel, bm=bm),
        grid_spec=grid_spec,
        out_shape=jax.ShapeDtypeStruct((n_out * ROW_TILES, LANES), F32),
        compiler_params=pltpu.CompilerParams(
            dimension_semantics=("arbitrary",), vmem_limit_bytes=VMEM_LIMIT),
        name="expert_mlp",
    )(blk_expert, n_used, tok3, tok3, dst_prev, x, wg, wu, wd)


PAD_ID = 2 ** 16 - 1


def _moe_plan(e_idx, bm, n_blocks):
    n_tok = e_idx.shape[0]
    n_asg = 2 * n_tok
    n_pad = n_blocks * bm - n_asg
    e_flat = e_idx.reshape(-1)
    experts = jnp.arange(N_EXPERTS, dtype=jnp.int32)
    counts = jnp.sum((e_flat[:, None] == experts[None, :]).astype(jnp.int32), axis=0)
    nblk = (counts + bm - 1) // bm
    blk_end = jnp.cumsum(nblk)
    n_used = blk_end[-1]
    j = jnp.arange(n_blocks, dtype=jnp.int32)
    be = jnp.sum((blk_end[None, :] <= jnp.minimum(j, n_used - 1)[:, None]).astype(jnp.int32), axis=1)
    be = jnp.minimum(be, N_EXPERTS - 1)
    assert n_asg < PAD_ID
    pad_end = jnp.cumsum(nblk * bm - counts)
    pad_expert = jnp.sum((pad_end[None, :] <= jnp.arange(n_pad, dtype=jnp.int32)[:, None]).astype(jnp.int32),
                         axis=1)
    keys = jnp.concatenate([2 * e_flat, 2 * pad_expert + 1])
    ids = jnp.concatenate([jnp.arange(n_asg, dtype=jnp.int32), jnp.full((n_pad,), PAD_ID, jnp.int32)])
    window = lax.sort(keys * (PAD_ID + 1) + ids) % (PAD_ID + 1)
    window = window.reshape(n_blocks, bm)
    valid = window != PAD_ID
    q = jnp.arange(bm, dtype=jnp.int32)
    tok = jnp.where(valid, window // 2, (j[:, None] * bm + q[None, :]) % n_tok)
    dst = jnp.where(valid, (window % 2) * n_tok + window // 2,
                    2 * n_tok + (j[:, None] % 2) * bm + q[None, :])
    return tok.astype(jnp.int32), dst.astype(jnp.int32), be, n_used.astype(jnp.int32).reshape(1)


def _final_kernel(h_ref, p0_ref, p1_ref, route_ref, w_ref, op_ref, os_ref, *, n_prompt_steps):
    route = route_ref[...]
    p0 = _load_row_tiles(p0_ref)
    p1 = _load_row_tiles(p1_ref)
    y = _rms(h_ref[...] + route[:, 2:3] * p0 + route[:, 3:4] * p1, w_ref[...])
    i = pl.program_id(0)

    @pl.when(i < n_prompt_steps)
    def _():
        op_ref[...] = y

    @pl.when(i >= n_prompt_steps)
    def _():
        os_ref[...] = y


def _final(h, parts, route, w, tf, n_prompt):
    n_tok = h.shape[0]
    assert n_tok % tf == 0 and n_prompt % tf == 0
    n_p = n_prompt // tf
    row = lambda d: pl.BlockSpec((tf, d), lambda i: (i, 0))
    part = lambda k: pl.BlockSpec((tf * ROW_TILES, LANES), lambda i: (k * (n_tok // tf) + i, 0))
    return pl.pallas_call(
        functools.partial(_final_kernel, n_prompt_steps=n_p),
        grid=(n_tok // tf,),
        in_specs=[row(D_MODEL), part(0), part(1), row(LANES), _const_spec(w.shape)],
        out_specs=[pl.BlockSpec((tf, D_MODEL), lambda i: (jnp.minimum(i, n_p - 1), 0)),
                   pl.BlockSpec((tf, D_MODEL), lambda i: (jnp.maximum(i - n_p, 0), 0))],
        out_shape=[jax.ShapeDtypeStruct((n_prompt, D_MODEL), F32),
                   jax.ShapeDtypeStruct((n_tok - n_prompt, D_MODEL), F32)],
        compiler_params=pltpu.CompilerParams(
            dimension_semantics=("arbitrary",), vmem_limit_bytes=VMEM_LIMIT),
        name="final_combine",
    )(h, parts, parts, route, w)


PROMPT_TILE = 256
MOE_BLOCK_ROWS = 512
FINAL_TILE = 512
STATE_SEQS = 8


def _pad_lanes(a, width=LANES):
    return jnp.pad(a, [(0, 0)] * (a.ndim - 1) + [(0, width - a.shape[-1])])


def _layer_weights(l, norm_mix_w, w_in, conv_w, conv_b, dt_bias, a_log, d_skip, ssm_norm_w, sg_norm_w,
                   sg_w, sg_b, w_out, norm_ffn_w, w_router_group, b_router_group, w_router_expert,
                   b_router_expert, n_dec):
    c0, c1, c2 = D_SSM, D_SSM + C_CONV, D_SSM + C_CONV + SSM_HEADS
    wi = w_in[l]
    causal = jnp.tril(jnp.ones((CHUNK, CHUNK), bool))
    sgw_l = jnp.where(causal, sg_w[l], 0.0)
    head_of_lane = jnp.arange(D_SSM, dtype=jnp.int32) // SSM_HEAD_DIM
    expand = (jnp.arange(LANES, dtype=jnp.int32)[:, None] == head_of_lane[None, :]).astype(F32)
    row = lambda a: a.reshape(1, -1).astype(F32)
    gap = ROUTER_EXPERT_ROW - N_EXPERT_GROUPS
    rest = LANES - ROUTER_EXPERT_ROW - N_EXPERTS
    wrt = jnp.concatenate([w_router_group[l].T, jnp.zeros((gap, D_MODEL), F32), w_router_expert[l].T,
                           jnp.zeros((rest, D_MODEL), F32)], axis=0).astype(F32)
    brc = jnp.concatenate([b_router_group[l], jnp.zeros((gap,), F32), b_router_expert[l],
                           jnp.zeros((rest,), F32)]).astype(F32).reshape(LANES, 1)
    return {
        'nmw': row(norm_mix_w[l]),
        'wz': wi[:, :c0].astype(BF16),
        'wxbc': wi[:, c0:c1].astype(BF16),
        'wdt': _pad_lanes(wi[:, c1:c2]).astype(BF16),
        'wuv': wi[:, c2:].astype(BF16),
        'convw': conv_w[l].astype(F32),
        'convb': row(conv_b[l]),
        'dtb': _pad_lanes(row(dt_bias[l])),
        'alog': _pad_lanes(row(a_log[l])),
        'dskip': row(jnp.repeat(d_skip[l], SSM_HEAD_DIM)),
        'ssmnw': row(ssm_norm_w[l]),
        'sgnw': row(sg_norm_w[l]),
        'sgw': sgw_l.astype(BF16),
        'sgb': jnp.repeat(sg_b[l].T, SG_HEAD_DIM, axis=1).astype(F32),
        'w4': jnp.repeat(jnp.transpose(sgw_l[:, :n_dec, :n_dec], (1, 2, 0)), SG_HEAD_DIM, axis=2).astype(F32),
        'b4': jnp.repeat(sg_b[l][:, :n_dec].T, SG_HEAD_DIM, axis=1).astype(F32),
        'expand': expand,
        'wout': w_out[l].astype(BF16),
        'nfw': row(norm_ffn_w[l]),
        'wr': jnp.stack([wrt.astype(BF16), (wrt - wrt.astype(BF16).astype(F32)).astype(BF16)]),
        'br': brc,
    }


def _layer(l, xp, xs_slab, state_conv, state_ssm, w, w_gate, w_up, w_down):
    bp, seq, _ = xp.shape
    nb, nt = state_conv.shape[1], xs_slab.shape[0] // state_conv.shape[1]
    q = HEADS_PER_GROUP * SSM_HEAD_DIM

    cprev = jnp.transpose(state_conv[l], (1, 0, 2))
    z, ypart, ea, ysg, v, convn, cmat, bmat, xw, dec = _sample_front(xs_slab, cprev, w, nb, nt)
    to_seq = lambda a, d: jnp.pad(
        jnp.transpose(a.reshape(nt, nb, SSM_GROUPS, d), (1, 2, 0, 3)),
        ((0, 0), (0, 0), (0, SUBLANES - nt), (0, 0)))
    ssm_s, zoff = _sample_state(
        dec[:, :SSM_HEADS], state_ssm[l].reshape(nb, SSM_GROUPS, q, SSM_STATE),
        to_seq(cmat, SSM_STATE), to_seq(bmat, SSM_STATE), to_seq(xw, q), STATE_SEQS)
    zoff = jnp.transpose(zoff[:, :, :nt], (2, 0, 1, 3)).reshape(nt * nb, D_SSM)
    h_s, hn2_s, route_s = _sample_back(xs_slab, ypart, ea, zoff, z, ysg, w)

    h, hn2, route, convt_p, ssm_p = _prompt_mixer(xp, w, PROMPT_TILE, h_s, hn2_s, route_s)

    n_tok = h.shape[0]
    bm = MOE_BLOCK_ROWS
    n_blocks = -(-(2 * n_tok) // bm) + N_EXPERTS
    tok, dst, blk_expert, n_used = _moe_plan(route[:, :2].astype(jnp.int32), bm, n_blocks)
    parts = _expert_mlp(blk_expert, n_used, tok, dst, hn2, w_gate[l], w_up[l], w_down[l], bm,
                        2 * n_tok + 2 * bm)
    outs = dict(
        h=h, parts=parts, route=route,
        conv_p=convt_p[:, SUBLANES - (CONV_K - 1):],
        ssm_p=ssm_p.reshape(bp, SSM_GROUPS, HEADS_PER_GROUP, SSM_HEAD_DIM, SSM_STATE),
        conv_s=jnp.transpose(convn, (1, 0, 2)),
        ssm_s=ssm_s.reshape(nb, SSM_GROUPS, HEADS_PER_GROUP, SSM_HEAD_DIM, SSM_STATE),
        v_s=jnp.transpose(v.reshape(nt, nb, D_SG), (1, 0, 2)),
    )
    return outs


def kernel(x_prompt, x_sample, state_conv, state_ssm, norm_mix_w, w_in, conv_w, conv_b, dt_bias, a_log, d_skip, ssm_norm_w, sg_norm_w, sg_w, sg_b, w_out, norm_ffn_w, w_router_group, b_router_group, w_router_expert, b_router_expert, w_gate, w_up, w_down, norm_final_w):
    depth = w_in.shape[0]
    assert depth == 1, "the fused final norm assumes a single layer"
    bp, seq, _ = x_prompt.shape
    nb, nt, _ = x_sample.shape
    l = 0
    w = _layer_weights(l, norm_mix_w, w_in, conv_w, conv_b, dt_bias, a_log, d_skip, ssm_norm_w, sg_norm_w,
                       sg_w, sg_b, w_out, norm_ffn_w, w_router_group, b_router_group, w_router_expert,
                       b_router_expert, nt)
    xs_slab = jnp.transpose(x_sample, (1, 0, 2)).reshape(nt * nb, D_MODEL)
    outs = _layer(l, x_prompt, xs_slab, state_conv, state_ssm, w, w_gate, w_up, w_down)
    y_p, y_s = _final(outs['h'], outs['parts'], outs['route'], norm_final_w.reshape(1, -1).astype(F32),
                      FINAL_TILE, bp * seq)
    y_prompt = y_p.reshape(bp, seq, D_MODEL)
    y_sample = jnp.transpose(y_s.reshape(nt, nb, D_MODEL), (1, 0, 2))
    return (y_prompt, y_sample, outs['conv_p'][None], outs['ssm_p'][None], outs['conv_s'][None],
            outs['ssm_s'][None], outs['v_s'][None])
```

```python
import functools

import jax
import jax.numpy as jnp
from jax import lax
from jax.experimental import pallas as pl
from jax.experimental.pallas import tpu as pltpu

D_MODEL = 1024
D_SSM = 1024
SSM_HEAD_DIM = 64
SSM_HEADS = 16
SSM_GROUPS = 2
HEADS_PER_GROUP = 8
SSM_STATE = 128
CONV_K = 4
C_CONV = D_SSM + 2 * SSM_GROUPS * SSM_STATE
D_SG = 1024
SG_HEADS = 8
SG_HEAD_DIM = 128
CHUNK = 128
N_EXPERT_GROUPS = 4
EXPERTS_PER_GROUP = 8
N_EXPERTS = 32
D_FF = 512
EPS = 1e-6
LOG2_E = 1.4426950408889634

LANES = 128
SUBLANES = 8
VMEM_LIMIT = 56 * 1024 * 1024

F32 = jnp.float32
BF16 = jnp.bfloat16
HIGHEST = lax.Precision.HIGHEST


def _dot(a, b):
    return jnp.dot(a, b, preferred_element_type=F32)


def _dot_f32(a, b):
    return jnp.dot(a, b, preferred_element_type=F32, precision=HIGHEST)


def _rms(x, w):
    return x * lax.rsqrt(jnp.mean(x * x, axis=-1, keepdims=True) + EPS) * w


def _silu(x):
    return x * (0.5 + 0.5 * jnp.tanh(0.5 * x))


def _gelu(x):
    return 0.5 * x * (1.0 + lax.erf(x * 0.7071067811865476))


def _softplus(x):
    return jnp.maximum(x, 0.0) + jnp.log1p(jnp.exp(-jnp.abs(x)))


def _gated_group_norm(y, z, w):
    g = y * _silu(z)
    half = D_SSM // SSM_GROUPS
    parts = []
    for k in range(SSM_GROUPS):
        gk = g[:, k * half:(k + 1) * half]
        parts.append(gk * lax.rsqrt(jnp.mean(gk * gk, axis=-1, keepdims=True) + EPS))
    return jnp.concatenate(parts, axis=1) * w


ROUTER_GROUP_ROW = 0
ROUTER_EXPERT_ROW = SUBLANES


def _route_t(lt):
    r = lt.shape[1]
    row = lax.broadcasted_iota(jnp.int32, (SUBLANES, r), 0)
    rowf = row.astype(F32)
    big = float(SUBLANES)
    red = lambda f, a: f(a, axis=0, keepdims=True)
    gl = jnp.where(row < N_EXPERT_GROUPS, lt[ROUTER_GROUP_ROW:ROUTER_GROUP_ROW + SUBLANES], -jnp.inf)
    ge = jnp.exp(gl - red(jnp.max, gl))
    p_grp = ge / red(jnp.sum, ge)
    g_p = red(jnp.max, p_grp)
    g_idx = red(jnp.min, jnp.where(p_grp == g_p, rowf, big))
    el = lt[ROUTER_EXPERT_ROW:ROUTER_EXPERT_ROW + EXPERTS_PER_GROUP]
    for g in range(1, N_EXPERT_GROUPS):
        lo = ROUTER_EXPERT_ROW + g * EXPERTS_PER_GROUP
        el = jnp.where(g_idx == float(g), lt[lo:lo + EXPERTS_PER_GROUP], el)
    ee = jnp.exp(el - red(jnp.max, el))
    pe = ee / red(jnp.sum, ee)
    v1 = red(jnp.max, pe)
    i1 = red(jnp.min, jnp.where(pe == v1, rowf, big))
    rest = rowf != i1
    pe2 = jnp.where(rest, pe, -1.0)
    v2 = red(jnp.max, pe2)
    i2 = red(jnp.min, jnp.where(rest & (pe2 == v2), rowf, big))
    den = v1 + v2
    base = g_idx * float(EXPERTS_PER_GROUP)
    out = jnp.where(row == 0, base + i1, 0.0)
    out = jnp.where(row == 1, base + i2, out)
    out = jnp.where(row == 2, g_p * v1 / den, out)
    out = jnp.where(row == 3, g_p * v2 / den, out)
    return jnp.concatenate([out, jnp.zeros((LANES - SUBLANES, r), F32)], axis=0).T


def _ffn_front(h, nfw, wrt_ref, brc):
    hn2 = _rms(h, nfw)
    hi = hn2.astype(BF16)
    lo = (hn2 - hi.astype(F32)).astype(BF16)
    nt = lambda a, b: lax.dot_general(a, b, (((1,), (1,)), ((), ())), preferred_element_type=F32)
    lt = nt(wrt_ref[0], hi) + nt(wrt_ref[0], lo) + nt(wrt_ref[1], hi) + brc
    return hn2, _route_t(lt)


ROW_TILES = D_MODEL // LANES


def _store_row_tiles(ref, val):
    r = val.shape[0]
    for k in range(ROW_TILES):
        ref[pl.ds(k, r, stride=ROW_TILES), :] = val[:, k * LANES:(k + 1) * LANES]


def _load_row_tiles(ref):
    r = ref.shape[0] // ROW_TILES
    return jnp.concatenate([ref[pl.ds(k, r, stride=ROW_TILES), :] for k in range(ROW_TILES)], axis=1)


def _prompt_step(
        t, nt, fillers, x_ref, nmw_ref, wz_ref, wxbc_ref, wdt_ref, wuv_ref, convw_ref, convb_ref, dtb_ref,
        alog_ref, dskip_ref, ssmnw_ref, sgnw_ref, sgw_ref, sgb_ref, convt_ref, ssm_ref,
        xbc_scr, state_scr, y_scr, mix_scr, tl):
    fillers = list(fillers)
    assert len(fillers) == 3 + tl // CHUNK

    @pl.when(t == 0)
    def _():
        xbc_scr[0:SUBLANES, :] = jnp.zeros((SUBLANES, C_CONV), F32)
        state_scr[...] = jnp.zeros(state_scr.shape, F32)

    x = x_ref[...]
    hn = _rms(x, nmw_ref[...]).astype(BF16)
    z = _dot(hn, wz_ref[...])
    xbc = _dot(hn, wxbc_ref[...])
    dtr = _dot(hn, wdt_ref[...])
    uv = _gelu(_dot(hn, wuv_ref[...]))
    fillers.pop(0)()

    xbc_scr[SUBLANES:SUBLANES + tl, :] = xbc
    conv = convb_ref[...]
    for k in range(CONV_K):
        off = SUBLANES - (CONV_K - 1) + k
        conv = conv + convw_ref[k:k + 1, :] * xbc_scr[off:off + tl, :]
    tail = xbc_scr[tl:tl + SUBLANES, :]
    xbc_scr[0:SUBLANES, :] = tail
    convt_ref[...] = tail
    act = _silu(conv)
    xs = act[:, :D_SSM]
    bmat = act[:, D_SSM:D_SSM + SSM_GROUPS * SSM_STATE]
    cmat = act[:, D_SSM + SSM_GROUPS * SSM_STATE:]
    fillers.pop(0)()

    dt = _softplus(dtr + dtb_ref[...])
    a_row = -jnp.exp(alog_ref[...]) * LOG2_E

    li = lax.broadcasted_iota(jnp.int32, (CHUNK, CHUNK), 0)
    si = lax.broadcasted_iota(jnp.int32, (CHUNK, CHUNK), 1)
    causal = li >= si
    tri = jnp.where(causal, 1.0, 0.0).astype(F32)
    lo_half = si < SSM_HEAD_DIM

    for c in range(tl // CHUNK):
        rows = slice(c * CHUNK, (c + 1) * CHUNK)
        dt_c = dt[rows]
        acum = _dot_f32(tri, dt_c * a_row)
        acum_t = acum.T
        dt_t = dt_c.T
        w_t = dt_t * jnp.exp2(acum_t[:, CHUNK - 1:CHUNK] - acum_t)
        xs_c = xs[rows]
        for g in range(SSM_GROUPS):
            gl = slice(g * SSM_STATE, (g + 1) * SSM_STATE)
            b_g = bmat[rows, gl]
            c_g = cmat[rows, gl]
            b_gt = b_g.T
            cb = _dot(c_g.astype(BF16), b_gt.astype(BF16))
            for j in range(HEADS_PER_GROUP // 2):
                gw = HEADS_PER_GROUP * SSM_HEAD_DIM
                pl_ = slice(g * gw + j * LANES, g * gw + (j + 1) * LANES)
                m_l, ec_l, s_l, ea_last = [], [], [], []
                for k in range(2):
                    hh = g * HEADS_PER_GROUP + 2 * j + k
                    colb = jnp.broadcast_to(acum[:, hh:hh + 1], (CHUNK, CHUNK))
                    rowb = jnp.broadcast_to(acum_t[hh:hh + 1, :], (CHUNK, CHUNK))
                    ea = jnp.exp2(colb)
                    lmat = jnp.where(causal, jnp.exp2(colb - rowb), 0.0)
                    m_l.append(cb * lmat * jnp.broadcast_to(dt_t[hh:hh + 1, :], (CHUNK, CHUNK)))
                    ec_l.append(c_g * ea)
                    s_l.append(b_gt * jnp.broadcast_to(w_t[hh:hh + 1, :], (CHUNK, CHUNK)))
                    ea_last.append(ea[CHUNK - 1:CHUNK, :])
                xs_p = xs_c[:, pl_]
                st_p = state_scr[g, :, j * LANES:(j + 1) * LANES]
                rx = jnp.concatenate([jnp.where(lo_half, xs_p, 0.0),
                                      jnp.where(lo_half, 0.0, xs_p)], axis=0).astype(BF16)
                rs = jnp.concatenate([jnp.where(lo_half, st_p, 0.0),
                                      jnp.where(lo_half, 0.0, st_p)], axis=0).astype(BF16)
                lhs = jnp.concatenate(m_l + ec_l, axis=1).astype(BF16)
                y_p = _dot(lhs, jnp.concatenate([rx, rs], axis=0))
                y_scr[rows, pl_] = y_p
                dec = jnp.where(lo_half[0:1, :], ea_last[0], ea_last[1])
                upd = _dot(jnp.concatenate(s_l, axis=1).astype(BF16), rx)
                state_scr[g, :, j * LANES:(j + 1) * LANES] = st_p * dec + upd
        fillers.pop(0)()

    y = y_scr[...] + dskip_ref[...] * xs
    y_ssm = _gated_group_norm(y, z, ssmnw_ref[...])
    mix_scr[:, :D_SSM] = y_ssm.astype(BF16)

    u = uv[:, :D_SG]
    v = _rms(uv[:, D_SG:], sgnw_ref[...])
    vb = v.astype(BF16)
    for c in range(tl // CHUNK):
        rows = slice(c * CHUNK, (c + 1) * CHUNK)
        for hd in range(SG_HEADS):
            hl = slice(hd * SG_HEAD_DIM, (hd + 1) * SG_HEAD_DIM)
            sv = _dot(sgw_ref[hd], vb[rows, hl]) + sgb_ref[:, hl]
            mix_scr[rows, D_SSM + hd * SG_HEAD_DIM:D_SSM + (hd + 1) * SG_HEAD_DIM] = (
                u[rows, hl] * sv).astype(BF16)

    fillers.pop(0)()

    @pl.when(t == nt - 1)
    def _():
        for g in range(SSM_GROUPS):
            ssm_ref[g] = state_scr[g].T


def _prompt_mixer_kernel(
        x_ref, xp_ref, nmw_ref, wz_ref, wxbc_ref, wdt_ref, wuv_ref, convw_ref, convb_ref, dtb_ref,
        alog_ref, dskip_ref, ssmnw_ref, sgnw_ref, sgw_ref, sgb_ref, wout_ref, nfw_ref,
        wr_ref, br_ref, hs_ref, hn2s_ref, routes_ref,
        h_ref, hn2_ref, route_ref, convt_ref, ssm_ref,
        xbc_scr, state_scr, y_scr, mix_scr, h_scr, *, tl, nt, n_main):
    i = pl.program_id(0)

    @pl.when(i == 0)
    def _():
        mix_scr[...] = jnp.zeros(mix_scr.shape, BF16)

    @pl.when(i > n_main)
    def _():
        h_ref[...] = hs_ref[...]
        hn2_ref[...] = hn2s_ref[...]
        route_ref[...] = routes_ref[...]

    n_out_chunks = 2 + tl // CHUNK
    width = D_MODEL // n_out_chunks
    assert width % LANES == 0 and width * n_out_chunks == D_MODEL

    def out_chunk(c):
        cols = slice(c * width, (c + 1) * width)
        h_scr[:, cols] = xp_ref[:, cols] + _dot(mix_scr[...], wout_ref[:, cols])

    def finish():
        h = h_scr[...]
        h_ref[...] = h
        hn2, route = _ffn_front(h, nfw_ref[...], wr_ref, br_ref[...])
        _store_row_tiles(hn2_ref, hn2)
        route_ref[...] = route

    def out_chunks(cs):
        for c in cs:
            out_chunk(c)

    @pl.when(i == n_main)
    def _():
        out_chunks(range(n_out_chunks))
        finish()

    @pl.when(i < n_main)
    def _():
        half = n_out_chunks // 2
        fillers = ([functools.partial(out_chunks, range(half)),
                    functools.partial(out_chunks, range(half, n_out_chunks)), finish]
                   + [lambda: None] * (tl // CHUNK))
        _prompt_step(
            lax.rem(i, nt), nt, fillers, x_ref, nmw_ref, wz_ref, wxbc_ref, wdt_ref, wuv_ref,
            convw_ref, convb_ref, dtb_ref, alog_ref, dskip_ref, ssmnw_ref, sgnw_ref, sgw_ref, sgb_ref,
            convt_ref, ssm_ref, xbc_scr, state_scr, y_scr, mix_scr, tl)


def _const_spec(shape):
    zeros = (0,) * len(shape)
    return pl.BlockSpec(shape, lambda *_: zeros)


def _prompt_mixer(x, w, tl, h_s, hn2_s, route_s):
    bsz, seq, _ = x.shape
    nt = seq // tl
    n_main = bsz * nt
    n_s = h_s.shape[0]
    assert seq % tl == 0 and n_s % tl == 0
    n_tok = bsz * seq + n_s
    x2 = x.reshape(bsz * seq, D_MODEL)
    main = lambda i: jnp.minimum(i, n_main - 1)
    prev = lambda i: jnp.clip(i - 1, 0, n_main - 1)
    tail = lambda i: jnp.maximum(i - n_main - 1, 0)
    out_row = lambda i: jnp.maximum(i - 1, 0)
    weights = [w['nmw'], w['wz'], w['wxbc'], w['wdt'], w['wuv'], w['convw'], w['convb'], w['dtb'],
               w['alog'], w['dskip'], w['ssmnw'], w['sgnw'], w['sgw'], w['sgb'], w['wout'], w['nfw'],
               w['wr'], w['br']]
    in_specs = ([pl.BlockSpec((tl, D_MODEL), lambda i: (main(i), 0)),
                 pl.BlockSpec((tl, D_MODEL), lambda i: (prev(i), 0))]
                + [_const_spec(a.shape) for a in weights]
                + [pl.BlockSpec(blk, lambda i: (tail(i), 0))
                   for blk in ((tl, D_MODEL), (tl * ROW_TILES, LANES), (tl, LANES))])
    q = HEADS_PER_GROUP * SSM_HEAD_DIM
    out_shape = [
        jax.ShapeDtypeStruct((n_tok, D_MODEL), F32),
        jax.ShapeDtypeStruct((n_tok * ROW_TILES, LANES), F32),
        jax.ShapeDtypeStruct((n_tok, LANES), F32),
        jax.ShapeDtypeStruct((bsz, SUBLANES, C_CONV), F32),
        jax.ShapeDtypeStruct((bsz, SSM_GROUPS, q, SSM_STATE), F32),
    ]
    out_specs = [
        pl.BlockSpec((tl, D_MODEL), lambda i: (out_row(i), 0)),
        pl.BlockSpec((tl * ROW_TILES, LANES), lambda i: (out_row(i), 0)),
        pl.BlockSpec((tl, LANES), lambda i: (out_row(i), 0)),
        pl.BlockSpec((None, SUBLANES, C_CONV), lambda i: (main(i) // nt, 0, 0)),
        pl.BlockSpec((None, SSM_GROUPS, q, SSM_STATE), lambda i: (main(i) // nt, 0, 0, 0)),
    ]
    scratch = [
        pltpu.VMEM((tl + SUBLANES, C_CONV), F32),
        pltpu.VMEM((SSM_GROUPS, SSM_STATE, q), F32),
        pltpu.VMEM((tl, D_SSM), F32),
        pltpu.VMEM((tl, D_SSM + D_SG), BF16),
        pltpu.VMEM((tl, D_MODEL), F32),
    ]
    return pl.pallas_call(
        functools.partial(_prompt_mixer_kernel, tl=tl, nt=nt, n_main=n_main),
        grid=(n_main + 1 + n_s // tl,),
        in_specs=in_specs,
        out_specs=out_specs,
        out_shape=out_shape,
        scratch_shapes=scratch,
        compiler_params=pltpu.CompilerParams(
            dimension_semantics=("arbitrary",), vmem_limit_bytes=VMEM_LIMIT),
        name="prompt_mixer",
    )(x2, x2, *weights, h_s, hn2_s, route_s)


def _sample_front_kernel(
        x_ref, cprev_ref, nmw_ref, wz_ref, wxbc_ref, wdt_ref, wuv_ref, convw_ref, convb_ref,
        dtb_ref, alog_ref, dskip_ref, sgnw_ref, w4_ref, b4_ref, expand_ref,
        z_ref, ypart_ref, ea_ref, ysg_ref, v_ref, convn_ref, c_ref, b_ref, xw_ref, dec_ref,
        *, nb, nt):
    x = x_ref[...]
    hn = _rms(x, nmw_ref[...]).astype(BF16)
    z_ref[...] = _dot(hn, wz_ref[...])
    xbc = _dot(hn, wxbc_ref[...])
    dtr = _dot(hn, wdt_ref[...])
    uv = _gelu(_dot(hn, wuv_ref[...]))

    slab = lambda a, t: a[t * nb:(t + 1) * nb]
    full = [cprev_ref[k] for k in range(CONV_K - 1)] + [slab(xbc, t) for t in range(nt)]
    for k in range(CONV_K - 1):
        convn_ref[k] = full[nt + k]
    xs, bm, cm = [], [], []
    for t in range(nt):
        conv = convb_ref[...]
        for k in range(CONV_K):
            conv = conv + convw_ref[k:k + 1, :] * full[t + k]
        act = _silu(conv)
        xs.append(act[:, :D_SSM])
        bm.append(act[:, D_SSM:D_SSM + SSM_GROUPS * SSM_STATE])
        cm.append(act[:, D_SSM + SSM_GROUPS * SSM_STATE:])
        b_ref[t * nb:(t + 1) * nb, :] = bm[t]
        c_ref[t * nb:(t + 1) * nb, :] = cm[t]

    dt = _softplus(dtr + dtb_ref[...])
    a_row = -jnp.exp(alog_ref[...])
    dts = [slab(dt, t) for t in range(nt)]
    cum = []
    for t in range(nt):
        da = dts[t] * a_row
        cum.append(da if t == 0 else cum[t - 1] + da)
    dec_ref[...] = jnp.exp(cum[nt - 1])

    lane = lax.broadcasted_iota(jnp.int32, (nb, LANES), 1)
    first_group = lane < HEADS_PER_GROUP
    facs = [jnp.exp(cum[t]) for t in range(nt)]
    facs += [dts[s] * jnp.exp(cum[nt - 1] - cum[s]) for s in range(nt)]
    pairs = []
    for t in range(nt):
        for s in range(t + 1):
            cb = []
            for g in range(SSM_GROUPS):
                gl = slice(g * SSM_STATE, (g + 1) * SSM_STATE)
                cb.append(jnp.sum(cm[t][:, gl] * bm[s][:, gl], axis=1, keepdims=True))
            cbh = jnp.where(first_group, cb[0], cb[1])
            facs.append(jnp.exp(cum[t] - cum[s]) * dts[s] * cbh)
            pairs.append((t, s))
    fx = _dot_f32(jnp.concatenate(facs, axis=0), expand_ref[...])
    fslab = lambda i: fx[i * nb:(i + 1) * nb]
    for t in range(nt):
        ea_ref[t * nb:(t + 1) * nb, :] = fslab(t)
        xw_ref[t * nb:(t + 1) * nb, :] = xs[t] * fslab(nt + t)
    for t in range(nt):
        acc = dskip_ref[...] * xs[t]
        for i, (tt, s) in enumerate(pairs):
            if tt == t:
                acc = acc + fslab(2 * nt + i) * xs[s]
        ypart_ref[t * nb:(t + 1) * nb, :] = acc

    u = uv[:, :D_SG]
    v = _rms(uv[:, D_SG:], sgnw_ref[...])
    v_ref[...] = v
    for t in range(nt):
        sv = b4_ref[t:t + 1, :]
        for s in range(t + 1):
            sv = sv + w4_ref[t, s:s + 1, :] * slab(v, s)
        ysg_ref[t * nb:(t + 1) * nb, :] = slab(u, t) * sv


def _sample_front(x_slab, cprev, w, nb, nt):
    r = nb * nt
    weights = [w['nmw'], w['wz'], w['wxbc'], w['wdt'], w['wuv'], w['convw'], w['convb'], w['dtb'],
               w['alog'], w['dskip'], w['sgnw'], w['w4'], w['b4'], w['expand']]
    ins = [x_slab, cprev] + weights
    out_shape = [
        jax.ShapeDtypeStruct((r, D_SSM), F32),
        jax.ShapeDtypeStruct((r, D_SSM), F32),
        jax.ShapeDtypeStruct((r, D_SSM), F32),
        jax.ShapeDtypeStruct((r, D_SG), F32),
        jax.ShapeDtypeStruct((r, D_SG), F32),
        jax.ShapeDtypeStruct((CONV_K - 1, nb, C_CONV), F32),
        jax.ShapeDtypeStruct((r, SSM_GROUPS * SSM_STATE), F32),
        jax.ShapeDtypeStruct((r, SSM_GROUPS * SSM_STATE), F32),
        jax.ShapeDtypeStruct((r, D_SSM), F32),
        jax.ShapeDtypeStruct((nb, LANES), F32),
    ]
    return pl.pallas_call(
        functools.partial(_sample_front_kernel, nb=nb, nt=nt),
        grid=(1,),
        in_specs=[_const_spec(a.shape) for a in ins],
        out_specs=[_const_spec(s.shape) for s in out_shape],
        out_shape=out_shape,
        compiler_params=pltpu.CompilerParams(
            dimension_semantics=("arbitrary",), vmem_limit_bytes=VMEM_LIMIT),
        name="sample_front",
    )(*ins)


def _sample_state_kernel(dec_ref, st_ref, cq_ref, bq_ref, xw_ref, so_ref, z_ref, *, seqs):
    for bb in range(seqs):
        for g in range(SSM_GROUPS):
            s0 = st_ref[bb, g]
            z_ref[bb, g] = lax.dot_general(
                cq_ref[bb, g].astype(BF16), s0.astype(BF16), (((1,), (1,)), ((), ())),
                preferred_element_type=F32)
            upd = lax.dot_general(
                xw_ref[bb, g].astype(BF16), bq_ref[bb, g].astype(BF16), (((0,), (0,)), ((), ())),
                preferred_element_type=F32)
            for hh in range(HEADS_PER_GROUP):
                hs = slice(hh * SSM_HEAD_DIM, (hh + 1) * SSM_HEAD_DIM)
                so_ref[bb, g, hs, :] = s0[hs] * dec_ref[bb, g * HEADS_PER_GROUP + hh] + upd[hs]


def _sample_state(dec, state, cq, bq, xwq, seqs):
    nb = state.shape[0]
    q = HEADS_PER_GROUP * SSM_HEAD_DIM
    blk = lambda *tail: pl.BlockSpec((seqs, SSM_GROUPS) + tail, lambda i: (i, 0, 0, 0))
    return pl.pallas_call(
        functools.partial(_sample_state_kernel, seqs=seqs),
        grid=(nb // seqs,),
        in_specs=[
            pl.BlockSpec((seqs, SSM_HEADS), lambda i: (i, 0), memory_space=pltpu.SMEM),
            blk(q, SSM_STATE), blk(SUBLANES, SSM_STATE), blk(SUBLANES, SSM_STATE), blk(SUBLANES, q),
        ],
        out_specs=[blk(q, SSM_STATE), blk(SUBLANES, q)],
        out_shape=[
            jax.ShapeDtypeStruct((nb, SSM_GROUPS, q, SSM_STATE), F32),
            jax.ShapeDtypeStruct((nb, SSM_GROUPS, SUBLANES, q), F32),
        ],
        compiler_params=pltpu.CompilerParams(
            dimension_semantics=("arbitrary",), vmem_limit_bytes=VMEM_LIMIT),
        name="sample_state",
    )(dec, state, cq, bq, xwq)


def _sample_back_kernel(x_ref, ypart_ref, ea_ref, zoff_ref, z_ref, ysg_ref, ssmnw_ref, wout_ref,
                        nfw_ref, wr_ref, br_ref, h_ref, hn2_ref, route_ref):
    y = ypart_ref[...] + ea_ref[...] * zoff_ref[...]
    y_ssm = _gated_group_norm(y, z_ref[...], ssmnw_ref[...])
    mix_in = jnp.concatenate([y_ssm.astype(BF16), ysg_ref[...].astype(BF16)], axis=1)
    h = x_ref[...] + _dot(mix_in, wout_ref[...])
    h_ref[...] = h
    hn2, route = _ffn_front(h, nfw_ref[...], wr_ref, br_ref[...])
    _store_row_tiles(hn2_ref, hn2)
    route_ref[...] = route


def _sample_back(x_slab, ypart, ea, zoff, z, ysg, w):
    r = x_slab.shape[0]
    ins = [x_slab, ypart, ea, zoff, z, ysg, w['ssmnw'], w['wout'], w['nfw'], w['wr'], w['br']]
    out_shape = [
        jax.ShapeDtypeStruct((r, D_MODEL), F32),
        jax.ShapeDtypeStruct((r * ROW_TILES, LANES), F32),
        jax.ShapeDtypeStruct((r, LANES), F32),
    ]
    return pl.pallas_call(
        _sample_back_kernel,
        grid=(1,),
        in_specs=[_const_spec(a.shape) for a in ins],
        out_specs=[_const_spec(s.shape) for s in out_shape],
        out_shape=out_shape,
        compiler_params=pltpu.CompilerParams(
            dimension_semantics=("arbitrary",), vmem_limit_bytes=VMEM_LIMIT),
        name="sample_back",
    )(*ins)


EXPERT_PIECES = 8
DMA_PIECES = 4


def _expert_mlp_kernel(be_ref, nu_ref, tok0_ref, tokn_ref, dstp_ref, x_hbm, wg_ref, wu_ref, wd_ref,
                       parts_hbm, xbuf, obuf, xb, hb, wg_b, wu_b, wd_b, gsem, ssem, *, bm):
    j = pl.program_id(0)
    nu = nu_ref[0]
    nt = ROW_TILES
    tile = lambda ref, start: ref.at[pl.ds(pl.multiple_of(start, nt), nt)]

    def gather_copy(idx_ref, s, r):
        return pltpu.make_async_copy(tile(x_hbm, idx_ref[0, r]), xbuf.at[s, pl.ds(r * nt, nt)], gsem.at[s])

    def scatter_copy(s, r):
        return pltpu.make_async_copy(obuf.at[s, pl.ds(r * nt, nt)], tile(parts_hbm, dstp_ref[0, r]),
                                     ssem.at[s])

    def gather_wait(s):
        pltpu.make_async_copy(x_hbm.at[pl.ds(0, bm * nt)], xbuf.at[s], gsem.at[s]).wait()

    def scatter_wait(s):
        pltpu.make_async_copy(obuf.at[s], parts_hbm.at[pl.ds(0, bm * nt)], ssem.at[s]).wait()

    def step(slot):
        gather_wait(slot)
        per = bm // DMA_PIECES

        def start_rows(piece):
            if piece >= DMA_PIECES:
                return
            for r in range(piece * per, (piece + 1) * per):
                gather_copy(tokn_ref, 1 - slot, r).start(priority=r % 2)
                scatter_copy(1 - slot, r).start(priority=(r + 1) % 2)

        xb[...] = _load_row_tiles(xbuf.at[slot]).astype(BF16)
        assert EXPERT_PIECES == 8
        half = D_FF // 2
        for c in range(2):
            cols = slice(c * half, (c + 1) * half)
            start_rows(2 * c)
            g = _dot(xb[...], wg_b[:, cols])
            start_rows(2 * c + 1)
            u = _dot(xb[...], wu_b[:, cols])
            hb[:, cols] = (_silu(g) * u).astype(BF16)
        quarter = D_MODEL // 4
        scatter_wait(slot)
        for c in range(4):
            start_rows(4 + c)
            o = _dot(hb[...], wd_b[:, c * quarter:(c + 1) * quarter])
            for k in range(quarter // LANES):
                kk = c * (quarter // LANES) + k
                obuf.at[slot][pl.ds(kk, bm, stride=nt), :] = o[:, k * LANES:(k + 1) * LANES]

    def drain(slot):
        gather_wait(slot)
        scatter_wait(slot)
        for r in range(bm):
            scatter_copy(1 - slot, r).start(priority=r % 2)
        scatter_wait(1 - slot)

    @pl.when(j == 0)
    def _():
        obuf[...] = jnp.zeros(obuf.shape, F32)
        n_real = parts_hbm.shape[0] - 2 * bm * nt
        pltpu.make_async_copy(obuf.at[0], parts_hbm.at[pl.ds(n_real, bm * nt)], ssem.at[0]).start()

        def start0(r, carry):
            pltpu.make_async_copy(tile(x_hbm, tok0_ref[0, r]), xbuf.at[0, pl.ds(pl.multiple_of(r * nt, nt), nt)],
                                  gsem.at[0]).start()
            return carry
        lax.fori_loop(0, bm, start0, 0)

    used = j < nu

    @pl.when(used & ((j == 0) | (be_ref[j] != be_ref[jnp.maximum(j - 1, 0)])))
    def _():
        wg_b[...] = wg_ref[...].astype(BF16)
        wu_b[...] = wu_ref[...].astype(BF16)
        wd_b[...] = wd_ref[...].astype(BF16)

    for slot in range(2):
        parity = lax.rem(j, 2) == slot

        @pl.when(used & parity)
        def _(slot=slot):
            step(slot)

        @pl.when((j == nu) & parity)
        def _(slot=slot):
            drain(slot)


def _expert_mlp(blk_expert, n_used, tok, dst, x, wg, wu, wd, bm, n_out):
    n_blocks = blk_expert.shape[0]
    assert bm % EXPERT_PIECES == 0
    w_map = lambda j, be, nu: (be[j], 0, 0)
    idx_spec = lambda f: pl.BlockSpec((None, 1, bm), lambda j, be, nu: (f(j), 0, 0),
                                      memory_space=pltpu.SMEM)
    any_spec = pl.BlockSpec(memory_space=pl.ANY)
    grid_spec = pltpu.PrefetchScalarGridSpec(
        num_scalar_prefetch=2,
        grid=(n_blocks + 1,),
        in_specs=[
            idx_spec(lambda j: 0),
            idx_spec(lambda j: jnp.minimum(j + 1, n_blocks - 1)),
            idx_spec(lambda j: j),
            any_spec,
            pl.BlockSpec((None, D_MODEL, D_FF), w_map),
            pl.BlockSpec((None, D_MODEL, D_FF), w_map),
            pl.BlockSpec((None, D_FF, D_MODEL), w_map),
        ],
        out_specs=any_spec,
        scratch_shapes=[
            pltpu.VMEM((2, bm * ROW_TILES, LANES), F32),
            pltpu.VMEM((2, bm * ROW_TILES, LANES), F32),
            pltpu.VMEM((bm, D_MODEL), BF16),
            pltpu.VMEM((bm, D_FF), BF16),
            pltpu.VMEM((D_MODEL, D_FF), BF16),
            pltpu.VMEM((D_MODEL, D_FF), BF16),
            pltpu.VMEM((D_FF, D_MODEL), BF16),
            pltpu.SemaphoreType.DMA((2,)),
            pltpu.SemaphoreType.DMA((2,)),
        ],
    )
    tok3 = (tok * ROW_TILES).reshape(n_blocks, 1, bm)
    spare = (n_out - bm + jnp.arange(bm, dtype=jnp.int32)).reshape(1, bm)
    dst_prev = (jnp.concatenate([spare, dst], axis=0) * ROW_TILES).reshape(n_blocks + 1, 1, bm)
    blk_expert = jnp.concatenate([blk_expert, blk_expert[-1:]])
    return pl.pallas_call(
        functools.partial(_expert_mlp_kernel, bm=bm),
        grid_spec=grid_spec,
        out_shape=jax.ShapeDtypeStruct((n_out * ROW_TILES, LANES), F32),
        compiler_params=pltpu.CompilerParams(
            dimension_semantics=("arbitrary",), vmem_limit_bytes=VMEM_LIMIT),
        name="expert_mlp",
    )(blk_expert, n_used, tok3, tok3, dst_prev, x, wg, wu, wd)


PAD_ID = 2 ** 16 - 1


def _moe_plan(e_idx, bm, n_blocks):
    n_tok = e_idx.shape[0]
    n_asg = 2 * n_tok
    n_pad = n_blocks * bm - n_asg
    e_flat = e_idx.reshape(-1)
    experts = jnp.arange(N_EXPERTS, dtype=jnp.int32)
    counts = jnp.sum((e_flat[:, None] == experts[None, :]).astype(jnp.int32), axis=0)
    nblk = (counts + bm - 1) // bm
    blk_end = jnp.cumsum(nblk)
    n_used = blk_end[-1]
    j = jnp.arange(n_blocks, dtype=jnp.int32)
    be = jnp.sum((blk_end[None, :] <= jnp.minimum(j, n_used - 1)[:, None]).astype(jnp.int32), axis=1)
    be = jnp.minimum(be, N_EXPERTS - 1)
    assert n_asg < PAD_ID
    pad_end = jnp.cumsum(nblk * bm - counts)
    pad_expert = jnp.sum((pad_end[None, :] <= jnp.arange(n_pad, dtype=jnp.int32)[:, None]).astype(jnp.int32),
                         axis=1)
    keys = jnp.concatenate([2 * e_flat, 2 * pad_expert + 1])
    ids = jnp.concatenate([jnp.arange(n_asg, dtype=jnp.int32), jnp.full((n_pad,), PAD_ID, jnp.int32)])
    window = lax.sort(keys * (PAD_ID + 1) + ids) % (PAD_ID + 1)
    window = window.reshape(n_blocks, bm)
    valid = window != PAD_ID
    q = jnp.arange(bm, dtype=jnp.int32)
    tok = jnp.where(valid, window // 2, (j[:, None] * bm + q[None, :]) % n_tok)
    dst = jnp.where(valid, (window % 2) * n_tok + window // 2,
                    2 * n_tok + (j[:, None] % 2) * bm + q[None, :])
    return tok.astype(jnp.int32), dst.astype(jnp.int32), be, n_used.astype(jnp.int32).reshape(1)


def _final_kernel(h_ref, p0_ref, p1_ref, route_ref, w_ref, op_ref, os_ref, *, n_prompt_steps):
    route = route_ref[...]
    p0 = _load_row_tiles(p0_ref)
    p1 = _load_row_tiles(p1_ref)
    y = _rms(h_ref[...] + route[:, 2:3] * p0 + route[:, 3:4] * p1, w_ref[...])
    i = pl.program_id(0)

    @pl.when(i < n_prompt_steps)
    def _():
        op_ref[...] = y

    @pl.when(i >= n_prompt_steps)
    def _():
        os_ref[...] = y


def _final(h, parts, route, w, tf, n_prompt):
    n_tok = h.shape[0]
    assert n_tok % tf == 0 and n_prompt % tf == 0
    n_p = n_prompt // tf
    row = lambda d: pl.BlockSpec((tf, d), lambda i: (i, 0))
    part = lambda k: pl.BlockSpec((tf * ROW_TILES, LANES), lambda i: (k * (n_tok // tf) + i, 0))
    return pl.pallas_call(
        functools.partial(_final_kernel, n_prompt_steps=n_p),
        grid=(n_tok // tf,),
        in_specs=[row(D_MODEL), part(0), part(1), row(LANES), _const_spec(w.shape)],
        out_specs=[pl.BlockSpec((tf, D_MODEL), lambda i: (jnp.minimum(i, n_p - 1), 0)),
                   pl.BlockSpec((tf, D_MODEL), lambda i: (jnp.maximum(i - n_p, 0), 0))],
        out_shape=[jax.ShapeDtypeStruct((n_prompt, D_MODEL), F32),
                   jax.ShapeDtypeStruct((n_tok - n_prompt, D_MODEL), F32)],
        compiler_params=pltpu.CompilerParams(
            dimension_semantics=("arbitrary",), vmem_limit_bytes=VMEM_LIMIT),
        name="final_combine",
    )(h, parts, parts, route, w)


PROMPT_TILE = 256
MOE_BLOCK_ROWS = 512
FINAL_TILE = 512
STATE_SEQS = 8


def _pad_lanes(a, width=LANES):
    return jnp.pad(a, [(0, 0)] * (a.ndim - 1) + [(0, width - a.shape[-1])])


def _layer_weights(l, norm_mix_w, w_in, conv_w, conv_b, dt_bias, a_log, d_skip, ssm_norm_w, sg_norm_w,
                   sg_w, sg_b, w_out, norm_ffn_w, w_router_group, b_router_group, w_router_expert,
                   b_router_expert, n_dec):
    c0, c1, c2 = D_SSM, D_SSM + C_CONV, D_SSM + C_CONV + SSM_HEADS
    wi = w_in[l]
    causal = jnp.tril(jnp.ones((CHUNK, CHUNK), bool))
    sgw_l = jnp.where(causal, sg_w[l], 0.0)
    head_of_lane = jnp.arange(D_SSM, dtype=jnp.int32) // SSM_HEAD_DIM
    expand = (jnp.arange(LANES, dtype=jnp.int32)[:, None] == head_of_lane[None, :]).astype(F32)
    row = lambda a: a.reshape(1, -1).astype(F32)
    gap = ROUTER_EXPERT_ROW - N_EXPERT_GROUPS
    rest = LANES - ROUTER_EXPERT_ROW - N_EXPERTS
    wrt = jnp.concatenate([w_router_group[l].T, jnp.zeros((gap, D_MODEL), F32), w_router_expert[l].T,
                           jnp.zeros((rest, D_MODEL), F32)], axis=0).astype(F32)
    brc = jnp.concatenate([b_router_group[l], jnp.zeros((gap,), F32), b_router_expert[l],
                           jnp.zeros((rest,), F32)]).astype(F32).reshape(LANES, 1)
    return {
        'nmw': row(norm_mix_w[l]),
        'wz': wi[:, :c0].astype(BF16),
        'wxbc': wi[:, c0:c1].astype(BF16),
        'wdt': _pad_lanes(wi[:, c1:c2]).astype(BF16),
        'wuv': wi[:, c2:].astype(BF16),
        'convw': conv_w[l].astype(F32),
        'convb': row(conv_b[l]),
        'dtb': _pad_lanes(row(dt_bias[l])),
        'alog': _pad_lanes(row(a_log[l])),
        'dskip': row(jnp.repeat(d_skip[l], SSM_HEAD_DIM)),
        'ssmnw': row(ssm_norm_w[l]),
        'sgnw': row(sg_norm_w[l]),
        'sgw': sgw_l.astype(BF16),
        'sgb': jnp.repeat(sg_b[l].T, SG_HEAD_DIM, axis=1).astype(F32),
        'w4': jnp.repeat(jnp.transpose(sgw_l[:, :n_dec, :n_dec], (1, 2, 0)), SG_HEAD_DIM, axis=2).astype(F32),
        'b4': jnp.repeat(sg_b[l][:, :n_dec].T, SG_HEAD_DIM, axis=1).astype(F32),
        'expand': expand,
        'wout': w_out[l].astype(BF16),
        'nfw': row(norm_ffn_w[l]),
        'wr': jnp.stack([wrt.astype(BF16), (wrt - wrt.astype(BF16).astype(F32)).astype(BF16)]),
        'br': brc,
    }


def _layer(l, xp, xs_slab, state_conv, state_ssm, w, w_gate, w_up, w_down):
    bp, seq, _ = xp.shape
    nb, nt = state_conv.shape[1], xs_slab.shape[0] // state_conv.shape[1]
    q = HEADS_PER_GROUP * SSM_HEAD_DIM

    cprev = jnp.transpose(state_conv[l], (1, 0, 2))
    z, ypart, ea, ysg, v, convn, cmat, bmat, xw, dec = _sample_front(xs_slab, cprev, w, nb, nt)
    to_seq = lambda a, d: jnp.pad(
        jnp.transpose(a.reshape(nt, nb, SSM_GROUPS, d), (1, 2, 0, 3)),
        ((0, 0), (0, 0), (0, SUBLANES - nt), (0, 0)))
    ssm_s, zoff = _sample_state(
        dec[:, :SSM_HEADS], state_ssm[l].reshape(nb, SSM_GROUPS, q, SSM_STATE),
        to_seq(cmat, SSM_STATE), to_seq(bmat, SSM_STATE), to_seq(xw, q), STATE_SEQS)
    zoff = jnp.transpose(zoff[:, :, :nt], (2, 0, 1, 3)).reshape(nt * nb, D_SSM)
    h_s, hn2_s, route_s = _sample_back(xs_slab, ypart, ea, zoff, z, ysg, w)

    h, hn2, route, convt_p, ssm_p = _prompt_mixer(xp, w, PROMPT_TILE, h_s, hn2_s, route_s)

    n_tok = h.shape[0]
    bm = MOE_BLOCK_ROWS
    n_blocks = -(-(2 * n_tok) // bm) + N_EXPERTS
    tok, dst, blk_expert, n_used = _moe_plan(route[:, :2].astype(jnp.int32), bm, n_blocks)
    parts = _expert_mlp(blk_expert, n_used, tok, dst, hn2, w_gate[l], w_up[l], w_down[l], bm,
                        2 * n_tok + 2 * bm)
    outs = dict(
        h=h, parts=parts, route=route,
        conv_p=convt_p[:, SUBLANES - (CONV_K - 1):],
        ssm_p=ssm_p.reshape(bp, SSM_GROUPS, HEADS_PER_GROUP, SSM_HEAD_DIM, SSM_STATE),
        conv_s=jnp.transpose(convn, (1, 0, 2)),
        ssm_s=ssm_s.reshape(nb, SSM_GROUPS, HEADS_PER_GROUP, SSM_HEAD_DIM, SSM_STATE),
        v_s=jnp.transpose(v.reshape(nt, nb, D_SG), (1, 0, 2)),
    )
    return outs


def kernel(x_prompt, x_sample, state_conv, state_ssm, norm_mix_w, w_in, conv_w, conv_b, dt_bias, a_log, d_skip, ssm_norm_w, sg_norm_w, sg_w, sg_b, w_out, norm_ffn_w, w_router_group, b_router_group, w_router_expert, b_router_expert, w_gate, w_up, w_down, norm_final_w):
    depth = w_in.shape[0]
    assert depth == 1, "the fused final norm assumes a single layer"
    bp, seq, _ = x_prompt.shape
    nb, nt, _ = x_sample.shape
    l = 0
    w = _layer_weights(l, norm_mix_w, w_in, conv_w, conv_b, dt_bias, a_log, d_skip, ssm_norm_w, sg_norm_w,
                       sg_w, sg_b, w_out, norm_ffn_w, w_router_group, b_router_group, w_router_expert,
                       b_router_expert, nt)
    xs_slab = jnp.transpose(x_sample, (1, 0, 2)).reshape(nt * nb, D_MODEL)
    outs = _layer(l, x_prompt, xs_slab, state_conv, state_ssm, w, w_gate, w_up, w_down)
    y_p, y_s = _final(outs['h'], outs['parts'], outs['route'], norm_final_w.reshape(1, -1).astype(F32),
                      FINAL_TILE, bp * seq)
    y_prompt = y_p.reshape(bp, seq, D_MODEL)
    y_sample = jnp.transpose(y_s.reshape(nt, nb, D_MODEL), (1, 0, 2))
    return (y_prompt, y_sample, outs['conv_p'][None], outs['ssm_p'][None], outs['conv_s'][None],
            outs['ssm_s'][None], outs['v_s'][None])
```

```python
import functools

import jax
import jax.numpy as jnp
from jax import lax
from jax.experimental import pallas as pl
from jax.experimental.pallas import tpu as pltpu

D_MODEL = 1024
D_SSM = 1024
SSM_HEAD_DIM = 64
SSM_HEADS = 16
SSM_GROUPS = 2
HEADS_PER_GROUP = 8
SSM_STATE = 128
CONV_K = 4
C_CONV = D_SSM + 2 * SSM_GROUPS * SSM_STATE
D_SG = 1024
SG_HEADS = 8
SG_HEAD_DIM = 128
CHUNK = 128
N_EXPERT_GROUPS = 4
EXPERTS_PER_GROUP = 8
N_EXPERTS = 32
D_FF = 512
EPS = 1e-6
LOG2_E = 1.4426950408889634

LANES = 128
SUBLANES = 8
VMEM_LIMIT = 56 * 1024 * 1024

F32 = jnp.float32
BF16 = jnp.bfloat16
HIGHEST = lax.Precision.HIGHEST


def _dot(a, b):
    return jnp.dot(a, b, preferred_element_type=F32)


def _dot_f32(a, b):
    return jnp.dot(a, b, preferred_element_type=F32, precision=HIGHEST)


def _rms(x, w):
    return x * lax.rsqrt(jnp.mean(x * x, axis=-1, keepdims=True) + EPS) * w


def _silu(x):
    return x * (0.5 + 0.5 * jnp.tanh(0.5 * x))


def _gelu(x):
    return 0.5 * x * (1.0 + lax.erf(x * 0.7071067811865476))


def _softplus(x):
    return jnp.maximum(x, 0.0) + jnp.log1p(jnp.exp(-jnp.abs(x)))


def _gated_group_norm(y, z, w):
    g = y * _silu(z)
    half = D_SSM // SSM_GROUPS
    parts = []
    for k in range(SSM_GROUPS):
        gk = g[:, k * half:(k + 1) * half]
        parts.append(gk * lax.rsqrt(jnp.mean(gk * gk, axis=-1, keepdims=True) + EPS))
    return jnp.concatenate(parts, axis=1) * w


ROUTER_GROUP_ROW = 0
ROUTER_EXPERT_ROW = SUBLANES


def _route_t(lt):
    r = lt.shape[1]
    row = lax.broadcasted_iota(jnp.int32, (SUBLANES, r), 0)
    rowf = row.astype(F32)
    big = float(SUBLANES)
    red = lambda f, a: f(a, axis=0, keepdims=True)
    gl = jnp.where(row < N_EXPERT_GROUPS, lt[ROUTER_GROUP_ROW:ROUTER_GROUP_ROW + SUBLANES], -jnp.inf)
    ge = jnp.exp(gl - red(jnp.max, gl))
    p_grp = ge / red(jnp.sum, ge)
    g_p = red(jnp.max, p_grp)
    g_idx = red(jnp.min, jnp.where(p_grp == g_p, rowf, big))
    el = lt[ROUTER_EXPERT_ROW:ROUTER_EXPERT_ROW + EXPERTS_PER_GROUP]
    for g in range(1, N_EXPERT_GROUPS):
        lo = ROUTER_EXPERT_ROW + g * EXPERTS_PER_GROUP
        el = jnp.where(g_idx == float(g), lt[lo:lo + EXPERTS_PER_GROUP], el)
    ee = jnp.exp(el - red(jnp.max, el))
    pe = ee / red(jnp.sum, ee)
    v1 = red(jnp.max, pe)
    i1 = red(jnp.min, jnp.where(pe == v1, rowf, big))
    rest = rowf != i1
    pe2 = jnp.where(rest, pe, -1.0)
    v2 = red(jnp.max, pe2)
    i2 = red(jnp.min, jnp.where(rest & (pe2 == v2), rowf, big))
    den = v1 + v2
    base = g_idx * float(EXPERTS_PER_GROUP)
    out = jnp.where(row == 0, base + i1, 0.0)
    out = jnp.where(row == 1, base + i2, out)
    out = jnp.where(row == 2, g_p * v1 / den, out)
    out = jnp.where(row == 3, g_p * v2 / den, out)
    return jnp.concatenate([out, jnp.zeros((LANES - SUBLANES, r), F32)], axis=0).T


def _ffn_front(h, nfw, wrt_ref, brc):
    hn2 = _rms(h, nfw)
    hi = hn2.astype(BF16)
    lo = (hn2 - hi.astype(F32)).astype(BF16)
    nt = lambda a, b: lax.dot_general(a, b, (((1,), (1,)), ((), ())), preferred_element_type=F32)
    lt = nt(wrt_ref[0], hi) + nt(wrt_ref[0], lo) + nt(wrt_ref[1], hi) + brc
    return hn2, _route_t(lt)


ROW_TILES = D_MODEL // LANES


def _store_row_tiles(ref, val):
    r = val.shape[0]
    for k in range(ROW_TILES):
        ref[pl.ds(k, r, stride=ROW_TILES), :] = val[:, k * LANES:(k + 1) * LANES]


def _load_row_tiles(ref):
    r = ref.shape[0] // ROW_TILES
    return jnp.concatenate([ref[pl.ds(k, r, stride=ROW_TILES), :] for k in range(ROW_TILES)], axis=1)


def _prompt_step(
        t, nt, fillers, x_ref, nmw_ref, wz_ref, wxbc_ref, wdt_ref, wuv_ref, convw_ref, convb_ref, dtb_ref,
        alog_ref, dskip_ref, ssmnw_ref, sgnw_ref, sgw_ref, sgb_ref, convt_ref, ssm_ref,
        xbc_scr, state_scr, y_scr, mix_scr, tl):
    fillers = list(fillers)
    assert len(fillers) == 3 + tl // CHUNK

    @pl.when(t == 0)
    def _():
        xbc_scr[0:SUBLANES, :] = jnp.zeros((SUBLANES, C_CONV), F32)
        state_scr[...] = jnp.zeros(state_scr.shape, F32)

    x = x_ref[...]
    hn = _rms(x, nmw_ref[...]).astype(BF16)
    z = _dot(hn, wz_ref[...])
    xbc = _dot(hn, wxbc_ref[...])
    dtr = _dot(hn, wdt_ref[...])
    uv = _gelu(_dot(hn, wuv_ref[...]))
    fillers.pop(0)()

    xbc_scr[SUBLANES:SUBLANES + tl, :] = xbc
    conv = convb_ref[...]
    for k in range(CONV_K):
        off = SUBLANES - (CONV_K - 1) + k
        conv = conv + convw_ref[k:k + 1, :] * xbc_scr[off:off + tl, :]
    tail = xbc_scr[tl:tl + SUBLANES, :]
    xbc_scr[0:SUBLANES, :] = tail
    convt_ref[...] = tail
    act = _silu(conv)
    xs = act[:, :D_SSM]
    bmat = act[:, D_SSM:D_SSM + SSM_GROUPS * SSM_STATE]
    cmat = act[:, D_SSM + SSM_GROUPS * SSM_STATE:]
    fillers.pop(0)()

    dt = _softplus(dtr + dtb_ref[...])
    a_row = -jnp.exp(alog_ref[...]) * LOG2_E

    li = lax.broadcasted_iota(jnp.int32, (CHUNK, CHUNK), 0)
    si = lax.broadcasted_iota(jnp.int32, (CHUNK, CHUNK), 1)
    causal = li >= si
    tri = jnp.where(causal, 1.0, 0.0).astype(F32)
    lo_half = si < SSM_HEAD_DIM

    for c in range(tl // CHUNK):
        rows = slice(c * CHUNK, (c + 1) * CHUNK)
        dt_c = dt[rows]
        acum = _dot_f32(tri, dt_c * a_row)
        acum_t = acum.T
        dt_t = dt_c.T
        w_t = dt_t * jnp.exp2(acum_t[:, CHUNK - 1:CHUNK] - acum_t)
        xs_c = xs[rows]
        for g in range(SSM_GROUPS):
            gl = slice(g * SSM_STATE, (g + 1) * SSM_STATE)
            b_g = bmat[rows, gl]
            c_g = cmat[rows, gl]
            b_gt = b_g.T
            cb = _dot(c_g.astype(BF16), b_gt.astype(BF16))
            for j in range(HEADS_PER_GROUP // 2):
                gw = HEADS_PER_GROUP * SSM_HEAD_DIM
                pl_ = slice(g * gw + j * LANES, g * gw + (j + 1) * LANES)
                m_l, ec_l, s_l, ea_last = [], [], [], []
                for k in range(2):
                    hh = g * HEADS_PER_GROUP + 2 * j + k
                    colb = jnp.broadcast_to(acum[:, hh:hh + 1], (CHUNK, CHUNK))
                    rowb = jnp.broadcast_to(acum_t[hh:hh + 1, :], (CHUNK, CHUNK))
                    ea = jnp.exp2(colb)
                    lmat = jnp.where(causal, jnp.exp2(colb - rowb), 0.0)
                    m_l.append(cb * lmat * jnp.broadcast_to(dt_t[hh:hh + 1, :], (CHUNK, CHUNK)))
                    ec_l.append(c_g * ea)
                    s_l.append(b_gt * jnp.broadcast_to(w_t[hh:hh + 1, :], (CHUNK, CHUNK)))
                    ea_last.append(ea[CHUNK - 1:CHUNK, :])
                xs_p = xs_c[:, pl_]
                st_p = state_scr[g, :, j * LANES:(j + 1) * LANES]
                rx = jnp.concatenate([jnp.where(lo_half, xs_p, 0.0),
                                      jnp.where(lo_half, 0.0, xs_p)], axis=0).astype(BF16)
                rs = jnp.concatenate([jnp.where(lo_half, st_p, 0.0),
                                      jnp.where(lo_half, 0.0, st_p)], axis=0).astype(BF16)
                lhs = jnp.concatenate(m_l + ec_l, axis=1).astype(BF16)
                y_p = _dot(lhs, jnp.concatenate([rx, rs], axis=0))
                y_scr[rows, pl_] = y_p
                dec = jnp.where(lo_half[0:1, :], ea_last[0], ea_last[1])
                upd = _dot(jnp.concatenate(s_l, axis=1).astype(BF16), rx)
                state_scr[g, :, j * LANES:(j + 1) * LANES] = st_p * dec + upd
        fillers.pop(0)()

    y = y_scr[...] + dskip_ref[...] * xs
    y_ssm = _gated_group_norm(y, z, ssmnw_ref[...])
    mix_scr[:, :D_SSM] = y_ssm.astype(BF16)

    u = uv[:, :D_SG]
    v = _rms(uv[:, D_SG:], sgnw_ref[...])
    vb = v.astype(BF16)
    for c in range(tl // CHUNK):
        rows = slice(c * CHUNK, (c + 1) * CHUNK)
        for hd in range(SG_HEADS):
            hl = slice(hd * SG_HEAD_DIM, (hd + 1) * SG_HEAD_DIM)
            sv = _dot(sgw_ref[hd], vb[rows, hl]) + sgb_ref[:, hl]
            mix_scr[rows, D_SSM + hd * SG_HEAD_DIM:D_SSM + (hd + 1) * SG_HEAD_DIM] = (
                u[rows, hl] * sv).astype(BF16)

    fillers.pop(0)()

    @pl.when(t == nt - 1)
    def _():
        for g in range(SSM_GROUPS):
            ssm_ref[g] = state_scr[g].T


def _prompt_mixer_kernel(
        x_ref, xp_ref, nmw_ref, wz_ref, wxbc_ref, wdt_ref, wuv_ref, convw_ref, convb_ref, dtb_ref,
        alog_ref, dskip_ref, ssmnw_ref, sgnw_ref, sgw_ref, sgb_ref, wout_ref, nfw_ref,
        wr_ref, br_ref, xs_ref, yps_ref, eas_ref, zos_ref, zs_ref, ysgs_ref,
        h_ref, hn2_ref, route_ref, convt_ref, ssm_ref,
        xbc_scr, state_scr, y_scr, mix_scr, h_scr, *, tl, nt, n_main):
    i = pl.program_id(0)

    @pl.when(i == 0)
    def _():
        mix_scr[...] = jnp.zeros(mix_scr.shape, BF16)

    def emit(h):
        h_ref[...] = h
        hn2, route = _ffn_front(h, nfw_ref[...], wr_ref, br_ref[...])
        _store_row_tiles(hn2_ref, hn2)
        route_ref[...] = route

    @pl.when(i > n_main)
    def _():
        y = yps_ref[...] + eas_ref[...] * zos_ref[...]
        y_ssm = _gated_group_norm(y, zs_ref[...], ssmnw_ref[...])
        mix_in = jnp.concatenate([y_ssm.astype(BF16), ysgs_ref[...].astype(BF16)], axis=1)
        emit(xs_ref[...] + _dot(mix_in, wout_ref[...]))

    n_out_chunks = 2 + tl // CHUNK
    width = D_MODEL // n_out_chunks
    assert width % LANES == 0 and width * n_out_chunks == D_MODEL

    def out_chunk(c):
        cols = slice(c * width, (c + 1) * width)
        h_scr[:, cols] = xp_ref[:, cols] + _dot(mix_scr[...], wout_ref[:, cols])

    def finish():
        emit(h_scr[...])

    def out_chunks(cs):
        for c in cs:
            out_chunk(c)

    @pl.when(i == n_main)
    def _():
        out_chunks(range(n_out_chunks))
        finish()

    @pl.when(i < n_main)
    def _():
        half = n_out_chunks // 2
        fillers = ([functools.partial(out_chunks, range(half)),
                    functools.partial(out_chunks, range(half, n_out_chunks)), finish]
                   + [lambda: None] * (tl // CHUNK))
        _prompt_step(
            lax.rem(i, nt), nt, fillers, x_ref, nmw_ref, wz_ref, wxbc_ref, wdt_ref, wuv_ref,
            convw_ref, convb_ref, dtb_ref, alog_ref, dskip_ref, ssmnw_ref, sgnw_ref, sgw_ref, sgb_ref,
            convt_ref, ssm_ref, xbc_scr, state_scr, y_scr, mix_scr, tl)


def _const_spec(shape, single_buffer=False):
    zeros = (0,) * len(shape)
    return pl.BlockSpec(shape, lambda *_: zeros, pipeline_mode=pl.Buffered(1) if single_buffer else None)


def _prompt_mixer(x, w, tl, sample_rows):
    bsz, seq, _ = x.shape
    nt = seq // tl
    n_main = bsz * nt
    n_s = sample_rows[0].shape[0]
    assert seq % tl == 0 and n_s % tl == 0
    n_tok = bsz * seq + n_s
    x2 = x.reshape(bsz * seq, D_MODEL)
    main = lambda i: jnp.minimum(i, n_main - 1)
    prev = lambda i: jnp.clip(i - 1, 0, n_main - 1)
    tail = lambda i: jnp.maximum(i - n_main - 1, 0)
    out_row = lambda i: jnp.maximum(i - 1, 0)
    weights = [w['nmw'], w['wz'], w['wxbc'], w['wdt'], w['wuv'], w['convw'], w['convb'], w['dtb'],
               w['alog'], w['dskip'], w['ssmnw'], w['sgnw'], w['sgw'], w['sgb'], w['wout'], w['nfw'],
               w['wr'], w['br']]
    in_specs = ([pl.BlockSpec((tl, D_MODEL), lambda i: (main(i), 0)),
                 pl.BlockSpec((tl, D_MODEL), lambda i: (prev(i), 0))]
                + [_const_spec(a.shape, single_buffer=True) for a in weights]
                + [pl.BlockSpec((tl, D_MODEL), lambda i: (tail(i), 0)) for _ in sample_rows])
    q = HEADS_PER_GROUP * SSM_HEAD_DIM
    out_shape = [
        jax.ShapeDtypeStruct((n_tok, D_MODEL), F32),
        jax.ShapeDtypeStruct((n_tok * ROW_TILES, LANES), F32),
        jax.ShapeDtypeStruct((n_tok, LANES), F32),
        jax.ShapeDtypeStruct((bsz, SUBLANES, C_CONV), F32),
        jax.ShapeDtypeStruct((bsz, SSM_GROUPS, q, SSM_STATE), F32),
    ]
    out_specs = [
        pl.BlockSpec((tl, D_MODEL), lambda i: (out_row(i), 0)),
        pl.BlockSpec((tl * ROW_TILES, LANES), lambda i: (out_row(i), 0)),
        pl.BlockSpec((tl, LANES), lambda i: (out_row(i), 0)),
        pl.BlockSpec((None, SUBLANES, C_CONV), lambda i: (main(i) // nt, 0, 0)),
        pl.BlockSpec((None, SSM_GROUPS, q, SSM_STATE), lambda i: (main(i) // nt, 0, 0, 0)),
    ]
    scratch = [
        pltpu.VMEM((tl + SUBLANES, C_CONV), F32),
        pltpu.VMEM((SSM_GROUPS, SSM_STATE, q), F32),
        pltpu.VMEM((tl, D_SSM), F32),
        pltpu.VMEM((tl, D_SSM + D_SG), BF16),
        pltpu.VMEM((tl, D_MODEL), F32),
    ]
    return pl.pallas_call(
        functools.partial(_prompt_mixer_kernel, tl=tl, nt=nt, n_main=n_main),
        grid=(n_main + 1 + n_s // tl,),
        in_specs=in_specs,
        out_specs=out_specs,
        out_shape=out_shape,
        scratch_shapes=scratch,
        compiler_params=pltpu.CompilerParams(
            dimension_semantics=("arbitrary",), vmem_limit_bytes=VMEM_LIMIT),
        name="prompt_mixer",
    )(x2, x2, *weights, *sample_rows)


def _sample_front_kernel(
        x_ref, cprev_ref, nmw_ref, wz_ref, wxbc_ref, wdt_ref, wuv_ref, convw_ref, convb_ref,
        dtb_ref, alog_ref, dskip_ref, sgnw_ref, w4_ref, b4_ref, expand_ref,
        z_ref, ypart_ref, ea_ref, ysg_ref, v_ref, convn_ref, c_ref, b_ref, xw_ref, dec_ref,
        *, nb, nt):
    x = x_ref[...]
    hn = _rms(x, nmw_ref[...]).astype(BF16)
    z_ref[...] = _dot(hn, wz_ref[...])
    xbc = _dot(hn, wxbc_ref[...])
    dtr = _dot(hn, wdt_ref[...])
    uv = _gelu(_dot(hn, wuv_ref[...]))

    slab = lambda a, t: a[t * nb:(t + 1) * nb]
    full = [cprev_ref[k] for k in range(CONV_K - 1)] + [slab(xbc, t) for t in range(nt)]
    for k in range(CONV_K - 1):
        convn_ref[k] = full[nt + k]
    xs, bm, cm = [], [], []
    for t in range(nt):
        conv = convb_ref[...]
        for k in range(CONV_K):
            conv = conv + convw_ref[k:k + 1, :] * full[t + k]
        act = _silu(conv)
        xs.append(act[:, :D_SSM])
        bm.append(act[:, D_SSM:D_SSM + SSM_GROUPS * SSM_STATE])
        cm.append(act[:, D_SSM + SSM_GROUPS * SSM_STATE:])
        b_ref[t * nb:(t + 1) * nb, :] = bm[t]
        c_ref[t * nb:(t + 1) * nb, :] = cm[t]

    dt = _softplus(dtr + dtb_ref[...])
    a_row = -jnp.exp(alog_ref[...])
    dts = [slab(dt, t) for t in range(nt)]
    cum = []
    for t in range(nt):
        da = dts[t] * a_row
        cum.append(da if t == 0 else cum[t - 1] + da)
    dec_ref[...] = jnp.exp(cum[nt - 1])

    lane = lax.broadcasted_iota(jnp.int32, (nb, LANES), 1)
    first_group = lane < HEADS_PER_GROUP
    facs = [jnp.exp(cum[t]) for t in range(nt)]
    facs += [dts[s] * jnp.exp(cum[nt - 1] - cum[s]) for s in range(nt)]
    pairs = []
    for t in range(nt):
        for s in range(t + 1):
            cb = []
            for g in range(SSM_GROUPS):
                gl = slice(g * SSM_STATE, (g + 1) * SSM_STATE)
                cb.append(jnp.sum(cm[t][:, gl] * bm[s][:, gl], axis=1, keepdims=True))
            cbh = jnp.where(first_group, cb[0], cb[1])
            facs.append(jnp.exp(cum[t] - cum[s]) * dts[s] * cbh)
            pairs.append((t, s))
    fx = _dot_f32(jnp.concatenate(facs, axis=0), expand_ref[...])
    fslab = lambda i: fx[i * nb:(i + 1) * nb]
    for t in range(nt):
        ea_ref[t * nb:(t + 1) * nb, :] = fslab(t)
        xw_ref[t * nb:(t + 1) * nb, :] = xs[t] * fslab(nt + t)
    for t in range(nt):
        acc = dskip_ref[...] * xs[t]
        for i, (tt, s) in enumerate(pairs):
            if tt == t:
                acc = acc + fslab(2 * nt + i) * xs[s]
        ypart_ref[t * nb:(t + 1) * nb, :] = acc

    u = uv[:, :D_SG]
    v = _rms(uv[:, D_SG:], sgnw_ref[...])
    v_ref[...] = v
    for t in range(nt):
        sv = b4_ref[t:t + 1, :]
        for s in range(t + 1):
            sv = sv + w4_ref[t, s:s + 1, :] * slab(v, s)
        ysg_ref[t * nb:(t + 1) * nb, :] = slab(u, t) * sv


def _sample_front(x_slab, cprev, w, nb, nt):
    r = nb * nt
    weights = [w['nmw'], w['wz'], w['wxbc'], w['wdt'], w['wuv'], w['convw'], w['convb'], w['dtb'],
               w['alog'], w['dskip'], w['sgnw'], w['w4'], w['b4'], w['expand']]
    ins = [x_slab, cprev] + weights
    out_shape = [
        jax.ShapeDtypeStruct((r, D_SSM), F32),
        jax.ShapeDtypeStruct((r, D_SSM), F32),
        jax.ShapeDtypeStruct((r, D_SSM), F32),
        jax.ShapeDtypeStruct((r, D_SG), F32),
        jax.ShapeDtypeStruct((r, D_SG), F32),
        jax.ShapeDtypeStruct((CONV_K - 1, nb, C_CONV), F32),
        jax.ShapeDtypeStruct((r, SSM_GROUPS * SSM_STATE), F32),
        jax.ShapeDtypeStruct((r, SSM_GROUPS * SSM_STATE), F32),
        jax.ShapeDtypeStruct((r, D_SSM), F32),
        jax.ShapeDtypeStruct((nb, LANES), F32),
    ]
    return pl.pallas_call(
        functools.partial(_sample_front_kernel, nb=nb, nt=nt),
        grid=(1,),
        in_specs=[_const_spec(a.shape) for a in ins],
        out_specs=[_const_spec(s.shape) for s in out_shape],
        out_shape=out_shape,
        compiler_params=pltpu.CompilerParams(
            dimension_semantics=("arbitrary",), vmem_limit_bytes=VMEM_LIMIT),
        name="sample_front",
    )(*ins)


def _sample_state_kernel(dec_ref, st_ref, cq_ref, bq_ref, xw_ref, so_ref, z_ref, *, seqs):
    for bb in range(seqs):
        for g in range(SSM_GROUPS):
            s0 = st_ref[bb, g]
            z_ref[bb, g] = lax.dot_general(
                cq_ref[bb, g].astype(BF16), s0.astype(BF16), (((1,), (1,)), ((), ())),
                preferred_element_type=F32)
            upd = lax.dot_general(
                xw_ref[bb, g].astype(BF16), bq_ref[bb, g].astype(BF16), (((0,), (0,)), ((), ())),
                preferred_element_type=F32)
            for hh in range(HEADS_PER_GROUP):
                hs = slice(hh * SSM_HEAD_DIM, (hh + 1) * SSM_HEAD_DIM)
                so_ref[bb, g, hs, :] = s0[hs] * dec_ref[bb, g * HEADS_PER_GROUP + hh] + upd[hs]


def _sample_state(dec, state, cq, bq, xwq, seqs):
    nb = state.shape[0]
    q = HEADS_PER_GROUP * SSM_HEAD_DIM
    blk = lambda *tail: pl.BlockSpec((seqs, SSM_GROUPS) + tail, lambda i: (i, 0, 0, 0))
    return pl.pallas_call(
        functools.partial(_sample_state_kernel, seqs=seqs),
        grid=(nb // seqs,),
        in_specs=[
            pl.BlockSpec((seqs, SSM_HEADS), lambda i: (i, 0), memory_space=pltpu.SMEM),
            blk(q, SSM_STATE), blk(SUBLANES, SSM_STATE), blk(SUBLANES, SSM_STATE), blk(SUBLANES, q),
        ],
        out_specs=[blk(q, SSM_STATE), blk(SUBLANES, q)],
        out_shape=[
            jax.ShapeDtypeStruct((nb, SSM_GROUPS, q, SSM_STATE), F32),
            jax.ShapeDtypeStruct((nb, SSM_GROUPS, SUBLANES, q), F32),
        ],
        compiler_params=pltpu.CompilerParams(
            dimension_semantics=("arbitrary",), vmem_limit_bytes=VMEM_LIMIT),
        name="sample_state",
    )(dec, state, cq, bq, xwq)


EXPERT_PIECES = 8
DMA_PIECES = 4


def _expert_mlp_kernel(be_ref, nu_ref, tok0_ref, tokn_ref, dstp_ref, x_hbm, wg_ref, wu_ref, wd_ref,
                       parts_hbm, xbuf, obuf, xb, hb, wg_b, wu_b, wd_b, gsem, ssem, *, bm):
    j = pl.program_id(0)
    nu = nu_ref[0]
    nt = ROW_TILES
    tile = lambda ref, start: ref.at[pl.ds(pl.multiple_of(start, nt), nt)]

    def gather_copy(idx_ref, s, r):
        return pltpu.make_async_copy(tile(x_hbm, idx_ref[0, r]), xbuf.at[s, pl.ds(r * nt, nt)], gsem.at[s])

    def scatter_copy(s, r):
        return pltpu.make_async_copy(obuf.at[s, pl.ds(r * nt, nt)], tile(parts_hbm, dstp_ref[0, r]),
                                     ssem.at[s])

    def gather_wait(s):
        pltpu.make_async_copy(x_hbm.at[pl.ds(0, bm * nt)], xbuf.at[s], gsem.at[s]).wait()

    def scatter_wait(s):
        pltpu.make_async_copy(obuf.at[s], parts_hbm.at[pl.ds(0, bm * nt)], ssem.at[s]).wait()

    def step(slot):
        gather_wait(slot)
        per = bm // DMA_PIECES

        def start_rows(piece):
            if piece >= DMA_PIECES:
                return
            for r in range(piece * per, (piece + 1) * per):
                gather_copy(tokn_ref, 1 - slot, r).start(priority=r % 2)
                scatter_copy(1 - slot, r).start(priority=(r + 1) % 2)

        xb[...] = _load_row_tiles(xbuf.at[slot]).astype(BF16)
        assert EXPERT_PIECES == 8
        half = D_FF // 2
        for c in range(2):
            cols = slice(c * half, (c + 1) * half)
            start_rows(2 * c)
            g = _dot(xb[...], wg_b[:, cols])
            start_rows(2 * c + 1)
            u = _dot(xb[...], wu_b[:, cols])
            hb[:, cols] = (_silu(g) * u).astype(BF16)
        quarter = D_MODEL // 4
        scatter_wait(slot)
        for c in range(4):
            start_rows(4 + c)
            o = _dot(hb[...], wd_b[:, c * quarter:(c + 1) * quarter])
            for k in range(quarter // LANES):
                kk = c * (quarter // LANES) + k
                obuf.at[slot][pl.ds(kk, bm, stride=nt), :] = o[:, k * LANES:(k + 1) * LANES]

    def drain(slot):
        gather_wait(slot)
        scatter_wait(slot)
        for r in range(bm):
            scatter_copy(1 - slot, r).start(priority=r % 2)
        scatter_wait(1 - slot)

    @pl.when(j == 0)
    def _():
        obuf[...] = jnp.zeros(obuf.shape, F32)
        n_real = parts_hbm.shape[0] - 2 * bm * nt
        pltpu.make_async_copy(obuf.at[0], parts_hbm.at[pl.ds(n_real, bm * nt)], ssem.at[0]).start()

        def start0(r, carry):
            pltpu.make_async_copy(tile(x_hbm, tok0_ref[0, r]), xbuf.at[0, pl.ds(pl.multiple_of(r * nt, nt), nt)],
                                  gsem.at[0]).start()
            return carry
        lax.fori_loop(0, bm, start0, 0)

    used = j < nu

    @pl.when(used & ((j == 0) | (be_ref[j] != be_ref[jnp.maximum(j - 1, 0)])))
    def _():
        wg_b[...] = wg_ref[...].astype(BF16)
        wu_b[...] = wu_ref[...].astype(BF16)
        wd_b[...] = wd_ref[...].astype(BF16)

    for slot in range(2):
        parity = lax.rem(j, 2) == slot

        @pl.when(used & parity)
        def _(slot=slot):
            step(slot)

        @pl.when((j == nu) & parity)
        def _(slot=slot):
            drain(slot)


def _expert_mlp(blk_expert, n_used, tok, dst, x, wg, wu, wd, bm, n_out):
    n_blocks = blk_expert.shape[0]
    assert bm % EXPERT_PIECES == 0
    w_map = lambda j, be, nu: (be[j], 0, 0)
    idx_spec = lambda f: pl.BlockSpec((None, 1, bm), lambda j, be, nu: (f(j), 0, 0),
                                      memory_space=pltpu.SMEM)
    any_spec = pl.BlockSpec(memory_space=pl.ANY)
    grid_spec = pltpu.PrefetchScalarGridSpec(
        num_scalar_prefetch=2,
        grid=(n_blocks + 1,),
        in_specs=[
            idx_spec(lambda j: 0),
            idx_spec(lambda j: jnp.minimum(j + 1, n_blocks - 1)),
            idx_spec(lambda j: j),
            any_spec,
            pl.BlockSpec((None, D_MODEL, D_FF), w_map),
            pl.BlockSpec((None, D_MODEL, D_FF), w_map),
            pl.BlockSpec((None, D_FF, D_MODEL), w_map),
        ],
        out_specs=any_spec,
        scratch_shapes=[
            pltpu.VMEM((2, bm * ROW_TILES, LANES), F32),
            pltpu.VMEM((2, bm * ROW_TILES, LANES), F32),
            pltpu.VMEM((bm, D_MODEL), BF16),
            pltpu.VMEM((bm, D_FF), BF16),
            pltpu.VMEM((D_MODEL, D_FF), BF16),
            pltpu.VMEM((D_MODEL, D_FF), BF16),
            pltpu.VMEM((D_FF, D_MODEL), BF16),
            pltpu.SemaphoreType.DMA((2,)),
            pltpu.SemaphoreType.DMA((2,)),
        ],
    )
    tok3 = (tok * ROW_TILES).reshape(n_blocks, 1, bm)
    spare = (n_out - bm + jnp.arange(bm, dtype=jnp.int32)).reshape(1, bm)
    dst_prev = (jnp.concatenate([spare, dst], axis=0) * ROW_TILES).reshape(n_blocks + 1, 1, bm)
    blk_expert = jnp.concatenate([blk_expert, blk_expert[-1:]])
    return pl.pallas_call(
        functools.partial(_expert_mlp_kernel, bm=bm),
        grid_spec=grid_spec,
        out_shape=jax.ShapeDtypeStruct((n_out * ROW_TILES, LANES), F32),
        compiler_params=pltpu.CompilerParams(
            dimension_semantics=("arbitrary",), vmem_limit_bytes=VMEM_LIMIT),
        name="expert_mlp",
    )(blk_expert, n_used, tok3, tok3, dst_prev, x, wg, wu, wd)


PAD_ID = 2 ** 16 - 1


def _moe_plan(e_idx, bm, n_blocks):
    n_tok = e_idx.shape[0]
    n_asg = 2 * n_tok
    n_pad = n_blocks * bm - n_asg
    e_flat = e_idx.reshape(-1)
    experts = jnp.arange(N_EXPERTS, dtype=jnp.int32)
    counts = jnp.sum((e_flat[:, None] == experts[None, :]).astype(jnp.int32), axis=0)
    nblk = (counts + bm - 1) // bm
    blk_end = jnp.cumsum(nblk)
    n_used = blk_end[-1]
    j = jnp.arange(n_blocks, dtype=jnp.int32)
    be = jnp.sum((blk_end[None, :] <= jnp.minimum(j, n_used - 1)[:, None]).astype(jnp.int32), axis=1)
    be = jnp.minimum(be, N_EXPERTS - 1)
    assert n_asg < PAD_ID
    pad_end = jnp.cumsum(nblk * bm - counts)
    pad_expert = jnp.sum((pad_end[None, :] <= jnp.arange(n_pad, dtype=jnp.int32)[:, None]).astype(jnp.int32),
                         axis=1)
    keys = jnp.concatenate([2 * e_flat, 2 * pad_expert + 1])
    ids = jnp.concatenate([jnp.arange(n_asg, dtype=jnp.int32), jnp.full((n_pad,), PAD_ID, jnp.int32)])
    window = lax.sort(keys * (PAD_ID + 1) + ids) % (PAD_ID + 1)
    window = window.reshape(n_blocks, bm)
    valid = window != PAD_ID
    q = jnp.arange(bm, dtype=jnp.int32)
    tok = jnp.where(valid, window // 2, (j[:, None] * bm + q[None, :]) % n_tok)
    dst = jnp.where(valid, (window % 2) * n_tok + window // 2,
                    2 * n_tok + (j[:, None] % 2) * bm + q[None, :])
    return tok.astype(jnp.int32), dst.astype(jnp.int32), be, n_used.astype(jnp.int32).reshape(1)


def _final_kernel(h_ref, p0_ref, p1_ref, route_ref, w_ref, op_ref, os_ref, *, n_prompt_steps):
    route = route_ref[...]
    p0 = _load_row_tiles(p0_ref)
    p1 = _load_row_tiles(p1_ref)
    y = _rms(h_ref[...] + route[:, 2:3] * p0 + route[:, 3:4] * p1, w_ref[...])
    i = pl.program_id(0)

    @pl.when(i < n_prompt_steps)
    def _():
        op_ref[...] = y

    @pl.when(i >= n_prompt_steps)
    def _():
        os_ref[...] = y


def _final(h, parts, route, w, tf, n_prompt):
    n_tok = h.shape[0]
    assert n_tok % tf == 0 and n_prompt % tf == 0
    n_p = n_prompt // tf
    row = lambda d: pl.BlockSpec((tf, d), lambda i: (i, 0))
    part = lambda k: pl.BlockSpec((tf * ROW_TILES, LANES), lambda i: (k * (n_tok // tf) + i, 0))
    return pl.pallas_call(
        functools.partial(_final_kernel, n_prompt_steps=n_p),
        grid=(n_tok // tf,),
        in_specs=[row(D_MODEL), part(0), part(1), row(LANES), _const_spec(w.shape)],
        out_specs=[pl.BlockSpec((tf, D_MODEL), lambda i: (jnp.minimum(i, n_p - 1), 0)),
                   pl.BlockSpec((tf, D_MODEL), lambda i: (jnp.maximum(i - n_p, 0), 0))],
        out_shape=[jax.ShapeDtypeStruct((n_prompt, D_MODEL), F32),
                   jax.ShapeDtypeStruct((n_tok - n_prompt, D_MODEL), F32)],
        compiler_params=pltpu.CompilerParams(
            dimension_semantics=("arbitrary",), vmem_limit_bytes=VMEM_LIMIT),
        name="final_combine",
    )(h, parts, parts, route, w)


PROMPT_TILE = 256
MOE_BLOCK_ROWS = 512
FINAL_TILE = 512
STATE_SEQS = 8


def _pad_lanes(a, width=LANES):
    return jnp.pad(a, [(0, 0)] * (a.ndim - 1) + [(0, width - a.shape[-1])])


def _layer_weights(l, norm_mix_w, w_in, conv_w, conv_b, dt_bias, a_log, d_skip, ssm_norm_w, sg_norm_w,
                   sg_w, sg_b, w_out, norm_ffn_w, w_router_group, b_router_group, w_router_expert,
                   b_router_expert, n_dec):
    c0, c1, c2 = D_SSM, D_SSM + C_CONV, D_SSM + C_CONV + SSM_HEADS
    wi = w_in[l]
    causal = jnp.tril(jnp.ones((CHUNK, CHUNK), bool))
    sgw_l = jnp.where(causal, sg_w[l], 0.0)
    head_of_lane = jnp.arange(D_SSM, dtype=jnp.int32) // SSM_HEAD_DIM
    expand = (jnp.arange(LANES, dtype=jnp.int32)[:, None] == head_of_lane[None, :]).astype(F32)
    row = lambda a: a.reshape(1, -1).astype(F32)
    gap = ROUTER_EXPERT_ROW - N_EXPERT_GROUPS
    rest = LANES - ROUTER_EXPERT_ROW - N_EXPERTS
    wrt = jnp.concatenate([w_router_group[l].T, jnp.zeros((gap, D_MODEL), F32), w_router_expert[l].T,
                           jnp.zeros((rest, D_MODEL), F32)], axis=0).astype(F32)
    brc = jnp.concatenate([b_router_group[l], jnp.zeros((gap,), F32), b_router_expert[l],
                           jnp.zeros((rest,), F32)]).astype(F32).reshape(LANES, 1)
    return {
        'nmw': row(norm_mix_w[l]),
        'wz': wi[:, :c0].astype(BF16),
        'wxbc': wi[:, c0:c1].astype(BF16),
        'wdt': _pad_lanes(wi[:, c1:c2]).astype(BF16),
        'wuv': wi[:, c2:].astype(BF16),
        'convw': conv_w[l].astype(F32),
        'convb': row(conv_b[l]),
        'dtb': _pad_lanes(row(dt_bias[l])),
        'alog': _pad_lanes(row(a_log[l])),
        'dskip': row(jnp.repeat(d_skip[l], SSM_HEAD_DIM)),
        'ssmnw': row(ssm_norm_w[l]),
        'sgnw': row(sg_norm_w[l]),
        'sgw': sgw_l.astype(BF16),
        'sgb': jnp.repeat(sg_b[l].T, SG_HEAD_DIM, axis=1).astype(F32),
        'w4': jnp.repeat(jnp.transpose(sgw_l[:, :n_dec, :n_dec], (1, 2, 0)), SG_HEAD_DIM, axis=2).astype(F32),
        'b4': jnp.repeat(sg_b[l][:, :n_dec].T, SG_HEAD_DIM, axis=1).astype(F32),
        'expand': expand,
        'wout': w_out[l].astype(BF16),
        'nfw': row(norm_ffn_w[l]),
        'wr': jnp.stack([wrt.astype(BF16), (wrt - wrt.astype(BF16).astype(F32)).astype(BF16)]),
        'br': brc,
    }


def _layer(l, xp, xs_slab, state_conv, state_ssm, w, w_gate, w_up, w_down):
    bp, seq, _ = xp.shape
    nb, nt = state_conv.shape[1], xs_slab.shape[0] // state_conv.shape[1]
    q = HEADS_PER_GROUP * SSM_HEAD_DIM

    cprev = jnp.transpose(state_conv[l], (1, 0, 2))
    z, ypart, ea, ysg, v, convn, cmat, bmat, xw, dec = _sample_front(xs_slab, cprev, w, nb, nt)
    to_seq = lambda a, d: jnp.pad(
        jnp.transpose(a.reshape(nt, nb, SSM_GROUPS, d), (1, 2, 0, 3)),
        ((0, 0), (0, 0), (0, SUBLANES - nt), (0, 0)))
    ssm_s, zoff = _sample_state(
        dec[:, :SSM_HEADS], state_ssm[l].reshape(nb, SSM_GROUPS, q, SSM_STATE),
        to_seq(cmat, SSM_STATE), to_seq(bmat, SSM_STATE), to_seq(xw, q), STATE_SEQS)
    zoff = jnp.transpose(zoff[:, :, :nt], (2, 0, 1, 3)).reshape(nt * nb, D_SSM)

    h, hn2, route, convt_p, ssm_p = _prompt_mixer(xp, w, PROMPT_TILE, (xs_slab, ypart, ea, zoff, z, ysg))

    n_tok = h.shape[0]
    bm = MOE_BLOCK_ROWS
    n_blocks = -(-(2 * n_tok) // bm) + N_EXPERTS
    tok, dst, blk_expert, n_used = _moe_plan(route[:, :2].astype(jnp.int32), bm, n_blocks)
    parts = _expert_mlp(blk_expert, n_used, tok, dst, hn2, w_gate[l], w_up[l], w_down[l], bm,
                        2 * n_tok + 2 * bm)
    outs = dict(
        h=h, parts=parts, route=route,
        conv_p=convt_p[:, SUBLANES - (CONV_K - 1):],
        ssm_p=ssm_p.reshape(bp, SSM_GROUPS, HEADS_PER_GROUP, SSM_HEAD_DIM, SSM_STATE),
        conv_s=jnp.transpose(convn, (1, 0, 2)),
        ssm_s=ssm_s.reshape(nb, SSM_GROUPS, HEADS_PER_GROUP, SSM_HEAD_DIM, SSM_STATE),
        v_s=jnp.transpose(v.reshape(nt, nb, D_SG), (1, 0, 2)),
    )
    return outs


def kernel(x_prompt, x_sample, state_conv, state_ssm, norm_mix_w, w_in, conv_w, conv_b, dt_bias, a_log, d_skip, ssm_norm_w, sg_norm_w, sg_w, sg_b, w_out, norm_ffn_w, w_router_group, b_router_group, w_router_expert, b_router_expert, w_gate, w_up, w_down, norm_final_w):
    depth = w_in.shape[0]
    assert depth == 1, "the fused final norm assumes a single layer"
    bp, seq, _ = x_prompt.shape
    nb, nt, _ = x_sample.shape
    l = 0
    w = _layer_weights(l, norm_mix_w, w_in, conv_w, conv_b, dt_bias, a_log, d_skip, ssm_norm_w, sg_norm_w,
                       sg_w, sg_b, w_out, norm_ffn_w, w_router_group, b_router_group, w_router_expert,
                       b_router_expert, nt)
    xs_slab = jnp.transpose(x_sample, (1, 0, 2)).reshape(nt * nb, D_MODEL)
    outs = _layer(l, x_prompt, xs_slab, state_conv, state_ssm, w, w_gate, w_up, w_down)
    y_p, y_s = _final(outs['h'], outs['parts'], outs['route'], norm_final_w.reshape(1, -1).astype(F32),
                      FINAL_TILE, bp * seq)
    y_prompt = y_p.reshape(bp, seq, D_MODEL)
    y_sample = jnp.transpose(y_s.reshape(nt, nb, D_MODEL), (1, 0, 2))
    return (y_prompt, y_sample, outs['conv_p'][None], outs['ssm_p'][None], outs['conv_s'][None],
            outs['ssm_s'][None], outs['v_s'][None])
```

```python
import functools

import jax
import jax.numpy as jnp
from jax import lax
from jax.experimental import pallas as pl
from jax.experimental.pallas import tpu as pltpu

D_MODEL = 1024
D_SSM = 1024
SSM_HEAD_DIM = 64
SSM_HEADS = 16
SSM_GROUPS = 2
HEADS_PER_GROUP = 8
SSM_STATE = 128
CONV_K = 4
C_CONV = D_SSM + 2 * SSM_GROUPS * SSM_STATE
D_SG = 1024
SG_HEADS = 8
SG_HEAD_DIM = 128
CHUNK = 128
N_EXPERT_GROUPS = 4
EXPERTS_PER_GROUP = 8
N_EXPERTS = 32
D_FF = 512
EPS = 1e-6
LOG2_E = 1.4426950408889634

LANES = 128
SUBLANES = 8
VMEM_LIMIT = 56 * 1024 * 1024

F32 = jnp.float32
BF16 = jnp.bfloat16
HIGHEST = lax.Precision.HIGHEST


def _dot(a, b):
    return jnp.dot(a, b, preferred_element_type=F32)


def _dot_f32(a, b):
    return jnp.dot(a, b, preferred_element_type=F32, precision=HIGHEST)


def _rms(x, w):
    return x * lax.rsqrt(jnp.mean(x * x, axis=-1, keepdims=True) + EPS) * w


def _silu(x):
    return x * (0.5 + 0.5 * jnp.tanh(0.5 * x))


def _gelu(x):
    return 0.5 * x * (1.0 + lax.erf(x * 0.7071067811865476))


def _softplus(x):
    return jnp.maximum(x, 0.0) + jnp.log1p(jnp.exp(-jnp.abs(x)))


def _gated_group_norm(y, z, w):
    g = y * _silu(z)
    half = D_SSM // SSM_GROUPS
    parts = []
    for k in range(SSM_GROUPS):
        gk = g[:, k * half:(k + 1) * half]
        parts.append(gk * lax.rsqrt(jnp.mean(gk * gk, axis=-1, keepdims=True) + EPS))
    return jnp.concatenate(parts, axis=1) * w


ROUTER_GROUP_ROW = 0
ROUTER_EXPERT_ROW = SUBLANES


def _route_t(lt):
    r = lt.shape[1]
    row = lax.broadcasted_iota(jnp.int32, (SUBLANES, r), 0)
    rowf = row.astype(F32)
    big = float(SUBLANES)
    red = lambda f, a: f(a, axis=0, keepdims=True)
    gl = jnp.where(row < N_EXPERT_GROUPS, lt[ROUTER_GROUP_ROW:ROUTER_GROUP_ROW + SUBLANES], -jnp.inf)
    ge = jnp.exp(gl - red(jnp.max, gl))
    p_grp = ge / red(jnp.sum, ge)
    g_p = red(jnp.max, p_grp)
    g_idx = red(jnp.min, jnp.where(p_grp == g_p, rowf, big))
    el = lt[ROUTER_EXPERT_ROW:ROUTER_EXPERT_ROW + EXPERTS_PER_GROUP]
    for g in range(1, N_EXPERT_GROUPS):
        lo = ROUTER_EXPERT_ROW + g * EXPERTS_PER_GROUP
        el = jnp.where(g_idx == float(g), lt[lo:lo + EXPERTS_PER_GROUP], el)
    ee = jnp.exp(el - red(jnp.max, el))
    pe = ee / red(jnp.sum, ee)
    v1 = red(jnp.max, pe)
    i1 = red(jnp.min, jnp.where(pe == v1, rowf, big))
    rest = rowf != i1
    pe2 = jnp.where(rest, pe, -1.0)
    v2 = red(jnp.max, pe2)
    i2 = red(jnp.min, jnp.where(rest & (pe2 == v2), rowf, big))
    den = v1 + v2
    base = g_idx * float(EXPERTS_PER_GROUP)
    out = jnp.where(row == 0, base + i1, 0.0)
    out = jnp.where(row == 1, base + i2, out)
    out = jnp.where(row == 2, g_p * v1 / den, out)
    out = jnp.where(row == 3, g_p * v2 / den, out)
    return jnp.concatenate([out, jnp.zeros((LANES - SUBLANES, r), F32)], axis=0).T


def _ffn_front(h, nfw, wrt_ref, brc):
    hn2 = _rms(h, nfw)
    hi = hn2.astype(BF16)
    lo = (hn2 - hi.astype(F32)).astype(BF16)
    nt = lambda a, b: lax.dot_general(a, b, (((1,), (1,)), ((), ())), preferred_element_type=F32)
    lt = nt(wrt_ref[0], hi) + nt(wrt_ref[0], lo) + nt(wrt_ref[1], hi) + brc
    return hn2, _route_t(lt)


ROW_TILES = D_MODEL // LANES


def _store_row_tiles(ref, val):
    r = val.shape[0]
    for k in range(ROW_TILES):
        ref[pl.ds(k, r, stride=ROW_TILES), :] = val[:, k * LANES:(k + 1) * LANES]


def _load_row_tiles(ref):
    r = ref.shape[0] // ROW_TILES
    return jnp.concatenate([ref[pl.ds(k, r, stride=ROW_TILES), :] for k in range(ROW_TILES)], axis=1)


def _prompt_step(
        t, nt, fillers, x_ref, nmw_ref, wz_ref, wxbc_ref, wdt_ref, wuv_ref, convw_ref, convb_ref, dtb_ref,
        alog_ref, dskip_ref, ssmnw_ref, sgnw_ref, sgw_ref, sgb_ref, convt_ref, ssm_ref,
        xbc_scr, state_scr, y_scr, mix_scr, tl):
    fillers = list(fillers)
    assert len(fillers) == 3 + tl // CHUNK

    @pl.when(t == 0)
    def _():
        xbc_scr[0:SUBLANES, :] = jnp.zeros((SUBLANES, C_CONV), F32)
        state_scr[...] = jnp.zeros(state_scr.shape, F32)

    x = x_ref[...]
    hn = _rms(x, nmw_ref[...]).astype(BF16)
    z = _dot(hn, wz_ref[...])
    xbc = _dot(hn, wxbc_ref[...])
    dtr = _dot(hn, wdt_ref[...])
    uv = _gelu(_dot(hn, wuv_ref[...]))
    fillers.pop(0)()

    xbc_scr[SUBLANES:SUBLANES + tl, :] = xbc
    conv = convb_ref[...]
    for k in range(CONV_K):
        off = SUBLANES - (CONV_K - 1) + k
        conv = conv + convw_ref[k:k + 1, :] * xbc_scr[off:off + tl, :]
    tail = xbc_scr[tl:tl + SUBLANES, :]
    xbc_scr[0:SUBLANES, :] = tail
    convt_ref[...] = tail
    act = _silu(conv)
    xs = act[:, :D_SSM]
    bmat = act[:, D_SSM:D_SSM + SSM_GROUPS * SSM_STATE]
    cmat = act[:, D_SSM + SSM_GROUPS * SSM_STATE:]
    fillers.pop(0)()

    dt = _softplus(dtr + dtb_ref[...])
    a_row = -jnp.exp(alog_ref[...]) * LOG2_E

    li = lax.broadcasted_iota(jnp.int32, (CHUNK, CHUNK), 0)
    si = lax.broadcasted_iota(jnp.int32, (CHUNK, CHUNK), 1)
    causal = li >= si
    tri = jnp.where(causal, 1.0, 0.0).astype(F32)
    lo_half = si < SSM_HEAD_DIM

    for c in range(tl // CHUNK):
        rows = slice(c * CHUNK, (c + 1) * CHUNK)
        dt_c = dt[rows]
        acum = _dot_f32(tri, dt_c * a_row)
        acum_t = acum.T
        dt_t = dt_c.T
        w_t = dt_t * jnp.exp2(acum_t[:, CHUNK - 1:CHUNK] - acum_t)
        xs_c = xs[rows]
        for g in range(SSM_GROUPS):
            gl = slice(g * SSM_STATE, (g + 1) * SSM_STATE)
            b_g = bmat[rows, gl]
            c_g = cmat[rows, gl]
            b_gt = b_g.T
            cb = _dot(c_g.astype(BF16), b_gt.astype(BF16))
            for j in range(HEADS_PER_GROUP // 2):
                gw = HEADS_PER_GROUP * SSM_HEAD_DIM
                pl_ = slice(g * gw + j * LANES, g * gw + (j + 1) * LANES)
                m_l, ec_l, s_l, ea_last = [], [], [], []
                for k in range(2):
                    hh = g * HEADS_PER_GROUP + 2 * j + k
                    colb = jnp.broadcast_to(acum[:, hh:hh + 1], (CHUNK, CHUNK))
                    rowb = jnp.broadcast_to(acum_t[hh:hh + 1, :], (CHUNK, CHUNK))
                    ea = jnp.exp2(colb)
                    lmat = jnp.where(causal, jnp.exp2(colb - rowb), 0.0)
                    m_l.append(cb * lmat * jnp.broadcast_to(dt_t[hh:hh + 1, :], (CHUNK, CHUNK)))
                    ec_l.append(c_g * ea)
                    s_l.append(b_gt * jnp.broadcast_to(w_t[hh:hh + 1, :], (CHUNK, CHUNK)))
                    ea_last.append(ea[CHUNK - 1:CHUNK, :])
                xs_p = xs_c[:, pl_]
                st_p = state_scr[g, :, j * LANES:(j + 1) * LANES]
                rx = jnp.concatenate([jnp.where(lo_half, xs_p, 0.0),
                                      jnp.where(lo_half, 0.0, xs_p)], axis=0).astype(BF16)
                rs = jnp.concatenate([jnp.where(lo_half, st_p, 0.0),
                                      jnp.where(lo_half, 0.0, st_p)], axis=0).astype(BF16)
                lhs = jnp.concatenate(m_l + ec_l, axis=1).astype(BF16)
                y_p = _dot(lhs, jnp.concatenate([rx, rs], axis=0))
                y_scr[rows, pl_] = y_p
                dec = jnp.where(lo_half[0:1, :], ea_last[0], ea_last[1])
                upd = _dot(jnp.concatenate(s_l, axis=1).astype(BF16), rx)
                state_scr[g, :, j * LANES:(j + 1) * LANES] = st_p * dec + upd
        fillers.pop(0)()

    y = y_scr[...] + dskip_ref[...] * xs
    y_ssm = _gated_group_norm(y, z, ssmnw_ref[...])
    mix_scr[:, :D_SSM] = y_ssm.astype(BF16)

    u = uv[:, :D_SG]
    v = _rms(uv[:, D_SG:], sgnw_ref[...])
    vb = v.astype(BF16)
    for c in range(tl // CHUNK):
        rows = slice(c * CHUNK, (c + 1) * CHUNK)
        for hd in range(SG_HEADS):
            hl = slice(hd * SG_HEAD_DIM, (hd + 1) * SG_HEAD_DIM)
            sv = _dot(sgw_ref[hd], vb[rows, hl]) + sgb_ref[:, hl]
            mix_scr[rows, D_SSM + hd * SG_HEAD_DIM:D_SSM + (hd + 1) * SG_HEAD_DIM] = (
                u[rows, hl] * sv).astype(BF16)

    fillers.pop(0)()

    @pl.when(t == nt - 1)
    def _():
        for g in range(SSM_GROUPS):
            ssm_ref[g] = state_scr[g].T


def _prompt_mixer_kernel(
        x_ref, xp_ref, nmw_ref, wz_ref, wxbc_ref, wdt_ref, wuv_ref, convw_ref, convb_ref, dtb_ref,
        alog_ref, dskip_ref, ssmnw_ref, sgnw_ref, sgw_ref, sgb_ref, wout_ref, nfw_ref,
        wr_ref, br_ref, xs_ref, yps_ref, eas_ref, zos_ref, zs_ref, ysgs_ref,
        h_ref, hn2_ref, route_ref, convt_ref, ssm_ref,
        xbc_scr, state_scr, y_scr, mix_scr, h_scr, *, tl, nt, n_main):
    i = pl.program_id(0)

    @pl.when(i == 0)
    def _():
        mix_scr[...] = jnp.zeros(mix_scr.shape, BF16)

    def emit(h):
        h_ref[...] = h
        hn2, route = _ffn_front(h, nfw_ref[...], wr_ref, br_ref[...])
        _store_row_tiles(hn2_ref, hn2)
        route_ref[...] = route

    @pl.when(i > n_main)
    def _():
        y = yps_ref[...] + eas_ref[...] * zos_ref[...]
        y_ssm = _gated_group_norm(y, zs_ref[...], ssmnw_ref[...])
        mix_in = jnp.concatenate([y_ssm.astype(BF16), ysgs_ref[...].astype(BF16)], axis=1)
        emit(xs_ref[...] + _dot(mix_in, wout_ref[...]))

    n_out_chunks = 2 + tl // CHUNK
    width = D_MODEL // n_out_chunks
    assert width % LANES == 0 and width * n_out_chunks == D_MODEL

    def out_chunk(c):
        cols = slice(c * width, (c + 1) * width)
        h_scr[:, cols] = xp_ref[:, cols] + _dot(mix_scr[...], wout_ref[:, cols])

    def finish():
        emit(h_scr[...])

    def out_chunks(cs):
        for c in cs:
            out_chunk(c)

    @pl.when(i == n_main)
    def _():
        out_chunks(range(n_out_chunks))
        finish()

    @pl.when(i < n_main)
    def _():
        half = n_out_chunks // 2
        assert tl // CHUNK == 2
        fillers = [lambda: None, functools.partial(out_chunks, range(half)),
                   functools.partial(out_chunks, range(half, n_out_chunks)), finish, lambda: None]
        _prompt_step(
            lax.rem(i, nt), nt, fillers, x_ref, nmw_ref, wz_ref, wxbc_ref, wdt_ref, wuv_ref,
            convw_ref, convb_ref, dtb_ref, alog_ref, dskip_ref, ssmnw_ref, sgnw_ref, sgw_ref, sgb_ref,
            convt_ref, ssm_ref, xbc_scr, state_scr, y_scr, mix_scr, tl)


def _const_spec(shape, single_buffer=False):
    zeros = (0,) * len(shape)
    return pl.BlockSpec(shape, lambda *_: zeros, pipeline_mode=pl.Buffered(1) if single_buffer else None)


def _prompt_mixer(x, w, tl, sample_rows):
    bsz, seq, _ = x.shape
    nt = seq // tl
    n_main = bsz * nt
    n_s = sample_rows[0].shape[0]
    assert seq % tl == 0 and n_s % tl == 0
    n_tok = bsz * seq + n_s
    x2 = x.reshape(bsz * seq, D_MODEL)
    main = lambda i: jnp.minimum(i, n_main - 1)
    prev = lambda i: jnp.clip(i - 1, 0, n_main - 1)
    tail = lambda i: jnp.maximum(i - n_main - 1, 0)
    out_row = lambda i: jnp.maximum(i - 1, 0)
    weights = [w['nmw'], w['wz'], w['wxbc'], w['wdt'], w['wuv'], w['convw'], w['convb'], w['dtb'],
               w['alog'], w['dskip'], w['ssmnw'], w['sgnw'], w['sgw'], w['sgb'], w['wout'], w['nfw'],
               w['wr'], w['br']]
    in_specs = ([pl.BlockSpec((tl, D_MODEL), lambda i: (main(i), 0)),
                 pl.BlockSpec((tl, D_MODEL), lambda i: (prev(i), 0))]
                + [_const_spec(a.shape, single_buffer=True) for a in weights]
                + [pl.BlockSpec((tl, D_MODEL), lambda i: (tail(i), 0)) for _ in sample_rows])
    q = HEADS_PER_GROUP * SSM_HEAD_DIM
    out_shape = [
        jax.ShapeDtypeStruct((n_tok, D_MODEL), F32),
        jax.ShapeDtypeStruct((n_tok * ROW_TILES, LANES), F32),
        jax.ShapeDtypeStruct((n_tok, LANES), F32),
        jax.ShapeDtypeStruct((bsz, SUBLANES, C_CONV), F32),
        jax.ShapeDtypeStruct((bsz, SSM_GROUPS, q, SSM_STATE), F32),
    ]
    out_specs = [
        pl.BlockSpec((tl, D_MODEL), lambda i: (out_row(i), 0)),
        pl.BlockSpec((tl * ROW_TILES, LANES), lambda i: (out_row(i), 0)),
        pl.BlockSpec((tl, LANES), lambda i: (out_row(i), 0)),
        pl.BlockSpec((None, SUBLANES, C_CONV), lambda i: (main(i) // nt, 0, 0)),
        pl.BlockSpec((None, SSM_GROUPS, q, SSM_STATE), lambda i: (main(i) // nt, 0, 0, 0)),
    ]
    scratch = [
        pltpu.VMEM((tl + SUBLANES, C_CONV), F32),
        pltpu.VMEM((SSM_GROUPS, SSM_STATE, q), F32),
        pltpu.VMEM((tl, D_SSM), F32),
        pltpu.VMEM((tl, D_SSM + D_SG), BF16),
        pltpu.VMEM((tl, D_MODEL), F32),
    ]
    return pl.pallas_call(
        functools.partial(_prompt_mixer_kernel, tl=tl, nt=nt, n_main=n_main),
        grid=(n_main + 1 + n_s // tl,),
        in_specs=in_specs,
        out_specs=out_specs,
        out_shape=out_shape,
        scratch_shapes=scratch,
        compiler_params=pltpu.CompilerParams(
            dimension_semantics=("arbitrary",), vmem_limit_bytes=VMEM_LIMIT),
        name="prompt_mixer",
    )(x2, x2, *weights, *sample_rows)


def _sample_front_kernel(
        x_ref, cprev_ref, nmw_ref, wz_ref, wxbc_ref, wdt_ref, wuv_ref, convw_ref, convb_ref,
        dtb_ref, alog_ref, dskip_ref, sgnw_ref, w4_ref, b4_ref, expand_ref,
        z_ref, ypart_ref, ea_ref, ysg_ref, v_ref, convn_ref, c_ref, b_ref, xw_ref, dec_ref,
        *, nb, nt):
    x = x_ref[...]
    hn = _rms(x, nmw_ref[...]).astype(BF16)
    z_ref[...] = _dot(hn, wz_ref[...])
    xbc = _dot(hn, wxbc_ref[...])
    dtr = _dot(hn, wdt_ref[...])
    uv = _gelu(_dot(hn, wuv_ref[...]))

    slab = lambda a, t: a[t * nb:(t + 1) * nb]
    full = [cprev_ref[k] for k in range(CONV_K - 1)] + [slab(xbc, t) for t in range(nt)]
    for k in range(CONV_K - 1):
        convn_ref[k] = full[nt + k]
    xs, bm, cm = [], [], []
    for t in range(nt):
        conv = convb_ref[...]
        for k in range(CONV_K):
            conv = conv + convw_ref[k:k + 1, :] * full[t + k]
        act = _silu(conv)
        xs.append(act[:, :D_SSM])
        bm.append(act[:, D_SSM:D_SSM + SSM_GROUPS * SSM_STATE])
        cm.append(act[:, D_SSM + SSM_GROUPS * SSM_STATE:])
        b_ref[t * nb:(t + 1) * nb, :] = bm[t]
        c_ref[t * nb:(t + 1) * nb, :] = cm[t]

    dt = _softplus(dtr + dtb_ref[...])
    a_row = -jnp.exp(alog_ref[...])
    dts = [slab(dt, t) for t in range(nt)]
    cum = []
    for t in range(nt):
        da = dts[t] * a_row
        cum.append(da if t == 0 else cum[t - 1] + da)
    dec_ref[...] = jnp.exp(cum[nt - 1])

    lane = lax.broadcasted_iota(jnp.int32, (nb, LANES), 1)
    first_group = lane < HEADS_PER_GROUP
    facs = [jnp.exp(cum[t]) for t in range(nt)]
    facs += [dts[s] * jnp.exp(cum[nt - 1] - cum[s]) for s in range(nt)]
    pairs = []
    for t in range(nt):
        for s in range(t + 1):
            cb = []
            for g in range(SSM_GROUPS):
                gl = slice(g * SSM_STATE, (g + 1) * SSM_STATE)
                cb.append(jnp.sum(cm[t][:, gl] * bm[s][:, gl], axis=1, keepdims=True))
            cbh = jnp.where(first_group, cb[0], cb[1])
            facs.append(jnp.exp(cum[t] - cum[s]) * dts[s] * cbh)
            pairs.append((t, s))
    fx = _dot_f32(jnp.concatenate(facs, axis=0), expand_ref[...])
    fslab = lambda i: fx[i * nb:(i + 1) * nb]
    for t in range(nt):
        ea_ref[t * nb:(t + 1) * nb, :] = fslab(t)
        xw_ref[t * nb:(t + 1) * nb, :] = xs[t] * fslab(nt + t)
    for t in range(nt):
        acc = dskip_ref[...] * xs[t]
        for i, (tt, s) in enumerate(pairs):
            if tt == t:
                acc = acc + fslab(2 * nt + i) * xs[s]
        ypart_ref[t * nb:(t + 1) * nb, :] = acc

    u = uv[:, :D_SG]
    v = _rms(uv[:, D_SG:], sgnw_ref[...])
    v_ref[...] = v
    for t in range(nt):
        sv = b4_ref[t:t + 1, :]
        for s in range(t + 1):
            sv = sv + w4_ref[t, s:s + 1, :] * slab(v, s)
        ysg_ref[t * nb:(t + 1) * nb, :] = slab(u, t) * sv


def _sample_front(x_slab, cprev, w, nb, nt):
    r = nb * nt
    weights = [w['nmw'], w['wz'], w['wxbc'], w['wdt'], w['wuv'], w['convw'], w['convb'], w['dtb'],
               w['alog'], w['dskip'], w['sgnw'], w['w4'], w['b4'], w['expand']]
    ins = [x_slab, cprev] + weights
    out_shape = [
        jax.ShapeDtypeStruct((r, D_SSM), F32),
        jax.ShapeDtypeStruct((r, D_SSM), F32),
        jax.ShapeDtypeStruct((r, D_SSM), F32),
        jax.ShapeDtypeStruct((r, D_SG), F32),
        jax.ShapeDtypeStruct((r, D_SG), F32),
        jax.ShapeDtypeStruct((CONV_K - 1, nb, C_CONV), F32),
        jax.ShapeDtypeStruct((r, SSM_GROUPS * SSM_STATE), F32),
        jax.ShapeDtypeStruct((r, SSM_GROUPS * SSM_STATE), F32),
        jax.ShapeDtypeStruct((r, D_SSM), F32),
        jax.ShapeDtypeStruct((nb, LANES), F32),
    ]
    return pl.pallas_call(
        functools.partial(_sample_front_kernel, nb=nb, nt=nt),
        grid=(1,),
        in_specs=[_const_spec(a.shape) for a in ins],
        out_specs=[_const_spec(s.shape) for s in out_shape],
        out_shape=out_shape,
        compiler_params=pltpu.CompilerParams(
            dimension_semantics=("arbitrary",), vmem_limit_bytes=VMEM_LIMIT),
        name="sample_front",
    )(*ins)


def _sample_state_kernel(dec_ref, st_ref, cq_ref, bq_ref, xw_ref, so_ref, z_ref, *, seqs):
    for bb in range(seqs):
        for g in range(SSM_GROUPS):
            s0 = st_ref[bb, g]
            z_ref[bb, g] = lax.dot_general(
                cq_ref[bb, g].astype(BF16), s0.astype(BF16), (((1,), (1,)), ((), ())),
                preferred_element_type=F32)
            upd = lax.dot_general(
                xw_ref[bb, g].astype(BF16), bq_ref[bb, g].astype(BF16), (((0,), (0,)), ((), ())),
                preferred_element_type=F32)
            for hh in range(HEADS_PER_GROUP):
                hs = slice(hh * SSM_HEAD_DIM, (hh + 1) * SSM_HEAD_DIM)
                so_ref[bb, g, hs, :] = s0[hs] * dec_ref[bb, g * HEADS_PER_GROUP + hh] + upd[hs]


def _sample_state(dec, state, cq, bq, xwq, seqs):
    nb = state.shape[0]
    q = HEADS_PER_GROUP * SSM_HEAD_DIM
    blk = lambda *tail: pl.BlockSpec((seqs, SSM_GROUPS) + tail, lambda i: (i, 0, 0, 0))
    return pl.pallas_call(
        functools.partial(_sample_state_kernel, seqs=seqs),
        grid=(nb // seqs,),
        in_specs=[
            pl.BlockSpec((seqs, SSM_HEADS), lambda i: (i, 0), memory_space=pltpu.SMEM),
            blk(q, SSM_STATE), blk(SUBLANES, SSM_STATE), blk(SUBLANES, SSM_STATE), blk(SUBLANES, q),
        ],
        out_specs=[blk(q, SSM_STATE), blk(SUBLANES, q)],
        out_shape=[
            jax.ShapeDtypeStruct((nb, SSM_GROUPS, q, SSM_STATE), F32),
            jax.ShapeDtypeStruct((nb, SSM_GROUPS, SUBLANES, q), F32),
        ],
        compiler_params=pltpu.CompilerParams(
            dimension_semantics=("arbitrary",), vmem_limit_bytes=VMEM_LIMIT),
        name="sample_state",
    )(dec, state, cq, bq, xwq)


EXPERT_PIECES = 8
DMA_PIECES = 4


def _expert_mlp_kernel(be_ref, nu_ref, tok0_ref, tokn_ref, dstp_ref, x_hbm, wg_ref, wu_ref, wd_ref,
                       parts_hbm, xbuf, obuf, xb, hb, wg_b, wu_b, wd_b, gsem, ssem, *, bm):
    j = pl.program_id(0)
    nu = nu_ref[0]
    nt = ROW_TILES
    tile = lambda ref, start: ref.at[pl.ds(pl.multiple_of(start, nt), nt)]

    def gather_copy(idx_ref, s, r):
        return pltpu.make_async_copy(tile(x_hbm, idx_ref[0, r]), xbuf.at[s, pl.ds(r * nt, nt)], gsem.at[s])

    def scatter_copy(s, r):
        return pltpu.make_async_copy(obuf.at[s, pl.ds(r * nt, nt)], tile(parts_hbm, dstp_ref[0, r]),
                                     ssem.at[s])

    def gather_wait(s):
        pltpu.make_async_copy(x_hbm.at[pl.ds(0, bm * nt)], xbuf.at[s], gsem.at[s]).wait()

    def scatter_wait(s):
        pltpu.make_async_copy(obuf.at[s], parts_hbm.at[pl.ds(0, bm * nt)], ssem.at[s]).wait()

    def step(slot):
        gather_wait(slot)
        per = bm // DMA_PIECES

        def start_rows(piece):
            if piece >= DMA_PIECES:
                return
            for r in range(piece * per, (piece + 1) * per):
                gather_copy(tokn_ref, 1 - slot, r).start(priority=r % 2)
                scatter_copy(1 - slot, r).start(priority=(r + 1) % 2)

        xb[...] = _load_row_tiles(xbuf.at[slot]).astype(BF16)
        assert EXPERT_PIECES == 8
        half = D_FF // 2
        for c in range(2):
            cols = slice(c * half, (c + 1) * half)
            start_rows(2 * c)
            g = _dot(xb[...], wg_b[:, cols])
            start_rows(2 * c + 1)
            u = _dot(xb[...], wu_b[:, cols])
            hb[:, cols] = (_silu(g) * u).astype(BF16)
        quarter = D_MODEL // 4
        scatter_wait(slot)
        for c in range(4):
            start_rows(4 + c)
            o = _dot(hb[...], wd_b[:, c * quarter:(c + 1) * quarter])
            for k in range(quarter // LANES):
                kk = c * (quarter // LANES) + k
                obuf.at[slot][pl.ds(kk, bm, stride=nt), :] = o[:, k * LANES:(k + 1) * LANES]

    def drain(slot):
        gather_wait(slot)
        scatter_wait(slot)
        for r in range(bm):
            scatter_copy(1 - slot, r).start(priority=r % 2)
        scatter_wait(1 - slot)

    @pl.when(j == 0)
    def _():
        obuf[...] = jnp.zeros(obuf.shape, F32)
        n_real = parts_hbm.shape[0] - 2 * bm * nt
        pltpu.make_async_copy(obuf.at[0], parts_hbm.at[pl.ds(n_real, bm * nt)], ssem.at[0]).start()

        def start0(r, carry):
            pltpu.make_async_copy(tile(x_hbm, tok0_ref[0, r]), xbuf.at[0, pl.ds(pl.multiple_of(r * nt, nt), nt)],
                                  gsem.at[0]).start()
            return carry
        lax.fori_loop(0, bm, start0, 0)

    used = j < nu

    @pl.when(used & ((j == 0) | (be_ref[j] != be_ref[jnp.maximum(j - 1, 0)])))
    def _():
        wg_b[...] = wg_ref[...].astype(BF16)
        wu_b[...] = wu_ref[...].astype(BF16)
        wd_b[...] = wd_ref[...].astype(BF16)

    for slot in range(2):
        parity = lax.rem(j, 2) == slot

        @pl.when(used & parity)
        def _(slot=slot):
            step(slot)

        @pl.when((j == nu) & parity)
        def _(slot=slot):
            drain(slot)


def _expert_mlp(blk_expert, n_used, tok, dst, x, wg, wu, wd, bm, n_out):
    n_blocks = blk_expert.shape[0]
    assert bm % EXPERT_PIECES == 0
    w_map = lambda j, be, nu: (be[j], 0, 0)
    idx_spec = lambda f: pl.BlockSpec((None, 1, bm), lambda j, be, nu: (f(j), 0, 0),
                                      memory_space=pltpu.SMEM)
    any_spec = pl.BlockSpec(memory_space=pl.ANY)
    grid_spec = pltpu.PrefetchScalarGridSpec(
        num_scalar_prefetch=2,
        grid=(n_blocks + 1,),
        in_specs=[
            idx_spec(lambda j: 0),
            idx_spec(lambda j: jnp.minimum(j + 1, n_blocks - 1)),
            idx_spec(lambda j: j),
            any_spec,
            pl.BlockSpec((None, D_MODEL, D_FF), w_map),
            pl.BlockSpec((None, D_MODEL, D_FF), w_map),
            pl.BlockSpec((None, D_FF, D_MODEL), w_map),
        ],
        out_specs=any_spec,
        scratch_shapes=[
            pltpu.VMEM((2, bm * ROW_TILES, LANES), F32),
            pltpu.VMEM((2, bm * ROW_TILES, LANES), F32),
            pltpu.VMEM((bm, D_MODEL), BF16),
            pltpu.VMEM((bm, D_FF), BF16),
            pltpu.VMEM((D_MODEL, D_FF), BF16),
            pltpu.VMEM((D_MODEL, D_FF), BF16),
            pltpu.VMEM((D_FF, D_MODEL), BF16),
            pltpu.SemaphoreType.DMA((2,)),
            pltpu.SemaphoreType.DMA((2,)),
        ],
    )
    tok3 = (tok * ROW_TILES).reshape(n_blocks, 1, bm)
    spare = (n_out - bm + jnp.arange(bm, dtype=jnp.int32)).reshape(1, bm)
    dst_prev = (jnp.concatenate([spare, dst], axis=0) * ROW_TILES).reshape(n_blocks + 1, 1, bm)
    blk_expert = jnp.concatenate([blk_expert, blk_expert[-1:]])
    return pl.pallas_call(
        functools.partial(_expert_mlp_kernel, bm=bm),
        grid_spec=grid_spec,
        out_shape=jax.ShapeDtypeStruct((n_out * ROW_TILES, LANES), F32),
        compiler_params=pltpu.CompilerParams(
            dimension_semantics=("arbitrary",), vmem_limit_bytes=VMEM_LIMIT),
        name="expert_mlp",
    )(blk_expert, n_used, tok3, tok3, dst_prev, x, wg, wu, wd)


PAD_ID = 2 ** 16 - 1


def _moe_plan(e_idx, bm, n_blocks):
    n_tok = e_idx.shape[0]
    n_asg = 2 * n_tok
    n_pad = n_blocks * bm - n_asg
    e_flat = e_idx.reshape(-1)
    experts = jnp.arange(N_EXPERTS, dtype=jnp.int32)
    counts = jnp.sum((e_flat[:, None] == experts[None, :]).astype(jnp.int32), axis=0)
    nblk = (counts + bm - 1) // bm
    blk_end = jnp.cumsum(nblk)
    n_used = blk_end[-1]
    j = jnp.arange(n_blocks, dtype=jnp.int32)
    be = jnp.sum((blk_end[None, :] <= jnp.minimum(j, n_used - 1)[:, None]).astype(jnp.int32), axis=1)
    be = jnp.minimum(be, N_EXPERTS - 1)
    assert n_asg < PAD_ID
    pad_end = jnp.cumsum(nblk * bm - counts)
    pad_expert = jnp.sum((pad_end[None, :] <= jnp.arange(n_pad, dtype=jnp.int32)[:, None]).astype(jnp.int32),
                         axis=1)
    keys = jnp.concatenate([2 * e_flat, 2 * pad_expert + 1])
    ids = jnp.concatenate([jnp.arange(n_asg, dtype=jnp.int32), jnp.full((n_pad,), PAD_ID, jnp.int32)])
    window = lax.sort(keys * (PAD_ID + 1) + ids) % (PAD_ID + 1)
    window = window.reshape(n_blocks, bm)
    valid = window != PAD_ID
    q = jnp.arange(bm, dtype=jnp.int32)
    tok = jnp.where(valid, window // 2, (j[:, None] * bm + q[None, :]) % n_tok)
    dst = jnp.where(valid, (window % 2) * n_tok + window // 2,
                    2 * n_tok + (j[:, None] % 2) * bm + q[None, :])
    return tok.astype(jnp.int32), dst.astype(jnp.int32), be, n_used.astype(jnp.int32).reshape(1)


def _final_kernel(h_ref, p0_ref, p1_ref, route_ref, w_ref, op_ref, os_ref, *, n_prompt_steps):
    route = route_ref[...]
    p0 = _load_row_tiles(p0_ref)
    p1 = _load_row_tiles(p1_ref)
    y = _rms(h_ref[...] + route[:, 2:3] * p0 + route[:, 3:4] * p1, w_ref[...])
    i = pl.program_id(0)

    @pl.when(i < n_prompt_steps)
    def _():
        op_ref[...] = y

    @pl.when(i >= n_prompt_steps)
    def _():
        os_ref[...] = y


def _final(h, parts, route, w, tf, n_prompt):
    n_tok = h.shape[0]
    assert n_tok % tf == 0 and n_prompt % tf == 0
    n_p = n_prompt // tf
    row = lambda d: pl.BlockSpec((tf, d), lambda i: (i, 0))
    part = lambda k: pl.BlockSpec((tf * ROW_TILES, LANES), lambda i: (k * (n_tok // tf) + i, 0))
    return pl.pallas_call(
        functools.partial(_final_kernel, n_prompt_steps=n_p),
        grid=(n_tok // tf,),
        in_specs=[row(D_MODEL), part(0), part(1), row(LANES), _const_spec(w.shape)],
        out_specs=[pl.BlockSpec((tf, D_MODEL), lambda i: (jnp.minimum(i, n_p - 1), 0)),
                   pl.BlockSpec((tf, D_MODEL), lambda i: (jnp.maximum(i - n_p, 0), 0))],
        out_shape=[jax.ShapeDtypeStruct((n_prompt, D_MODEL), F32),
                   jax.ShapeDtypeStruct((n_tok - n_prompt, D_MODEL), F32)],
        compiler_params=pltpu.CompilerParams(
            dimension_semantics=("arbitrary",), vmem_limit_bytes=VMEM_LIMIT),
        name="final_combine",
    )(h, parts, parts, route, w)


PROMPT_TILE = 256
MOE_BLOCK_ROWS = 512
FINAL_TILE = 512
STATE_SEQS = 8


def _pad_lanes(a, width=LANES):
    return jnp.pad(a, [(0, 0)] * (a.ndim - 1) + [(0, width - a.shape[-1])])


def _layer_weights(l, norm_mix_w, w_in, conv_w, conv_b, dt_bias, a_log, d_skip, ssm_norm_w, sg_norm_w,
                   sg_w, sg_b, w_out, norm_ffn_w, w_router_group, b_router_group, w_router_expert,
                   b_router_expert, n_dec):
    c0, c1, c2 = D_SSM, D_SSM + C_CONV, D_SSM + C_CONV + SSM_HEADS
    wi = w_in[l]
    causal = jnp.tril(jnp.ones((CHUNK, CHUNK), bool))
    sgw_l = jnp.where(causal, sg_w[l], 0.0)
    head_of_lane = jnp.arange(D_SSM, dtype=jnp.int32) // SSM_HEAD_DIM
    expand = (jnp.arange(LANES, dtype=jnp.int32)[:, None] == head_of_lane[None, :]).astype(F32)
    row = lambda a: a.reshape(1, -1).astype(F32)
    gap = ROUTER_EXPERT_ROW - N_EXPERT_GROUPS
    rest = LANES - ROUTER_EXPERT_ROW - N_EXPERTS
    wrt = jnp.concatenate([w_router_group[l].T, jnp.zeros((gap, D_MODEL), F32), w_router_expert[l].T,
                           jnp.zeros((rest, D_MODEL), F32)], axis=0).astype(F32)
    brc = jnp.concatenate([b_router_group[l], jnp.zeros((gap,), F32), b_router_expert[l],
                           jnp.zeros((rest,), F32)]).astype(F32).reshape(LANES, 1)
    return {
        'nmw': row(norm_mix_w[l]),
        'wz': wi[:, :c0].astype(BF16),
        'wxbc': wi[:, c0:c1].astype(BF16),
        'wdt': _pad_lanes(wi[:, c1:c2]).astype(BF16),
        'wuv': wi[:, c2:].astype(BF16),
        'convw': conv_w[l].astype(F32),
        'convb': row(conv_b[l]),
        'dtb': _pad_lanes(row(dt_bias[l])),
        'alog': _pad_lanes(row(a_log[l])),
        'dskip': row(jnp.repeat(d_skip[l], SSM_HEAD_DIM)),
        'ssmnw': row(ssm_norm_w[l]),
        'sgnw': row(sg_norm_w[l]),
        'sgw': sgw_l.astype(BF16),
        'sgb': jnp.repeat(sg_b[l].T, SG_HEAD_DIM, axis=1).astype(F32),
        'w4': jnp.repeat(jnp.transpose(sgw_l[:, :n_dec, :n_dec], (1, 2, 0)), SG_HEAD_DIM, axis=2).astype(F32),
        'b4': jnp.repeat(sg_b[l][:, :n_dec].T, SG_HEAD_DIM, axis=1).astype(F32),
        'expand': expand,
        'wout': w_out[l].astype(BF16),
        'nfw': row(norm_ffn_w[l]),
        'wr': jnp.stack([wrt.astype(BF16), (wrt - wrt.astype(BF16).astype(F32)).astype(BF16)]),
        'br': brc,
    }


def _layer(l, xp, xs_slab, state_conv, state_ssm, w, w_gate, w_up, w_down):
    bp, seq, _ = xp.shape
    nb, nt = state_conv.shape[1], xs_slab.shape[0] // state_conv.shape[1]
    q = HEADS_PER_GROUP * SSM_HEAD_DIM

    cprev = jnp.transpose(state_conv[l], (1, 0, 2))
    z, ypart, ea, ysg, v, convn, cmat, bmat, xw, dec = _sample_front(xs_slab, cprev, w, nb, nt)
    to_seq = lambda a, d: jnp.pad(
        jnp.transpose(a.reshape(nt, nb, SSM_GROUPS, d), (1, 2, 0, 3)),
        ((0, 0), (0, 0), (0, SUBLANES - nt), (0, 0)))
    ssm_s, zoff = _sample_state(
        dec[:, :SSM_HEADS], state_ssm[l].reshape(nb, SSM_GROUPS, q, SSM_STATE),
        to_seq(cmat, SSM_STATE), to_seq(bmat, SSM_STATE), to_seq(xw, q), STATE_SEQS)
    zoff = jnp.transpose(zoff[:, :, :nt], (2, 0, 1, 3)).reshape(nt * nb, D_SSM)

    h, hn2, route, convt_p, ssm_p = _prompt_mixer(xp, w, PROMPT_TILE, (xs_slab, ypart, ea, zoff, z, ysg))

    n_tok = h.shape[0]
    bm = MOE_BLOCK_ROWS
    n_blocks = -(-(2 * n_tok) // bm) + N_EXPERTS
    tok, dst, blk_expert, n_used = _moe_plan(route[:, :2].astype(jnp.int32), bm, n_blocks)
    parts = _expert_mlp(blk_expert, n_used, tok, dst, hn2, w_gate[l], w_up[l], w_down[l], bm,
                        2 * n_tok + 2 * bm)
    outs = dict(
        h=h, parts=parts, route=route,
        conv_p=convt_p[:, SUBLANES - (CONV_K - 1):],
        ssm_p=ssm_p.reshape(bp, SSM_GROUPS, HEADS_PER_GROUP, SSM_HEAD_DIM, SSM_STATE),
        conv_s=jnp.transpose(convn, (1, 0, 2)),
        ssm_s=ssm_s.reshape(nb, SSM_GROUPS, HEADS_PER_GROUP, SSM_HEAD_DIM, SSM_STATE),
        v_s=jnp.transpose(v.reshape(nt, nb, D_SG), (1, 0, 2)),
    )
    return outs


def kernel(x_prompt, x_sample, state_conv, state_ssm, norm_mix_w, w_in, conv_w, conv_b, dt_bias, a_log, d_skip, ssm_norm_w, sg_norm_w, sg_w, sg_b, w_out, norm_ffn_w, w_router_group, b_router_group, w_router_expert, b_router_expert, w_gate, w_up, w_down, norm_final_w):
    depth = w_in.shape[0]
    assert depth == 1, "the fused final norm assumes a single layer"
    bp, seq, _ = x_prompt.shape
    nb, nt, _ = x_sample.shape
    l = 0
    w = _layer_weights(l, norm_mix_w, w_in, conv_w, conv_b, dt_bias, a_log, d_skip, ssm_norm_w, sg_norm_w,
                       sg_w, sg_b, w_out, norm_ffn_w, w_router_group, b_router_group, w_router_expert,
                       b_router_expert, nt)
    xs_slab = jnp.transpose(x_sample, (1, 0, 2)).reshape(nt * nb, D_MODEL)
    outs = _layer(l, x_prompt, xs_slab, state_conv, state_ssm, w, w_gate, w_up, w_down)
    y_p, y_s = _final(outs['h'], outs['parts'], outs['route'], norm_final_w.reshape(1, -1).astype(F32),
                      FINAL_TILE, bp * seq)
    y_prompt = y_p.reshape(bp, seq, D_MODEL)
    y_sample = jnp.transpose(y_s.reshape(nt, nb, D_MODEL), (1, 0, 2))
    return (y_prompt, y_sample, outs['conv_p'][None], outs['ssm_p'][None], outs['conv_s'][None],
            outs['ssm_s'][None], outs['v_s'][None])
```
